```python
import math
import jax, jax.numpy as jnp
from jax import lax
import numpy as np

D_MODEL = 2048
BATCH = 8
SEQ = 8192
DEPTH = 1

BLOCK = 128
WINDOW = 128
A_HEADS = 8
A_HEAD_DIM = 128
A_WIDTH = A_HEADS * A_HEAD_DIM
B_HEADS = 16
B_KV_HEADS = 2
HEAD_DIM = 64
B_WIDTH = B_HEADS * HEAD_DIM
KV_WIDTH = B_KV_HEADS * HEAD_DIM
MIX_WIDTH = A_WIDTH + B_WIDTH
IN_COLS = 2 * A_WIDTH + B_WIDTH + 2 * KV_WIDTH
D_FF = 5632
N_BUCKETS = 32
MAX_DISTANCE = 128
N_MOD = 9
EPS = 1e-6

kernel_name = "hybrid_gmlp_swa_macaron_adaln_layer"


def rms_norm(x, g):
    xf = x.astype(jnp.float32)
    y = xf * lax.rsqrt(jnp.mean(xf * xf, axis=-1, keepdims=True) + EPS)
    return (y * g.astype(jnp.float32)).astype(x.dtype)


def modulate(h, shift, scale):
    return h * (1 + scale[:, None, :]) + shift[:, None, :]


def swiglu(h, w1, w3, w2):
    return (jax.nn.silu(h @ w1) * (h @ w3)) @ w2


def t5_bucket(n):
    max_exact = N_BUCKETS // 2
    nf = jnp.maximum(n, 1).astype(jnp.float32)
    large = max_exact + (jnp.log(nf / max_exact) / math.log(MAX_DISTANCE / max_exact)
                         * (N_BUCKETS - max_exact)).astype(jnp.int32)
    large = jnp.minimum(large, N_BUCKETS - 1)
    return jnp.where(n < max_exact, n, large)


def gmlp_mixer(u, v, spatial_w, spatial_b, g_v):
    bsz, seq, _ = u.shape
    nb = seq // BLOCK
    u = u.reshape(bsz, nb, BLOCK, A_HEADS, A_HEAD_DIM)
    v = rms_norm(v.reshape(bsz, nb, BLOCK, A_HEADS, A_HEAD_DIM), g_v)
    causal = jnp.tril(jnp.ones((BLOCK, BLOCK), dtype=bool))
    w = jnp.where(causal[None], spatial_w, 0)
    mixed = jnp.einsum('hij,bnjhd->bnihd', w, v) + spatial_b.T[None, None, :, :, None]
    return (u * mixed).reshape(bsz, seq, A_WIDTH)


def swa_mixer(q, k, v, g_q, g_k, sinks, rel_bias):
    bsz, seq, _ = q.shape
    nb = seq // BLOCK
    grp = B_HEADS // B_KV_HEADS
    q = rms_norm(q.reshape(bsz, seq, B_HEADS, HEAD_DIM), g_q)
    k = rms_norm(k.reshape(bsz, seq, B_KV_HEADS, HEAD_DIM), g_k)
    q = q.reshape(bsz, nb, BLOCK, B_KV_HEADS, grp, HEAD_DIM)
    k = k.reshape(bsz, nb, BLOCK, B_KV_HEADS, HEAD_DIM)
    v = v.reshape(bsz, nb, BLOCK, B_KV_HEADS, HEAD_DIM)

    def band(t):
        prev = jnp.pad(t[:, :-1], ((0, 0), (1, 0), (0, 0), (0, 0), (0, 0)))
        return jnp.concatenate([prev, t], axis=2)

    kb, vb = band(k), band(v)
    s = jnp.einsum('bnikgd,bnjkd->bnkgij', q, kb).astype(jnp.float32) * (HEAD_DIM ** -0.5)

    qi = jnp.arange(BLOCK)[:, None]
    kj = jnp.arange(2 * BLOCK)[None, :]
    dist = qi + BLOCK - kj
    in_window = (dist >= 0) & (dist < WINDOW)
    bucket = t5_bucket(jnp.clip(dist, 0, None))
    bias = jnp.transpose(rel_bias[bucket], (2, 0, 1)).astype(jnp.float32)
    bias = bias.reshape(B_KV_HEADS, grp, BLOCK, 2 * BLOCK)
    key_pos = jnp.arange(nb)[:, None] * BLOCK - BLOCK + kj
    valid = in_window[None] & (key_pos >= 0)[:, None, :]

    s = jnp.where(valid[None, :, None, None], s + bias, -jnp.inf)
    sink = sinks.astype(jnp.float32).reshape(B_KV_HEADS, grp)[:, :, None, None]
    m = jnp.maximum(jnp.max(s, axis=-1, keepdims=True), sink)
    p = jnp.exp(s - m)
    w = p / (jnp.sum(p, axis=-1, keepdims=True) + jnp.exp(sink - m))
    o = jnp.einsum('bnkgij,bnjkd->bnikgd', w.astype(vb.dtype), vb)
    return o.reshape(bsz, seq, B_WIDTH)


def _fwd_setup_inputs(seed: int = 0) -> dict:
    key = jax.random.key(seed)
    ks = jax.random.split(key, 24)
    f32 = jnp.float32
    nrm = lambda k, shape, s: jax.random.normal(k, shape, f32) * s
    gain = lambda k, shape: 1.0 + 0.02 * jax.random.normal(k, shape, f32)
    L, D = DEPTH, D_MODEL
    return {
        "x": nrm(ks[0], (BATCH, SEQ, D), 1.0),
        "c": nrm(ks[1], (BATCH, D), 1.0),
        "w_ada": nrm(ks[2], (L, D, N_MOD * D), 0.5 * D ** -0.5),
        "b_ada": nrm(ks[3], (L, N_MOD * D), 0.01),
        "g_ffn1": gain(ks[4], (L, D)),
        "w1_ffn1": nrm(ks[5], (L, D, D_FF), D ** -0.5),
        "w3_ffn1": nrm(ks[6], (L, D, D_FF), D ** -0.5),
        "w2_ffn1": nrm(ks[7], (L, D_FF, D), D_FF ** -0.5),
        "g_mix": gain(ks[8], (L, D)),
        "w_in": nrm(ks[9], (L, D, IN_COLS), D ** -0.5),
        "spatial_w": nrm(ks[10], (L, A_HEADS, BLOCK, BLOCK), BLOCK ** -0.5),
        "spatial_b": 1.0 + nrm(ks[11], (L, A_HEADS, BLOCK), 0.01),
        "g_v": gain(ks[12], (L, A_HEADS, A_HEAD_DIM)),
        "g_q": gain(ks[13], (L, HEAD_DIM)),
        "g_k": gain(ks[14], (L, HEAD_DIM)),
        "sinks": nrm(ks[15], (L, B_HEADS), 1.0),
        "rel_bias": nrm(ks[16], (N_BUCKETS, B_HEADS), 0.5),
        "w_out": nrm(ks[17], (L, MIX_WIDTH, D), MIX_WIDTH ** -0.5),
        "g_ffn2": gain(ks[18], (L, D)),
        "w1_ffn2": nrm(ks[19], (L, D, D_FF), D ** -0.5),
        "w3_ffn2": nrm(ks[20], (L, D, D_FF), D ** -0.5),
        "w2_ffn2": nrm(ks[21], (L, D_FF, D), D_FF ** -0.5),
    }


def _fwd_reference(x, c, w_ada, b_ada, g_ffn1, w1_ffn1, w3_ffn1, w2_ffn1, g_mix, w_in,
              spatial_w, spatial_b, g_v, g_q, g_k, sinks, rel_bias, w_out,
              g_ffn2, w1_ffn2, w3_ffn2, w2_ffn2):
    c_act = jax.nn.silu(c)
    split_pts = [A_WIDTH, 2 * A_WIDTH, 2 * A_WIDTH + B_WIDTH, 2 * A_WIDTH + B_WIDTH + KV_WIDTH]
    for l in range(DEPTH):
        mod = c_act @ w_ada[l] + b_ada[l]
        sh1, sc1, gt1, sh2, sc2, gt2, sh3, sc3, gt3 = jnp.split(mod, N_MOD, axis=-1)

        h = modulate(rms_norm(x, g_ffn1[l]), sh1, sc1)
        x = x + 0.5 * gt1[:, None, :] * swiglu(h, w1_ffn1[l], w3_ffn1[l], w2_ffn1[l])

        h = modulate(rms_norm(x, g_mix[l]), sh2, sc2)
        z = h @ w_in[l]
        za_u, za_v, zq, zk, zv = jnp.split(z, split_pts, axis=-1)
        ya = gmlp_mixer(jax.nn.gelu(za_u, approximate=False), jax.nn.gelu(za_v, approximate=False),
                        spatial_w[l], spatial_b[l], g_v[l])
        yb = swa_mixer(zq, zk, zv, g_q[l], g_k[l], sinks[l], rel_bias)
        y = jnp.concatenate([ya, yb], axis=-1) @ w_out[l]
        x = x + gt2[:, None, :] * y

        h = modulate(rms_norm(x, g_ffn2[l]), sh3, sc3)
        x = x + 0.5 * gt3[:, None, :] * swiglu(h, w1_ffn2[l], w3_ffn2[l], w2_ffn2[l])
    return x


import jax as _jax
import jax.numpy as _jnp

TWIN_FORMAT = 'train_step'
FWD_PARAMS = ['x', 'c', 'w_ada', 'b_ada', 'g_ffn1', 'w1_ffn1', 'w3_ffn1', 'w2_ffn1', 'g_mix', 'w_in', 'spatial_w', 'spatial_b', 'g_v', 'g_q', 'g_k', 'sinks', 'rel_bias', 'w_out', 'g_ffn2', 'w1_ffn2', 'w3_ffn2', 'w2_ffn2']
TWIN_WEIGHTS = ['w_ada', 'b_ada', 'g_ffn1', 'w1_ffn1', 'w3_ffn1', 'w2_ffn1', 'g_mix', 'w_in', 'spatial_w', 'spatial_b', 'g_v', 'g_q', 'g_k', 'sinks', 'rel_bias', 'w_out', 'g_ffn2', 'w1_ffn2', 'w3_ffn2', 'w2_ffn2']
TWIN_DIFF_INPUT = 'x'
TWIN_INPUTS = ['x', 'c', 'w_ada', 'b_ada', 'g_ffn1', 'w1_ffn1', 'w3_ffn1', 'w2_ffn1', 'g_mix', 'w_in', 'spatial_w', 'spatial_b', 'g_v', 'g_q', 'g_k', 'sinks', 'rel_bias', 'w_out', 'g_ffn2', 'w1_ffn2', 'w3_ffn2', 'w2_ffn2', 'loss_target', 'm_w_ada', 'm_b_ada', 'm_g_ffn1', 'm_w1_ffn1', 'm_w3_ffn1', 'm_w2_ffn1', 'm_g_mix', 'm_w_in', 'm_spatial_w', 'm_spatial_b', 'm_g_v', 'm_g_q', 'm_g_k', 'm_sinks', 'm_rel_bias', 'm_w_out', 'm_g_ffn2', 'm_w1_ffn2', 'm_w3_ffn2', 'm_w2_ffn2', 'v_w_ada', 'v_b_ada', 'v_g_ffn1', 'v_w1_ffn1', 'v_w3_ffn1', 'v_w2_ffn1', 'v_g_mix', 'v_w_in', 'v_spatial_w', 'v_spatial_b', 'v_g_v', 'v_g_q', 'v_g_k', 'v_sinks', 'v_rel_bias', 'v_w_out', 'v_g_ffn2', 'v_w1_ffn2', 'v_w3_ffn2', 'v_w2_ffn2']
TWIN_OUTPUTS = ['loss', 'grad_x', 'grad_w_ada', 'grad_b_ada', 'grad_g_ffn1', 'grad_w1_ffn1', 'grad_w3_ffn1', 'grad_w2_ffn1', 'grad_g_mix', 'grad_w_in', 'grad_spatial_w', 'grad_spatial_b', 'grad_g_v', 'grad_g_q', 'grad_g_k', 'grad_sinks', 'grad_rel_bias', 'grad_w_out', 'grad_g_ffn2', 'grad_w1_ffn2', 'grad_w3_ffn2', 'grad_w2_ffn2', 'delta_w_ada', 'delta_b_ada', 'delta_g_ffn1', 'delta_w1_ffn1', 'delta_w3_ffn1', 'delta_w2_ffn1', 'delta_g_mix', 'delta_w_in', 'delta_spatial_w', 'delta_spatial_b', 'delta_g_v', 'delta_g_q', 'delta_g_k', 'delta_sinks', 'delta_rel_bias', 'delta_w_out', 'delta_g_ffn2', 'delta_w1_ffn2', 'delta_w3_ffn2', 'delta_w2_ffn2', 'new_m_w_ada', 'new_m_b_ada', 'new_m_g_ffn1', 'new_m_w1_ffn1', 'new_m_w3_ffn1', 'new_m_w2_ffn1', 'new_m_g_mix', 'new_m_w_in', 'new_m_spatial_w', 'new_m_spatial_b', 'new_m_g_v', 'new_m_g_q', 'new_m_g_k', 'new_m_sinks', 'new_m_rel_bias', 'new_m_w_out', 'new_m_g_ffn2', 'new_m_w1_ffn2', 'new_m_w3_ffn2', 'new_m_w2_ffn2', 'new_v_w_ada', 'new_v_b_ada', 'new_v_g_ffn1', 'new_v_w1_ffn1', 'new_v_w3_ffn1', 'new_v_w2_ffn1', 'new_v_g_mix', 'new_v_w_in', 'new_v_spatial_w', 'new_v_spatial_b', 'new_v_g_v', 'new_v_g_q', 'new_v_g_k', 'new_v_sinks', 'new_v_rel_bias', 'new_v_w_out', 'new_v_g_ffn2', 'new_v_w1_ffn2', 'new_v_w3_ffn2', 'new_v_w2_ffn2']
TWIN_LEAF_KINDS = {'loss': 'loss', 'grad_x': 'grad_x', 'grad_w_ada': 'grad_w', 'grad_b_ada': 'grad_w', 'grad_g_ffn1': 'grad_w', 'grad_w1_ffn1': 'grad_w', 'grad_w3_ffn1': 'grad_w', 'grad_w2_ffn1': 'grad_w', 'grad_g_mix': 'grad_w', 'grad_w_in': 'grad_w', 'grad_spatial_w': 'grad_w', 'grad_spatial_b': 'grad_w', 'grad_g_v': 'grad_w', 'grad_g_q': 'grad_w', 'grad_g_k': 'grad_w', 'grad_sinks': 'grad_w', 'grad_rel_bias': 'grad_w', 'grad_w_out': 'grad_w', 'grad_g_ffn2': 'grad_w', 'grad_w1_ffn2': 'grad_w', 'grad_w3_ffn2': 'grad_w', 'grad_w2_ffn2': 'grad_w', 'delta_w_ada': 'delta_w', 'delta_b_ada': 'delta_w', 'delta_g_ffn1': 'delta_w', 'delta_w1_ffn1': 'delta_w', 'delta_w3_ffn1': 'delta_w', 'delta_w2_ffn1': 'delta_w', 'delta_g_mix': 'delta_w', 'delta_w_in': 'delta_w', 'delta_spatial_w': 'delta_w', 'delta_spatial_b': 'delta_w', 'delta_g_v': 'delta_w', 'delta_g_q': 'delta_w', 'delta_g_k': 'delta_w', 'delta_sinks': 'delta_w', 'delta_rel_bias': 'delta_w', 'delta_w_out': 'delta_w', 'delta_g_ffn2': 'delta_w', 'delta_w1_ffn2': 'delta_w', 'delta_w3_ffn2': 'delta_w', 'delta_w2_ffn2': 'delta_w', 'new_m_w_ada': 'new_m', 'new_m_b_ada': 'new_m', 'new_m_g_ffn1': 'new_m', 'new_m_w1_ffn1': 'new_m', 'new_m_w3_ffn1': 'new_m', 'new_m_w2_ffn1': 'new_m', 'new_m_g_mix': 'new_m', 'new_m_w_in': 'new_m', 'new_m_spatial_w': 'new_m', 'new_m_spatial_b': 'new_m', 'new_m_g_v': 'new_m', 'new_m_g_q': 'new_m', 'new_m_g_k': 'new_m', 'new_m_sinks': 'new_m', 'new_m_rel_bias': 'new_m', 'new_m_w_out': 'new_m', 'new_m_g_ffn2': 'new_m', 'new_m_w1_ffn2': 'new_m', 'new_m_w3_ffn2': 'new_m', 'new_m_w2_ffn2': 'new_m', 'new_v_w_ada': 'new_v', 'new_v_b_ada': 'new_v', 'new_v_g_ffn1': 'new_v', 'new_v_w1_ffn1': 'new_v', 'new_v_w3_ffn1': 'new_v', 'new_v_w2_ffn1': 'new_v', 'new_v_g_mix': 'new_v', 'new_v_w_in': 'new_v', 'new_v_spatial_w': 'new_v', 'new_v_spatial_b': 'new_v', 'new_v_g_v': 'new_v', 'new_v_g_q': 'new_v', 'new_v_g_k': 'new_v', 'new_v_sinks': 'new_v', 'new_v_rel_bias': 'new_v', 'new_v_w_out': 'new_v', 'new_v_g_ffn2': 'new_v', 'new_v_w1_ffn2': 'new_v', 'new_v_w3_ffn2': 'new_v', 'new_v_w2_ffn2': 'new_v'}


def _forward(args):
    return _fwd_reference(*[args[k] for k in FWD_PARAMS])


def _output_shape():
    def fwd():
        inp = _fwd_setup_inputs(0)
        return _fwd_reference(*[inp[k] for k in FWD_PARAMS])
    out = _jax.eval_shape(fwd)
    return out.shape, out.dtype

N_MICROBATCH = 1
ADAM_LR = 0.001
ADAM_B1 = 0.9
ADAM_B2 = 0.999
ADAM_EPS = 1e-08
ADAM_WD = 0.01
ADAM_STEP = 10
PER_EXAMPLE_BATCH_AXIS = {'x': 0, 'c': 0, 'loss_target': 0}
SHARED_INPUTS = []
_WEIGHT_DTYPES = {'w_ada': _jnp.float32, 'b_ada': _jnp.float32, 'g_ffn1': _jnp.float32, 'w1_ffn1': _jnp.float32, 'w3_ffn1': _jnp.float32, 'w2_ffn1': _jnp.float32, 'g_mix': _jnp.float32, 'w_in': _jnp.float32, 'spatial_w': _jnp.float32, 'spatial_b': _jnp.float32, 'g_v': _jnp.float32, 'g_q': _jnp.float32, 'g_k': _jnp.float32, 'sinks': _jnp.float32, 'rel_bias': _jnp.float32, 'w_out': _jnp.float32, 'g_ffn2': _jnp.float32, 'w1_ffn2': _jnp.float32, 'w3_ffn2': _jnp.float32, 'w2_ffn2': _jnp.float32}
MOMENT_SCALE = {'w_ada': 3.975769e-01, 'b_ada': 1.001154e+00, 'g_ffn1': 7.592217e-01, 'w1_ffn1': 1.595280e-02, 'w3_ffn1': 1.493516e-02, 'w2_ffn1': 2.431761e-02, 'g_mix': 1.189364e+00, 'w_in': 1.891888e-01, 'spatial_w': 4.906043e-01, 'spatial_b': 1.518038e+00, 'g_v': 7.600923e-01, 'g_q': 6.586590e-01, 'g_k': 6.616420e-01, 'sinks': 1.214298e-01, 'rel_bias': 3.350184e-02, 'w_out': 1.994265e-01, 'g_ffn2': 7.398677e-01, 'w1_ffn2': 1.558513e-02, 'w3_ffn2': 1.417988e-02, 'w2_ffn2': 2.306372e-02}


def _to_microbatches(a, axis):
    t = _jnp.moveaxis(a, axis, 0)
    t = t.reshape((N_MICROBATCH, t.shape[0] // N_MICROBATCH) + t.shape[1:])
    return _jnp.moveaxis(t, 1, axis + 1)


def setup_inputs(seed: int = 0) -> dict:
    inp = _fwd_setup_inputs(seed)
    key = _jax.random.fold_in(_jax.random.key(seed), 7919)
    shape, _ = _output_shape()
    out = dict(inp)
    out["loss_target"] = _jax.random.normal(_jax.random.fold_in(key, 0), shape, _jnp.float32)
    for i, name in enumerate(TWIN_WEIGHTS):
        w = inp[name].astype(_jnp.float32)
        if MOMENT_SCALE is None:
            s = _jnp.sqrt(_jnp.mean(_jnp.square(w)) + 1e-30)
        else:
            s = MOMENT_SCALE[name]
        km, kv = _jax.random.split(_jax.random.fold_in(key, i + 1))
        out[name] = w
        out["m_" + name] = s * _jax.random.normal(km, w.shape, _jnp.float32)
        out["v_" + name] = (s * s) * _jax.random.uniform(kv, w.shape, _jnp.float32, 0.5, 1.5)
    if N_MICROBATCH > 1:
        for name, axis in PER_EXAMPLE_BATCH_AXIS.items():
            out[name] = _to_microbatches(out[name], axis)
    return {'x': out['x'], 'c': out['c'], 'w_ada': out['w_ada'], 'b_ada': out['b_ada'], 'g_ffn1': out['g_ffn1'], 'w1_ffn1': out['w1_ffn1'], 'w3_ffn1': out['w3_ffn1'], 'w2_ffn1': out['w2_ffn1'], 'g_mix': out['g_mix'], 'w_in': out['w_in'], 'spatial_w': out['spatial_w'], 'spatial_b': out['spatial_b'], 'g_v': out['g_v'], 'g_q': out['g_q'], 'g_k': out['g_k'], 'sinks': out['sinks'], 'rel_bias': out['rel_bias'], 'w_out': out['w_out'], 'g_ffn2': out['g_ffn2'], 'w1_ffn2': out['w1_ffn2'], 'w3_ffn2': out['w3_ffn2'], 'w2_ffn2': out['w2_ffn2'], 'loss_target': out['loss_target'], 'm_w_ada': out['m_w_ada'], 'm_b_ada': out['m_b_ada'], 'm_g_ffn1': out['m_g_ffn1'], 'm_w1_ffn1': out['m_w1_ffn1'], 'm_w3_ffn1': out['m_w3_ffn1'], 'm_w2_ffn1': out['m_w2_ffn1'], 'm_g_mix': out['m_g_mix'], 'm_w_in': out['m_w_in'], 'm_spatial_w': out['m_spatial_w'], 'm_spatial_b': out['m_spatial_b'], 'm_g_v': out['m_g_v'], 'm_g_q': out['m_g_q'], 'm_g_k': out['m_g_k'], 'm_sinks': out['m_sinks'], 'm_rel_bias': out['m_rel_bias'], 'm_w_out': out['m_w_out'], 'm_g_ffn2': out['m_g_ffn2'], 'm_w1_ffn2': out['m_w1_ffn2'], 'm_w3_ffn2': out['m_w3_ffn2'], 'm_w2_ffn2': out['m_w2_ffn2'], 'v_w_ada': out['v_w_ada'], 'v_b_ada': out['v_b_ada'], 'v_g_ffn1': out['v_g_ffn1'], 'v_w1_ffn1': out['v_w1_ffn1'], 'v_w3_ffn1': out['v_w3_ffn1'], 'v_w2_ffn1': out['v_w2_ffn1'], 'v_g_mix': out['v_g_mix'], 'v_w_in': out['v_w_in'], 'v_spatial_w': out['v_spatial_w'], 'v_spatial_b': out['v_spatial_b'], 'v_g_v': out['v_g_v'], 'v_g_q': out['v_g_q'], 'v_g_k': out['v_g_k'], 'v_sinks': out['v_sinks'], 'v_rel_bias': out['v_rel_bias'], 'v_w_out': out['v_w_out'], 'v_g_ffn2': out['v_g_ffn2'], 'v_w1_ffn2': out['v_w1_ffn2'], 'v_w3_ffn2': out['v_w3_ffn2'], 'v_w2_ffn2': out['v_w2_ffn2']}


def _loss(weights, diff, rest, loss_target):
    with _jax.named_scope("forward"):
        args = {**rest, TWIN_DIFF_INPUT: diff, **{k: w.astype(_WEIGHT_DTYPES[k]) for k, w in weights.items()}}
        y = _forward(args)
    with _jax.named_scope("loss_head"):
        err = _jnp.square(y.astype(_jnp.float32) - loss_target)
        return 0.5 * _jnp.sum(_jnp.mean(err, axis=-1)) if err.ndim else 0.5 * err


def _adamw(w, g, m, v):
    m = ADAM_B1 * m + (1.0 - ADAM_B1) * g
    v = ADAM_B2 * v + (1.0 - ADAM_B2) * _jnp.square(g)
    m_hat = m / (1.0 - ADAM_B1 ** ADAM_STEP)
    v_hat = v / (1.0 - ADAM_B2 ** ADAM_STEP)
    delta = -ADAM_LR * (m_hat / (_jnp.sqrt(v_hat) + ADAM_EPS) + ADAM_WD * w)
    return delta, m, v


def reference(x, c, w_ada, b_ada, g_ffn1, w1_ffn1, w3_ffn1, w2_ffn1, g_mix, w_in, spatial_w, spatial_b, g_v, g_q, g_k, sinks, rel_bias, w_out, g_ffn2, w1_ffn2, w3_ffn2, w2_ffn2, loss_target, m_w_ada, m_b_ada, m_g_ffn1, m_w1_ffn1, m_w3_ffn1, m_w2_ffn1, m_g_mix, m_w_in, m_spatial_w, m_spatial_b, m_g_v, m_g_q, m_g_k, m_sinks, m_rel_bias, m_w_out, m_g_ffn2, m_w1_ffn2, m_w3_ffn2, m_w2_ffn2, v_w_ada, v_b_ada, v_g_ffn1, v_w1_ffn1, v_w3_ffn1, v_w2_ffn1, v_g_mix, v_w_in, v_spatial_w, v_spatial_b, v_g_v, v_g_q, v_g_k, v_sinks, v_rel_bias, v_w_out, v_g_ffn2, v_w1_ffn2, v_w3_ffn2, v_w2_ffn2):
    given = dict(x=x, c=c, w_ada=w_ada, b_ada=b_ada, g_ffn1=g_ffn1, w1_ffn1=w1_ffn1, w3_ffn1=w3_ffn1, w2_ffn1=w2_ffn1, g_mix=g_mix, w_in=w_in, spatial_w=spatial_w, spatial_b=spatial_b, g_v=g_v, g_q=g_q, g_k=g_k, sinks=sinks, rel_bias=rel_bias, w_out=w_out, g_ffn2=g_ffn2, w1_ffn2=w1_ffn2, w3_ffn2=w3_ffn2, w2_ffn2=w2_ffn2, loss_target=loss_target, m_w_ada=m_w_ada, m_b_ada=m_b_ada, m_g_ffn1=m_g_ffn1, m_w1_ffn1=m_w1_ffn1, m_w3_ffn1=m_w3_ffn1, m_w2_ffn1=m_w2_ffn1, m_g_mix=m_g_mix, m_w_in=m_w_in, m_spatial_w=m_spatial_w, m_spatial_b=m_spatial_b, m_g_v=m_g_v, m_g_q=m_g_q, m_g_k=m_g_k, m_sinks=m_sinks, m_rel_bias=m_rel_bias, m_w_out=m_w_out, m_g_ffn2=m_g_ffn2, m_w1_ffn2=m_w1_ffn2, m_w3_ffn2=m_w3_ffn2, m_w2_ffn2=m_w2_ffn2, v_w_ada=v_w_ada, v_b_ada=v_b_ada, v_g_ffn1=v_g_ffn1, v_w1_ffn1=v_w1_ffn1, v_w3_ffn1=v_w3_ffn1, v_w2_ffn1=v_w2_ffn1, v_g_mix=v_g_mix, v_w_in=v_w_in, v_spatial_w=v_spatial_w, v_spatial_b=v_spatial_b, v_g_v=v_g_v, v_g_q=v_g_q, v_g_k=v_g_k, v_sinks=v_sinks, v_rel_bias=v_rel_bias, v_w_out=v_w_out, v_g_ffn2=v_g_ffn2, v_w1_ffn2=v_w1_ffn2, v_w3_ffn2=v_w3_ffn2, v_w2_ffn2=v_w2_ffn2)
    weights = {n: given[n] for n in TWIN_WEIGHTS}
    shared = {n: given[n] for n in SHARED_INPUTS}
    per_example = {n: given[n] for n in ['x', 'c']}
    grad_fn = _jax.value_and_grad(_loss, argnums=(0, 1))

    def one_microbatch(ex, loss_target):
        ex = dict(ex)
        diff = ex.pop(TWIN_DIFF_INPUT)
        return grad_fn(weights, diff, {**shared, **ex}, loss_target)

    if N_MICROBATCH == 1:
        loss, (grad_w, grad_x) = one_microbatch(per_example, given["loss_target"])
    else:
        def body(carry, xs):
            loss_sum, grad_sum = carry
            l_k, (gw_k, gx_k) = one_microbatch(xs[0], xs[1])
            with _jax.named_scope("update"):
                return (loss_sum + l_k, _jax.tree.map(_jnp.add, grad_sum, gw_k)), gx_k

        init = (_jnp.zeros((), _jnp.float32), _jax.tree.map(_jnp.zeros_like, weights))
        (loss, grad_w), grad_x = _jax.lax.scan(body, init, (per_example, given["loss_target"]))
    with _jax.named_scope("update"):
        delta_w, new_m, new_v = {}, {}, {}
        for n in TWIN_WEIGHTS:
            delta_w[n], new_m[n], new_v[n] = _adamw(weights[n], grad_w[n], given["m_" + n], given["v_" + n])
    return (loss, grad_x, *[grad_w[n] for n in TWIN_WEIGHTS], *[delta_w[n] for n in TWIN_WEIGHTS],
            *[new_m[n] for n in TWIN_WEIGHTS], *[new_v[n] for n in TWIN_WEIGHTS])
```

```python
import functools
import math

import jax
import jax.numpy as jnp
import numpy as np
from jax import lax
from jax.experimental import pallas as pl
from jax.experimental.pallas import tpu as pltpu

F32 = jnp.float32
BF16 = jnp.bfloat16
MESH = pl.DeviceIdType.MESH
ANY = pl.BlockSpec(memory_space=pl.ANY)

EPS = 1e-6
BLOCK = 128
A_HEADS = 8
A_DIM = 128
A_WIDTH = A_HEADS * A_DIM
B_HEADS = 16
KV_HEADS = 2
GROUP = B_HEADS // KV_HEADS
HEAD_DIM = 64
B_WIDTH = B_HEADS * HEAD_DIM
KV_WIDTH = KV_HEADS * HEAD_DIM
Q_OFF = 2 * A_WIDTH
K_OFF = Q_OFF + B_WIDTH
V_OFF = K_OFF + KV_WIDTH
IN_COLS = V_OFF + KV_WIDTH
N_BUCKETS = 32
MAX_DISTANCE = 128
N_MOD = 9
N_CHIPS = 4
N_DEV = 8
NEG = -1e30

ADAM_LR = 0.001
ADAM_B1 = 0.9
ADAM_B2 = 0.999
ADAM_EPS = 1e-08
ADAM_WD = 0.01
ADAM_STEP = 10

LANES = 128
SUBLANES = 8
BF16_ROWS = 16
VMEM_LIMIT = 56 * 1024 * 1024

INV_SQRT2 = 1.0 / math.sqrt(2.0)
INV_SQRT_2PI = 1.0 / math.sqrt(2.0 * math.pi)


def _tile(n, pref, mult=LANES):
    t = (min(pref, n) // mult) * mult
    while t >= mult:
        if n % t == 0:
            return t
        t -= mult
    return n


def _params(sem):
    return pltpu.CompilerParams(dimension_semantics=sem, vmem_limit_bytes=VMEM_LIMIT)


def _dot(a, b, dims=(((1,), (0,)), ((), ()))):
    return lax.dot_general(a, b, dims, preferred_element_type=F32)


NN = (((1,), (0,)), ((), ()))
NT = (((1,), (1,)), ((), ()))
TN = (((0,), (0,)), ((), ()))


def _sigmoid(x):
    return 1.0 / (1.0 + jnp.exp(-x))


def _gelu_and_grad(x):
    cdf = 0.5 * (1.0 + lax.erf(x * INV_SQRT2))
    pdf = jnp.exp(-0.5 * x * x) * INV_SQRT_2PI
    return x * cdf, cdf + x * pdf


def _gelu(x):
    return x * (0.5 * (1.0 + lax.erf(x * INV_SQRT2)))


def _rms(x):
    r = lax.rsqrt(jnp.mean(x * x, axis=-1, keepdims=True) + EPS)
    return x * r, r


ROW_CHUNK = 64


def _for_rows(tm, fn):
    rc = min(ROW_CHUNK, tm)

    def step(r, carry):
        fn(pl.ds(pl.multiple_of(r * rc, rc), rc))
        return carry

    lax.fori_loop(0, tm // rc, step, 0)


def _rms_bwd(dy, xhat, r):
    return r * (dy - xhat * jnp.mean(dy * xhat, axis=-1, keepdims=True))


def _matmul(a, b, mode, out_dtype, tm, tn, tk, name, shard_major=False):
    if mode == "nn":
        (M, K), N = a.shape, b.shape[1]
    elif mode == "nt":
        (M, K), N = a.shape, b.shape[0]
    else:
        (K, M), N = a.shape, b.shape[1]
    tm, tn, tk = min(tm, M), min(tn, N), min(tk, K)
    assert M % tm == 0 and N % tn == 0 and K % tk == 0, (name, M, N, K, tm, tn, tk)
    nk = K // tk
    dims = {"nn": NN, "nt": NT, "tn": TN}[mode]
    a_spec = pl.BlockSpec((tk, tm), lambda i, j, k: (k, i)) if mode == "tn" else pl.BlockSpec((tm, tk), lambda i, j, k: (i, k))
    b_spec = pl.BlockSpec((tn, tk), lambda i, j, k: (j, k)) if mode == "nt" else pl.BlockSpec((tk, tn), lambda i, j, k: (k, j))
    if shard_major:
        assert tn * N_CHIPS == N
        out_shape = jax.ShapeDtypeStruct((N_CHIPS, M, tn), out_dtype)
        o_spec = pl.BlockSpec((None, tm, tn), lambda i, j, k: (j, i, 0))
    else:
        out_shape = jax.ShapeDtypeStruct((M, N), out_dtype)
        o_spec = pl.BlockSpec((tm, tn), lambda i, j, k: (i, j))

    def body(a_ref, b_ref, o_ref, acc_ref):
        k = pl.program_id(2)
        part = _dot(a_ref[...], b_ref[...], dims)
        if nk == 1:
            o_ref[...] = part.astype(o_ref.dtype)
            return

        @pl.when(k == 0)
        def _():
            acc_ref[...] = part

        @pl.when(k > 0)
        def _():
            acc_ref[...] += part

        @pl.when(k == nk - 1)
        def _():
            o_ref[...] = acc_ref[...].astype(o_ref.dtype)

    return pl.pallas_call(
        body,
        name=name,
        grid=(M // tm, N // tn, nk),
        in_specs=[a_spec, b_spec],
        out_specs=o_spec,
        out_shape=out_shape,
        scratch_shapes=[pltpu.VMEM((tm, tn), F32)],
        compiler_params=_params(("parallel", "parallel", "arbitrary")),
    )(a, b)


def _mod_partial(c_all, w_ada, b_sh, name):
    R, D = c_all.shape
    N = w_ada.shape[1]
    tn = _tile(N, 512)

    def body(c_ref, w_ref, b_ref, o_ref, ca_ref):
        cv = c_ref[...]
        ca = (cv * _sigmoid(cv)).astype(BF16)
        ca_ref[...] = ca
        o_ref[...] = _dot(ca, w_ref[...].astype(BF16)) + b_ref[...]

    return pl.pallas_call(
        body,
        name=name,
        grid=(N // tn,),
        in_specs=[
            pl.BlockSpec((R, D), lambda j: (0, 0)),
            pl.BlockSpec((D, tn), lambda j: (0, j)),
            pl.BlockSpec((1, tn), lambda j: (0, j)),
        ],
        out_specs=[pl.BlockSpec((R, tn), lambda j: (0, j)), pl.BlockSpec((R, D), lambda j: (0, 0))],
        out_shape=[jax.ShapeDtypeStruct((R, N), F32), jax.ShapeDtypeStruct((R, D), BF16)],
        compiler_params=_params(("arbitrary",)),
    )(c_all, w_ada, b_sh)


def _ffn_fwd(x, g, sh, sc, gt, w1, w3, w2, tgt, name):
    S, D = x.shape
    F = w1.shape[1]
    tm, tf = _tile(S, 512), _tile(F, 512)
    ni, nj = S // tm, F // tf
    with_loss = tgt is not None

    def body(*refs):
        if with_loss:
            (x_ref, g_ref, sh_ref, sc_ref, gt_ref, w1_ref, w3_ref, w2_ref, tgt_ref,
             gout_ref, df_ref, h_ref, a_ref, b_ref, dgt_ref, loss_ref, hs_ref, acc_ref) = refs
        else:
            (x_ref, g_ref, sh_ref, sc_ref, gt_ref, w1_ref, w3_ref, w2_ref,
             xo_ref, h_ref, a_ref, b_ref, f_ref, hs_ref, acc_ref) = refs
        i, j = pl.program_id(0), pl.program_id(1)

        @pl.when(j == 0)
        def _():
            def prologue(rows):
                xhat, _ = _rms(x_ref[rows, :])
                hb = ((xhat * g_ref[...]) * (1.0 + sc_ref[...]) + sh_ref[...]).astype(BF16)
                hs_ref[rows, :] = hb
                h_ref[rows, :] = hb

            _for_rows(tm, prologue)

        hb = hs_ref[...]
        av = _dot(hb, w1_ref[...])
        bv = _dot(hb, w3_ref[...])
        a_ref[...] = av.astype(BF16)
        b_ref[...] = bv.astype(BF16)
        sv = ((av * _sigmoid(av)) * bv).astype(BF16)
        part = _dot(sv, w2_ref[...])

        @pl.when(j == 0)
        def _():
            acc_ref[...] = part

        @pl.when(j > 0)
        def _():
            acc_ref[...] += part

        @pl.when(j == nj - 1)
        def _():
            if with_loss:
                @pl.when(i == 0)
                def _():
                    dgt_ref[...] = jnp.zeros(dgt_ref.shape, F32)
                    loss_ref[...] = jnp.zeros(loss_ref.shape, F32)

            def epilogue(rows):
                fv = acc_ref[rows, :]
                half_gate = 0.5 * gt_ref[...]
                xo = x_ref[rows, :] + half_gate * fv
                if not with_loss:
                    xo_ref[rows, :] = xo
                    f_ref[rows, :] = fv.astype(f_ref.dtype)
                    return
                err = xo - tgt_ref[rows, :]
                gout = err * (1.0 / D)
                gout_ref[rows, :] = gout
                df_ref[rows, :] = (half_gate * gout).astype(BF16)
                dgt_ref[...] += 0.5 * jnp.sum(gout * fv, axis=0, keepdims=True)
                loss_part = jnp.sum(jnp.sum(err * err, axis=1, keepdims=True), axis=0, keepdims=True)
                loss_ref[...] += jnp.broadcast_to(loss_part, loss_ref.shape)

            _for_rows(tm, epilogue)

    row = pl.BlockSpec((tm, D), lambda i, j: (i, 0))
    row_in = pl.BlockSpec((tm, D), lambda i, j: (i, 0), pipeline_mode=pl.Buffered(1))
    vec = pl.BlockSpec((1, D), lambda i, j: (0, 0))
    col = pl.BlockSpec((tm, tf), lambda i, j: (i, j))
    in_specs = [row_in, vec, vec, vec, vec,
                pl.BlockSpec((D, tf), lambda i, j: (0, j)),
                pl.BlockSpec((D, tf), lambda i, j: (0, j)),
                pl.BlockSpec((tf, D), lambda i, j: (j, 0))]
    args = [x, g, sh, sc, gt, w1, w3, w2]
    act = jax.ShapeDtypeStruct((S, F), BF16)
    if with_loss:
        in_specs.append(row_in)
        args.append(tgt)
        out_specs = [row, row, row, col, col, vec, pl.BlockSpec((1, LANES), lambda i, j: (0, 0))]
        out_shape = [jax.ShapeDtypeStruct((S, D), F32), jax.ShapeDtypeStruct((S, D), BF16),
                     jax.ShapeDtypeStruct((S, D), BF16), act, act,
                     jax.ShapeDtypeStruct((1, D), F32), jax.ShapeDtypeStruct((1, LANES), F32)]
    else:
        out_specs = [row, row, col, col, row]
        out_shape = [jax.ShapeDtypeStruct((S, D), F32), jax.ShapeDtypeStruct((S, D), BF16), act, act,
                     jax.ShapeDtypeStruct((S, D), BF16)]
    return pl.pallas_call(
        body,
        name=name,
        grid=(ni, nj),
        in_specs=in_specs,
        out_specs=out_specs,
        out_shape=out_shape,
        scratch_shapes=[pltpu.VMEM((tm, D), BF16), pltpu.VMEM((tm, D), F32)],
        compiler_params=_params(("arbitrary", "arbitrary")),
    )(*args)


def _ffn_bwd(df, a, b, w1, w3, w2, name):
    S, D = df.shape
    F = a.shape[1]
    tm, tf = _tile(S, 512), _tile(F, 512)
    nj = F // tf

    def body(df_ref, a_ref, b_ref, w1_ref, w3_ref, w2_ref, da_ref, db_ref, s_ref, dh_ref, acc_ref):
        j = pl.program_id(1)
        ds = _dot(df_ref[...], w2_ref[...], NT)
        av = a_ref[...].astype(F32)
        bv = b_ref[...].astype(F32)
        sig = _sigmoid(av)
        sil = av * sig
        da = ((ds * bv) * (sig * (1.0 + av * (1.0 - sig)))).astype(BF16)
        db = (ds * sil).astype(BF16)
        da_ref[...] = da
        db_ref[...] = db
        s_ref[...] = (sil * bv).astype(BF16)
        part = _dot(da, w1_ref[...], NT) + _dot(db, w3_ref[...], NT)

        @pl.when(j == 0)
        def _():
            acc_ref[...] = part

        @pl.when(j > 0)
        def _():
            acc_ref[...] += part

        @pl.when(j == nj - 1)
        def _():
            dh_ref[...] = acc_ref[...]

    row = pl.BlockSpec((tm, D), lambda i, j: (i, 0))
    col = pl.BlockSpec((tm, tf), lambda i, j: (i, j))
    act = jax.ShapeDtypeStruct((S, F), BF16)
    return pl.pallas_call(
        body,
        name=name,
        grid=(S // tm, nj),
        in_specs=[row, col, col,
                  pl.BlockSpec((D, tf), lambda i, j: (0, j)),
                  pl.BlockSpec((D, tf), lambda i, j: (0, j)),
                  pl.BlockSpec((tf, D), lambda i, j: (j, 0))],
        out_specs=[col, col, col, row],
        out_shape=[act, act, act, jax.ShapeDtypeStruct((S, D), F32)],
        scratch_shapes=[pltpu.VMEM((tm, D), F32)],
        compiler_params=_params(("parallel", "arbitrary")),
    )(df, a, b, w1, w3, w2)


def _norm_mod(x, g, sh, sc, name):
    S, D = x.shape
    tm = _tile(S, 512)

    def body(x_ref, g_ref, sh_ref, sc_ref, h_ref):
        def step(rows):
            xhat, _ = _rms(x_ref[rows, :])
            h_ref[rows, :] = ((xhat * g_ref[...]) * (1.0 + sc_ref[...]) + sh_ref[...]).astype(BF16)

        _for_rows(tm, step)

    row = pl.BlockSpec((tm, D), lambda i: (i, 0))
    vec = pl.BlockSpec((1, D), lambda i: (0, 0))
    return pl.pallas_call(
        body,
        name=name,
        grid=(S // tm,),
        in_specs=[row, vec, vec, vec],
        out_specs=row,
        out_shape=jax.ShapeDtypeStruct((S, D), BF16),
        compiler_params=_params(("parallel",)),
    )(x, g, sh, sc)


def _norm_bwd(dh, x, gres, g, sc, prev, name):
    S, D = x.shape
    tm = _tile(S, 256)
    has_prev = prev is not None
    coef = prev[2] if has_prev else None

    def body(*refs):
        if has_prev:
            (dh_ref, x_ref, gr_ref, g_ref, sc_ref, f_ref, gt_ref,
             go_ref, dsh_ref, dsc_ref, dg_ref, dp_ref, dgt_ref) = refs
        else:
            dh_ref, x_ref, gr_ref, g_ref, sc_ref, go_ref, dsh_ref, dsc_ref, dg_ref = refs
        sum_refs = [dsh_ref, dsc_ref, dg_ref] + ([dgt_ref] if has_prev else [])

        @pl.when(pl.program_id(0) == 0)
        def _():
            for ref in sum_refs:
                ref[...] = jnp.zeros(ref.shape, F32)

        def step(rows):
            dh = dh_ref[rows, :]
            xhat, r = _rms(x_ref[rows, :])
            gain = g_ref[...]
            scale1 = 1.0 + sc_ref[...]
            gout = gr_ref[rows, :] + _rms_bwd(dh * scale1 * gain, xhat, r)
            go_ref[rows, :] = gout
            sums = [dh, dh * (xhat * gain), dh * scale1 * xhat]
            if has_prev:
                dp_ref[rows, :] = ((coef * gt_ref[...]) * gout).astype(BF16)
                sums.append(coef * (gout * f_ref[rows, :].astype(F32)))
            for ref, v in zip(sum_refs, sums):
                ref[...] += jnp.sum(v, axis=0, keepdims=True)

        _for_rows(tm, step)

    row = pl.BlockSpec((tm, D), lambda i: (i, 0))
    vec = pl.BlockSpec((1, D), lambda i: (0, 0))
    vshape = jax.ShapeDtypeStruct((1, D), F32)
    in_specs = [row, row, row, vec, vec]
    args = [dh, x, gres, g, sc]
    out_specs = [row, vec, vec, vec]
    out_shape = [jax.ShapeDtypeStruct((S, D), F32), vshape, vshape, vshape]
    if has_prev:
        in_specs += [row, vec]
        args += [prev[0], prev[1]]
        out_specs += [row, vec]
        out_shape += [jax.ShapeDtypeStruct((S, D), BF16), vshape]
    return pl.pallas_call(
        body,
        name=name,
        grid=(S // tm,),
        in_specs=in_specs,
        out_specs=out_specs,
        out_shape=out_shape,
        compiler_params=_params(("arbitrary",)),
    )(*args)


def _attn_probs(qn_b, kb, bias, sink):
    s = _dot(qn_b, kb, NT) * (HEAD_DIM ** -0.5) + bias
    m = jnp.maximum(jnp.max(s, axis=-1, keepdims=True), sink)
    p = jnp.exp(s - m)
    e_sink = jnp.exp(sink - m)
    inv = 1.0 / (jnp.sum(p, axis=-1, keepdims=True) + e_sink)
    return p * inv, e_sink * inv


def _mixer_fwd(z, wm, sb_t, gv, gq, gk, sinks, biasm, name):
    S = z.shape[0]
    nb = S // BLOCK

    def body(z_ref, zp_ref, wm_ref, sbt_ref, gv_ref, gq_ref, gk_ref, sk_ref, bias_ref, mix_ref):
        for h in range(A_HEADS):
            lo = h * A_DIM
            u = _gelu(z_ref[:, lo:lo + A_DIM])
            v = _gelu(z_ref[:, A_WIDTH + lo:A_WIDTH + lo + A_DIM])
            vhat, _ = _rms(v)
            vn = (vhat * gv_ref[h:h + 1, :]).astype(BF16)
            mixed = _dot(wm_ref[h], vn) + sbt_ref[:, h:h + 1]
            mix_ref[:, lo:lo + A_DIM] = (u * mixed).astype(BF16)
        gq_v, gk_v = gq_ref[...], gk_ref[...]
        for kh in range(KV_HEADS):
            ko = kh * HEAD_DIM
            kp, _ = _rms(zp_ref[:, ko:ko + HEAD_DIM])
            kc, _ = _rms(z_ref[:, K_OFF + ko:K_OFF + ko + HEAD_DIM])
            kb = jnp.concatenate([kp * gk_v, kc * gk_v], axis=0).astype(BF16)
            vb = jnp.concatenate([zp_ref[:, KV_WIDTH + ko:KV_WIDTH + ko + HEAD_DIM],
                                  z_ref[:, V_OFF + ko:V_OFF + ko + HEAD_DIM]], axis=0).astype(BF16)
            for g in range(GROUP):
                h = kh * GROUP + g
                qo = Q_OFF + h * HEAD_DIM
                qhat, _ = _rms(z_ref[:, qo:qo + HEAD_DIM])
                w, _ = _attn_probs((qhat * gq_v).astype(BF16), kb, bias_ref[h], sk_ref[h])
                o = _dot(w.astype(BF16), vb)
                mix_ref[:, A_WIDTH + h * HEAD_DIM:A_WIDTH + (h + 1) * HEAD_DIM] = o.astype(BF16)

    full = lambda shape: pl.BlockSpec(shape, lambda n: (0,) * len(shape))
    return pl.pallas_call(
        body,
        name=name,
        grid=(nb,),
        in_specs=[
            pl.BlockSpec((BLOCK, IN_COLS), lambda n: (n, 0)),
            pl.BlockSpec((BLOCK, 2 * KV_WIDTH), lambda n: (jnp.maximum(n - 1, 0), K_OFF // (2 * KV_WIDTH))),
            full((A_HEADS, BLOCK, BLOCK)), full((BLOCK, A_HEADS)), full((A_HEADS, A_DIM)),
            full((1, HEAD_DIM)), full((1, HEAD_DIM)),
            pl.BlockSpec(memory_space=pltpu.SMEM),
            pl.BlockSpec((None, B_HEADS, BLOCK, 2 * BLOCK), lambda n: (jnp.minimum(n, 1), 0, 0, 0)),
        ],
        out_specs=pl.BlockSpec((BLOCK, A_WIDTH + B_WIDTH), lambda n: (n, 0)),
        out_shape=jax.ShapeDtypeStruct((S, A_WIDTH + B_WIDTH), BF16),
        compiler_params=_params(("parallel",)),
    )(z, z, wm, sb_t, gv, gq, gk, sinks, biasm)


def _mixer_bwd(z, dmix, wm, wm_t, sb_t, gv, gq, gk, sinks, biasm, name):
    S = z.shape[0]
    nb = S // BLOCK

    def body(z_ref, zp_ref, dmix_ref, wm_ref, wmt_ref, sbt_ref, gv_ref, gq_ref, gk_ref, sk_ref, bias_ref,
             dz_ref, dzkv_ref, dwm_ref, dsb_ref, dgv_ref, dgq_ref, dgk_ref, dsk_ref, dst_ref,
             carry_ref, tot_ref, sbacc_ref):
        n = pl.program_id(0)

        @pl.when(n == 0)
        def _():
            for ref in (dwm_ref, dgv_ref, dgq_ref, dgk_ref, dsk_ref, dst_ref, carry_ref, sbacc_ref):
                ref[...] = jnp.zeros(ref.shape, ref.dtype)

        @pl.when(n < nb)
        def _():
            for h in range(A_HEADS):
                lo = h * A_DIM
                u, du_dz = _gelu_and_grad(z_ref[:, lo:lo + A_DIM])
                v, dv_dz = _gelu_and_grad(z_ref[:, A_WIDTH + lo:A_WIDTH + lo + A_DIM])
                vhat, rv = _rms(v)
                gvh = gv_ref[h:h + 1, :]
                vn = (vhat * gvh).astype(BF16)
                mixed = _dot(wm_ref[h], vn) + sbt_ref[:, h:h + 1]
                dya = dmix_ref[:, lo:lo + A_DIM].astype(F32)
                dmx = dya * u
                sbacc_ref[h] += dmx
                dmx_b = dmx.astype(BF16)
                dwm_ref[h] += _dot(dmx_b, vn, NT)
                dvn = _dot(wmt_ref[h], dmx_b)
                dgv_ref[h:h + 1, :] += jnp.sum(dvn * vhat, axis=0, keepdims=True)
                dz_ref[:, lo:lo + A_DIM] = ((dya * mixed) * du_dz).astype(BF16)
                dz_ref[:, A_WIDTH + lo:A_WIDTH + lo + A_DIM] = (_rms_bwd(dvn * gvh, vhat, rv) * dv_dz).astype(BF16)
            gq_v, gk_v = gq_ref[...], gk_ref[...]
            dgq = jnp.zeros((1, HEAD_DIM), F32)
            for kh in range(KV_HEADS):
                ko = kh * HEAD_DIM
                kp, _ = _rms(zp_ref[:, ko:ko + HEAD_DIM])
                kc, _ = _rms(z_ref[:, K_OFF + ko:K_OFF + ko + HEAD_DIM])
                kb = jnp.concatenate([kp * gk_v, kc * gk_v], axis=0).astype(BF16)
                vb = jnp.concatenate([zp_ref[:, KV_WIDTH + ko:KV_WIDTH + ko + HEAD_DIM],
                                      z_ref[:, V_OFF + ko:V_OFF + ko + HEAD_DIM]], axis=0).astype(BF16)
                dkb = jnp.zeros((2 * BLOCK, HEAD_DIM), F32)
                dvb = jnp.zeros((2 * BLOCK, HEAD_DIM), F32)
                for g in range(GROUP):
                    h = kh * GROUP + g
                    qo = Q_OFF + h * HEAD_DIM
                    qhat, rq = _rms(z_ref[:, qo:qo + HEAD_DIM])
                    qn_b = (qhat * gq_v).astype(BF16)
                    w, w_sink = _attn_probs(qn_b, kb, bias_ref[h], sk_ref[h])
                    do = dmix_ref[:, A_WIDTH + h * HEAD_DIM:A_WIDTH + (h + 1) * HEAD_DIM]
                    dp = _dot(do, vb, NT)
                    delta = jnp.sum(w * dp, axis=-1, keepdims=True)
                    ds = w * (dp - delta)
                    dsk_ref[0:1, h:h + 1] += -jnp.sum(w_sink * delta, axis=0, keepdims=True)
                    dst_ref[h] += ds
                    ds_b = (ds * (HEAD_DIM ** -0.5)).astype(BF16)
                    dqn = _dot(ds_b, kb)
                    dkb += _dot(ds_b, qn_b, TN)
                    dvb += _dot(w.astype(BF16), do, TN)
                    dgq += jnp.sum(dqn * qhat, axis=0, keepdims=True)
                    dz_ref[:, qo:qo + HEAD_DIM] = _rms_bwd(dqn * gq_v, qhat, rq).astype(BF16)
                tot_ref[0, :, ko:ko + HEAD_DIM] = carry_ref[0, :, ko:ko + HEAD_DIM] + dkb[:BLOCK]
                tot_ref[1, :, ko:ko + HEAD_DIM] = carry_ref[1, :, ko:ko + HEAD_DIM] + dvb[:BLOCK]
                carry_ref[0, :, ko:ko + HEAD_DIM] = dkb[BLOCK:]
                carry_ref[1, :, ko:ko + HEAD_DIM] = dvb[BLOCK:]
            dgq_ref[...] += dgq

        @pl.when(n == nb)
        def _():
            tot_ref[...] = carry_ref[...]
            for h in range(A_HEADS):
                dsb_ref[:, h:h + 1] = jnp.sum(sbacc_ref[h], axis=1, keepdims=True)

        gk_v = gk_ref[...]
        dgk = jnp.zeros((1, HEAD_DIM), F32)
        for kh in range(KV_HEADS):
            ko = kh * HEAD_DIM
            khat, rk = _rms(zp_ref[:, ko:ko + HEAD_DIM])
            dkn = tot_ref[0, :, ko:ko + HEAD_DIM]
            dgk += jnp.sum(dkn * khat, axis=0, keepdims=True)
            dzkv_ref[:, ko:ko + HEAD_DIM] = _rms_bwd(dkn * gk_v, khat, rk).astype(BF16)
        dgk_ref[...] += dgk
        dzkv_ref[:, KV_WIDTH:] = tot_ref[1].astype(BF16)

    last = nb - 1
    full = lambda shape: pl.BlockSpec(shape, lambda n: (0,) * len(shape))
    return pl.pallas_call(
        body,
        name=name,
        grid=(nb + 1,),
        in_specs=[
            pl.BlockSpec((BLOCK, IN_COLS), lambda n: (jnp.minimum(n, last), 0)),
            pl.BlockSpec((BLOCK, 2 * KV_WIDTH), lambda n: (jnp.maximum(n - 1, 0), K_OFF // (2 * KV_WIDTH))),
            pl.BlockSpec((BLOCK, A_WIDTH + B_WIDTH), lambda n: (jnp.minimum(n, last), 0)),
            full((A_HEADS, BLOCK, BLOCK)), full((A_HEADS, BLOCK, BLOCK)), full((BLOCK, A_HEADS)),
            full((A_HEADS, A_DIM)), full((1, HEAD_DIM)), full((1, HEAD_DIM)),
            pl.BlockSpec(memory_space=pltpu.SMEM),
            pl.BlockSpec((None, B_HEADS, BLOCK, 2 * BLOCK), lambda n: (jnp.minimum(n, 1), 0, 0, 0)),
        ],
        out_specs=[
            pl.BlockSpec((BLOCK, K_OFF), lambda n: (jnp.minimum(n, last), 0)),
            pl.BlockSpec((BLOCK, 2 * KV_WIDTH), lambda n: (jnp.maximum(n - 1, 0), 0)),
            full((A_HEADS, BLOCK, BLOCK)), full((BLOCK, A_HEADS)), full((A_HEADS, A_DIM)),
            full((1, HEAD_DIM)), full((1, HEAD_DIM)), full((1, B_HEADS)),
            full((B_HEADS, BLOCK, 2 * BLOCK)),
        ],
        out_shape=[
            jax.ShapeDtypeStruct((S, K_OFF), BF16),
            jax.ShapeDtypeStruct((S, 2 * KV_WIDTH), BF16),
            jax.ShapeDtypeStruct((A_HEADS, BLOCK, BLOCK), F32),
            jax.ShapeDtypeStruct((BLOCK, A_HEADS), F32),
            jax.ShapeDtypeStruct((A_HEADS, A_DIM), F32),
            jax.ShapeDtypeStruct((1, HEAD_DIM), F32),
            jax.ShapeDtypeStruct((1, HEAD_DIM), F32),
            jax.ShapeDtypeStruct((1, B_HEADS), F32),
            jax.ShapeDtypeStruct((B_HEADS, BLOCK, 2 * BLOCK), F32),
        ],
        scratch_shapes=[
            pltpu.VMEM((2, BLOCK, KV_WIDTH), F32),
            pltpu.VMEM((2, BLOCK, KV_WIDTH), F32),
            pltpu.VMEM((A_HEADS, BLOCK, A_DIM), F32),
        ],
        compiler_params=_params(("arbitrary",)),
    )(z, z, dmix, wm, wm_t, sb_t, gv, gq, gk, sinks, biasm)


def _mixer_out(mix, w_out, x, gt, name):
    S, D = x.shape
    K = mix.shape[1]
    tm, tn = _tile(S, 512), _tile(D, 1024)

    def body(m_ref, w_ref, x_ref, gt_ref, xo_ref, y_ref):
        y = _dot(m_ref[...], w_ref[...])
        y_ref[...] = y
        xo_ref[...] = x_ref[...] + gt_ref[...] * y

    blk = pl.BlockSpec((tm, tn), lambda j, i: (i, j))
    return pl.pallas_call(
        body,
        name=name,
        grid=(D // tn, S // tm),
        in_specs=[pl.BlockSpec((tm, K), lambda j, i: (i, 0)), pl.BlockSpec((K, tn), lambda j, i: (0, j)),
                  blk, pl.BlockSpec((1, tn), lambda j, i: (0, j))],
        out_specs=[blk, blk],
        out_shape=[jax.ShapeDtypeStruct((S, D), F32), jax.ShapeDtypeStruct((S, D), F32)],
        compiler_params=_params(("parallel", "parallel")),
    )(mix, w_out, x, gt)


def _bucket_sum(dst, onehot, name):
    def body(d_ref, o_ref, out_ref):
        out_ref[...] = lax.dot_general(o_ref[...], d_ref[...], NT, precision=lax.Precision.HIGHEST,
                                       preferred_element_type=F32)

    return pl.pallas_call(
        body,
        name=name,
        out_shape=jax.ShapeDtypeStruct((N_BUCKETS, B_HEADS), F32),
    )(dst, onehot)


def _adamw_math(w, g, m, v):
    m = ADAM_B1 * m + (1.0 - ADAM_B1) * g
    v = ADAM_B2 * v + (1.0 - ADAM_B2) * (g * g)
    m_hat = m / (1.0 - ADAM_B1 ** ADAM_STEP)
    v_hat = v / (1.0 - ADAM_B2 ** ADAM_STEP)
    delta = -ADAM_LR * (m_hat / (jnp.sqrt(v_hat) + ADAM_EPS) + ADAM_WD * w)
    return delta, m, v


def _adamw(w, g, m, v, name):
    R, C = w.shape
    tr = _tile(R, max(SUBLANES, (1 << 19) // C), SUBLANES)

    def body(w_ref, g_ref, m_ref, v_ref, d_ref, mo_ref, vo_ref):
        d, mn, vn = _adamw_math(w_ref[...], g_ref[...], m_ref[...], v_ref[...])
        d_ref[...] = d
        mo_ref[...] = mn
        vo_ref[...] = vn

    blk = pl.BlockSpec((tr, C), lambda i: (i, 0))
    shape = jax.ShapeDtypeStruct((R, C), F32)
    return pl.pallas_call(
        body,
        name=name,
        grid=(R // tr,),
        in_specs=[blk] * 4,
        out_specs=[blk] * 3,
        out_shape=[shape] * 3,
        compiler_params=_params(("parallel",)),
    )(w, g, m, v)


def _small_update(parts, w, m, v, name):
    R = w.shape[0]

    def body(p_ref, w_ref, m_ref, v_ref, g_ref, d_ref, mo_ref, vo_ref):
        g = p_ref[0]
        for dev in range(1, N_DEV):
            g = g + p_ref[dev]
        g_ref[...] = g
        d, mn, vn = _adamw_math(w_ref[...], g, m_ref[...], v_ref[...])
        d_ref[...] = d
        mo_ref[...] = mn
        vo_ref[...] = vn

    shape = jax.ShapeDtypeStruct((R, LANES), F32)
    return pl.pallas_call(
        body,
        name=name,
        out_shape=[shape] * 4,
        compiler_params=pltpu.CompilerParams(vmem_limit_bytes=VMEM_LIMIT),
    )(parts, w, m, v)


def _place():
    x, y, c = lax.axis_index("x"), lax.axis_index("y"), lax.axis_index("c")
    chips = [(1 - x, y), (x, 1 - y), (1 - x, 1 - y)]
    return x, y, c, chips


def _remote(src, dst, send_sem, recv_sem, to):
    return pltpu.make_async_remote_copy(src_ref=src, dst_ref=dst, send_sem=send_sem, recv_sem=recv_sem,
                                        device_id=to, device_id_type=MESH)


def _allgather_small(block, name):
    m_per, n = block.shape

    def body(x_ref, out_ref, send_sems, recv_sems, local_sem):
        x, y, c, chips = _place()
        me, sibling = (x, y, c), (x, y, 1 - c)

        def rows(px, py, pc):
            return out_ref.at[pl.ds((4 * px + 2 * py + pc) * m_per, m_per), :]

        def copy(k, blk, to, src=None):
            return _remote(rows(*blk) if src is None else src, rows(*blk), send_sems.at[k], recv_sems.at[k], to)

        mine = pltpu.make_async_copy(x_ref, rows(*me), local_sem)
        mine.start()
        first = [copy(0, me, sibling, src=x_ref)]
        first += [copy(1 + j, me, (*chip, c), src=x_ref) for j, chip in enumerate(chips)]
        for cp in first:
            cp.start()
        passed = [copy(4 + j, (*chip, c), sibling) for j, chip in enumerate(chips)]
        for j, chip in enumerate(chips):
            copy(1 + j, (*chip, c), me).wait_recv()
            passed[j].start()
        copy(0, sibling, me).wait_recv()
        for j, chip in enumerate(chips):
            copy(4 + j, (*chip, 1 - c), me).wait_recv()
        for cp in first + passed:
            cp.wait_send()
        mine.wait()

    return pl.pallas_call(
        body,
        name=name,
        out_shape=jax.ShapeDtypeStruct((N_DEV * m_per, n), block.dtype),
        in_specs=[pl.BlockSpec(memory_space=pltpu.VMEM)],
        out_specs=pl.BlockSpec(memory_space=pltpu.VMEM),
        scratch_shapes=[pltpu.SemaphoreType.DMA((7,)), pltpu.SemaphoreType.DMA((7,)), pltpu.SemaphoreType.DMA],
        compiler_params=pltpu.CompilerParams(vmem_limit_bytes=VMEM_LIMIT),
    )(block)


def _half(ref, c, rows):
    start = pl.multiple_of(c * rows, BF16_ROWS)
    if len(ref.shape) == 2:
        return ref.at[pl.ds(start, rows), :]
    return ref.at[:, pl.ds(start, rows), :]


def _gather_weights(shards, name):
    n = len(shards)

    def body(*refs):
        ins, outs = refs[:n], refs[n:2 * n]
        send_sems, recv_sems, local_sems = refs[2 * n:]
        x, y, c, chips = _place()
        j = 2 * x + y
        sibling = (x, y, 1 - c)
        hrows = [s.shape[0] // 2 for s in shards]
        local = [pltpu.make_async_copy(ins[t], outs[t].at[j], local_sems.at[t]) for t in range(n)]
        for cp in local:
            cp.start()
        sent = []
        for t in range(n):
            for k, (px, py) in enumerate(chips):
                cp = _remote(_half(ins[t], c, hrows[t]), _half(outs[t].at[j], c, hrows[t]),
                             send_sems.at[6 * t + k], recv_sems.at[6 * t + k], (px, py, c))
                cp.start()
                sent.append(cp)
        for t in range(n):
            for k, (px, py) in enumerate(chips):
                landed = _half(outs[t].at[2 * px + py], c, hrows[t])
                _remote(landed, landed, send_sems.at[6 * t + k], recv_sems.at[6 * t + k], (px, py, c)).wait_recv()
                cp = _remote(landed, landed, send_sems.at[6 * t + 3 + k], recv_sems.at[6 * t + 3 + k], sibling)
                cp.start()
                sent.append(cp)
        for t in range(n):
            for k, (px, py) in enumerate(chips):
                other = _half(outs[t].at[2 * px + py], 1 - c, hrows[t])
                _remote(other, other, send_sems.at[6 * t + 3 + k], recv_sems.at[6 * t + 3 + k], sibling).wait_recv()
        for cp in sent:
            cp.wait_send()
        for cp in local:
            cp.wait()

    return pl.pallas_call(
        body,
        name=name,
        out_shape=[jax.ShapeDtypeStruct((N_CHIPS,) + s.shape, s.dtype) for s in shards],
        in_specs=[ANY] * n,
        out_specs=[ANY] * n,
        scratch_shapes=[pltpu.SemaphoreType.DMA((6 * n,)), pltpu.SemaphoreType.DMA((6 * n,)),
                        pltpu.SemaphoreType.DMA((n,))],
    )(*shards)


def _swap_halves(grads, name):
    n = len(grads)

    def body(*refs):
        ins, outs = refs[:n], refs[n:2 * n]
        send_sems, recv_sems = refs[2 * n:]
        x, y, c, _ = _place()
        sibling = (x, y, 1 - c)
        sent = []
        for t in range(n):
            cp = _remote(_half(ins[t], 1 - c, grads[t].shape[1] // 2), outs[t],
                         send_sems.at[t], recv_sems.at[t], sibling)
            cp.start()
            sent.append(cp)
        for cp in sent:
            cp.wait_recv()
        for cp in sent:
            cp.wait_send()

    return pl.pallas_call(
        body,
        name=name,
        out_shape=[jax.ShapeDtypeStruct((g.shape[0], g.shape[1] // 2, g.shape[2]), g.dtype) for g in grads],
        in_specs=[ANY] * n,
        out_specs=[ANY] * n,
        scratch_shapes=[pltpu.SemaphoreType.DMA((n,)), pltpu.SemaphoreType.DMA((n,))],
    )(*grads)


def _scatter_to_owners(sums, name):
    n = len(sums)

    def body(*refs):
        ins, outs = refs[:n], refs[n:2 * n]
        send_sems, recv_sems = refs[2 * n:]
        x, y, c, chips = _place()
        sent = []
        for t in range(n):
            for k, (px, py) in enumerate(chips):
                cp = _remote(ins[t].at[2 * px + py], outs[t].at[k], send_sems.at[3 * t + k], recv_sems.at[3 * t + k],
                             (px, py, c))
                cp.start()
                sent.append(cp)
        for cp in sent:
            cp.wait_recv()
        for cp in sent:
            cp.wait_send()

    return pl.pallas_call(
        body,
        name=name,
        out_shape=[jax.ShapeDtypeStruct((N_CHIPS - 1,) + s.shape[1:], s.dtype) for s in sums],
        in_specs=[ANY] * n,
        out_specs=[ANY] * n,
        scratch_shapes=[pltpu.SemaphoreType.DMA((3 * n,)), pltpu.SemaphoreType.DMA((3 * n,))],
    )(*sums)


def _join_halves(halves, name):
    n = len(halves)

    def body(*refs):
        ins, outs = refs[:n], refs[n:2 * n]
        send_sems, recv_sems, local_sems = refs[2 * n:]
        x, y, c, _ = _place()
        sibling = (x, y, 1 - c)
        copies = []
        for t in range(n):
            rows = halves[t].shape[0]
            mine = pltpu.make_async_copy(ins[t], _half(outs[t], c, rows), local_sems.at[t])
            mine.start()
            cp = _remote(ins[t], _half(outs[t], c, rows), send_sems.at[t], recv_sems.at[t], sibling)
            cp.start()
            copies.append((mine, cp, rows))
        for t, (mine, cp, rows) in enumerate(copies):
            other = _half(outs[t], 1 - c, rows)
            _remote(other, other, send_sems.at[t], recv_sems.at[t], sibling).wait_recv()
        for mine, cp, _ in copies:
            cp.wait_send()
            mine.wait()

    return pl.pallas_call(
        body,
        name=name,
        out_shape=[jax.ShapeDtypeStruct((2 * h.shape[0], h.shape[1]), h.dtype) for h in halves],
        in_specs=[ANY] * n,
        out_specs=[ANY] * n,
        scratch_shapes=[pltpu.SemaphoreType.DMA((n,)), pltpu.SemaphoreType.DMA((n,)), pltpu.SemaphoreType.DMA((n,))],
    )(*halves)


def _chip_sum(grad, recv, c_arr, name):
    _, A, B = grad.shape
    hA = A // 2
    ta = _tile(hA, max(BF16_ROWS, (1 << 19) // B), BF16_ROWS)
    nh = hA // ta

    def body(c_ref, g_ref, r_ref, o_ref):
        o_ref[...] = (g_ref[...] + r_ref[...]).astype(BF16)

    return pl.pallas_call(
        body,
        name=name,
        grid_spec=pltpu.PrefetchScalarGridSpec(
            num_scalar_prefetch=1,
            grid=(N_CHIPS, nh),
            in_specs=[pl.BlockSpec((None, ta, B), lambda s, i, c_ref: (s, c_ref[0] * nh + i, 0)),
                      pl.BlockSpec((None, ta, B), lambda s, i, c_ref: (s, i, 0))],
            out_specs=pl.BlockSpec((None, ta, B), lambda s, i, c_ref: (s, i, 0)),
        ),
        out_shape=jax.ShapeDtypeStruct((N_CHIPS, hA, B), BF16),
        compiler_params=_params(("parallel", "parallel")),
    )(c_arr, grad, recv)


def _owner_sum(grad, recv, landed, jc_arr, name):
    _, A, B = grad.shape
    hA = A // 2
    ta = _tile(hA, max(BF16_ROWS, (1 << 19) // B), BF16_ROWS)
    nh = hA // ta

    def body(jc_ref, g_ref, r_ref, l0_ref, l1_ref, l2_ref, o_ref):
        total = g_ref[...] + r_ref[...]
        for ref in (l0_ref, l1_ref, l2_ref):
            total = total + ref[...].astype(F32)
        o_ref[...] = total

    def landed_spec(k):
        return pl.BlockSpec((None, ta, B), lambda i, jc_ref: (k, i, 0))

    return pl.pallas_call(
        body,
        name=name,
        grid_spec=pltpu.PrefetchScalarGridSpec(
            num_scalar_prefetch=1,
            grid=(nh,),
            in_specs=[pl.BlockSpec((None, ta, B), lambda i, jc_ref: (jc_ref[0], jc_ref[1] * nh + i, 0)),
                      pl.BlockSpec((None, ta, B), lambda i, jc_ref: (jc_ref[0], i, 0)),
                      landed_spec(0), landed_spec(1), landed_spec(2)],
            out_specs=pl.BlockSpec((ta, B), lambda i, jc_ref: (i, 0)),
        ),
        out_shape=jax.ShapeDtypeStruct((hA, B), F32),
        compiler_params=_params(("parallel",)),
    )(jc_arr, grad, recv, landed, landed, landed)


def _reduce_scatter(grads, tag):
    x, y, c = lax.axis_index("x"), lax.axis_index("y"), lax.axis_index("c")
    c_arr = jnp.reshape(c, (1,)).astype(jnp.int32)
    jc_arr = jnp.stack([2 * x + y, c]).astype(jnp.int32)
    recv = _swap_halves(grads, f"swap_halves_{tag}")
    sums = [_chip_sum(g, r, c_arr, f"chip_sum_{tag}_{t}") for t, (g, r) in enumerate(zip(grads, recv))]
    landed = _scatter_to_owners(sums, f"scatter_{tag}")
    halves = [_owner_sum(g, r, l, jc_arr, f"owner_sum_{tag}_{t}")
              for t, (g, r, l) in enumerate(zip(grads, recv, landed))]
    return _join_halves(halves, f"join_halves_{tag}")


def _pack(parts):
    rows = []
    for p in parts:
        flat = jnp.reshape(p.astype(F32), (-1,))
        tile = SUBLANES * LANES
        padded = -(-flat.shape[0] // tile) * tile
        rows.append(jnp.reshape(jnp.pad(flat, (0, padded - flat.shape[0])), (-1, LANES)))
    return jnp.concatenate(rows, axis=0)


def _unpack(pack, shapes):
    out, row = [], 0
    for shape in shapes:
        size = int(np.prod(shape))
        nrows = -(-size // (SUBLANES * LANES)) * SUBLANES
        out.append(jnp.reshape(jnp.reshape(pack[row:row + nrows], (-1,))[:size], shape))
        row += nrows
    return out


def _bias_tables():
    qi = np.arange(BLOCK)[:, None]
    kj = np.arange(2 * BLOCK)[None, :]
    dist = qi + BLOCK - kj
    in_window = (dist >= 0) & (dist < BLOCK)
    n = np.clip(dist, 0, None)
    max_exact = N_BUCKETS // 2
    nf = np.maximum(n, 1).astype(np.float32)
    large = max_exact + (np.log(nf / max_exact) / math.log(MAX_DISTANCE / max_exact)
                         * (N_BUCKETS - max_exact)).astype(np.int32)
    large = np.minimum(large, N_BUCKETS - 1)
    bucket = np.where(n < max_exact, n, large)
    onehot = (bucket[None] == np.arange(N_BUCKETS)[:, None, None]) & in_window[None]
    first = in_window & (kj >= BLOCK)
    return onehot.astype(np.float32), in_window, first


def kernel(x, c, w_ada, b_ada, g_ffn1, w1_ffn1, w3_ffn1, w2_ffn1, g_mix, w_in, spatial_w, spatial_b, g_v, g_q, g_k, sinks, rel_bias, w_out, g_ffn2, w1_ffn2, w3_ffn2, w2_ffn2, loss_target, m_w_ada, m_b_ada, m_g_ffn1, m_w1_ffn1, m_w3_ffn1, m_w2_ffn1, m_g_mix, m_w_in, m_spatial_w, m_spatial_b, m_g_v, m_g_q, m_g_k, m_sinks, m_rel_bias, m_w_out, m_g_ffn2, m_w1_ffn2, m_w3_ffn2, m_w2_ffn2, v_w_ada, v_b_ada, v_g_ffn1, v_w1_ffn1, v_w3_ffn1, v_w2_ffn1, v_g_mix, v_w_in, v_spatial_w, v_spatial_b, v_g_v, v_g_q, v_g_k, v_sinks, v_rel_bias, v_w_out, v_g_ffn2, v_w1_ffn2, v_w3_ffn2, v_w2_ffn2):
    ax, ay, ac = lax.axis_index("x"), lax.axis_index("y"), lax.axis_index("c")
    chip = 2 * ax + ay
    dev = 2 * chip + ac
    xs = x[0]
    tgt = loss_target[0]
    S, D = xs.shape
    F = N_CHIPS * w1_ffn1.shape[2]
    mod_cols = w_ada.shape[2]

    c_all = _allgather_small(jnp.pad(c, ((0, SUBLANES - 1), (0, 0))), "gather_c")
    c_all = jnp.pad(c_all[::SUBLANES], ((0, BF16_ROWS - N_DEV), (0, 0)))
    b_sh = lax.dynamic_slice(b_ada, (0, chip * mod_cols), (1, mod_cols))
    mod_part, c_act = _mod_partial(c_all, w_ada[0], b_sh, "mod_partial")
    mod_all = _allgather_small(mod_part[:N_DEV], "gather_mod")
    mod_all = jnp.reshape(mod_all, (N_CHIPS, 2, N_DEV, mod_cols))[:, 0]
    mod = jnp.reshape(lax.dynamic_index_in_dim(mod_all, dev, axis=1, keepdims=False), (1, N_MOD * D))
    sh1, sc1, gt1, sh2, sc2, gt2, sh3, sc3, gt3 = [mod[:, i * D:(i + 1) * D] for i in range(N_MOD)]

    def cols_to_natural(w4):
        return jnp.reshape(jnp.transpose(w4, (1, 0, 2)), (w4.shape[1], -1))

    bf = lambda w: w[0].astype(BF16)
    g1w1, g1w3, g1w2 = _gather_weights([bf(w1_ffn1), bf(w3_ffn1), bf(w2_ffn1)], "gather_ffn1")
    gwin, gwout = _gather_weights([bf(w_in), bf(w_out)], "gather_mixer")
    g2w1, g2w3, g2w2 = _gather_weights([bf(w1_ffn2), bf(w3_ffn2), bf(w2_ffn2)], "gather_ffn2")
    w1a, w3a, w2a = cols_to_natural(g1w1), cols_to_natural(g1w3), jnp.reshape(g1w2, (F, D))
    w1b, w3b, w2b = cols_to_natural(g2w1), cols_to_natural(g2w3), jnp.reshape(g2w2, (F, D))
    win, wout = cols_to_natural(gwin), jnp.reshape(gwout, (-1, D))

    onehot_np, in_window_np, first_np = _bias_tables()
    onehot = jnp.asarray(onehot_np)
    bias = jnp.einsum("bij,bh->hij", onehot, rel_bias, precision=lax.Precision.HIGHEST)
    biasm = jnp.stack([jnp.where(jnp.asarray(first_np)[None], bias, NEG),
                       jnp.where(jnp.asarray(in_window_np)[None], bias, NEG)])
    causal = jnp.asarray(np.tril(np.ones((BLOCK, BLOCK), dtype=bool)))
    wm = jnp.where(causal[None], spatial_w[0], 0.0).astype(BF16)
    wm_t = jnp.transpose(wm, (0, 2, 1))
    sb_t = jnp.transpose(spatial_b[0])
    sink_vec = sinks[0]

    x1, h1, a1, b1, f1 = _ffn_fwd(xs, g_ffn1, sh1, sc1, gt1, w1a, w3a, w2a, None, "ffn1_fwd")
    h2 = _norm_mod(x1, g_mix, sh2, sc2, "mixer_norm")
    z = _matmul(h2, win, "nn", F32, 512, _tile(IN_COLS, 1664), D, "mixer_in")
    mix = _mixer_fwd(z, wm, sb_t, g_v[0], g_q, g_k, sink_vec, biasm, "mixer_fwd")
    x2, ymix = _mixer_out(mix, wout, x1, gt2, "mixer_out")
    g3, df3, h3, a3, b3, dgt3, loss_sum = _ffn_fwd(x2, g_ffn2, sh3, sc3, gt3, w1b, w3b, w2b, tgt, "ffn2_fwd_loss")
    loss = lax.psum(loss_sum[0, 0] * (0.5 / D), ("x", "y", "c"))

    tk = _tile(S, 512)

    def ffn_weight_grads(h, da, db, s, df, tag):
        gw1 = _matmul(h, da, "tn", F32, 1024, F // N_CHIPS, tk, f"grad_w1_{tag}", shard_major=True)
        gw3 = _matmul(h, db, "tn", F32, 1024, F // N_CHIPS, tk, f"grad_w3_{tag}", shard_major=True)
        gw2 = _matmul(s, df, "tn", F32, _tile(F, 1408), 1024, tk, f"grad_w2_{tag}")
        return gw1, gw3, jnp.reshape(gw2, (N_CHIPS, F // N_CHIPS, D))

    da3, db3, s3, dh3 = _ffn_bwd(df3, a3, b3, w1b, w3b, w2b, "ffn2_bwd")
    grads_ffn2 = ffn_weight_grads(h3, da3, db3, s3, df3, "ffn2")
    g2, dsh3, dsc3, dgn3, dy, dgt2 = _norm_bwd(dh3, x2, g3, g_ffn2, sc3, (ymix, gt2, 1.0), "ffn2_norm_bwd")

    dmix = _matmul(dy, wout, "nt", BF16, 512, 1024, D, "mixer_out_bwd")
    gwout_full = _matmul(mix, dy, "tn", F32, 1024, 1024, tk, "grad_w_out")
    (dz_main, dz_kv, dwm, dsb_t, dgv, dgq, dgk, dsk, dst) = _mixer_bwd(
        z, dmix, wm, wm_t, sb_t, g_v[0], g_q, g_k, sink_vec, biasm, "mixer_bwd")
    dz = jnp.concatenate([dz_main, dz_kv], axis=1)
    dh2 = _matmul(dz, win, "nt", F32, 512, 1024, _tile(IN_COLS, 1664), "mixer_in_bwd")
    gwin_full = _matmul(h2, dz, "tn", F32, 1024, _tile(IN_COLS, 1664), tk, "grad_w_in")
    drel = _bucket_sum(jnp.reshape(dst, (B_HEADS, -1)), jnp.reshape(onehot, (N_BUCKETS, -1)), "bucket_sum")
    g1, dsh2, dsc2, dgn2, df1, dgt1 = _norm_bwd(dh2, x1, g2, g_mix, sc2, (f1, gt1, 0.5), "mixer_norm_bwd")

    da1, db1, s1, dh1 = _ffn_bwd(df1, a1, b1, w1a, w3a, w2a, "ffn1_bwd")
    grads_ffn1 = ffn_weight_grads(h1, da1, db1, s1, df1, "ffn1")
    grad_x, dsh1, dsc1, dgn1 = _norm_bwd(dh1, xs, g1, g_ffn1, sc1, None, "ffn1_norm_bwd")

    dmod = jnp.concatenate([dsh1, dsc1, dgt1, dsh2, dsc2, dgt2, dsh3, dsc3, dgt3], axis=1)
    small_w = [b_ada, g_ffn1, g_mix, g_ffn2, spatial_w, spatial_b, g_v, g_q, g_k, sinks, rel_bias]
    small_m = [m_b_ada, m_g_ffn1, m_g_mix, m_g_ffn2, m_spatial_w, m_spatial_b, m_g_v, m_g_q, m_g_k, m_sinks, m_rel_bias]
    small_v = [v_b_ada, v_g_ffn1, v_g_mix, v_g_ffn2, v_spatial_w, v_spatial_b, v_g_v, v_g_q, v_g_k, v_sinks, v_rel_bias]
    small_g = [dmod, dgn1, dgn2, dgn3, jnp.where(causal[None], dwm, 0.0), jnp.transpose(dsb_t), dgv, dgq, dgk, dsk, drel]
    shapes = [w.shape for w in small_w]
    gpack = _pack(small_g)
    rows = gpack.shape[0]
    gall = jnp.reshape(_allgather_small(gpack, "gather_small"), (N_DEV, rows, LANES))
    sg, sd, sm, sv = _small_update(gall, _pack(small_w), _pack(small_m), _pack(small_v), "small_update")
    sg, sd, sm, sv = [_unpack(p, shapes) for p in (sg, sd, sm, sv)]

    mod_rows = -(-N_MOD * D // (SUBLANES * LANES)) * SUBLANES
    dmod_all = jnp.reshape(gall[:, :mod_rows], (N_DEV, -1))[:, :N_MOD * D]
    dmod_sh = lax.dynamic_slice(dmod_all, (0, chip * mod_cols), (N_DEV, mod_cols))
    dmod_sh = jnp.pad(dmod_sh, ((0, BF16_ROWS - N_DEV), (0, 0))).astype(BF16)
    g_wada = _matmul(c_act, dmod_sh, "tn", F32, 1024, _tile(mod_cols, 512), BF16_ROWS, "grad_w_ada")
    d_wada, nm_wada, nv_wada = _adamw(w_ada[0], g_wada, m_w_ada[0], v_w_ada[0], "adamw_w_ada")

    gwin4 = jnp.transpose(jnp.reshape(gwin_full, (D, N_CHIPS, -1)), (1, 0, 2))
    gwout4 = jnp.reshape(gwout_full, (N_CHIPS, -1, D))
    gf1 = _reduce_scatter(list(grads_ffn1), "ffn1")
    gmx = _reduce_scatter([gwin4, gwout4], "mixer")
    gf2 = _reduce_scatter(list(grads_ffn2), "ffn2")
    big = {}
    for nm, w, g, m, v in [
        ("w1_ffn1", w1_ffn1, gf1[0], m_w1_ffn1, v_w1_ffn1), ("w3_ffn1", w3_ffn1, gf1[1], m_w3_ffn1, v_w3_ffn1),
        ("w2_ffn1", w2_ffn1, gf1[2], m_w2_ffn1, v_w2_ffn1), ("w_in", w_in, gmx[0], m_w_in, v_w_in),
        ("w_out", w_out, gmx[1], m_w_out, v_w_out), ("w1_ffn2", w1_ffn2, gf2[0], m_w1_ffn2, v_w1_ffn2),
        ("w3_ffn2", w3_ffn2, gf2[1], m_w3_ffn2, v_w3_ffn2), ("w2_ffn2", w2_ffn2, gf2[2], m_w2_ffn2, v_w2_ffn2),
    ]:
        d, nm_, nv_ = _adamw(w[0], g, m[0], v[0], f"adamw_{nm}")
        big[nm] = (g[None], d[None], nm_[None], nv_[None])
    big["w_ada"] = (g_wada[None], d_wada[None], nm_wada[None], nv_wada[None])

    order = ["w_ada", "b_ada", "g_ffn1", "w1_ffn1", "w3_ffn1", "w2_ffn1", "g_mix", "w_in", "spatial_w", "spatial_b",
             "g_v", "g_q", "g_k", "sinks", "rel_bias", "w_out", "g_ffn2", "w1_ffn2", "w3_ffn2", "w2_ffn2"]
    small_names = ["b_ada", "g_ffn1", "g_mix", "g_ffn2", "spatial_w", "spatial_b", "g_v", "g_q", "g_k", "sinks", "rel_bias"]
    for i, nm in enumerate(small_names):
        big[nm] = (sg[i], sd[i], sm[i], sv[i])
    outs = [loss, grad_x[None]]
    for kind in range(4):
        outs += [big[nm][kind] for nm in order]
    return tuple(outs)
```

```python
import functools
import math

import jax
import jax.numpy as jnp
import numpy as np
from jax import lax
from jax.experimental import pallas as pl
from jax.experimental.pallas import tpu as pltpu

F32 = jnp.float32
BF16 = jnp.bfloat16
MESH = pl.DeviceIdType.MESH
ANY = pl.BlockSpec(memory_space=pl.ANY)

EPS = 1e-6
BLOCK = 128
A_HEADS = 8
A_DIM = 128
A_WIDTH = A_HEADS * A_DIM
B_HEADS = 16
KV_HEADS = 2
GROUP = B_HEADS // KV_HEADS
HEAD_DIM = 64
B_WIDTH = B_HEADS * HEAD_DIM
KV_WIDTH = KV_HEADS * HEAD_DIM
Q_OFF = 2 * A_WIDTH
K_OFF = Q_OFF + B_WIDTH
V_OFF = K_OFF + KV_WIDTH
IN_COLS = V_OFF + KV_WIDTH
N_BUCKETS = 32
MAX_DISTANCE = 128
N_MOD = 9
N_CHIPS = 4
N_DEV = 8
NEG = -1e30

ADAM_LR = 0.001
ADAM_B1 = 0.9
ADAM_B2 = 0.999
ADAM_EPS = 1e-08
ADAM_WD = 0.01
ADAM_STEP = 10

LANES = 128
SUBLANES = 8
BF16_ROWS = 16
VMEM_LIMIT = 56 * 1024 * 1024

INV_SQRT2 = 1.0 / math.sqrt(2.0)
INV_SQRT_2PI = 1.0 / math.sqrt(2.0 * math.pi)


def _tile(n, pref, mult=LANES):
    t = (min(pref, n) // mult) * mult
    while t >= mult:
        if n % t == 0:
            return t
        t -= mult
    return n


def _params(sem):
    return pltpu.CompilerParams(dimension_semantics=sem, vmem_limit_bytes=VMEM_LIMIT)


class _Exchange:
    def __init__(self, operands, out_shapes, aliases, n_sems, plan):
        self.operands, self.out_shapes, self.aliases, self.n_sems, self.plan = operands, out_shapes, aliases, n_sems, plan


def _pallas(body, *, name, grid, in_specs, out_specs, out_shape, args, scratch_shapes=(), semantics=None, ex=None):
    if ex is None:
        return pl.pallas_call(body, name=name, grid=grid, in_specs=in_specs, out_specs=out_specs, out_shape=out_shape,
                              scratch_shapes=list(scratch_shapes), compiler_params=_params(semantics))(*args)
    n_in, n_out, n_scr = len(in_specs), len(out_specs), len(scratch_shapes)
    e_in, e_out = len(ex.operands), len(ex.out_shapes)

    def wrapped(*refs):
        ins, refs = refs[:n_in], refs[n_in:]
        ex_ins, refs = refs[:e_in], refs[e_in:]
        outs, refs = refs[:n_out], refs[n_out:]
        ex_outs, refs = refs[:e_out], refs[e_out:]
        scratch, (send_sems, recv_sems) = refs[:n_scr], refs[n_scr:]
        first, last = True, True
        for d, size in enumerate(grid):
            first = jnp.logical_and(first, pl.program_id(d) == 0)
            last = jnp.logical_and(last, pl.program_id(d) == size - 1)

        def start():
            sends, _ = ex.plan(ex_ins, ex_outs, send_sems, recv_sems)
            for cp in sends:
                _remote(*cp).start()

        def finish():
            sends, arrivals = ex.plan(ex_ins, ex_outs, send_sems, recv_sems)
            for cp in arrivals:
                _remote(*cp).wait_recv()
            for cp in sends:
                _remote(*cp).wait_send()

        if grid:
            pl.when(first)(start)
        else:
            start()
        if body is not None:
            body(*ins, *outs, *scratch)
        if grid:
            pl.when(last)(finish)
        else:
            finish()

    kwargs = dict(grid=grid) if grid else {}
    return pl.pallas_call(
        wrapped,
        name=name,
        in_specs=list(in_specs) + [ANY] * e_in,
        out_specs=list(out_specs) + [ANY] * e_out,
        out_shape=list(out_shape) + list(ex.out_shapes),
        input_output_aliases={n_in + i: n_out + o for i, o in ex.aliases.items()},
        scratch_shapes=list(scratch_shapes) + [pltpu.SemaphoreType.DMA((ex.n_sems,)), pltpu.SemaphoreType.DMA((ex.n_sems,))],
        compiler_params=_params(("arbitrary",) * len(grid) if grid else None),
        **kwargs,
    )(*args, *ex.operands)


def _dot(a, b, dims=(((1,), (0,)), ((), ()))):
    return lax.dot_general(a, b, dims, preferred_element_type=F32)


NN = (((1,), (0,)), ((), ()))
NT = (((1,), (1,)), ((), ()))
TN = (((0,), (0,)), ((), ()))


def _sigmoid(x):
    return 1.0 / (1.0 + jnp.exp(-x))


def _gelu_and_grad(x):
    cdf = 0.5 * (1.0 + lax.erf(x * INV_SQRT2))
    pdf = jnp.exp(-0.5 * x * x) * INV_SQRT_2PI
    return x * cdf, cdf + x * pdf


def _gelu(x):
    return x * (0.5 * (1.0 + lax.erf(x * INV_SQRT2)))


def _rms(x):
    r = lax.rsqrt(jnp.mean(x * x, axis=-1, keepdims=True) + EPS)
    return x * r, r


ROW_CHUNK = 64


def _for_rows(tm, fn):
    rc = min(ROW_CHUNK, tm)

    def step(r, carry):
        fn(pl.ds(pl.multiple_of(r * rc, rc), rc))
        return carry

    lax.fori_loop(0, tm // rc, step, 0)


def _rms_bwd(dy, xhat, r):
    return r * (dy - xhat * jnp.mean(dy * xhat, axis=-1, keepdims=True))


def _matmul(a, b, mode, out_dtype, tm, tn, tk, name, shard_major=False, ex=None):
    if mode == "nn":
        (M, K), N = a.shape, b.shape[1]
    elif mode == "nt":
        (M, K), N = a.shape, b.shape[0]
    else:
        (K, M), N = a.shape, b.shape[1]
    tm, tn, tk = min(tm, M), min(tn, N), min(tk, K)
    assert M % tm == 0 and N % tn == 0 and K % tk == 0, (name, M, N, K, tm, tn, tk)
    nk = K // tk
    dims = {"nn": NN, "nt": NT, "tn": TN}[mode]
    a_spec = pl.BlockSpec((tk, tm), lambda i, j, k: (k, i)) if mode == "tn" else pl.BlockSpec((tm, tk), lambda i, j, k: (i, k))
    b_spec = pl.BlockSpec((tn, tk), lambda i, j, k: (j, k)) if mode == "nt" else pl.BlockSpec((tk, tn), lambda i, j, k: (k, j))
    if shard_major:
        assert tn * N_CHIPS == N
        out_shape = jax.ShapeDtypeStruct((N_CHIPS, M, tn), out_dtype)
        o_spec = pl.BlockSpec((None, tm, tn), lambda i, j, k: (j, i, 0))
    else:
        out_shape = jax.ShapeDtypeStruct((M, N), out_dtype)
        o_spec = pl.BlockSpec((tm, tn), lambda i, j, k: (i, j))

    def body(a_ref, b_ref, o_ref, acc_ref):
        k = pl.program_id(2)
        part = _dot(a_ref[...], b_ref[...], dims)
        if nk == 1:
            o_ref[...] = part.astype(o_ref.dtype)
            return

        @pl.when(k == 0)
        def _():
            acc_ref[...] = part

        @pl.when(k > 0)
        def _():
            acc_ref[...] += part

        @pl.when(k == nk - 1)
        def _():
            o_ref[...] = acc_ref[...].astype(o_ref.dtype)

    outs = _pallas(body, name=name, grid=(M // tm, N // tn, nk), in_specs=[a_spec, b_spec], out_specs=[o_spec],
                   out_shape=[out_shape], scratch_shapes=[pltpu.VMEM((tm, tn), F32)],
                   semantics=("parallel", "parallel", "arbitrary"), args=[a, b], ex=ex)
    return outs[0] if ex is None else outs


def _mod_partial(c_all, w_ada, b_sh, name):
    R, D = c_all.shape
    N = w_ada.shape[1]
    tn = _tile(N, 512)

    def body(c_ref, w_ref, b_ref, o_ref, ca_ref):
        cv = c_ref[...]
        ca = (cv * _sigmoid(cv)).astype(BF16)
        ca_ref[...] = ca
        o_ref[...] = _dot(ca, w_ref[...].astype(BF16)) + b_ref[...]

    return pl.pallas_call(
        body,
        name=name,
        grid=(N // tn,),
        in_specs=[
            pl.BlockSpec((R, D), lambda j: (0, 0)),
            pl.BlockSpec((D, tn), lambda j: (0, j)),
            pl.BlockSpec((1, tn), lambda j: (0, j)),
        ],
        out_specs=[pl.BlockSpec((R, tn), lambda j: (0, j)), pl.BlockSpec((R, D), lambda j: (0, 0))],
        out_shape=[jax.ShapeDtypeStruct((R, N), F32), jax.ShapeDtypeStruct((R, D), BF16)],
        compiler_params=_params(("arbitrary",)),
    )(c_all, w_ada, b_sh)


def _ffn_fwd(x, g, sh, sc, gt, w1, w3, w2, tgt, name, ex=None):
    S, D = x.shape
    F = w1.shape[1]
    tm, tf = _tile(S, 512), _tile(F, 512)
    ni, nj = S // tm, F // tf
    with_loss = tgt is not None

    def body(*refs):
        if with_loss:
            (x_ref, g_ref, sh_ref, sc_ref, gt_ref, w1_ref, w3_ref, w2_ref, tgt_ref,
             gout_ref, df_ref, h_ref, a_ref, b_ref, dgt_ref, loss_ref, hs_ref, acc_ref) = refs
        else:
            (x_ref, g_ref, sh_ref, sc_ref, gt_ref, w1_ref, w3_ref, w2_ref,
             xo_ref, h_ref, a_ref, b_ref, f_ref, hs_ref, acc_ref) = refs
        i, j = pl.program_id(0), pl.program_id(1)

        @pl.when(j == 0)
        def _():
            def prologue(rows):
                xhat, _ = _rms(x_ref[rows, :])
                hb = ((xhat * g_ref[...]) * (1.0 + sc_ref[...]) + sh_ref[...]).astype(BF16)
                hs_ref[rows, :] = hb
                h_ref[rows, :] = hb

            _for_rows(tm, prologue)

        hb = hs_ref[...]
        av = _dot(hb, w1_ref[...])
        bv = _dot(hb, w3_ref[...])
        a_ref[...] = av.astype(BF16)
        b_ref[...] = bv.astype(BF16)
        sv = ((av * _sigmoid(av)) * bv).astype(BF16)
        part = _dot(sv, w2_ref[...])

        @pl.when(j == 0)
        def _():
            acc_ref[...] = part

        @pl.when(j > 0)
        def _():
            acc_ref[...] += part

        @pl.when(j == nj - 1)
        def _():
            if with_loss:
                @pl.when(i == 0)
                def _():
                    dgt_ref[...] = jnp.zeros(dgt_ref.shape, F32)
                    loss_ref[...] = jnp.zeros(loss_ref.shape, F32)

            def epilogue(rows):
                fv = acc_ref[rows, :]
                half_gate = 0.5 * gt_ref[...]
                xo = x_ref[rows, :] + half_gate * fv
                if not with_loss:
                    xo_ref[rows, :] = xo
                    f_ref[rows, :] = fv.astype(f_ref.dtype)
                    return
                err = xo - tgt_ref[rows, :]
                gout = err * (1.0 / D)
                gout_ref[rows, :] = gout
                df_ref[rows, :] = (half_gate * gout).astype(BF16)
                dgt_ref[...] += 0.5 * jnp.sum(gout * fv, axis=0, keepdims=True)
                loss_part = jnp.sum(jnp.sum(err * err, axis=1, keepdims=True), axis=0, keepdims=True)
                loss_ref[...] += jnp.broadcast_to(loss_part, loss_ref.shape)

            _for_rows(tm, epilogue)

    row = pl.BlockSpec((tm, D), lambda i, j: (i, 0))
    row_in = pl.BlockSpec((tm, D), lambda i, j: (i, 0), pipeline_mode=pl.Buffered(1))
    vec = pl.BlockSpec((1, D), lambda i, j: (0, 0))
    col = pl.BlockSpec((tm, tf), lambda i, j: (i, j))
    in_specs = [row_in, vec, vec, vec, vec,
                pl.BlockSpec((D, tf), lambda i, j: (0, j)),
                pl.BlockSpec((D, tf), lambda i, j: (0, j)),
                pl.BlockSpec((tf, D), lambda i, j: (j, 0))]
    args = [x, g, sh, sc, gt, w1, w3, w2]
    act = jax.ShapeDtypeStruct((S, F), BF16)
    if with_loss:
        in_specs.append(row_in)
        args.append(tgt)
        out_specs = [row, row, row, col, col, vec, pl.BlockSpec((1, LANES), lambda i, j: (0, 0))]
        out_shape = [jax.ShapeDtypeStruct((S, D), F32), jax.ShapeDtypeStruct((S, D), BF16),
                     jax.ShapeDtypeStruct((S, D), BF16), act, act,
                     jax.ShapeDtypeStruct((1, D), F32), jax.ShapeDtypeStruct((1, LANES), F32)]
    else:
        out_specs = [row, row, col, col, row]
        out_shape = [jax.ShapeDtypeStruct((S, D), F32), jax.ShapeDtypeStruct((S, D), BF16), act, act,
                     jax.ShapeDtypeStruct((S, D), BF16)]
    return _pallas(body, name=name, grid=(ni, nj), in_specs=in_specs, out_specs=out_specs, out_shape=out_shape,
                   scratch_shapes=[pltpu.VMEM((tm, D), BF16), pltpu.VMEM((tm, D), F32)],
                   semantics=("arbitrary", "arbitrary"), args=args, ex=ex)


def _ffn_bwd(df, a, b, w1, w3, w2, name, ex=None):
    S, D = df.shape
    F = a.shape[1]
    tm, tf = _tile(S, 512), _tile(F, 512)
    nj = F // tf

    def body(df_ref, a_ref, b_ref, w1_ref, w3_ref, w2_ref, da_ref, db_ref, s_ref, dh_ref, acc_ref):
        j = pl.program_id(1)
        ds = _dot(df_ref[...], w2_ref[...], NT)
        av = a_ref[...].astype(F32)
        bv = b_ref[...].astype(F32)
        sig = _sigmoid(av)
        sil = av * sig
        da = ((ds * bv) * (sig * (1.0 + av * (1.0 - sig)))).astype(BF16)
        db = (ds * sil).astype(BF16)
        da_ref[...] = da
        db_ref[...] = db
        s_ref[...] = (sil * bv).astype(BF16)
        part = _dot(da, w1_ref[...], NT) + _dot(db, w3_ref[...], NT)

        @pl.when(j == 0)
        def _():
            acc_ref[...] = part

        @pl.when(j > 0)
        def _():
            acc_ref[...] += part

        @pl.when(j == nj - 1)
        def _():
            dh_ref[...] = acc_ref[...]

    row = pl.BlockSpec((tm, D), lambda i, j: (i, 0))
    col = pl.BlockSpec((tm, tf), lambda i, j: (i, j))
    act = jax.ShapeDtypeStruct((S, F), BF16)
    return _pallas(body, name=name, grid=(S // tm, nj),
                   in_specs=[row, col, col,
                             pl.BlockSpec((D, tf), lambda i, j: (0, j)),
                             pl.BlockSpec((D, tf), lambda i, j: (0, j)),
                             pl.BlockSpec((tf, D), lambda i, j: (j, 0))],
                   out_specs=[col, col, col, row],
                   out_shape=[act, act, act, jax.ShapeDtypeStruct((S, D), F32)],
                   scratch_shapes=[pltpu.VMEM((tm, D), F32)],
                   semantics=("parallel", "arbitrary"), args=[df, a, b, w1, w3, w2], ex=ex)


def _norm_mod(x, g, sh, sc, name, ex=None):
    S, D = x.shape
    tm = _tile(S, 512)

    def body(x_ref, g_ref, sh_ref, sc_ref, h_ref):
        def step(rows):
            xhat, _ = _rms(x_ref[rows, :])
            h_ref[rows, :] = ((xhat * g_ref[...]) * (1.0 + sc_ref[...]) + sh_ref[...]).astype(BF16)

        _for_rows(tm, step)

    row = pl.BlockSpec((tm, D), lambda i: (i, 0))
    vec = pl.BlockSpec((1, D), lambda i: (0, 0))
    outs = _pallas(body, name=name, grid=(S // tm,), in_specs=[row, vec, vec, vec], out_specs=[row],
                   out_shape=[jax.ShapeDtypeStruct((S, D), BF16)], semantics=("parallel",), args=[x, g, sh, sc], ex=ex)
    return outs[0] if ex is None else outs


def _norm_bwd(dh, x, gres, g, sc, prev, name, ex=None):
    S, D = x.shape
    tm = _tile(S, 256)
    has_prev = prev is not None
    coef = prev[2] if has_prev else None

    def body(*refs):
        if has_prev:
            (dh_ref, x_ref, gr_ref, g_ref, sc_ref, f_ref, gt_ref,
             go_ref, dsh_ref, dsc_ref, dg_ref, dp_ref, dgt_ref) = refs
        else:
            dh_ref, x_ref, gr_ref, g_ref, sc_ref, go_ref, dsh_ref, dsc_ref, dg_ref = refs
        sum_refs = [dsh_ref, dsc_ref, dg_ref] + ([dgt_ref] if has_prev else [])

        @pl.when(pl.program_id(0) == 0)
        def _():
            for ref in sum_refs:
                ref[...] = jnp.zeros(ref.shape, F32)

        def step(rows):
            dh = dh_ref[rows, :]
            xhat, r = _rms(x_ref[rows, :])
            gain = g_ref[...]
            scale1 = 1.0 + sc_ref[...]
            gout = gr_ref[rows, :] + _rms_bwd(dh * scale1 * gain, xhat, r)
            go_ref[rows, :] = gout
            sums = [dh, dh * (xhat * gain), dh * scale1 * xhat]
            if has_prev:
                dp_ref[rows, :] = ((coef * gt_ref[...]) * gout).astype(BF16)
                sums.append(coef * (gout * f_ref[rows, :].astype(F32)))
            for ref, v in zip(sum_refs, sums):
                ref[...] += jnp.sum(v, axis=0, keepdims=True)

        _for_rows(tm, step)

    row = pl.BlockSpec((tm, D), lambda i: (i, 0))
    vec = pl.BlockSpec((1, D), lambda i: (0, 0))
    vshape = jax.ShapeDtypeStruct((1, D), F32)
    in_specs = [row, row, row, vec, vec]
    args = [dh, x, gres, g, sc]
    out_specs = [row, vec, vec, vec]
    out_shape = [jax.ShapeDtypeStruct((S, D), F32), vshape, vshape, vshape]
    if has_prev:
        in_specs += [row, vec]
        args += [prev[0], prev[1]]
        out_specs += [row, vec]
        out_shape += [jax.ShapeDtypeStruct((S, D), BF16), vshape]
    return _pallas(body, name=name, grid=(S // tm,), in_specs=in_specs, out_specs=out_specs, out_shape=out_shape,
                   semantics=("arbitrary",), args=args, ex=ex)


def _attn_probs(qn_b, kb, bias, sink):
    s = _dot(qn_b, kb, NT) * (HEAD_DIM ** -0.5) + bias
    m = jnp.maximum(jnp.max(s, axis=-1, keepdims=True), sink)
    p = jnp.exp(s - m)
    e_sink = jnp.exp(sink - m)
    inv = 1.0 / (jnp.sum(p, axis=-1, keepdims=True) + e_sink)
    return p * inv, e_sink * inv


def _mixer_fwd(z, wm, sb_t, gv, gq, gk, sinks, biasm, name):
    S = z.shape[0]
    nb = S // BLOCK

    def body(z_ref, zp_ref, wm_ref, sbt_ref, gv_ref, gq_ref, gk_ref, sk_ref, bias_ref, mix_ref):
        for h in range(A_HEADS):
            lo = h * A_DIM
            u = _gelu(z_ref[:, lo:lo + A_DIM])
            v = _gelu(z_ref[:, A_WIDTH + lo:A_WIDTH + lo + A_DIM])
            vhat, _ = _rms(v)
            vn = (vhat * gv_ref[h:h + 1, :]).astype(BF16)
            mixed = _dot(wm_ref[h], vn) + sbt_ref[:, h:h + 1]
            mix_ref[:, lo:lo + A_DIM] = (u * mixed).astype(BF16)
        gq_v, gk_v = gq_ref[...], gk_ref[...]
        for kh in range(KV_HEADS):
            ko = kh * HEAD_DIM
            kp, _ = _rms(zp_ref[:, ko:ko + HEAD_DIM])
            kc, _ = _rms(z_ref[:, K_OFF + ko:K_OFF + ko + HEAD_DIM])
            kb = jnp.concatenate([kp * gk_v, kc * gk_v], axis=0).astype(BF16)
            vb = jnp.concatenate([zp_ref[:, KV_WIDTH + ko:KV_WIDTH + ko + HEAD_DIM],
                                  z_ref[:, V_OFF + ko:V_OFF + ko + HEAD_DIM]], axis=0).astype(BF16)
            for g in range(GROUP):
                h = kh * GROUP + g
                qo = Q_OFF + h * HEAD_DIM
                qhat, _ = _rms(z_ref[:, qo:qo + HEAD_DIM])
                w, _ = _attn_probs((qhat * gq_v).astype(BF16), kb, bias_ref[h], sk_ref[h])
                o = _dot(w.astype(BF16), vb)
                mix_ref[:, A_WIDTH + h * HEAD_DIM:A_WIDTH + (h + 1) * HEAD_DIM] = o.astype(BF16)

    full = lambda shape: pl.BlockSpec(shape, lambda n: (0,) * len(shape))
    return pl.pallas_call(
        body,
        name=name,
        grid=(nb,),
        in_specs=[
            pl.BlockSpec((BLOCK, IN_COLS), lambda n: (n, 0)),
            pl.BlockSpec((BLOCK, 2 * KV_WIDTH), lambda n: (jnp.maximum(n - 1, 0), K_OFF // (2 * KV_WIDTH))),
            full((A_HEADS, BLOCK, BLOCK)), full((BLOCK, A_HEADS)), full((A_HEADS, A_DIM)),
            full((1, HEAD_DIM)), full((1, HEAD_DIM)),
            pl.BlockSpec(memory_space=pltpu.SMEM),
            pl.BlockSpec((None, B_HEADS, BLOCK, 2 * BLOCK), lambda n: (jnp.minimum(n, 1), 0, 0, 0)),
        ],
        out_specs=pl.BlockSpec((BLOCK, A_WIDTH + B_WIDTH), lambda n: (n, 0)),
        out_shape=jax.ShapeDtypeStruct((S, A_WIDTH + B_WIDTH), BF16),
        compiler_params=_params(("parallel",)),
    )(z, z, wm, sb_t, gv, gq, gk, sinks, biasm)


def _mixer_bwd(z, dmix, wm, wm_t, sb_t, gv, gq, gk, sinks, biasm, name, ex=None):
    S = z.shape[0]
    nb = S // BLOCK

    def body(z_ref, zp_ref, dmix_ref, wm_ref, wmt_ref, sbt_ref, gv_ref, gq_ref, gk_ref, sk_ref, bias_ref,
             dz_ref, dzkv_ref, dwm_ref, dsb_ref, dgv_ref, dgq_ref, dgk_ref, dsk_ref, dst_ref,
             carry_ref, tot_ref, sbacc_ref):
        n = pl.program_id(0)

        @pl.when(n == 0)
        def _():
            for ref in (dwm_ref, dgv_ref, dgq_ref, dgk_ref, dsk_ref, dst_ref, carry_ref, sbacc_ref):
                ref[...] = jnp.zeros(ref.shape, ref.dtype)

        @pl.when(n < nb)
        def _():
            for h in range(A_HEADS):
                lo = h * A_DIM
                u, du_dz = _gelu_and_grad(z_ref[:, lo:lo + A_DIM])
                v, dv_dz = _gelu_and_grad(z_ref[:, A_WIDTH + lo:A_WIDTH + lo + A_DIM])
                vhat, rv = _rms(v)
                gvh = gv_ref[h:h + 1, :]
                vn = (vhat * gvh).astype(BF16)
                mixed = _dot(wm_ref[h], vn) + sbt_ref[:, h:h + 1]
                dya = dmix_ref[:, lo:lo + A_DIM].astype(F32)
                dmx = dya * u
                sbacc_ref[h] += dmx
                dmx_b = dmx.astype(BF16)
                dwm_ref[h] += _dot(dmx_b, vn, NT)
                dvn = _dot(wmt_ref[h], dmx_b)
                dgv_ref[h:h + 1, :] += jnp.sum(dvn * vhat, axis=0, keepdims=True)
                dz_ref[:, lo:lo + A_DIM] = ((dya * mixed) * du_dz).astype(BF16)
                dz_ref[:, A_WIDTH + lo:A_WIDTH + lo + A_DIM] = (_rms_bwd(dvn * gvh, vhat, rv) * dv_dz).astype(BF16)
            gq_v, gk_v = gq_ref[...], gk_ref[...]
            dgq = jnp.zeros((1, HEAD_DIM), F32)
            for kh in range(KV_HEADS):
                ko = kh * HEAD_DIM
                kp, _ = _rms(zp_ref[:, ko:ko + HEAD_DIM])
                kc, _ = _rms(z_ref[:, K_OFF + ko:K_OFF + ko + HEAD_DIM])
                kb = jnp.concatenate([kp * gk_v, kc * gk_v], axis=0).astype(BF16)
                vb = jnp.concatenate([zp_ref[:, KV_WIDTH + ko:KV_WIDTH + ko + HEAD_DIM],
                                      z_ref[:, V_OFF + ko:V_OFF + ko + HEAD_DIM]], axis=0).astype(BF16)
                dkb = jnp.zeros((2 * BLOCK, HEAD_DIM), F32)
                dvb = jnp.zeros((2 * BLOCK, HEAD_DIM), F32)
                for g in range(GROUP):
                    h = kh * GROUP + g
                    qo = Q_OFF + h * HEAD_DIM
                    qhat, rq = _rms(z_ref[:, qo:qo + HEAD_DIM])
                    qn_b = (qhat * gq_v).astype(BF16)
                    w, w_sink = _attn_probs(qn_b, kb, bias_ref[h], sk_ref[h])
                    do = dmix_ref[:, A_WIDTH + h * HEAD_DIM:A_WIDTH + (h + 1) * HEAD_DIM]
                    dp = _dot(do, vb, NT)
                    delta = jnp.sum(w * dp, axis=-1, keepdims=True)
                    ds = w * (dp - delta)
                    dsk_ref[0:1, h:h + 1] += -jnp.sum(w_sink * delta, axis=0, keepdims=True)
                    dst_ref[h] += ds
                    ds_b = (ds * (HEAD_DIM ** -0.5)).astype(BF16)
                    dqn = _dot(ds_b, kb)
                    dkb += _dot(ds_b, qn_b, TN)
                    dvb += _dot(w.astype(BF16), do, TN)
                    dgq += jnp.sum(dqn * qhat, axis=0, keepdims=True)
                    dz_ref[:, qo:qo + HEAD_DIM] = _rms_bwd(dqn * gq_v, qhat, rq).astype(BF16)
                tot_ref[0, :, ko:ko + HEAD_DIM] = carry_ref[0, :, ko:ko + HEAD_DIM] + dkb[:BLOCK]
                tot_ref[1, :, ko:ko + HEAD_DIM] = carry_ref[1, :, ko:ko + HEAD_DIM] + dvb[:BLOCK]
                carry_ref[0, :, ko:ko + HEAD_DIM] = dkb[BLOCK:]
                carry_ref[1, :, ko:ko + HEAD_DIM] = dvb[BLOCK:]
            dgq_ref[...] += dgq

        @pl.when(n == nb)
        def _():
            tot_ref[...] = carry_ref[...]
            for h in range(A_HEADS):
                dsb_ref[:, h:h + 1] = jnp.sum(sbacc_ref[h], axis=1, keepdims=True)

        gk_v = gk_ref[...]
        dgk = jnp.zeros((1, HEAD_DIM), F32)
        for kh in range(KV_HEADS):
            ko = kh * HEAD_DIM
            khat, rk = _rms(zp_ref[:, ko:ko + HEAD_DIM])
            dkn = tot_ref[0, :, ko:ko + HEAD_DIM]
            dgk += jnp.sum(dkn * khat, axis=0, keepdims=True)
            dzkv_ref[:, ko:ko + HEAD_DIM] = _rms_bwd(dkn * gk_v, khat, rk).astype(BF16)
        dgk_ref[...] += dgk
        dzkv_ref[:, KV_WIDTH:] = tot_ref[1].astype(BF16)

    last = nb - 1
    full = lambda shape: pl.BlockSpec(shape, lambda n: (0,) * len(shape))
    return _pallas(
        body,
        name=name,
        grid=(nb + 1,),
        ex=ex,
        in_specs=[
            pl.BlockSpec((BLOCK, IN_COLS), lambda n: (jnp.minimum(n, last), 0)),
            pl.BlockSpec((BLOCK, 2 * KV_WIDTH), lambda n: (jnp.maximum(n - 1, 0), K_OFF // (2 * KV_WIDTH))),
            pl.BlockSpec((BLOCK, A_WIDTH + B_WIDTH), lambda n: (jnp.minimum(n, last), 0)),
            full((A_HEADS, BLOCK, BLOCK)), full((A_HEADS, BLOCK, BLOCK)), full((BLOCK, A_HEADS)),
            full((A_HEADS, A_DIM)), full((1, HEAD_DIM)), full((1, HEAD_DIM)),
            pl.BlockSpec(memory_space=pltpu.SMEM),
            pl.BlockSpec((None, B_HEADS, BLOCK, 2 * BLOCK), lambda n: (jnp.minimum(n, 1), 0, 0, 0)),
        ],
        out_specs=[
            pl.BlockSpec((BLOCK, K_OFF), lambda n: (jnp.minimum(n, last), 0)),
            pl.BlockSpec((BLOCK, 2 * KV_WIDTH), lambda n: (jnp.maximum(n - 1, 0), 0)),
            full((A_HEADS, BLOCK, BLOCK)), full((BLOCK, A_HEADS)), full((A_HEADS, A_DIM)),
            full((1, HEAD_DIM)), full((1, HEAD_DIM)), full((1, B_HEADS)),
            full((B_HEADS, BLOCK, 2 * BLOCK)),
        ],
        out_shape=[
            jax.ShapeDtypeStruct((S, K_OFF), BF16),
            jax.ShapeDtypeStruct((S, 2 * KV_WIDTH), BF16),
            jax.ShapeDtypeStruct((A_HEADS, BLOCK, BLOCK), F32),
            jax.ShapeDtypeStruct((BLOCK, A_HEADS), F32),
            jax.ShapeDtypeStruct((A_HEADS, A_DIM), F32),
            jax.ShapeDtypeStruct((1, HEAD_DIM), F32),
            jax.ShapeDtypeStruct((1, HEAD_DIM), F32),
            jax.ShapeDtypeStruct((1, B_HEADS), F32),
            jax.ShapeDtypeStruct((B_HEADS, BLOCK, 2 * BLOCK), F32),
        ],
        scratch_shapes=[
            pltpu.VMEM((2, BLOCK, KV_WIDTH), F32),
            pltpu.VMEM((2, BLOCK, KV_WIDTH), F32),
            pltpu.VMEM((A_HEADS, BLOCK, A_DIM), F32),
        ],
        semantics=("arbitrary",),
        args=[z, z, dmix, wm, wm_t, sb_t, gv, gq, gk, sinks, biasm],
    )


def _mixer_out(mix, w_out, x, gt, name):
    S, D = x.shape
    K = mix.shape[1]
    tm, tn = _tile(S, 512), _tile(D, 1024)

    def body(m_ref, w_ref, x_ref, gt_ref, xo_ref, y_ref):
        y = _dot(m_ref[...], w_ref[...])
        y_ref[...] = y
        xo_ref[...] = x_ref[...] + gt_ref[...] * y

    blk = pl.BlockSpec((tm, tn), lambda j, i: (i, j))
    return pl.pallas_call(
        body,
        name=name,
        grid=(D // tn, S // tm),
        in_specs=[pl.BlockSpec((tm, K), lambda j, i: (i, 0)), pl.BlockSpec((K, tn), lambda j, i: (0, j)),
                  blk, pl.BlockSpec((1, tn), lambda j, i: (0, j))],
        out_specs=[blk, blk],
        out_shape=[jax.ShapeDtypeStruct((S, D), F32), jax.ShapeDtypeStruct((S, D), F32)],
        compiler_params=_params(("parallel", "parallel")),
    )(mix, w_out, x, gt)


def _bucket_sum(dst, onehot, name):
    def body(d_ref, o_ref, out_ref):
        out_ref[...] = lax.dot_general(o_ref[...], d_ref[...], NT, precision=lax.Precision.HIGHEST,
                                       preferred_element_type=F32)

    return pl.pallas_call(
        body,
        name=name,
        out_shape=jax.ShapeDtypeStruct((N_BUCKETS, B_HEADS), F32),
    )(dst, onehot)


def _adamw_math(w, g, m, v):
    m = ADAM_B1 * m + (1.0 - ADAM_B1) * g
    v = ADAM_B2 * v + (1.0 - ADAM_B2) * (g * g)
    m_hat = m / (1.0 - ADAM_B1 ** ADAM_STEP)
    v_hat = v / (1.0 - ADAM_B2 ** ADAM_STEP)
    delta = -ADAM_LR * (m_hat / (jnp.sqrt(v_hat) + ADAM_EPS) + ADAM_WD * w)
    return delta, m, v


def _adamw(w, g, m, v, name, ex=None):
    R, C = w.shape
    tr = _tile(R, max(SUBLANES, (1 << 19) // C), SUBLANES)

    def body(w_ref, g_ref, m_ref, v_ref, d_ref, mo_ref, vo_ref):
        d, mn, vn = _adamw_math(w_ref[...], g_ref[...], m_ref[...], v_ref[...])
        d_ref[...] = d
        mo_ref[...] = mn
        vo_ref[...] = vn

    blk = pl.BlockSpec((tr, C), lambda i: (i, 0))
    shape = jax.ShapeDtypeStruct((R, C), F32)
    return _pallas(body, name=name, grid=(R // tr,), in_specs=[blk] * 4, out_specs=[blk] * 3, out_shape=[shape] * 3,
                   semantics=("parallel",), args=[w, g, m, v], ex=ex)


def _small_update(parts, w, m, v, name):
    R = w.shape[0]

    def body(p_ref, w_ref, m_ref, v_ref, g_ref, d_ref, mo_ref, vo_ref):
        g = p_ref[0]
        for dev in range(1, N_DEV):
            g = g + p_ref[dev]
        g_ref[...] = g
        d, mn, vn = _adamw_math(w_ref[...], g, m_ref[...], v_ref[...])
        d_ref[...] = d
        mo_ref[...] = mn
        vo_ref[...] = vn

    shape = jax.ShapeDtypeStruct((R, LANES), F32)
    return pl.pallas_call(
        body,
        name=name,
        out_shape=[shape] * 4,
        compiler_params=pltpu.CompilerParams(vmem_limit_bytes=VMEM_LIMIT),
    )(parts, w, m, v)


def _place():
    x, y, c = lax.axis_index("x"), lax.axis_index("y"), lax.axis_index("c")
    chips = [(1 - x, y), (x, 1 - y), (1 - x, 1 - y)]
    return x, y, c, chips


def _remote(src, dst, send_sem, recv_sem, to):
    return pltpu.make_async_remote_copy(src_ref=src, dst_ref=dst, send_sem=send_sem, recv_sem=recv_sem,
                                        device_id=to, device_id_type=MESH)


def _allgather_small(block, name):
    m_per, n = block.shape

    def body(x_ref, out_ref, send_sems, recv_sems, local_sem):
        x, y, c, chips = _place()
        me, sibling = (x, y, c), (x, y, 1 - c)

        def rows(px, py, pc):
            return out_ref.at[pl.ds((4 * px + 2 * py + pc) * m_per, m_per), :]

        def copy(k, blk, to, src=None):
            return _remote(rows(*blk) if src is None else src, rows(*blk), send_sems.at[k], recv_sems.at[k], to)

        mine = pltpu.make_async_copy(x_ref, rows(*me), local_sem)
        mine.start()
        first = [copy(0, me, sibling, src=x_ref)]
        first += [copy(1 + j, me, (*chip, c), src=x_ref) for j, chip in enumerate(chips)]
        for cp in first:
            cp.start()
        passed = [copy(4 + j, (*chip, c), sibling) for j, chip in enumerate(chips)]
        for j, chip in enumerate(chips):
            copy(1 + j, (*chip, c), me).wait_recv()
            passed[j].start()
        copy(0, sibling, me).wait_recv()
        for j, chip in enumerate(chips):
            copy(4 + j, (*chip, 1 - c), me).wait_recv()
        for cp in first + passed:
            cp.wait_send()
        mine.wait()

    return pl.pallas_call(
        body,
        name=name,
        out_shape=jax.ShapeDtypeStruct((N_DEV * m_per, n), block.dtype),
        in_specs=[pl.BlockSpec(memory_space=pltpu.VMEM)],
        out_specs=pl.BlockSpec(memory_space=pltpu.VMEM),
        scratch_shapes=[pltpu.SemaphoreType.DMA((7,)), pltpu.SemaphoreType.DMA((7,)), pltpu.SemaphoreType.DMA],
        compiler_params=pltpu.CompilerParams(vmem_limit_bytes=VMEM_LIMIT),
    )(block)


def _half(ref, c, rows):
    start = pl.multiple_of(c * rows, BF16_ROWS)
    if len(ref.shape) == 2:
        return ref.at[pl.ds(start, rows), :]
    return ref.at[:, pl.ds(start, rows), :]


def _same(arrays):
    return [jax.ShapeDtypeStruct(a.shape, a.dtype) for a in arrays], {t: t for t in range(len(arrays))}


def _ex_gather_ici(bufs):
    def plan(ins, outs, send_sems, recv_sems):
        x, y, c, chips = _place()
        sends, arrivals = [], []
        for t, buf in enumerate(bufs):
            rows = buf.shape[1] // 2
            mine = _half(outs[t].at[2 * x + y], c, rows)
            for k, (px, py) in enumerate(chips):
                sems = (send_sems.at[3 * t + k], recv_sems.at[3 * t + k], (px, py, c))
                landed = _half(outs[t].at[2 * px + py], c, rows)
                sends.append((mine, mine, *sems))
                arrivals.append((landed, landed, *sems))
        return sends, arrivals

    shapes, aliases = _same(bufs)
    return _Exchange(bufs, shapes, aliases, 3 * len(bufs), plan)


def _ex_gather_d2d(bufs):
    def plan(ins, outs, send_sems, recv_sems):
        x, y, c, chips = _place()
        sends, arrivals = [], []
        for t, buf in enumerate(bufs):
            rows = buf.shape[1] // 2
            for k, (px, py) in enumerate(chips):
                sems = (send_sems.at[3 * t + k], recv_sems.at[3 * t + k], (x, y, 1 - c))
                landed = _half(outs[t].at[2 * px + py], c, rows)
                other = _half(outs[t].at[2 * px + py], 1 - c, rows)
                sends.append((landed, landed, *sems))
                arrivals.append((other, other, *sems))
        return sends, arrivals

    shapes, aliases = _same(bufs)
    return _Exchange(bufs, shapes, aliases, 3 * len(bufs), plan)


def _ex_swap_halves(grads):
    def plan(ins, outs, send_sems, recv_sems):
        x, y, c, _ = _place()
        sends = [(_half(ins[t], 1 - c, g.shape[1] // 2), outs[t], send_sems.at[t], recv_sems.at[t], (x, y, 1 - c))
                 for t, g in enumerate(grads)]
        return sends, sends

    shapes = [jax.ShapeDtypeStruct((g.shape[0], g.shape[1] // 2, g.shape[2]), g.dtype) for g in grads]
    return _Exchange(grads, shapes, {}, len(grads), plan)


def _ex_scatter(sums):
    def plan(ins, outs, send_sems, recv_sems):
        x, y, c, chips = _place()
        sends = [(ins[t].at[2 * px + py], outs[t].at[k], send_sems.at[3 * t + k], recv_sems.at[3 * t + k], (px, py, c))
                 for t in range(len(sums)) for k, (px, py) in enumerate(chips)]
        return sends, sends

    shapes = [jax.ShapeDtypeStruct((N_CHIPS - 1,) + s.shape[1:], s.dtype) for s in sums]
    return _Exchange(sums, shapes, {}, 3 * len(sums), plan)


def _ex_join_halves(fulls):
    def plan(ins, outs, send_sems, recv_sems):
        x, y, c, _ = _place()
        sends, arrivals = [], []
        for t, full in enumerate(fulls):
            rows = full.shape[0] // 2
            sems = (send_sems.at[t], recv_sems.at[t], (x, y, 1 - c))
            mine, other = _half(outs[t], c, rows), _half(outs[t], 1 - c, rows)
            sends.append((mine, mine, *sems))
            arrivals.append((other, other, *sems))
        return sends, arrivals

    shapes, aliases = _same(fulls)
    return _Exchange(fulls, shapes, aliases, len(fulls), plan)


def _exchange_alone(ex, name):
    return _pallas(None, name=name, grid=(), in_specs=[], out_specs=[], out_shape=[], args=[], ex=ex)


def _cast_to_slot(w, chip_arr, name):
    A, B = w.shape
    ta = _tile(A, max(BF16_ROWS, (1 << 19) // B), BF16_ROWS)

    def body(j_ref, w_ref, o_ref):
        o_ref[...] = w_ref[...].astype(BF16)

    return pl.pallas_call(
        body,
        name=name,
        grid_spec=pltpu.PrefetchScalarGridSpec(
            num_scalar_prefetch=1,
            grid=(A // ta,),
            in_specs=[pl.BlockSpec((ta, B), lambda i, j_ref: (i, 0))],
            out_specs=pl.BlockSpec((None, ta, B), lambda i, j_ref: (j_ref[0], i, 0)),
        ),
        out_shape=jax.ShapeDtypeStruct((N_CHIPS, A, B), BF16),
        compiler_params=_params(("parallel",)),
    )(chip_arr, w)


def _chip_sum(grad, recv, c_arr, name):
    _, A, B = grad.shape
    hA = A // 2
    ta = _tile(hA, max(BF16_ROWS, (1 << 19) // B), BF16_ROWS)
    nh = hA // ta

    def body(c_ref, g_ref, r_ref, o_ref):
        o_ref[...] = (g_ref[...] + r_ref[...]).astype(BF16)

    return pl.pallas_call(
        body,
        name=name,
        grid_spec=pltpu.PrefetchScalarGridSpec(
            num_scalar_prefetch=1,
            grid=(N_CHIPS, nh),
            in_specs=[pl.BlockSpec((None, ta, B), lambda s, i, c_ref: (s, c_ref[0] * nh + i, 0)),
                      pl.BlockSpec((None, ta, B), lambda s, i, c_ref: (s, i, 0))],
            out_specs=pl.BlockSpec((None, ta, B), lambda s, i, c_ref: (s, i, 0)),
        ),
        out_shape=jax.ShapeDtypeStruct((N_CHIPS, hA, B), BF16),
        compiler_params=_params(("parallel", "parallel")),
    )(c_arr, grad, recv)


def _owner_sum(grad, recv, landed, jc_arr, name):
    _, A, B = grad.shape
    hA = A // 2
    ta = _tile(hA, max(BF16_ROWS, (1 << 19) // B), BF16_ROWS)
    nh = hA // ta

    def body(jc_ref, g_ref, r_ref, l0_ref, l1_ref, l2_ref, o_ref):
        total = g_ref[...] + r_ref[...]
        for ref in (l0_ref, l1_ref, l2_ref):
            total = total + ref[...].astype(F32)
        o_ref[...] = total

    def landed_spec(k):
        return pl.BlockSpec((None, ta, B), lambda i, jc_ref: (k, i, 0))

    return pl.pallas_call(
        body,
        name=name,
        grid_spec=pltpu.PrefetchScalarGridSpec(
            num_scalar_prefetch=1,
            grid=(nh,),
            in_specs=[pl.BlockSpec((None, ta, B), lambda i, jc_ref: (jc_ref[0], jc_ref[1] * nh + i, 0)),
                      pl.BlockSpec((None, ta, B), lambda i, jc_ref: (jc_ref[0], i, 0)),
                      landed_spec(0), landed_spec(1), landed_spec(2)],
            out_specs=pl.BlockSpec((ta, B), lambda i, jc_ref: (jc_ref[1] * nh + i, 0)),
        ),
        out_shape=jax.ShapeDtypeStruct((A, B), F32),
        compiler_params=_params(("parallel",)),
    )(jc_arr, grad, recv, landed, landed, landed)


def _pack(parts):
    rows = []
    for p in parts:
        flat = jnp.reshape(p.astype(F32), (-1,))
        tile = SUBLANES * LANES
        padded = -(-flat.shape[0] // tile) * tile
        rows.append(jnp.reshape(jnp.pad(flat, (0, padded - flat.shape[0])), (-1, LANES)))
    return jnp.concatenate(rows, axis=0)


def _unpack(pack, shapes):
    out, row = [], 0
    for shape in shapes:
        size = int(np.prod(shape))
        nrows = -(-size // (SUBLANES * LANES)) * SUBLANES
        out.append(jnp.reshape(jnp.reshape(pack[row:row + nrows], (-1,))[:size], shape))
        row += nrows
    return out


def _bias_tables():
    qi = np.arange(BLOCK)[:, None]
    kj = np.arange(2 * BLOCK)[None, :]
    dist = qi + BLOCK - kj
    in_window = (dist >= 0) & (dist < BLOCK)
    n = np.clip(dist, 0, None)
    max_exact = N_BUCKETS // 2
    nf = np.maximum(n, 1).astype(np.float32)
    large = max_exact + (np.log(nf / max_exact) / math.log(MAX_DISTANCE / max_exact)
                         * (N_BUCKETS - max_exact)).astype(np.int32)
    large = np.minimum(large, N_BUCKETS - 1)
    bucket = np.where(n < max_exact, n, large)
    onehot = (bucket[None] == np.arange(N_BUCKETS)[:, None, None]) & in_window[None]
    first = in_window & (kj >= BLOCK)
    return onehot.astype(np.float32), in_window, first


def kernel(x, c, w_ada, b_ada, g_ffn1, w1_ffn1, w3_ffn1, w2_ffn1, g_mix, w_in, spatial_w, spatial_b, g_v, g_q, g_k, sinks, rel_bias, w_out, g_ffn2, w1_ffn2, w3_ffn2, w2_ffn2, loss_target, m_w_ada, m_b_ada, m_g_ffn1, m_w1_ffn1, m_w3_ffn1, m_w2_ffn1, m_g_mix, m_w_in, m_spatial_w, m_spatial_b, m_g_v, m_g_q, m_g_k, m_sinks, m_rel_bias, m_w_out, m_g_ffn2, m_w1_ffn2, m_w3_ffn2, m_w2_ffn2, v_w_ada, v_b_ada, v_g_ffn1, v_w1_ffn1, v_w3_ffn1, v_w2_ffn1, v_g_mix, v_w_in, v_spatial_w, v_spatial_b, v_g_v, v_g_q, v_g_k, v_sinks, v_rel_bias, v_w_out, v_g_ffn2, v_w1_ffn2, v_w3_ffn2, v_w2_ffn2):
    ax, ay, ac = lax.axis_index("x"), lax.axis_index("y"), lax.axis_index("c")
    chip = 2 * ax + ay
    dev = 2 * chip + ac
    xs = x[0]
    tgt = loss_target[0]
    S, D = xs.shape
    F = N_CHIPS * w1_ffn1.shape[2]
    mod_cols = w_ada.shape[2]

    c_all = _allgather_small(jnp.pad(c, ((0, SUBLANES - 1), (0, 0))), "gather_c")
    c_all = jnp.pad(c_all[::SUBLANES], ((0, BF16_ROWS - N_DEV), (0, 0)))
    b_sh = lax.dynamic_slice(b_ada, (0, chip * mod_cols), (1, mod_cols))
    mod_part, c_act = _mod_partial(c_all, w_ada[0], b_sh, "mod_partial")
    mod_all = _allgather_small(mod_part[:N_DEV], "gather_mod")
    mod_all = jnp.reshape(mod_all, (N_CHIPS, 2, N_DEV, mod_cols))[:, 0]
    mod = jnp.reshape(lax.dynamic_index_in_dim(mod_all, dev, axis=1, keepdims=False), (1, N_MOD * D))
    sh1, sc1, gt1, sh2, sc2, gt2, sh3, sc3, gt3 = [mod[:, i * D:(i + 1) * D] for i in range(N_MOD)]

    def cols_to_natural(w4):
        return jnp.reshape(jnp.transpose(w4, (1, 0, 2)), (w4.shape[1], -1))

    chip_arr = jnp.reshape(chip, (1,)).astype(jnp.int32)
    c_arr = jnp.reshape(ac, (1,)).astype(jnp.int32)
    jc_arr = jnp.stack([chip, ac]).astype(jnp.int32)
    cast = lambda w, nm: _cast_to_slot(w[0], chip_arr, f"cast_{nm}")
    ffn1_bufs = [cast(w1_ffn1, "w1_ffn1"), cast(w3_ffn1, "w3_ffn1"), cast(w2_ffn1, "w2_ffn1")]
    mixer_bufs = [cast(w_in, "w_in"), cast(w_out, "w_out")]
    ffn2_bufs = [cast(w1_ffn2, "w1_ffn2"), cast(w3_ffn2, "w3_ffn2"), cast(w2_ffn2, "w2_ffn2")]
    ffn1_bufs = _exchange_alone(_ex_gather_ici(ffn1_bufs), "gather_ffn1_ici")
    ffn1_bufs = _exchange_alone(_ex_gather_d2d(ffn1_bufs), "gather_ffn1_d2d")
    w1a, w3a, w2a = cols_to_natural(ffn1_bufs[0]), cols_to_natural(ffn1_bufs[1]), jnp.reshape(ffn1_bufs[2], (F, D))

    onehot_np, in_window_np, first_np = _bias_tables()
    onehot = jnp.asarray(onehot_np)
    bias = jnp.einsum("bij,bh->hij", onehot, rel_bias, precision=lax.Precision.HIGHEST)
    biasm = jnp.stack([jnp.where(jnp.asarray(first_np)[None], bias, NEG),
                       jnp.where(jnp.asarray(in_window_np)[None], bias, NEG)])
    causal = jnp.asarray(np.tril(np.ones((BLOCK, BLOCK), dtype=bool)))
    wm = jnp.where(causal[None], spatial_w[0], 0.0).astype(BF16)
    wm_t = jnp.transpose(wm, (0, 2, 1))
    sb_t = jnp.transpose(spatial_b[0])
    sink_vec = sinks[0]

    res = _ffn_fwd(xs, g_ffn1, sh1, sc1, gt1, w1a, w3a, w2a, None, "ffn1_fwd", ex=_ex_gather_ici(mixer_bufs + ffn2_bufs))
    (x1, h1, a1, b1, f1), mixer_bufs, ffn2_bufs = res[:5], res[5:7], res[7:]
    h2, *mixer_bufs = _norm_mod(x1, g_mix, sh2, sc2, "mixer_norm", ex=_ex_gather_d2d(mixer_bufs))
    win, wout = cols_to_natural(mixer_bufs[0]), jnp.reshape(mixer_bufs[1], (-1, D))
    z, *ffn2_bufs = _matmul(h2, win, "nn", F32, 512, _tile(IN_COLS, 1664), D, "mixer_in", ex=_ex_gather_d2d(ffn2_bufs))
    w1b, w3b, w2b = cols_to_natural(ffn2_bufs[0]), cols_to_natural(ffn2_bufs[1]), jnp.reshape(ffn2_bufs[2], (F, D))
    mix = _mixer_fwd(z, wm, sb_t, g_v[0], g_q, g_k, sink_vec, biasm, "mixer_fwd")
    x2, ymix = _mixer_out(mix, wout, x1, gt2, "mixer_out")
    g3, df3, h3, a3, b3, dgt3, loss_sum = _ffn_fwd(x2, g_ffn2, sh3, sc3, gt3, w1b, w3b, w2b, tgt, "ffn2_fwd_loss")
    loss = lax.psum(loss_sum[0, 0] * (0.5 / D), ("x", "y", "c"))

    tk = _tile(S, 512)

    def ffn_weight_grads(h, da, db, s, df, tag, ex=None):
        gw1 = _matmul(h, da, "tn", F32, 1024, F // N_CHIPS, tk, f"grad_w1_{tag}", shard_major=True, ex=ex)
        gw1, carried = (gw1, []) if ex is None else (gw1[0], gw1[1:])
        gw3 = _matmul(h, db, "tn", F32, 1024, F // N_CHIPS, tk, f"grad_w3_{tag}", shard_major=True)
        gw2 = _matmul(s, df, "tn", F32, _tile(F, 1408), 1024, tk, f"grad_w2_{tag}")
        return [gw1, gw3, jnp.reshape(gw2, (N_CHIPS, F // N_CHIPS, D))], carried

    def chip_sums(grads, recv, tag):
        return [_chip_sum(g, r, c_arr, f"chip_sum_{tag}_{t}") for t, (g, r) in enumerate(zip(grads, recv))]

    def owner_sums(grads, recv, landed, tag):
        return [_owner_sum(g, r, l, jc_arr, f"owner_sum_{tag}_{t}") for t, (g, r, l) in enumerate(zip(grads, recv, landed))]

    da3, db3, s3, dh3 = _ffn_bwd(df3, a3, b3, w1b, w3b, w2b, "ffn2_bwd")
    grads_ffn2, _ = ffn_weight_grads(h3, da3, db3, s3, df3, "ffn2")
    res = _norm_bwd(dh3, x2, g3, g_ffn2, sc3, (ymix, gt2, 1.0), "ffn2_norm_bwd", ex=_ex_swap_halves(grads_ffn2))
    (g2, dsh3, dsc3, dgn3, dy, dgt2), recv_ffn2 = res[:6], res[6:]
    sums_ffn2 = chip_sums(grads_ffn2, recv_ffn2, "ffn2")

    dmix = _matmul(dy, wout, "nt", BF16, 512, 1024, D, "mixer_out_bwd")
    gwout_full = _matmul(mix, dy, "tn", F32, 1024, 1024, tk, "grad_w_out")
    res = _mixer_bwd(z, dmix, wm, wm_t, sb_t, g_v[0], g_q, g_k, sink_vec, biasm, "mixer_bwd", ex=_ex_scatter(sums_ffn2))
    (dz_main, dz_kv, dwm, dsb_t, dgv, dgq, dgk, dsk, dst), landed_ffn2 = res[:9], res[9:]
    fulls_ffn2 = owner_sums(grads_ffn2, recv_ffn2, landed_ffn2, "ffn2")
    dz = jnp.concatenate([dz_main, dz_kv], axis=1)
    dh2, *gf2 = _matmul(dz, win, "nt", F32, 512, 1024, _tile(IN_COLS, 1664), "mixer_in_bwd", ex=_ex_join_halves(fulls_ffn2))
    gwin_full = _matmul(h2, dz, "tn", F32, 1024, _tile(IN_COLS, 1664), tk, "grad_w_in")
    drel = _bucket_sum(jnp.reshape(dst, (B_HEADS, -1)), jnp.reshape(onehot, (N_BUCKETS, -1)), "bucket_sum")
    grads_mixer = [jnp.transpose(jnp.reshape(gwin_full, (D, N_CHIPS, -1)), (1, 0, 2)), jnp.reshape(gwout_full, (N_CHIPS, -1, D))]
    res = _norm_bwd(dh2, x1, g2, g_mix, sc2, (f1, gt1, 0.5), "mixer_norm_bwd", ex=_ex_swap_halves(grads_mixer))
    (g1, dsh2, dsc2, dgn2, df1, dgt1), recv_mixer = res[:6], res[6:]
    sums_mixer = chip_sums(grads_mixer, recv_mixer, "mixer")

    res = _ffn_bwd(df1, a1, b1, w1a, w3a, w2a, "ffn1_bwd", ex=_ex_scatter(sums_mixer))
    (da1, db1, s1, dh1), landed_mixer = res[:4], res[4:]
    fulls_mixer = owner_sums(grads_mixer, recv_mixer, landed_mixer, "mixer")
    grads_ffn1, gmx = ffn_weight_grads(h1, da1, db1, s1, df1, "ffn1", ex=_ex_join_halves(fulls_mixer))
    res = _norm_bwd(dh1, xs, g1, g_ffn1, sc1, None, "ffn1_norm_bwd", ex=_ex_swap_halves(grads_ffn1))
    (grad_x, dsh1, dsc1, dgn1), recv_ffn1 = res[:4], res[4:]
    sums_ffn1 = chip_sums(grads_ffn1, recv_ffn1, "ffn1")

    dmod = jnp.concatenate([dsh1, dsc1, dgt1, dsh2, dsc2, dgt2, dsh3, dsc3, dgt3], axis=1)
    small_w = [b_ada, g_ffn1, g_mix, g_ffn2, spatial_w, spatial_b, g_v, g_q, g_k, sinks, rel_bias]
    small_m = [m_b_ada, m_g_ffn1, m_g_mix, m_g_ffn2, m_spatial_w, m_spatial_b, m_g_v, m_g_q, m_g_k, m_sinks, m_rel_bias]
    small_v = [v_b_ada, v_g_ffn1, v_g_mix, v_g_ffn2, v_spatial_w, v_spatial_b, v_g_v, v_g_q, v_g_k, v_sinks, v_rel_bias]
    small_g = [dmod, dgn1, dgn2, dgn3, jnp.where(causal[None], dwm, 0.0), jnp.transpose(dsb_t), dgv, dgq, dgk, dsk, drel]
    shapes = [w.shape for w in small_w]
    gpack = _pack(small_g)
    rows = gpack.shape[0]
    gall = jnp.reshape(_allgather_small(gpack, "gather_small"), (N_DEV, rows, LANES))
    sg, sd, sm, sv = _small_update(gall, _pack(small_w), _pack(small_m), _pack(small_v), "small_update")
    sg, sd, sm, sv = [_unpack(p, shapes) for p in (sg, sd, sm, sv)]

    mod_rows = -(-N_MOD * D // (SUBLANES * LANES)) * SUBLANES
    dmod_all = jnp.reshape(gall[:, :mod_rows], (N_DEV, -1))[:, :N_MOD * D]
    dmod_sh = lax.dynamic_slice(dmod_all, (0, chip * mod_cols), (N_DEV, mod_cols))
    dmod_sh = jnp.pad(dmod_sh, ((0, BF16_ROWS - N_DEV), (0, 0))).astype(BF16)
    g_wada = _matmul(c_act, dmod_sh, "tn", F32, 1024, _tile(mod_cols, 512), BF16_ROWS, "grad_w_ada")
    res = _adamw(w_ada[0], g_wada, m_w_ada[0], v_w_ada[0], "adamw_w_ada", ex=_ex_scatter(sums_ffn1))
    (d_wada, nm_wada, nv_wada), landed_ffn1 = res[:3], res[3:]

    fulls_ffn1 = owner_sums(grads_ffn1, recv_ffn1, landed_ffn1, "ffn1")
    gf1 = _exchange_alone(_ex_join_halves(fulls_ffn1), "join_halves_ffn1")
    big = {}
    for nm, w, g, m, v in [
        ("w1_ffn1", w1_ffn1, gf1[0], m_w1_ffn1, v_w1_ffn1), ("w3_ffn1", w3_ffn1, gf1[1], m_w3_ffn1, v_w3_ffn1),
        ("w2_ffn1", w2_ffn1, gf1[2], m_w2_ffn1, v_w2_ffn1), ("w_in", w_in, gmx[0], m_w_in, v_w_in),
        ("w_out", w_out, gmx[1], m_w_out, v_w_out), ("w1_ffn2", w1_ffn2, gf2[0], m_w1_ffn2, v_w1_ffn2),
        ("w3_ffn2", w3_ffn2, gf2[1], m_w3_ffn2, v_w3_ffn2), ("w2_ffn2", w2_ffn2, gf2[2], m_w2_ffn2, v_w2_ffn2),
    ]:
        d, nm_, nv_ = _adamw(w[0], g, m[0], v[0], f"adamw_{nm}")
        big[nm] = (g[None], d[None], nm_[None], nv_[None])
    big["w_ada"] = (g_wada[None], d_wada[None], nm_wada[None], nv_wada[None])

    order = ["w_ada", "b_ada", "g_ffn1", "w1_ffn1", "w3_ffn1", "w2_ffn1", "g_mix", "w_in", "spatial_w", "spatial_b",
             "g_v", "g_q", "g_k", "sinks", "rel_bias", "w_out", "g_ffn2", "w1_ffn2", "w3_ffn2", "w2_ffn2"]
    small_names = ["b_ada", "g_ffn1", "g_mix", "g_ffn2", "spatial_w", "spatial_b", "g_v", "g_q", "g_k", "sinks", "rel_bias"]
    for i, nm in enumerate(small_names):
        big[nm] = (sg[i], sd[i], sm[i], sv[i])
    outs = [loss, grad_x[None]]
    for kind in range(4):
        outs += [big[nm][kind] for nm in order]
    return tuple(outs)
```

```python
import functools
import math

import jax
import jax.numpy as jnp
import numpy as np
from jax import lax
from jax.experimental import pallas as pl
from jax.experimental.pallas import tpu as pltpu

F32 = jnp.float32
BF16 = jnp.bfloat16
MESH = pl.DeviceIdType.MESH
ANY = pl.BlockSpec(memory_space=pl.ANY)

EPS = 1e-6
BLOCK = 128
A_HEADS = 8
A_DIM = 128
A_WIDTH = A_HEADS * A_DIM
B_HEADS = 16
KV_HEADS = 2
GROUP = B_HEADS // KV_HEADS
HEAD_DIM = 64
B_WIDTH = B_HEADS * HEAD_DIM
KV_WIDTH = KV_HEADS * HEAD_DIM
Q_OFF = 2 * A_WIDTH
K_OFF = Q_OFF + B_WIDTH
V_OFF = K_OFF + KV_WIDTH
IN_COLS = V_OFF + KV_WIDTH
N_BUCKETS = 32
MAX_DISTANCE = 128
N_MOD = 9
N_CHIPS = 4
N_DEV = 8
NEG = -1e30

ADAM_LR = 0.001
ADAM_B1 = 0.9
ADAM_B2 = 0.999
ADAM_EPS = 1e-08
ADAM_WD = 0.01
ADAM_STEP = 10

LANES = 128
SUBLANES = 8
BF16_ROWS = 16
VMEM_LIMIT = 56 * 1024 * 1024

INV_SQRT2 = 1.0 / math.sqrt(2.0)
INV_SQRT_2PI = 1.0 / math.sqrt(2.0 * math.pi)


def _tile(n, pref, mult=LANES):
    t = (min(pref, n) // mult) * mult
    while t >= mult:
        if n % t == 0:
            return t
        t -= mult
    return n


def _params(sem):
    return pltpu.CompilerParams(dimension_semantics=sem, vmem_limit_bytes=VMEM_LIMIT)


class _Exchange:
    def __init__(self, operands, out_shapes, aliases, n_sems, plan):
        self.operands, self.out_shapes, self.aliases, self.n_sems, self.plan = operands, out_shapes, aliases, n_sems, plan


def _pallas(body, *, name, grid, in_specs, out_specs, out_shape, args, scratch_shapes=(), semantics=None, ex=None):
    if ex is None:
        return pl.pallas_call(body, name=name, grid=grid, in_specs=in_specs, out_specs=out_specs, out_shape=out_shape,
                              scratch_shapes=list(scratch_shapes), compiler_params=_params(semantics))(*args)
    n_in, n_out, n_scr = len(in_specs), len(out_specs), len(scratch_shapes)
    e_in, e_out = len(ex.operands), len(ex.out_shapes)

    def wrapped(*refs):
        ins, refs = refs[:n_in], refs[n_in:]
        ex_ins, refs = refs[:e_in], refs[e_in:]
        outs, refs = refs[:n_out], refs[n_out:]
        ex_outs, refs = refs[:e_out], refs[e_out:]
        scratch, (send_sems, recv_sems) = refs[:n_scr], refs[n_scr:]
        first, last = True, True
        for d, size in enumerate(grid):
            first = jnp.logical_and(first, pl.program_id(d) == 0)
            last = jnp.logical_and(last, pl.program_id(d) == size - 1)

        def start():
            sends, _ = ex.plan(ex_ins, ex_outs, send_sems, recv_sems)
            for cp in sends:
                _remote(*cp).start()

        def finish():
            sends, arrivals = ex.plan(ex_ins, ex_outs, send_sems, recv_sems)
            for cp in arrivals:
                _remote(*cp).wait_recv()
            for cp in sends:
                _remote(*cp).wait_send()

        if grid:
            pl.when(first)(start)
        else:
            start()
        if body is not None:
            body(*ins, *outs, *scratch)
        if grid:
            pl.when(last)(finish)
        else:
            finish()

    kwargs = dict(grid=grid) if grid else {}
    return pl.pallas_call(
        wrapped,
        name=name,
        in_specs=list(in_specs) + [ANY] * e_in,
        out_specs=list(out_specs) + [ANY] * e_out,
        out_shape=list(out_shape) + list(ex.out_shapes),
        input_output_aliases={n_in + i: n_out + o for i, o in ex.aliases.items()},
        scratch_shapes=list(scratch_shapes) + [pltpu.SemaphoreType.DMA((ex.n_sems,)), pltpu.SemaphoreType.DMA((ex.n_sems,))],
        compiler_params=_params(("arbitrary",) * len(grid) if grid else None),
        **kwargs,
    )(*args, *ex.operands)


def _dot(a, b, dims=(((1,), (0,)), ((), ()))):
    return lax.dot_general(a, b, dims, preferred_element_type=F32)


NN = (((1,), (0,)), ((), ()))
NT = (((1,), (1,)), ((), ()))
TN = (((0,), (0,)), ((), ()))


def _sigmoid(x):
    return 1.0 / (1.0 + jnp.exp(-x))


def _gelu_and_grad(x):
    cdf = 0.5 * (1.0 + lax.erf(x * INV_SQRT2))
    pdf = jnp.exp(-0.5 * x * x) * INV_SQRT_2PI
    return x * cdf, cdf + x * pdf


def _gelu(x):
    return x * (0.5 * (1.0 + lax.erf(x * INV_SQRT2)))


def _rms(x):
    r = lax.rsqrt(jnp.mean(x * x, axis=-1, keepdims=True) + EPS)
    return x * r, r


ROW_CHUNK = 64


def _for_rows(tm, fn):
    rc = min(ROW_CHUNK, tm)

    def step(r, carry):
        fn(pl.ds(pl.multiple_of(r * rc, rc), rc))
        return carry

    lax.fori_loop(0, tm // rc, step, 0)


def _rms_bwd(dy, xhat, r):
    return r * (dy - xhat * jnp.mean(dy * xhat, axis=-1, keepdims=True))


def _matmul(a, b, mode, out_dtype, tm, tn, tk, name, shard_major=False, ex=None):
    if mode == "nn":
        (M, K), N = a.shape, b.shape[1]
    elif mode == "nt":
        (M, K), N = a.shape, b.shape[0]
    else:
        (K, M), N = a.shape, b.shape[1]
    tm, tn, tk = min(tm, M), min(tn, N), min(tk, K)
    assert M % tm == 0 and N % tn == 0 and K % tk == 0, (name, M, N, K, tm, tn, tk)
    nk = K // tk
    dims = {"nn": NN, "nt": NT, "tn": TN}[mode]
    a_spec = pl.BlockSpec((tk, tm), lambda i, j, k: (k, i)) if mode == "tn" else pl.BlockSpec((tm, tk), lambda i, j, k: (i, k))
    b_spec = pl.BlockSpec((tn, tk), lambda i, j, k: (j, k)) if mode == "nt" else pl.BlockSpec((tk, tn), lambda i, j, k: (k, j))
    if shard_major:
        assert tn * N_CHIPS == N
        out_shape = jax.ShapeDtypeStruct((N_CHIPS, M, tn), out_dtype)
        o_spec = pl.BlockSpec((None, tm, tn), lambda i, j, k: (j, i, 0))
    else:
        out_shape = jax.ShapeDtypeStruct((M, N), out_dtype)
        o_spec = pl.BlockSpec((tm, tn), lambda i, j, k: (i, j))

    direct = nk == 1 or out_dtype == F32

    def body(a_ref, b_ref, o_ref, *scratch):
        k = pl.program_id(2)
        if nk == 1:
            o_ref[...] = _dot(a_ref[...], b_ref[...], dims).astype(o_ref.dtype)
            return
        acc_ref = o_ref if direct else scratch[0]

        @pl.when(k == 0)
        def _():
            acc_ref[...] = jnp.zeros(acc_ref.shape, F32)

        acc_ref[...] += _dot(a_ref[...], b_ref[...], dims)
        if not direct:
            @pl.when(k == nk - 1)
            def _():
                o_ref[...] = acc_ref[...].astype(o_ref.dtype)

    outs = _pallas(body, name=name, grid=(M // tm, N // tn, nk), in_specs=[a_spec, b_spec], out_specs=[o_spec],
                   out_shape=[out_shape], scratch_shapes=[] if direct else [pltpu.VMEM((tm, tn), F32)],
                   semantics=("parallel", "parallel", "arbitrary"), args=[a, b], ex=ex)
    return outs[0] if ex is None else outs


def _mod_partial(c_all, w_ada, b_sh, name):
    R, D = c_all.shape
    N = w_ada.shape[1]
    tn = _tile(N, 512)

    def body(c_ref, w_ref, b_ref, o_ref, ca_ref):
        cv = c_ref[...]
        ca = (cv * _sigmoid(cv)).astype(BF16)
        ca_ref[...] = ca
        o_ref[...] = _dot(ca, w_ref[...].astype(BF16)) + b_ref[...]

    return pl.pallas_call(
        body,
        name=name,
        grid=(N // tn,),
        in_specs=[
            pl.BlockSpec((R, D), lambda j: (0, 0)),
            pl.BlockSpec((D, tn), lambda j: (0, j)),
            pl.BlockSpec((1, tn), lambda j: (0, j)),
        ],
        out_specs=[pl.BlockSpec((R, tn), lambda j: (0, j)), pl.BlockSpec((R, D), lambda j: (0, 0))],
        out_shape=[jax.ShapeDtypeStruct((R, N), F32), jax.ShapeDtypeStruct((R, D), BF16)],
        compiler_params=_params(("arbitrary",)),
    )(c_all, w_ada, b_sh)


def _ffn_fwd(x, g, sh, sc, gt, w1, w3, w2, tgt, name, ex=None):
    S, D = x.shape
    F = w1.shape[1]
    tm, tf = _tile(S, 512), _tile(F, 512)
    ni, nj = S // tm, F // tf
    with_loss = tgt is not None

    def body(*refs):
        if with_loss:
            (x_ref, g_ref, sh_ref, sc_ref, gt_ref, w1_ref, w3_ref, w2_ref, tgt_ref,
             gout_ref, df_ref, h_ref, a_ref, b_ref, dgt_ref, loss_ref, hs_ref, acc_ref) = refs
        else:
            (x_ref, g_ref, sh_ref, sc_ref, gt_ref, w1_ref, w3_ref, w2_ref,
             xo_ref, h_ref, a_ref, b_ref, f_ref, hs_ref, acc_ref) = refs
        i, j = pl.program_id(0), pl.program_id(1)

        @pl.when(j == 0)
        def _():
            def prologue(rows):
                xhat, _ = _rms(x_ref[rows, :])
                hb = ((xhat * g_ref[...]) * (1.0 + sc_ref[...]) + sh_ref[...]).astype(BF16)
                hs_ref[rows, :] = hb
                h_ref[rows, :] = hb

            _for_rows(tm, prologue)

        hb = hs_ref[...]
        av = _dot(hb, w1_ref[...])
        bv = _dot(hb, w3_ref[...])
        a_ref[...] = av.astype(BF16)
        b_ref[...] = bv.astype(BF16)
        sv = ((av * _sigmoid(av)) * bv).astype(BF16)

        @pl.when(j == 0)
        def _():
            acc_ref[...] = jnp.zeros(acc_ref.shape, F32)

        acc_ref[...] += _dot(sv, w2_ref[...])

        @pl.when(j == nj - 1)
        def _():
            if with_loss:
                @pl.when(i == 0)
                def _():
                    dgt_ref[...] = jnp.zeros(dgt_ref.shape, F32)
                    loss_ref[...] = jnp.zeros(loss_ref.shape, F32)

            def epilogue(rows):
                fv = acc_ref[rows, :]
                half_gate = 0.5 * gt_ref[...]
                xo = x_ref[rows, :] + half_gate * fv
                if not with_loss:
                    xo_ref[rows, :] = xo
                    f_ref[rows, :] = fv.astype(f_ref.dtype)
                    return
                err = xo - tgt_ref[rows, :]
                gout = err * (1.0 / D)
                gout_ref[rows, :] = gout
                df_ref[rows, :] = (half_gate * gout).astype(BF16)
                dgt_ref[...] += 0.5 * jnp.sum(gout * fv, axis=0, keepdims=True)
                loss_part = jnp.sum(jnp.sum(err * err, axis=1, keepdims=True), axis=0, keepdims=True)
                loss_ref[...] += jnp.broadcast_to(loss_part, loss_ref.shape)

            _for_rows(tm, epilogue)

    row = pl.BlockSpec((tm, D), lambda i, j: (i, 0))
    row_in = pl.BlockSpec((tm, D), lambda i, j: (i, 0), pipeline_mode=pl.Buffered(1))
    vec = pl.BlockSpec((1, D), lambda i, j: (0, 0))
    col = pl.BlockSpec((tm, tf), lambda i, j: (i, j))
    in_specs = [row_in, vec, vec, vec, vec,
                pl.BlockSpec((D, tf), lambda i, j: (0, j)),
                pl.BlockSpec((D, tf), lambda i, j: (0, j)),
                pl.BlockSpec((tf, D), lambda i, j: (j, 0))]
    args = [x, g, sh, sc, gt, w1, w3, w2]
    act = jax.ShapeDtypeStruct((S, F), BF16)
    if with_loss:
        in_specs.append(row_in)
        args.append(tgt)
        out_specs = [row, row, row, col, col, vec, pl.BlockSpec((1, LANES), lambda i, j: (0, 0))]
        out_shape = [jax.ShapeDtypeStruct((S, D), F32), jax.ShapeDtypeStruct((S, D), BF16),
                     jax.ShapeDtypeStruct((S, D), BF16), act, act,
                     jax.ShapeDtypeStruct((1, D), F32), jax.ShapeDtypeStruct((1, LANES), F32)]
    else:
        out_specs = [row, row, col, col, row]
        out_shape = [jax.ShapeDtypeStruct((S, D), F32), jax.ShapeDtypeStruct((S, D), BF16), act, act,
                     jax.ShapeDtypeStruct((S, D), BF16)]
    return _pallas(body, name=name, grid=(ni, nj), in_specs=in_specs, out_specs=out_specs, out_shape=out_shape,
                   scratch_shapes=[pltpu.VMEM((tm, D), BF16), pltpu.VMEM((tm, D), F32)],
                   semantics=("arbitrary", "arbitrary"), args=args, ex=ex)


def _ffn_bwd(df, a, b, w1, w3, w2, name, ex=None):
    S, D = df.shape
    F = a.shape[1]
    tm, tf = _tile(S, 512), _tile(F, 512)
    nj = F // tf

    def body(df_ref, a_ref, b_ref, w1_ref, w3_ref, w2_ref, da_ref, db_ref, s_ref, dh_ref):
        j = pl.program_id(1)
        ds = _dot(df_ref[...], w2_ref[...], NT)
        av = a_ref[...].astype(F32)
        bv = b_ref[...].astype(F32)
        sig = _sigmoid(av)
        sil = av * sig
        da = ((ds * bv) * (sig * (1.0 + av * (1.0 - sig)))).astype(BF16)
        db = (ds * sil).astype(BF16)
        da_ref[...] = da
        db_ref[...] = db
        s_ref[...] = (sil * bv).astype(BF16)
        @pl.when(j == 0)
        def _():
            dh_ref[...] = jnp.zeros(dh_ref.shape, F32)

        dh_ref[...] += _dot(da, w1_ref[...], NT) + _dot(db, w3_ref[...], NT)

    row = pl.BlockSpec((tm, D), lambda i, j: (i, 0))
    col = pl.BlockSpec((tm, tf), lambda i, j: (i, j))
    act = jax.ShapeDtypeStruct((S, F), BF16)
    return _pallas(body, name=name, grid=(S // tm, nj),
                   in_specs=[row, col, col,
                             pl.BlockSpec((D, tf), lambda i, j: (0, j)),
                             pl.BlockSpec((D, tf), lambda i, j: (0, j)),
                             pl.BlockSpec((tf, D), lambda i, j: (j, 0))],
                   out_specs=[col, col, col, row],
                   out_shape=[act, act, act, jax.ShapeDtypeStruct((S, D), F32)],
                   semantics=("parallel", "arbitrary"), args=[df, a, b, w1, w3, w2], ex=ex)


def _norm_mod(x, g, sh, sc, name, ex=None):
    S, D = x.shape
    tm = _tile(S, 512)

    def body(x_ref, g_ref, sh_ref, sc_ref, h_ref):
        def step(rows):
            xhat, _ = _rms(x_ref[rows, :])
            h_ref[rows, :] = ((xhat * g_ref[...]) * (1.0 + sc_ref[...]) + sh_ref[...]).astype(BF16)

        _for_rows(tm, step)

    row = pl.BlockSpec((tm, D), lambda i: (i, 0))
    vec = pl.BlockSpec((1, D), lambda i: (0, 0))
    outs = _pallas(body, name=name, grid=(S // tm,), in_specs=[row, vec, vec, vec], out_specs=[row],
                   out_shape=[jax.ShapeDtypeStruct((S, D), BF16)], semantics=("parallel",), args=[x, g, sh, sc], ex=ex)
    return outs[0] if ex is None else outs


def _norm_bwd(dh, x, gres, g, sc, prev, name, ex=None):
    S, D = x.shape
    tm = _tile(S, 256)
    has_prev = prev is not None
    coef = prev[2] if has_prev else None

    def body(*refs):
        if has_prev:
            (dh_ref, x_ref, gr_ref, g_ref, sc_ref, f_ref, gt_ref,
             go_ref, dsh_ref, dsc_ref, dg_ref, dp_ref, dgt_ref) = refs
        else:
            dh_ref, x_ref, gr_ref, g_ref, sc_ref, go_ref, dsh_ref, dsc_ref, dg_ref = refs
        sum_refs = [dsh_ref, dsc_ref, dg_ref] + ([dgt_ref] if has_prev else [])

        @pl.when(pl.program_id(0) == 0)
        def _():
            for ref in sum_refs:
                ref[...] = jnp.zeros(ref.shape, F32)

        def step(rows):
            dh = dh_ref[rows, :]
            xhat, r = _rms(x_ref[rows, :])
            gain = g_ref[...]
            scale1 = 1.0 + sc_ref[...]
            gout = gr_ref[rows, :] + _rms_bwd(dh * scale1 * gain, xhat, r)
            go_ref[rows, :] = gout
            sums = [dh, dh * (xhat * gain), dh * scale1 * xhat]
            if has_prev:
                dp_ref[rows, :] = ((coef * gt_ref[...]) * gout).astype(BF16)
                sums.append(coef * (gout * f_ref[rows, :].astype(F32)))
            for ref, v in zip(sum_refs, sums):
                ref[...] += jnp.sum(v, axis=0, keepdims=True)

        _for_rows(tm, step)

    row = pl.BlockSpec((tm, D), lambda i: (i, 0))
    vec = pl.BlockSpec((1, D), lambda i: (0, 0))
    vshape = jax.ShapeDtypeStruct((1, D), F32)
    in_specs = [row, row, row, vec, vec]
    args = [dh, x, gres, g, sc]
    out_specs = [row, vec, vec, vec]
    out_shape = [jax.ShapeDtypeStruct((S, D), F32), vshape, vshape, vshape]
    if has_prev:
        in_specs += [row, vec]
        args += [prev[0], prev[1]]
        out_specs += [row, vec]
        out_shape += [jax.ShapeDtypeStruct((S, D), BF16), vshape]
    return _pallas(body, name=name, grid=(S // tm,), in_specs=in_specs, out_specs=out_specs, out_shape=out_shape,
                   semantics=("arbitrary",), args=args, ex=ex)


def _attn_probs(qn_b, kb, bias, sink):
    s = _dot(qn_b, kb, NT) * (HEAD_DIM ** -0.5) + bias
    m = jnp.maximum(jnp.max(s, axis=-1, keepdims=True), sink)
    p = jnp.exp(s - m)
    e_sink = jnp.exp(sink - m)
    inv = 1.0 / (jnp.sum(p, axis=-1, keepdims=True) + e_sink)
    return p * inv, e_sink * inv


def _mixer_fwd(z, wm, sb_t, gv, gq, gk, sinks, biasm, name):
    S = z.shape[0]
    nb = S // BLOCK

    def body(z_ref, zp_ref, wm_ref, sbt_ref, gv_ref, gq_ref, gk_ref, sk_ref, bias_ref, mix_ref):
        for h in range(A_HEADS):
            lo = h * A_DIM
            u = _gelu(z_ref[:, lo:lo + A_DIM])
            v = _gelu(z_ref[:, A_WIDTH + lo:A_WIDTH + lo + A_DIM])
            vhat, _ = _rms(v)
            vn = (vhat * gv_ref[h:h + 1, :]).astype(BF16)
            mixed = _dot(wm_ref[h], vn) + sbt_ref[:, h:h + 1]
            mix_ref[:, lo:lo + A_DIM] = (u * mixed).astype(BF16)
        gq_v, gk_v = gq_ref[...], gk_ref[...]
        for kh in range(KV_HEADS):
            ko = kh * HEAD_DIM
            kp, _ = _rms(zp_ref[:, ko:ko + HEAD_DIM])
            kc, _ = _rms(z_ref[:, K_OFF + ko:K_OFF + ko + HEAD_DIM])
            kb = jnp.concatenate([kp * gk_v, kc * gk_v], axis=0).astype(BF16)
            vb = jnp.concatenate([zp_ref[:, KV_WIDTH + ko:KV_WIDTH + ko + HEAD_DIM],
                                  z_ref[:, V_OFF + ko:V_OFF + ko + HEAD_DIM]], axis=0).astype(BF16)
            for g in range(GROUP):
                h = kh * GROUP + g
                qo = Q_OFF + h * HEAD_DIM
                qhat, _ = _rms(z_ref[:, qo:qo + HEAD_DIM])
                w, _ = _attn_probs((qhat * gq_v).astype(BF16), kb, bias_ref[h], sk_ref[h])
                o = _dot(w.astype(BF16), vb)
                mix_ref[:, A_WIDTH + h * HEAD_DIM:A_WIDTH + (h + 1) * HEAD_DIM] = o.astype(BF16)

    full = lambda shape: pl.BlockSpec(shape, lambda n: (0,) * len(shape))
    return pl.pallas_call(
        body,
        name=name,
        grid=(nb,),
        in_specs=[
            pl.BlockSpec((BLOCK, IN_COLS), lambda n: (n, 0)),
            pl.BlockSpec((BLOCK, 2 * KV_WIDTH), lambda n: (jnp.maximum(n - 1, 0), K_OFF // (2 * KV_WIDTH))),
            full((A_HEADS, BLOCK, BLOCK)), full((BLOCK, A_HEADS)), full((A_HEADS, A_DIM)),
            full((1, HEAD_DIM)), full((1, HEAD_DIM)),
            pl.BlockSpec(memory_space=pltpu.SMEM),
            pl.BlockSpec((None, B_HEADS, BLOCK, 2 * BLOCK), lambda n: (jnp.minimum(n, 1), 0, 0, 0)),
        ],
        out_specs=pl.BlockSpec((BLOCK, A_WIDTH + B_WIDTH), lambda n: (n, 0)),
        out_shape=jax.ShapeDtypeStruct((S, A_WIDTH + B_WIDTH), BF16),
        compiler_params=_params(("parallel",)),
    )(z, z, wm, sb_t, gv, gq, gk, sinks, biasm)


def _mixer_bwd(z, dmix, wm, wm_t, sb_t, gv, gq, gk, sinks, biasm, name, ex=None):
    S = z.shape[0]
    nb = S // BLOCK

    def body(z_ref, zp_ref, dmix_ref, wm_ref, wmt_ref, sbt_ref, gv_ref, gq_ref, gk_ref, sk_ref, bias_ref,
             dz_ref, dzkv_ref, dwm_ref, dsb_ref, dgv_ref, dgq_ref, dgk_ref, dsk_ref, dst_ref,
             carry_ref, tot_ref, sbacc_ref):
        n = pl.program_id(0)

        @pl.when(n == 0)
        def _():
            for ref in (dwm_ref, dgv_ref, dgq_ref, dgk_ref, dsk_ref, dst_ref, carry_ref, sbacc_ref):
                ref[...] = jnp.zeros(ref.shape, ref.dtype)

        @pl.when(n < nb)
        def _():
            for h in range(A_HEADS):
                lo = h * A_DIM
                u, du_dz = _gelu_and_grad(z_ref[:, lo:lo + A_DIM])
                v, dv_dz = _gelu_and_grad(z_ref[:, A_WIDTH + lo:A_WIDTH + lo + A_DIM])
                vhat, rv = _rms(v)
                gvh = gv_ref[h:h + 1, :]
                vn = (vhat * gvh).astype(BF16)
                mixed = _dot(wm_ref[h], vn) + sbt_ref[:, h:h + 1]
                dya = dmix_ref[:, lo:lo + A_DIM].astype(F32)
                dmx = dya * u
                sbacc_ref[h] += dmx
                dmx_b = dmx.astype(BF16)
                dwm_ref[h] += _dot(dmx_b, vn, NT)
                dvn = _dot(wmt_ref[h], dmx_b)
                dgv_ref[h:h + 1, :] += jnp.sum(dvn * vhat, axis=0, keepdims=True)
                dz_ref[:, lo:lo + A_DIM] = ((dya * mixed) * du_dz).astype(BF16)
                dz_ref[:, A_WIDTH + lo:A_WIDTH + lo + A_DIM] = (_rms_bwd(dvn * gvh, vhat, rv) * dv_dz).astype(BF16)
            gq_v, gk_v = gq_ref[...], gk_ref[...]
            dgq = jnp.zeros((1, HEAD_DIM), F32)
            for kh in range(KV_HEADS):
                ko = kh * HEAD_DIM
                kp, _ = _rms(zp_ref[:, ko:ko + HEAD_DIM])
                kc, _ = _rms(z_ref[:, K_OFF + ko:K_OFF + ko + HEAD_DIM])
                kb = jnp.concatenate([kp * gk_v, kc * gk_v], axis=0).astype(BF16)
                vb = jnp.concatenate([zp_ref[:, KV_WIDTH + ko:KV_WIDTH + ko + HEAD_DIM],
                                      z_ref[:, V_OFF + ko:V_OFF + ko + HEAD_DIM]], axis=0).astype(BF16)
                dkb = jnp.zeros((2 * BLOCK, HEAD_DIM), F32)
                dvb = jnp.zeros((2 * BLOCK, HEAD_DIM), F32)
                for g in range(GROUP):
                    h = kh * GROUP + g
                    qo = Q_OFF + h * HEAD_DIM
                    qhat, rq = _rms(z_ref[:, qo:qo + HEAD_DIM])
                    qn_b = (qhat * gq_v).astype(BF16)
                    w, w_sink = _attn_probs(qn_b, kb, bias_ref[h], sk_ref[h])
                    do = dmix_ref[:, A_WIDTH + h * HEAD_DIM:A_WIDTH + (h + 1) * HEAD_DIM]
                    dp = _dot(do, vb, NT)
                    delta = jnp.sum(w * dp, axis=-1, keepdims=True)
                    ds = w * (dp - delta)
                    dsk_ref[0:1, h:h + 1] += -jnp.sum(w_sink * delta, axis=0, keepdims=True)
                    dst_ref[h] += ds
                    ds_b = (ds * (HEAD_DIM ** -0.5)).astype(BF16)
                    dqn = _dot(ds_b, kb)
                    dkb += _dot(ds_b, qn_b, TN)
                    dvb += _dot(w.astype(BF16), do, TN)
                    dgq += jnp.sum(dqn * qhat, axis=0, keepdims=True)
                    dz_ref[:, qo:qo + HEAD_DIM] = _rms_bwd(dqn * gq_v, qhat, rq).astype(BF16)
                tot_ref[0, :, ko:ko + HEAD_DIM] = carry_ref[0, :, ko:ko + HEAD_DIM] + dkb[:BLOCK]
                tot_ref[1, :, ko:ko + HEAD_DIM] = carry_ref[1, :, ko:ko + HEAD_DIM] + dvb[:BLOCK]
                carry_ref[0, :, ko:ko + HEAD_DIM] = dkb[BLOCK:]
                carry_ref[1, :, ko:ko + HEAD_DIM] = dvb[BLOCK:]
            dgq_ref[...] += dgq

        @pl.when(n == nb)
        def _():
            tot_ref[...] = carry_ref[...]
            for h in range(A_HEADS):
                dsb_ref[:, h:h + 1] = jnp.sum(sbacc_ref[h], axis=1, keepdims=True)

        gk_v = gk_ref[...]
        dgk = jnp.zeros((1, HEAD_DIM), F32)
        for kh in range(KV_HEADS):
            ko = kh * HEAD_DIM
            khat, rk = _rms(zp_ref[:, ko:ko + HEAD_DIM])
            dkn = tot_ref[0, :, ko:ko + HEAD_DIM]
            dgk += jnp.sum(dkn * khat, axis=0, keepdims=True)
            dzkv_ref[:, ko:ko + HEAD_DIM] = _rms_bwd(dkn * gk_v, khat, rk).astype(BF16)
        dgk_ref[...] += dgk
        dzkv_ref[:, KV_WIDTH:] = tot_ref[1].astype(BF16)

    last = nb - 1
    full = lambda shape: pl.BlockSpec(shape, lambda n: (0,) * len(shape))
    return _pallas(
        body,
        name=name,
        grid=(nb + 1,),
        ex=ex,
        in_specs=[
            pl.BlockSpec((BLOCK, IN_COLS), lambda n: (jnp.minimum(n, last), 0)),
            pl.BlockSpec((BLOCK, 2 * KV_WIDTH), lambda n: (jnp.maximum(n - 1, 0), K_OFF // (2 * KV_WIDTH))),
            pl.BlockSpec((BLOCK, A_WIDTH + B_WIDTH), lambda n: (jnp.minimum(n, last), 0)),
            full((A_HEADS, BLOCK, BLOCK)), full((A_HEADS, BLOCK, BLOCK)), full((BLOCK, A_HEADS)),
            full((A_HEADS, A_DIM)), full((1, HEAD_DIM)), full((1, HEAD_DIM)),
            pl.BlockSpec(memory_space=pltpu.SMEM),
            pl.BlockSpec((None, B_HEADS, BLOCK, 2 * BLOCK), lambda n: (jnp.minimum(n, 1), 0, 0, 0)),
        ],
        out_specs=[
            pl.BlockSpec((BLOCK, K_OFF), lambda n: (jnp.minimum(n, last), 0)),
            pl.BlockSpec((BLOCK, 2 * KV_WIDTH), lambda n: (jnp.maximum(n - 1, 0), 0)),
            full((A_HEADS, BLOCK, BLOCK)), full((BLOCK, A_HEADS)), full((A_HEADS, A_DIM)),
            full((1, HEAD_DIM)), full((1, HEAD_DIM)), full((1, B_HEADS)),
            full((B_HEADS, BLOCK, 2 * BLOCK)),
        ],
        out_shape=[
            jax.ShapeDtypeStruct((S, K_OFF), BF16),
            jax.ShapeDtypeStruct((S, 2 * KV_WIDTH), BF16),
            jax.ShapeDtypeStruct((A_HEADS, BLOCK, BLOCK), F32),
            jax.ShapeDtypeStruct((BLOCK, A_HEADS), F32),
            jax.ShapeDtypeStruct((A_HEADS, A_DIM), F32),
            jax.ShapeDtypeStruct((1, HEAD_DIM), F32),
            jax.ShapeDtypeStruct((1, HEAD_DIM), F32),
            jax.ShapeDtypeStruct((1, B_HEADS), F32),
            jax.ShapeDtypeStruct((B_HEADS, BLOCK, 2 * BLOCK), F32),
        ],
        scratch_shapes=[
            pltpu.VMEM((2, BLOCK, KV_WIDTH), F32),
            pltpu.VMEM((2, BLOCK, KV_WIDTH), F32),
            pltpu.VMEM((A_HEADS, BLOCK, A_DIM), F32),
        ],
        semantics=("arbitrary",),
        args=[z, z, dmix, wm, wm_t, sb_t, gv, gq, gk, sinks, biasm],
    )


def _mixer_out(mix, w_out, x, gt, name):
    S, D = x.shape
    K = mix.shape[1]
    tm, tn = _tile(S, 512), _tile(D, 1024)

    def body(m_ref, w_ref, x_ref, gt_ref, xo_ref, y_ref):
        y = _dot(m_ref[...], w_ref[...])
        y_ref[...] = y
        xo_ref[...] = x_ref[...] + gt_ref[...] * y

    blk = pl.BlockSpec((tm, tn), lambda j, i: (i, j))
    return pl.pallas_call(
        body,
        name=name,
        grid=(D // tn, S // tm),
        in_specs=[pl.BlockSpec((tm, K), lambda j, i: (i, 0)), pl.BlockSpec((K, tn), lambda j, i: (0, j)),
                  blk, pl.BlockSpec((1, tn), lambda j, i: (0, j))],
        out_specs=[blk, blk],
        out_shape=[jax.ShapeDtypeStruct((S, D), F32), jax.ShapeDtypeStruct((S, D), F32)],
        compiler_params=_params(("parallel", "parallel")),
    )(mix, w_out, x, gt)


def _bucket_sum(dst, onehot, name):
    def body(d_ref, o_ref, out_ref):
        out_ref[...] = lax.dot_general(o_ref[...], d_ref[...], NT, precision=lax.Precision.HIGHEST,
                                       preferred_element_type=F32)

    return pl.pallas_call(
        body,
        name=name,
        out_shape=jax.ShapeDtypeStruct((N_BUCKETS, B_HEADS), F32),
    )(dst, onehot)


def _adamw_math(w, g, m, v):
    m = ADAM_B1 * m + (1.0 - ADAM_B1) * g
    v = ADAM_B2 * v + (1.0 - ADAM_B2) * (g * g)
    m_hat = m / (1.0 - ADAM_B1 ** ADAM_STEP)
    v_hat = v / (1.0 - ADAM_B2 ** ADAM_STEP)
    delta = -ADAM_LR * (m_hat / (jnp.sqrt(v_hat) + ADAM_EPS) + ADAM_WD * w)
    return delta, m, v


def _adamw(w, g, m, v, name, ex=None, emit_grad=False):
    R, C = w.shape
    tr = _tile(R, max(SUBLANES, (1 << 19) // C), SUBLANES)

    def body(w_ref, g_ref, m_ref, v_ref, *out_refs):
        gv = g_ref[...]
        results = _adamw_math(w_ref[...], gv, m_ref[...], v_ref[...])
        for ref, val in zip(out_refs, ((gv,) if emit_grad else ()) + results):
            ref[...] = val

    blk = pl.BlockSpec((tr, C), lambda i: (i, 0))
    shape = jax.ShapeDtypeStruct((R, C), F32)
    n_out = 4 if emit_grad else 3
    return _pallas(body, name=name, grid=(R // tr,), in_specs=[blk] * 4, out_specs=[blk] * n_out,
                   out_shape=[shape] * n_out, semantics=("parallel",), args=[w, g, m, v], ex=ex)


def _small_update(parts, w, m, v, name):
    R = w.shape[0]

    def body(p_ref, w_ref, m_ref, v_ref, g_ref, d_ref, mo_ref, vo_ref):
        g = p_ref[0]
        for dev in range(1, N_DEV):
            g = g + p_ref[dev]
        g_ref[...] = g
        d, mn, vn = _adamw_math(w_ref[...], g, m_ref[...], v_ref[...])
        d_ref[...] = d
        mo_ref[...] = mn
        vo_ref[...] = vn

    shape = jax.ShapeDtypeStruct((R, LANES), F32)
    return pl.pallas_call(
        body,
        name=name,
        out_shape=[shape] * 4,
        compiler_params=pltpu.CompilerParams(vmem_limit_bytes=VMEM_LIMIT),
    )(parts, w, m, v)


def _place():
    x, y, c = lax.axis_index("x"), lax.axis_index("y"), lax.axis_index("c")
    chips = [(1 - x, y), (x, 1 - y), (1 - x, 1 - y)]
    return x, y, c, chips


def _remote(src, dst, send_sem, recv_sem, to):
    return pltpu.make_async_remote_copy(src_ref=src, dst_ref=dst, send_sem=send_sem, recv_sem=recv_sem,
                                        device_id=to, device_id_type=MESH)


def _allgather_small(block, name):
    m_per, n = block.shape

    def body(x_ref, out_ref, send_sems, recv_sems, local_sem):
        x, y, c, chips = _place()
        me, sibling = (x, y, c), (x, y, 1 - c)

        def rows(px, py, pc):
            return out_ref.at[pl.ds((4 * px + 2 * py + pc) * m_per, m_per), :]

        def copy(k, blk, to, src=None):
            return _remote(rows(*blk) if src is None else src, rows(*blk), send_sems.at[k], recv_sems.at[k], to)

        mine = pltpu.make_async_copy(x_ref, rows(*me), local_sem)
        mine.start()
        first = [copy(0, me, sibling, src=x_ref)]
        first += [copy(1 + j, me, (*chip, c), src=x_ref) for j, chip in enumerate(chips)]
        for cp in first:
            cp.start()
        passed = [copy(4 + j, (*chip, c), sibling) for j, chip in enumerate(chips)]
        for j, chip in enumerate(chips):
            copy(1 + j, (*chip, c), me).wait_recv()
            passed[j].start()
        copy(0, sibling, me).wait_recv()
        for j, chip in enumerate(chips):
            copy(4 + j, (*chip, 1 - c), me).wait_recv()
        for cp in first + passed:
            cp.wait_send()
        mine.wait()

    return pl.pallas_call(
        body,
        name=name,
        out_shape=jax.ShapeDtypeStruct((N_DEV * m_per, n), block.dtype),
        in_specs=[pl.BlockSpec(memory_space=pltpu.VMEM)],
        out_specs=pl.BlockSpec(memory_space=pltpu.VMEM),
        scratch_shapes=[pltpu.SemaphoreType.DMA((7,)), pltpu.SemaphoreType.DMA((7,)), pltpu.SemaphoreType.DMA],
        compiler_params=pltpu.CompilerParams(vmem_limit_bytes=VMEM_LIMIT),
    )(block)


def _half(ref, c, rows):
    start = pl.multiple_of(c * rows, BF16_ROWS)
    if len(ref.shape) == 2:
        return ref.at[pl.ds(start, rows), :]
    return ref.at[:, pl.ds(start, rows), :]


def _same(arrays):
    return [jax.ShapeDtypeStruct(a.shape, a.dtype) for a in arrays], {t: t for t in range(len(arrays))}


def _ex_gather_ici(bufs):
    def plan(ins, outs, send_sems, recv_sems):
        x, y, c, chips = _place()
        sends, arrivals = [], []
        for t, buf in enumerate(bufs):
            rows = buf.shape[1] // 2
            mine = _half(outs[t].at[2 * x + y], c, rows)
            for k, (px, py) in enumerate(chips):
                sems = (send_sems.at[3 * t + k], recv_sems.at[3 * t + k], (px, py, c))
                landed = _half(outs[t].at[2 * px + py], c, rows)
                sends.append((mine, mine, *sems))
                arrivals.append((landed, landed, *sems))
        return sends, arrivals

    shapes, aliases = _same(bufs)
    return _Exchange(bufs, shapes, aliases, 3 * len(bufs), plan)


def _ex_gather_d2d(bufs):
    def plan(ins, outs, send_sems, recv_sems):
        x, y, c, chips = _place()
        sends, arrivals = [], []
        for t, buf in enumerate(bufs):
            rows = buf.shape[1] // 2
            for k, (px, py) in enumerate(chips):
                sems = (send_sems.at[3 * t + k], recv_sems.at[3 * t + k], (x, y, 1 - c))
                landed = _half(outs[t].at[2 * px + py], c, rows)
                other = _half(outs[t].at[2 * px + py], 1 - c, rows)
                sends.append((landed, landed, *sems))
                arrivals.append((other, other, *sems))
        return sends, arrivals

    shapes, aliases = _same(bufs)
    return _Exchange(bufs, shapes, aliases, 3 * len(bufs), plan)


def _ex_swap_halves(grads):
    def plan(ins, outs, send_sems, recv_sems):
        x, y, c, _ = _place()
        sends = [(_half(ins[t], 1 - c, g.shape[1] // 2), outs[t], send_sems.at[t], recv_sems.at[t], (x, y, 1 - c))
                 for t, g in enumerate(grads)]
        return sends, sends

    shapes = [jax.ShapeDtypeStruct((g.shape[0], g.shape[1] // 2, g.shape[2]), g.dtype) for g in grads]
    return _Exchange(grads, shapes, {}, len(grads), plan)


def _ex_scatter(sums):
    def plan(ins, outs, send_sems, recv_sems):
        x, y, c, chips = _place()
        sends = [(ins[t].at[2 * px + py], outs[t].at[k], send_sems.at[3 * t + k], recv_sems.at[3 * t + k], (px, py, c))
                 for t in range(len(sums)) for k, (px, py) in enumerate(chips)]
        return sends, sends

    shapes = [jax.ShapeDtypeStruct((N_CHIPS - 1,) + s.shape[1:], s.dtype) for s in sums]
    return _Exchange(sums, shapes, {}, 3 * len(sums), plan)


def _ex_join_halves(fulls):
    def plan(ins, outs, send_sems, recv_sems):
        x, y, c, _ = _place()
        sends, arrivals = [], []
        for t, full in enumerate(fulls):
            rows = full.shape[0] // 2
            sems = (send_sems.at[t], recv_sems.at[t], (x, y, 1 - c))
            mine, other = _half(outs[t], c, rows), _half(outs[t], 1 - c, rows)
            sends.append((mine, mine, *sems))
            arrivals.append((other, other, *sems))
        return sends, arrivals

    shapes, aliases = _same(fulls)
    return _Exchange(fulls, shapes, aliases, len(fulls), plan)


class _Shifted:
    def __init__(self, sems, offset):
        self.sems, self.offset = sems, offset

    @property
    def at(self):
        return self

    def __getitem__(self, k):
        return self.sems.at[self.offset + k]


def _combine(exchanges):
    operands, out_shapes, aliases, starts = [], [], {}, []
    n_sems = 0
    for e in exchanges:
        starts.append((len(operands), len(out_shapes), n_sems))
        aliases.update({len(operands) + i: len(out_shapes) + o for i, o in e.aliases.items()})
        operands += list(e.operands)
        out_shapes += list(e.out_shapes)
        n_sems += e.n_sems

    def plan(ins, outs, send_sems, recv_sems):
        sends, arrivals = [], []
        for e, (i0, o0, s0) in zip(exchanges, starts):
            s, a = e.plan(ins[i0:i0 + len(e.operands)], outs[o0:o0 + len(e.out_shapes)],
                          _Shifted(send_sems, s0), _Shifted(recv_sems, s0))
            sends += s
            arrivals += a
        return sends, arrivals

    return _Exchange(operands, out_shapes, aliases, n_sems, plan)


class _Reduction:
    def __init__(self, grad, tag, c_arr, jc_arr):
        self.grad, self.tag, self.c_arr, self.jc_arr, self.stage = grad, tag, c_arr, jc_arr, 0

    def exchange(self):
        if self.stage == 0:
            return _ex_swap_halves([self.grad])
        if self.stage == 1:
            return _ex_scatter([self.sums])
        return _ex_join_halves([self.full])

    def advance(self, landed):
        if self.stage == 0:
            self.recv = landed
            self.sums = _chip_sum(self.grad, landed, self.c_arr, f"chip_sum_{self.tag}")
        elif self.stage == 1:
            self.full = _owner_sum(self.grad, self.recv, landed, self.jc_arr, f"owner_sum_{self.tag}")
        else:
            self.result = landed
        self.stage += 1


def _ride(reductions):
    def done(carried):
        for r, landed in zip(reductions, carried):
            r.advance(landed)

    return _combine([r.exchange() for r in reductions]), done


def _exchange_alone(ex, name):
    return _pallas(None, name=name, grid=(), in_specs=[], out_specs=[], out_shape=[], args=[], ex=ex)


def _cast_to_slot(w, chip_arr, name):
    A, B = w.shape
    ta = _tile(A, max(BF16_ROWS, (1 << 19) // B), BF16_ROWS)

    def body(j_ref, w_ref, o_ref):
        o_ref[...] = w_ref[...].astype(BF16)

    return pl.pallas_call(
        body,
        name=name,
        grid_spec=pltpu.PrefetchScalarGridSpec(
            num_scalar_prefetch=1,
            grid=(A // ta,),
            in_specs=[pl.BlockSpec((ta, B), lambda i, j_ref: (i, 0))],
            out_specs=pl.BlockSpec((None, ta, B), lambda i, j_ref: (j_ref[0], i, 0)),
        ),
        out_shape=jax.ShapeDtypeStruct((N_CHIPS, A, B), BF16),
        compiler_params=_params(("parallel",)),
    )(chip_arr, w)


def _chip_sum(grad, recv, c_arr, name):
    _, A, B = grad.shape
    hA = A // 2
    ta = _tile(hA, max(BF16_ROWS, (1 << 19) // B), BF16_ROWS)
    nh = hA // ta

    def body(c_ref, g_ref, r_ref, o_ref):
        o_ref[...] = (g_ref[...] + r_ref[...]).astype(BF16)

    return pl.pallas_call(
        body,
        name=name,
        grid_spec=pltpu.PrefetchScalarGridSpec(
            num_scalar_prefetch=1,
            grid=(N_CHIPS, nh),
            in_specs=[pl.BlockSpec((None, ta, B), lambda s, i, c_ref: (s, c_ref[0] * nh + i, 0)),
                      pl.BlockSpec((None, ta, B), lambda s, i, c_ref: (s, i, 0))],
            out_specs=pl.BlockSpec((None, ta, B), lambda s, i, c_ref: (s, i, 0)),
        ),
        out_shape=jax.ShapeDtypeStruct((N_CHIPS, hA, B), BF16),
        compiler_params=_params(("parallel", "parallel")),
    )(c_arr, grad, recv)


def _owner_sum(grad, recv, landed, jc_arr, name):
    _, A, B = grad.shape
    hA = A // 2
    ta = _tile(hA, max(BF16_ROWS, (1 << 19) // B), BF16_ROWS)
    nh = hA // ta

    def body(jc_ref, g_ref, r_ref, l0_ref, l1_ref, l2_ref, o_ref):
        total = g_ref[...] + r_ref[...]
        for ref in (l0_ref, l1_ref, l2_ref):
            total = total + ref[...].astype(F32)
        o_ref[...] = total

    def landed_spec(k):
        return pl.BlockSpec((None, ta, B), lambda i, jc_ref: (k, i, 0))

    return pl.pallas_call(
        body,
        name=name,
        grid_spec=pltpu.PrefetchScalarGridSpec(
            num_scalar_prefetch=1,
            grid=(nh,),
            in_specs=[pl.BlockSpec((None, ta, B), lambda i, jc_ref: (jc_ref[0], jc_ref[1] * nh + i, 0)),
                      pl.BlockSpec((None, ta, B), lambda i, jc_ref: (jc_ref[0], i, 0)),
                      landed_spec(0), landed_spec(1), landed_spec(2)],
            out_specs=pl.BlockSpec((ta, B), lambda i, jc_ref: (jc_ref[1] * nh + i, 0)),
        ),
        out_shape=jax.ShapeDtypeStruct((A, B), F32),
        compiler_params=_params(("parallel",)),
    )(jc_arr, grad, recv, landed, landed, landed)


def _pack(parts):
    rows = []
    for p in parts:
        flat = jnp.reshape(p.astype(F32), (-1,))
        tile = SUBLANES * LANES
        padded = -(-flat.shape[0] // tile) * tile
        rows.append(jnp.reshape(jnp.pad(flat, (0, padded - flat.shape[0])), (-1, LANES)))
    return jnp.concatenate(rows, axis=0)


def _unpack(pack, shapes):
    out, row = [], 0
    for shape in shapes:
        size = int(np.prod(shape))
        nrows = -(-size // (SUBLANES * LANES)) * SUBLANES
        out.append(jnp.reshape(jnp.reshape(pack[row:row + nrows], (-1,))[:size], shape))
        row += nrows
    return out


def _bias_tables():
    qi = np.arange(BLOCK)[:, None]
    kj = np.arange(2 * BLOCK)[None, :]
    dist = qi + BLOCK - kj
    in_window = (dist >= 0) & (dist < BLOCK)
    n = np.clip(dist, 0, None)
    max_exact = N_BUCKETS // 2
    nf = np.maximum(n, 1).astype(np.float32)
    large = max_exact + (np.log(nf / max_exact) / math.log(MAX_DISTANCE / max_exact)
                         * (N_BUCKETS - max_exact)).astype(np.int32)
    large = np.minimum(large, N_BUCKETS - 1)
    bucket = np.where(n < max_exact, n, large)
    onehot = (bucket[None] == np.arange(N_BUCKETS)[:, None, None]) & in_window[None]
    first = in_window & (kj >= BLOCK)
    return onehot.astype(np.float32), in_window, first


def kernel(x, c, w_ada, b_ada, g_ffn1, w1_ffn1, w3_ffn1, w2_ffn1, g_mix, w_in, spatial_w, spatial_b, g_v, g_q, g_k, sinks, rel_bias, w_out, g_ffn2, w1_ffn2, w3_ffn2, w2_ffn2, loss_target, m_w_ada, m_b_ada, m_g_ffn1, m_w1_ffn1, m_w3_ffn1, m_w2_ffn1, m_g_mix, m_w_in, m_spatial_w, m_spatial_b, m_g_v, m_g_q, m_g_k, m_sinks, m_rel_bias, m_w_out, m_g_ffn2, m_w1_ffn2, m_w3_ffn2, m_w2_ffn2, v_w_ada, v_b_ada, v_g_ffn1, v_w1_ffn1, v_w3_ffn1, v_w2_ffn1, v_g_mix, v_w_in, v_spatial_w, v_spatial_b, v_g_v, v_g_q, v_g_k, v_sinks, v_rel_bias, v_w_out, v_g_ffn2, v_w1_ffn2, v_w3_ffn2, v_w2_ffn2):
    ax, ay, ac = lax.axis_index("x"), lax.axis_index("y"), lax.axis_index("c")
    chip = 2 * ax + ay
    dev = 2 * chip + ac
    xs = x[0]
    tgt = loss_target[0]
    S, D = xs.shape
    F = N_CHIPS * w1_ffn1.shape[2]
    mod_cols = w_ada.shape[2]

    c_all = _allgather_small(jnp.pad(c, ((0, SUBLANES - 1), (0, 0))), "gather_c")
    c_all = jnp.pad(c_all[::SUBLANES], ((0, BF16_ROWS - N_DEV), (0, 0)))
    b_sh = lax.dynamic_slice(b_ada, (0, chip * mod_cols), (1, mod_cols))
    mod_part, c_act = _mod_partial(c_all, w_ada[0], b_sh, "mod_partial")
    mod_all = _allgather_small(mod_part[:N_DEV], "gather_mod")
    mod_all = jnp.reshape(mod_all, (N_CHIPS, 2, N_DEV, mod_cols))[:, 0]
    mod = jnp.reshape(lax.dynamic_index_in_dim(mod_all, dev, axis=1, keepdims=False), (1, N_MOD * D))
    sh1, sc1, gt1, sh2, sc2, gt2, sh3, sc3, gt3 = [mod[:, i * D:(i + 1) * D] for i in range(N_MOD)]

    def cols_to_natural(w4):
        return jnp.reshape(jnp.transpose(w4, (1, 0, 2)), (w4.shape[1], -1))

    chip_arr = jnp.reshape(chip, (1,)).astype(jnp.int32)
    c_arr = jnp.reshape(ac, (1,)).astype(jnp.int32)
    jc_arr = jnp.stack([chip, ac]).astype(jnp.int32)
    cast = lambda w, nm: _cast_to_slot(w[0], chip_arr, f"cast_{nm}")
    ffn1_bufs = [cast(w1_ffn1, "w1_ffn1"), cast(w3_ffn1, "w3_ffn1"), cast(w2_ffn1, "w2_ffn1")]
    mixer_bufs = [cast(w_in, "w_in"), cast(w_out, "w_out")]
    ffn2_bufs = [cast(w1_ffn2, "w1_ffn2"), cast(w3_ffn2, "w3_ffn2"), cast(w2_ffn2, "w2_ffn2")]
    ffn1_bufs = _exchange_alone(_ex_gather_ici(ffn1_bufs), "gather_ffn1_ici")
    ffn1_bufs = _exchange_alone(_ex_gather_d2d(ffn1_bufs), "gather_ffn1_d2d")
    w1a, w3a, w2a = cols_to_natural(ffn1_bufs[0]), cols_to_natural(ffn1_bufs[1]), jnp.reshape(ffn1_bufs[2], (F, D))

    onehot_np, in_window_np, first_np = _bias_tables()
    onehot = jnp.asarray(onehot_np)
    bias = jnp.einsum("bij,bh->hij", onehot, rel_bias, precision=lax.Precision.HIGHEST)
    biasm = jnp.stack([jnp.where(jnp.asarray(first_np)[None], bias, NEG),
                       jnp.where(jnp.asarray(in_window_np)[None], bias, NEG)])
    causal = jnp.asarray(np.tril(np.ones((BLOCK, BLOCK), dtype=bool)))
    wm = jnp.where(causal[None], spatial_w[0], 0.0).astype(BF16)
    wm_t = jnp.transpose(wm, (0, 2, 1))
    sb_t = jnp.transpose(spatial_b[0])
    sink_vec = sinks[0]

    res = _ffn_fwd(xs, g_ffn1, sh1, sc1, gt1, w1a, w3a, w2a, None, "ffn1_fwd", ex=_ex_gather_ici(mixer_bufs + ffn2_bufs))
    (x1, h1, a1, b1, f1), mixer_bufs, ffn2_bufs = res[:5], res[5:7], res[7:]
    h2, *mixer_bufs = _norm_mod(x1, g_mix, sh2, sc2, "mixer_norm", ex=_ex_gather_d2d(mixer_bufs))
    win, wout = cols_to_natural(mixer_bufs[0]), jnp.reshape(mixer_bufs[1], (-1, D))
    z, *ffn2_bufs = _matmul(h2, win, "nn", F32, 512, _tile(IN_COLS, 1664), D, "mixer_in", ex=_ex_gather_d2d(ffn2_bufs))
    w1b, w3b, w2b = cols_to_natural(ffn2_bufs[0]), cols_to_natural(ffn2_bufs[1]), jnp.reshape(ffn2_bufs[2], (F, D))
    mix = _mixer_fwd(z, wm, sb_t, g_v[0], g_q, g_k, sink_vec, biasm, "mixer_fwd")
    x2, ymix = _mixer_out(mix, wout, x1, gt2, "mixer_out")
    g3, df3, h3, a3, b3, dgt3, loss_sum = _ffn_fwd(x2, g_ffn2, sh3, sc3, gt3, w1b, w3b, w2b, tgt, "ffn2_fwd_loss")
    loss = lax.psum(loss_sum[0, 0] * (0.5 / D), ("x", "y", "c"))

    tk = _tile(S, 2048)

    def ffn_weight_grads(h, da, db, s, df, tag, riding):
        ex, done = _ride(riding) if riding else (None, None)
        gw1 = _matmul(h, da, "tn", F32, 1024, F // N_CHIPS, tk, f"grad_w1_{tag}", shard_major=True, ex=ex)
        if riding:
            done(gw1[1:])
            gw1 = gw1[0]
        r1 = _Reduction(gw1, f"w1_{tag}", c_arr, jc_arr)
        ex, done = _ride([r1])
        gw3, *carried = _matmul(h, db, "tn", F32, 1024, F // N_CHIPS, tk, f"grad_w3_{tag}", shard_major=True, ex=ex)
        done(carried)
        r3 = _Reduction(gw3, f"w3_{tag}", c_arr, jc_arr)
        ex, done = _ride([r1, r3])
        gw2, *carried = _matmul(s, df, "tn", F32, _tile(F, 1408), 1024, tk, f"grad_w2_{tag}", ex=ex)
        done(carried)
        r2 = _Reduction(jnp.reshape(gw2, (N_CHIPS, F // N_CHIPS, D)), f"w2_{tag}", c_arr, jc_arr)
        return r1, r3, r2

    da3, db3, s3, dh3 = _ffn_bwd(df3, a3, b3, w1b, w3b, w2b, "ffn2_bwd")
    r21, r23, r22 = ffn_weight_grads(h3, da3, db3, s3, df3, "ffn2", [])
    ex, done = _ride([r21, r23, r22])
    res = _norm_bwd(dh3, x2, g3, g_ffn2, sc3, (ymix, gt2, 1.0), "ffn2_norm_bwd", ex=ex)
    g2, dsh3, dsc3, dgn3, dy, dgt2 = res[:6]
    done(res[6:])

    dmix = _matmul(dy, wout, "nt", BF16, 512, 1024, D, "mixer_out_bwd")
    gwout_full = _matmul(mix, dy, "tn", F32, 1024, 1024, tk, "grad_w_out")
    ex, done = _ride([r23, r22])
    res = _mixer_bwd(z, dmix, wm, wm_t, sb_t, g_v[0], g_q, g_k, sink_vec, biasm, "mixer_bwd", ex=ex)
    dz_main, dz_kv, dwm, dsb_t, dgv, dgq, dgk, dsk, dst = res[:9]
    done(res[9:])
    dz = jnp.concatenate([dz_main, dz_kv], axis=1)
    ex, done = _ride([r22])
    dh2, *carried = _matmul(dz, win, "nt", F32, 512, 1024, _tile(IN_COLS, 1664), "mixer_in_bwd", ex=ex)
    done(carried)
    gwin_full = _matmul(h2, dz, "tn", F32, 1024, _tile(IN_COLS, 1664), tk, "grad_w_in")
    drel = _bucket_sum(jnp.reshape(dst, (B_HEADS, -1)), jnp.reshape(onehot, (N_BUCKETS, -1)), "bucket_sum")
    rm_in = _Reduction(jnp.transpose(jnp.reshape(gwin_full, (D, N_CHIPS, -1)), (1, 0, 2)), "w_in", c_arr, jc_arr)
    rm_out = _Reduction(jnp.reshape(gwout_full, (N_CHIPS, -1, D)), "w_out", c_arr, jc_arr)
    ex, done = _ride([rm_in, rm_out])
    res = _norm_bwd(dh2, x1, g2, g_mix, sc2, (f1, gt1, 0.5), "mixer_norm_bwd", ex=ex)
    g1, dsh2, dsc2, dgn2, df1, dgt1 = res[:6]
    done(res[6:])

    ex, done = _ride([rm_in, rm_out])
    res = _ffn_bwd(df1, a1, b1, w1a, w3a, w2a, "ffn1_bwd", ex=ex)
    da1, db1, s1, dh1 = res[:4]
    done(res[4:])
    r11, r13, r12 = ffn_weight_grads(h1, da1, db1, s1, df1, "ffn1", [rm_in, rm_out])
    ex, done = _ride([r11, r13, r12])
    res = _norm_bwd(dh1, xs, g1, g_ffn1, sc1, None, "ffn1_norm_bwd", ex=ex)
    grad_x, dsh1, dsc1, dgn1 = res[:4]
    done(res[4:])

    dmod = jnp.concatenate([dsh1, dsc1, dgt1, dsh2, dsc2, dgt2, dsh3, dsc3, dgt3], axis=1)
    small_w = [b_ada, g_ffn1, g_mix, g_ffn2, spatial_w, spatial_b, g_v, g_q, g_k, sinks, rel_bias]
    small_m = [m_b_ada, m_g_ffn1, m_g_mix, m_g_ffn2, m_spatial_w, m_spatial_b, m_g_v, m_g_q, m_g_k, m_sinks, m_rel_bias]
    small_v = [v_b_ada, v_g_ffn1, v_g_mix, v_g_ffn2, v_spatial_w, v_spatial_b, v_g_v, v_g_q, v_g_k, v_sinks, v_rel_bias]
    small_g = [dmod, dgn1, dgn2, dgn3, jnp.where(causal[None], dwm, 0.0), jnp.transpose(dsb_t), dgv, dgq, dgk, dsk, drel]
    shapes = [w.shape for w in small_w]
    gpack = _pack(small_g)
    rows = gpack.shape[0]
    gall = jnp.reshape(_allgather_small(gpack, "gather_small"), (N_DEV, rows, LANES))
    sg, sd, sm, sv = _small_update(gall, _pack(small_w), _pack(small_m), _pack(small_v), "small_update")
    sg, sd, sm, sv = [_unpack(p, shapes) for p in (sg, sd, sm, sv)]

    mod_rows = -(-N_MOD * D // (SUBLANES * LANES)) * SUBLANES
    dmod_all = jnp.reshape(gall[:, :mod_rows], (N_DEV, -1))[:, :N_MOD * D]
    dmod_sh = lax.dynamic_slice(dmod_all, (0, chip * mod_cols), (N_DEV, mod_cols))
    dmod_sh = jnp.pad(dmod_sh, ((0, BF16_ROWS - N_DEV), (0, 0))).astype(BF16)
    g_wada = _matmul(c_act, dmod_sh, "tn", F32, 1024, _tile(mod_cols, 512), BF16_ROWS, "grad_w_ada")
    ex, done = _ride([r13, r12])
    res = _adamw(w_ada[0], g_wada, m_w_ada[0], v_w_ada[0], "adamw_w_ada", ex=ex)
    d_wada, nm_wada, nv_wada = res[:3]
    done(res[3:])

    ex, done = _ride([r12])
    done(_exchange_alone(ex, "join_halves_w2_ffn1"))
    gf1 = [r11.result, r13.result, r12.result]
    gf2 = [r21.result, r23.result, r22.result]
    gmx = [rm_in.result, rm_out.result]
    big = {}
    for nm, w, g, m, v in [
        ("w1_ffn1", w1_ffn1, gf1[0], m_w1_ffn1, v_w1_ffn1), ("w3_ffn1", w3_ffn1, gf1[1], m_w3_ffn1, v_w3_ffn1),
        ("w2_ffn1", w2_ffn1, gf1[2], m_w2_ffn1, v_w2_ffn1), ("w_in", w_in, gmx[0], m_w_in, v_w_in),
        ("w_out", w_out, gmx[1], m_w_out, v_w_out), ("w1_ffn2", w1_ffn2, gf2[0], m_w1_ffn2, v_w1_ffn2),
        ("w3_ffn2", w3_ffn2, gf2[1], m_w3_ffn2, v_w3_ffn2), ("w2_ffn2", w2_ffn2, gf2[2], m_w2_ffn2, v_w2_ffn2),
    ]:
        g_out, d, nm_, nv_ = _adamw(w[0], g, m[0], v[0], f"adamw_{nm}", emit_grad=True)
        big[nm] = (g_out[None], d[None], nm_[None], nv_[None])
    big["w_ada"] = (g_wada[None], d_wada[None], nm_wada[None], nv_wada[None])

    order = ["w_ada", "b_ada", "g_ffn1", "w1_ffn1", "w3_ffn1", "w2_ffn1", "g_mix", "w_in", "spatial_w", "spatial_b",
             "g_v", "g_q", "g_k", "sinks", "rel_bias", "w_out", "g_ffn2", "w1_ffn2", "w3_ffn2", "w2_ffn2"]
    small_names = ["b_ada", "g_ffn1", "g_mix", "g_ffn2", "spatial_w", "spatial_b", "g_v", "g_q", "g_k", "sinks", "rel_bias"]
    for i, nm in enumerate(small_names):
        big[nm] = (sg[i], sd[i], sm[i], sv[i])
    outs = [loss, grad_x[None]]
    for kind in range(4):
        outs += [big[nm][kind] for nm in order]
    return tuple(outs)
```

```python
import functools
import math

import jax
import jax.numpy as jnp
import numpy as np
from jax import lax
from jax.experimental import pallas as pl
from jax.experimental.pallas import tpu as pltpu

F32 = jnp.float32
BF16 = jnp.bfloat16
MESH = pl.DeviceIdType.MESH
ANY = pl.BlockSpec(memory_space=pl.ANY)

EPS = 1e-6
BLOCK = 128
A_HEADS = 8
A_DIM = 128
A_WIDTH = A_HEADS * A_DIM
B_HEADS = 16
KV_HEADS = 2
GROUP = B_HEADS // KV_HEADS
HEAD_DIM = 64
B_WIDTH = B_HEADS * HEAD_DIM
KV_WIDTH = KV_HEADS * HEAD_DIM
Q_OFF = 2 * A_WIDTH
K_OFF = Q_OFF + B_WIDTH
V_OFF = K_OFF + KV_WIDTH
IN_COLS = V_OFF + KV_WIDTH
N_BUCKETS = 32
MAX_DISTANCE = 128
N_MOD = 9
N_CHIPS = 4
N_DEV = 8
NEG = -1e30

ADAM_LR = 0.001
ADAM_B1 = 0.9
ADAM_B2 = 0.999
ADAM_EPS = 1e-08
ADAM_WD = 0.01
ADAM_STEP = 10

LANES = 128
SUBLANES = 8
BF16_ROWS = 16
VMEM_LIMIT = 56 * 1024 * 1024

INV_SQRT2 = 1.0 / math.sqrt(2.0)
INV_SQRT_2PI = 1.0 / math.sqrt(2.0 * math.pi)


def _tile(n, pref, mult=LANES):
    t = (min(pref, n) // mult) * mult
    while t >= mult:
        if n % t == 0:
            return t
        t -= mult
    return n


def _params(sem):
    return pltpu.CompilerParams(dimension_semantics=sem, vmem_limit_bytes=VMEM_LIMIT)


class _Exchange:
    def __init__(self, operands, out_shapes, aliases, n_sems, plan):
        self.operands, self.out_shapes, self.aliases, self.n_sems, self.plan = operands, out_shapes, aliases, n_sems, plan


def _pallas(body, *, name, grid, in_specs, out_specs, out_shape, args, scratch_shapes=(), semantics=None, ex=None):
    if ex is None:
        return pl.pallas_call(body, name=name, grid=grid, in_specs=in_specs, out_specs=out_specs, out_shape=out_shape,
                              scratch_shapes=list(scratch_shapes), compiler_params=_params(semantics))(*args)
    n_in, n_out, n_scr = len(in_specs), len(out_specs), len(scratch_shapes)
    e_in, e_out = len(ex.operands), len(ex.out_shapes)

    def wrapped(*refs):
        ins, refs = refs[:n_in], refs[n_in:]
        ex_ins, refs = refs[:e_in], refs[e_in:]
        outs, refs = refs[:n_out], refs[n_out:]
        ex_outs, refs = refs[:e_out], refs[e_out:]
        scratch, (send_sems, recv_sems) = refs[:n_scr], refs[n_scr:]
        first, last = True, True
        for d, size in enumerate(grid):
            first = jnp.logical_and(first, pl.program_id(d) == 0)
            last = jnp.logical_and(last, pl.program_id(d) == size - 1)

        def start():
            sends, _ = ex.plan(ex_ins, ex_outs, send_sems, recv_sems)
            for cp in sends:
                _remote(*cp).start()

        def finish():
            sends, arrivals = ex.plan(ex_ins, ex_outs, send_sems, recv_sems)
            for cp in arrivals:
                _remote(*cp).wait_recv()
            for cp in sends:
                _remote(*cp).wait_send()

        if grid:
            pl.when(first)(start)
        else:
            start()
        if body is not None:
            body(*ins, *outs, *scratch)
        if grid:
            pl.when(last)(finish)
        else:
            finish()

    kwargs = dict(grid=grid) if grid else {}
    return pl.pallas_call(
        wrapped,
        name=name,
        in_specs=list(in_specs) + [ANY] * e_in,
        out_specs=list(out_specs) + [ANY] * e_out,
        out_shape=list(out_shape) + list(ex.out_shapes),
        input_output_aliases={n_in + i: n_out + o for i, o in ex.aliases.items()},
        scratch_shapes=list(scratch_shapes) + [pltpu.SemaphoreType.DMA((ex.n_sems,)), pltpu.SemaphoreType.DMA((ex.n_sems,))],
        compiler_params=_params(("arbitrary",) * len(grid) if grid else None),
        **kwargs,
    )(*args, *ex.operands)


def _dot(a, b, dims=(((1,), (0,)), ((), ()))):
    return lax.dot_general(a, b, dims, preferred_element_type=F32)


NN = (((1,), (0,)), ((), ()))
NT = (((1,), (1,)), ((), ()))
TN = (((0,), (0,)), ((), ()))


def _sigmoid(x):
    return 1.0 / (1.0 + jnp.exp(-x))


def _gelu_and_grad(x):
    cdf = 0.5 * (1.0 + lax.erf(x * INV_SQRT2))
    pdf = jnp.exp(-0.5 * x * x) * INV_SQRT_2PI
    return x * cdf, cdf + x * pdf


def _gelu(x):
    return x * (0.5 * (1.0 + lax.erf(x * INV_SQRT2)))


def _rms(x):
    r = lax.rsqrt(jnp.mean(x * x, axis=-1, keepdims=True) + EPS)
    return x * r, r


ROW_CHUNK = 64


def _for_rows(tm, fn):
    rc = min(ROW_CHUNK, tm)

    def step(r, carry):
        fn(pl.ds(pl.multiple_of(r * rc, rc), rc))
        return carry

    lax.fori_loop(0, tm // rc, step, 0)


def _rms_bwd(dy, xhat, r):
    return r * (dy - xhat * jnp.mean(dy * xhat, axis=-1, keepdims=True))


def _matmul(a, b, mode, out_dtype, tm, tn, tk, name, shard_major=False, ex=None):
    if mode == "nn":
        (M, K), N = a.shape, b.shape[1]
    elif mode == "nt":
        (M, K), N = a.shape, b.shape[0]
    else:
        (K, M), N = a.shape, b.shape[1]
    tm, tn, tk = min(tm, M), min(tn, N), min(tk, K)
    assert M % tm == 0 and N % tn == 0 and K % tk == 0, (name, M, N, K, tm, tn, tk)
    nk = K // tk
    dims = {"nn": NN, "nt": NT, "tn": TN}[mode]
    a_spec = pl.BlockSpec((tk, tm), lambda i, j, k: (k, i)) if mode == "tn" else pl.BlockSpec((tm, tk), lambda i, j, k: (i, k))
    b_spec = pl.BlockSpec((tn, tk), lambda i, j, k: (j, k)) if mode == "nt" else pl.BlockSpec((tk, tn), lambda i, j, k: (k, j))
    if shard_major:
        assert tn * N_CHIPS == N
        out_shape = jax.ShapeDtypeStruct((N_CHIPS, M, tn), out_dtype)
        o_spec = pl.BlockSpec((None, tm, tn), lambda i, j, k: (j, i, 0))
    else:
        out_shape = jax.ShapeDtypeStruct((M, N), out_dtype)
        o_spec = pl.BlockSpec((tm, tn), lambda i, j, k: (i, j))

    direct = nk == 1 or out_dtype == F32

    def body(a_ref, b_ref, o_ref, *scratch):
        k = pl.program_id(2)
        if nk == 1:
            o_ref[...] = _dot(a_ref[...], b_ref[...], dims).astype(o_ref.dtype)
            return
        acc_ref = o_ref if direct else scratch[0]

        @pl.when(k == 0)
        def _():
            acc_ref[...] = jnp.zeros(acc_ref.shape, F32)

        acc_ref[...] += _dot(a_ref[...], b_ref[...], dims)
        if not direct:
            @pl.when(k == nk - 1)
            def _():
                o_ref[...] = acc_ref[...].astype(o_ref.dtype)

    outs = _pallas(body, name=name, grid=(M // tm, N // tn, nk), in_specs=[a_spec, b_spec], out_specs=[o_spec],
                   out_shape=[out_shape], scratch_shapes=[] if direct else [pltpu.VMEM((tm, tn), F32)],
                   semantics=("parallel", "parallel", "arbitrary"), args=[a, b], ex=ex)
    return outs[0] if ex is None else outs


def _mod_partial(c_all, w_ada, b_sh, name):
    R, D = c_all.shape
    N = w_ada.shape[1]
    tn = _tile(N, 512)

    def body(c_ref, w_ref, b_ref, o_ref, ca_ref):
        cv = c_ref[...]
        ca = (cv * _sigmoid(cv)).astype(BF16)
        ca_ref[...] = ca
        o_ref[...] = _dot(ca, w_ref[...].astype(BF16)) + b_ref[...]

    return pl.pallas_call(
        body,
        name=name,
        grid=(N // tn,),
        in_specs=[
            pl.BlockSpec((R, D), lambda j: (0, 0)),
            pl.BlockSpec((D, tn), lambda j: (0, j)),
            pl.BlockSpec((1, tn), lambda j: (0, j)),
        ],
        out_specs=[pl.BlockSpec((R, tn), lambda j: (0, j)), pl.BlockSpec((R, D), lambda j: (0, 0))],
        out_shape=[jax.ShapeDtypeStruct((R, N), F32), jax.ShapeDtypeStruct((R, D), BF16)],
        compiler_params=_params(("arbitrary",)),
    )(c_all, w_ada, b_sh)


def _ffn_fwd(x, g, sh, sc, gt, w1, w3, w2, tgt, name, ex=None):
    S, D = x.shape
    F = w1.shape[1]
    tm, tf = _tile(S, 512), _tile(F, 512)
    ni, nj = S // tm, F // tf
    with_loss = tgt is not None

    def body(*refs):
        if with_loss:
            (x_ref, g_ref, sh_ref, sc_ref, gt_ref, w1_ref, w3_ref, w2_ref, tgt_ref,
             gout_ref, df_ref, h_ref, a_ref, b_ref, dgt_ref, loss_ref, hs_ref, acc_ref) = refs
        else:
            (x_ref, g_ref, sh_ref, sc_ref, gt_ref, w1_ref, w3_ref, w2_ref,
             xo_ref, h_ref, a_ref, b_ref, f_ref, hs_ref, acc_ref) = refs
        i, j = pl.program_id(0), pl.program_id(1)

        @pl.when(j == 0)
        def _():
            def prologue(rows):
                xhat, _ = _rms(x_ref[rows, :])
                hb = ((xhat * g_ref[...]) * (1.0 + sc_ref[...]) + sh_ref[...]).astype(BF16)
                hs_ref[rows, :] = hb
                h_ref[rows, :] = hb

            _for_rows(tm, prologue)

        hb = hs_ref[...]
        av = _dot(hb, w1_ref[...])
        bv = _dot(hb, w3_ref[...])
        a_ref[...] = av.astype(BF16)
        b_ref[...] = bv.astype(BF16)
        sv = ((av * _sigmoid(av)) * bv).astype(BF16)

        @pl.when(j == 0)
        def _():
            acc_ref[...] = jnp.zeros(acc_ref.shape, F32)

        acc_ref[...] += _dot(sv, w2_ref[...])

        @pl.when(j == nj - 1)
        def _():
            if with_loss:
                @pl.when(i == 0)
                def _():
                    dgt_ref[...] = jnp.zeros(dgt_ref.shape, F32)
                    loss_ref[...] = jnp.zeros(loss_ref.shape, F32)

            def epilogue(rows):
                fv = acc_ref[rows, :]
                half_gate = 0.5 * gt_ref[...]
                xo = x_ref[rows, :] + half_gate * fv
                if not with_loss:
                    xo_ref[rows, :] = xo
                    f_ref[rows, :] = fv.astype(f_ref.dtype)
                    return
                err = xo - tgt_ref[rows, :]
                gout = err * (1.0 / D)
                gout_ref[rows, :] = gout
                df_ref[rows, :] = (half_gate * gout).astype(BF16)
                dgt_ref[...] += 0.5 * jnp.sum(gout * fv, axis=0, keepdims=True)
                loss_part = jnp.sum(jnp.sum(err * err, axis=1, keepdims=True), axis=0, keepdims=True)
                loss_ref[...] += jnp.broadcast_to(loss_part, loss_ref.shape)

            _for_rows(tm, epilogue)

    row = pl.BlockSpec((tm, D), lambda i, j: (i, 0))
    row_in = pl.BlockSpec((tm, D), lambda i, j: (i, 0), pipeline_mode=pl.Buffered(1))
    vec = pl.BlockSpec((1, D), lambda i, j: (0, 0))
    col = pl.BlockSpec((tm, tf), lambda i, j: (i, j))
    in_specs = [row_in, vec, vec, vec, vec,
                pl.BlockSpec((D, tf), lambda i, j: (0, j)),
                pl.BlockSpec((D, tf), lambda i, j: (0, j)),
                pl.BlockSpec((tf, D), lambda i, j: (j, 0))]
    args = [x, g, sh, sc, gt, w1, w3, w2]
    act = jax.ShapeDtypeStruct((S, F), BF16)
    if with_loss:
        in_specs.append(row_in)
        args.append(tgt)
        out_specs = [row, row, row, col, col, vec, pl.BlockSpec((1, LANES), lambda i, j: (0, 0))]
        out_shape = [jax.ShapeDtypeStruct((S, D), F32), jax.ShapeDtypeStruct((S, D), BF16),
                     jax.ShapeDtypeStruct((S, D), BF16), act, act,
                     jax.ShapeDtypeStruct((1, D), F32), jax.ShapeDtypeStruct((1, LANES), F32)]
    else:
        out_specs = [row, row, col, col, row]
        out_shape = [jax.ShapeDtypeStruct((S, D), F32), jax.ShapeDtypeStruct((S, D), BF16), act, act,
                     jax.ShapeDtypeStruct((S, D), BF16)]
    return _pallas(body, name=name, grid=(ni, nj), in_specs=in_specs, out_specs=out_specs, out_shape=out_shape,
                   scratch_shapes=[pltpu.VMEM((tm, D), BF16), pltpu.VMEM((tm, D), F32)],
                   semantics=("arbitrary", "arbitrary"), args=args, ex=ex)


def _ffn_bwd(df, a, b, w1, w3, w2, name, ex=None):
    S, D = df.shape
    F = a.shape[1]
    tm, tf = _tile(S, 512), _tile(F, 512)
    nj = F // tf

    def body(df_ref, a_ref, b_ref, w1_ref, w3_ref, w2_ref, da_ref, db_ref, s_ref, dh_ref):
        j = pl.program_id(1)
        ds = _dot(df_ref[...], w2_ref[...], NT)
        av = a_ref[...].astype(F32)
        bv = b_ref[...].astype(F32)
        sig = _sigmoid(av)
        sil = av * sig
        da = ((ds * bv) * (sig * (1.0 + av * (1.0 - sig)))).astype(BF16)
        db = (ds * sil).astype(BF16)
        da_ref[...] = da
        db_ref[...] = db
        s_ref[...] = (sil * bv).astype(BF16)
        @pl.when(j == 0)
        def _():
            dh_ref[...] = jnp.zeros(dh_ref.shape, F32)

        dh_ref[...] += _dot(da, w1_ref[...], NT) + _dot(db, w3_ref[...], NT)

    row = pl.BlockSpec((tm, D), lambda i, j: (i, 0))
    col = pl.BlockSpec((tm, tf), lambda i, j: (i, j))
    act = jax.ShapeDtypeStruct((S, F), BF16)
    return _pallas(body, name=name, grid=(S // tm, nj),
                   in_specs=[row, col, col,
                             pl.BlockSpec((D, tf), lambda i, j: (0, j)),
                             pl.BlockSpec((D, tf), lambda i, j: (0, j)),
                             pl.BlockSpec((tf, D), lambda i, j: (j, 0))],
                   out_specs=[col, col, col, row],
                   out_shape=[act, act, act, jax.ShapeDtypeStruct((S, D), F32)],
                   semantics=("parallel", "arbitrary"), args=[df, a, b, w1, w3, w2], ex=ex)


def _norm_mod(x, g, sh, sc, name, ex=None):
    S, D = x.shape
    tm = _tile(S, 512)

    def body(x_ref, g_ref, sh_ref, sc_ref, h_ref):
        def step(rows):
            xhat, _ = _rms(x_ref[rows, :])
            h_ref[rows, :] = ((xhat * g_ref[...]) * (1.0 + sc_ref[...]) + sh_ref[...]).astype(BF16)

        _for_rows(tm, step)

    row = pl.BlockSpec((tm, D), lambda i: (i, 0))
    vec = pl.BlockSpec((1, D), lambda i: (0, 0))
    outs = _pallas(body, name=name, grid=(S // tm,), in_specs=[row, vec, vec, vec], out_specs=[row],
                   out_shape=[jax.ShapeDtypeStruct((S, D), BF16)], semantics=("parallel",), args=[x, g, sh, sc], ex=ex)
    return outs[0] if ex is None else outs


def _norm_bwd(dh, x, gres, g, sc, prev, name, ex=None):
    S, D = x.shape
    tm = _tile(S, 256)
    has_prev = prev is not None
    coef = prev[2] if has_prev else None

    def body(*refs):
        if has_prev:
            (dh_ref, x_ref, gr_ref, g_ref, sc_ref, f_ref, gt_ref,
             go_ref, dsh_ref, dsc_ref, dg_ref, dp_ref, dgt_ref) = refs
        else:
            dh_ref, x_ref, gr_ref, g_ref, sc_ref, go_ref, dsh_ref, dsc_ref, dg_ref = refs
        sum_refs = [dsh_ref, dsc_ref, dg_ref] + ([dgt_ref] if has_prev else [])

        @pl.when(pl.program_id(0) == 0)
        def _():
            for ref in sum_refs:
                ref[...] = jnp.zeros(ref.shape, F32)

        def step(rows):
            dh = dh_ref[rows, :]
            xhat, r = _rms(x_ref[rows, :])
            gain = g_ref[...]
            scale1 = 1.0 + sc_ref[...]
            gout = gr_ref[rows, :] + _rms_bwd(dh * scale1 * gain, xhat, r)
            go_ref[rows, :] = gout
            sums = [dh, dh * (xhat * gain), dh * scale1 * xhat]
            if has_prev:
                dp_ref[rows, :] = ((coef * gt_ref[...]) * gout).astype(BF16)
                sums.append(coef * (gout * f_ref[rows, :].astype(F32)))
            for ref, v in zip(sum_refs, sums):
                ref[...] += jnp.sum(v, axis=0, keepdims=True)

        _for_rows(tm, step)

    row = pl.BlockSpec((tm, D), lambda i: (i, 0))
    vec = pl.BlockSpec((1, D), lambda i: (0, 0))
    vshape = jax.ShapeDtypeStruct((1, D), F32)
    in_specs = [row, row, row, vec, vec]
    args = [dh, x, gres, g, sc]
    out_specs = [row, vec, vec, vec]
    out_shape = [jax.ShapeDtypeStruct((S, D), F32), vshape, vshape, vshape]
    if has_prev:
        in_specs += [row, vec]
        args += [prev[0], prev[1]]
        out_specs += [row, vec]
        out_shape += [jax.ShapeDtypeStruct((S, D), BF16), vshape]
    return _pallas(body, name=name, grid=(S // tm,), in_specs=in_specs, out_specs=out_specs, out_shape=out_shape,
                   semantics=("arbitrary",), args=args, ex=ex)


PAIRS = B_HEADS // 2
PAIR_ROWS = (PAIRS // KV_HEADS) * BLOCK
BAND = 2 * BLOCK


def _stack(ref, offset, count):
    return jnp.concatenate([ref[:, offset + p * LANES:offset + (p + 1) * LANES] for p in range(count)], axis=0)


def _seg_mean(x, e_ref):
    return lax.dot_general(x, e_ref[...], NN, precision=lax.Precision.HIGH,
                           preferred_element_type=F32) * (1.0 / HEAD_DIM)


def _block_diag(x, x_rolled, left, kv_head):
    if kv_head == 0:
        top, bottom = jnp.where(left, x, 0.0), jnp.where(left, 0.0, x_rolled)
    else:
        top, bottom = jnp.where(left, x_rolled, 0.0), jnp.where(left, 0.0, x)
    return jnp.concatenate([top, bottom], axis=0).astype(BF16)


def _from_block_diag(g, left, kv_head):
    a, b = g[:BAND], g[BAND:]
    if kv_head == 0:
        return jnp.where(left, a + pltpu.roll(b, HEAD_DIM, 1), 0.0)
    return jnp.where(left, 0.0, pltpu.roll(a, HEAD_DIM, 1) + b)


def _pair_softmax(st, sk_ref, kv_head):
    out = []
    for e in range(2):
        seg = st[e * BAND:(e + 1) * BAND]
        sink = jnp.concatenate([jnp.full((1, BLOCK), sk_ref[kv_head * GROUP + 2 * p + e], F32)
                                for p in range(PAIRS // KV_HEADS)], axis=1)
        m = jnp.maximum(jnp.max(seg, axis=0, keepdims=True), sink)
        p_ = jnp.exp(seg - m)
        e_sink = jnp.exp(sink - m)
        inv = 1.0 / (jnp.sum(p_, axis=0, keepdims=True) + e_sink)
        out.append((p_ * inv, e_sink * inv))
    return out


def _lane_mean(x, ones_ref):
    return lax.dot_general(x, ones_ref[...], NN, precision=lax.Precision.HIGH, preferred_element_type=F32)


def _mixer_specs(nb, last):
    full = lambda shape: pl.BlockSpec(shape, lambda n: (0,) * len(shape))
    z_spec = pl.BlockSpec((BLOCK, IN_COLS), lambda n: (jnp.minimum(n, last), 0))
    zp_spec = pl.BlockSpec((BLOCK, 2 * KV_WIDTH), lambda n: (jnp.clip(n - 1, 0, last), K_OFF // (2 * KV_WIDTH)))
    consts = [full((A_HEADS * BLOCK, A_DIM)), full((A_HEADS * BLOCK, A_DIM)), full((1, LANES)), full((1, LANES)),
              full((LANES, LANES)), full((LANES, LANES)), pl.BlockSpec(memory_space=pltpu.SMEM),
              pl.BlockSpec((None, KV_HEADS, PAIR_ROWS, 2 * BAND), lambda n: (jnp.minimum(n, 1), 0, 0, 0))]
    return full, z_spec, zp_spec, consts


def _mixer_fwd(z, wm, sbp, gvp, gq2, gk2, seg_ones, lane_ones, sinks, biasp, name):
    S = z.shape[0]
    nb = S // BLOCK

    def body(z_ref, zp_ref, wm_ref, sbp_ref, gvp_ref, gq2_ref, gk2_ref, e_ref, l_ref, sk_ref, bias_ref, mix_ref):
        u = _gelu(_stack(z_ref, 0, A_HEADS))
        v = _gelu(_stack(z_ref, A_WIDTH, A_HEADS))
        vhat = v * lax.rsqrt(_lane_mean(v * v, l_ref) + EPS)
        vn = (vhat * gvp_ref[...]).astype(BF16)
        mixed = jnp.concatenate([_dot(wm_ref[h], vn[h * BLOCK:(h + 1) * BLOCK]) for h in range(A_HEADS)], axis=0)
        ya = (u * (mixed + sbp_ref[...])).astype(BF16)
        for h in range(A_HEADS):
            mix_ref[:, h * A_DIM:(h + 1) * A_DIM] = ya[h * BLOCK:(h + 1) * BLOCK]

        left = lax.broadcasted_iota(jnp.int32, (1, LANES), 1) < HEAD_DIM
        kv = jnp.concatenate([zp_ref[...], z_ref[:, K_OFF:K_OFF + 2 * KV_WIDTH]], axis=0)
        k2, v2 = kv[:, :KV_WIDTH], kv[:, KV_WIDTH:]
        kn2 = k2 * lax.rsqrt(_seg_mean(k2 * k2, e_ref) + EPS) * gk2_ref[...]
        kn2_r, v2_r = pltpu.roll(kn2, HEAD_DIM, 1), pltpu.roll(v2, HEAD_DIM, 1)
        qp = _stack(z_ref, Q_OFF, PAIRS)
        qn = (qp * lax.rsqrt(_seg_mean(qp * qp, e_ref) + EPS) * gq2_ref[...]).astype(BF16)
        for kh in range(KV_HEADS):
            kbd, vbd = _block_diag(kn2, kn2_r, left, kh), _block_diag(v2, v2_r, left, kh)
            st = _dot(kbd, qn[kh * PAIR_ROWS:(kh + 1) * PAIR_ROWS], NT) * (HEAD_DIM ** -0.5) + bias_ref[kh]
            wt = jnp.concatenate([w_e for w_e, _ in _pair_softmax(st, sk_ref, kh)], axis=0).astype(BF16)
            o = _dot(wt, vbd, TN).astype(BF16)
            for p in range(PAIRS // KV_HEADS):
                col = A_WIDTH + (kh * (PAIRS // KV_HEADS) + p) * LANES
                mix_ref[:, col:col + LANES] = o[p * BLOCK:(p + 1) * BLOCK]

    full, z_spec, zp_spec, consts = _mixer_specs(nb, nb - 1)
    return pl.pallas_call(
        body,
        name=name,
        grid=(nb,),
        in_specs=[z_spec, zp_spec, full((A_HEADS, BLOCK, BLOCK))] + consts,
        out_specs=pl.BlockSpec((BLOCK, A_WIDTH + B_WIDTH), lambda n: (n, 0)),
        out_shape=jax.ShapeDtypeStruct((S, A_WIDTH + B_WIDTH), BF16),
        compiler_params=_params(("parallel",)),
    )(z, z, wm, sbp, gvp, gq2, gk2, seg_ones, lane_ones, sinks, biasp)


def _mixer_bwd(z, dmix, wm, wm_t, sbp, gvp, gq2, gk2, seg_ones, lane_ones, sinks, biasp, pair_fold, name, ex=None):
    S = z.shape[0]
    nb = S // BLOCK

    def body(z_ref, zp_ref, dmix_ref, wm_ref, wmt_ref, sbp_ref, gvp_ref, gq2_ref, gk2_ref, e_ref, l_ref, sk_ref,
             bias_ref, fold_ref,
             dz_ref, dzkv_ref, dwm_ref, dsb_ref, dgv_ref, dgq_ref, dgk_ref, dsk_ref, dst_ref,
             carry_ref, tot_ref, sbacc_ref, skacc_ref, gqacc_ref, gkacc_ref):
        n = pl.program_id(0)
        left = lax.broadcasted_iota(jnp.int32, (1, LANES), 1) < HEAD_DIM

        @pl.when(n == 0)
        def _():
            for ref in (dwm_ref, dgv_ref, dst_ref, carry_ref, sbacc_ref, skacc_ref, gqacc_ref, gkacc_ref):
                ref[...] = jnp.zeros(ref.shape, ref.dtype)

        @pl.when(n < nb)
        def _():
            u, du_dz = _gelu_and_grad(_stack(z_ref, 0, A_HEADS))
            v, dv_dz = _gelu_and_grad(_stack(z_ref, A_WIDTH, A_HEADS))
            rv = lax.rsqrt(_lane_mean(v * v, l_ref) + EPS)
            vhat = v * rv
            gvp = gvp_ref[...]
            vn = (vhat * gvp).astype(BF16)
            rows = lambda a, h: a[h * BLOCK:(h + 1) * BLOCK]
            mixed = jnp.concatenate([_dot(wm_ref[h], rows(vn, h)) for h in range(A_HEADS)], axis=0) + sbp_ref[...]
            dya = _stack(dmix_ref, 0, A_HEADS).astype(F32)
            dmx = dya * u
            sbacc_ref[...] += dmx
            dmx_b = dmx.astype(BF16)
            for h in range(A_HEADS):
                dwm_ref[h] += _dot(rows(dmx_b, h), rows(vn, h), NT)
            dvn = jnp.concatenate([_dot(wmt_ref[h], rows(dmx_b, h)) for h in range(A_HEADS)], axis=0)
            dgv_ref[...] += jnp.sum(jnp.reshape(dvn * vhat, (A_HEADS, BLOCK, A_DIM)), axis=1)
            dzu = ((dya * mixed) * du_dz).astype(BF16)
            tv = dvn * gvp
            dzv = ((rv * (tv - vhat * _lane_mean(tv * vhat, l_ref))) * dv_dz).astype(BF16)
            for h in range(A_HEADS):
                dz_ref[:, h * A_DIM:(h + 1) * A_DIM] = rows(dzu, h)
                dz_ref[:, A_WIDTH + h * A_DIM:A_WIDTH + (h + 1) * A_DIM] = rows(dzv, h)

            kv = jnp.concatenate([zp_ref[...], z_ref[:, K_OFF:K_OFF + 2 * KV_WIDTH]], axis=0)
            k2, v2 = kv[:, :KV_WIDTH], kv[:, KV_WIDTH:]
            kn2 = k2 * lax.rsqrt(_seg_mean(k2 * k2, e_ref) + EPS) * gk2_ref[...]
            kn2_r, v2_r = pltpu.roll(kn2, HEAD_DIM, 1), pltpu.roll(v2, HEAD_DIM, 1)
            qp = _stack(z_ref, Q_OFF, PAIRS)
            rq = lax.rsqrt(_seg_mean(qp * qp, e_ref) + EPS)
            qhat = qp * rq
            gq2 = gq2_ref[...]
            qn = (qhat * gq2).astype(BF16)
            dop = _stack(dmix_ref, A_WIDTH, PAIRS)
            dqn_parts = []
            dk2n = jnp.zeros((BAND, KV_WIDTH), F32)
            dv2 = jnp.zeros((BAND, KV_WIDTH), F32)
            for kh in range(KV_HEADS):
                kbd, vbd = _block_diag(kn2, kn2_r, left, kh), _block_diag(v2, v2_r, left, kh)
                qg = qn[kh * PAIR_ROWS:(kh + 1) * PAIR_ROWS]
                dog = dop[kh * PAIR_ROWS:(kh + 1) * PAIR_ROWS]
                st = _dot(kbd, qg, NT) * (HEAD_DIM ** -0.5) + bias_ref[kh]
                halves = _pair_softmax(st, sk_ref, kh)
                dpt = _dot(vbd, dog, NT)
                ds_halves, t_halves = [], []
                for e, (w_e, w_sink) in enumerate(halves):
                    dp_e = dpt[e * BAND:(e + 1) * BAND]
                    delta = jnp.sum(w_e * dp_e, axis=0, keepdims=True)
                    ds_halves.append(w_e * (dp_e - delta))
                    t_halves.append(-(w_sink * delta))
                dst = jnp.concatenate(ds_halves, axis=0)
                dst_ref[kh] += dst
                skacc_ref[2 * kh:2 * kh + 2, :] += jnp.concatenate(t_halves, axis=0)
                ds_b = (dst * (HEAD_DIM ** -0.5)).astype(BF16)
                w_b = jnp.concatenate([w_e for w_e, _ in halves], axis=0).astype(BF16)
                dqn_parts.append(_dot(ds_b, kbd, TN))
                dk2n += _from_block_diag(_dot(ds_b, qg), left, kh)
                dv2 += _from_block_diag(_dot(w_b, dog), left, kh)
            dqn = jnp.concatenate(dqn_parts, axis=0)
            gqacc_ref[...] += jnp.sum(dqn * qhat, axis=0, keepdims=True)
            t = dqn * gq2
            dzq = (rq * (t - qhat * _seg_mean(t * qhat, e_ref))).astype(BF16)
            for p in range(PAIRS):
                dz_ref[:, Q_OFF + p * LANES:Q_OFF + (p + 1) * LANES] = rows(dzq, p)
            tot_ref[0] = carry_ref[0] + dk2n[:BLOCK]
            tot_ref[1] = carry_ref[1] + dv2[:BLOCK]
            carry_ref[0] = dk2n[BLOCK:]
            carry_ref[1] = dv2[BLOCK:]

        @pl.when(n == nb)
        def _():
            tot_ref[...] = carry_ref[...]

        kp = zp_ref[:, :KV_WIDTH]
        rk = lax.rsqrt(_seg_mean(kp * kp, e_ref) + EPS)
        khat = kp * rk
        dkn = tot_ref[0]
        gkacc_ref[...] += jnp.sum(dkn * khat, axis=0, keepdims=True)
        t = dkn * gk2_ref[...]
        dzkv_ref[:, :KV_WIDTH] = (rk * (t - khat * _seg_mean(t * khat, e_ref))).astype(BF16)
        dzkv_ref[:, KV_WIDTH:] = tot_ref[1].astype(BF16)

        @pl.when(n == nb)
        def _():
            dsb_ref[...] = jnp.broadcast_to(jnp.sum(sbacc_ref[...], axis=1, keepdims=True), dsb_ref.shape)
            dsk_ref[...] = lax.dot_general(skacc_ref[...], fold_ref[...], NN, precision=lax.Precision.HIGHEST,
                                           preferred_element_type=F32)
            dgq_ref[...] = gqacc_ref[...] + pltpu.roll(gqacc_ref[...], HEAD_DIM, 1)
            dgk_ref[...] = gkacc_ref[...] + pltpu.roll(gkacc_ref[...], HEAD_DIM, 1)

    last = nb - 1
    full, z_spec, zp_spec, consts = _mixer_specs(nb, last)
    return _pallas(
        body,
        name=name,
        grid=(nb + 1,),
        ex=ex,
        in_specs=[z_spec, zp_spec, pl.BlockSpec((BLOCK, A_WIDTH + B_WIDTH), lambda n: (jnp.minimum(n, last), 0)),
                  full((A_HEADS, BLOCK, BLOCK)), full((A_HEADS, BLOCK, BLOCK))] + consts + [full((PAIR_ROWS, LANES))],
        out_specs=[
            pl.BlockSpec((BLOCK, K_OFF), lambda n: (jnp.minimum(n, last), 0)),
            pl.BlockSpec((BLOCK, 2 * KV_WIDTH), lambda n: (jnp.maximum(n - 1, 0), 0)),
            full((A_HEADS, BLOCK, BLOCK)), full((A_HEADS * BLOCK, A_DIM)), full((A_HEADS, A_DIM)),
            full((1, LANES)), full((1, LANES)), full((SUBLANES, LANES)),
            full((KV_HEADS, PAIR_ROWS, 2 * BAND)),
        ],
        out_shape=[
            jax.ShapeDtypeStruct((S, K_OFF), BF16),
            jax.ShapeDtypeStruct((S, 2 * KV_WIDTH), BF16),
            jax.ShapeDtypeStruct((A_HEADS, BLOCK, BLOCK), F32),
            jax.ShapeDtypeStruct((A_HEADS * BLOCK, A_DIM), F32),
            jax.ShapeDtypeStruct((A_HEADS, A_DIM), F32),
            jax.ShapeDtypeStruct((1, LANES), F32),
            jax.ShapeDtypeStruct((1, LANES), F32),
            jax.ShapeDtypeStruct((SUBLANES, LANES), F32),
            jax.ShapeDtypeStruct((KV_HEADS, PAIR_ROWS, 2 * BAND), F32),
        ],
        scratch_shapes=[
            pltpu.VMEM((2, BLOCK, KV_WIDTH), F32),
            pltpu.VMEM((2, BLOCK, KV_WIDTH), F32),
            pltpu.VMEM((A_HEADS * BLOCK, A_DIM), F32),
            pltpu.VMEM((SUBLANES, PAIR_ROWS), F32),
            pltpu.VMEM((1, LANES), F32),
            pltpu.VMEM((1, LANES), F32),
        ],
        semantics=("arbitrary",),
        args=[z, z, dmix, wm, wm_t, sbp, gvp, gq2, gk2, seg_ones, lane_ones, sinks, biasp, pair_fold],
    )


def _mixer_out(mix, w_out, x, gt, name):
    S, D = x.shape
    K = mix.shape[1]
    tm, tn = _tile(S, 512), _tile(D, 1024)

    def body(m_ref, w_ref, x_ref, gt_ref, xo_ref, y_ref):
        y = _dot(m_ref[...], w_ref[...])
        y_ref[...] = y
        xo_ref[...] = x_ref[...] + gt_ref[...] * y

    blk = pl.BlockSpec((tm, tn), lambda j, i: (i, j))
    return pl.pallas_call(
        body,
        name=name,
        grid=(D // tn, S // tm),
        in_specs=[pl.BlockSpec((tm, K), lambda j, i: (i, 0)), pl.BlockSpec((K, tn), lambda j, i: (0, j)),
                  blk, pl.BlockSpec((1, tn), lambda j, i: (0, j))],
        out_specs=[blk, blk],
        out_shape=[jax.ShapeDtypeStruct((S, D), F32), jax.ShapeDtypeStruct((S, D), F32)],
        compiler_params=_params(("parallel", "parallel")),
    )(mix, w_out, x, gt)


def _bucket_sum(dst, onehot, name):
    def body(d_ref, o_ref, out_ref):
        out_ref[...] = lax.dot_general(o_ref[...], d_ref[...], NT, precision=lax.Precision.HIGHEST,
                                       preferred_element_type=F32)

    return pl.pallas_call(
        body,
        name=name,
        out_shape=jax.ShapeDtypeStruct((N_BUCKETS, B_HEADS), F32),
    )(dst, onehot)


def _adamw_math(w, g, m, v):
    m = ADAM_B1 * m + (1.0 - ADAM_B1) * g
    v = ADAM_B2 * v + (1.0 - ADAM_B2) * (g * g)
    m_hat = m / (1.0 - ADAM_B1 ** ADAM_STEP)
    v_hat = v / (1.0 - ADAM_B2 ** ADAM_STEP)
    delta = -ADAM_LR * (m_hat / (jnp.sqrt(v_hat) + ADAM_EPS) + ADAM_WD * w)
    return delta, m, v


def _adamw(w, g, m, v, name, ex=None, emit_grad=False):
    R, C = w.shape
    tr = _tile(R, max(SUBLANES, (1 << 19) // C), SUBLANES)

    def body(w_ref, g_ref, m_ref, v_ref, *out_refs):
        gv = g_ref[...]
        results = _adamw_math(w_ref[...], gv, m_ref[...], v_ref[...])
        for ref, val in zip(out_refs, ((gv,) if emit_grad else ()) + results):
            ref[...] = val

    blk = pl.BlockSpec((tr, C), lambda i: (i, 0))
    shape = jax.ShapeDtypeStruct((R, C), F32)
    n_out = 4 if emit_grad else 3
    return _pallas(body, name=name, grid=(R // tr,), in_specs=[blk] * 4, out_specs=[blk] * n_out,
                   out_shape=[shape] * n_out, semantics=("parallel",), args=[w, g, m, v], ex=ex)


def _small_update(parts, w, m, v, name):
    R = w.shape[0]

    def body(p_ref, w_ref, m_ref, v_ref, g_ref, d_ref, mo_ref, vo_ref):
        g = p_ref[0]
        for dev in range(1, N_DEV):
            g = g + p_ref[dev]
        g_ref[...] = g
        d, mn, vn = _adamw_math(w_ref[...], g, m_ref[...], v_ref[...])
        d_ref[...] = d
        mo_ref[...] = mn
        vo_ref[...] = vn

    shape = jax.ShapeDtypeStruct((R, LANES), F32)
    return pl.pallas_call(
        body,
        name=name,
        out_shape=[shape] * 4,
        compiler_params=pltpu.CompilerParams(vmem_limit_bytes=VMEM_LIMIT),
    )(parts, w, m, v)


def _place():
    x, y, c = lax.axis_index("x"), lax.axis_index("y"), lax.axis_index("c")
    chips = [(1 - x, y), (x, 1 - y), (1 - x, 1 - y)]
    return x, y, c, chips


def _remote(src, dst, send_sem, recv_sem, to):
    return pltpu.make_async_remote_copy(src_ref=src, dst_ref=dst, send_sem=send_sem, recv_sem=recv_sem,
                                        device_id=to, device_id_type=MESH)


def _allgather_small(block, name):
    m_per, n = block.shape

    def body(x_ref, out_ref, send_sems, recv_sems, local_sem):
        x, y, c, chips = _place()
        me, sibling = (x, y, c), (x, y, 1 - c)

        def rows(px, py, pc):
            return out_ref.at[pl.ds((4 * px + 2 * py + pc) * m_per, m_per), :]

        def copy(k, blk, to, src=None):
            return _remote(rows(*blk) if src is None else src, rows(*blk), send_sems.at[k], recv_sems.at[k], to)

        mine = pltpu.make_async_copy(x_ref, rows(*me), local_sem)
        mine.start()
        first = [copy(0, me, sibling, src=x_ref)]
        first += [copy(1 + j, me, (*chip, c), src=x_ref) for j, chip in enumerate(chips)]
        for cp in first:
            cp.start()
        passed = [copy(4 + j, (*chip, c), sibling) for j, chip in enumerate(chips)]
        for j, chip in enumerate(chips):
            copy(1 + j, (*chip, c), me).wait_recv()
            passed[j].start()
        copy(0, sibling, me).wait_recv()
        for j, chip in enumerate(chips):
            copy(4 + j, (*chip, 1 - c), me).wait_recv()
        for cp in first + passed:
            cp.wait_send()
        mine.wait()

    return pl.pallas_call(
        body,
        name=name,
        out_shape=jax.ShapeDtypeStruct((N_DEV * m_per, n), block.dtype),
        in_specs=[pl.BlockSpec(memory_space=pltpu.VMEM)],
        out_specs=pl.BlockSpec(memory_space=pltpu.VMEM),
        scratch_shapes=[pltpu.SemaphoreType.DMA((7,)), pltpu.SemaphoreType.DMA((7,)), pltpu.SemaphoreType.DMA],
        compiler_params=pltpu.CompilerParams(vmem_limit_bytes=VMEM_LIMIT),
    )(block)


def _half(ref, c, rows):
    start = pl.multiple_of(c * rows, BF16_ROWS)
    if len(ref.shape) == 2:
        return ref.at[pl.ds(start, rows), :]
    return ref.at[:, pl.ds(start, rows), :]


def _same(arrays):
    return [jax.ShapeDtypeStruct(a.shape, a.dtype) for a in arrays], {t: t for t in range(len(arrays))}


def _ex_gather_ici(bufs):
    def plan(ins, outs, send_sems, recv_sems):
        x, y, c, chips = _place()
        sends, arrivals = [], []
        for t, buf in enumerate(bufs):
            rows = buf.shape[1] // 2
            mine = _half(outs[t].at[2 * x + y], c, rows)
            for k, (px, py) in enumerate(chips):
                sems = (send_sems.at[3 * t + k], recv_sems.at[3 * t + k], (px, py, c))
                landed = _half(outs[t].at[2 * px + py], c, rows)
                sends.append((mine, mine, *sems))
                arrivals.append((landed, landed, *sems))
        return sends, arrivals

    shapes, aliases = _same(bufs)
    return _Exchange(bufs, shapes, aliases, 3 * len(bufs), plan)


def _ex_gather_d2d(bufs):
    def plan(ins, outs, send_sems, recv_sems):
        x, y, c, chips = _place()
        sends, arrivals = [], []
        for t, buf in enumerate(bufs):
            rows = buf.shape[1] // 2
            for k, (px, py) in enumerate(chips):
                sems = (send_sems.at[3 * t + k], recv_sems.at[3 * t + k], (x, y, 1 - c))
                landed = _half(outs[t].at[2 * px + py], c, rows)
                other = _half(outs[t].at[2 * px + py], 1 - c, rows)
                sends.append((landed, landed, *sems))
                arrivals.append((other, other, *sems))
        return sends, arrivals

    shapes, aliases = _same(bufs)
    return _Exchange(bufs, shapes, aliases, 3 * len(bufs), plan)


def _ex_swap_halves(grads):
    def plan(ins, outs, send_sems, recv_sems):
        x, y, c, _ = _place()
        sends = [(_half(ins[t], 1 - c, g.shape[1] // 2), outs[t], send_sems.at[t], recv_sems.at[t], (x, y, 1 - c))
                 for t, g in enumerate(grads)]
        return sends, sends

    shapes = [jax.ShapeDtypeStruct((g.shape[0], g.shape[1] // 2, g.shape[2]), g.dtype) for g in grads]
    return _Exchange(grads, shapes, {}, len(grads), plan)


def _ex_scatter(sums):
    def plan(ins, outs, send_sems, recv_sems):
        x, y, c, chips = _place()
        sends = [(ins[t].at[2 * px + py], outs[t].at[k], send_sems.at[3 * t + k], recv_sems.at[3 * t + k], (px, py, c))
                 for t in range(len(sums)) for k, (px, py) in enumerate(chips)]
        return sends, sends

    shapes = [jax.ShapeDtypeStruct((N_CHIPS - 1,) + s.shape[1:], s.dtype) for s in sums]
    return _Exchange(sums, shapes, {}, 3 * len(sums), plan)


def _ex_join_halves(fulls):
    def plan(ins, outs, send_sems, recv_sems):
        x, y, c, _ = _place()
        sends, arrivals = [], []
        for t, full in enumerate(fulls):
            rows = full.shape[0] // 2
            sems = (send_sems.at[t], recv_sems.at[t], (x, y, 1 - c))
            mine, other = _half(outs[t], c, rows), _half(outs[t], 1 - c, rows)
            sends.append((mine, mine, *sems))
            arrivals.append((other, other, *sems))
        return sends, arrivals

    shapes, aliases = _same(fulls)
    return _Exchange(fulls, shapes, aliases, len(fulls), plan)


class _Shifted:
    def __init__(self, sems, offset):
        self.sems, self.offset = sems, offset

    @property
    def at(self):
        return self

    def __getitem__(self, k):
        return self.sems.at[self.offset + k]


def _combine(exchanges):
    operands, out_shapes, aliases, starts = [], [], {}, []
    n_sems = 0
    for e in exchanges:
        starts.append((len(operands), len(out_shapes), n_sems))
        aliases.update({len(operands) + i: len(out_shapes) + o for i, o in e.aliases.items()})
        operands += list(e.operands)
        out_shapes += list(e.out_shapes)
        n_sems += e.n_sems

    def plan(ins, outs, send_sems, recv_sems):
        sends, arrivals = [], []
        for e, (i0, o0, s0) in zip(exchanges, starts):
            s, a = e.plan(ins[i0:i0 + len(e.operands)], outs[o0:o0 + len(e.out_shapes)],
                          _Shifted(send_sems, s0), _Shifted(recv_sems, s0))
            sends += s
            arrivals += a
        return sends, arrivals

    return _Exchange(operands, out_shapes, aliases, n_sems, plan)


class _Reduction:
    def __init__(self, grad, tag, c_arr, jc_arr):
        self.grad, self.tag, self.c_arr, self.jc_arr, self.stage = grad, tag, c_arr, jc_arr, 0

    def exchange(self):
        if self.stage == 0:
            return _ex_swap_halves([self.grad])
        if self.stage == 1:
            return _ex_scatter([self.sums])
        return _ex_join_halves([self.full])

    def advance(self, landed):
        if self.stage == 0:
            self.recv = landed
            self.sums = _chip_sum(self.grad, landed, self.c_arr, f"chip_sum_{self.tag}")
        elif self.stage == 1:
            self.full = _owner_sum(self.grad, self.recv, landed, self.jc_arr, f"owner_sum_{self.tag}")
        else:
            self.result = landed
        self.stage += 1


def _ride(reductions):
    def done(carried):
        for r, landed in zip(reductions, carried):
            r.advance(landed)

    return _combine([r.exchange() for r in reductions]), done


def _exchange_alone(ex, name):
    return _pallas(None, name=name, grid=(), in_specs=[], out_specs=[], out_shape=[], args=[], ex=ex)


def _cast_to_slot(w, chip_arr, name):
    A, B = w.shape
    ta = _tile(A, max(BF16_ROWS, (1 << 19) // B), BF16_ROWS)

    def body(j_ref, w_ref, o_ref):
        o_ref[...] = w_ref[...].astype(BF16)

    return pl.pallas_call(
        body,
        name=name,
        grid_spec=pltpu.PrefetchScalarGridSpec(
            num_scalar_prefetch=1,
            grid=(A // ta,),
            in_specs=[pl.BlockSpec((ta, B), lambda i, j_ref: (i, 0))],
            out_specs=pl.BlockSpec((None, ta, B), lambda i, j_ref: (j_ref[0], i, 0)),
        ),
        out_shape=jax.ShapeDtypeStruct((N_CHIPS, A, B), BF16),
        compiler_params=_params(("parallel",)),
    )(chip_arr, w)


def _chip_sum(grad, recv, c_arr, name):
    _, A, B = grad.shape
    hA = A // 2
    ta = _tile(hA, max(BF16_ROWS, (1 << 19) // B), BF16_ROWS)
    nh = hA // ta

    def body(c_ref, g_ref, r_ref, o_ref):
        o_ref[...] = (g_ref[...] + r_ref[...]).astype(BF16)

    return pl.pallas_call(
        body,
        name=name,
        grid_spec=pltpu.PrefetchScalarGridSpec(
            num_scalar_prefetch=1,
            grid=(N_CHIPS, nh),
            in_specs=[pl.BlockSpec((None, ta, B), lambda s, i, c_ref: (s, c_ref[0] * nh + i, 0)),
                      pl.BlockSpec((None, ta, B), lambda s, i, c_ref: (s, i, 0))],
            out_specs=pl.BlockSpec((None, ta, B), lambda s, i, c_ref: (s, i, 0)),
        ),
        out_shape=jax.ShapeDtypeStruct((N_CHIPS, hA, B), BF16),
        compiler_params=_params(("parallel", "parallel")),
    )(c_arr, grad, recv)


def _owner_sum(grad, recv, landed, jc_arr, name):
    _, A, B = grad.shape
    hA = A // 2
    ta = _tile(hA, max(BF16_ROWS, (1 << 19) // B), BF16_ROWS)
    nh = hA // ta

    def body(jc_ref, g_ref, r_ref, l0_ref, l1_ref, l2_ref, o_ref):
        total = g_ref[...] + r_ref[...]
        for ref in (l0_ref, l1_ref, l2_ref):
            total = total + ref[...].astype(F32)
        o_ref[...] = total

    def landed_spec(k):
        return pl.BlockSpec((None, ta, B), lambda i, jc_ref: (k, i, 0))

    return pl.pallas_call(
        body,
        name=name,
        grid_spec=pltpu.PrefetchScalarGridSpec(
            num_scalar_prefetch=1,
            grid=(nh,),
            in_specs=[pl.BlockSpec((None, ta, B), lambda i, jc_ref: (jc_ref[0], jc_ref[1] * nh + i, 0)),
                      pl.BlockSpec((None, ta, B), lambda i, jc_ref: (jc_ref[0], i, 0)),
                      landed_spec(0), landed_spec(1), landed_spec(2)],
            out_specs=pl.BlockSpec((ta, B), lambda i, jc_ref: (jc_ref[1] * nh + i, 0)),
        ),
        out_shape=jax.ShapeDtypeStruct((A, B), F32),
        compiler_params=_params(("parallel",)),
    )(jc_arr, grad, recv, landed, landed, landed)


def _pack(parts):
    rows = []
    for p in parts:
        flat = jnp.reshape(p.astype(F32), (-1,))
        tile = SUBLANES * LANES
        padded = -(-flat.shape[0] // tile) * tile
        rows.append(jnp.reshape(jnp.pad(flat, (0, padded - flat.shape[0])), (-1, LANES)))
    return jnp.concatenate(rows, axis=0)


def _unpack(pack, shapes):
    out, row = [], 0
    for shape in shapes:
        size = int(np.prod(shape))
        nrows = -(-size // (SUBLANES * LANES)) * SUBLANES
        out.append(jnp.reshape(jnp.reshape(pack[row:row + nrows], (-1,))[:size], shape))
        row += nrows
    return out


def _bias_tables():
    qi = np.arange(BLOCK)[:, None]
    kj = np.arange(2 * BLOCK)[None, :]
    dist = qi + BLOCK - kj
    in_window = (dist >= 0) & (dist < BLOCK)
    n = np.clip(dist, 0, None)
    max_exact = N_BUCKETS // 2
    nf = np.maximum(n, 1).astype(np.float32)
    large = max_exact + (np.log(nf / max_exact) / math.log(MAX_DISTANCE / max_exact)
                         * (N_BUCKETS - max_exact)).astype(np.int32)
    large = np.minimum(large, N_BUCKETS - 1)
    bucket = np.where(n < max_exact, n, large)
    onehot = (bucket[None] == np.arange(N_BUCKETS)[:, None, None]) & in_window[None]
    first = in_window & (kj >= BLOCK)
    return onehot.astype(np.float32), in_window, first


def kernel(x, c, w_ada, b_ada, g_ffn1, w1_ffn1, w3_ffn1, w2_ffn1, g_mix, w_in, spatial_w, spatial_b, g_v, g_q, g_k, sinks, rel_bias, w_out, g_ffn2, w1_ffn2, w3_ffn2, w2_ffn2, loss_target, m_w_ada, m_b_ada, m_g_ffn1, m_w1_ffn1, m_w3_ffn1, m_w2_ffn1, m_g_mix, m_w_in, m_spatial_w, m_spatial_b, m_g_v, m_g_q, m_g_k, m_sinks, m_rel_bias, m_w_out, m_g_ffn2, m_w1_ffn2, m_w3_ffn2, m_w2_ffn2, v_w_ada, v_b_ada, v_g_ffn1, v_w1_ffn1, v_w3_ffn1, v_w2_ffn1, v_g_mix, v_w_in, v_spatial_w, v_spatial_b, v_g_v, v_g_q, v_g_k, v_sinks, v_rel_bias, v_w_out, v_g_ffn2, v_w1_ffn2, v_w3_ffn2, v_w2_ffn2):
    ax, ay, ac = lax.axis_index("x"), lax.axis_index("y"), lax.axis_index("c")
    chip = 2 * ax + ay
    dev = 2 * chip + ac
    xs = x[0]
    tgt = loss_target[0]
    S, D = xs.shape
    F = N_CHIPS * w1_ffn1.shape[2]
    mod_cols = w_ada.shape[2]

    c_all = _allgather_small(jnp.pad(c, ((0, SUBLANES - 1), (0, 0))), "gather_c")
    c_all = jnp.pad(c_all[::SUBLANES], ((0, BF16_ROWS - N_DEV), (0, 0)))
    b_sh = lax.dynamic_slice(b_ada, (0, chip * mod_cols), (1, mod_cols))
    mod_part, c_act = _mod_partial(c_all, w_ada[0], b_sh, "mod_partial")
    mod_all = _allgather_small(mod_part[:N_DEV], "gather_mod")
    mod_all = jnp.reshape(mod_all, (N_CHIPS, 2, N_DEV, mod_cols))[:, 0]
    mod = jnp.reshape(lax.dynamic_index_in_dim(mod_all, dev, axis=1, keepdims=False), (1, N_MOD * D))
    sh1, sc1, gt1, sh2, sc2, gt2, sh3, sc3, gt3 = [mod[:, i * D:(i + 1) * D] for i in range(N_MOD)]

    def cols_to_natural(w4):
        return jnp.reshape(jnp.transpose(w4, (1, 0, 2)), (w4.shape[1], -1))

    chip_arr = jnp.reshape(chip, (1,)).astype(jnp.int32)
    c_arr = jnp.reshape(ac, (1,)).astype(jnp.int32)
    jc_arr = jnp.stack([chip, ac]).astype(jnp.int32)
    cast = lambda w, nm: _cast_to_slot(w[0], chip_arr, f"cast_{nm}")
    ffn1_bufs = [cast(w1_ffn1, "w1_ffn1"), cast(w3_ffn1, "w3_ffn1"), cast(w2_ffn1, "w2_ffn1")]
    mixer_bufs = [cast(w_in, "w_in"), cast(w_out, "w_out")]
    ffn2_bufs = [cast(w1_ffn2, "w1_ffn2"), cast(w3_ffn2, "w3_ffn2"), cast(w2_ffn2, "w2_ffn2")]
    ffn1_bufs = _exchange_alone(_ex_gather_ici(ffn1_bufs), "gather_ffn1_ici")
    ffn1_bufs = _exchange_alone(_ex_gather_d2d(ffn1_bufs), "gather_ffn1_d2d")
    w1a, w3a, w2a = cols_to_natural(ffn1_bufs[0]), cols_to_natural(ffn1_bufs[1]), jnp.reshape(ffn1_bufs[2], (F, D))

    onehot_np, in_window_np, first_np = _bias_tables()
    onehot = jnp.asarray(onehot_np)
    bias = jnp.einsum("bij,bh->hij", onehot, rel_bias, precision=lax.Precision.HIGHEST)
    biasm = jnp.stack([jnp.where(jnp.asarray(first_np)[None], bias, NEG),
                       jnp.where(jnp.asarray(in_window_np)[None], bias, NEG)])
    causal = jnp.asarray(np.tril(np.ones((BLOCK, BLOCK), dtype=bool)))
    wm = jnp.where(causal[None], spatial_w[0], 0.0).astype(BF16)
    wm_t = jnp.transpose(wm, (0, 2, 1))
    sink_vec = sinks[0]
    per_group = PAIRS // KV_HEADS
    sbp = jnp.broadcast_to(jnp.reshape(spatial_b[0], (A_HEADS * BLOCK, 1)), (A_HEADS * BLOCK, A_DIM))
    gvp = jnp.repeat(g_v[0], BLOCK, axis=0)
    gq2, gk2 = jnp.concatenate([g_q, g_q], axis=1), jnp.concatenate([g_k, g_k], axis=1)
    seg_ones = jnp.asarray(np.kron(np.eye(2, dtype=np.float32), np.ones((HEAD_DIM, HEAD_DIM), np.float32)))
    lane_ones = jnp.full((LANES, LANES), 1.0 / LANES, F32)
    pair_fold = jnp.asarray(np.kron(np.eye(per_group, LANES, dtype=np.float32), np.ones((BLOCK, 1), np.float32)))
    biasp = jnp.reshape(jnp.transpose(jnp.reshape(biasm, (2, KV_HEADS, per_group, 2, BLOCK, BAND)), (0, 1, 3, 5, 2, 4)),
                        (2, KV_HEADS, 2 * BAND, PAIR_ROWS))

    res = _ffn_fwd(xs, g_ffn1, sh1, sc1, gt1, w1a, w3a, w2a, None, "ffn1_fwd", ex=_ex_gather_ici(mixer_bufs + ffn2_bufs))
    (x1, h1, a1, b1, f1), mixer_bufs, ffn2_bufs = res[:5], res[5:7], res[7:]
    h2, *mixer_bufs = _norm_mod(x1, g_mix, sh2, sc2, "mixer_norm", ex=_ex_gather_d2d(mixer_bufs))
    win, wout = cols_to_natural(mixer_bufs[0]), jnp.reshape(mixer_bufs[1], (-1, D))
    z, *ffn2_bufs = _matmul(h2, win, "nn", F32, 512, _tile(IN_COLS, 1664), D, "mixer_in", ex=_ex_gather_d2d(ffn2_bufs))
    w1b, w3b, w2b = cols_to_natural(ffn2_bufs[0]), cols_to_natural(ffn2_bufs[1]), jnp.reshape(ffn2_bufs[2], (F, D))
    mix = _mixer_fwd(z, wm, sbp, gvp, gq2, gk2, seg_ones, lane_ones, sink_vec, biasp, "mixer_fwd")
    x2, ymix = _mixer_out(mix, wout, x1, gt2, "mixer_out")
    g3, df3, h3, a3, b3, dgt3, loss_sum = _ffn_fwd(x2, g_ffn2, sh3, sc3, gt3, w1b, w3b, w2b, tgt, "ffn2_fwd_loss")
    loss = lax.psum(loss_sum[0, 0] * (0.5 / D), ("x", "y", "c"))

    tk = _tile(S, 2048)

    def ffn_weight_grads(h, da, db, s, df, tag, riding):
        ex, done = _ride(riding) if riding else (None, None)
        gw1 = _matmul(h, da, "tn", F32, 1024, F // N_CHIPS, tk, f"grad_w1_{tag}", shard_major=True, ex=ex)
        if riding:
            done(gw1[1:])
            gw1 = gw1[0]
        r1 = _Reduction(gw1, f"w1_{tag}", c_arr, jc_arr)
        ex, done = _ride([r1])
        gw3, *carried = _matmul(h, db, "tn", F32, 1024, F // N_CHIPS, tk, f"grad_w3_{tag}", shard_major=True, ex=ex)
        done(carried)
        r3 = _Reduction(gw3, f"w3_{tag}", c_arr, jc_arr)
        ex, done = _ride([r1, r3])
        gw2, *carried = _matmul(s, df, "tn", F32, _tile(F, 1408), 1024, tk, f"grad_w2_{tag}", ex=ex)
        done(carried)
        r2 = _Reduction(jnp.reshape(gw2, (N_CHIPS, F // N_CHIPS, D)), f"w2_{tag}", c_arr, jc_arr)
        return r1, r3, r2

    da3, db3, s3, dh3 = _ffn_bwd(df3, a3, b3, w1b, w3b, w2b, "ffn2_bwd")
    r21, r23, r22 = ffn_weight_grads(h3, da3, db3, s3, df3, "ffn2", [])
    ex, done = _ride([r21, r23, r22])
    res = _norm_bwd(dh3, x2, g3, g_ffn2, sc3, (ymix, gt2, 1.0), "ffn2_norm_bwd", ex=ex)
    g2, dsh3, dsc3, dgn3, dy, dgt2 = res[:6]
    done(res[6:])

    dmix = _matmul(dy, wout, "nt", BF16, 512, 1024, D, "mixer_out_bwd")
    gwout_full = _matmul(mix, dy, "tn", F32, 1024, 1024, tk, "grad_w_out")
    ex, done = _ride([r23, r22])
    res = _mixer_bwd(z, dmix, wm, wm_t, sbp, gvp, gq2, gk2, seg_ones, lane_ones, sink_vec, biasp, pair_fold,
                     "mixer_bwd", ex=ex)
    dz_main, dz_kv, dwm, dsb, dgv, dgq, dgk, dsk, dst = res[:9]
    dsb = jnp.reshape(dsb[:, 0], (A_HEADS, BLOCK))
    dgq, dgk = dgq[:, :HEAD_DIM], dgk[:, :HEAD_DIM]
    dsk = jnp.reshape(jnp.transpose(jnp.reshape(dsk[:2 * KV_HEADS, :per_group], (KV_HEADS, 2, per_group)), (0, 2, 1)),
                      (1, B_HEADS))
    dst = jnp.reshape(jnp.transpose(jnp.reshape(dst, (KV_HEADS, 2, BAND, per_group, BLOCK)), (0, 3, 1, 4, 2)),
                      (B_HEADS, BLOCK * BAND))
    done(res[9:])
    dz = jnp.concatenate([dz_main, dz_kv], axis=1)
    ex, done = _ride([r22])
    dh2, *carried = _matmul(dz, win, "nt", F32, 512, 1024, _tile(IN_COLS, 1664), "mixer_in_bwd", ex=ex)
    done(carried)
    gwin_full = _matmul(h2, dz, "tn", F32, 1024, _tile(IN_COLS, 1664), tk, "grad_w_in")
    drel = _bucket_sum(dst, jnp.reshape(onehot, (N_BUCKETS, -1)), "bucket_sum")
    rm_in = _Reduction(jnp.transpose(jnp.reshape(gwin_full, (D, N_CHIPS, -1)), (1, 0, 2)), "w_in", c_arr, jc_arr)
    rm_out = _Reduction(jnp.reshape(gwout_full, (N_CHIPS, -1, D)), "w_out", c_arr, jc_arr)
    ex, done = _ride([rm_in, rm_out])
    res = _norm_bwd(dh2, x1, g2, g_mix, sc2, (f1, gt1, 0.5), "mixer_norm_bwd", ex=ex)
    g1, dsh2, dsc2, dgn2, df1, dgt1 = res[:6]
    done(res[6:])

    ex, done = _ride([rm_in, rm_out])
    res = _ffn_bwd(df1, a1, b1, w1a, w3a, w2a, "ffn1_bwd", ex=ex)
    da1, db1, s1, dh1 = res[:4]
    done(res[4:])
    r11, r13, r12 = ffn_weight_grads(h1, da1, db1, s1, df1, "ffn1", [rm_in, rm_out])
    ex, done = _ride([r11, r13, r12])
    res = _norm_bwd(dh1, xs, g1, g_ffn1, sc1, None, "ffn1_norm_bwd", ex=ex)
    grad_x, dsh1, dsc1, dgn1 = res[:4]
    done(res[4:])

    dmod = jnp.concatenate([dsh1, dsc1, dgt1, dsh2, dsc2, dgt2, dsh3, dsc3, dgt3], axis=1)
    small_w = [b_ada, g_ffn1, g_mix, g_ffn2, spatial_w, spatial_b, g_v, g_q, g_k, sinks, rel_bias]
    small_m = [m_b_ada, m_g_ffn1, m_g_mix, m_g_ffn2, m_spatial_w, m_spatial_b, m_g_v, m_g_q, m_g_k, m_sinks, m_rel_bias]
    small_v = [v_b_ada, v_g_ffn1, v_g_mix, v_g_ffn2, v_spatial_w, v_spatial_b, v_g_v, v_g_q, v_g_k, v_sinks, v_rel_bias]
    small_g = [dmod, dgn1, dgn2, dgn3, jnp.where(causal[None], dwm, 0.0), dsb, dgv, dgq, dgk, dsk, drel]
    shapes = [w.shape for w in small_w]
    gpack = _pack(small_g)
    rows = gpack.shape[0]
    gall = jnp.reshape(_allgather_small(gpack, "gather_small"), (N_DEV, rows, LANES))
    sg, sd, sm, sv = _small_update(gall, _pack(small_w), _pack(small_m), _pack(small_v), "small_update")
    sg, sd, sm, sv = [_unpack(p, shapes) for p in (sg, sd, sm, sv)]

    mod_rows = -(-N_MOD * D // (SUBLANES * LANES)) * SUBLANES
    dmod_all = jnp.reshape(gall[:, :mod_rows], (N_DEV, -1))[:, :N_MOD * D]
    dmod_sh = lax.dynamic_slice(dmod_all, (0, chip * mod_cols), (N_DEV, mod_cols))
    dmod_sh = jnp.pad(dmod_sh, ((0, BF16_ROWS - N_DEV), (0, 0))).astype(BF16)
    g_wada = _matmul(c_act, dmod_sh, "tn", F32, 1024, _tile(mod_cols, 512), BF16_ROWS, "grad_w_ada")
    ex, done = _ride([r13, r12])
    res = _adamw(w_ada[0], g_wada, m_w_ada[0], v_w_ada[0], "adamw_w_ada", ex=ex)
    d_wada, nm_wada, nv_wada = res[:3]
    done(res[3:])

    ex, done = _ride([r12])
    done(_exchange_alone(ex, "join_halves_w2_ffn1"))
    gf1 = [r11.result, r13.result, r12.result]
    gf2 = [r21.result, r23.result, r22.result]
    gmx = [rm_in.result, rm_out.result]
    big = {}
    for nm, w, g, m, v in [
        ("w1_ffn1", w1_ffn1, gf1[0], m_w1_ffn1, v_w1_ffn1), ("w3_ffn1", w3_ffn1, gf1[1], m_w3_ffn1, v_w3_ffn1),
        ("w2_ffn1", w2_ffn1, gf1[2], m_w2_ffn1, v_w2_ffn1), ("w_in", w_in, gmx[0], m_w_in, v_w_in),
        ("w_out", w_out, gmx[1], m_w_out, v_w_out), ("w1_ffn2", w1_ffn2, gf2[0], m_w1_ffn2, v_w1_ffn2),
        ("w3_ffn2", w3_ffn2, gf2[1], m_w3_ffn2, v_w3_ffn2), ("w2_ffn2", w2_ffn2, gf2[2], m_w2_ffn2, v_w2_ffn2),
    ]:
        g_out, d, nm_, nv_ = _adamw(w[0], g, m[0], v[0], f"adamw_{nm}", emit_grad=True)
        big[nm] = (g_out[None], d[None], nm_[None], nv_[None])
    big["w_ada"] = (g_wada[None], d_wada[None], nm_wada[None], nv_wada[None])

    order = ["w_ada", "b_ada", "g_ffn1", "w1_ffn1", "w3_ffn1", "w2_ffn1", "g_mix", "w_in", "spatial_w", "spatial_b",
             "g_v", "g_q", "g_k", "sinks", "rel_bias", "w_out", "g_ffn2", "w1_ffn2", "w3_ffn2", "w2_ffn2"]
    small_names = ["b_ada", "g_ffn1", "g_mix", "g_ffn2", "spatial_w", "spatial_b", "g_v", "g_q", "g_k", "sinks", "rel_bias"]
    for i, nm in enumerate(small_names):
        big[nm] = (sg[i], sd[i], sm[i], sv[i])
    outs = [loss, grad_x[None]]
    for kind in range(4):
        outs += [big[nm][kind] for nm in order]
    return tuple(outs)
```

```python
import functools
import math

import jax
import jax.numpy as jnp
import numpy as np
from jax import lax
from jax.experimental import pallas as pl
from jax.experimental.pallas import tpu as pltpu

F32 = jnp.float32
BF16 = jnp.bfloat16
MESH = pl.DeviceIdType.MESH
ANY = pl.BlockSpec(memory_space=pl.ANY)

EPS = 1e-6
BLOCK = 128
A_HEADS = 8
A_DIM = 128
A_WIDTH = A_HEADS * A_DIM
B_HEADS = 16
KV_HEADS = 2
GROUP = B_HEADS // KV_HEADS
HEAD_DIM = 64
B_WIDTH = B_HEADS * HEAD_DIM
KV_WIDTH = KV_HEADS * HEAD_DIM
Q_OFF = 2 * A_WIDTH
K_OFF = Q_OFF + B_WIDTH
V_OFF = K_OFF + KV_WIDTH
IN_COLS = V_OFF + KV_WIDTH
N_BUCKETS = 32
MAX_DISTANCE = 128
N_MOD = 9
N_CHIPS = 4
N_DEV = 8
NEG = -1e30

ADAM_LR = 0.001
ADAM_B1 = 0.9
ADAM_B2 = 0.999
ADAM_EPS = 1e-08
ADAM_WD = 0.01
ADAM_STEP = 10

LANES = 128
SUBLANES = 8
BF16_ROWS = 16
VMEM_LIMIT = 56 * 1024 * 1024

INV_SQRT2 = 1.0 / math.sqrt(2.0)
INV_SQRT_2PI = 1.0 / math.sqrt(2.0 * math.pi)


def _tile(n, pref, mult=LANES):
    t = (min(pref, n) // mult) * mult
    while t >= mult:
        if n % t == 0:
            return t
        t -= mult
    return n


def _params(sem):
    return pltpu.CompilerParams(dimension_semantics=sem, vmem_limit_bytes=VMEM_LIMIT)


class _Exchange:
    def __init__(self, operands, out_shapes, aliases, n_sems, plan):
        self.operands, self.out_shapes, self.aliases, self.n_sems, self.plan = operands, out_shapes, aliases, n_sems, plan


def _pallas(body, *, name, grid, in_specs, out_specs, out_shape, args, scratch_shapes=(), semantics=None, ex=None):
    if ex is None:
        return pl.pallas_call(body, name=name, grid=grid, in_specs=in_specs, out_specs=out_specs, out_shape=out_shape,
                              scratch_shapes=list(scratch_shapes), compiler_params=_params(semantics))(*args)
    n_in, n_out, n_scr = len(in_specs), len(out_specs), len(scratch_shapes)
    e_in, e_out = len(ex.operands), len(ex.out_shapes)

    def wrapped(*refs):
        ins, refs = refs[:n_in], refs[n_in:]
        ex_ins, refs = refs[:e_in], refs[e_in:]
        outs, refs = refs[:n_out], refs[n_out:]
        ex_outs, refs = refs[:e_out], refs[e_out:]
        scratch, (send_sems, recv_sems) = refs[:n_scr], refs[n_scr:]
        first, last = True, True
        for d, size in enumerate(grid):
            first = jnp.logical_and(first, pl.program_id(d) == 0)
            last = jnp.logical_and(last, pl.program_id(d) == size - 1)

        def start():
            sends, _ = ex.plan(ex_ins, ex_outs, send_sems, recv_sems)
            for cp in sends:
                _remote(*cp).start()

        def finish():
            sends, arrivals = ex.plan(ex_ins, ex_outs, send_sems, recv_sems)
            for cp in arrivals:
                _remote(*cp).wait_recv()
            for cp in sends:
                _remote(*cp).wait_send()

        if grid:
            pl.when(first)(start)
        else:
            start()
        if body is not None:
            body(*ins, *outs, *scratch)
        if grid:
            pl.when(last)(finish)
        else:
            finish()

    kwargs = dict(grid=grid) if grid else {}
    return pl.pallas_call(
        wrapped,
        name=name,
        in_specs=list(in_specs) + [ANY] * e_in,
        out_specs=list(out_specs) + [ANY] * e_out,
        out_shape=list(out_shape) + list(ex.out_shapes),
        input_output_aliases={n_in + i: n_out + o for i, o in ex.aliases.items()},
        scratch_shapes=list(scratch_shapes) + [pltpu.SemaphoreType.DMA((ex.n_sems,)), pltpu.SemaphoreType.DMA((ex.n_sems,))],
        compiler_params=_params(("arbitrary",) * len(grid) if grid else None),
        **kwargs,
    )(*args, *ex.operands)


def _dot(a, b, dims=(((1,), (0,)), ((), ()))):
    return lax.dot_general(a, b, dims, preferred_element_type=F32)


NN = (((1,), (0,)), ((), ()))
NT = (((1,), (1,)), ((), ()))
TN = (((0,), (0,)), ((), ()))


def _sigmoid(x):
    return 1.0 / (1.0 + jnp.exp(-x))


def _gelu_and_grad(x):
    cdf = 0.5 * (1.0 + lax.erf(x * INV_SQRT2))
    pdf = jnp.exp(-0.5 * x * x) * INV_SQRT_2PI
    return x * cdf, cdf + x * pdf


def _gelu(x):
    return x * (0.5 * (1.0 + lax.erf(x * INV_SQRT2)))


def _rms(x):
    r = lax.rsqrt(jnp.mean(x * x, axis=-1, keepdims=True) + EPS)
    return x * r, r


ROW_CHUNK = 64


def _for_rows(tm, fn):
    rc = min(ROW_CHUNK, tm)

    def step(r, carry):
        fn(pl.ds(pl.multiple_of(r * rc, rc), rc))
        return carry

    lax.fori_loop(0, tm // rc, step, 0)


def _rms_bwd(dy, xhat, r):
    return r * (dy - xhat * jnp.mean(dy * xhat, axis=-1, keepdims=True))


def _matmul(a, b, mode, out_dtype, tm, tn, tk, name, shard_major=False, ex=None):
    if mode == "nn":
        (M, K), N = a.shape, b.shape[1]
    elif mode == "nt":
        (M, K), N = a.shape, b.shape[0]
    else:
        (K, M), N = a.shape, b.shape[1]
    tm, tn, tk = min(tm, M), min(tn, N), min(tk, K)
    assert M % tm == 0 and N % tn == 0 and K % tk == 0, (name, M, N, K, tm, tn, tk)
    nk = K // tk
    dims = {"nn": NN, "nt": NT, "tn": TN}[mode]
    a_spec = pl.BlockSpec((tk, tm), lambda i, j, k: (k, i)) if mode == "tn" else pl.BlockSpec((tm, tk), lambda i, j, k: (i, k))
    b_spec = pl.BlockSpec((tn, tk), lambda i, j, k: (j, k)) if mode == "nt" else pl.BlockSpec((tk, tn), lambda i, j, k: (k, j))
    if shard_major:
        assert tn * N_CHIPS == N
        out_shape = jax.ShapeDtypeStruct((N_CHIPS, M, tn), out_dtype)
        o_spec = pl.BlockSpec((None, tm, tn), lambda i, j, k: (j, i, 0))
    else:
        out_shape = jax.ShapeDtypeStruct((M, N), out_dtype)
        o_spec = pl.BlockSpec((tm, tn), lambda i, j, k: (i, j))

    direct = nk == 1 or out_dtype == F32

    def body(a_ref, b_ref, o_ref, *scratch):
        k = pl.program_id(2)
        if nk == 1:
            o_ref[...] = _dot(a_ref[...], b_ref[...], dims).astype(o_ref.dtype)
            return
        acc_ref = o_ref if direct else scratch[0]

        @pl.when(k == 0)
        def _():
            acc_ref[...] = jnp.zeros(acc_ref.shape, F32)

        acc_ref[...] += _dot(a_ref[...], b_ref[...], dims)
        if not direct:
            @pl.when(k == nk - 1)
            def _():
                o_ref[...] = acc_ref[...].astype(o_ref.dtype)

    outs = _pallas(body, name=name, grid=(M // tm, N // tn, nk), in_specs=[a_spec, b_spec], out_specs=[o_spec],
                   out_shape=[out_shape], scratch_shapes=[] if direct else [pltpu.VMEM((tm, tn), F32)],
                   semantics=("parallel", "parallel", "arbitrary"), args=[a, b], ex=ex)
    return outs[0] if ex is None else outs


def _mod_partial(c_all, w_ada, b_sh, name):
    R, D = c_all.shape
    N = w_ada.shape[1]
    tn = _tile(N, 512)

    def body(c_ref, w_ref, b_ref, o_ref, ca_ref):
        cv = c_ref[...]
        ca = (cv * _sigmoid(cv)).astype(BF16)
        ca_ref[...] = ca
        o_ref[...] = _dot(ca, w_ref[...].astype(BF16)) + b_ref[...]

    return pl.pallas_call(
        body,
        name=name,
        grid=(N // tn,),
        in_specs=[
            pl.BlockSpec((R, D), lambda j: (0, 0)),
            pl.BlockSpec((D, tn), lambda j: (0, j)),
            pl.BlockSpec((1, tn), lambda j: (0, j)),
        ],
        out_specs=[pl.BlockSpec((R, tn), lambda j: (0, j)), pl.BlockSpec((R, D), lambda j: (0, 0))],
        out_shape=[jax.ShapeDtypeStruct((R, N), F32), jax.ShapeDtypeStruct((R, D), BF16)],
        compiler_params=_params(("arbitrary",)),
    )(c_all, w_ada, b_sh)


def _ffn_fwd(x, g, sh, sc, gt, w1, w3, w2, tgt, name, ex=None):
    S, D = x.shape
    F = w1.shape[1]
    tm, tf = _tile(S, 512), _tile(F, 512)
    ni, nj = S // tm, F // tf
    with_loss = tgt is not None

    def body(*refs):
        if with_loss:
            (x_ref, g_ref, sh_ref, sc_ref, gt_ref, w1_ref, w3_ref, w2_ref, tgt_ref,
             gout_ref, df_ref, h_ref, a_ref, b_ref, dgt_ref, loss_ref, hs_ref, acc_ref) = refs
        else:
            (x_ref, g_ref, sh_ref, sc_ref, gt_ref, w1_ref, w3_ref, w2_ref,
             xo_ref, h_ref, a_ref, b_ref, f_ref, hs_ref, acc_ref) = refs
        i, j = pl.program_id(0), pl.program_id(1)

        @pl.when(j == 0)
        def _():
            def prologue(rows):
                xhat, _ = _rms(x_ref[rows, :])
                hb = ((xhat * g_ref[...]) * (1.0 + sc_ref[...]) + sh_ref[...]).astype(BF16)
                hs_ref[rows, :] = hb
                h_ref[rows, :] = hb

            _for_rows(tm, prologue)

        hb = hs_ref[...]
        av = _dot(hb, w1_ref[...])
        bv = _dot(hb, w3_ref[...])
        a_ref[...] = av.astype(BF16)
        b_ref[...] = bv.astype(BF16)
        sv = ((av * _sigmoid(av)) * bv).astype(BF16)

        @pl.when(j == 0)
        def _():
            acc_ref[...] = jnp.zeros(acc_ref.shape, F32)

        acc_ref[...] += _dot(sv, w2_ref[...])

        @pl.when(j == nj - 1)
        def _():
            if with_loss:
                @pl.when(i == 0)
                def _():
                    dgt_ref[...] = jnp.zeros(dgt_ref.shape, F32)
                    loss_ref[...] = jnp.zeros(loss_ref.shape, F32)

            def epilogue(rows):
                fv = acc_ref[rows, :]
                half_gate = 0.5 * gt_ref[...]
                xo = x_ref[rows, :] + half_gate * fv
                if not with_loss:
                    xo_ref[rows, :] = xo
                    f_ref[rows, :] = fv.astype(f_ref.dtype)
                    return
                err = xo - tgt_ref[rows, :]
                gout = err * (1.0 / D)
                gout_ref[rows, :] = gout
                df_ref[rows, :] = (half_gate * gout).astype(BF16)
                dgt_ref[...] += 0.5 * jnp.sum(gout * fv, axis=0, keepdims=True)
                loss_part = jnp.sum(jnp.sum(err * err, axis=1, keepdims=True), axis=0, keepdims=True)
                loss_ref[...] += jnp.broadcast_to(loss_part, loss_ref.shape)

            _for_rows(tm, epilogue)

    row = pl.BlockSpec((tm, D), lambda i, j: (i, 0))
    row_in = pl.BlockSpec((tm, D), lambda i, j: (i, 0), pipeline_mode=pl.Buffered(1))
    vec = pl.BlockSpec((1, D), lambda i, j: (0, 0))
    col = pl.BlockSpec((tm, tf), lambda i, j: (i, j))
    in_specs = [row_in, vec, vec, vec, vec,
                pl.BlockSpec((D, tf), lambda i, j: (0, j)),
                pl.BlockSpec((D, tf), lambda i, j: (0, j)),
                pl.BlockSpec((tf, D), lambda i, j: (j, 0))]
    args = [x, g, sh, sc, gt, w1, w3, w2]
    act = jax.ShapeDtypeStruct((S, F), BF16)
    if with_loss:
        in_specs.append(row_in)
        args.append(tgt)
        out_specs = [row, row, row, col, col, vec, pl.BlockSpec((1, LANES), lambda i, j: (0, 0))]
        out_shape = [jax.ShapeDtypeStruct((S, D), F32), jax.ShapeDtypeStruct((S, D), BF16),
                     jax.ShapeDtypeStruct((S, D), BF16), act, act,
                     jax.ShapeDtypeStruct((1, D), F32), jax.ShapeDtypeStruct((1, LANES), F32)]
    else:
        out_specs = [row, row, col, col, row]
        out_shape = [jax.ShapeDtypeStruct((S, D), F32), jax.ShapeDtypeStruct((S, D), BF16), act, act,
                     jax.ShapeDtypeStruct((S, D), BF16)]
    return _pallas(body, name=name, grid=(ni, nj), in_specs=in_specs, out_specs=out_specs, out_shape=out_shape,
                   scratch_shapes=[pltpu.VMEM((tm, D), BF16), pltpu.VMEM((tm, D), F32)],
                   semantics=("arbitrary", "arbitrary"), args=args, ex=ex)


def _ffn_bwd(df, a, b, w1, w3, w2, name, ex=None):
    S, D = df.shape
    F = a.shape[1]
    tm, tf = _tile(S, 512), _tile(F, 512)
    nj = F // tf

    def body(df_ref, a_ref, b_ref, w1_ref, w3_ref, w2_ref, da_ref, db_ref, s_ref, dh_ref):
        j = pl.program_id(1)
        ds = _dot(df_ref[...], w2_ref[...], NT)
        av = a_ref[...].astype(F32)
        bv = b_ref[...].astype(F32)
        sig = _sigmoid(av)
        sil = av * sig
        da = ((ds * bv) * (sig * (1.0 + av * (1.0 - sig)))).astype(BF16)
        db = (ds * sil).astype(BF16)
        da_ref[...] = da
        db_ref[...] = db
        s_ref[...] = (sil * bv).astype(BF16)

        @pl.when(j == 0)
        def _():
            dh_ref[...] = jnp.zeros(dh_ref.shape, F32)

        dh_ref[...] += _dot(da, w1_ref[...], NT) + _dot(db, w3_ref[...], NT)

    row = pl.BlockSpec((tm, D), lambda i, j: (i, 0))
    col = pl.BlockSpec((tm, tf), lambda i, j: (i, j))
    act = jax.ShapeDtypeStruct((S, F), BF16)
    return _pallas(body, name=name, grid=(S // tm, nj),
                   in_specs=[row, col, col,
                             pl.BlockSpec((D, tf), lambda i, j: (0, j)),
                             pl.BlockSpec((D, tf), lambda i, j: (0, j)),
                             pl.BlockSpec((tf, D), lambda i, j: (j, 0))],
                   out_specs=[col, col, col, row],
                   out_shape=[act, act, act, jax.ShapeDtypeStruct((S, D), F32)],
                   semantics=("parallel", "arbitrary"), args=[df, a, b, w1, w3, w2], ex=ex)


def _norm_mod(x, g, sh, sc, name, ex=None):
    S, D = x.shape
    tm = _tile(S, 512)

    def body(x_ref, g_ref, sh_ref, sc_ref, h_ref):
        def step(rows):
            xhat, _ = _rms(x_ref[rows, :])
            h_ref[rows, :] = ((xhat * g_ref[...]) * (1.0 + sc_ref[...]) + sh_ref[...]).astype(BF16)

        _for_rows(tm, step)

    row = pl.BlockSpec((tm, D), lambda i: (i, 0))
    vec = pl.BlockSpec((1, D), lambda i: (0, 0))
    outs = _pallas(body, name=name, grid=(S // tm,), in_specs=[row, vec, vec, vec], out_specs=[row],
                   out_shape=[jax.ShapeDtypeStruct((S, D), BF16)], semantics=("parallel",), args=[x, g, sh, sc], ex=ex)
    return outs[0] if ex is None else outs


def _norm_bwd(dh, x, gres, g, sc, prev, name, ex=None):
    S, D = x.shape
    tm = _tile(S, 256)
    has_prev = prev is not None
    coef = prev[2] if has_prev else None

    def body(*refs):
        if has_prev:
            (dh_ref, x_ref, gr_ref, g_ref, sc_ref, f_ref, gt_ref,
             go_ref, dsh_ref, dsc_ref, dg_ref, dp_ref, dgt_ref) = refs
        else:
            dh_ref, x_ref, gr_ref, g_ref, sc_ref, go_ref, dsh_ref, dsc_ref, dg_ref = refs
        sum_refs = [dsh_ref, dsc_ref, dg_ref] + ([dgt_ref] if has_prev else [])

        @pl.when(pl.program_id(0) == 0)
        def _():
            for ref in sum_refs:
                ref[...] = jnp.zeros(ref.shape, F32)

        def step(rows):
            dh = dh_ref[rows, :]
            xhat, r = _rms(x_ref[rows, :])
            gain = g_ref[...]
            scale1 = 1.0 + sc_ref[...]
            gout = gr_ref[rows, :] + _rms_bwd(dh * scale1 * gain, xhat, r)
            go_ref[rows, :] = gout
            sums = [dh, dh * (xhat * gain), dh * scale1 * xhat]
            if has_prev:
                dp_ref[rows, :] = ((coef * gt_ref[...]) * gout).astype(BF16)
                sums.append(coef * (gout * f_ref[rows, :].astype(F32)))
            for ref, v in zip(sum_refs, sums):
                ref[...] += jnp.sum(v, axis=0, keepdims=True)

        _for_rows(tm, step)

    row = pl.BlockSpec((tm, D), lambda i: (i, 0))
    vec = pl.BlockSpec((1, D), lambda i: (0, 0))
    vshape = jax.ShapeDtypeStruct((1, D), F32)
    in_specs = [row, row, row, vec, vec]
    args = [dh, x, gres, g, sc]
    out_specs = [row, vec, vec, vec]
    out_shape = [jax.ShapeDtypeStruct((S, D), F32), vshape, vshape, vshape]
    if has_prev:
        in_specs += [row, vec]
        args += [prev[0], prev[1]]
        out_specs += [row, vec]
        out_shape += [jax.ShapeDtypeStruct((S, D), BF16), vshape]
    return _pallas(body, name=name, grid=(S // tm,), in_specs=in_specs, out_specs=out_specs, out_shape=out_shape,
                   semantics=("arbitrary",), args=args, ex=ex)


PAIRS = B_HEADS // 2
PAIR_ROWS = (PAIRS // KV_HEADS) * BLOCK
BAND = 2 * BLOCK


def _stack(ref, offset, count):
    return jnp.concatenate([ref[:, offset + p * LANES:offset + (p + 1) * LANES] for p in range(count)], axis=0)


def _seg_mean(x, e_ref):
    return lax.dot_general(x, e_ref[...], NN, precision=lax.Precision.HIGH,
                           preferred_element_type=F32) * (1.0 / HEAD_DIM)


def _block_diag(x, x_rolled, left, kv_head):
    if kv_head == 0:
        top, bottom = jnp.where(left, x, 0.0), jnp.where(left, 0.0, x_rolled)
    else:
        top, bottom = jnp.where(left, x_rolled, 0.0), jnp.where(left, 0.0, x)
    return jnp.concatenate([top, bottom], axis=0).astype(BF16)


def _from_block_diag(g, left, kv_head):
    a, b = g[:BAND], g[BAND:]
    if kv_head == 0:
        return jnp.where(left, a + pltpu.roll(b, HEAD_DIM, 1), 0.0)
    return jnp.where(left, 0.0, pltpu.roll(a, HEAD_DIM, 1) + b)


def _pair_softmax(st, sk_ref, kv_head):
    out = []
    for e in range(2):
        seg = st[e * BAND:(e + 1) * BAND]
        sink = jnp.concatenate([jnp.full((1, BLOCK), sk_ref[kv_head * GROUP + 2 * p + e], F32)
                                for p in range(PAIRS // KV_HEADS)], axis=1)
        m = jnp.maximum(jnp.max(seg, axis=0, keepdims=True), sink)
        p_ = jnp.exp(seg - m)
        e_sink = jnp.exp(sink - m)
        inv = 1.0 / (jnp.sum(p_, axis=0, keepdims=True) + e_sink)
        out.append((p_ * inv, e_sink * inv))
    return out


def _lane_mean(x, ones_ref):
    return lax.dot_general(x, ones_ref[...], NN, precision=lax.Precision.HIGH, preferred_element_type=F32)


def _mixer_specs(nb, last):
    full = lambda shape: pl.BlockSpec(shape, lambda n: (0,) * len(shape))
    z_spec = pl.BlockSpec((BLOCK, IN_COLS), lambda n: (jnp.minimum(n, last), 0))
    zp_spec = pl.BlockSpec((BLOCK, 2 * KV_WIDTH), lambda n: (jnp.clip(n - 1, 0, last), K_OFF // (2 * KV_WIDTH)))
    consts = [full((A_HEADS * BLOCK, A_DIM)), full((A_HEADS * BLOCK, A_DIM)), full((1, LANES)), full((1, LANES)),
              full((LANES, LANES)), full((LANES, LANES)), pl.BlockSpec(memory_space=pltpu.SMEM),
              pl.BlockSpec((None, KV_HEADS, PAIR_ROWS, 2 * BAND), lambda n: (jnp.minimum(n, 1), 0, 0, 0))]
    return full, z_spec, zp_spec, consts


def _mixer_fwd(z, wm, sbp, gvp, gq2, gk2, seg_ones, lane_ones, sinks, biasp, name):
    S = z.shape[0]
    nb = S // BLOCK

    def body(z_ref, zp_ref, wm_ref, sbp_ref, gvp_ref, gq2_ref, gk2_ref, e_ref, l_ref, sk_ref, bias_ref, mix_ref):
        u = _gelu(_stack(z_ref, 0, A_HEADS))
        v = _gelu(_stack(z_ref, A_WIDTH, A_HEADS))
        vhat = v * lax.rsqrt(_lane_mean(v * v, l_ref) + EPS)
        vn = (vhat * gvp_ref[...]).astype(BF16)
        mixed = jnp.concatenate([_dot(wm_ref[h], vn[h * BLOCK:(h + 1) * BLOCK]) for h in range(A_HEADS)], axis=0)
        ya = (u * (mixed + sbp_ref[...])).astype(BF16)
        for h in range(A_HEADS):
            mix_ref[:, h * A_DIM:(h + 1) * A_DIM] = ya[h * BLOCK:(h + 1) * BLOCK]

        left = lax.broadcasted_iota(jnp.int32, (1, LANES), 1) < HEAD_DIM
        kv = jnp.concatenate([zp_ref[...], z_ref[:, K_OFF:K_OFF + 2 * KV_WIDTH]], axis=0)
        k2, v2 = kv[:, :KV_WIDTH], kv[:, KV_WIDTH:]
        kn2 = k2 * lax.rsqrt(_seg_mean(k2 * k2, e_ref) + EPS) * gk2_ref[...]
        kn2_r, v2_r = pltpu.roll(kn2, HEAD_DIM, 1), pltpu.roll(v2, HEAD_DIM, 1)
        qp = _stack(z_ref, Q_OFF, PAIRS)
        qn = (qp * lax.rsqrt(_seg_mean(qp * qp, e_ref) + EPS) * gq2_ref[...]).astype(BF16)
        for kh in range(KV_HEADS):
            kbd, vbd = _block_diag(kn2, kn2_r, left, kh), _block_diag(v2, v2_r, left, kh)
            st = _dot(kbd, qn[kh * PAIR_ROWS:(kh + 1) * PAIR_ROWS], NT) * (HEAD_DIM ** -0.5) + bias_ref[kh]
            wt = jnp.concatenate([w_e for w_e, _ in _pair_softmax(st, sk_ref, kh)], axis=0).astype(BF16)
            o = _dot(wt, vbd, TN).astype(BF16)
            for p in range(PAIRS // KV_HEADS):
                col = A_WIDTH + (kh * (PAIRS // KV_HEADS) + p) * LANES
                mix_ref[:, col:col + LANES] = o[p * BLOCK:(p + 1) * BLOCK]

    full, z_spec, zp_spec, consts = _mixer_specs(nb, nb - 1)
    return pl.pallas_call(
        body,
        name=name,
        grid=(nb,),
        in_specs=[z_spec, zp_spec, full((A_HEADS, BLOCK, BLOCK))] + consts,
        out_specs=pl.BlockSpec((BLOCK, A_WIDTH + B_WIDTH), lambda n: (n, 0)),
        out_shape=jax.ShapeDtypeStruct((S, A_WIDTH + B_WIDTH), BF16),
        compiler_params=_params(("parallel",)),
    )(z, z, wm, sbp, gvp, gq2, gk2, seg_ones, lane_ones, sinks, biasp)


def _mixer_bwd(z, dmix, wm, wm_t, sbp, gvp, gq2, gk2, seg_ones, lane_ones, sinks, biasp, pair_fold, name, ex=None):
    S = z.shape[0]
    nb = S // BLOCK

    def body(z_ref, zp_ref, dmix_ref, wm_ref, wmt_ref, sbp_ref, gvp_ref, gq2_ref, gk2_ref, e_ref, l_ref, sk_ref,
             bias_ref, fold_ref,
             dz_ref, dzkv_ref, dwm_ref, dsb_ref, dgv_ref, dgq_ref, dgk_ref, dsk_ref, dst_ref,
             carry_ref, tot_ref, sbacc_ref, skacc_ref, gqacc_ref, gkacc_ref):
        n = pl.program_id(0)
        left = lax.broadcasted_iota(jnp.int32, (1, LANES), 1) < HEAD_DIM

        @pl.when(n == 0)
        def _():
            for ref in (dwm_ref, dgv_ref, dst_ref, carry_ref, sbacc_ref, skacc_ref, gqacc_ref, gkacc_ref):
                ref[...] = jnp.zeros(ref.shape, ref.dtype)

        @pl.when(n < nb)
        def _():
            u, du_dz = _gelu_and_grad(_stack(z_ref, 0, A_HEADS))
            v, dv_dz = _gelu_and_grad(_stack(z_ref, A_WIDTH, A_HEADS))
            rv = lax.rsqrt(_lane_mean(v * v, l_ref) + EPS)
            vhat = v * rv
            gvp = gvp_ref[...]
            vn = (vhat * gvp).astype(BF16)
            rows = lambda a, h: a[h * BLOCK:(h + 1) * BLOCK]
            mixed = jnp.concatenate([_dot(wm_ref[h], rows(vn, h)) for h in range(A_HEADS)], axis=0) + sbp_ref[...]
            dya = _stack(dmix_ref, 0, A_HEADS).astype(F32)
            dmx = dya * u
            sbacc_ref[...] += dmx
            dmx_b = dmx.astype(BF16)
            for h in range(A_HEADS):
                dwm_ref[h] += _dot(rows(dmx_b, h), rows(vn, h), NT)
            dvn = jnp.concatenate([_dot(wmt_ref[h], rows(dmx_b, h)) for h in range(A_HEADS)], axis=0)
            dgv_ref[...] += jnp.sum(jnp.reshape(dvn * vhat, (A_HEADS, BLOCK, A_DIM)), axis=1)
            dzu = ((dya * mixed) * du_dz).astype(BF16)
            tv = dvn * gvp
            dzv = ((rv * (tv - vhat * _lane_mean(tv * vhat, l_ref))) * dv_dz).astype(BF16)
            for h in range(A_HEADS):
                dz_ref[:, h * A_DIM:(h + 1) * A_DIM] = rows(dzu, h)
                dz_ref[:, A_WIDTH + h * A_DIM:A_WIDTH + (h + 1) * A_DIM] = rows(dzv, h)

            kv = jnp.concatenate([zp_ref[...], z_ref[:, K_OFF:K_OFF + 2 * KV_WIDTH]], axis=0)
            k2, v2 = kv[:, :KV_WIDTH], kv[:, KV_WIDTH:]
            kn2 = k2 * lax.rsqrt(_seg_mean(k2 * k2, e_ref) + EPS) * gk2_ref[...]
            kn2_r, v2_r = pltpu.roll(kn2, HEAD_DIM, 1), pltpu.roll(v2, HEAD_DIM, 1)
            qp = _stack(z_ref, Q_OFF, PAIRS)
            rq = lax.rsqrt(_seg_mean(qp * qp, e_ref) + EPS)
            qhat = qp * rq
            gq2 = gq2_ref[...]
            qn = (qhat * gq2).astype(BF16)
            dop = _stack(dmix_ref, A_WIDTH, PAIRS)
            dqn_parts = []
            dk2n = jnp.zeros((BAND, KV_WIDTH), F32)
            dv2 = jnp.zeros((BAND, KV_WIDTH), F32)
            for kh in range(KV_HEADS):
                kbd, vbd = _block_diag(kn2, kn2_r, left, kh), _block_diag(v2, v2_r, left, kh)
                qg = qn[kh * PAIR_ROWS:(kh + 1) * PAIR_ROWS]
                dog = dop[kh * PAIR_ROWS:(kh + 1) * PAIR_ROWS]
                st = _dot(kbd, qg, NT) * (HEAD_DIM ** -0.5) + bias_ref[kh]
                halves = _pair_softmax(st, sk_ref, kh)
                dpt = _dot(vbd, dog, NT)
                ds_halves, t_halves = [], []
                for e, (w_e, w_sink) in enumerate(halves):
                    dp_e = dpt[e * BAND:(e + 1) * BAND]
                    delta = jnp.sum(w_e * dp_e, axis=0, keepdims=True)
                    ds_halves.append(w_e * (dp_e - delta))
                    t_halves.append(-(w_sink * delta))
                dst = jnp.concatenate(ds_halves, axis=0)
                dst_ref[kh] += dst
                skacc_ref[2 * kh:2 * kh + 2, :] += jnp.concatenate(t_halves, axis=0)
                ds_b = (dst * (HEAD_DIM ** -0.5)).astype(BF16)
                w_b = jnp.concatenate([w_e for w_e, _ in halves], axis=0).astype(BF16)
                dqn_parts.append(_dot(ds_b, kbd, TN))
                dk2n += _from_block_diag(_dot(ds_b, qg), left, kh)
                dv2 += _from_block_diag(_dot(w_b, dog), left, kh)
            dqn = jnp.concatenate(dqn_parts, axis=0)
            gqacc_ref[...] += jnp.sum(dqn * qhat, axis=0, keepdims=True)
            t = dqn * gq2
            dzq = (rq * (t - qhat * _seg_mean(t * qhat, e_ref))).astype(BF16)
            for p in range(PAIRS):
                dz_ref[:, Q_OFF + p * LANES:Q_OFF + (p + 1) * LANES] = rows(dzq, p)
            tot_ref[0] = carry_ref[0] + dk2n[:BLOCK]
            tot_ref[1] = carry_ref[1] + dv2[:BLOCK]
            carry_ref[0] = dk2n[BLOCK:]
            carry_ref[1] = dv2[BLOCK:]

        @pl.when(n == nb)
        def _():
            tot_ref[...] = carry_ref[...]

        kp = zp_ref[:, :KV_WIDTH]
        rk = lax.rsqrt(_seg_mean(kp * kp, e_ref) + EPS)
        khat = kp * rk
        dkn = tot_ref[0]
        gkacc_ref[...] += jnp.sum(dkn * khat, axis=0, keepdims=True)
        t = dkn * gk2_ref[...]
        dzkv_ref[:, :KV_WIDTH] = (rk * (t - khat * _seg_mean(t * khat, e_ref))).astype(BF16)
        dzkv_ref[:, KV_WIDTH:] = tot_ref[1].astype(BF16)

        @pl.when(n == nb)
        def _():
            dsb_ref[...] = jnp.broadcast_to(jnp.sum(sbacc_ref[...], axis=1, keepdims=True), dsb_ref.shape)
            dsk_ref[...] = lax.dot_general(skacc_ref[...], fold_ref[...], NN, precision=lax.Precision.HIGHEST,
                                           preferred_element_type=F32)
            dgq_ref[...] = gqacc_ref[...] + pltpu.roll(gqacc_ref[...], HEAD_DIM, 1)
            dgk_ref[...] = gkacc_ref[...] + pltpu.roll(gkacc_ref[...], HEAD_DIM, 1)

    last = nb - 1
    full, z_spec, zp_spec, consts = _mixer_specs(nb, last)
    return _pallas(
        body,
        name=name,
        grid=(nb + 1,),
        ex=ex,
        in_specs=[z_spec, zp_spec, pl.BlockSpec((BLOCK, A_WIDTH + B_WIDTH), lambda n: (jnp.minimum(n, last), 0)),
                  full((A_HEADS, BLOCK, BLOCK)), full((A_HEADS, BLOCK, BLOCK))] + consts + [full((PAIR_ROWS, LANES))],
        out_specs=[
            pl.BlockSpec((BLOCK, K_OFF), lambda n: (jnp.minimum(n, last), 0)),
            pl.BlockSpec((BLOCK, 2 * KV_WIDTH), lambda n: (jnp.maximum(n - 1, 0), 0)),
            full((A_HEADS, BLOCK, BLOCK)), full((A_HEADS * BLOCK, A_DIM)), full((A_HEADS, A_DIM)),
            full((1, LANES)), full((1, LANES)), full((SUBLANES, LANES)),
            full((KV_HEADS, PAIR_ROWS, 2 * BAND)),
        ],
        out_shape=[
            jax.ShapeDtypeStruct((S, K_OFF), BF16),
            jax.ShapeDtypeStruct((S, 2 * KV_WIDTH), BF16),
            jax.ShapeDtypeStruct((A_HEADS, BLOCK, BLOCK), F32),
            jax.ShapeDtypeStruct((A_HEADS * BLOCK, A_DIM), F32),
            jax.ShapeDtypeStruct((A_HEADS, A_DIM), F32),
            jax.ShapeDtypeStruct((1, LANES), F32),
            jax.ShapeDtypeStruct((1, LANES), F32),
            jax.ShapeDtypeStruct((SUBLANES, LANES), F32),
            jax.ShapeDtypeStruct((KV_HEADS, PAIR_ROWS, 2 * BAND), F32),
        ],
        scratch_shapes=[
            pltpu.VMEM((2, BLOCK, KV_WIDTH), F32),
            pltpu.VMEM((2, BLOCK, KV_WIDTH), F32),
            pltpu.VMEM((A_HEADS * BLOCK, A_DIM), F32),
            pltpu.VMEM((SUBLANES, PAIR_ROWS), F32),
            pltpu.VMEM((1, LANES), F32),
            pltpu.VMEM((1, LANES), F32),
        ],
        semantics=("arbitrary",),
        args=[z, z, dmix, wm, wm_t, sbp, gvp, gq2, gk2, seg_ones, lane_ones, sinks, biasp, pair_fold],
    )


def _mixer_out(mix, w_out, x, gt, name):
    S, D = x.shape
    K = mix.shape[1]
    tm, tn = _tile(S, 512), _tile(D, 1024)

    def body(m_ref, w_ref, x_ref, gt_ref, xo_ref, y_ref):
        y = _dot(m_ref[...], w_ref[...])
        y_ref[...] = y
        xo_ref[...] = x_ref[...] + gt_ref[...] * y

    blk = pl.BlockSpec((tm, tn), lambda j, i: (i, j))
    return pl.pallas_call(
        body,
        name=name,
        grid=(D // tn, S // tm),
        in_specs=[pl.BlockSpec((tm, K), lambda j, i: (i, 0)), pl.BlockSpec((K, tn), lambda j, i: (0, j)),
                  blk, pl.BlockSpec((1, tn), lambda j, i: (0, j))],
        out_specs=[blk, blk],
        out_shape=[jax.ShapeDtypeStruct((S, D), F32), jax.ShapeDtypeStruct((S, D), F32)],
        compiler_params=_params(("parallel", "parallel")),
    )(mix, w_out, x, gt)


def _bucket_sum(dst, onehot, name):
    def body(d_ref, o_ref, out_ref):
        out_ref[...] = lax.dot_general(o_ref[...], d_ref[...], NT, precision=lax.Precision.HIGHEST,
                                       preferred_element_type=F32)

    return pl.pallas_call(
        body,
        name=name,
        out_shape=jax.ShapeDtypeStruct((N_BUCKETS, B_HEADS), F32),
    )(dst, onehot)


def _adamw_math(w, g, m, v):
    m = ADAM_B1 * m + (1.0 - ADAM_B1) * g
    v = ADAM_B2 * v + (1.0 - ADAM_B2) * (g * g)
    m_hat = m / (1.0 - ADAM_B1 ** ADAM_STEP)
    v_hat = v / (1.0 - ADAM_B2 ** ADAM_STEP)
    delta = -ADAM_LR * (m_hat / (jnp.sqrt(v_hat) + ADAM_EPS) + ADAM_WD * w)
    return delta, m, v


def _adamw(w, g, m, v, name, ex=None, emit_grad=False):
    R, C = w.shape
    tr = _tile(R, max(SUBLANES, (1 << 19) // C), SUBLANES)

    def body(w_ref, g_ref, m_ref, v_ref, *out_refs):
        gv = g_ref[...]
        results = _adamw_math(w_ref[...], gv, m_ref[...], v_ref[...])
        for ref, val in zip(out_refs, ((gv,) if emit_grad else ()) + results):
            ref[...] = val

    blk = pl.BlockSpec((tr, C), lambda i: (i, 0))
    shape = jax.ShapeDtypeStruct((R, C), F32)
    n_out = 4 if emit_grad else 3
    return _pallas(body, name=name, grid=(R // tr,), in_specs=[blk] * 4, out_specs=[blk] * n_out,
                   out_shape=[shape] * n_out, semantics=("parallel",), args=[w, g, m, v], ex=ex)


def _small_update(parts, w, m, v, name):
    R = w.shape[0]

    def body(p_ref, w_ref, m_ref, v_ref, g_ref, d_ref, mo_ref, vo_ref):
        g = p_ref[0]
        for dev in range(1, N_DEV):
            g = g + p_ref[dev]
        g_ref[...] = g
        d, mn, vn = _adamw_math(w_ref[...], g, m_ref[...], v_ref[...])
        d_ref[...] = d
        mo_ref[...] = mn
        vo_ref[...] = vn

    shape = jax.ShapeDtypeStruct((R, LANES), F32)
    return pl.pallas_call(
        body,
        name=name,
        out_shape=[shape] * 4,
        compiler_params=pltpu.CompilerParams(vmem_limit_bytes=VMEM_LIMIT),
    )(parts, w, m, v)


def _place():
    x, y, c = lax.axis_index("x"), lax.axis_index("y"), lax.axis_index("c")
    chips = [(1 - x, y), (x, 1 - y), (1 - x, 1 - y)]
    return x, y, c, chips


def _remote(src, dst, send_sem, recv_sem, to):
    return pltpu.make_async_remote_copy(src_ref=src, dst_ref=dst, send_sem=send_sem, recv_sem=recv_sem,
                                        device_id=to, device_id_type=MESH)


def _allgather_small(block, name):
    m_per, n = block.shape

    def body(x_ref, out_ref, send_sems, recv_sems, local_sem):
        x, y, c, chips = _place()
        me, sibling = (x, y, c), (x, y, 1 - c)

        def rows(px, py, pc):
            return out_ref.at[pl.ds((4 * px + 2 * py + pc) * m_per, m_per), :]

        def copy(k, blk, to, src=None):
            return _remote(rows(*blk) if src is None else src, rows(*blk), send_sems.at[k], recv_sems.at[k], to)

        mine = pltpu.make_async_copy(x_ref, rows(*me), local_sem)
        mine.start()
        first = [copy(0, me, sibling, src=x_ref)]
        first += [copy(1 + j, me, (*chip, c), src=x_ref) for j, chip in enumerate(chips)]
        for cp in first:
            cp.start()
        passed = [copy(4 + j, (*chip, c), sibling) for j, chip in enumerate(chips)]
        for j, chip in enumerate(chips):
            copy(1 + j, (*chip, c), me).wait_recv()
            passed[j].start()
        copy(0, sibling, me).wait_recv()
        for j, chip in enumerate(chips):
            copy(4 + j, (*chip, 1 - c), me).wait_recv()
        for cp in first + passed:
            cp.wait_send()
        mine.wait()

    return pl.pallas_call(
        body,
        name=name,
        out_shape=jax.ShapeDtypeStruct((N_DEV * m_per, n), block.dtype),
        in_specs=[pl.BlockSpec(memory_space=pltpu.VMEM)],
        out_specs=pl.BlockSpec(memory_space=pltpu.VMEM),
        scratch_shapes=[pltpu.SemaphoreType.DMA((7,)), pltpu.SemaphoreType.DMA((7,)), pltpu.SemaphoreType.DMA],
        compiler_params=pltpu.CompilerParams(vmem_limit_bytes=VMEM_LIMIT),
    )(block)


def _half(ref, c, rows):
    start = pl.multiple_of(c * rows, BF16_ROWS)
    if len(ref.shape) == 2:
        return ref.at[pl.ds(start, rows), :]
    return ref.at[:, pl.ds(start, rows), :]


def _same(arrays):
    return [jax.ShapeDtypeStruct(a.shape, a.dtype) for a in arrays], {t: t for t in range(len(arrays))}


def _ex_gather_ici(bufs):
    def plan(ins, outs, send_sems, recv_sems):
        x, y, c, chips = _place()
        sends, arrivals = [], []
        for t, buf in enumerate(bufs):
            rows = buf.shape[1] // 2
            mine = _half(outs[t].at[2 * x + y], c, rows)
            for k, (px, py) in enumerate(chips):
                sems = (send_sems.at[3 * t + k], recv_sems.at[3 * t + k], (px, py, c))
                landed = _half(outs[t].at[2 * px + py], c, rows)
                sends.append((mine, mine, *sems))
                arrivals.append((landed, landed, *sems))
        return sends, arrivals

    shapes, aliases = _same(bufs)
    return _Exchange(bufs, shapes, aliases, 3 * len(bufs), plan)


def _ex_gather_d2d(bufs):
    def plan(ins, outs, send_sems, recv_sems):
        x, y, c, chips = _place()
        sends, arrivals = [], []
        for t, buf in enumerate(bufs):
            rows = buf.shape[1] // 2
            for k, (px, py) in enumerate(chips):
                sems = (send_sems.at[3 * t + k], recv_sems.at[3 * t + k], (x, y, 1 - c))
                landed = _half(outs[t].at[2 * px + py], c, rows)
                other = _half(outs[t].at[2 * px + py], 1 - c, rows)
                sends.append((landed, landed, *sems))
                arrivals.append((other, other, *sems))
        return sends, arrivals

    shapes, aliases = _same(bufs)
    return _Exchange(bufs, shapes, aliases, 3 * len(bufs), plan)


def _ex_swap_halves(grads):
    def plan(ins, outs, send_sems, recv_sems):
        x, y, c, _ = _place()
        sends = [(_half(ins[t], 1 - c, g.shape[1] // 2), outs[t], send_sems.at[t], recv_sems.at[t], (x, y, 1 - c))
                 for t, g in enumerate(grads)]
        return sends, sends

    shapes = [jax.ShapeDtypeStruct((g.shape[0], g.shape[1] // 2, g.shape[2]), g.dtype) for g in grads]
    return _Exchange(grads, shapes, {}, len(grads), plan)


def _ex_scatter(sums):
    def plan(ins, outs, send_sems, recv_sems):
        x, y, c, chips = _place()
        sends = [(ins[t].at[2 * px + py], outs[t].at[k], send_sems.at[3 * t + k], recv_sems.at[3 * t + k], (px, py, c))
                 for t in range(len(sums)) for k, (px, py) in enumerate(chips)]
        return sends, sends

    shapes = [jax.ShapeDtypeStruct((N_CHIPS - 1,) + s.shape[1:], s.dtype) for s in sums]
    return _Exchange(sums, shapes, {}, 3 * len(sums), plan)


def _ex_join_halves(fulls):
    def plan(ins, outs, send_sems, recv_sems):
        x, y, c, _ = _place()
        sends, arrivals = [], []
        for t, full in enumerate(fulls):
            rows = full.shape[0] // 2
            sems = (send_sems.at[t], recv_sems.at[t], (x, y, 1 - c))
            mine, other = _half(outs[t], c, rows), _half(outs[t], 1 - c, rows)
            sends.append((mine, mine, *sems))
            arrivals.append((other, other, *sems))
        return sends, arrivals

    shapes, aliases = _same(fulls)
    return _Exchange(fulls, shapes, aliases, len(fulls), plan)


class _Shifted:
    def __init__(self, sems, offset):
        self.sems, self.offset = sems, offset

    @property
    def at(self):
        return self

    def __getitem__(self, k):
        return self.sems.at[self.offset + k]


def _combine(exchanges):
    operands, out_shapes, aliases, starts = [], [], {}, []
    n_sems = 0
    for e in exchanges:
        starts.append((len(operands), len(out_shapes), n_sems))
        aliases.update({len(operands) + i: len(out_shapes) + o for i, o in e.aliases.items()})
        operands += list(e.operands)
        out_shapes += list(e.out_shapes)
        n_sems += e.n_sems

    def plan(ins, outs, send_sems, recv_sems):
        sends, arrivals = [], []
        for e, (i0, o0, s0) in zip(exchanges, starts):
            s, a = e.plan(ins[i0:i0 + len(e.operands)], outs[o0:o0 + len(e.out_shapes)],
                          _Shifted(send_sems, s0), _Shifted(recv_sems, s0))
            sends += s
            arrivals += a
        return sends, arrivals

    return _Exchange(operands, out_shapes, aliases, n_sems, plan)


class _Reduction:
    def __init__(self, grad, tag, c_arr, jc_arr):
        self.grad, self.tag, self.c_arr, self.jc_arr, self.stage = grad, tag, c_arr, jc_arr, 0

    def exchange(self):
        if self.stage == 0:
            return _ex_swap_halves([self.grad])
        if self.stage == 1:
            return _ex_scatter([self.sums])
        return _ex_join_halves([self.full])

    def advance(self, landed):
        if self.stage == 0:
            self.recv = landed
            self.sums = _chip_sum(self.grad, landed, self.c_arr, f"chip_sum_{self.tag}")
        elif self.stage == 1:
            self.full = _owner_sum(self.grad, self.recv, landed, self.jc_arr, f"owner_sum_{self.tag}")
        else:
            self.result = landed
        self.stage += 1


def _ride(reductions):
    def done(carried):
        for r, landed in zip(reductions, carried):
            r.advance(landed)

    return _combine([r.exchange() for r in reductions]), done


def _exchange_alone(ex, name):
    return _pallas(None, name=name, grid=(), in_specs=[], out_specs=[], out_shape=[], args=[], ex=ex)


def _cast_to_slot(w, chip_arr, name):
    A, B = w.shape
    ta = _tile(A, max(BF16_ROWS, (1 << 19) // B), BF16_ROWS)

    def body(j_ref, w_ref, o_ref):
        o_ref[...] = w_ref[...].astype(BF16)

    return pl.pallas_call(
        body,
        name=name,
        grid_spec=pltpu.PrefetchScalarGridSpec(
            num_scalar_prefetch=1,
            grid=(A // ta,),
            in_specs=[pl.BlockSpec((ta, B), lambda i, j_ref: (i, 0))],
            out_specs=pl.BlockSpec((None, ta, B), lambda i, j_ref: (j_ref[0], i, 0)),
        ),
        out_shape=jax.ShapeDtypeStruct((N_CHIPS, A, B), BF16),
        compiler_params=_params(("parallel",)),
    )(chip_arr, w)


def _chip_sum(grad, recv, c_arr, name):
    _, A, B = grad.shape
    hA = A // 2
    ta = _tile(hA, max(BF16_ROWS, (1 << 19) // B), BF16_ROWS)
    nh = hA // ta

    def body(c_ref, g_ref, r_ref, o_ref):
        o_ref[...] = (g_ref[...] + r_ref[...]).astype(BF16)

    return pl.pallas_call(
        body,
        name=name,
        grid_spec=pltpu.PrefetchScalarGridSpec(
            num_scalar_prefetch=1,
            grid=(N_CHIPS, nh),
            in_specs=[pl.BlockSpec((None, ta, B), lambda s, i, c_ref: (s, c_ref[0] * nh + i, 0)),
                      pl.BlockSpec((None, ta, B), lambda s, i, c_ref: (s, i, 0))],
            out_specs=pl.BlockSpec((None, ta, B), lambda s, i, c_ref: (s, i, 0)),
        ),
        out_shape=jax.ShapeDtypeStruct((N_CHIPS, hA, B), BF16),
        compiler_params=_params(("parallel", "parallel")),
    )(c_arr, grad, recv)


def _owner_sum(grad, recv, landed, jc_arr, name):
    _, A, B = grad.shape
    hA = A // 2
    ta = _tile(hA, max(BF16_ROWS, (1 << 19) // B), BF16_ROWS)
    nh = hA // ta

    def body(jc_ref, g_ref, r_ref, l0_ref, l1_ref, l2_ref, o_ref):
        total = g_ref[...] + r_ref[...]
        for ref in (l0_ref, l1_ref, l2_ref):
            total = total + ref[...].astype(F32)
        o_ref[...] = total

    def landed_spec(k):
        return pl.BlockSpec((None, ta, B), lambda i, jc_ref: (k, i, 0))

    return pl.pallas_call(
        body,
        name=name,
        grid_spec=pltpu.PrefetchScalarGridSpec(
            num_scalar_prefetch=1,
            grid=(nh,),
            in_specs=[pl.BlockSpec((None, ta, B), lambda i, jc_ref: (jc_ref[0], jc_ref[1] * nh + i, 0)),
                      pl.BlockSpec((None, ta, B), lambda i, jc_ref: (jc_ref[0], i, 0)),
                      landed_spec(0), landed_spec(1), landed_spec(2)],
            out_specs=pl.BlockSpec((ta, B), lambda i, jc_ref: (jc_ref[1] * nh + i, 0)),
        ),
        out_shape=jax.ShapeDtypeStruct((A, B), F32),
        compiler_params=_params(("parallel",)),
    )(jc_arr, grad, recv, landed, landed, landed)


def _pack(parts):
    rows = []
    for p in parts:
        flat = jnp.reshape(p.astype(F32), (-1,))
        tile = SUBLANES * LANES
        padded = -(-flat.shape[0] // tile) * tile
        rows.append(jnp.reshape(jnp.pad(flat, (0, padded - flat.shape[0])), (-1, LANES)))
    return jnp.concatenate(rows, axis=0)


def _unpack(pack, shapes):
    out, row = [], 0
    for shape in shapes:
        size = int(np.prod(shape))
        nrows = -(-size // (SUBLANES * LANES)) * SUBLANES
        out.append(jnp.reshape(jnp.reshape(pack[row:row + nrows], (-1,))[:size], shape))
        row += nrows
    return out


def _bias_tables():
    qi = np.arange(BLOCK)[:, None]
    kj = np.arange(2 * BLOCK)[None, :]
    dist = qi + BLOCK - kj
    in_window = (dist >= 0) & (dist < BLOCK)
    n = np.clip(dist, 0, None)
    max_exact = N_BUCKETS // 2
    nf = np.maximum(n, 1).astype(np.float32)
    large = max_exact + (np.log(nf / max_exact) / math.log(MAX_DISTANCE / max_exact)
                         * (N_BUCKETS - max_exact)).astype(np.int32)
    large = np.minimum(large, N_BUCKETS - 1)
    bucket = np.where(n < max_exact, n, large)
    onehot = (bucket[None] == np.arange(N_BUCKETS)[:, None, None]) & in_window[None]
    first = in_window & (kj >= BLOCK)
    return onehot.astype(np.float32), in_window, first


def kernel(x, c, w_ada, b_ada, g_ffn1, w1_ffn1, w3_ffn1, w2_ffn1, g_mix, w_in, spatial_w, spatial_b, g_v, g_q, g_k, sinks, rel_bias, w_out, g_ffn2, w1_ffn2, w3_ffn2, w2_ffn2, loss_target, m_w_ada, m_b_ada, m_g_ffn1, m_w1_ffn1, m_w3_ffn1, m_w2_ffn1, m_g_mix, m_w_in, m_spatial_w, m_spatial_b, m_g_v, m_g_q, m_g_k, m_sinks, m_rel_bias, m_w_out, m_g_ffn2, m_w1_ffn2, m_w3_ffn2, m_w2_ffn2, v_w_ada, v_b_ada, v_g_ffn1, v_w1_ffn1, v_w3_ffn1, v_w2_ffn1, v_g_mix, v_w_in, v_spatial_w, v_spatial_b, v_g_v, v_g_q, v_g_k, v_sinks, v_rel_bias, v_w_out, v_g_ffn2, v_w1_ffn2, v_w3_ffn2, v_w2_ffn2):
    ax, ay, ac = lax.axis_index("x"), lax.axis_index("y"), lax.axis_index("c")
    chip = 2 * ax + ay
    dev = 2 * chip + ac
    xs = x[0]
    tgt = loss_target[0]
    S, D = xs.shape
    F = N_CHIPS * w1_ffn1.shape[2]
    mod_cols = w_ada.shape[2]

    c_all = _allgather_small(jnp.pad(c, ((0, SUBLANES - 1), (0, 0))), "gather_c")
    c_all = jnp.pad(c_all[::SUBLANES], ((0, BF16_ROWS - N_DEV), (0, 0)))
    b_sh = lax.dynamic_slice(b_ada, (0, chip * mod_cols), (1, mod_cols))
    mod_part, c_act = _mod_partial(c_all, w_ada[0], b_sh, "mod_partial")
    mod_all = _allgather_small(mod_part[:N_DEV], "gather_mod")
    mod_all = jnp.reshape(mod_all, (N_CHIPS, 2, N_DEV, mod_cols))[:, 0]
    mod = jnp.reshape(lax.dynamic_index_in_dim(mod_all, dev, axis=1, keepdims=False), (1, N_MOD * D))
    sh1, sc1, gt1, sh2, sc2, gt2, sh3, sc3, gt3 = [mod[:, i * D:(i + 1) * D] for i in range(N_MOD)]

    def cols_to_natural(w4):
        return jnp.reshape(jnp.transpose(w4, (1, 0, 2)), (w4.shape[1], -1))

    chip_arr = jnp.reshape(chip, (1,)).astype(jnp.int32)
    c_arr = jnp.reshape(ac, (1,)).astype(jnp.int32)
    jc_arr = jnp.stack([chip, ac]).astype(jnp.int32)
    cast = lambda w, nm: _cast_to_slot(w[0], chip_arr, f"cast_{nm}")
    ffn1_bufs = [cast(w1_ffn1, "w1_ffn1"), cast(w3_ffn1, "w3_ffn1"), cast(w2_ffn1, "w2_ffn1")]
    mixer_bufs = [cast(w_in, "w_in"), cast(w_out, "w_out")]
    ffn2_bufs = [cast(w1_ffn2, "w1_ffn2"), cast(w3_ffn2, "w3_ffn2"), cast(w2_ffn2, "w2_ffn2")]
    ffn1_bufs = _exchange_alone(_ex_gather_ici(ffn1_bufs), "gather_ffn1_ici")
    ffn1_bufs = _exchange_alone(_ex_gather_d2d(ffn1_bufs), "gather_ffn1_d2d")
    w1a, w3a, w2a = cols_to_natural(ffn1_bufs[0]), cols_to_natural(ffn1_bufs[1]), jnp.reshape(ffn1_bufs[2], (F, D))

    onehot_np, in_window_np, first_np = _bias_tables()
    onehot = jnp.asarray(onehot_np)
    bias = jnp.einsum("bij,bh->hij", onehot, rel_bias, precision=lax.Precision.HIGHEST)
    biasm = jnp.stack([jnp.where(jnp.asarray(first_np)[None], bias, NEG),
                       jnp.where(jnp.asarray(in_window_np)[None], bias, NEG)])
    causal = jnp.asarray(np.tril(np.ones((BLOCK, BLOCK), dtype=bool)))
    wm = jnp.where(causal[None], spatial_w[0], 0.0).astype(BF16)
    wm_t = jnp.transpose(wm, (0, 2, 1))
    sink_vec = sinks[0]
    per_group = PAIRS // KV_HEADS
    sbp = jnp.broadcast_to(jnp.reshape(spatial_b[0], (A_HEADS * BLOCK, 1)), (A_HEADS * BLOCK, A_DIM))
    gvp = jnp.repeat(g_v[0], BLOCK, axis=0)
    gq2, gk2 = jnp.concatenate([g_q, g_q], axis=1), jnp.concatenate([g_k, g_k], axis=1)
    seg_ones = jnp.asarray(np.kron(np.eye(2, dtype=np.float32), np.ones((HEAD_DIM, HEAD_DIM), np.float32)))
    lane_ones = jnp.full((LANES, LANES), 1.0 / LANES, F32)
    pair_fold = jnp.asarray(np.kron(np.eye(per_group, LANES, dtype=np.float32), np.ones((BLOCK, 1), np.float32)))
    biasp = jnp.reshape(jnp.transpose(jnp.reshape(biasm, (2, KV_HEADS, per_group, 2, BLOCK, BAND)), (0, 1, 3, 5, 2, 4)),
                        (2, KV_HEADS, 2 * BAND, PAIR_ROWS))

    res = _ffn_fwd(xs, g_ffn1, sh1, sc1, gt1, w1a, w3a, w2a, None, "ffn1_fwd", ex=_ex_gather_ici(mixer_bufs + ffn2_bufs))
    (x1, h1, a1, b1, f1), mixer_bufs, ffn2_bufs = res[:5], res[5:7], res[7:]
    h2, *mixer_bufs = _norm_mod(x1, g_mix, sh2, sc2, "mixer_norm", ex=_ex_gather_d2d(mixer_bufs))
    win, wout = cols_to_natural(mixer_bufs[0]), jnp.reshape(mixer_bufs[1], (-1, D))
    z, *ffn2_bufs = _matmul(h2, win, "nn", F32, 512, _tile(IN_COLS, 1664), D, "mixer_in", ex=_ex_gather_d2d(ffn2_bufs))
    w1b, w3b, w2b = cols_to_natural(ffn2_bufs[0]), cols_to_natural(ffn2_bufs[1]), jnp.reshape(ffn2_bufs[2], (F, D))
    mix = _mixer_fwd(z, wm, sbp, gvp, gq2, gk2, seg_ones, lane_ones, sink_vec, biasp, "mixer_fwd")
    x2, ymix = _mixer_out(mix, wout, x1, gt2, "mixer_out")
    g3, df3, h3, a3, b3, dgt3, loss_sum = _ffn_fwd(x2, g_ffn2, sh3, sc3, gt3, w1b, w3b, w2b, tgt, "ffn2_fwd_loss")
    loss = lax.psum(loss_sum[0, 0] * (0.5 / D), ("x", "y", "c"))

    tk = _tile(S, 2048)

    def ffn_weight_grads(h, da, db, s, df, tag, riding):
        ex, done = _ride(riding) if riding else (None, None)
        gw1 = _matmul(h, da, "tn", F32, 1024, F // N_CHIPS, tk, f"grad_w1_{tag}", shard_major=True, ex=ex)
        if riding:
            done(gw1[1:])
            gw1 = gw1[0]
        r1 = _Reduction(gw1, f"w1_{tag}", c_arr, jc_arr)
        ex, done = _ride([r1])
        gw3, *carried = _matmul(h, db, "tn", F32, 1024, F // N_CHIPS, tk, f"grad_w3_{tag}", shard_major=True, ex=ex)
        done(carried)
        r3 = _Reduction(gw3, f"w3_{tag}", c_arr, jc_arr)
        ex, done = _ride([r1, r3])
        gw2, *carried = _matmul(s, df, "tn", F32, _tile(F, 1408), 1024, tk, f"grad_w2_{tag}", ex=ex)
        done(carried)
        r2 = _Reduction(jnp.reshape(gw2, (N_CHIPS, F // N_CHIPS, D)), f"w2_{tag}", c_arr, jc_arr)
        return r1, r3, r2

    da3, db3, s3, dh3 = _ffn_bwd(df3, a3, b3, w1b, w3b, w2b, "ffn2_bwd")
    r21, r23, r22 = ffn_weight_grads(h3, da3, db3, s3, df3, "ffn2", [])
    ex, done = _ride([r22])
    res = _norm_bwd(dh3, x2, g3, g_ffn2, sc3, (ymix, gt2, 1.0), "ffn2_norm_bwd", ex=ex)
    g2, dsh3, dsc3, dgn3, dy, dgt2 = res[:6]
    done(res[6:])

    ex, done = _ride([r21, r23])
    dmix, *carried = _matmul(dy, wout, "nt", BF16, 512, 1024, D, "mixer_out_bwd", ex=ex)
    done(carried)
    ex, done = _ride([r23, r22])
    res = _mixer_bwd(z, dmix, wm, wm_t, sbp, gvp, gq2, gk2, seg_ones, lane_ones, sink_vec, biasp, pair_fold,
                     "mixer_bwd", ex=ex)
    dz_main, dz_kv, dwm, dsb, dgv, dgq, dgk, dsk, dst = res[:9]
    dsb = jnp.reshape(dsb[:, 0], (A_HEADS, BLOCK))
    dgq, dgk = dgq[:, :HEAD_DIM], dgk[:, :HEAD_DIM]
    dsk = jnp.reshape(jnp.transpose(jnp.reshape(dsk[:2 * KV_HEADS, :per_group], (KV_HEADS, 2, per_group)), (0, 2, 1)),
                      (1, B_HEADS))
    dst = jnp.reshape(jnp.transpose(jnp.reshape(dst, (KV_HEADS, 2, BAND, per_group, BLOCK)), (0, 3, 1, 4, 2)),
                      (B_HEADS, BLOCK * BAND))
    done(res[9:])
    dz = jnp.concatenate([dz_main, dz_kv], axis=1)
    ex, done = _ride([r22])
    dh2, *carried = _matmul(dz, win, "nt", F32, 512, 1024, _tile(IN_COLS, 1664), "mixer_in_bwd", ex=ex)
    done(carried)
    drel = _bucket_sum(dst, jnp.reshape(onehot, (N_BUCKETS, -1)), "bucket_sum")
    g1, dsh2, dsc2, dgn2, df1, dgt1 = _norm_bwd(dh2, x1, g2, g_mix, sc2, (f1, gt1, 0.5), "mixer_norm_bwd")

    da1, db1, s1, dh1 = _ffn_bwd(df1, a1, b1, w1a, w3a, w2a, "ffn1_bwd")
    r11, r13, r12 = ffn_weight_grads(h1, da1, db1, s1, df1, "ffn1", [])
    ex, done = _ride([r11, r13, r12])
    gwin_full, *carried = _matmul(h2, dz, "tn", F32, 1024, _tile(IN_COLS, 1664), tk, "grad_w_in", ex=ex)
    done(carried)
    rm_in = _Reduction(jnp.transpose(jnp.reshape(gwin_full, (D, N_CHIPS, -1)), (1, 0, 2)), "w_in", c_arr, jc_arr)
    grad_x, dsh1, dsc1, dgn1 = _norm_bwd(dh1, xs, g1, g_ffn1, sc1, None, "ffn1_norm_bwd")
    ex, done = _ride([r13, r12, rm_in])
    gwout_full, *carried = _matmul(mix, dy, "tn", F32, 1024, 1024, tk, "grad_w_out", ex=ex)
    done(carried)
    rm_out = _Reduction(jnp.reshape(gwout_full, (N_CHIPS, -1, D)), "w_out", c_arr, jc_arr)
    for stage, riding in enumerate([[r12, rm_in, rm_out], [rm_in, rm_out], [rm_out]]):
        ex, done = _ride(riding)
        done(_exchange_alone(ex, f"reduce_tail_{stage}"))

    dmod = jnp.concatenate([dsh1, dsc1, dgt1, dsh2, dsc2, dgt2, dsh3, dsc3, dgt3], axis=1)
    small_w = [b_ada, g_ffn1, g_mix, g_ffn2, spatial_w, spatial_b, g_v, g_q, g_k, sinks, rel_bias]
    small_m = [m_b_ada, m_g_ffn1, m_g_mix, m_g_ffn2, m_spatial_w, m_spatial_b, m_g_v, m_g_q, m_g_k, m_sinks, m_rel_bias]
    small_v = [v_b_ada, v_g_ffn1, v_g_mix, v_g_ffn2, v_spatial_w, v_spatial_b, v_g_v, v_g_q, v_g_k, v_sinks, v_rel_bias]
    small_g = [dmod, dgn1, dgn2, dgn3, jnp.where(causal[None], dwm, 0.0), dsb, dgv, dgq, dgk, dsk, drel]
    shapes = [w.shape for w in small_w]
    gpack = _pack(small_g)
    rows = gpack.shape[0]
    gall = jnp.reshape(_allgather_small(gpack, "gather_small"), (N_DEV, rows, LANES))
    sg, sd, sm, sv = _small_update(gall, _pack(small_w), _pack(small_m), _pack(small_v), "small_update")
    sg, sd, sm, sv = [_unpack(p, shapes) for p in (sg, sd, sm, sv)]

    mod_rows = -(-N_MOD * D // (SUBLANES * LANES)) * SUBLANES
    dmod_all = jnp.reshape(gall[:, :mod_rows], (N_DEV, -1))[:, :N_MOD * D]
    dmod_sh = lax.dynamic_slice(dmod_all, (0, chip * mod_cols), (N_DEV, mod_cols))
    dmod_sh = jnp.pad(dmod_sh, ((0, BF16_ROWS - N_DEV), (0, 0))).astype(BF16)
    g_wada = _matmul(c_act, dmod_sh, "tn", F32, 1024, _tile(mod_cols, 512), BF16_ROWS, "grad_w_ada")
    d_wada, nm_wada, nv_wada = _adamw(w_ada[0], g_wada, m_w_ada[0], v_w_ada[0], "adamw_w_ada")

    gf1 = [r11.result, r13.result, r12.result]
    gf2 = [r21.result, r23.result, r22.result]
    gmx = [rm_in.result, rm_out.result]
    big = {}
    for nm, w, g, m, v in [
        ("w1_ffn1", w1_ffn1, gf1[0], m_w1_ffn1, v_w1_ffn1), ("w3_ffn1", w3_ffn1, gf1[1], m_w3_ffn1, v_w3_ffn1),
        ("w2_ffn1", w2_ffn1, gf1[2], m_w2_ffn1, v_w2_ffn1), ("w_in", w_in, gmx[0], m_w_in, v_w_in),
        ("w_out", w_out, gmx[1], m_w_out, v_w_out), ("w1_ffn2", w1_ffn2, gf2[0], m_w1_ffn2, v_w1_ffn2),
        ("w3_ffn2", w3_ffn2, gf2[1], m_w3_ffn2, v_w3_ffn2), ("w2_ffn2", w2_ffn2, gf2[2], m_w2_ffn2, v_w2_ffn2),
    ]:
        g_out, d, nm_, nv_ = _adamw(w[0], g, m[0], v[0], f"adamw_{nm}", emit_grad=True)
        big[nm] = (g_out[None], d[None], nm_[None], nv_[None])
    big["w_ada"] = (g_wada[None], d_wada[None], nm_wada[None], nv_wada[None])

    order = ["w_ada", "b_ada", "g_ffn1", "w1_ffn1", "w3_ffn1", "w2_ffn1", "g_mix", "w_in", "spatial_w", "spatial_b",
             "g_v", "g_q", "g_k", "sinks", "rel_bias", "w_out", "g_ffn2", "w1_ffn2", "w3_ffn2", "w2_ffn2"]
    small_names = ["b_ada", "g_ffn1", "g_mix", "g_ffn2", "spatial_w", "spatial_b", "g_v", "g_q", "g_k", "sinks", "rel_bias"]
    for i, nm in enumerate(small_names):
        big[nm] = (sg[i], sd[i], sm[i], sv[i])
    outs = [loss, grad_x[None]]
    for kind in range(4):
        outs += [big[nm][kind] for nm in order]
    return tuple(outs)
```

```python
import functools
import math

import jax
import jax.numpy as jnp
import numpy as np
from jax import lax
from jax.experimental import pallas as pl
from jax.experimental.pallas import tpu as pltpu

F32 = jnp.float32
BF16 = jnp.bfloat16
MESH = pl.DeviceIdType.MESH
ANY = pl.BlockSpec(memory_space=pl.ANY)

EPS = 1e-6
BLOCK = 128
A_HEADS = 8
A_DIM = 128
A_WIDTH = A_HEADS * A_DIM
B_HEADS = 16
KV_HEADS = 2
GROUP = B_HEADS // KV_HEADS
HEAD_DIM = 64
B_WIDTH = B_HEADS * HEAD_DIM
KV_WIDTH = KV_HEADS * HEAD_DIM
Q_OFF = 2 * A_WIDTH
K_OFF = Q_OFF + B_WIDTH
V_OFF = K_OFF + KV_WIDTH
IN_COLS = V_OFF + KV_WIDTH
N_BUCKETS = 32
MAX_DISTANCE = 128
N_MOD = 9
N_CHIPS = 4
N_DEV = 8
NEG = -1e30

ADAM_LR = 0.001
ADAM_B1 = 0.9
ADAM_B2 = 0.999
ADAM_EPS = 1e-08
ADAM_WD = 0.01
ADAM_STEP = 10

LANES = 128
SUBLANES = 8
BF16_ROWS = 16
VMEM_LIMIT = 60 * 1024 * 1024

INV_SQRT2 = 1.0 / math.sqrt(2.0)
INV_SQRT_2PI = 1.0 / math.sqrt(2.0 * math.pi)


def _tile(n, pref, mult=LANES):
    t = (min(pref, n) // mult) * mult
    while t >= mult:
        if n % t == 0:
            return t
        t -= mult
    return n


def _params(sem):
    return pltpu.CompilerParams(dimension_semantics=sem, vmem_limit_bytes=VMEM_LIMIT)


class _Exchange:
    def __init__(self, operands, out_shapes, aliases, n_sems, plan):
        self.operands, self.out_shapes, self.aliases, self.n_sems, self.plan = operands, out_shapes, aliases, n_sems, plan


def _pallas(body, *, name, grid, in_specs, out_specs, out_shape, args, scratch_shapes=(), semantics=None, ex=None):
    if ex is None:
        return pl.pallas_call(body, name=name, grid=grid, in_specs=in_specs, out_specs=out_specs, out_shape=out_shape,
                              scratch_shapes=list(scratch_shapes), compiler_params=_params(semantics))(*args)
    n_in, n_out, n_scr = len(in_specs), len(out_specs), len(scratch_shapes)
    e_in, e_out = len(ex.operands), len(ex.out_shapes)

    def wrapped(*refs):
        ins, refs = refs[:n_in], refs[n_in:]
        ex_ins, refs = refs[:e_in], refs[e_in:]
        outs, refs = refs[:n_out], refs[n_out:]
        ex_outs, refs = refs[:e_out], refs[e_out:]
        scratch, (send_sems, recv_sems) = refs[:n_scr], refs[n_scr:]
        first, last = True, True
        for d, size in enumerate(grid):
            first = jnp.logical_and(first, pl.program_id(d) == 0)
            last = jnp.logical_and(last, pl.program_id(d) == size - 1)

        def start():
            sends, _ = ex.plan(ex_ins, ex_outs, send_sems, recv_sems)
            for cp in sends:
                _remote(*cp).start()

        def finish():
            sends, arrivals = ex.plan(ex_ins, ex_outs, send_sems, recv_sems)
            for cp in arrivals:
                _remote(*cp).wait_recv()
            for cp in sends:
                _remote(*cp).wait_send()

        if grid:
            pl.when(first)(start)
        else:
            start()
        if body is not None:
            body(*ins, *outs, *scratch)
        if grid:
            pl.when(last)(finish)
        else:
            finish()

    kwargs = dict(grid=grid) if grid else {}
    return pl.pallas_call(
        wrapped,
        name=name,
        in_specs=list(in_specs) + [ANY] * e_in,
        out_specs=list(out_specs) + [ANY] * e_out,
        out_shape=list(out_shape) + list(ex.out_shapes),
        input_output_aliases={n_in + i: n_out + o for i, o in ex.aliases.items()},
        scratch_shapes=list(scratch_shapes) + [pltpu.SemaphoreType.DMA((ex.n_sems,)), pltpu.SemaphoreType.DMA((ex.n_sems,))],
        compiler_params=_params(("arbitrary",) * len(grid) if grid else None),
        **kwargs,
    )(*args, *ex.operands)


def _dot(a, b, dims=(((1,), (0,)), ((), ()))):
    return lax.dot_general(a, b, dims, preferred_element_type=F32)


NN = (((1,), (0,)), ((), ()))
NT = (((1,), (1,)), ((), ()))
TN = (((0,), (0,)), ((), ()))


def _sigmoid(x):
    return 1.0 / (1.0 + jnp.exp(-x))


def _gelu_and_grad(x):
    cdf = 0.5 * (1.0 + lax.erf(x * INV_SQRT2))
    pdf = jnp.exp(-0.5 * x * x) * INV_SQRT_2PI
    return x * cdf, cdf + x * pdf


def _gelu(x):
    return x * (0.5 * (1.0 + lax.erf(x * INV_SQRT2)))


def _rms(x):
    r = lax.rsqrt(jnp.mean(x * x, axis=-1, keepdims=True) + EPS)
    return x * r, r


ROW_CHUNK = 64


def _for_rows(tm, fn):
    rc = min(ROW_CHUNK, tm)

    def step(r, carry):
        fn(pl.ds(pl.multiple_of(r * rc, rc), rc))
        return carry

    lax.fori_loop(0, tm // rc, step, 0)


def _rms_bwd(dy, xhat, r):
    return r * (dy - xhat * jnp.mean(dy * xhat, axis=-1, keepdims=True))


def _matmul(a, b, mode, out_dtype, tm, tn, tk, name, shard_major=False, ex=None):
    if mode == "nn":
        (M, K), N = a.shape, b.shape[1]
    elif mode == "nt":
        (M, K), N = a.shape, b.shape[0]
    else:
        (K, M), N = a.shape, b.shape[1]
    tm, tn, tk = min(tm, M), min(tn, N), min(tk, K)
    assert M % tm == 0 and N % tn == 0 and K % tk == 0, (name, M, N, K, tm, tn, tk)
    nk = K // tk
    dims = {"nn": NN, "nt": NT, "tn": TN}[mode]
    a_spec = pl.BlockSpec((tk, tm), lambda i, j, k: (k, i)) if mode == "tn" else pl.BlockSpec((tm, tk), lambda i, j, k: (i, k))
    b_spec = pl.BlockSpec((tn, tk), lambda i, j, k: (j, k)) if mode == "nt" else pl.BlockSpec((tk, tn), lambda i, j, k: (k, j))
    if shard_major:
        assert tn * N_CHIPS == N
        out_shape = jax.ShapeDtypeStruct((N_CHIPS, M, tn), out_dtype)
        o_spec = pl.BlockSpec((None, tm, tn), lambda i, j, k: (j, i, 0))
    else:
        out_shape = jax.ShapeDtypeStruct((M, N), out_dtype)
        o_spec = pl.BlockSpec((tm, tn), lambda i, j, k: (i, j))

    direct = nk == 1 or out_dtype == F32

    def body(a_ref, b_ref, o_ref, *scratch):
        k = pl.program_id(2)
        if nk == 1:
            o_ref[...] = _dot(a_ref[...], b_ref[...], dims).astype(o_ref.dtype)
            return
        acc_ref = o_ref if direct else scratch[0]

        @pl.when(k == 0)
        def _():
            acc_ref[...] = jnp.zeros(acc_ref.shape, F32)

        acc_ref[...] += _dot(a_ref[...], b_ref[...], dims)
        if not direct:
            @pl.when(k == nk - 1)
            def _():
                o_ref[...] = acc_ref[...].astype(o_ref.dtype)

    outs = _pallas(body, name=name, grid=(M // tm, N // tn, nk), in_specs=[a_spec, b_spec], out_specs=[o_spec],
                   out_shape=[out_shape], scratch_shapes=[] if direct else [pltpu.VMEM((tm, tn), F32)],
                   semantics=("parallel", "parallel", "arbitrary"), args=[a, b], ex=ex)
    return outs[0] if ex is None else outs


def _mod_partial(c_all, w_ada, b_sh, name):
    R, D = c_all.shape
    N = w_ada.shape[1]
    tn = _tile(N, 512)

    def body(c_ref, w_ref, b_ref, o_ref, ca_ref):
        cv = c_ref[...]
        ca = (cv * _sigmoid(cv)).astype(BF16)
        ca_ref[...] = ca
        o_ref[...] = _dot(ca, w_ref[...].astype(BF16)) + b_ref[...]

    return pl.pallas_call(
        body,
        name=name,
        grid=(N // tn,),
        in_specs=[
            pl.BlockSpec((R, D), lambda j: (0, 0)),
            pl.BlockSpec((D, tn), lambda j: (0, j)),
            pl.BlockSpec((1, tn), lambda j: (0, j)),
        ],
        out_specs=[pl.BlockSpec((R, tn), lambda j: (0, j)), pl.BlockSpec((R, D), lambda j: (0, 0))],
        out_shape=[jax.ShapeDtypeStruct((R, N), F32), jax.ShapeDtypeStruct((R, D), BF16)],
        compiler_params=_params(("arbitrary",)),
    )(c_all, w_ada, b_sh)


FFN_BLOCK = 1024


def _ffn_blocks(F):
    if F % FFN_BLOCK == 0 or F < FFN_BLOCK:
        tf = _tile(F, FFN_BLOCK)
        return tf, F // tf, tf
    nj = -(-F // FFN_BLOCK)
    tail = F - (nj - 1) * FFN_BLOCK
    assert tail % LANES == 0
    return FFN_BLOCK, nj, tail
def _ffn_fwd(x, g, sh, sc, gt, w1, w3, w2, tgt, name, ex=None):
    S, D = x.shape
    F = w1.shape[1]
    tm, tf, nj, tail = _tile(S, 512), *_ffn_blocks(F)
    ni = S // tm
    with_loss = tgt is not None

    def body(*refs):
        if with_loss:
            (x_ref, g_ref, sh_ref, sc_ref, gt_ref, w1_ref, w3_ref, w2_ref, tgt_ref,
             gout_ref, df_ref, h_ref, a_ref, b_ref, dgt_ref, loss_ref, hs_ref, acc_ref) = refs
        else:
            (x_ref, g_ref, sh_ref, sc_ref, gt_ref, w1_ref, w3_ref, w2_ref,
             xo_ref, h_ref, a_ref, b_ref, f_ref, hs_ref, acc_ref) = refs
        i, j = pl.program_id(0), pl.program_id(1)

        @pl.when(j == 0)
        def _():
            def prologue(rows):
                xhat, _ = _rms(x_ref[rows, :])
                hb = ((xhat * g_ref[...]) * (1.0 + sc_ref[...]) + sh_ref[...]).astype(BF16)
                hs_ref[rows, :] = hb
                h_ref[rows, :] = hb

            _for_rows(tm, prologue)

        @pl.when(j == 0)
        def _():
            acc_ref[...] = jnp.zeros(acc_ref.shape, F32)

        def columns(width):
            def run():
                hb = hs_ref[...]
                av = _dot(hb, w1_ref[:, :width])
                bv = _dot(hb, w3_ref[:, :width])
                a_ref[:, :width] = av.astype(BF16)
                b_ref[:, :width] = bv.astype(BF16)
                sv = ((av * _sigmoid(av)) * bv).astype(BF16)
                acc_ref[...] += _dot(sv, w2_ref[:width, :])
            return run

        if tail == tf:
            columns(tf)()
        else:
            pl.when(j < nj - 1)(columns(tf))
            pl.when(j == nj - 1)(columns(tail))

        @pl.when(j == nj - 1)
        def _():
            if with_loss:
                @pl.when(i == 0)
                def _():
                    dgt_ref[...] = jnp.zeros(dgt_ref.shape, F32)
                    loss_ref[...] = jnp.zeros(loss_ref.shape, F32)

            def epilogue(rows):
                fv = acc_ref[rows, :]
                half_gate = 0.5 * gt_ref[...]
                xo = x_ref[rows, :] + half_gate * fv
                if not with_loss:
                    xo_ref[rows, :] = xo
                    f_ref[rows, :] = fv.astype(f_ref.dtype)
                    return
                err = xo - tgt_ref[rows, :]
                gout = err * (1.0 / D)
                gout_ref[rows, :] = gout
                df_ref[rows, :] = (half_gate * gout).astype(BF16)
                dgt_ref[...] += 0.5 * jnp.sum(gout * fv, axis=0, keepdims=True)
                loss_part = jnp.sum(jnp.sum(err * err, axis=1, keepdims=True), axis=0, keepdims=True)
                loss_ref[...] += jnp.broadcast_to(loss_part, loss_ref.shape)

            _for_rows(tm, epilogue)

    row = row_in = pl.BlockSpec((tm, D), lambda i, j: (i, 0), pipeline_mode=pl.Buffered(1))
    vec = pl.BlockSpec((1, D), lambda i, j: (0, 0))
    col = pl.BlockSpec((tm, tf), lambda i, j: (i, j))
    in_specs = [row_in, vec, vec, vec, vec,
                pl.BlockSpec((D, tf), lambda i, j: (0, j)),
                pl.BlockSpec((D, tf), lambda i, j: (0, j)),
                pl.BlockSpec((tf, D), lambda i, j: (j, 0))]
    args = [x, g, sh, sc, gt, w1, w3, w2]
    act = jax.ShapeDtypeStruct((S, F), BF16)
    if with_loss:
        in_specs.append(row_in)
        args.append(tgt)
        out_specs = [row, row, row, col, col, vec, pl.BlockSpec((1, LANES), lambda i, j: (0, 0))]
        out_shape = [jax.ShapeDtypeStruct((S, D), F32), jax.ShapeDtypeStruct((S, D), BF16),
                     jax.ShapeDtypeStruct((S, D), BF16), act, act,
                     jax.ShapeDtypeStruct((1, D), F32), jax.ShapeDtypeStruct((1, LANES), F32)]
    else:
        out_specs = [row, row, col, col, row]
        out_shape = [jax.ShapeDtypeStruct((S, D), F32), jax.ShapeDtypeStruct((S, D), BF16), act, act,
                     jax.ShapeDtypeStruct((S, D), BF16)]
    return _pallas(body, name=name, grid=(ni, nj), in_specs=in_specs, out_specs=out_specs, out_shape=out_shape,
                   scratch_shapes=[pltpu.VMEM((tm, D), BF16), pltpu.VMEM((tm, D), F32)],
                   semantics=("arbitrary", "arbitrary"), args=args, ex=ex)


def _ffn_bwd(df, a, b, w1, w3, w2, name, ex=None):
    S, D = df.shape
    F = a.shape[1]
    tm, tf, nj, tail = _tile(S, 512), *_ffn_blocks(F)

    def body(df_ref, a_ref, b_ref, w1_ref, w3_ref, w2_ref, da_ref, db_ref, s_ref, dh_ref):
        j = pl.program_id(1)

        @pl.when(j == 0)
        def _():
            dh_ref[...] = jnp.zeros(dh_ref.shape, F32)

        def columns(width):
            def run():
                ds = _dot(df_ref[...], w2_ref[:width, :], NT)
                av = a_ref[:, :width].astype(F32)
                bv = b_ref[:, :width].astype(F32)
                sig = _sigmoid(av)
                sil = av * sig
                da = ((ds * bv) * (sig * (1.0 + av * (1.0 - sig)))).astype(BF16)
                db = (ds * sil).astype(BF16)
                da_ref[:, :width] = da
                db_ref[:, :width] = db
                s_ref[:, :width] = (sil * bv).astype(BF16)
                dh_ref[...] += _dot(da, w1_ref[:, :width], NT) + _dot(db, w3_ref[:, :width], NT)
            return run

        if tail == tf:
            columns(tf)()
        else:
            pl.when(j < nj - 1)(columns(tf))
            pl.when(j == nj - 1)(columns(tail))

    row = pl.BlockSpec((tm, D), lambda i, j: (i, 0), pipeline_mode=pl.Buffered(1))
    col = pl.BlockSpec((tm, tf), lambda i, j: (i, j))
    act = jax.ShapeDtypeStruct((S, F), BF16)
    return _pallas(body, name=name, grid=(S // tm, nj),
                   in_specs=[row, col, col,
                             pl.BlockSpec((D, tf), lambda i, j: (0, j)),
                             pl.BlockSpec((D, tf), lambda i, j: (0, j)),
                             pl.BlockSpec((tf, D), lambda i, j: (j, 0))],
                   out_specs=[col, col, col, row],
                   out_shape=[act, act, act, jax.ShapeDtypeStruct((S, D), F32)],
                   semantics=("parallel", "arbitrary"), args=[df, a, b, w1, w3, w2], ex=ex)


def _norm_mod(x, g, sh, sc, name, ex=None):
    S, D = x.shape
    tm = _tile(S, 512)

    def body(x_ref, g_ref, sh_ref, sc_ref, h_ref):
        def step(rows):
            xhat, _ = _rms(x_ref[rows, :])
            h_ref[rows, :] = ((xhat * g_ref[...]) * (1.0 + sc_ref[...]) + sh_ref[...]).astype(BF16)

        _for_rows(tm, step)

    row = pl.BlockSpec((tm, D), lambda i: (i, 0))
    vec = pl.BlockSpec((1, D), lambda i: (0, 0))
    outs = _pallas(body, name=name, grid=(S // tm,), in_specs=[row, vec, vec, vec], out_specs=[row],
                   out_shape=[jax.ShapeDtypeStruct((S, D), BF16)], semantics=("parallel",), args=[x, g, sh, sc], ex=ex)
    return outs[0] if ex is None else outs


def _norm_bwd(dh, x, gres, g, sc, prev, name, ex=None):
    S, D = x.shape
    tm = _tile(S, 256)
    has_prev = prev is not None
    coef = prev[2] if has_prev else None

    def body(*refs):
        if has_prev:
            (dh_ref, x_ref, gr_ref, g_ref, sc_ref, f_ref, gt_ref,
             go_ref, dsh_ref, dsc_ref, dg_ref, dp_ref, dgt_ref) = refs
        else:
            dh_ref, x_ref, gr_ref, g_ref, sc_ref, go_ref, dsh_ref, dsc_ref, dg_ref = refs
        sum_refs = [dsh_ref, dsc_ref, dg_ref] + ([dgt_ref] if has_prev else [])

        @pl.when(pl.program_id(0) == 0)
        def _():
            for ref in sum_refs:
                ref[...] = jnp.zeros(ref.shape, F32)

        def step(rows):
            dh = dh_ref[rows, :]
            xhat, r = _rms(x_ref[rows, :])
            gain = g_ref[...]
            scale1 = 1.0 + sc_ref[...]
            gout = gr_ref[rows, :] + _rms_bwd(dh * scale1 * gain, xhat, r)
            go_ref[rows, :] = gout
            sums = [dh, dh * (xhat * gain), dh * scale1 * xhat]
            if has_prev:
                dp_ref[rows, :] = ((coef * gt_ref[...]) * gout).astype(BF16)
                sums.append(coef * (gout * f_ref[rows, :].astype(F32)))
            for ref, v in zip(sum_refs, sums):
                ref[...] += jnp.sum(v, axis=0, keepdims=True)

        _for_rows(tm, step)

    row = pl.BlockSpec((tm, D), lambda i: (i, 0))
    vec = pl.BlockSpec((1, D), lambda i: (0, 0))
    vshape = jax.ShapeDtypeStruct((1, D), F32)
    in_specs = [row, row, row, vec, vec]
    args = [dh, x, gres, g, sc]
    out_specs = [row, vec, vec, vec]
    out_shape = [jax.ShapeDtypeStruct((S, D), F32), vshape, vshape, vshape]
    if has_prev:
        in_specs += [row, vec]
        args += [prev[0], prev[1]]
        out_specs += [row, vec]
        out_shape += [jax.ShapeDtypeStruct((S, D), BF16), vshape]
    return _pallas(body, name=name, grid=(S // tm,), in_specs=in_specs, out_specs=out_specs, out_shape=out_shape,
                   semantics=("arbitrary",), args=args, ex=ex)


PAIRS = B_HEADS // 2
PAIR_ROWS = (PAIRS // KV_HEADS) * BLOCK
BAND = 2 * BLOCK


def _stack(ref, offset, count):
    return jnp.concatenate([ref[:, offset + p * LANES:offset + (p + 1) * LANES] for p in range(count)], axis=0)


def _seg_mean(x, e_ref):
    return lax.dot_general(x, e_ref[...], NN, precision=lax.Precision.HIGH,
                           preferred_element_type=F32) * (1.0 / HEAD_DIM)


def _block_diag(x, x_rolled, left, kv_head):
    if kv_head == 0:
        top, bottom = jnp.where(left, x, 0.0), jnp.where(left, 0.0, x_rolled)
    else:
        top, bottom = jnp.where(left, x_rolled, 0.0), jnp.where(left, 0.0, x)
    return jnp.concatenate([top, bottom], axis=0).astype(BF16)


def _from_block_diag(g, left, kv_head):
    a, b = g[:BAND], g[BAND:]
    if kv_head == 0:
        return jnp.where(left, a + pltpu.roll(b, HEAD_DIM, 1), 0.0)
    return jnp.where(left, 0.0, pltpu.roll(a, HEAD_DIM, 1) + b)


def _pair_softmax(st, sk_ref, kv_head):
    out = []
    for e in range(2):
        seg = st[e * BAND:(e + 1) * BAND]
        sink = jnp.concatenate([jnp.full((1, BLOCK), sk_ref[kv_head * GROUP + 2 * p + e], F32)
                                for p in range(PAIRS // KV_HEADS)], axis=1)
        m = jnp.maximum(jnp.max(seg, axis=0, keepdims=True), sink)
        p_ = jnp.exp(seg - m)
        e_sink = jnp.exp(sink - m)
        inv = 1.0 / (jnp.sum(p_, axis=0, keepdims=True) + e_sink)
        out.append((p_ * inv, e_sink * inv))
    return out


def _lane_mean(x, ones_ref):
    return lax.dot_general(x, ones_ref[...], NN, precision=lax.Precision.HIGH, preferred_element_type=F32)


def _mixer_specs(nb, last):
    full = lambda shape: pl.BlockSpec(shape, lambda n: (0,) * len(shape))
    z_spec = pl.BlockSpec((BLOCK, IN_COLS), lambda n: (jnp.minimum(n, last), 0))
    zp_spec = pl.BlockSpec((BLOCK, 2 * KV_WIDTH), lambda n: (jnp.clip(n - 1, 0, last), K_OFF // (2 * KV_WIDTH)))
    consts = [full((A_HEADS * BLOCK, A_DIM)), full((A_HEADS * BLOCK, A_DIM)), full((1, LANES)), full((1, LANES)),
              full((LANES, LANES)), full((LANES, LANES)), pl.BlockSpec(memory_space=pltpu.SMEM),
              pl.BlockSpec((None, KV_HEADS, PAIR_ROWS, 2 * BAND), lambda n: (jnp.minimum(n, 1), 0, 0, 0))]
    return full, z_spec, zp_spec, consts


def _mixer_fwd(z, wm, sbp, gvp, gq2, gk2, seg_ones, lane_ones, sinks, biasp, name):
    S = z.shape[0]
    nb = S // BLOCK

    def body(z_ref, zp_ref, wm_ref, sbp_ref, gvp_ref, gq2_ref, gk2_ref, e_ref, l_ref, sk_ref, bias_ref, mix_ref):
        u = _gelu(_stack(z_ref, 0, A_HEADS))
        v = _gelu(_stack(z_ref, A_WIDTH, A_HEADS))
        vhat = v * lax.rsqrt(_lane_mean(v * v, l_ref) + EPS)
        vn = (vhat * gvp_ref[...]).astype(BF16)
        mixed = jnp.concatenate([_dot(wm_ref[h], vn[h * BLOCK:(h + 1) * BLOCK]) for h in range(A_HEADS)], axis=0)
        ya = (u * (mixed + sbp_ref[...])).astype(BF16)
        for h in range(A_HEADS):
            mix_ref[:, h * A_DIM:(h + 1) * A_DIM] = ya[h * BLOCK:(h + 1) * BLOCK]

        left = lax.broadcasted_iota(jnp.int32, (1, LANES), 1) < HEAD_DIM
        kv = jnp.concatenate([zp_ref[...], z_ref[:, K_OFF:K_OFF + 2 * KV_WIDTH]], axis=0)
        k2, v2 = kv[:, :KV_WIDTH], kv[:, KV_WIDTH:]
        kn2 = k2 * lax.rsqrt(_seg_mean(k2 * k2, e_ref) + EPS) * gk2_ref[...]
        kn2_r, v2_r = pltpu.roll(kn2, HEAD_DIM, 1), pltpu.roll(v2, HEAD_DIM, 1)
        qp = _stack(z_ref, Q_OFF, PAIRS)
        qn = (qp * lax.rsqrt(_seg_mean(qp * qp, e_ref) + EPS) * gq2_ref[...]).astype(BF16)
        for kh in range(KV_HEADS):
            kbd, vbd = _block_diag(kn2, kn2_r, left, kh), _block_diag(v2, v2_r, left, kh)
            st = _dot(kbd, qn[kh * PAIR_ROWS:(kh + 1) * PAIR_ROWS], NT) * (HEAD_DIM ** -0.5) + bias_ref[kh]
            wt = jnp.concatenate([w_e for w_e, _ in _pair_softmax(st, sk_ref, kh)], axis=0).astype(BF16)
            o = _dot(wt, vbd, TN).astype(BF16)
            for p in range(PAIRS // KV_HEADS):
                col = A_WIDTH + (kh * (PAIRS // KV_HEADS) + p) * LANES
                mix_ref[:, col:col + LANES] = o[p * BLOCK:(p + 1) * BLOCK]

    full, z_spec, zp_spec, consts = _mixer_specs(nb, nb - 1)
    return pl.pallas_call(
        body,
        name=name,
        grid=(nb,),
        in_specs=[z_spec, zp_spec, full((A_HEADS, BLOCK, BLOCK))] + consts,
        out_specs=pl.BlockSpec((BLOCK, A_WIDTH + B_WIDTH), lambda n: (n, 0)),
        out_shape=jax.ShapeDtypeStruct((S, A_WIDTH + B_WIDTH), BF16),
        compiler_params=_params(("parallel",)),
    )(z, z, wm, sbp, gvp, gq2, gk2, seg_ones, lane_ones, sinks, biasp)


def _mixer_bwd(z, dmix, wm, wm_t, sbp, gvp, gq2, gk2, seg_ones, lane_ones, sinks, biasp, pair_fold, name, ex=None):
    S = z.shape[0]
    nb = S // BLOCK

    def body(z_ref, zp_ref, dmix_ref, wm_ref, wmt_ref, sbp_ref, gvp_ref, gq2_ref, gk2_ref, e_ref, l_ref, sk_ref,
             bias_ref, fold_ref,
             dz_ref, dzkv_ref, dwm_ref, dsb_ref, dgv_ref, dgq_ref, dgk_ref, dsk_ref, dst_ref,
             carry_ref, tot_ref, sbacc_ref, skacc_ref, gqacc_ref, gkacc_ref):
        n = pl.program_id(0)
        left = lax.broadcasted_iota(jnp.int32, (1, LANES), 1) < HEAD_DIM

        @pl.when(n == 0)
        def _():
            for ref in (dwm_ref, dgv_ref, dst_ref, carry_ref, sbacc_ref, skacc_ref, gqacc_ref, gkacc_ref):
                ref[...] = jnp.zeros(ref.shape, ref.dtype)

        @pl.when(n < nb)
        def _():
            u, du_dz = _gelu_and_grad(_stack(z_ref, 0, A_HEADS))
            v, dv_dz = _gelu_and_grad(_stack(z_ref, A_WIDTH, A_HEADS))
            rv = lax.rsqrt(_lane_mean(v * v, l_ref) + EPS)
            vhat = v * rv
            gvp = gvp_ref[...]
            vn = (vhat * gvp).astype(BF16)
            rows = lambda a, h: a[h * BLOCK:(h + 1) * BLOCK]
            mixed = jnp.concatenate([_dot(wm_ref[h], rows(vn, h)) for h in range(A_HEADS)], axis=0) + sbp_ref[...]
            dya = _stack(dmix_ref, 0, A_HEADS).astype(F32)
            dmx = dya * u
            sbacc_ref[...] += dmx
            dmx_b = dmx.astype(BF16)
            for h in range(A_HEADS):
                dwm_ref[h] += _dot(rows(dmx_b, h), rows(vn, h), NT)
            dvn = jnp.concatenate([_dot(wmt_ref[h], rows(dmx_b, h)) for h in range(A_HEADS)], axis=0)
            dgv_ref[...] += jnp.sum(jnp.reshape(dvn * vhat, (A_HEADS, BLOCK, A_DIM)), axis=1)
            dzu = ((dya * mixed) * du_dz).astype(BF16)
            tv = dvn * gvp
            dzv = ((rv * (tv - vhat * _lane_mean(tv * vhat, l_ref))) * dv_dz).astype(BF16)
            for h in range(A_HEADS):
                dz_ref[:, h * A_DIM:(h + 1) * A_DIM] = rows(dzu, h)
                dz_ref[:, A_WIDTH + h * A_DIM:A_WIDTH + (h + 1) * A_DIM] = rows(dzv, h)

            kv = jnp.concatenate([zp_ref[...], z_ref[:, K_OFF:K_OFF + 2 * KV_WIDTH]], axis=0)
            k2, v2 = kv[:, :KV_WIDTH], kv[:, KV_WIDTH:]
            kn2 = k2 * lax.rsqrt(_seg_mean(k2 * k2, e_ref) + EPS) * gk2_ref[...]
            kn2_r, v2_r = pltpu.roll(kn2, HEAD_DIM, 1), pltpu.roll(v2, HEAD_DIM, 1)
            qp = _stack(z_ref, Q_OFF, PAIRS)
            rq = lax.rsqrt(_seg_mean(qp * qp, e_ref) + EPS)
            qhat = qp * rq
            gq2 = gq2_ref[...]
            qn = (qhat * gq2).astype(BF16)
            dop = _stack(dmix_ref, A_WIDTH, PAIRS)
            dqn_parts = []
            dk2n = jnp.zeros((BAND, KV_WIDTH), F32)
            dv2 = jnp.zeros((BAND, KV_WIDTH), F32)
            for kh in range(KV_HEADS):
                kbd, vbd = _block_diag(kn2, kn2_r, left, kh), _block_diag(v2, v2_r, left, kh)
                qg = qn[kh * PAIR_ROWS:(kh + 1) * PAIR_ROWS]
                dog = dop[kh * PAIR_ROWS:(kh + 1) * PAIR_ROWS]
                st = _dot(kbd, qg, NT) * (HEAD_DIM ** -0.5) + bias_ref[kh]
                halves = _pair_softmax(st, sk_ref, kh)
                dpt = _dot(vbd, dog, NT)
                ds_halves, t_halves = [], []
                for e, (w_e, w_sink) in enumerate(halves):
                    dp_e = dpt[e * BAND:(e + 1) * BAND]
                    delta = jnp.sum(w_e * dp_e, axis=0, keepdims=True)
                    ds_halves.append(w_e * (dp_e - delta))
                    t_halves.append(-(w_sink * delta))
                dst = jnp.concatenate(ds_halves, axis=0)
                dst_ref[kh] += dst
                skacc_ref[2 * kh:2 * kh + 2, :] += jnp.concatenate(t_halves, axis=0)
                ds_b = (dst * (HEAD_DIM ** -0.5)).astype(BF16)
                w_b = jnp.concatenate([w_e for w_e, _ in halves], axis=0).astype(BF16)
                dqn_parts.append(_dot(ds_b, kbd, TN))
                dk2n += _from_block_diag(_dot(ds_b, qg), left, kh)
                dv2 += _from_block_diag(_dot(w_b, dog), left, kh)
            dqn = jnp.concatenate(dqn_parts, axis=0)
            gqacc_ref[...] += jnp.sum(dqn * qhat, axis=0, keepdims=True)
            t = dqn * gq2
            dzq = (rq * (t - qhat * _seg_mean(t * qhat, e_ref))).astype(BF16)
            for p in range(PAIRS):
                dz_ref[:, Q_OFF + p * LANES:Q_OFF + (p + 1) * LANES] = rows(dzq, p)
            tot_ref[0] = carry_ref[0] + dk2n[:BLOCK]
            tot_ref[1] = carry_ref[1] + dv2[:BLOCK]
            carry_ref[0] = dk2n[BLOCK:]
            carry_ref[1] = dv2[BLOCK:]

        @pl.when(n == nb)
        def _():
            tot_ref[...] = carry_ref[...]

        kp = zp_ref[:, :KV_WIDTH]
        rk = lax.rsqrt(_seg_mean(kp * kp, e_ref) + EPS)
        khat = kp * rk
        dkn = tot_ref[0]
        gkacc_ref[...] += jnp.sum(dkn * khat, axis=0, keepdims=True)
        t = dkn * gk2_ref[...]
        dzkv_ref[:, :KV_WIDTH] = (rk * (t - khat * _seg_mean(t * khat, e_ref))).astype(BF16)
        dzkv_ref[:, KV_WIDTH:] = tot_ref[1].astype(BF16)

        @pl.when(n == nb)
        def _():
            dsb_ref[...] = jnp.broadcast_to(jnp.sum(sbacc_ref[...], axis=1, keepdims=True), dsb_ref.shape)
            dsk_ref[...] = lax.dot_general(skacc_ref[...], fold_ref[...], NN, precision=lax.Precision.HIGHEST,
                                           preferred_element_type=F32)
            dgq_ref[...] = gqacc_ref[...] + pltpu.roll(gqacc_ref[...], HEAD_DIM, 1)
            dgk_ref[...] = gkacc_ref[...] + pltpu.roll(gkacc_ref[...], HEAD_DIM, 1)

    last = nb - 1
    full, z_spec, zp_spec, consts = _mixer_specs(nb, last)
    return _pallas(
        body,
        name=name,
        grid=(nb + 1,),
        ex=ex,
        in_specs=[z_spec, zp_spec, pl.BlockSpec((BLOCK, A_WIDTH + B_WIDTH), lambda n: (jnp.minimum(n, last), 0)),
                  full((A_HEADS, BLOCK, BLOCK)), full((A_HEADS, BLOCK, BLOCK))] + consts + [full((PAIR_ROWS, LANES))],
        out_specs=[
            pl.BlockSpec((BLOCK, K_OFF), lambda n: (jnp.minimum(n, last), 0)),
            pl.BlockSpec((BLOCK, 2 * KV_WIDTH), lambda n: (jnp.maximum(n - 1, 0), 0)),
            full((A_HEADS, BLOCK, BLOCK)), full((A_HEADS * BLOCK, A_DIM)), full((A_HEADS, A_DIM)),
            full((1, LANES)), full((1, LANES)), full((SUBLANES, LANES)),
            full((KV_HEADS, PAIR_ROWS, 2 * BAND)),
        ],
        out_shape=[
            jax.ShapeDtypeStruct((S, K_OFF), BF16),
            jax.ShapeDtypeStruct((S, 2 * KV_WIDTH), BF16),
            jax.ShapeDtypeStruct((A_HEADS, BLOCK, BLOCK), F32),
            jax.ShapeDtypeStruct((A_HEADS * BLOCK, A_DIM), F32),
            jax.ShapeDtypeStruct((A_HEADS, A_DIM), F32),
            jax.ShapeDtypeStruct((1, LANES), F32),
            jax.ShapeDtypeStruct((1, LANES), F32),
            jax.ShapeDtypeStruct((SUBLANES, LANES), F32),
            jax.ShapeDtypeStruct((KV_HEADS, PAIR_ROWS, 2 * BAND), F32),
        ],
        scratch_shapes=[
            pltpu.VMEM((2, BLOCK, KV_WIDTH), F32),
            pltpu.VMEM((2, BLOCK, KV_WIDTH), F32),
            pltpu.VMEM((A_HEADS * BLOCK, A_DIM), F32),
            pltpu.VMEM((SUBLANES, PAIR_ROWS), F32),
            pltpu.VMEM((1, LANES), F32),
            pltpu.VMEM((1, LANES), F32),
        ],
        semantics=("arbitrary",),
        args=[z, z, dmix, wm, wm_t, sbp, gvp, gq2, gk2, seg_ones, lane_ones, sinks, biasp, pair_fold],
    )


def _mixer_out(mix, w_out, x, gt, name):
    S, D = x.shape
    K = mix.shape[1]
    tm, tn = _tile(S, 512), _tile(D, 1024)

    def body(m_ref, w_ref, x_ref, gt_ref, xo_ref, y_ref):
        y = _dot(m_ref[...], w_ref[...])
        y_ref[...] = y
        xo_ref[...] = x_ref[...] + gt_ref[...] * y

    blk = pl.BlockSpec((tm, tn), lambda j, i: (i, j))
    return pl.pallas_call(
        body,
        name=name,
        grid=(D // tn, S // tm),
        in_specs=[pl.BlockSpec((tm, K), lambda j, i: (i, 0)), pl.BlockSpec((K, tn), lambda j, i: (0, j)),
                  blk, pl.BlockSpec((1, tn), lambda j, i: (0, j))],
        out_specs=[blk, blk],
        out_shape=[jax.ShapeDtypeStruct((S, D), F32), jax.ShapeDtypeStruct((S, D), F32)],
        compiler_params=_params(("parallel", "parallel")),
    )(mix, w_out, x, gt)


def _bucket_sum(dst, onehot, name):
    def body(d_ref, o_ref, out_ref):
        out_ref[...] = lax.dot_general(o_ref[...], d_ref[...], NT, precision=lax.Precision.HIGHEST,
                                       preferred_element_type=F32)

    return pl.pallas_call(
        body,
        name=name,
        out_shape=jax.ShapeDtypeStruct((N_BUCKETS, B_HEADS), F32),
    )(dst, onehot)


def _adamw_math(w, g, m, v):
    m = ADAM_B1 * m + (1.0 - ADAM_B1) * g
    v = ADAM_B2 * v + (1.0 - ADAM_B2) * (g * g)
    m_hat = m / (1.0 - ADAM_B1 ** ADAM_STEP)
    v_hat = v / (1.0 - ADAM_B2 ** ADAM_STEP)
    delta = -ADAM_LR * (m_hat / (jnp.sqrt(v_hat) + ADAM_EPS) + ADAM_WD * w)
    return delta, m, v


def _adamw(w, g, m, v, name, ex=None, emit_grad=False):
    R, C = w.shape
    tr = _tile(R, max(SUBLANES, (1 << 19) // C), SUBLANES)

    def body(w_ref, g_ref, m_ref, v_ref, *out_refs):
        gv = g_ref[...]
        results = _adamw_math(w_ref[...], gv, m_ref[...], v_ref[...])
        for ref, val in zip(out_refs, ((gv,) if emit_grad else ()) + results):
            ref[...] = val

    blk = pl.BlockSpec((tr, C), lambda i: (i, 0))
    shape = jax.ShapeDtypeStruct((R, C), F32)
    n_out = 4 if emit_grad else 3
    return _pallas(body, name=name, grid=(R // tr,), in_specs=[blk] * 4, out_specs=[blk] * n_out,
                   out_shape=[shape] * n_out, semantics=("parallel",), args=[w, g, m, v], ex=ex)


def _small_update(parts, w, m, v, name):
    R = w.shape[0]

    def body(p_ref, w_ref, m_ref, v_ref, g_ref, d_ref, mo_ref, vo_ref):
        g = p_ref[0]
        for dev in range(1, N_DEV):
            g = g + p_ref[dev]
        g_ref[...] = g
        d, mn, vn = _adamw_math(w_ref[...], g, m_ref[...], v_ref[...])
        d_ref[...] = d
        mo_ref[...] = mn
        vo_ref[...] = vn

    shape = jax.ShapeDtypeStruct((R, LANES), F32)
    return pl.pallas_call(
        body,
        name=name,
        out_shape=[shape] * 4,
        compiler_params=pltpu.CompilerParams(vmem_limit_bytes=VMEM_LIMIT),
    )(parts, w, m, v)


def _place():
    x, y, c = lax.axis_index("x"), lax.axis_index("y"), lax.axis_index("c")
    chips = [(1 - x, y), (x, 1 - y), (1 - x, 1 - y)]
    return x, y, c, chips


def _remote(src, dst, send_sem, recv_sem, to):
    return pltpu.make_async_remote_copy(src_ref=src, dst_ref=dst, send_sem=send_sem, recv_sem=recv_sem,
                                        device_id=to, device_id_type=MESH)


def _allgather_small(block, name):
    m_per, n = block.shape

    def body(x_ref, out_ref, send_sems, recv_sems, local_sem):
        x, y, c, chips = _place()
        me, sibling = (x, y, c), (x, y, 1 - c)

        def rows(px, py, pc):
            return out_ref.at[pl.ds((4 * px + 2 * py + pc) * m_per, m_per), :]

        def copy(k, blk, to, src=None):
            return _remote(rows(*blk) if src is None else src, rows(*blk), send_sems.at[k], recv_sems.at[k], to)

        mine = pltpu.make_async_copy(x_ref, rows(*me), local_sem)
        mine.start()
        first = [copy(0, me, sibling, src=x_ref)]
        first += [copy(1 + j, me, (*chip, c), src=x_ref) for j, chip in enumerate(chips)]
        for cp in first:
            cp.start()
        passed = [copy(4 + j, (*chip, c), sibling) for j, chip in enumerate(chips)]
        for j, chip in enumerate(chips):
            copy(1 + j, (*chip, c), me).wait_recv()
            passed[j].start()
        copy(0, sibling, me).wait_recv()
        for j, chip in enumerate(chips):
            copy(4 + j, (*chip, 1 - c), me).wait_recv()
        for cp in first + passed:
            cp.wait_send()
        mine.wait()

    return pl.pallas_call(
        body,
        name=name,
        out_shape=jax.ShapeDtypeStruct((N_DEV * m_per, n), block.dtype),
        in_specs=[pl.BlockSpec(memory_space=pltpu.VMEM)],
        out_specs=pl.BlockSpec(memory_space=pltpu.VMEM),
        scratch_shapes=[pltpu.SemaphoreType.DMA((7,)), pltpu.SemaphoreType.DMA((7,)), pltpu.SemaphoreType.DMA],
        compiler_params=pltpu.CompilerParams(vmem_limit_bytes=VMEM_LIMIT),
    )(block)


def _half(ref, c, rows):
    start = pl.multiple_of(c * rows, BF16_ROWS)
    if len(ref.shape) == 2:
        return ref.at[pl.ds(start, rows), :]
    return ref.at[:, pl.ds(start, rows), :]


def _same(arrays):
    return [jax.ShapeDtypeStruct(a.shape, a.dtype) for a in arrays], {t: t for t in range(len(arrays))}


def _ex_gather_ici(bufs):
    def plan(ins, outs, send_sems, recv_sems):
        x, y, c, chips = _place()
        sends, arrivals = [], []
        for t, buf in enumerate(bufs):
            rows = buf.shape[1] // 2
            mine = _half(outs[t].at[2 * x + y], c, rows)
            for k, (px, py) in enumerate(chips):
                sems = (send_sems.at[3 * t + k], recv_sems.at[3 * t + k], (px, py, c))
                landed = _half(outs[t].at[2 * px + py], c, rows)
                sends.append((mine, mine, *sems))
                arrivals.append((landed, landed, *sems))
        return sends, arrivals

    shapes, aliases = _same(bufs)
    return _Exchange(bufs, shapes, aliases, 3 * len(bufs), plan)


def _ex_gather_d2d(bufs):
    def plan(ins, outs, send_sems, recv_sems):
        x, y, c, chips = _place()
        sends, arrivals = [], []
        for t, buf in enumerate(bufs):
            rows = buf.shape[1] // 2
            for k, (px, py) in enumerate(chips):
                sems = (send_sems.at[3 * t + k], recv_sems.at[3 * t + k], (x, y, 1 - c))
                landed = _half(outs[t].at[2 * px + py], c, rows)
                other = _half(outs[t].at[2 * px + py], 1 - c, rows)
                sends.append((landed, landed, *sems))
                arrivals.append((other, other, *sems))
        return sends, arrivals

    shapes, aliases = _same(bufs)
    return _Exchange(bufs, shapes, aliases, 3 * len(bufs), plan)


def _ex_swap_halves(grads):
    def plan(ins, outs, send_sems, recv_sems):
        x, y, c, _ = _place()
        sends = [(_half(ins[t], 1 - c, g.shape[1] // 2), outs[t], send_sems.at[t], recv_sems.at[t], (x, y, 1 - c))
                 for t, g in enumerate(grads)]
        return sends, sends

    shapes = [jax.ShapeDtypeStruct((g.shape[0], g.shape[1] // 2, g.shape[2]), g.dtype) for g in grads]
    return _Exchange(grads, shapes, {}, len(grads), plan)


def _ex_scatter(sums):
    def plan(ins, outs, send_sems, recv_sems):
        x, y, c, chips = _place()
        sends = [(ins[t].at[2 * px + py], outs[t].at[k], send_sems.at[3 * t + k], recv_sems.at[3 * t + k], (px, py, c))
                 for t in range(len(sums)) for k, (px, py) in enumerate(chips)]
        return sends, sends

    shapes = [jax.ShapeDtypeStruct((N_CHIPS - 1,) + s.shape[1:], s.dtype) for s in sums]
    return _Exchange(sums, shapes, {}, 3 * len(sums), plan)


def _ex_join_halves(fulls):
    def plan(ins, outs, send_sems, recv_sems):
        x, y, c, _ = _place()
        sends, arrivals = [], []
        for t, full in enumerate(fulls):
            rows = full.shape[0] // 2
            sems = (send_sems.at[t], recv_sems.at[t], (x, y, 1 - c))
            mine, other = _half(outs[t], c, rows), _half(outs[t], 1 - c, rows)
            sends.append((mine, mine, *sems))
            arrivals.append((other, other, *sems))
        return sends, arrivals

    shapes, aliases = _same(fulls)
    return _Exchange(fulls, shapes, aliases, len(fulls), plan)


class _Shifted:
    def __init__(self, sems, offset):
        self.sems, self.offset = sems, offset

    @property
    def at(self):
        return self

    def __getitem__(self, k):
        return self.sems.at[self.offset + k]


def _combine(exchanges):
    operands, out_shapes, aliases, starts = [], [], {}, []
    n_sems = 0
    for e in exchanges:
        starts.append((len(operands), len(out_shapes), n_sems))
        aliases.update({len(operands) + i: len(out_shapes) + o for i, o in e.aliases.items()})
        operands += list(e.operands)
        out_shapes += list(e.out_shapes)
        n_sems += e.n_sems

    def plan(ins, outs, send_sems, recv_sems):
        sends, arrivals = [], []
        for e, (i0, o0, s0) in zip(exchanges, starts):
            s, a = e.plan(ins[i0:i0 + len(e.operands)], outs[o0:o0 + len(e.out_shapes)],
                          _Shifted(send_sems, s0), _Shifted(recv_sems, s0))
            sends += s
            arrivals += a
        return sends, arrivals

    return _Exchange(operands, out_shapes, aliases, n_sems, plan)


class _Reduction:
    def __init__(self, grad, tag, c_arr, jc_arr):
        self.grad, self.tag, self.c_arr, self.jc_arr, self.stage = grad, tag, c_arr, jc_arr, 0

    def exchange(self):
        if self.stage == 0:
            return _ex_swap_halves([self.grad])
        if self.stage == 1:
            return _ex_scatter([self.sums])
        return _ex_join_halves([self.full])

    def advance(self, landed):
        if self.stage == 0:
            self.recv = landed
            self.sums = _chip_sum(self.grad, landed, self.c_arr, f"chip_sum_{self.tag}")
        elif self.stage == 1:
            self.full = _owner_sum(self.grad, self.recv, landed, self.jc_arr, f"owner_sum_{self.tag}")
        else:
            self.result = landed
        self.stage += 1


def _ride(reductions):
    def done(carried):
        for r, landed in zip(reductions, carried):
            r.advance(landed)

    return _combine([r.exchange() for r in reductions]), done


def _exchange_alone(ex, name):
    return _pallas(None, name=name, grid=(), in_specs=[], out_specs=[], out_shape=[], args=[], ex=ex)


def _cast_to_slot(w, chip_arr, name):
    A, B = w.shape
    ta = _tile(A, max(BF16_ROWS, (1 << 19) // B), BF16_ROWS)

    def body(j_ref, w_ref, o_ref):
        o_ref[...] = w_ref[...].astype(BF16)

    return pl.pallas_call(
        body,
        name=name,
        grid_spec=pltpu.PrefetchScalarGridSpec(
            num_scalar_prefetch=1,
            grid=(A // ta,),
            in_specs=[pl.BlockSpec((ta, B), lambda i, j_ref: (i, 0))],
            out_specs=pl.BlockSpec((None, ta, B), lambda i, j_ref: (j_ref[0], i, 0)),
        ),
        out_shape=jax.ShapeDtypeStruct((N_CHIPS, A, B), BF16),
        compiler_params=_params(("parallel",)),
    )(chip_arr, w)


def _chip_sum(grad, recv, c_arr, name):
    _, A, B = grad.shape
    hA = A // 2
    ta = _tile(hA, max(BF16_ROWS, (1 << 19) // B), BF16_ROWS)
    nh = hA // ta

    def body(c_ref, g_ref, r_ref, o_ref):
        o_ref[...] = (g_ref[...] + r_ref[...]).astype(BF16)

    return pl.pallas_call(
        body,
        name=name,
        grid_spec=pltpu.PrefetchScalarGridSpec(
            num_scalar_prefetch=1,
            grid=(N_CHIPS, nh),
            in_specs=[pl.BlockSpec((None, ta, B), lambda s, i, c_ref: (s, c_ref[0] * nh + i, 0)),
                      pl.BlockSpec((None, ta, B), lambda s, i, c_ref: (s, i, 0))],
            out_specs=pl.BlockSpec((None, ta, B), lambda s, i, c_ref: (s, i, 0)),
        ),
        out_shape=jax.ShapeDtypeStruct((N_CHIPS, hA, B), BF16),
        compiler_params=_params(("parallel", "parallel")),
    )(c_arr, grad, recv)


def _owner_sum(grad, recv, landed, jc_arr, name):
    _, A, B = grad.shape
    hA = A // 2
    ta = _tile(hA, max(BF16_ROWS, (1 << 19) // B), BF16_ROWS)
    nh = hA // ta

    def body(jc_ref, g_ref, r_ref, l0_ref, l1_ref, l2_ref, o_ref):
        total = g_ref[...] + r_ref[...]
        for ref in (l0_ref, l1_ref, l2_ref):
            total = total + ref[...].astype(F32)
        o_ref[...] = total

    def landed_spec(k):
        return pl.BlockSpec((None, ta, B), lambda i, jc_ref: (k, i, 0))

    return pl.pallas_call(
        body,
        name=name,
        grid_spec=pltpu.PrefetchScalarGridSpec(
            num_scalar_prefetch=1,
            grid=(nh,),
            in_specs=[pl.BlockSpec((None, ta, B), lambda i, jc_ref: (jc_ref[0], jc_ref[1] * nh + i, 0)),
                      pl.BlockSpec((None, ta, B), lambda i, jc_ref: (jc_ref[0], i, 0)),
                      landed_spec(0), landed_spec(1), landed_spec(2)],
            out_specs=pl.BlockSpec((ta, B), lambda i, jc_ref: (jc_ref[1] * nh + i, 0)),
        ),
        out_shape=jax.ShapeDtypeStruct((A, B), F32),
        compiler_params=_params(("parallel",)),
    )(jc_arr, grad, recv, landed, landed, landed)


def _pack(parts):
    rows = []
    for p in parts:
        flat = jnp.reshape(p.astype(F32), (-1,))
        tile = SUBLANES * LANES
        padded = -(-flat.shape[0] // tile) * tile
        rows.append(jnp.reshape(jnp.pad(flat, (0, padded - flat.shape[0])), (-1, LANES)))
    return jnp.concatenate(rows, axis=0)


def _unpack(pack, shapes):
    out, row = [], 0
    for shape in shapes:
        size = int(np.prod(shape))
        nrows = -(-size // (SUBLANES * LANES)) * SUBLANES
        out.append(jnp.reshape(jnp.reshape(pack[row:row + nrows], (-1,))[:size], shape))
        row += nrows
    return out


def _bias_tables():
    qi = np.arange(BLOCK)[:, None]
    kj = np.arange(2 * BLOCK)[None, :]
    dist = qi + BLOCK - kj
    in_window = (dist >= 0) & (dist < BLOCK)
    n = np.clip(dist, 0, None)
    max_exact = N_BUCKETS // 2
    nf = np.maximum(n, 1).astype(np.float32)
    large = max_exact + (np.log(nf / max_exact) / math.log(MAX_DISTANCE / max_exact)
                         * (N_BUCKETS - max_exact)).astype(np.int32)
    large = np.minimum(large, N_BUCKETS - 1)
    bucket = np.where(n < max_exact, n, large)
    onehot = (bucket[None] == np.arange(N_BUCKETS)[:, None, None]) & in_window[None]
    first = in_window & (kj >= BLOCK)
    return onehot.astype(np.float32), in_window, first


def kernel(x, c, w_ada, b_ada, g_ffn1, w1_ffn1, w3_ffn1, w2_ffn1, g_mix, w_in, spatial_w, spatial_b, g_v, g_q, g_k, sinks, rel_bias, w_out, g_ffn2, w1_ffn2, w3_ffn2, w2_ffn2, loss_target, m_w_ada, m_b_ada, m_g_ffn1, m_w1_ffn1, m_w3_ffn1, m_w2_ffn1, m_g_mix, m_w_in, m_spatial_w, m_spatial_b, m_g_v, m_g_q, m_g_k, m_sinks, m_rel_bias, m_w_out, m_g_ffn2, m_w1_ffn2, m_w3_ffn2, m_w2_ffn2, v_w_ada, v_b_ada, v_g_ffn1, v_w1_ffn1, v_w3_ffn1, v_w2_ffn1, v_g_mix, v_w_in, v_spatial_w, v_spatial_b, v_g_v, v_g_q, v_g_k, v_sinks, v_rel_bias, v_w_out, v_g_ffn2, v_w1_ffn2, v_w3_ffn2, v_w2_ffn2):
    ax, ay, ac = lax.axis_index("x"), lax.axis_index("y"), lax.axis_index("c")
    chip = 2 * ax + ay
    dev = 2 * chip + ac
    xs = x[0]
    tgt = loss_target[0]
    S, D = xs.shape
    F = N_CHIPS * w1_ffn1.shape[2]
    mod_cols = w_ada.shape[2]

    c_all = _allgather_small(jnp.pad(c, ((0, SUBLANES - 1), (0, 0))), "gather_c")
    c_all = jnp.pad(c_all[::SUBLANES], ((0, BF16_ROWS - N_DEV), (0, 0)))
    b_sh = lax.dynamic_slice(b_ada, (0, chip * mod_cols), (1, mod_cols))
    mod_part, c_act = _mod_partial(c_all, w_ada[0], b_sh, "mod_partial")
    mod_all = _allgather_small(mod_part[:N_DEV], "gather_mod")
    mod_all = jnp.reshape(mod_all, (N_CHIPS, 2, N_DEV, mod_cols))[:, 0]
    mod = jnp.reshape(lax.dynamic_index_in_dim(mod_all, dev, axis=1, keepdims=False), (1, N_MOD * D))
    sh1, sc1, gt1, sh2, sc2, gt2, sh3, sc3, gt3 = [mod[:, i * D:(i + 1) * D] for i in range(N_MOD)]

    def cols_to_natural(w4):
        return jnp.reshape(jnp.transpose(w4, (1, 0, 2)), (w4.shape[1], -1))

    chip_arr = jnp.reshape(chip, (1,)).astype(jnp.int32)
    c_arr = jnp.reshape(ac, (1,)).astype(jnp.int32)
    jc_arr = jnp.stack([chip, ac]).astype(jnp.int32)
    cast = lambda w, nm: _cast_to_slot(w[0], chip_arr, f"cast_{nm}")
    ffn1_bufs = [cast(w1_ffn1, "w1_ffn1"), cast(w3_ffn1, "w3_ffn1"), cast(w2_ffn1, "w2_ffn1")]
    mixer_bufs = [cast(w_in, "w_in"), cast(w_out, "w_out")]
    ffn2_bufs = [cast(w1_ffn2, "w1_ffn2"), cast(w3_ffn2, "w3_ffn2"), cast(w2_ffn2, "w2_ffn2")]
    ffn1_bufs = _exchange_alone(_ex_gather_ici(ffn1_bufs), "gather_ffn1_ici")
    ffn1_bufs = _exchange_alone(_ex_gather_d2d(ffn1_bufs), "gather_ffn1_d2d")
    w1a, w3a, w2a = cols_to_natural(ffn1_bufs[0]), cols_to_natural(ffn1_bufs[1]), jnp.reshape(ffn1_bufs[2], (F, D))

    onehot_np, in_window_np, first_np = _bias_tables()
    onehot = jnp.asarray(onehot_np)
    bias = jnp.einsum("bij,bh->hij", onehot, rel_bias, precision=lax.Precision.HIGHEST)
    biasm = jnp.stack([jnp.where(jnp.asarray(first_np)[None], bias, NEG),
                       jnp.where(jnp.asarray(in_window_np)[None], bias, NEG)])
    causal = jnp.asarray(np.tril(np.ones((BLOCK, BLOCK), dtype=bool)))
    wm = jnp.where(causal[None], spatial_w[0], 0.0).astype(BF16)
    wm_t = jnp.transpose(wm, (0, 2, 1))
    sink_vec = sinks[0]
    per_group = PAIRS // KV_HEADS
    sbp = jnp.broadcast_to(jnp.reshape(spatial_b[0], (A_HEADS * BLOCK, 1)), (A_HEADS * BLOCK, A_DIM))
    gvp = jnp.repeat(g_v[0], BLOCK, axis=0)
    gq2, gk2 = jnp.concatenate([g_q, g_q], axis=1), jnp.concatenate([g_k, g_k], axis=1)
    seg_ones = jnp.asarray(np.kron(np.eye(2, dtype=np.float32), np.ones((HEAD_DIM, HEAD_DIM), np.float32)))
    lane_ones = jnp.full((LANES, LANES), 1.0 / LANES, F32)
    pair_fold = jnp.asarray(np.kron(np.eye(per_group, LANES, dtype=np.float32), np.ones((BLOCK, 1), np.float32)))
    biasp = jnp.reshape(jnp.transpose(jnp.reshape(biasm, (2, KV_HEADS, per_group, 2, BLOCK, BAND)), (0, 1, 3, 5, 2, 4)),
                        (2, KV_HEADS, 2 * BAND, PAIR_ROWS))

    res = _ffn_fwd(xs, g_ffn1, sh1, sc1, gt1, w1a, w3a, w2a, None, "ffn1_fwd", ex=_ex_gather_ici(mixer_bufs + ffn2_bufs))
    (x1, h1, a1, b1, f1), mixer_bufs, ffn2_bufs = res[:5], res[5:7], res[7:]
    h2, *mixer_bufs = _norm_mod(x1, g_mix, sh2, sc2, "mixer_norm", ex=_ex_gather_d2d(mixer_bufs))
    win, wout = cols_to_natural(mixer_bufs[0]), jnp.reshape(mixer_bufs[1], (-1, D))
    z, *ffn2_bufs = _matmul(h2, win, "nn", F32, 512, _tile(IN_COLS, 1664), D, "mixer_in", ex=_ex_gather_d2d(ffn2_bufs))
    w1b, w3b, w2b = cols_to_natural(ffn2_bufs[0]), cols_to_natural(ffn2_bufs[1]), jnp.reshape(ffn2_bufs[2], (F, D))
    mix = _mixer_fwd(z, wm, sbp, gvp, gq2, gk2, seg_ones, lane_ones, sink_vec, biasp, "mixer_fwd")
    x2, ymix = _mixer_out(mix, wout, x1, gt2, "mixer_out")
    g3, df3, h3, a3, b3, dgt3, loss_sum = _ffn_fwd(x2, g_ffn2, sh3, sc3, gt3, w1b, w3b, w2b, tgt, "ffn2_fwd_loss")
    loss = lax.psum(loss_sum[0, 0] * (0.5 / D), ("x", "y", "c"))

    tk = _tile(S, 2048)

    def ffn_weight_grads(h, da, db, s, df, tag, riding):
        ex, done = _ride(riding) if riding else (None, None)
        gw1 = _matmul(h, da, "tn", F32, 1024, F // N_CHIPS, tk, f"grad_w1_{tag}", shard_major=True, ex=ex)
        if riding:
            done(gw1[1:])
            gw1 = gw1[0]
        r1 = _Reduction(gw1, f"w1_{tag}", c_arr, jc_arr)
        ex, done = _ride([r1])
        gw3, *carried = _matmul(h, db, "tn", F32, 1024, F // N_CHIPS, tk, f"grad_w3_{tag}", shard_major=True, ex=ex)
        done(carried)
        r3 = _Reduction(gw3, f"w3_{tag}", c_arr, jc_arr)
        ex, done = _ride([r1, r3])
        gw2, *carried = _matmul(s, df, "tn", F32, _tile(F, 1408), 1024, tk, f"grad_w2_{tag}", ex=ex)
        done(carried)
        r2 = _Reduction(jnp.reshape(gw2, (N_CHIPS, F // N_CHIPS, D)), f"w2_{tag}", c_arr, jc_arr)
        return r1, r3, r2

    da3, db3, s3, dh3 = _ffn_bwd(df3, a3, b3, w1b, w3b, w2b, "ffn2_bwd")
    r21, r23, r22 = ffn_weight_grads(h3, da3, db3, s3, df3, "ffn2", [])
    ex, done = _ride([r22])
    res = _norm_bwd(dh3, x2, g3, g_ffn2, sc3, (ymix, gt2, 1.0), "ffn2_norm_bwd", ex=ex)
    g2, dsh3, dsc3, dgn3, dy, dgt2 = res[:6]
    done(res[6:])

    ex, done = _ride([r21, r23])
    dmix, *carried = _matmul(dy, wout, "nt", BF16, 512, 1024, D, "mixer_out_bwd", ex=ex)
    done(carried)
    ex, done = _ride([r23, r22])
    res = _mixer_bwd(z, dmix, wm, wm_t, sbp, gvp, gq2, gk2, seg_ones, lane_ones, sink_vec, biasp, pair_fold,
                     "mixer_bwd", ex=ex)
    dz_main, dz_kv, dwm, dsb, dgv, dgq, dgk, dsk, dst = res[:9]
    dsb = jnp.reshape(dsb[:, 0], (A_HEADS, BLOCK))
    dgq, dgk = dgq[:, :HEAD_DIM], dgk[:, :HEAD_DIM]
    dsk = jnp.reshape(jnp.transpose(jnp.reshape(dsk[:2 * KV_HEADS, :per_group], (KV_HEADS, 2, per_group)), (0, 2, 1)),
                      (1, B_HEADS))
    dst = jnp.reshape(jnp.transpose(jnp.reshape(dst, (KV_HEADS, 2, BAND, per_group, BLOCK)), (0, 3, 1, 4, 2)),
                      (B_HEADS, BLOCK * BAND))
    done(res[9:])
    dz = jnp.concatenate([dz_main, dz_kv], axis=1)
    ex, done = _ride([r22])
    dh2, *carried = _matmul(dz, win, "nt", F32, 512, 1024, _tile(IN_COLS, 1664), "mixer_in_bwd", ex=ex)
    done(carried)
    drel = _bucket_sum(dst, jnp.reshape(onehot, (N_BUCKETS, -1)), "bucket_sum")
    g1, dsh2, dsc2, dgn2, df1, dgt1 = _norm_bwd(dh2, x1, g2, g_mix, sc2, (f1, gt1, 0.5), "mixer_norm_bwd")

    da1, db1, s1, dh1 = _ffn_bwd(df1, a1, b1, w1a, w3a, w2a, "ffn1_bwd")
    r11, r13, r12 = ffn_weight_grads(h1, da1, db1, s1, df1, "ffn1", [])
    ex, done = _ride([r11, r13, r12])
    gwin_full, *carried = _matmul(h2, dz, "tn", F32, 1024, _tile(IN_COLS, 1664), tk, "grad_w_in", ex=ex)
    done(carried)
    rm_in = _Reduction(jnp.transpose(jnp.reshape(gwin_full, (D, N_CHIPS, -1)), (1, 0, 2)), "w_in", c_arr, jc_arr)
    grad_x, dsh1, dsc1, dgn1 = _norm_bwd(dh1, xs, g1, g_ffn1, sc1, None, "ffn1_norm_bwd")
    ex, done = _ride([r13, r12, rm_in])
    gwout_full, *carried = _matmul(mix, dy, "tn", F32, 1024, 1024, tk, "grad_w_out", ex=ex)
    done(carried)
    rm_out = _Reduction(jnp.reshape(gwout_full, (N_CHIPS, -1, D)), "w_out", c_arr, jc_arr)
    for stage, riding in enumerate([[r12, rm_in, rm_out], [rm_in, rm_out], [rm_out]]):
        ex, done = _ride(riding)
        done(_exchange_alone(ex, f"reduce_tail_{stage}"))

    dmod = jnp.concatenate([dsh1, dsc1, dgt1, dsh2, dsc2, dgt2, dsh3, dsc3, dgt3], axis=1)
    small_w = [b_ada, g_ffn1, g_mix, g_ffn2, spatial_w, spatial_b, g_v, g_q, g_k, sinks, rel_bias]
    small_m = [m_b_ada, m_g_ffn1, m_g_mix, m_g_ffn2, m_spatial_w, m_spatial_b, m_g_v, m_g_q, m_g_k, m_sinks, m_rel_bias]
    small_v = [v_b_ada, v_g_ffn1, v_g_mix, v_g_ffn2, v_spatial_w, v_spatial_b, v_g_v, v_g_q, v_g_k, v_sinks, v_rel_bias]
    small_g = [dmod, dgn1, dgn2, dgn3, jnp.where(causal[None], dwm, 0.0), dsb, dgv, dgq, dgk, dsk, drel]
    shapes = [w.shape for w in small_w]
    gpack = _pack(small_g)
    rows = gpack.shape[0]
    gall = jnp.reshape(_allgather_small(gpack, "gather_small"), (N_DEV, rows, LANES))
    sg, sd, sm, sv = _small_update(gall, _pack(small_w), _pack(small_m), _pack(small_v), "small_update")
    sg, sd, sm, sv = [_unpack(p, shapes) for p in (sg, sd, sm, sv)]

    mod_rows = -(-N_MOD * D // (SUBLANES * LANES)) * SUBLANES
    dmod_all = jnp.reshape(gall[:, :mod_rows], (N_DEV, -1))[:, :N_MOD * D]
    dmod_sh = lax.dynamic_slice(dmod_all, (0, chip * mod_cols), (N_DEV, mod_cols))
    dmod_sh = jnp.pad(dmod_sh, ((0, BF16_ROWS - N_DEV), (0, 0))).astype(BF16)
    g_wada = _matmul(c_act, dmod_sh, "tn", F32, 1024, _tile(mod_cols, 512), BF16_ROWS, "grad_w_ada")
    d_wada, nm_wada, nv_wada = _adamw(w_ada[0], g_wada, m_w_ada[0], v_w_ada[0], "adamw_w_ada")

    gf1 = [r11.result, r13.result, r12.result]
    gf2 = [r21.result, r23.result, r22.result]
    gmx = [rm_in.result, rm_out.result]
    big = {}
    for nm, w, g, m, v in [
        ("w1_ffn1", w1_ffn1, gf1[0], m_w1_ffn1, v_w1_ffn1), ("w3_ffn1", w3_ffn1, gf1[1], m_w3_ffn1, v_w3_ffn1),
        ("w2_ffn1", w2_ffn1, gf1[2], m_w2_ffn1, v_w2_ffn1), ("w_in", w_in, gmx[0], m_w_in, v_w_in),
        ("w_out", w_out, gmx[1], m_w_out, v_w_out), ("w1_ffn2", w1_ffn2, gf2[0], m_w1_ffn2, v_w1_ffn2),
        ("w3_ffn2", w3_ffn2, gf2[1], m_w3_ffn2, v_w3_ffn2), ("w2_ffn2", w2_ffn2, gf2[2], m_w2_ffn2, v_w2_ffn2),
    ]:
        g_out, d, nm_, nv_ = _adamw(w[0], g, m[0], v[0], f"adamw_{nm}", emit_grad=True)
        big[nm] = (g_out[None], d[None], nm_[None], nv_[None])
    big["w_ada"] = (g_wada[None], d_wada[None], nm_wada[None], nv_wada[None])

    order = ["w_ada", "b_ada", "g_ffn1", "w1_ffn1", "w3_ffn1", "w2_ffn1", "g_mix", "w_in", "spatial_w", "spatial_b",
             "g_v", "g_q", "g_k", "sinks", "rel_bias", "w_out", "g_ffn2", "w1_ffn2", "w3_ffn2", "w2_ffn2"]
    small_names = ["b_ada", "g_ffn1", "g_mix", "g_ffn2", "spatial_w", "spatial_b", "g_v", "g_q", "g_k", "sinks", "rel_bias"]
    for i, nm in enumerate(small_names):
        big[nm] = (sg[i], sd[i], sm[i], sv[i])
    outs = [loss, grad_x[None]]
    for kind in range(4):
        outs += [big[nm][kind] for nm in order]
    return tuple(outs)
```

```python
import functools
import math

import jax
import jax.numpy as jnp
import numpy as np
from jax import lax
from jax.experimental import pallas as pl
from jax.experimental.pallas import tpu as pltpu

F32 = jnp.float32
BF16 = jnp.bfloat16
MESH = pl.DeviceIdType.MESH
ANY = pl.BlockSpec(memory_space=pl.ANY)

EPS = 1e-6
BLOCK = 128
A_HEADS = 8
A_DIM = 128
A_WIDTH = A_HEADS * A_DIM
B_HEADS = 16
KV_HEADS = 2
GROUP = B_HEADS // KV_HEADS
HEAD_DIM = 64
B_WIDTH = B_HEADS * HEAD_DIM
KV_WIDTH = KV_HEADS * HEAD_DIM
Q_OFF = 2 * A_WIDTH
K_OFF = Q_OFF + B_WIDTH
V_OFF = K_OFF + KV_WIDTH
IN_COLS = V_OFF + KV_WIDTH
N_BUCKETS = 32
MAX_DISTANCE = 128
N_MOD = 9
N_CHIPS = 4
N_DEV = 8
NEG = -1e30

ADAM_LR = 0.001
ADAM_B1 = 0.9
ADAM_B2 = 0.999
ADAM_EPS = 1e-08
ADAM_WD = 0.01
ADAM_STEP = 10

LANES = 128
SUBLANES = 8
BF16_ROWS = 16
VMEM_LIMIT = 60 * 1024 * 1024

INV_SQRT2 = 1.0 / math.sqrt(2.0)
INV_SQRT_2PI = 1.0 / math.sqrt(2.0 * math.pi)


def _tile(n, pref, mult=LANES):
    t = (min(pref, n) // mult) * mult
    while t >= mult:
        if n % t == 0:
            return t
        t -= mult
    return n


def _params(sem):
    return pltpu.CompilerParams(dimension_semantics=sem, vmem_limit_bytes=VMEM_LIMIT)


class _Exchange:
    def __init__(self, operands, out_shapes, aliases, n_sems, plan):
        self.operands, self.out_shapes, self.aliases, self.n_sems, self.plan = operands, out_shapes, aliases, n_sems, plan


def _pallas(body, *, name, grid, in_specs, out_specs, out_shape, args, scratch_shapes=(), semantics=None, ex=None):
    if ex is None:
        return pl.pallas_call(body, name=name, grid=grid, in_specs=in_specs, out_specs=out_specs, out_shape=out_shape,
                              scratch_shapes=list(scratch_shapes), compiler_params=_params(semantics))(*args)
    n_in, n_out, n_scr = len(in_specs), len(out_specs), len(scratch_shapes)
    e_in, e_out = len(ex.operands), len(ex.out_shapes)

    def wrapped(*refs):
        ins, refs = refs[:n_in], refs[n_in:]
        ex_ins, refs = refs[:e_in], refs[e_in:]
        outs, refs = refs[:n_out], refs[n_out:]
        ex_outs, refs = refs[:e_out], refs[e_out:]
        scratch, (send_sems, recv_sems) = refs[:n_scr], refs[n_scr:]
        first, last = True, True
        for d, size in enumerate(grid):
            first = jnp.logical_and(first, pl.program_id(d) == 0)
            last = jnp.logical_and(last, pl.program_id(d) == size - 1)

        def start():
            sends, _ = ex.plan(ex_ins, ex_outs, send_sems, recv_sems)
            for cp in sends:
                _remote(*cp).start()

        def finish():
            sends, arrivals = ex.plan(ex_ins, ex_outs, send_sems, recv_sems)
            for cp in arrivals:
                _remote(*cp).wait_recv()
            for cp in sends:
                _remote(*cp).wait_send()

        if grid:
            pl.when(first)(start)
        else:
            start()
        if body is not None:
            body(*ins, *outs, *scratch)
        if grid:
            pl.when(last)(finish)
        else:
            finish()

    kwargs = dict(grid=grid) if grid else {}
    return pl.pallas_call(
        wrapped,
        name=name,
        in_specs=list(in_specs) + [ANY] * e_in,
        out_specs=list(out_specs) + [ANY] * e_out,
        out_shape=list(out_shape) + list(ex.out_shapes),
        input_output_aliases={n_in + i: n_out + o for i, o in ex.aliases.items()},
        scratch_shapes=list(scratch_shapes) + [pltpu.SemaphoreType.DMA((ex.n_sems,)), pltpu.SemaphoreType.DMA((ex.n_sems,))],
        compiler_params=_params(("arbitrary",) * len(grid) if grid else None),
        **kwargs,
    )(*args, *ex.operands)


def _dot(a, b, dims=(((1,), (0,)), ((), ()))):
    return lax.dot_general(a, b, dims, preferred_element_type=F32)


NN = (((1,), (0,)), ((), ()))
NT = (((1,), (1,)), ((), ()))
TN = (((0,), (0,)), ((), ()))


def _sigmoid(x):
    return 1.0 / (1.0 + jnp.exp(-x))


def _gelu_and_grad(x):
    cdf = 0.5 * (1.0 + lax.erf(x * INV_SQRT2))
    pdf = jnp.exp(-0.5 * x * x) * INV_SQRT_2PI
    return x * cdf, cdf + x * pdf


def _gelu(x):
    return x * (0.5 * (1.0 + lax.erf(x * INV_SQRT2)))


def _rms(x):
    r = lax.rsqrt(jnp.mean(x * x, axis=-1, keepdims=True) + EPS)
    return x * r, r


ROW_CHUNK = 64


def _for_rows(tm, fn):
    rc = min(ROW_CHUNK, tm)

    def step(r, carry):
        fn(pl.ds(pl.multiple_of(r * rc, rc), rc))
        return carry

    lax.fori_loop(0, tm // rc, step, 0)


def _rms_bwd(dy, xhat, r):
    return r * (dy - xhat * jnp.mean(dy * xhat, axis=-1, keepdims=True))


def _matmul(a, b, mode, out_dtype, tm, tn, tk, name, shard_major=False, ex=None):
    if mode == "nn":
        (M, K), N = a.shape, b.shape[1]
    elif mode == "nt":
        (M, K), N = a.shape, b.shape[0]
    else:
        (K, M), N = a.shape, b.shape[1]
    tm, tn, tk = min(tm, M), min(tn, N), min(tk, K)
    assert M % tm == 0 and N % tn == 0 and K % tk == 0, (name, M, N, K, tm, tn, tk)
    nk = K // tk
    dims = {"nn": NN, "nt": NT, "tn": TN}[mode]
    a_spec = pl.BlockSpec((tk, tm), lambda i, j, k: (k, i)) if mode == "tn" else pl.BlockSpec((tm, tk), lambda i, j, k: (i, k))
    b_spec = pl.BlockSpec((tn, tk), lambda i, j, k: (j, k)) if mode == "nt" else pl.BlockSpec((tk, tn), lambda i, j, k: (k, j))
    if shard_major:
        assert tn * N_CHIPS == N
        out_shape = jax.ShapeDtypeStruct((N_CHIPS, M, tn), out_dtype)
        o_spec = pl.BlockSpec((None, tm, tn), lambda i, j, k: (j, i, 0))
    else:
        out_shape = jax.ShapeDtypeStruct((M, N), out_dtype)
        o_spec = pl.BlockSpec((tm, tn), lambda i, j, k: (i, j))

    direct = nk == 1 or out_dtype == F32

    def body(a_ref, b_ref, o_ref, *scratch):
        k = pl.program_id(2)
        if nk == 1:
            o_ref[...] = _dot(a_ref[...], b_ref[...], dims).astype(o_ref.dtype)
            return
        acc_ref = o_ref if direct else scratch[0]

        @pl.when(k == 0)
        def _():
            acc_ref[...] = jnp.zeros(acc_ref.shape, F32)

        acc_ref[...] += _dot(a_ref[...], b_ref[...], dims)
        if not direct:
            @pl.when(k == nk - 1)
            def _():
                o_ref[...] = acc_ref[...].astype(o_ref.dtype)

    outs = _pallas(body, name=name, grid=(M // tm, N // tn, nk), in_specs=[a_spec, b_spec], out_specs=[o_spec],
                   out_shape=[out_shape], scratch_shapes=[] if direct else [pltpu.VMEM((tm, tn), F32)],
                   semantics=("parallel", "parallel", "arbitrary"), args=[a, b], ex=ex)
    return outs[0] if ex is None else outs


def _mod_partial(c_all, w_ada, b_sh, name):
    R, D = c_all.shape
    N = w_ada.shape[1]
    tn = _tile(N, 512)

    def body(c_ref, w_ref, b_ref, o_ref, ca_ref):
        cv = c_ref[...]
        ca = (cv * _sigmoid(cv)).astype(BF16)
        ca_ref[...] = ca
        o_ref[...] = _dot(ca, w_ref[...].astype(BF16)) + b_ref[...]

    return pl.pallas_call(
        body,
        name=name,
        grid=(N // tn,),
        in_specs=[
            pl.BlockSpec((R, D), lambda j: (0, 0)),
            pl.BlockSpec((D, tn), lambda j: (0, j)),
            pl.BlockSpec((1, tn), lambda j: (0, j)),
        ],
        out_specs=[pl.BlockSpec((R, tn), lambda j: (0, j)), pl.BlockSpec((R, D), lambda j: (0, 0))],
        out_shape=[jax.ShapeDtypeStruct((R, N), F32), jax.ShapeDtypeStruct((R, D), BF16)],
        compiler_params=_params(("arbitrary",)),
    )(c_all, w_ada, b_sh)


FFN_BLOCK = 1024


def _ffn_blocks(F):
    if F % FFN_BLOCK == 0 or F < FFN_BLOCK:
        tf = _tile(F, FFN_BLOCK)
        return tf, F // tf, tf
    nj = -(-F // FFN_BLOCK)
    tail = F - (nj - 1) * FFN_BLOCK
    assert tail % LANES == 0
    return FFN_BLOCK, nj, tail
def _ffn_fwd(x, g, sh, sc, gt, w1, w3, w2, tgt, name, ex=None):
    S, D = x.shape
    F = w1.shape[1]
    tm, tf, nj, tail = _tile(S, 512), *_ffn_blocks(F)
    ni = S // tm
    with_loss = tgt is not None

    def body(*refs):
        if with_loss:
            (x_ref, g_ref, sh_ref, sc_ref, gt_ref, w1_ref, w3_ref, w2_ref, tgt_ref,
             gout_ref, df_ref, h_ref, a_ref, b_ref, dgt_ref, loss_ref, hs_ref, acc_ref) = refs
        else:
            (x_ref, g_ref, sh_ref, sc_ref, gt_ref, w1_ref, w3_ref, w2_ref,
             xo_ref, h_ref, a_ref, b_ref, f_ref, hs_ref, acc_ref) = refs
        i, j = pl.program_id(0), pl.program_id(1)

        @pl.when(j == 0)
        def _():
            def prologue(rows):
                xhat, _ = _rms(x_ref[rows, :])
                hb = ((xhat * g_ref[...]) * (1.0 + sc_ref[...]) + sh_ref[...]).astype(BF16)
                hs_ref[rows, :] = hb
                h_ref[rows, :] = hb

            _for_rows(tm, prologue)

        @pl.when(j == 0)
        def _():
            acc_ref[...] = jnp.zeros(acc_ref.shape, F32)

        def columns(width):
            def run():
                hb = hs_ref[...]
                av = _dot(hb, w1_ref[:, :width])
                bv = _dot(hb, w3_ref[:, :width])
                a_ref[:, :width] = av.astype(BF16)
                b_ref[:, :width] = bv.astype(BF16)
                sv = ((av * _sigmoid(av)) * bv).astype(BF16)
                acc_ref[...] += _dot(sv, w2_ref[:width, :])
            return run

        if tail == tf:
            columns(tf)()
        else:
            pl.when(j < nj - 1)(columns(tf))
            pl.when(j == nj - 1)(columns(tail))

        @pl.when(j == nj - 1)
        def _():
            if with_loss:
                @pl.when(i == 0)
                def _():
                    dgt_ref[...] = jnp.zeros(dgt_ref.shape, F32)
                    loss_ref[...] = jnp.zeros(loss_ref.shape, F32)

            def epilogue(rows):
                fv = acc_ref[rows, :]
                half_gate = 0.5 * gt_ref[...]
                xo = x_ref[rows, :] + half_gate * fv
                if not with_loss:
                    xo_ref[rows, :] = xo
                    f_ref[rows, :] = fv.astype(f_ref.dtype)
                    return
                err = xo - tgt_ref[rows, :]
                gout = err * (1.0 / D)
                gout_ref[rows, :] = gout
                df_ref[rows, :] = (half_gate * gout).astype(BF16)
                dgt_ref[...] += 0.5 * jnp.sum(gout * fv, axis=0, keepdims=True)
                loss_part = jnp.sum(jnp.sum(err * err, axis=1, keepdims=True), axis=0, keepdims=True)
                loss_ref[...] += jnp.broadcast_to(loss_part, loss_ref.shape)

            _for_rows(tm, epilogue)

    row = pl.BlockSpec((tm, D), lambda i, j: (i, 0))
    row_in = row_one = pl.BlockSpec((tm, D), lambda i, j: (i, 0), pipeline_mode=pl.Buffered(1))
    vec = pl.BlockSpec((1, D), lambda i, j: (0, 0))
    col = pl.BlockSpec((tm, tf), lambda i, j: (i, j))
    in_specs = [row_in, vec, vec, vec, vec,
                pl.BlockSpec((D, tf), lambda i, j: (0, j)),
                pl.BlockSpec((D, tf), lambda i, j: (0, j)),
                pl.BlockSpec((tf, D), lambda i, j: (j, 0))]
    args = [x, g, sh, sc, gt, w1, w3, w2]
    act = jax.ShapeDtypeStruct((S, F), BF16)
    if with_loss:
        in_specs.append(row_in)
        args.append(tgt)
        out_specs = [row, row_one, row_one, col, col, vec, pl.BlockSpec((1, LANES), lambda i, j: (0, 0))]
        out_shape = [jax.ShapeDtypeStruct((S, D), F32), jax.ShapeDtypeStruct((S, D), BF16),
                     jax.ShapeDtypeStruct((S, D), BF16), act, act,
                     jax.ShapeDtypeStruct((1, D), F32), jax.ShapeDtypeStruct((1, LANES), F32)]
    else:
        out_specs = [row, row, col, col, row]
        out_shape = [jax.ShapeDtypeStruct((S, D), F32), jax.ShapeDtypeStruct((S, D), BF16), act, act,
                     jax.ShapeDtypeStruct((S, D), BF16)]
    return _pallas(body, name=name, grid=(ni, nj), in_specs=in_specs, out_specs=out_specs, out_shape=out_shape,
                   scratch_shapes=[pltpu.VMEM((tm, D), BF16), pltpu.VMEM((tm, D), F32)],
                   semantics=("arbitrary", "arbitrary"), args=args, ex=ex)


def _ffn_bwd(df, a, b, w1, w3, w2, name, ex=None):
    S, D = df.shape
    F = a.shape[1]
    tm, tf, nj, tail = _tile(S, 512), *_ffn_blocks(F)

    def body(df_ref, a_ref, b_ref, w1_ref, w3_ref, w2_ref, da_ref, db_ref, s_ref, dh_ref):
        j = pl.program_id(1)

        @pl.when(j == 0)
        def _():
            dh_ref[...] = jnp.zeros(dh_ref.shape, F32)

        def columns(width):
            def run():
                ds = _dot(df_ref[...], w2_ref[:width, :], NT)
                av = a_ref[:, :width].astype(F32)
                bv = b_ref[:, :width].astype(F32)
                sig = _sigmoid(av)
                sil = av * sig
                da = ((ds * bv) * (sig * (1.0 + av * (1.0 - sig)))).astype(BF16)
                db = (ds * sil).astype(BF16)
                da_ref[:, :width] = da
                db_ref[:, :width] = db
                s_ref[:, :width] = (sil * bv).astype(BF16)
                dh_ref[...] += _dot(da, w1_ref[:, :width], NT) + _dot(db, w3_ref[:, :width], NT)
            return run

        if tail == tf:
            columns(tf)()
        else:
            pl.when(j < nj - 1)(columns(tf))
            pl.when(j == nj - 1)(columns(tail))

    row = pl.BlockSpec((tm, D), lambda i, j: (i, 0))
    col = pl.BlockSpec((tm, tf), lambda i, j: (i, j))
    act = jax.ShapeDtypeStruct((S, F), BF16)
    return _pallas(body, name=name, grid=(S // tm, nj),
                   in_specs=[row, col, col,
                             pl.BlockSpec((D, tf), lambda i, j: (0, j)),
                             pl.BlockSpec((D, tf), lambda i, j: (0, j)),
                             pl.BlockSpec((tf, D), lambda i, j: (j, 0))],
                   out_specs=[col, col, col, row],
                   out_shape=[act, act, act, jax.ShapeDtypeStruct((S, D), F32)],
                   semantics=("parallel", "arbitrary"), args=[df, a, b, w1, w3, w2], ex=ex)


def _norm_mod(x, g, sh, sc, name, ex=None):
    S, D = x.shape
    tm = _tile(S, 512)

    def body(x_ref, g_ref, sh_ref, sc_ref, h_ref):
        def step(rows):
            xhat, _ = _rms(x_ref[rows, :])
            h_ref[rows, :] = ((xhat * g_ref[...]) * (1.0 + sc_ref[...]) + sh_ref[...]).astype(BF16)

        _for_rows(tm, step)

    row = pl.BlockSpec((tm, D), lambda i: (i, 0))
    vec = pl.BlockSpec((1, D), lambda i: (0, 0))
    outs = _pallas(body, name=name, grid=(S // tm,), in_specs=[row, vec, vec, vec], out_specs=[row],
                   out_shape=[jax.ShapeDtypeStruct((S, D), BF16)], semantics=("parallel",), args=[x, g, sh, sc], ex=ex)
    return outs[0] if ex is None else outs


def _norm_bwd(dh, x, gres, g, sc, prev, name, ex=None):
    S, D = x.shape
    tm = _tile(S, 256)
    has_prev = prev is not None
    coef = prev[2] if has_prev else None

    def body(*refs):
        if has_prev:
            (dh_ref, x_ref, gr_ref, g_ref, sc_ref, f_ref, gt_ref,
             go_ref, dsh_ref, dsc_ref, dg_ref, dp_ref, dgt_ref) = refs
        else:
            dh_ref, x_ref, gr_ref, g_ref, sc_ref, go_ref, dsh_ref, dsc_ref, dg_ref = refs
        sum_refs = [dsh_ref, dsc_ref, dg_ref] + ([dgt_ref] if has_prev else [])

        @pl.when(pl.program_id(0) == 0)
        def _():
            for ref in sum_refs:
                ref[...] = jnp.zeros(ref.shape, F32)

        def step(rows):
            dh = dh_ref[rows, :]
            xhat, r = _rms(x_ref[rows, :])
            gain = g_ref[...]
            scale1 = 1.0 + sc_ref[...]
            gout = gr_ref[rows, :] + _rms_bwd(dh * scale1 * gain, xhat, r)
            go_ref[rows, :] = gout
            sums = [dh, dh * (xhat * gain), dh * scale1 * xhat]
            if has_prev:
                dp_ref[rows, :] = ((coef * gt_ref[...]) * gout).astype(BF16)
                sums.append(coef * (gout * f_ref[rows, :].astype(F32)))
            for ref, v in zip(sum_refs, sums):
                ref[...] += jnp.sum(v, axis=0, keepdims=True)

        _for_rows(tm, step)

    row = pl.BlockSpec((tm, D), lambda i: (i, 0))
    vec = pl.BlockSpec((1, D), lambda i: (0, 0))
    vshape = jax.ShapeDtypeStruct((1, D), F32)
    in_specs = [row, row, row, vec, vec]
    args = [dh, x, gres, g, sc]
    out_specs = [row, vec, vec, vec]
    out_shape = [jax.ShapeDtypeStruct((S, D), F32), vshape, vshape, vshape]
    if has_prev:
        in_specs += [row, vec]
        args += [prev[0], prev[1]]
        out_specs += [row, vec]
        out_shape += [jax.ShapeDtypeStruct((S, D), BF16), vshape]
    return _pallas(body, name=name, grid=(S // tm,), in_specs=in_specs, out_specs=out_specs, out_shape=out_shape,
                   semantics=("arbitrary",), args=args, ex=ex)


PAIRS = B_HEADS // 2
PAIR_ROWS = (PAIRS // KV_HEADS) * BLOCK
BAND = 2 * BLOCK


def _stack(ref, offset, count):
    return jnp.concatenate([ref[:, offset + p * LANES:offset + (p + 1) * LANES] for p in range(count)], axis=0)


def _seg_mean(x, e_ref):
    return lax.dot_general(x, e_ref[...], NN, precision=lax.Precision.HIGH,
                           preferred_element_type=F32) * (1.0 / HEAD_DIM)


def _block_diag(x, x_rolled, left, kv_head):
    if kv_head == 0:
        top, bottom = jnp.where(left, x, 0.0), jnp.where(left, 0.0, x_rolled)
    else:
        top, bottom = jnp.where(left, x_rolled, 0.0), jnp.where(left, 0.0, x)
    return jnp.concatenate([top, bottom], axis=0).astype(BF16)


def _from_block_diag(g, left, kv_head):
    a, b = g[:BAND], g[BAND:]
    if kv_head == 0:
        return jnp.where(left, a + pltpu.roll(b, HEAD_DIM, 1), 0.0)
    return jnp.where(left, 0.0, pltpu.roll(a, HEAD_DIM, 1) + b)


def _pair_softmax(st, sk_ref, kv_head):
    out = []
    for e in range(2):
        seg = st[e * BAND:(e + 1) * BAND]
        sink = jnp.concatenate([jnp.full((1, BLOCK), sk_ref[kv_head * GROUP + 2 * p + e], F32)
                                for p in range(PAIRS // KV_HEADS)], axis=1)
        m = jnp.maximum(jnp.max(seg, axis=0, keepdims=True), sink)
        p_ = jnp.exp(seg - m)
        e_sink = jnp.exp(sink - m)
        inv = 1.0 / (jnp.sum(p_, axis=0, keepdims=True) + e_sink)
        out.append((p_ * inv, e_sink * inv))
    return out


def _lane_mean(x, ones_ref):
    return lax.dot_general(x, ones_ref[...], NN, precision=lax.Precision.HIGH, preferred_element_type=F32)


def _mixer_specs(nb, last):
    full = lambda shape: pl.BlockSpec(shape, lambda n: (0,) * len(shape))
    z_spec = pl.BlockSpec((BLOCK, IN_COLS), lambda n: (jnp.minimum(n, last), 0))
    zp_spec = pl.BlockSpec((BLOCK, 2 * KV_WIDTH), lambda n: (jnp.clip(n - 1, 0, last), K_OFF // (2 * KV_WIDTH)))
    consts = [full((A_HEADS * BLOCK, A_DIM)), full((A_HEADS * BLOCK, A_DIM)), full((1, LANES)), full((1, LANES)),
              full((LANES, LANES)), full((LANES, LANES)), pl.BlockSpec(memory_space=pltpu.SMEM),
              pl.BlockSpec((None, KV_HEADS, PAIR_ROWS, 2 * BAND), lambda n: (jnp.minimum(n, 1), 0, 0, 0))]
    return full, z_spec, zp_spec, consts


def _mixer_fwd(z, wm, sbp, gvp, gq2, gk2, seg_ones, lane_ones, sinks, biasp, name):
    S = z.shape[0]
    nb = S // BLOCK

    def body(z_ref, zp_ref, wm_ref, sbp_ref, gvp_ref, gq2_ref, gk2_ref, e_ref, l_ref, sk_ref, bias_ref, mix_ref):
        u = _gelu(_stack(z_ref, 0, A_HEADS))
        v = _gelu(_stack(z_ref, A_WIDTH, A_HEADS))
        vhat = v * lax.rsqrt(_lane_mean(v * v, l_ref) + EPS)
        vn = (vhat * gvp_ref[...]).astype(BF16)
        mixed = jnp.concatenate([_dot(wm_ref[h], vn[h * BLOCK:(h + 1) * BLOCK]) for h in range(A_HEADS)], axis=0)
        ya = (u * (mixed + sbp_ref[...])).astype(BF16)
        for h in range(A_HEADS):
            mix_ref[:, h * A_DIM:(h + 1) * A_DIM] = ya[h * BLOCK:(h + 1) * BLOCK]

        left = lax.broadcasted_iota(jnp.int32, (1, LANES), 1) < HEAD_DIM
        kv = jnp.concatenate([zp_ref[...], z_ref[:, K_OFF:K_OFF + 2 * KV_WIDTH]], axis=0)
        k2, v2 = kv[:, :KV_WIDTH], kv[:, KV_WIDTH:]
        kn2 = k2 * lax.rsqrt(_seg_mean(k2 * k2, e_ref) + EPS) * gk2_ref[...]
        kn2_r, v2_r = pltpu.roll(kn2, HEAD_DIM, 1), pltpu.roll(v2, HEAD_DIM, 1)
        qp = _stack(z_ref, Q_OFF, PAIRS)
        qn = (qp * lax.rsqrt(_seg_mean(qp * qp, e_ref) + EPS) * gq2_ref[...]).astype(BF16)
        for kh in range(KV_HEADS):
            kbd, vbd = _block_diag(kn2, kn2_r, left, kh), _block_diag(v2, v2_r, left, kh)
            st = _dot(kbd, qn[kh * PAIR_ROWS:(kh + 1) * PAIR_ROWS], NT) * (HEAD_DIM ** -0.5) + bias_ref[kh]
            wt = jnp.concatenate([w_e for w_e, _ in _pair_softmax(st, sk_ref, kh)], axis=0).astype(BF16)
            o = _dot(wt, vbd, TN).astype(BF16)
            for p in range(PAIRS // KV_HEADS):
                col = A_WIDTH + (kh * (PAIRS // KV_HEADS) + p) * LANES
                mix_ref[:, col:col + LANES] = o[p * BLOCK:(p + 1) * BLOCK]

    full, z_spec, zp_spec, consts = _mixer_specs(nb, nb - 1)
    return pl.pallas_call(
        body,
        name=name,
        grid=(nb,),
        in_specs=[z_spec, zp_spec, full((A_HEADS, BLOCK, BLOCK))] + consts,
        out_specs=pl.BlockSpec((BLOCK, A_WIDTH + B_WIDTH), lambda n: (n, 0)),
        out_shape=jax.ShapeDtypeStruct((S, A_WIDTH + B_WIDTH), BF16),
        compiler_params=_params(("parallel",)),
    )(z, z, wm, sbp, gvp, gq2, gk2, seg_ones, lane_ones, sinks, biasp)


def _mixer_bwd(z, dmix, wm, wm_t, sbp, gvp, gq2, gk2, seg_ones, lane_ones, sinks, biasp, pair_fold, name, ex=None):
    S = z.shape[0]
    nb = S // BLOCK

    def body(z_ref, zp_ref, dmix_ref, wm_ref, wmt_ref, sbp_ref, gvp_ref, gq2_ref, gk2_ref, e_ref, l_ref, sk_ref,
             bias_ref, fold_ref,
             dz_ref, dzkv_ref, dwm_ref, dsb_ref, dgv_ref, dgq_ref, dgk_ref, dsk_ref, dst_ref,
             carry_ref, tot_ref, sbacc_ref, skacc_ref, gqacc_ref, gkacc_ref):
        n = pl.program_id(0)
        left = lax.broadcasted_iota(jnp.int32, (1, LANES), 1) < HEAD_DIM

        @pl.when(n == 0)
        def _():
            for ref in (dwm_ref, dgv_ref, dst_ref, carry_ref, sbacc_ref, skacc_ref, gqacc_ref, gkacc_ref):
                ref[...] = jnp.zeros(ref.shape, ref.dtype)

        @pl.when(n < nb)
        def _():
            u, du_dz = _gelu_and_grad(_stack(z_ref, 0, A_HEADS))
            v, dv_dz = _gelu_and_grad(_stack(z_ref, A_WIDTH, A_HEADS))
            rv = lax.rsqrt(_lane_mean(v * v, l_ref) + EPS)
            vhat = v * rv
            gvp = gvp_ref[...]
            vn = (vhat * gvp).astype(BF16)
            rows = lambda a, h: a[h * BLOCK:(h + 1) * BLOCK]
            mixed = jnp.concatenate([_dot(wm_ref[h], rows(vn, h)) for h in range(A_HEADS)], axis=0) + sbp_ref[...]
            dya = _stack(dmix_ref, 0, A_HEADS).astype(F32)
            dmx = dya * u
            sbacc_ref[...] += dmx
            dmx_b = dmx.astype(BF16)
            for h in range(A_HEADS):
                dwm_ref[h] += _dot(rows(dmx_b, h), rows(vn, h), NT)
            dvn = jnp.concatenate([_dot(wmt_ref[h], rows(dmx_b, h)) for h in range(A_HEADS)], axis=0)
            dgv_ref[...] += jnp.sum(jnp.reshape(dvn * vhat, (A_HEADS, BLOCK, A_DIM)), axis=1)
            dzu = ((dya * mixed) * du_dz).astype(BF16)
            tv = dvn * gvp
            dzv = ((rv * (tv - vhat * _lane_mean(tv * vhat, l_ref))) * dv_dz).astype(BF16)
            for h in range(A_HEADS):
                dz_ref[:, h * A_DIM:(h + 1) * A_DIM] = rows(dzu, h)
                dz_ref[:, A_WIDTH + h * A_DIM:A_WIDTH + (h + 1) * A_DIM] = rows(dzv, h)

            kv = jnp.concatenate([zp_ref[...], z_ref[:, K_OFF:K_OFF + 2 * KV_WIDTH]], axis=0)
            k2, v2 = kv[:, :KV_WIDTH], kv[:, KV_WIDTH:]
            kn2 = k2 * lax.rsqrt(_seg_mean(k2 * k2, e_ref) + EPS) * gk2_ref[...]
            kn2_r, v2_r = pltpu.roll(kn2, HEAD_DIM, 1), pltpu.roll(v2, HEAD_DIM, 1)
            qp = _stack(z_ref, Q_OFF, PAIRS)
            rq = lax.rsqrt(_seg_mean(qp * qp, e_ref) + EPS)
            qhat = qp * rq
            gq2 = gq2_ref[...]
            qn = (qhat * gq2).astype(BF16)
            dop = _stack(dmix_ref, A_WIDTH, PAIRS)
            dqn_parts = []
            dk2n = jnp.zeros((BAND, KV_WIDTH), F32)
            dv2 = jnp.zeros((BAND, KV_WIDTH), F32)
            for kh in range(KV_HEADS):
                kbd, vbd = _block_diag(kn2, kn2_r, left, kh), _block_diag(v2, v2_r, left, kh)
                qg = qn[kh * PAIR_ROWS:(kh + 1) * PAIR_ROWS]
                dog = dop[kh * PAIR_ROWS:(kh + 1) * PAIR_ROWS]
                st = _dot(kbd, qg, NT) * (HEAD_DIM ** -0.5) + bias_ref[kh]
                halves = _pair_softmax(st, sk_ref, kh)
                dpt = _dot(vbd, dog, NT)
                ds_halves, t_halves = [], []
                for e, (w_e, w_sink) in enumerate(halves):
                    dp_e = dpt[e * BAND:(e + 1) * BAND]
                    delta = jnp.sum(w_e * dp_e, axis=0, keepdims=True)
                    ds_halves.append(w_e * (dp_e - delta))
                    t_halves.append(-(w_sink * delta))
                dst = jnp.concatenate(ds_halves, axis=0)
                dst_ref[kh] += dst
                skacc_ref[2 * kh:2 * kh + 2, :] += jnp.concatenate(t_halves, axis=0)
                ds_b = (dst * (HEAD_DIM ** -0.5)).astype(BF16)
                w_b = jnp.concatenate([w_e for w_e, _ in halves], axis=0).astype(BF16)
                dqn_parts.append(_dot(ds_b, kbd, TN))
                dk2n += _from_block_diag(_dot(ds_b, qg), left, kh)
                dv2 += _from_block_diag(_dot(w_b, dog), left, kh)
            dqn = jnp.concatenate(dqn_parts, axis=0)
            gqacc_ref[...] += jnp.sum(dqn * qhat, axis=0, keepdims=True)
            t = dqn * gq2
            dzq = (rq * (t - qhat * _seg_mean(t * qhat, e_ref))).astype(BF16)
            for p in range(PAIRS):
                dz_ref[:, Q_OFF + p * LANES:Q_OFF + (p + 1) * LANES] = rows(dzq, p)
            tot_ref[0] = carry_ref[0] + dk2n[:BLOCK]
            tot_ref[1] = carry_ref[1] + dv2[:BLOCK]
            carry_ref[0] = dk2n[BLOCK:]
            carry_ref[1] = dv2[BLOCK:]

        @pl.when(n == nb)
        def _():
            tot_ref[...] = carry_ref[...]

        kp = zp_ref[:, :KV_WIDTH]
        rk = lax.rsqrt(_seg_mean(kp * kp, e_ref) + EPS)
        khat = kp * rk
        dkn = tot_ref[0]
        gkacc_ref[...] += jnp.sum(dkn * khat, axis=0, keepdims=True)
        t = dkn * gk2_ref[...]
        dzkv_ref[:, :KV_WIDTH] = (rk * (t - khat * _seg_mean(t * khat, e_ref))).astype(BF16)
        dzkv_ref[:, KV_WIDTH:] = tot_ref[1].astype(BF16)

        @pl.when(n == nb)
        def _():
            dsb_ref[...] = jnp.broadcast_to(jnp.sum(sbacc_ref[...], axis=1, keepdims=True), dsb_ref.shape)
            dsk_ref[...] = lax.dot_general(skacc_ref[...], fold_ref[...], NN, precision=lax.Precision.HIGHEST,
                                           preferred_element_type=F32)
            dgq_ref[...] = gqacc_ref[...] + pltpu.roll(gqacc_ref[...], HEAD_DIM, 1)
            dgk_ref[...] = gkacc_ref[...] + pltpu.roll(gkacc_ref[...], HEAD_DIM, 1)

    last = nb - 1
    full, z_spec, zp_spec, consts = _mixer_specs(nb, last)
    return _pallas(
        body,
        name=name,
        grid=(nb + 1,),
        ex=ex,
        in_specs=[z_spec, zp_spec, pl.BlockSpec((BLOCK, A_WIDTH + B_WIDTH), lambda n: (jnp.minimum(n, last), 0)),
                  full((A_HEADS, BLOCK, BLOCK)), full((A_HEADS, BLOCK, BLOCK))] + consts + [full((PAIR_ROWS, LANES))],
        out_specs=[
            pl.BlockSpec((BLOCK, K_OFF), lambda n: (jnp.minimum(n, last), 0)),
            pl.BlockSpec((BLOCK, 2 * KV_WIDTH), lambda n: (jnp.maximum(n - 1, 0), 0)),
            full((A_HEADS, BLOCK, BLOCK)), full((A_HEADS * BLOCK, A_DIM)), full((A_HEADS, A_DIM)),
            full((1, LANES)), full((1, LANES)), full((SUBLANES, LANES)),
            full((KV_HEADS, PAIR_ROWS, 2 * BAND)),
        ],
        out_shape=[
            jax.ShapeDtypeStruct((S, K_OFF), BF16),
            jax.ShapeDtypeStruct((S, 2 * KV_WIDTH), BF16),
            jax.ShapeDtypeStruct((A_HEADS, BLOCK, BLOCK), F32),
            jax.ShapeDtypeStruct((A_HEADS * BLOCK, A_DIM), F32),
            jax.ShapeDtypeStruct((A_HEADS, A_DIM), F32),
            jax.ShapeDtypeStruct((1, LANES), F32),
            jax.ShapeDtypeStruct((1, LANES), F32),
            jax.ShapeDtypeStruct((SUBLANES, LANES), F32),
            jax.ShapeDtypeStruct((KV_HEADS, PAIR_ROWS, 2 * BAND), F32),
        ],
        scratch_shapes=[
            pltpu.VMEM((2, BLOCK, KV_WIDTH), F32),
            pltpu.VMEM((2, BLOCK, KV_WIDTH), F32),
            pltpu.VMEM((A_HEADS * BLOCK, A_DIM), F32),
            pltpu.VMEM((SUBLANES, PAIR_ROWS), F32),
            pltpu.VMEM((1, LANES), F32),
            pltpu.VMEM((1, LANES), F32),
        ],
        semantics=("arbitrary",),
        args=[z, z, dmix, wm, wm_t, sbp, gvp, gq2, gk2, seg_ones, lane_ones, sinks, biasp, pair_fold],
    )


def _mixer_out(mix, w_out, x, gt, name):
    S, D = x.shape
    K = mix.shape[1]
    tm, tn = _tile(S, 1024), _tile(D, 1024)

    def body(m_ref, w_ref, x_ref, gt_ref, xo_ref, y_ref):
        y = _dot(m_ref[...], w_ref[...])
        y_ref[...] = y.astype(BF16)
        xo_ref[...] = x_ref[...] + gt_ref[...] * y

    blk = pl.BlockSpec((tm, tn), lambda j, i: (i, j))
    return pl.pallas_call(
        body,
        name=name,
        grid=(D // tn, S // tm),
        in_specs=[pl.BlockSpec((tm, K), lambda j, i: (i, 0)), pl.BlockSpec((K, tn), lambda j, i: (0, j)),
                  blk, pl.BlockSpec((1, tn), lambda j, i: (0, j))],
        out_specs=[blk, blk],
        out_shape=[jax.ShapeDtypeStruct((S, D), F32), jax.ShapeDtypeStruct((S, D), BF16)],
        compiler_params=_params(("parallel", "parallel")),
    )(mix, w_out, x, gt)


def _bucket_sum(dst, onehot, name):
    def body(d_ref, o_ref, out_ref):
        out_ref[...] = lax.dot_general(o_ref[...], d_ref[...], NT, precision=lax.Precision.HIGHEST,
                                       preferred_element_type=F32)

    return pl.pallas_call(
        body,
        name=name,
        out_shape=jax.ShapeDtypeStruct((N_BUCKETS, B_HEADS), F32),
    )(dst, onehot)


def _adamw_math(w, g, m, v):
    m = ADAM_B1 * m + (1.0 - ADAM_B1) * g
    v = ADAM_B2 * v + (1.0 - ADAM_B2) * (g * g)
    m_hat = m / (1.0 - ADAM_B1 ** ADAM_STEP)
    v_hat = v / (1.0 - ADAM_B2 ** ADAM_STEP)
    delta = -ADAM_LR * (m_hat / (jnp.sqrt(v_hat) + ADAM_EPS) + ADAM_WD * w)
    return delta, m, v


def _adamw(w, g, m, v, name, ex=None, emit_grad=False):
    R, C = w.shape
    tr = _tile(R, max(SUBLANES, (1 << 19) // C), SUBLANES)

    def body(w_ref, g_ref, m_ref, v_ref, *out_refs):
        gv = g_ref[...]
        results = _adamw_math(w_ref[...], gv, m_ref[...], v_ref[...])
        for ref, val in zip(out_refs, ((gv,) if emit_grad else ()) + results):
            ref[...] = val

    blk = pl.BlockSpec((tr, C), lambda i: (i, 0))
    shape = jax.ShapeDtypeStruct((R, C), F32)
    n_out = 4 if emit_grad else 3
    return _pallas(body, name=name, grid=(R // tr,), in_specs=[blk] * 4, out_specs=[blk] * n_out,
                   out_shape=[shape] * n_out, semantics=("parallel",), args=[w, g, m, v], ex=ex)


def _small_update(parts, w, m, v, name):
    R = w.shape[0]

    def body(p_ref, w_ref, m_ref, v_ref, g_ref, d_ref, mo_ref, vo_ref):
        g = p_ref[0]
        for dev in range(1, N_DEV):
            g = g + p_ref[dev]
        g_ref[...] = g
        d, mn, vn = _adamw_math(w_ref[...], g, m_ref[...], v_ref[...])
        d_ref[...] = d
        mo_ref[...] = mn
        vo_ref[...] = vn

    shape = jax.ShapeDtypeStruct((R, LANES), F32)
    return pl.pallas_call(
        body,
        name=name,
        out_shape=[shape] * 4,
        compiler_params=pltpu.CompilerParams(vmem_limit_bytes=VMEM_LIMIT),
    )(parts, w, m, v)


def _place():
    x, y, c = lax.axis_index("x"), lax.axis_index("y"), lax.axis_index("c")
    chips = [(1 - x, y), (x, 1 - y), (1 - x, 1 - y)]
    return x, y, c, chips


def _remote(src, dst, send_sem, recv_sem, to):
    return pltpu.make_async_remote_copy(src_ref=src, dst_ref=dst, send_sem=send_sem, recv_sem=recv_sem,
                                        device_id=to, device_id_type=MESH)


def _allgather_small(block, name):
    m_per, n = block.shape

    def body(x_ref, out_ref, send_sems, recv_sems, local_sem):
        x, y, c, chips = _place()
        me, sibling = (x, y, c), (x, y, 1 - c)

        def rows(px, py, pc):
            return out_ref.at[pl.ds((4 * px + 2 * py + pc) * m_per, m_per), :]

        def copy(k, blk, to, src=None):
            return _remote(rows(*blk) if src is None else src, rows(*blk), send_sems.at[k], recv_sems.at[k], to)

        mine = pltpu.make_async_copy(x_ref, rows(*me), local_sem)
        mine.start()
        first = [copy(0, me, sibling, src=x_ref)]
        first += [copy(1 + j, me, (*chip, c), src=x_ref) for j, chip in enumerate(chips)]
        for cp in first:
            cp.start()
        passed = [copy(4 + j, (*chip, c), sibling) for j, chip in enumerate(chips)]
        for j, chip in enumerate(chips):
            copy(1 + j, (*chip, c), me).wait_recv()
            passed[j].start()
        copy(0, sibling, me).wait_recv()
        for j, chip in enumerate(chips):
            copy(4 + j, (*chip, 1 - c), me).wait_recv()
        for cp in first + passed:
            cp.wait_send()
        mine.wait()

    return pl.pallas_call(
        body,
        name=name,
        out_shape=jax.ShapeDtypeStruct((N_DEV * m_per, n), block.dtype),
        in_specs=[pl.BlockSpec(memory_space=pltpu.VMEM)],
        out_specs=pl.BlockSpec(memory_space=pltpu.VMEM),
        scratch_shapes=[pltpu.SemaphoreType.DMA((7,)), pltpu.SemaphoreType.DMA((7,)), pltpu.SemaphoreType.DMA],
        compiler_params=pltpu.CompilerParams(vmem_limit_bytes=VMEM_LIMIT),
    )(block)


def _half(ref, c, rows):
    start = pl.multiple_of(c * rows, BF16_ROWS)
    if len(ref.shape) == 2:
        return ref.at[pl.ds(start, rows), :]
    return ref.at[:, pl.ds(start, rows), :]


def _same(arrays):
    return [jax.ShapeDtypeStruct(a.shape, a.dtype) for a in arrays], {t: t for t in range(len(arrays))}


def _ex_gather_ici(bufs):
    def plan(ins, outs, send_sems, recv_sems):
        x, y, c, chips = _place()
        sends, arrivals = [], []
        for t, buf in enumerate(bufs):
            rows = buf.shape[1] // 2
            mine = _half(outs[t].at[2 * x + y], c, rows)
            for k, (px, py) in enumerate(chips):
                sems = (send_sems.at[3 * t + k], recv_sems.at[3 * t + k], (px, py, c))
                landed = _half(outs[t].at[2 * px + py], c, rows)
                sends.append((mine, mine, *sems))
                arrivals.append((landed, landed, *sems))
        return sends, arrivals

    shapes, aliases = _same(bufs)
    return _Exchange(bufs, shapes, aliases, 3 * len(bufs), plan)


def _ex_gather_d2d(bufs):
    def plan(ins, outs, send_sems, recv_sems):
        x, y, c, chips = _place()
        sends, arrivals = [], []
        for t, buf in enumerate(bufs):
            rows = buf.shape[1] // 2
            for k, (px, py) in enumerate(chips):
                sems = (send_sems.at[3 * t + k], recv_sems.at[3 * t + k], (x, y, 1 - c))
                landed = _half(outs[t].at[2 * px + py], c, rows)
                other = _half(outs[t].at[2 * px + py], 1 - c, rows)
                sends.append((landed, landed, *sems))
                arrivals.append((other, other, *sems))
        return sends, arrivals

    shapes, aliases = _same(bufs)
    return _Exchange(bufs, shapes, aliases, 3 * len(bufs), plan)


def _ex_swap_halves(grads):
    def plan(ins, outs, send_sems, recv_sems):
        x, y, c, _ = _place()
        sends = [(_half(ins[t], 1 - c, g.shape[1] // 2), outs[t], send_sems.at[t], recv_sems.at[t], (x, y, 1 - c))
                 for t, g in enumerate(grads)]
        return sends, sends

    shapes = [jax.ShapeDtypeStruct((g.shape[0], g.shape[1] // 2, g.shape[2]), g.dtype) for g in grads]
    return _Exchange(grads, shapes, {}, len(grads), plan)


def _ex_scatter(sums):
    def plan(ins, outs, send_sems, recv_sems):
        x, y, c, chips = _place()
        sends = [(ins[t].at[2 * px + py], outs[t].at[k], send_sems.at[3 * t + k], recv_sems.at[3 * t + k], (px, py, c))
                 for t in range(len(sums)) for k, (px, py) in enumerate(chips)]
        return sends, sends

    shapes = [jax.ShapeDtypeStruct((N_CHIPS - 1,) + s.shape[1:], s.dtype) for s in sums]
    return _Exchange(sums, shapes, {}, 3 * len(sums), plan)


def _ex_join_halves(fulls):
    def plan(ins, outs, send_sems, recv_sems):
        x, y, c, _ = _place()
        sends, arrivals = [], []
        for t, full in enumerate(fulls):
            rows = full.shape[0] // 2
            sems = (send_sems.at[t], recv_sems.at[t], (x, y, 1 - c))
            mine, other = _half(outs[t], c, rows), _half(outs[t], 1 - c, rows)
            sends.append((mine, mine, *sems))
            arrivals.append((other, other, *sems))
        return sends, arrivals

    shapes, aliases = _same(fulls)
    return _Exchange(fulls, shapes, aliases, len(fulls), plan)


class _Shifted:
    def __init__(self, sems, offset):
        self.sems, self.offset = sems, offset

    @property
    def at(self):
        return self

    def __getitem__(self, k):
        return self.sems.at[self.offset + k]


def _combine(exchanges):
    operands, out_shapes, aliases, starts = [], [], {}, []
    n_sems = 0
    for e in exchanges:
        starts.append((len(operands), len(out_shapes), n_sems))
        aliases.update({len(operands) + i: len(out_shapes) + o for i, o in e.aliases.items()})
        operands += list(e.operands)
        out_shapes += list(e.out_shapes)
        n_sems += e.n_sems

    def plan(ins, outs, send_sems, recv_sems):
        sends, arrivals = [], []
        for e, (i0, o0, s0) in zip(exchanges, starts):
            s, a = e.plan(ins[i0:i0 + len(e.operands)], outs[o0:o0 + len(e.out_shapes)],
                          _Shifted(send_sems, s0), _Shifted(recv_sems, s0))
            sends += s
            arrivals += a
        return sends, arrivals

    return _Exchange(operands, out_shapes, aliases, n_sems, plan)


class _Reduction:
    def __init__(self, grad, tag, c_arr, jc_arr):
        self.grad, self.tag, self.c_arr, self.jc_arr, self.stage = grad, tag, c_arr, jc_arr, 0

    def exchange(self):
        if self.stage == 0:
            return _ex_swap_halves([self.grad])
        if self.stage == 1:
            return _ex_scatter([self.sums])
        return _ex_join_halves([self.full])

    def advance(self, landed):
        if self.stage == 0:
            self.recv = landed
            self.sums = _chip_sum(self.grad, landed, self.c_arr, f"chip_sum_{self.tag}")
        elif self.stage == 1:
            self.full = _owner_sum(self.grad, self.recv, landed, self.jc_arr, f"owner_sum_{self.tag}")
        else:
            self.result = landed
        self.stage += 1


def _ride(reductions):
    def done(carried):
        for r, landed in zip(reductions, carried):
            r.advance(landed)

    return _combine([r.exchange() for r in reductions]), done


def _exchange_alone(ex, name):
    return _pallas(None, name=name, grid=(), in_specs=[], out_specs=[], out_shape=[], args=[], ex=ex)


def _cast_to_slot(w, chip_arr, name):
    A, B = w.shape
    ta = _tile(A, max(BF16_ROWS, (1 << 19) // B), BF16_ROWS)

    def body(j_ref, w_ref, o_ref):
        o_ref[...] = w_ref[...].astype(BF16)

    return pl.pallas_call(
        body,
        name=name,
        grid_spec=pltpu.PrefetchScalarGridSpec(
            num_scalar_prefetch=1,
            grid=(A // ta,),
            in_specs=[pl.BlockSpec((ta, B), lambda i, j_ref: (i, 0))],
            out_specs=pl.BlockSpec((None, ta, B), lambda i, j_ref: (j_ref[0], i, 0)),
        ),
        out_shape=jax.ShapeDtypeStruct((N_CHIPS, A, B), BF16),
        compiler_params=_params(("parallel",)),
    )(chip_arr, w)


def _chip_sum(grad, recv, c_arr, name):
    _, A, B = grad.shape
    hA = A // 2
    ta = _tile(hA, max(BF16_ROWS, (1 << 19) // B), BF16_ROWS)
    nh = hA // ta

    def body(c_ref, g_ref, r_ref, o_ref):
        o_ref[...] = (g_ref[...] + r_ref[...]).astype(BF16)

    return pl.pallas_call(
        body,
        name=name,
        grid_spec=pltpu.PrefetchScalarGridSpec(
            num_scalar_prefetch=1,
            grid=(N_CHIPS, nh),
            in_specs=[pl.BlockSpec((None, ta, B), lambda s, i, c_ref: (s, c_ref[0] * nh + i, 0)),
                      pl.BlockSpec((None, ta, B), lambda s, i, c_ref: (s, i, 0))],
            out_specs=pl.BlockSpec((None, ta, B), lambda s, i, c_ref: (s, i, 0)),
        ),
        out_shape=jax.ShapeDtypeStruct((N_CHIPS, hA, B), BF16),
        compiler_params=_params(("parallel", "parallel")),
    )(c_arr, grad, recv)


def _owner_sum(grad, recv, landed, jc_arr, name):
    _, A, B = grad.shape
    hA = A // 2
    ta = _tile(hA, max(BF16_ROWS, (1 << 19) // B), BF16_ROWS)
    nh = hA // ta

    def body(jc_ref, g_ref, r_ref, l0_ref, l1_ref, l2_ref, o_ref):
        total = g_ref[...] + r_ref[...]
        for ref in (l0_ref, l1_ref, l2_ref):
            total = total + ref[...].astype(F32)
        o_ref[...] = total

    def landed_spec(k):
        return pl.BlockSpec((None, ta, B), lambda i, jc_ref: (k, i, 0))

    return pl.pallas_call(
        body,
        name=name,
        grid_spec=pltpu.PrefetchScalarGridSpec(
            num_scalar_prefetch=1,
            grid=(nh,),
            in_specs=[pl.BlockSpec((None, ta, B), lambda i, jc_ref: (jc_ref[0], jc_ref[1] * nh + i, 0)),
                      pl.BlockSpec((None, ta, B), lambda i, jc_ref: (jc_ref[0], i, 0)),
                      landed_spec(0), landed_spec(1), landed_spec(2)],
            out_specs=pl.BlockSpec((ta, B), lambda i, jc_ref: (jc_ref[1] * nh + i, 0)),
        ),
        out_shape=jax.ShapeDtypeStruct((A, B), F32),
        compiler_params=_params(("parallel",)),
    )(jc_arr, grad, recv, landed, landed, landed)


def _pack(parts):
    rows = []
    for p in parts:
        flat = jnp.reshape(p.astype(F32), (-1,))
        tile = SUBLANES * LANES
        padded = -(-flat.shape[0] // tile) * tile
        rows.append(jnp.reshape(jnp.pad(flat, (0, padded - flat.shape[0])), (-1, LANES)))
    return jnp.concatenate(rows, axis=0)


def _unpack(pack, shapes):
    out, row = [], 0
    for shape in shapes:
        size = int(np.prod(shape))
        nrows = -(-size // (SUBLANES * LANES)) * SUBLANES
        out.append(jnp.reshape(jnp.reshape(pack[row:row + nrows], (-1,))[:size], shape))
        row += nrows
    return out


def _bias_tables():
    qi = np.arange(BLOCK)[:, None]
    kj = np.arange(2 * BLOCK)[None, :]
    dist = qi + BLOCK - kj
    in_window = (dist >= 0) & (dist < BLOCK)
    n = np.clip(dist, 0, None)
    max_exact = N_BUCKETS // 2
    nf = np.maximum(n, 1).astype(np.float32)
    large = max_exact + (np.log(nf / max_exact) / math.log(MAX_DISTANCE / max_exact)
                         * (N_BUCKETS - max_exact)).astype(np.int32)
    large = np.minimum(large, N_BUCKETS - 1)
    bucket = np.where(n < max_exact, n, large)
    onehot = (bucket[None] == np.arange(N_BUCKETS)[:, None, None]) & in_window[None]
    first = in_window & (kj >= BLOCK)
    return onehot.astype(np.float32), in_window, first


def kernel(x, c, w_ada, b_ada, g_ffn1, w1_ffn1, w3_ffn1, w2_ffn1, g_mix, w_in, spatial_w, spatial_b, g_v, g_q, g_k, sinks, rel_bias, w_out, g_ffn2, w1_ffn2, w3_ffn2, w2_ffn2, loss_target, m_w_ada, m_b_ada, m_g_ffn1, m_w1_ffn1, m_w3_ffn1, m_w2_ffn1, m_g_mix, m_w_in, m_spatial_w, m_spatial_b, m_g_v, m_g_q, m_g_k, m_sinks, m_rel_bias, m_w_out, m_g_ffn2, m_w1_ffn2, m_w3_ffn2, m_w2_ffn2, v_w_ada, v_b_ada, v_g_ffn1, v_w1_ffn1, v_w3_ffn1, v_w2_ffn1, v_g_mix, v_w_in, v_spatial_w, v_spatial_b, v_g_v, v_g_q, v_g_k, v_sinks, v_rel_bias, v_w_out, v_g_ffn2, v_w1_ffn2, v_w3_ffn2, v_w2_ffn2):
    ax, ay, ac = lax.axis_index("x"), lax.axis_index("y"), lax.axis_index("c")
    chip = 2 * ax + ay
    dev = 2 * chip + ac
    xs = x[0]
    tgt = loss_target[0]
    S, D = xs.shape
    F = N_CHIPS * w1_ffn1.shape[2]
    mod_cols = w_ada.shape[2]

    c_all = _allgather_small(jnp.pad(c, ((0, SUBLANES - 1), (0, 0))), "gather_c")
    c_all = jnp.pad(c_all[::SUBLANES], ((0, BF16_ROWS - N_DEV), (0, 0)))
    b_sh = lax.dynamic_slice(b_ada, (0, chip * mod_cols), (1, mod_cols))
    mod_part, c_act = _mod_partial(c_all, w_ada[0], b_sh, "mod_partial")
    mod_all = _allgather_small(mod_part[:N_DEV], "gather_mod")
    mod_all = jnp.reshape(mod_all, (N_CHIPS, 2, N_DEV, mod_cols))[:, 0]
    mod = jnp.reshape(lax.dynamic_index_in_dim(mod_all, dev, axis=1, keepdims=False), (1, N_MOD * D))
    sh1, sc1, gt1, sh2, sc2, gt2, sh3, sc3, gt3 = [mod[:, i * D:(i + 1) * D] for i in range(N_MOD)]

    def cols_to_natural(w4):
        return jnp.reshape(jnp.transpose(w4, (1, 0, 2)), (w4.shape[1], -1))

    chip_arr = jnp.reshape(chip, (1,)).astype(jnp.int32)
    c_arr = jnp.reshape(ac, (1,)).astype(jnp.int32)
    jc_arr = jnp.stack([chip, ac]).astype(jnp.int32)
    cast = lambda w, nm: _cast_to_slot(w[0], chip_arr, f"cast_{nm}")
    ffn1_bufs = [cast(w1_ffn1, "w1_ffn1"), cast(w3_ffn1, "w3_ffn1"), cast(w2_ffn1, "w2_ffn1")]
    mixer_bufs = [cast(w_in, "w_in"), cast(w_out, "w_out")]
    ffn2_bufs = [cast(w1_ffn2, "w1_ffn2"), cast(w3_ffn2, "w3_ffn2"), cast(w2_ffn2, "w2_ffn2")]
    ffn1_bufs = _exchange_alone(_ex_gather_ici(ffn1_bufs), "gather_ffn1_ici")
    ffn1_bufs = _exchange_alone(_ex_gather_d2d(ffn1_bufs), "gather_ffn1_d2d")
    w1a, w3a, w2a = cols_to_natural(ffn1_bufs[0]), cols_to_natural(ffn1_bufs[1]), jnp.reshape(ffn1_bufs[2], (F, D))

    onehot_np, in_window_np, first_np = _bias_tables()
    onehot = jnp.asarray(onehot_np)
    bias = jnp.einsum("bij,bh->hij", onehot, rel_bias, precision=lax.Precision.HIGHEST)
    biasm = jnp.stack([jnp.where(jnp.asarray(first_np)[None], bias, NEG),
                       jnp.where(jnp.asarray(in_window_np)[None], bias, NEG)])
    causal = jnp.asarray(np.tril(np.ones((BLOCK, BLOCK), dtype=bool)))
    wm = jnp.where(causal[None], spatial_w[0], 0.0).astype(BF16)
    wm_t = jnp.transpose(wm, (0, 2, 1))
    sink_vec = sinks[0]
    per_group = PAIRS // KV_HEADS
    sbp = jnp.broadcast_to(jnp.reshape(spatial_b[0], (A_HEADS * BLOCK, 1)), (A_HEADS * BLOCK, A_DIM))
    gvp = jnp.repeat(g_v[0], BLOCK, axis=0)
    gq2, gk2 = jnp.concatenate([g_q, g_q], axis=1), jnp.concatenate([g_k, g_k], axis=1)
    seg_ones = jnp.asarray(np.kron(np.eye(2, dtype=np.float32), np.ones((HEAD_DIM, HEAD_DIM), np.float32)))
    lane_ones = jnp.full((LANES, LANES), 1.0 / LANES, F32)
    pair_fold = jnp.asarray(np.kron(np.eye(per_group, LANES, dtype=np.float32), np.ones((BLOCK, 1), np.float32)))
    biasp = jnp.reshape(jnp.transpose(jnp.reshape(biasm, (2, KV_HEADS, per_group, 2, BLOCK, BAND)), (0, 1, 3, 5, 2, 4)),
                        (2, KV_HEADS, 2 * BAND, PAIR_ROWS))

    res = _ffn_fwd(xs, g_ffn1, sh1, sc1, gt1, w1a, w3a, w2a, None, "ffn1_fwd", ex=_ex_gather_ici(mixer_bufs + ffn2_bufs))
    (x1, h1, a1, b1, f1), mixer_bufs, ffn2_bufs = res[:5], res[5:7], res[7:]
    h2, *mixer_bufs = _norm_mod(x1, g_mix, sh2, sc2, "mixer_norm", ex=_ex_gather_d2d(mixer_bufs))
    win, wout = cols_to_natural(mixer_bufs[0]), jnp.reshape(mixer_bufs[1], (-1, D))
    z, *ffn2_bufs = _matmul(h2, win, "nn", F32, 1024, _tile(IN_COLS, 1664), D, "mixer_in", ex=_ex_gather_d2d(ffn2_bufs))
    w1b, w3b, w2b = cols_to_natural(ffn2_bufs[0]), cols_to_natural(ffn2_bufs[1]), jnp.reshape(ffn2_bufs[2], (F, D))
    mix = _mixer_fwd(z, wm, sbp, gvp, gq2, gk2, seg_ones, lane_ones, sink_vec, biasp, "mixer_fwd")
    x2, ymix = _mixer_out(mix, wout, x1, gt2, "mixer_out")
    g3, df3, h3, a3, b3, dgt3, loss_sum = _ffn_fwd(x2, g_ffn2, sh3, sc3, gt3, w1b, w3b, w2b, tgt, "ffn2_fwd_loss")
    loss = lax.psum(loss_sum[0, 0] * (0.5 / D), ("x", "y", "c"))

    tk = _tile(S, 2048)

    def ffn_weight_grads(h, da, db, s, df, tag, riding):
        ex, done = _ride(riding) if riding else (None, None)
        gw1 = _matmul(h, da, "tn", F32, 1024, F // N_CHIPS, tk, f"grad_w1_{tag}", shard_major=True, ex=ex)
        if riding:
            done(gw1[1:])
            gw1 = gw1[0]
        r1 = _Reduction(gw1, f"w1_{tag}", c_arr, jc_arr)
        ex, done = _ride([r1])
        gw3, *carried = _matmul(h, db, "tn", F32, 1024, F // N_CHIPS, tk, f"grad_w3_{tag}", shard_major=True, ex=ex)
        done(carried)
        r3 = _Reduction(gw3, f"w3_{tag}", c_arr, jc_arr)
        ex, done = _ride([r1, r3])
        gw2, *carried = _matmul(s, df, "tn", F32, _tile(F, 1408), 1024, tk, f"grad_w2_{tag}", ex=ex)
        done(carried)
        r2 = _Reduction(jnp.reshape(gw2, (N_CHIPS, F // N_CHIPS, D)), f"w2_{tag}", c_arr, jc_arr)
        return r1, r3, r2

    da3, db3, s3, dh3 = _ffn_bwd(df3, a3, b3, w1b, w3b, w2b, "ffn2_bwd")
    r21, r23, r22 = ffn_weight_grads(h3, da3, db3, s3, df3, "ffn2", [])
    ex, done = _ride([r22])
    res = _norm_bwd(dh3, x2, g3, g_ffn2, sc3, (ymix, gt2, 1.0), "ffn2_norm_bwd", ex=ex)
    g2, dsh3, dsc3, dgn3, dy, dgt2 = res[:6]
    done(res[6:])

    ex, done = _ride([r21, r23])
    dmix, *carried = _matmul(dy, wout, "nt", BF16, 1024, 2048, D, "mixer_out_bwd", ex=ex)
    done(carried)
    ex, done = _ride([r23, r22])
    res = _mixer_bwd(z, dmix, wm, wm_t, sbp, gvp, gq2, gk2, seg_ones, lane_ones, sink_vec, biasp, pair_fold,
                     "mixer_bwd", ex=ex)
    dz_main, dz_kv, dwm, dsb, dgv, dgq, dgk, dsk, dst = res[:9]
    dsb = jnp.reshape(dsb[:, 0], (A_HEADS, BLOCK))
    dgq, dgk = dgq[:, :HEAD_DIM], dgk[:, :HEAD_DIM]
    dsk = jnp.reshape(jnp.transpose(jnp.reshape(dsk[:2 * KV_HEADS, :per_group], (KV_HEADS, 2, per_group)), (0, 2, 1)),
                      (1, B_HEADS))
    dst = jnp.reshape(jnp.transpose(jnp.reshape(dst, (KV_HEADS, 2, BAND, per_group, BLOCK)), (0, 3, 1, 4, 2)),
                      (B_HEADS, BLOCK * BAND))
    done(res[9:])
    dz = jnp.concatenate([dz_main, dz_kv], axis=1)
    ex, done = _ride([r22])
    dh2, *carried = _matmul(dz, win, "nt", F32, 1024, 2048, _tile(IN_COLS, 1664), "mixer_in_bwd", ex=ex)
    done(carried)
    drel = _bucket_sum(dst, jnp.reshape(onehot, (N_BUCKETS, -1)), "bucket_sum")
    g1, dsh2, dsc2, dgn2, df1, dgt1 = _norm_bwd(dh2, x1, g2, g_mix, sc2, (f1, gt1, 0.5), "mixer_norm_bwd")

    da1, db1, s1, dh1 = _ffn_bwd(df1, a1, b1, w1a, w3a, w2a, "ffn1_bwd")
    r11, r13, r12 = ffn_weight_grads(h1, da1, db1, s1, df1, "ffn1", [])
    ex, done = _ride([r11, r13, r12])
    gwin_full, *carried = _matmul(h2, dz, "tn", F32, 1024, _tile(IN_COLS, 1664), tk, "grad_w_in", ex=ex)
    done(carried)
    rm_in = _Reduction(jnp.transpose(jnp.reshape(gwin_full, (D, N_CHIPS, -1)), (1, 0, 2)), "w_in", c_arr, jc_arr)
    grad_x, dsh1, dsc1, dgn1 = _norm_bwd(dh1, xs, g1, g_ffn1, sc1, None, "ffn1_norm_bwd")
    ex, done = _ride([r13, r12, rm_in])
    gwout_full, *carried = _matmul(mix, dy, "tn", F32, 1024, 1024, tk, "grad_w_out", ex=ex)
    done(carried)
    rm_out = _Reduction(jnp.reshape(gwout_full, (N_CHIPS, -1, D)), "w_out", c_arr, jc_arr)
    for stage, riding in enumerate([[r12, rm_in, rm_out], [rm_in, rm_out], [rm_out]]):
        ex, done = _ride(riding)
        done(_exchange_alone(ex, f"reduce_tail_{stage}"))

    dmod = jnp.concatenate([dsh1, dsc1, dgt1, dsh2, dsc2, dgt2, dsh3, dsc3, dgt3], axis=1)
    small_w = [b_ada, g_ffn1, g_mix, g_ffn2, spatial_w, spatial_b, g_v, g_q, g_k, sinks, rel_bias]
    small_m = [m_b_ada, m_g_ffn1, m_g_mix, m_g_ffn2, m_spatial_w, m_spatial_b, m_g_v, m_g_q, m_g_k, m_sinks, m_rel_bias]
    small_v = [v_b_ada, v_g_ffn1, v_g_mix, v_g_ffn2, v_spatial_w, v_spatial_b, v_g_v, v_g_q, v_g_k, v_sinks, v_rel_bias]
    small_g = [dmod, dgn1, dgn2, dgn3, jnp.where(causal[None], dwm, 0.0), dsb, dgv, dgq, dgk, dsk, drel]
    shapes = [w.shape for w in small_w]
    gpack = _pack(small_g)
    rows = gpack.shape[0]
    gall = jnp.reshape(_allgather_small(gpack, "gather_small"), (N_DEV, rows, LANES))
    sg, sd, sm, sv = _small_update(gall, _pack(small_w), _pack(small_m), _pack(small_v), "small_update")
    sg, sd, sm, sv = [_unpack(p, shapes) for p in (sg, sd, sm, sv)]

    mod_rows = -(-N_MOD * D // (SUBLANES * LANES)) * SUBLANES
    dmod_all = jnp.reshape(gall[:, :mod_rows], (N_DEV, -1))[:, :N_MOD * D]
    dmod_sh = lax.dynamic_slice(dmod_all, (0, chip * mod_cols), (N_DEV, mod_cols))
    dmod_sh = jnp.pad(dmod_sh, ((0, BF16_ROWS - N_DEV), (0, 0))).astype(BF16)
    g_wada = _matmul(c_act, dmod_sh, "tn", F32, 1024, _tile(mod_cols, 512), BF16_ROWS, "grad_w_ada")
    d_wada, nm_wada, nv_wada = _adamw(w_ada[0], g_wada, m_w_ada[0], v_w_ada[0], "adamw_w_ada")

    gf1 = [r11.result, r13.result, r12.result]
    gf2 = [r21.result, r23.result, r22.result]
    gmx = [rm_in.result, rm_out.result]
    big = {}
    for nm, w, g, m, v in [
        ("w1_ffn1", w1_ffn1, gf1[0], m_w1_ffn1, v_w1_ffn1), ("w3_ffn1", w3_ffn1, gf1[1], m_w3_ffn1, v_w3_ffn1),
        ("w2_ffn1", w2_ffn1, gf1[2], m_w2_ffn1, v_w2_ffn1), ("w_in", w_in, gmx[0], m_w_in, v_w_in),
        ("w_out", w_out, gmx[1], m_w_out, v_w_out), ("w1_ffn2", w1_ffn2, gf2[0], m_w1_ffn2, v_w1_ffn2),
        ("w3_ffn2", w3_ffn2, gf2[1], m_w3_ffn2, v_w3_ffn2), ("w2_ffn2", w2_ffn2, gf2[2], m_w2_ffn2, v_w2_ffn2),
    ]:
        g_out, d, nm_, nv_ = _adamw(w[0], g, m[0], v[0], f"adamw_{nm}", emit_grad=True)
        big[nm] = (g_out[None], d[None], nm_[None], nv_[None])
    big["w_ada"] = (g_wada[None], d_wada[None], nm_wada[None], nv_wada[None])

    order = ["w_ada", "b_ada", "g_ffn1", "w1_ffn1", "w3_ffn1", "w2_ffn1", "g_mix", "w_in", "spatial_w", "spatial_b",
             "g_v", "g_q", "g_k", "sinks", "rel_bias", "w_out", "g_ffn2", "w1_ffn2", "w3_ffn2", "w2_ffn2"]
    small_names = ["b_ada", "g_ffn1", "g_mix", "g_ffn2", "spatial_w", "spatial_b", "g_v", "g_q", "g_k", "sinks", "rel_bias"]
    for i, nm in enumerate(small_names):
        big[nm] = (sg[i], sd[i], sm[i], sv[i])
    outs = [loss, grad_x[None]]
    for kind in range(4):
        outs += [big[nm][kind] for nm in order]
    return tuple(outs)
```

```python
import functools
import math

import jax
import jax.numpy as jnp
import numpy as np
from jax import lax
from jax.experimental import pallas as pl
from jax.experimental.pallas import tpu as pltpu

F32 = jnp.float32
BF16 = jnp.bfloat16
MESH = pl.DeviceIdType.MESH
ANY = pl.BlockSpec(memory_space=pl.ANY)

EPS = 1e-6
BLOCK = 128
A_HEADS = 8
A_DIM = 128
A_WIDTH = A_HEADS * A_DIM
B_HEADS = 16
KV_HEADS = 2
GROUP = B_HEADS // KV_HEADS
HEAD_DIM = 64
B_WIDTH = B_HEADS * HEAD_DIM
KV_WIDTH = KV_HEADS * HEAD_DIM
Q_OFF = 2 * A_WIDTH
K_OFF = Q_OFF + B_WIDTH
V_OFF = K_OFF + KV_WIDTH
IN_COLS = V_OFF + KV_WIDTH
N_BUCKETS = 32
MAX_DISTANCE = 128
N_MOD = 9
N_CHIPS = 4
N_DEV = 8
NEG = -1e30

ADAM_LR = 0.001
ADAM_B1 = 0.9
ADAM_B2 = 0.999
ADAM_EPS = 1e-08
ADAM_WD = 0.01
ADAM_STEP = 10

LANES = 128
SUBLANES = 8
BF16_ROWS = 16
VMEM_LIMIT = 60 * 1024 * 1024

INV_SQRT2 = 1.0 / math.sqrt(2.0)
INV_SQRT_2PI = 1.0 / math.sqrt(2.0 * math.pi)


def _tile(n, pref, mult=LANES):
    t = (min(pref, n) // mult) * mult
    while t >= mult:
        if n % t == 0:
            return t
        t -= mult
    return n


def _params(sem):
    return pltpu.CompilerParams(dimension_semantics=sem, vmem_limit_bytes=VMEM_LIMIT)


class _Exchange:
    def __init__(self, operands, out_shapes, aliases, n_sems, plan):
        self.operands, self.out_shapes, self.aliases, self.n_sems, self.plan = operands, out_shapes, aliases, n_sems, plan


def _pallas(body, *, name, grid, in_specs, out_specs, out_shape, args, scratch_shapes=(), semantics=None, ex=None):
    if ex is None:
        return pl.pallas_call(body, name=name, grid=grid, in_specs=in_specs, out_specs=out_specs, out_shape=out_shape,
                              scratch_shapes=list(scratch_shapes), compiler_params=_params(semantics))(*args)
    n_in, n_out, n_scr = len(in_specs), len(out_specs), len(scratch_shapes)
    e_in, e_out = len(ex.operands), len(ex.out_shapes)

    def wrapped(*refs):
        ins, refs = refs[:n_in], refs[n_in:]
        ex_ins, refs = refs[:e_in], refs[e_in:]
        outs, refs = refs[:n_out], refs[n_out:]
        ex_outs, refs = refs[:e_out], refs[e_out:]
        scratch, (send_sems, recv_sems) = refs[:n_scr], refs[n_scr:]
        first, last = True, True
        for d, size in enumerate(grid):
            first = jnp.logical_and(first, pl.program_id(d) == 0)
            last = jnp.logical_and(last, pl.program_id(d) == size - 1)

        def start():
            sends, _ = ex.plan(ex_ins, ex_outs, send_sems, recv_sems)
            for cp in sends:
                _remote(*cp).start()

        def finish():
            sends, arrivals = ex.plan(ex_ins, ex_outs, send_sems, recv_sems)
            for cp in arrivals:
                _remote(*cp).wait_recv()
            for cp in sends:
                _remote(*cp).wait_send()

        if grid:
            pl.when(first)(start)
        else:
            start()
        if body is not None:
            body(*ins, *outs, *scratch)
        if grid:
            pl.when(last)(finish)
        else:
            finish()

    kwargs = dict(grid=grid) if grid else {}
    return pl.pallas_call(
        wrapped,
        name=name,
        in_specs=list(in_specs) + [ANY] * e_in,
        out_specs=list(out_specs) + [ANY] * e_out,
        out_shape=list(out_shape) + list(ex.out_shapes),
        input_output_aliases={n_in + i: n_out + o for i, o in ex.aliases.items()},
        scratch_shapes=list(scratch_shapes) + [pltpu.SemaphoreType.DMA((ex.n_sems,)), pltpu.SemaphoreType.DMA((ex.n_sems,))],
        compiler_params=_params(("arbitrary",) * len(grid) if grid else None),
        **kwargs,
    )(*args, *ex.operands)


def _dot(a, b, dims=(((1,), (0,)), ((), ()))):
    return lax.dot_general(a, b, dims, preferred_element_type=F32)


NN = (((1,), (0,)), ((), ()))
NT = (((1,), (1,)), ((), ()))
TN = (((0,), (0,)), ((), ()))


def _sigmoid(x):
    return 1.0 / (1.0 + jnp.exp(-x))


def _gelu_and_grad(x):
    cdf = 0.5 * (1.0 + lax.erf(x * INV_SQRT2))
    pdf = jnp.exp(-0.5 * x * x) * INV_SQRT_2PI
    return x * cdf, cdf + x * pdf


def _gelu(x):
    return x * (0.5 * (1.0 + lax.erf(x * INV_SQRT2)))


def _rms(x):
    r = lax.rsqrt(jnp.mean(x * x, axis=-1, keepdims=True) + EPS)
    return x * r, r


ROW_CHUNK = 64


def _for_rows(tm, fn):
    rc = min(ROW_CHUNK, tm)

    def step(r, carry):
        fn(pl.ds(pl.multiple_of(r * rc, rc), rc))
        return carry

    lax.fori_loop(0, tm // rc, step, 0)


def _rms_bwd(dy, xhat, r):
    return r * (dy - xhat * jnp.mean(dy * xhat, axis=-1, keepdims=True))


def _matmul(a, b, mode, out_dtype, tm, tn, tk, name, shard_major=False, ex=None):
    if mode == "nn":
        (M, K), N = a.shape, b.shape[1]
    elif mode == "nt":
        (M, K), N = a.shape, b.shape[0]
    else:
        (K, M), N = a.shape, b.shape[1]
    tm, tn, tk = min(tm, M), min(tn, N), min(tk, K)
    assert M % tm == 0 and N % tn == 0 and K % tk == 0, (name, M, N, K, tm, tn, tk)
    nk = K // tk
    dims = {"nn": NN, "nt": NT, "tn": TN}[mode]
    a_spec = pl.BlockSpec((tk, tm), lambda i, j, k: (k, i)) if mode == "tn" else pl.BlockSpec((tm, tk), lambda i, j, k: (i, k))
    b_spec = pl.BlockSpec((tn, tk), lambda i, j, k: (j, k)) if mode == "nt" else pl.BlockSpec((tk, tn), lambda i, j, k: (k, j))
    if shard_major:
        assert tn * N_CHIPS == N
        out_shape = jax.ShapeDtypeStruct((N_CHIPS, M, tn), out_dtype)
        o_spec = pl.BlockSpec((None, tm, tn), lambda i, j, k: (j, i, 0))
    else:
        out_shape = jax.ShapeDtypeStruct((M, N), out_dtype)
        o_spec = pl.BlockSpec((tm, tn), lambda i, j, k: (i, j))

    direct = nk == 1 or out_dtype == F32

    def body(a_ref, b_ref, o_ref, *scratch):
        k = pl.program_id(2)
        if nk == 1:
            o_ref[...] = _dot(a_ref[...], b_ref[...], dims).astype(o_ref.dtype)
            return
        acc_ref = o_ref if direct else scratch[0]

        @pl.when(k == 0)
        def _():
            acc_ref[...] = jnp.zeros(acc_ref.shape, F32)

        acc_ref[...] += _dot(a_ref[...], b_ref[...], dims)
        if not direct:
            @pl.when(k == nk - 1)
            def _():
                o_ref[...] = acc_ref[...].astype(o_ref.dtype)

    outs = _pallas(body, name=name, grid=(M // tm, N // tn, nk), in_specs=[a_spec, b_spec], out_specs=[o_spec],
                   out_shape=[out_shape], scratch_shapes=[] if direct else [pltpu.VMEM((tm, tn), F32)],
                   semantics=("parallel", "parallel", "arbitrary"), args=[a, b], ex=ex)
    return outs[0] if ex is None else outs


def _mod_partial(c_all, w_ada, b_sh, name):
    R, D = c_all.shape
    N = w_ada.shape[1]
    tn = _tile(N, 512)

    def body(c_ref, w_ref, b_ref, o_ref, ca_ref):
        cv = c_ref[...]
        ca = (cv * _sigmoid(cv)).astype(BF16)
        ca_ref[...] = ca
        o_ref[...] = _dot(ca, w_ref[...].astype(BF16)) + b_ref[...]

    return pl.pallas_call(
        body,
        name=name,
        grid=(N // tn,),
        in_specs=[
            pl.BlockSpec((R, D), lambda j: (0, 0)),
            pl.BlockSpec((D, tn), lambda j: (0, j)),
            pl.BlockSpec((1, tn), lambda j: (0, j)),
        ],
        out_specs=[pl.BlockSpec((R, tn), lambda j: (0, j)), pl.BlockSpec((R, D), lambda j: (0, 0))],
        out_shape=[jax.ShapeDtypeStruct((R, N), F32), jax.ShapeDtypeStruct((R, D), BF16)],
        compiler_params=_params(("arbitrary",)),
    )(c_all, w_ada, b_sh)


FFN_BLOCK = 1024


def _ffn_blocks(F):
    if F % FFN_BLOCK == 0 or F < FFN_BLOCK:
        tf = _tile(F, FFN_BLOCK)
        return tf, F // tf, tf
    nj = -(-F // FFN_BLOCK)
    tail = F - (nj - 1) * FFN_BLOCK
    assert tail % LANES == 0
    return FFN_BLOCK, nj, tail
def _ffn_fwd(x, g, sh, sc, gt, w1, w3, w2, tgt, name, ex=None):
    S, D = x.shape
    F = w1.shape[1]
    tm, tf, nj, tail = _tile(S, 512), *_ffn_blocks(F)
    ni = S // tm
    with_loss = tgt is not None

    def body(*refs):
        if with_loss:
            (x_ref, g_ref, sh_ref, sc_ref, gt_ref, w1_ref, w3_ref, w2_ref, tgt_ref,
             gout_ref, df_ref, h_ref, a_ref, b_ref, dgt_ref, loss_ref, acc_ref) = refs
        else:
            (x_ref, g_ref, sh_ref, sc_ref, gt_ref, w1_ref, w3_ref, w2_ref,
             xo_ref, h_ref, a_ref, b_ref, f_ref, acc_ref) = refs
        i, j = pl.program_id(0), pl.program_id(1)

        @pl.when(j == 0)
        def _():
            def prologue(rows):
                xhat, _ = _rms(x_ref[rows, :])
                h_ref[rows, :] = ((xhat * g_ref[...]) * (1.0 + sc_ref[...]) + sh_ref[...]).astype(BF16)

            _for_rows(tm, prologue)

        @pl.when(j == 0)
        def _():
            acc_ref[...] = jnp.zeros(acc_ref.shape, F32)

        def columns(width):
            def run():
                hb = h_ref[...]
                av = _dot(hb, w1_ref[:, :width])
                bv = _dot(hb, w3_ref[:, :width])
                a_ref[:, :width] = av.astype(BF16)
                b_ref[:, :width] = bv.astype(BF16)
                sv = ((av * _sigmoid(av)) * bv).astype(BF16)
                acc_ref[...] += _dot(sv, w2_ref[:width, :])
            return run

        if tail == tf:
            columns(tf)()
        else:
            pl.when(j < nj - 1)(columns(tf))
            pl.when(j == nj - 1)(columns(tail))

        @pl.when(j == nj - 1)
        def _():
            if with_loss:
                @pl.when(i == 0)
                def _():
                    dgt_ref[...] = jnp.zeros(dgt_ref.shape, F32)
                    loss_ref[...] = jnp.zeros(loss_ref.shape, F32)

            def epilogue(rows):
                fv = acc_ref[rows, :]
                half_gate = 0.5 * gt_ref[...]
                xo = x_ref[rows, :] + half_gate * fv
                if not with_loss:
                    xo_ref[rows, :] = xo
                    f_ref[rows, :] = fv.astype(f_ref.dtype)
                    return
                err = xo - tgt_ref[rows, :]
                gout = err * (1.0 / D)
                gout_ref[rows, :] = gout
                df_ref[rows, :] = (half_gate * gout).astype(BF16)
                dgt_ref[...] += 0.5 * jnp.sum(gout * fv, axis=0, keepdims=True)
                loss_part = jnp.sum(jnp.sum(err * err, axis=1, keepdims=True), axis=0, keepdims=True)
                loss_ref[...] += jnp.broadcast_to(loss_part, loss_ref.shape)

            _for_rows(tm, epilogue)

    row = pl.BlockSpec((tm, D), lambda i, j: (i, 0))
    row_in = pl.BlockSpec((tm, D), lambda i, j: (i, 0), pipeline_mode=pl.Buffered(1))
    vec = pl.BlockSpec((1, D), lambda i, j: (0, 0))
    col = pl.BlockSpec((tm, tf), lambda i, j: (i, j))
    in_specs = [row_in, vec, vec, vec, vec,
                pl.BlockSpec((D, tf), lambda i, j: (0, j)),
                pl.BlockSpec((D, tf), lambda i, j: (0, j)),
                pl.BlockSpec((tf, D), lambda i, j: (j, 0))]
    args = [x, g, sh, sc, gt, w1, w3, w2]
    act = jax.ShapeDtypeStruct((S, F), BF16)
    if with_loss:
        in_specs.append(row_in)
        args.append(tgt)
        out_specs = [row, row, row_in, col, col, vec, pl.BlockSpec((1, LANES), lambda i, j: (0, 0))]
        out_shape = [jax.ShapeDtypeStruct((S, D), F32), jax.ShapeDtypeStruct((S, D), BF16),
                     jax.ShapeDtypeStruct((S, D), BF16), act, act,
                     jax.ShapeDtypeStruct((1, D), F32), jax.ShapeDtypeStruct((1, LANES), F32)]
    else:
        out_specs = [row, row, col, col, row]
        out_shape = [jax.ShapeDtypeStruct((S, D), F32), jax.ShapeDtypeStruct((S, D), BF16), act, act,
                     jax.ShapeDtypeStruct((S, D), BF16)]
    return _pallas(body, name=name, grid=(ni, nj), in_specs=in_specs, out_specs=out_specs, out_shape=out_shape,
                   scratch_shapes=[pltpu.VMEM((tm, D), F32)],
                   semantics=("arbitrary", "arbitrary"), args=args, ex=ex)


def _ffn_bwd(df, a, b, w1, w3, w2, name, ex=None):
    S, D = df.shape
    F = a.shape[1]
    tm, tf, nj, tail = _tile(S, 512), *_ffn_blocks(F)

    def body(df_ref, a_ref, b_ref, w1_ref, w3_ref, w2_ref, da_ref, db_ref, s_ref, dh_ref):
        j = pl.program_id(1)

        @pl.when(j == 0)
        def _():
            dh_ref[...] = jnp.zeros(dh_ref.shape, F32)

        def columns(width):
            def run():
                ds = _dot(df_ref[...], w2_ref[:width, :], NT)
                av = a_ref[:, :width].astype(F32)
                bv = b_ref[:, :width].astype(F32)
                sig = _sigmoid(av)
                sil = av * sig
                da = ((ds * bv) * (sig * (1.0 + av * (1.0 - sig)))).astype(BF16)
                db = (ds * sil).astype(BF16)
                da_ref[:, :width] = da
                db_ref[:, :width] = db
                s_ref[:, :width] = (sil * bv).astype(BF16)
                dh_ref[...] += _dot(da, w1_ref[:, :width], NT) + _dot(db, w3_ref[:, :width], NT)
            return run

        if tail == tf:
            columns(tf)()
        else:
            pl.when(j < nj - 1)(columns(tf))
            pl.when(j == nj - 1)(columns(tail))

    row = pl.BlockSpec((tm, D), lambda i, j: (i, 0))
    col = pl.BlockSpec((tm, tf), lambda i, j: (i, j))
    act = jax.ShapeDtypeStruct((S, F), BF16)
    return _pallas(body, name=name, grid=(S // tm, nj),
                   in_specs=[row, col, col,
                             pl.BlockSpec((D, tf), lambda i, j: (0, j)),
                             pl.BlockSpec((D, tf), lambda i, j: (0, j)),
                             pl.BlockSpec((tf, D), lambda i, j: (j, 0))],
                   out_specs=[col, col, col, row],
                   out_shape=[act, act, act, jax.ShapeDtypeStruct((S, D), F32)],
                   semantics=("parallel", "arbitrary"), args=[df, a, b, w1, w3, w2], ex=ex)


def _norm_mod(x, g, sh, sc, name, ex=None):
    S, D = x.shape
    tm = _tile(S, 512)

    def body(x_ref, g_ref, sh_ref, sc_ref, h_ref):
        def step(rows):
            xhat, _ = _rms(x_ref[rows, :])
            h_ref[rows, :] = ((xhat * g_ref[...]) * (1.0 + sc_ref[...]) + sh_ref[...]).astype(BF16)

        _for_rows(tm, step)

    row = pl.BlockSpec((tm, D), lambda i: (i, 0))
    vec = pl.BlockSpec((1, D), lambda i: (0, 0))
    outs = _pallas(body, name=name, grid=(S // tm,), in_specs=[row, vec, vec, vec], out_specs=[row],
                   out_shape=[jax.ShapeDtypeStruct((S, D), BF16)], semantics=("parallel",), args=[x, g, sh, sc], ex=ex)
    return outs[0] if ex is None else outs


def _norm_bwd(dh, x, gres, g, sc, prev, name, ex=None):
    S, D = x.shape
    tm = _tile(S, 256)
    has_prev = prev is not None
    coef = prev[2] if has_prev else None

    def body(*refs):
        if has_prev:
            (dh_ref, x_ref, gr_ref, g_ref, sc_ref, f_ref, gt_ref,
             go_ref, dsh_ref, dsc_ref, dg_ref, dp_ref, dgt_ref) = refs
        else:
            dh_ref, x_ref, gr_ref, g_ref, sc_ref, go_ref, dsh_ref, dsc_ref, dg_ref = refs
        sum_refs = [dsh_ref, dsc_ref, dg_ref] + ([dgt_ref] if has_prev else [])

        @pl.when(pl.program_id(0) == 0)
        def _():
            for ref in sum_refs:
                ref[...] = jnp.zeros(ref.shape, F32)

        def step(rows):
            dh = dh_ref[rows, :]
            xhat, r = _rms(x_ref[rows, :])
            gain = g_ref[...]
            scale1 = 1.0 + sc_ref[...]
            gout = gr_ref[rows, :] + _rms_bwd(dh * scale1 * gain, xhat, r)
            go_ref[rows, :] = gout
            sums = [dh, dh * (xhat * gain), dh * scale1 * xhat]
            if has_prev:
                dp_ref[rows, :] = ((coef * gt_ref[...]) * gout).astype(BF16)
                sums.append(coef * (gout * f_ref[rows, :].astype(F32)))
            for ref, v in zip(sum_refs, sums):
                ref[...] += jnp.sum(v, axis=0, keepdims=True)

        _for_rows(tm, step)

    row = pl.BlockSpec((tm, D), lambda i: (i, 0))
    vec = pl.BlockSpec((1, D), lambda i: (0, 0))
    vshape = jax.ShapeDtypeStruct((1, D), F32)
    in_specs = [row, row, row, vec, vec]
    args = [dh, x, gres, g, sc]
    out_specs = [row, vec, vec, vec]
    out_shape = [jax.ShapeDtypeStruct((S, D), F32), vshape, vshape, vshape]
    if has_prev:
        in_specs += [row, vec]
        args += [prev[0], prev[1]]
        out_specs += [row, vec]
        out_shape += [jax.ShapeDtypeStruct((S, D), BF16), vshape]
    return _pallas(body, name=name, grid=(S // tm,), in_specs=in_specs, out_specs=out_specs, out_shape=out_shape,
                   semantics=("arbitrary",), args=args, ex=ex)


PAIRS = B_HEADS // 2
PAIR_ROWS = (PAIRS // KV_HEADS) * BLOCK
BAND = 2 * BLOCK


def _stack(ref, offset, count):
    return jnp.concatenate([ref[:, offset + p * LANES:offset + (p + 1) * LANES] for p in range(count)], axis=0)


def _seg_mean(x, e_ref):
    return _dot(x.astype(BF16), e_ref[...]) * (1.0 / HEAD_DIM)


def _block_diag(x, x_rolled, left, kv_head):
    if kv_head == 0:
        top, bottom = jnp.where(left, x, 0.0), jnp.where(left, 0.0, x_rolled)
    else:
        top, bottom = jnp.where(left, x_rolled, 0.0), jnp.where(left, 0.0, x)
    return jnp.concatenate([top, bottom], axis=0).astype(BF16)


def _from_block_diag(g, left, kv_head):
    a, b = g[:BAND], g[BAND:]
    if kv_head == 0:
        return jnp.where(left, a + pltpu.roll(b, HEAD_DIM, 1), 0.0)
    return jnp.where(left, 0.0, pltpu.roll(a, HEAD_DIM, 1) + b)


def _pair_softmax(st, sk_ref, kv_head):
    out = []
    for e in range(2):
        seg = st[e * BAND:(e + 1) * BAND]
        sink = jnp.concatenate([jnp.full((1, BLOCK), sk_ref[kv_head * GROUP + 2 * p + e], F32)
                                for p in range(PAIRS // KV_HEADS)], axis=1)
        m = jnp.maximum(jnp.max(seg, axis=0, keepdims=True), sink)
        p_ = jnp.exp(seg - m)
        e_sink = jnp.exp(sink - m)
        inv = 1.0 / (jnp.sum(p_, axis=0, keepdims=True) + e_sink)
        out.append((p_ * inv, e_sink * inv))
    return out


def _lane_mean(x, ones_ref):
    return _dot(x.astype(BF16), ones_ref[...])


def _mixer_specs(nb, last):
    full = lambda shape: pl.BlockSpec(shape, lambda n: (0,) * len(shape))
    z_spec = pl.BlockSpec((BLOCK, IN_COLS), lambda n: (jnp.minimum(n, last), 0))
    zp_spec = pl.BlockSpec((BLOCK, 2 * KV_WIDTH), lambda n: (jnp.clip(n - 1, 0, last), K_OFF // (2 * KV_WIDTH)))
    consts = [full((A_HEADS * BLOCK, A_DIM)), full((A_HEADS * BLOCK, A_DIM)), full((1, LANES)), full((1, LANES)),
              full((LANES, LANES)), full((LANES, LANES)), pl.BlockSpec(memory_space=pltpu.SMEM),
              pl.BlockSpec((None, KV_HEADS, PAIR_ROWS, 2 * BAND), lambda n: (jnp.minimum(n, 1), 0, 0, 0))]
    return full, z_spec, zp_spec, consts


def _mixer_fwd(z, wm, sbp, gvp, gq2, gk2, seg_ones, lane_ones, sinks, biasp, name):
    S = z.shape[0]
    nb = S // BLOCK

    def body(z_ref, zp_ref, wm_ref, sbp_ref, gvp_ref, gq2_ref, gk2_ref, e_ref, l_ref, sk_ref, bias_ref, mix_ref):
        u = _gelu(_stack(z_ref, 0, A_HEADS))
        v = _gelu(_stack(z_ref, A_WIDTH, A_HEADS))
        vhat = v * lax.rsqrt(_lane_mean(v * v, l_ref) + EPS)
        vn = (vhat * gvp_ref[...]).astype(BF16)
        mixed = jnp.concatenate([_dot(wm_ref[h], vn[h * BLOCK:(h + 1) * BLOCK]) for h in range(A_HEADS)], axis=0)
        ya = (u * (mixed + sbp_ref[...])).astype(BF16)
        for h in range(A_HEADS):
            mix_ref[:, h * A_DIM:(h + 1) * A_DIM] = ya[h * BLOCK:(h + 1) * BLOCK]

        left = lax.broadcasted_iota(jnp.int32, (1, LANES), 1) < HEAD_DIM
        kv = jnp.concatenate([zp_ref[...], z_ref[:, K_OFF:K_OFF + 2 * KV_WIDTH]], axis=0)
        k2, v2 = kv[:, :KV_WIDTH], kv[:, KV_WIDTH:]
        kn2 = k2 * lax.rsqrt(_seg_mean(k2 * k2, e_ref) + EPS) * gk2_ref[...]
        kn2_r, v2_r = pltpu.roll(kn2, HEAD_DIM, 1), pltpu.roll(v2, HEAD_DIM, 1)
        qp = _stack(z_ref, Q_OFF, PAIRS)
        qn = (qp * lax.rsqrt(_seg_mean(qp * qp, e_ref) + EPS) * gq2_ref[...]).astype(BF16)
        for kh in range(KV_HEADS):
            kbd, vbd = _block_diag(kn2, kn2_r, left, kh), _block_diag(v2, v2_r, left, kh)
            st = _dot(kbd, qn[kh * PAIR_ROWS:(kh + 1) * PAIR_ROWS], NT) * (HEAD_DIM ** -0.5) + bias_ref[kh]
            wt = jnp.concatenate([w_e for w_e, _ in _pair_softmax(st, sk_ref, kh)], axis=0).astype(BF16)
            o = _dot(wt, vbd, TN).astype(BF16)
            for p in range(PAIRS // KV_HEADS):
                col = A_WIDTH + (kh * (PAIRS // KV_HEADS) + p) * LANES
                mix_ref[:, col:col + LANES] = o[p * BLOCK:(p + 1) * BLOCK]

    full, z_spec, zp_spec, consts = _mixer_specs(nb, nb - 1)
    return pl.pallas_call(
        body,
        name=name,
        grid=(nb,),
        in_specs=[z_spec, zp_spec, full((A_HEADS, BLOCK, BLOCK))] + consts,
        out_specs=pl.BlockSpec((BLOCK, A_WIDTH + B_WIDTH), lambda n: (n, 0)),
        out_shape=jax.ShapeDtypeStruct((S, A_WIDTH + B_WIDTH), BF16),
        compiler_params=_params(("parallel",)),
    )(z, z, wm, sbp, gvp, gq2, gk2, seg_ones, lane_ones, sinks, biasp)


def _mixer_bwd(z, dmix, wm, wm_t, sbp, gvp, gq2, gk2, seg_ones, lane_ones, sinks, biasp, pair_fold, name, ex=None):
    S = z.shape[0]
    nb = S // BLOCK

    def body(z_ref, zp_ref, dmix_ref, wm_ref, wmt_ref, sbp_ref, gvp_ref, gq2_ref, gk2_ref, e_ref, l_ref, sk_ref,
             bias_ref, fold_ref,
             dz_ref, dzkv_ref, dwm_ref, dsb_ref, dgv_ref, dgq_ref, dgk_ref, dsk_ref, dst_ref,
             carry_ref, tot_ref, sbacc_ref, skacc_ref, gqacc_ref, gkacc_ref):
        n = pl.program_id(0)
        left = lax.broadcasted_iota(jnp.int32, (1, LANES), 1) < HEAD_DIM

        @pl.when(n == 0)
        def _():
            for ref in (dwm_ref, dgv_ref, dst_ref, carry_ref, sbacc_ref, skacc_ref, gqacc_ref, gkacc_ref):
                ref[...] = jnp.zeros(ref.shape, ref.dtype)

        @pl.when(n < nb)
        def _():
            u, du_dz = _gelu_and_grad(_stack(z_ref, 0, A_HEADS))
            v, dv_dz = _gelu_and_grad(_stack(z_ref, A_WIDTH, A_HEADS))
            rv = lax.rsqrt(_lane_mean(v * v, l_ref) + EPS)
            vhat = v * rv
            gvp = gvp_ref[...]
            vn = (vhat * gvp).astype(BF16)
            rows = lambda a, h: a[h * BLOCK:(h + 1) * BLOCK]
            mixed = jnp.concatenate([_dot(wm_ref[h], rows(vn, h)) for h in range(A_HEADS)], axis=0) + sbp_ref[...]
            dya = _stack(dmix_ref, 0, A_HEADS).astype(F32)
            dmx = dya * u
            sbacc_ref[...] += dmx
            dmx_b = dmx.astype(BF16)
            for h in range(A_HEADS):
                dwm_ref[h] += _dot(rows(dmx_b, h), rows(vn, h), NT)
            dvn = jnp.concatenate([_dot(wmt_ref[h], rows(dmx_b, h)) for h in range(A_HEADS)], axis=0)
            dgv_ref[...] += jnp.sum(jnp.reshape(dvn * vhat, (A_HEADS, BLOCK, A_DIM)), axis=1)
            dzu = ((dya * mixed) * du_dz).astype(BF16)
            tv = dvn * gvp
            dzv = ((rv * (tv - vhat * _lane_mean(tv * vhat, l_ref))) * dv_dz).astype(BF16)
            for h in range(A_HEADS):
                dz_ref[:, h * A_DIM:(h + 1) * A_DIM] = rows(dzu, h)
                dz_ref[:, A_WIDTH + h * A_DIM:A_WIDTH + (h + 1) * A_DIM] = rows(dzv, h)

            kv = jnp.concatenate([zp_ref[...], z_ref[:, K_OFF:K_OFF + 2 * KV_WIDTH]], axis=0)
            k2, v2 = kv[:, :KV_WIDTH], kv[:, KV_WIDTH:]
            kn2 = k2 * lax.rsqrt(_seg_mean(k2 * k2, e_ref) + EPS) * gk2_ref[...]
            kn2_r, v2_r = pltpu.roll(kn2, HEAD_DIM, 1), pltpu.roll(v2, HEAD_DIM, 1)
            qp = _stack(z_ref, Q_OFF, PAIRS)
            rq = lax.rsqrt(_seg_mean(qp * qp, e_ref) + EPS)
            qhat = qp * rq
            gq2 = gq2_ref[...]
            qn = (qhat * gq2).astype(BF16)
            dop = _stack(dmix_ref, A_WIDTH, PAIRS)
            dqn_parts = []
            dk2n = jnp.zeros((BAND, KV_WIDTH), F32)
            dv2 = jnp.zeros((BAND, KV_WIDTH), F32)
            for kh in range(KV_HEADS):
                kbd, vbd = _block_diag(kn2, kn2_r, left, kh), _block_diag(v2, v2_r, left, kh)
                qg = qn[kh * PAIR_ROWS:(kh + 1) * PAIR_ROWS]
                dog = dop[kh * PAIR_ROWS:(kh + 1) * PAIR_ROWS]
                st = _dot(kbd, qg, NT) * (HEAD_DIM ** -0.5) + bias_ref[kh]
                halves = _pair_softmax(st, sk_ref, kh)
                dpt = _dot(vbd, dog, NT)
                ds_halves, t_halves = [], []
                for e, (w_e, w_sink) in enumerate(halves):
                    dp_e = dpt[e * BAND:(e + 1) * BAND]
                    delta = jnp.sum(w_e * dp_e, axis=0, keepdims=True)
                    ds_halves.append(w_e * (dp_e - delta))
                    t_halves.append(-(w_sink * delta))
                dst = jnp.concatenate(ds_halves, axis=0)
                dst_ref[kh] += dst
                skacc_ref[2 * kh:2 * kh + 2, :] += jnp.concatenate(t_halves, axis=0)
                ds_b = (dst * (HEAD_DIM ** -0.5)).astype(BF16)
                w_b = jnp.concatenate([w_e for w_e, _ in halves], axis=0).astype(BF16)
                dqn_parts.append(_dot(ds_b, kbd, TN))
                dk2n += _from_block_diag(_dot(ds_b, qg), left, kh)
                dv2 += _from_block_diag(_dot(w_b, dog), left, kh)
            dqn = jnp.concatenate(dqn_parts, axis=0)
            gqacc_ref[...] += jnp.sum(dqn * qhat, axis=0, keepdims=True)
            t = dqn * gq2
            dzq = (rq * (t - qhat * _seg_mean(t * qhat, e_ref))).astype(BF16)
            for p in range(PAIRS):
                dz_ref[:, Q_OFF + p * LANES:Q_OFF + (p + 1) * LANES] = rows(dzq, p)
            tot_ref[0] = carry_ref[0] + dk2n[:BLOCK]
            tot_ref[1] = carry_ref[1] + dv2[:BLOCK]
            carry_ref[0] = dk2n[BLOCK:]
            carry_ref[1] = dv2[BLOCK:]

        @pl.when(n == nb)
        def _():
            tot_ref[...] = carry_ref[...]

        kp = zp_ref[:, :KV_WIDTH]
        rk = lax.rsqrt(_seg_mean(kp * kp, e_ref) + EPS)
        khat = kp * rk
        dkn = tot_ref[0]
        gkacc_ref[...] += jnp.sum(dkn * khat, axis=0, keepdims=True)
        t = dkn * gk2_ref[...]
        dzkv_ref[:, :KV_WIDTH] = (rk * (t - khat * _seg_mean(t * khat, e_ref))).astype(BF16)
        dzkv_ref[:, KV_WIDTH:] = tot_ref[1].astype(BF16)

        @pl.when(n == nb)
        def _():
            dsb_ref[...] = jnp.broadcast_to(jnp.sum(sbacc_ref[...], axis=1, keepdims=True), dsb_ref.shape)
            dsk_ref[...] = lax.dot_general(skacc_ref[...], fold_ref[...], NN, precision=lax.Precision.HIGHEST,
                                           preferred_element_type=F32)
            dgq_ref[...] = gqacc_ref[...] + pltpu.roll(gqacc_ref[...], HEAD_DIM, 1)
            dgk_ref[...] = gkacc_ref[...] + pltpu.roll(gkacc_ref[...], HEAD_DIM, 1)

    last = nb - 1
    full, z_spec, zp_spec, consts = _mixer_specs(nb, last)
    return _pallas(
        body,
        name=name,
        grid=(nb + 1,),
        ex=ex,
        in_specs=[z_spec, zp_spec, pl.BlockSpec((BLOCK, A_WIDTH + B_WIDTH), lambda n: (jnp.minimum(n, last), 0)),
                  full((A_HEADS, BLOCK, BLOCK)), full((A_HEADS, BLOCK, BLOCK))] + consts + [full((PAIR_ROWS, LANES))],
        out_specs=[
            pl.BlockSpec((BLOCK, K_OFF), lambda n: (jnp.minimum(n, last), 0)),
            pl.BlockSpec((BLOCK, 2 * KV_WIDTH), lambda n: (jnp.maximum(n - 1, 0), 0)),
            full((A_HEADS, BLOCK, BLOCK)), full((A_HEADS * BLOCK, A_DIM)), full((A_HEADS, A_DIM)),
            full((1, LANES)), full((1, LANES)), full((SUBLANES, LANES)),
            full((KV_HEADS, PAIR_ROWS, 2 * BAND)),
        ],
        out_shape=[
            jax.ShapeDtypeStruct((S, K_OFF), BF16),
            jax.ShapeDtypeStruct((S, 2 * KV_WIDTH), BF16),
            jax.ShapeDtypeStruct((A_HEADS, BLOCK, BLOCK), F32),
            jax.ShapeDtypeStruct((A_HEADS * BLOCK, A_DIM), F32),
            jax.ShapeDtypeStruct((A_HEADS, A_DIM), F32),
            jax.ShapeDtypeStruct((1, LANES), F32),
            jax.ShapeDtypeStruct((1, LANES), F32),
            jax.ShapeDtypeStruct((SUBLANES, LANES), F32),
            jax.ShapeDtypeStruct((KV_HEADS, PAIR_ROWS, 2 * BAND), F32),
        ],
        scratch_shapes=[
            pltpu.VMEM((2, BLOCK, KV_WIDTH), F32),
            pltpu.VMEM((2, BLOCK, KV_WIDTH), F32),
            pltpu.VMEM((A_HEADS * BLOCK, A_DIM), F32),
            pltpu.VMEM((SUBLANES, PAIR_ROWS), F32),
            pltpu.VMEM((1, LANES), F32),
            pltpu.VMEM((1, LANES), F32),
        ],
        semantics=("arbitrary",),
        args=[z, z, dmix, wm, wm_t, sbp, gvp, gq2, gk2, seg_ones, lane_ones, sinks, biasp, pair_fold],
    )


def _mixer_out(mix, w_out, x, gt, name):
    S, D = x.shape
    K = mix.shape[1]
    tm, tn = _tile(S, 1024), _tile(D, 1024)

    def body(m_ref, w_ref, x_ref, gt_ref, xo_ref, y_ref):
        y = _dot(m_ref[...], w_ref[...])
        y_ref[...] = y.astype(BF16)
        xo_ref[...] = x_ref[...] + gt_ref[...] * y

    blk = pl.BlockSpec((tm, tn), lambda j, i: (i, j))
    return pl.pallas_call(
        body,
        name=name,
        grid=(D // tn, S // tm),
        in_specs=[pl.BlockSpec((tm, K), lambda j, i: (i, 0)), pl.BlockSpec((K, tn), lambda j, i: (0, j)),
                  blk, pl.BlockSpec((1, tn), lambda j, i: (0, j))],
        out_specs=[blk, blk],
        out_shape=[jax.ShapeDtypeStruct((S, D), F32), jax.ShapeDtypeStruct((S, D), BF16)],
        compiler_params=_params(("parallel", "parallel")),
    )(mix, w_out, x, gt)


def _bucket_sum(dst, onehot, name):
    def body(d_ref, o_ref, out_ref):
        out_ref[...] = lax.dot_general(o_ref[...], d_ref[...], NT, precision=lax.Precision.HIGHEST,
                                       preferred_element_type=F32)

    return pl.pallas_call(
        body,
        name=name,
        out_shape=jax.ShapeDtypeStruct((N_BUCKETS, B_HEADS), F32),
    )(dst, onehot)


def _adamw_math(w, g, m, v):
    m = ADAM_B1 * m + (1.0 - ADAM_B1) * g
    v = ADAM_B2 * v + (1.0 - ADAM_B2) * (g * g)
    m_hat = m / (1.0 - ADAM_B1 ** ADAM_STEP)
    v_hat = v / (1.0 - ADAM_B2 ** ADAM_STEP)
    delta = -ADAM_LR * (m_hat / (jnp.sqrt(v_hat) + ADAM_EPS) + ADAM_WD * w)
    return delta, m, v


def _adamw(w, g, m, v, name, ex=None, emit_grad=False):
    R, C = w.shape
    tr = _tile(R, max(SUBLANES, (1 << 19) // C), SUBLANES)

    def body(w_ref, g_ref, m_ref, v_ref, *out_refs):
        gv = g_ref[...]
        results = _adamw_math(w_ref[...], gv, m_ref[...], v_ref[...])
        for ref, val in zip(out_refs, ((gv,) if emit_grad else ()) + results):
            ref[...] = val

    blk = pl.BlockSpec((tr, C), lambda i: (i, 0))
    shape = jax.ShapeDtypeStruct((R, C), F32)
    n_out = 4 if emit_grad else 3
    return _pallas(body, name=name, grid=(R // tr,), in_specs=[blk] * 4, out_specs=[blk] * n_out,
                   out_shape=[shape] * n_out, semantics=("parallel",), args=[w, g, m, v], ex=ex)


def _small_update(parts, w, m, v, name):
    R = w.shape[0]

    def body(p_ref, w_ref, m_ref, v_ref, g_ref, d_ref, mo_ref, vo_ref):
        g = p_ref[0]
        for dev in range(1, N_DEV):
            g = g + p_ref[dev]
        g_ref[...] = g
        d, mn, vn = _adamw_math(w_ref[...], g, m_ref[...], v_ref[...])
        d_ref[...] = d
        mo_ref[...] = mn
        vo_ref[...] = vn

    shape = jax.ShapeDtypeStruct((R, LANES), F32)
    return pl.pallas_call(
        body,
        name=name,
        out_shape=[shape] * 4,
        compiler_params=pltpu.CompilerParams(vmem_limit_bytes=VMEM_LIMIT),
    )(parts, w, m, v)


def _place():
    x, y, c = lax.axis_index("x"), lax.axis_index("y"), lax.axis_index("c")
    chips = [(1 - x, y), (x, 1 - y), (1 - x, 1 - y)]
    return x, y, c, chips


def _remote(src, dst, send_sem, recv_sem, to):
    return pltpu.make_async_remote_copy(src_ref=src, dst_ref=dst, send_sem=send_sem, recv_sem=recv_sem,
                                        device_id=to, device_id_type=MESH)


def _allgather_small(block, name):
    m_per, n = block.shape

    def body(x_ref, out_ref, send_sems, recv_sems, local_sem):
        x, y, c, chips = _place()
        me, sibling = (x, y, c), (x, y, 1 - c)

        def rows(px, py, pc):
            return out_ref.at[pl.ds((4 * px + 2 * py + pc) * m_per, m_per), :]

        def copy(k, blk, to, src=None):
            return _remote(rows(*blk) if src is None else src, rows(*blk), send_sems.at[k], recv_sems.at[k], to)

        mine = pltpu.make_async_copy(x_ref, rows(*me), local_sem)
        mine.start()
        first = [copy(0, me, sibling, src=x_ref)]
        first += [copy(1 + j, me, (*chip, c), src=x_ref) for j, chip in enumerate(chips)]
        for cp in first:
            cp.start()
        passed = [copy(4 + j, (*chip, c), sibling) for j, chip in enumerate(chips)]
        for j, chip in enumerate(chips):
            copy(1 + j, (*chip, c), me).wait_recv()
            passed[j].start()
        copy(0, sibling, me).wait_recv()
        for j, chip in enumerate(chips):
            copy(4 + j, (*chip, 1 - c), me).wait_recv()
        for cp in first + passed:
            cp.wait_send()
        mine.wait()

    return pl.pallas_call(
        body,
        name=name,
        out_shape=jax.ShapeDtypeStruct((N_DEV * m_per, n), block.dtype),
        in_specs=[pl.BlockSpec(memory_space=pltpu.VMEM)],
        out_specs=pl.BlockSpec(memory_space=pltpu.VMEM),
        scratch_shapes=[pltpu.SemaphoreType.DMA((7,)), pltpu.SemaphoreType.DMA((7,)), pltpu.SemaphoreType.DMA],
        compiler_params=pltpu.CompilerParams(vmem_limit_bytes=VMEM_LIMIT),
    )(block)


def _half(ref, c, rows):
    start = pl.multiple_of(c * rows, BF16_ROWS)
    if len(ref.shape) == 2:
        return ref.at[pl.ds(start, rows), :]
    return ref.at[:, pl.ds(start, rows), :]


def _same(arrays):
    return [jax.ShapeDtypeStruct(a.shape, a.dtype) for a in arrays], {t: t for t in range(len(arrays))}


def _ex_gather_ici(bufs):
    def plan(ins, outs, send_sems, recv_sems):
        x, y, c, chips = _place()
        sends, arrivals = [], []
        for t, buf in enumerate(bufs):
            rows = buf.shape[1] // 2
            mine = _half(outs[t].at[2 * x + y], c, rows)
            for k, (px, py) in enumerate(chips):
                sems = (send_sems.at[3 * t + k], recv_sems.at[3 * t + k], (px, py, c))
                landed = _half(outs[t].at[2 * px + py], c, rows)
                sends.append((mine, mine, *sems))
                arrivals.append((landed, landed, *sems))
        return sends, arrivals

    shapes, aliases = _same(bufs)
    return _Exchange(bufs, shapes, aliases, 3 * len(bufs), plan)


def _ex_gather_d2d(bufs):
    def plan(ins, outs, send_sems, recv_sems):
        x, y, c, chips = _place()
        sends, arrivals = [], []
        for t, buf in enumerate(bufs):
            rows = buf.shape[1] // 2
            for k, (px, py) in enumerate(chips):
                sems = (send_sems.at[3 * t + k], recv_sems.at[3 * t + k], (x, y, 1 - c))
                landed = _half(outs[t].at[2 * px + py], c, rows)
                other = _half(outs[t].at[2 * px + py], 1 - c, rows)
                sends.append((landed, landed, *sems))
                arrivals.append((other, other, *sems))
        return sends, arrivals

    shapes, aliases = _same(bufs)
    return _Exchange(bufs, shapes, aliases, 3 * len(bufs), plan)


def _ex_swap_halves(grads):
    def plan(ins, outs, send_sems, recv_sems):
        x, y, c, _ = _place()
        sends = [(_half(ins[t], 1 - c, g.shape[1] // 2), outs[t], send_sems.at[t], recv_sems.at[t], (x, y, 1 - c))
                 for t, g in enumerate(grads)]
        return sends, sends

    shapes = [jax.ShapeDtypeStruct((g.shape[0], g.shape[1] // 2, g.shape[2]), g.dtype) for g in grads]
    return _Exchange(grads, shapes, {}, len(grads), plan)


def _ex_scatter(sums):
    def plan(ins, outs, send_sems, recv_sems):
        x, y, c, chips = _place()
        sends = [(ins[t].at[2 * px + py], outs[t].at[k], send_sems.at[3 * t + k], recv_sems.at[3 * t + k], (px, py, c))
                 for t in range(len(sums)) for k, (px, py) in enumerate(chips)]
        return sends, sends

    shapes = [jax.ShapeDtypeStruct((N_CHIPS - 1,) + s.shape[1:], s.dtype) for s in sums]
    return _Exchange(sums, shapes, {}, 3 * len(sums), plan)


def _ex_join_halves(fulls):
    def plan(ins, outs, send_sems, recv_sems):
        x, y, c, _ = _place()
        sends, arrivals = [], []
        for t, full in enumerate(fulls):
            rows = full.shape[0] // 2
            sems = (send_sems.at[t], recv_sems.at[t], (x, y, 1 - c))
            mine, other = _half(outs[t], c, rows), _half(outs[t], 1 - c, rows)
            sends.append((mine, mine, *sems))
            arrivals.append((other, other, *sems))
        return sends, arrivals

    shapes, aliases = _same(fulls)
    return _Exchange(fulls, shapes, aliases, len(fulls), plan)


class _Shifted:
    def __init__(self, sems, offset):
        self.sems, self.offset = sems, offset

    @property
    def at(self):
        return self

    def __getitem__(self, k):
        return self.sems.at[self.offset + k]


def _combine(exchanges):
    operands, out_shapes, aliases, starts = [], [], {}, []
    n_sems = 0
    for e in exchanges:
        starts.append((len(operands), len(out_shapes), n_sems))
        aliases.update({len(operands) + i: len(out_shapes) + o for i, o in e.aliases.items()})
        operands += list(e.operands)
        out_shapes += list(e.out_shapes)
        n_sems += e.n_sems

    def plan(ins, outs, send_sems, recv_sems):
        sends, arrivals = [], []
        for e, (i0, o0, s0) in zip(exchanges, starts):
            s, a = e.plan(ins[i0:i0 + len(e.operands)], outs[o0:o0 + len(e.out_shapes)],
                          _Shifted(send_sems, s0), _Shifted(recv_sems, s0))
            sends += s
            arrivals += a
        return sends, arrivals

    return _Exchange(operands, out_shapes, aliases, n_sems, plan)


class _Reduction:
    def __init__(self, grad, tag, c_arr, jc_arr):
        self.grad, self.tag, self.c_arr, self.jc_arr, self.stage = grad, tag, c_arr, jc_arr, 0

    def exchange(self):
        if self.stage == 0:
            return _ex_swap_halves([self.grad])
        if self.stage == 1:
            return _ex_scatter([self.sums])
        return _ex_join_halves([self.full])

    def advance(self, landed):
        if self.stage == 0:
            self.recv = landed
            self.sums = _chip_sum(self.grad, landed, self.c_arr, f"chip_sum_{self.tag}")
        elif self.stage == 1:
            self.full = _owner_sum(self.grad, self.recv, landed, self.jc_arr, f"owner_sum_{self.tag}")
        else:
            self.result = landed
        self.stage += 1


def _ride(reductions):
    def done(carried):
        for r, landed in zip(reductions, carried):
            r.advance(landed)

    return _combine([r.exchange() for r in reductions]), done


def _exchange_alone(ex, name):
    return _pallas(None, name=name, grid=(), in_specs=[], out_specs=[], out_shape=[], args=[], ex=ex)


def _cast_to_slot(w, chip_arr, name):
    A, B = w.shape
    ta = _tile(A, max(BF16_ROWS, (1 << 19) // B), BF16_ROWS)

    def body(j_ref, w_ref, o_ref):
        o_ref[...] = w_ref[...].astype(BF16)

    return pl.pallas_call(
        body,
        name=name,
        grid_spec=pltpu.PrefetchScalarGridSpec(
            num_scalar_prefetch=1,
            grid=(A // ta,),
            in_specs=[pl.BlockSpec((ta, B), lambda i, j_ref: (i, 0))],
            out_specs=pl.BlockSpec((None, ta, B), lambda i, j_ref: (j_ref[0], i, 0)),
        ),
        out_shape=jax.ShapeDtypeStruct((N_CHIPS, A, B), BF16),
        compiler_params=_params(("parallel",)),
    )(chip_arr, w)


def _chip_sum(grad, recv, c_arr, name):
    _, A, B = grad.shape
    hA = A // 2
    ta = _tile(hA, max(BF16_ROWS, (1 << 19) // B), BF16_ROWS)
    nh = hA // ta

    def body(c_ref, g_ref, r_ref, o_ref):
        o_ref[...] = (g_ref[...] + r_ref[...]).astype(BF16)

    return pl.pallas_call(
        body,
        name=name,
        grid_spec=pltpu.PrefetchScalarGridSpec(
            num_scalar_prefetch=1,
            grid=(N_CHIPS, nh),
            in_specs=[pl.BlockSpec((None, ta, B), lambda s, i, c_ref: (s, c_ref[0] * nh + i, 0)),
                      pl.BlockSpec((None, ta, B), lambda s, i, c_ref: (s, i, 0))],
            out_specs=pl.BlockSpec((None, ta, B), lambda s, i, c_ref: (s, i, 0)),
        ),
        out_shape=jax.ShapeDtypeStruct((N_CHIPS, hA, B), BF16),
        compiler_params=_params(("parallel", "parallel")),
    )(c_arr, grad, recv)


def _owner_sum(grad, recv, landed, jc_arr, name):
    _, A, B = grad.shape
    hA = A // 2
    ta = _tile(hA, max(BF16_ROWS, (1 << 19) // B), BF16_ROWS)
    nh = hA // ta

    def body(jc_ref, g_ref, r_ref, l0_ref, l1_ref, l2_ref, o_ref):
        total = g_ref[...] + r_ref[...]
        for ref in (l0_ref, l1_ref, l2_ref):
            total = total + ref[...].astype(F32)
        o_ref[...] = total

    def landed_spec(k):
        return pl.BlockSpec((None, ta, B), lambda i, jc_ref: (k, i, 0))

    return pl.pallas_call(
        body,
        name=name,
        grid_spec=pltpu.PrefetchScalarGridSpec(
            num_scalar_prefetch=1,
            grid=(nh,),
            in_specs=[pl.BlockSpec((None, ta, B), lambda i, jc_ref: (jc_ref[0], jc_ref[1] * nh + i, 0)),
                      pl.BlockSpec((None, ta, B), lambda i, jc_ref: (jc_ref[0], i, 0)),
                      landed_spec(0), landed_spec(1), landed_spec(2)],
            out_specs=pl.BlockSpec((ta, B), lambda i, jc_ref: (jc_ref[1] * nh + i, 0)),
        ),
        out_shape=jax.ShapeDtypeStruct((A, B), F32),
        compiler_params=_params(("parallel",)),
    )(jc_arr, grad, recv, landed, landed, landed)


def _pack(parts):
    rows = []
    for p in parts:
        flat = jnp.reshape(p.astype(F32), (-1,))
        tile = SUBLANES * LANES
        padded = -(-flat.shape[0] // tile) * tile
        rows.append(jnp.reshape(jnp.pad(flat, (0, padded - flat.shape[0])), (-1, LANES)))
    return jnp.concatenate(rows, axis=0)


def _unpack(pack, shapes):
    out, row = [], 0
    for shape in shapes:
        size = int(np.prod(shape))
        nrows = -(-size // (SUBLANES * LANES)) * SUBLANES
        out.append(jnp.reshape(jnp.reshape(pack[row:row + nrows], (-1,))[:size], shape))
        row += nrows
    return out


def _bias_tables():
    qi = np.arange(BLOCK)[:, None]
    kj = np.arange(2 * BLOCK)[None, :]
    dist = qi + BLOCK - kj
    in_window = (dist >= 0) & (dist < BLOCK)
    n = np.clip(dist, 0, None)
    max_exact = N_BUCKETS // 2
    nf = np.maximum(n, 1).astype(np.float32)
    large = max_exact + (np.log(nf / max_exact) / math.log(MAX_DISTANCE / max_exact)
                         * (N_BUCKETS - max_exact)).astype(np.int32)
    large = np.minimum(large, N_BUCKETS - 1)
    bucket = np.where(n < max_exact, n, large)
    onehot = (bucket[None] == np.arange(N_BUCKETS)[:, None, None]) & in_window[None]
    first = in_window & (kj >= BLOCK)
    return onehot.astype(np.float32), in_window, first


def kernel(x, c, w_ada, b_ada, g_ffn1, w1_ffn1, w3_ffn1, w2_ffn1, g_mix, w_in, spatial_w, spatial_b, g_v, g_q, g_k, sinks, rel_bias, w_out, g_ffn2, w1_ffn2, w3_ffn2, w2_ffn2, loss_target, m_w_ada, m_b_ada, m_g_ffn1, m_w1_ffn1, m_w3_ffn1, m_w2_ffn1, m_g_mix, m_w_in, m_spatial_w, m_spatial_b, m_g_v, m_g_q, m_g_k, m_sinks, m_rel_bias, m_w_out, m_g_ffn2, m_w1_ffn2, m_w3_ffn2, m_w2_ffn2, v_w_ada, v_b_ada, v_g_ffn1, v_w1_ffn1, v_w3_ffn1, v_w2_ffn1, v_g_mix, v_w_in, v_spatial_w, v_spatial_b, v_g_v, v_g_q, v_g_k, v_sinks, v_rel_bias, v_w_out, v_g_ffn2, v_w1_ffn2, v_w3_ffn2, v_w2_ffn2):
    ax, ay, ac = lax.axis_index("x"), lax.axis_index("y"), lax.axis_index("c")
    chip = 2 * ax + ay
    dev = 2 * chip + ac
    xs = x[0]
    tgt = loss_target[0]
    S, D = xs.shape
    F = N_CHIPS * w1_ffn1.shape[2]
    mod_cols = w_ada.shape[2]

    c_all = _allgather_small(jnp.pad(c, ((0, SUBLANES - 1), (0, 0))), "gather_c")
    c_all = jnp.pad(c_all[::SUBLANES], ((0, BF16_ROWS - N_DEV), (0, 0)))
    b_sh = lax.dynamic_slice(b_ada, (0, chip * mod_cols), (1, mod_cols))
    mod_part, c_act = _mod_partial(c_all, w_ada[0], b_sh, "mod_partial")
    mod_all = _allgather_small(mod_part[:N_DEV], "gather_mod")
    mod_all = jnp.reshape(mod_all, (N_CHIPS, 2, N_DEV, mod_cols))[:, 0]
    mod = jnp.reshape(lax.dynamic_index_in_dim(mod_all, dev, axis=1, keepdims=False), (1, N_MOD * D))
    sh1, sc1, gt1, sh2, sc2, gt2, sh3, sc3, gt3 = [mod[:, i * D:(i + 1) * D] for i in range(N_MOD)]

    def cols_to_natural(w4):
        return jnp.reshape(jnp.transpose(w4, (1, 0, 2)), (w4.shape[1], -1))

    chip_arr = jnp.reshape(chip, (1,)).astype(jnp.int32)
    c_arr = jnp.reshape(ac, (1,)).astype(jnp.int32)
    jc_arr = jnp.stack([chip, ac]).astype(jnp.int32)
    cast = lambda w, nm: _cast_to_slot(w[0], chip_arr, f"cast_{nm}")
    ffn1_bufs = [cast(w1_ffn1, "w1_ffn1"), cast(w3_ffn1, "w3_ffn1"), cast(w2_ffn1, "w2_ffn1")]
    mixer_bufs = [cast(w_in, "w_in"), cast(w_out, "w_out")]
    ffn2_bufs = [cast(w1_ffn2, "w1_ffn2"), cast(w3_ffn2, "w3_ffn2"), cast(w2_ffn2, "w2_ffn2")]
    ffn1_bufs = _exchange_alone(_ex_gather_ici(ffn1_bufs), "gather_ffn1_ici")
    ffn1_bufs = _exchange_alone(_ex_gather_d2d(ffn1_bufs), "gather_ffn1_d2d")
    w1a, w3a, w2a = cols_to_natural(ffn1_bufs[0]), cols_to_natural(ffn1_bufs[1]), jnp.reshape(ffn1_bufs[2], (F, D))

    onehot_np, in_window_np, first_np = _bias_tables()
    onehot = jnp.asarray(onehot_np)
    bias = jnp.einsum("bij,bh->hij", onehot, rel_bias, precision=lax.Precision.HIGHEST)
    biasm = jnp.stack([jnp.where(jnp.asarray(first_np)[None], bias, NEG),
                       jnp.where(jnp.asarray(in_window_np)[None], bias, NEG)])
    causal = jnp.asarray(np.tril(np.ones((BLOCK, BLOCK), dtype=bool)))
    wm = jnp.where(causal[None], spatial_w[0], 0.0).astype(BF16)
    wm_t = jnp.transpose(wm, (0, 2, 1))
    sink_vec = sinks[0]
    per_group = PAIRS // KV_HEADS
    sbp = jnp.broadcast_to(jnp.reshape(spatial_b[0], (A_HEADS * BLOCK, 1)), (A_HEADS * BLOCK, A_DIM))
    gvp = jnp.repeat(g_v[0], BLOCK, axis=0)
    gq2, gk2 = jnp.concatenate([g_q, g_q], axis=1), jnp.concatenate([g_k, g_k], axis=1)
    seg_ones = jnp.asarray(np.kron(np.eye(2, dtype=np.float32), np.ones((HEAD_DIM, HEAD_DIM), np.float32)), BF16)
    lane_ones = jnp.full((LANES, LANES), 1.0 / LANES, BF16)
    pair_fold = jnp.asarray(np.kron(np.eye(per_group, LANES, dtype=np.float32), np.ones((BLOCK, 1), np.float32)))
    biasp = jnp.reshape(jnp.transpose(jnp.reshape(biasm, (2, KV_HEADS, per_group, 2, BLOCK, BAND)), (0, 1, 3, 5, 2, 4)),
                        (2, KV_HEADS, 2 * BAND, PAIR_ROWS))

    res = _ffn_fwd(xs, g_ffn1, sh1, sc1, gt1, w1a, w3a, w2a, None, "ffn1_fwd", ex=_ex_gather_ici(mixer_bufs + ffn2_bufs))
    (x1, h1, a1, b1, f1), mixer_bufs, ffn2_bufs = res[:5], res[5:7], res[7:]
    h2, *mixer_bufs = _norm_mod(x1, g_mix, sh2, sc2, "mixer_norm", ex=_ex_gather_d2d(mixer_bufs))
    win, wout = cols_to_natural(mixer_bufs[0]), jnp.reshape(mixer_bufs[1], (-1, D))
    z, *ffn2_bufs = _matmul(h2, win, "nn", F32, 1024, _tile(IN_COLS, 1664), D, "mixer_in", ex=_ex_gather_d2d(ffn2_bufs))
    w1b, w3b, w2b = cols_to_natural(ffn2_bufs[0]), cols_to_natural(ffn2_bufs[1]), jnp.reshape(ffn2_bufs[2], (F, D))
    mix = _mixer_fwd(z, wm, sbp, gvp, gq2, gk2, seg_ones, lane_ones, sink_vec, biasp, "mixer_fwd")
    x2, ymix = _mixer_out(mix, wout, x1, gt2, "mixer_out")
    g3, df3, h3, a3, b3, dgt3, loss_sum = _ffn_fwd(x2, g_ffn2, sh3, sc3, gt3, w1b, w3b, w2b, tgt, "ffn2_fwd_loss")
    loss = lax.psum(loss_sum[0, 0] * (0.5 / D), ("x", "y", "c"))

    tk = _tile(S, 2048)

    def ffn_weight_grads(h, da, db, s, df, tag, riding):
        ex, done = _ride(riding) if riding else (None, None)
        gw1 = _matmul(h, da, "tn", F32, 1024, F // N_CHIPS, tk, f"grad_w1_{tag}", shard_major=True, ex=ex)
        if riding:
            done(gw1[1:])
            gw1 = gw1[0]
        r1 = _Reduction(gw1, f"w1_{tag}", c_arr, jc_arr)
        ex, done = _ride([r1])
        gw3, *carried = _matmul(h, db, "tn", F32, 1024, F // N_CHIPS, tk, f"grad_w3_{tag}", shard_major=True, ex=ex)
        done(carried)
        r3 = _Reduction(gw3, f"w3_{tag}", c_arr, jc_arr)
        ex, done = _ride([r1, r3])
        gw2, *carried = _matmul(s, df, "tn", F32, _tile(F, 1408), 1024, tk, f"grad_w2_{tag}", ex=ex)
        done(carried)
        r2 = _Reduction(jnp.reshape(gw2, (N_CHIPS, F // N_CHIPS, D)), f"w2_{tag}", c_arr, jc_arr)
        return r1, r3, r2

    da3, db3, s3, dh3 = _ffn_bwd(df3, a3, b3, w1b, w3b, w2b, "ffn2_bwd")
    r21, r23, r22 = ffn_weight_grads(h3, da3, db3, s3, df3, "ffn2", [])
    ex, done = _ride([r22])
    res = _norm_bwd(dh3, x2, g3, g_ffn2, sc3, (ymix, gt2, 1.0), "ffn2_norm_bwd", ex=ex)
    g2, dsh3, dsc3, dgn3, dy, dgt2 = res[:6]
    done(res[6:])

    ex, done = _ride([r21])
    dmix, *carried = _matmul(dy, wout, "nt", BF16, 1024, 2048, D, "mixer_out_bwd", ex=ex)
    done(carried)
    ex, done = _ride([r23, r22])
    res = _mixer_bwd(z, dmix, wm, wm_t, sbp, gvp, gq2, gk2, seg_ones, lane_ones, sink_vec, biasp, pair_fold,
                     "mixer_bwd", ex=ex)
    dz_main, dz_kv, dwm, dsb, dgv, dgq, dgk, dsk, dst = res[:9]
    dsb = jnp.reshape(dsb[:, 0], (A_HEADS, BLOCK))
    dgq, dgk = dgq[:, :HEAD_DIM], dgk[:, :HEAD_DIM]
    dsk = jnp.reshape(jnp.transpose(jnp.reshape(dsk[:2 * KV_HEADS, :per_group], (KV_HEADS, 2, per_group)), (0, 2, 1)),
                      (1, B_HEADS))
    dst = jnp.reshape(jnp.transpose(jnp.reshape(dst, (KV_HEADS, 2, BAND, per_group, BLOCK)), (0, 3, 1, 4, 2)),
                      (B_HEADS, BLOCK * BAND))
    done(res[9:])
    dz = jnp.concatenate([dz_main, dz_kv], axis=1)
    ex, done = _ride([r23, r22])
    dh2, *carried = _matmul(dz, win, "nt", F32, 1024, 2048, _tile(IN_COLS, 1664), "mixer_in_bwd", ex=ex)
    done(carried)
    drel = _bucket_sum(dst, jnp.reshape(onehot, (N_BUCKETS, -1)), "bucket_sum")
    g1, dsh2, dsc2, dgn2, df1, dgt1 = _norm_bwd(dh2, x1, g2, g_mix, sc2, (f1, gt1, 0.5), "mixer_norm_bwd")

    da1, db1, s1, dh1 = _ffn_bwd(df1, a1, b1, w1a, w3a, w2a, "ffn1_bwd")
    r11, r13, r12 = ffn_weight_grads(h1, da1, db1, s1, df1, "ffn1", [])
    ex, done = _ride([r11, r13, r12])
    gwin_full, *carried = _matmul(h2, dz, "tn", F32, 1024, _tile(IN_COLS, 1664), tk, "grad_w_in", ex=ex)
    done(carried)
    rm_in = _Reduction(jnp.transpose(jnp.reshape(gwin_full, (D, N_CHIPS, -1)), (1, 0, 2)), "w_in", c_arr, jc_arr)
    grad_x, dsh1, dsc1, dgn1 = _norm_bwd(dh1, xs, g1, g_ffn1, sc1, None, "ffn1_norm_bwd")
    ex, done = _ride([r13, r12, rm_in])
    gwout_full, *carried = _matmul(mix, dy, "tn", F32, 1024, 1024, tk, "grad_w_out", ex=ex)
    done(carried)
    rm_out = _Reduction(jnp.reshape(gwout_full, (N_CHIPS, -1, D)), "w_out", c_arr, jc_arr)
    for stage, riding in enumerate([[r12, rm_in, rm_out], [rm_in, rm_out], [rm_out]]):
        ex, done = _ride(riding)
        done(_exchange_alone(ex, f"reduce_tail_{stage}"))

    dmod = jnp.concatenate([dsh1, dsc1, dgt1, dsh2, dsc2, dgt2, dsh3, dsc3, dgt3], axis=1)
    small_w = [b_ada, g_ffn1, g_mix, g_ffn2, spatial_w, spatial_b, g_v, g_q, g_k, sinks, rel_bias]
    small_m = [m_b_ada, m_g_ffn1, m_g_mix, m_g_ffn2, m_spatial_w, m_spatial_b, m_g_v, m_g_q, m_g_k, m_sinks, m_rel_bias]
    small_v = [v_b_ada, v_g_ffn1, v_g_mix, v_g_ffn2, v_spatial_w, v_spatial_b, v_g_v, v_g_q, v_g_k, v_sinks, v_rel_bias]
    small_g = [dmod, dgn1, dgn2, dgn3, jnp.where(causal[None], dwm, 0.0), dsb, dgv, dgq, dgk, dsk, drel]
    shapes = [w.shape for w in small_w]
    gpack = _pack(small_g)
    rows = gpack.shape[0]
    gall = jnp.reshape(_allgather_small(gpack, "gather_small"), (N_DEV, rows, LANES))
    sg, sd, sm, sv = _small_update(gall, _pack(small_w), _pack(small_m), _pack(small_v), "small_update")
    sg, sd, sm, sv = [_unpack(p, shapes) for p in (sg, sd, sm, sv)]

    mod_rows = -(-N_MOD * D // (SUBLANES * LANES)) * SUBLANES
    dmod_all = jnp.reshape(gall[:, :mod_rows], (N_DEV, -1))[:, :N_MOD * D]
    dmod_sh = lax.dynamic_slice(dmod_all, (0, chip * mod_cols), (N_DEV, mod_cols))
    dmod_sh = jnp.pad(dmod_sh, ((0, BF16_ROWS - N_DEV), (0, 0))).astype(BF16)
    g_wada = _matmul(c_act, dmod_sh, "tn", F32, 1024, _tile(mod_cols, 512), BF16_ROWS, "grad_w_ada")
    d_wada, nm_wada, nv_wada = _adamw(w_ada[0], g_wada, m_w_ada[0], v_w_ada[0], "adamw_w_ada")

    gf1 = [r11.result, r13.result, r12.result]
    gf2 = [r21.result, r23.result, r22.result]
    gmx = [rm_in.result, rm_out.result]
    big = {}
    for nm, w, g, m, v in [
        ("w1_ffn1", w1_ffn1, gf1[0], m_w1_ffn1, v_w1_ffn1), ("w3_ffn1", w3_ffn1, gf1[1], m_w3_ffn1, v_w3_ffn1),
        ("w2_ffn1", w2_ffn1, gf1[2], m_w2_ffn1, v_w2_ffn1), ("w_in", w_in, gmx[0], m_w_in, v_w_in),
        ("w_out", w_out, gmx[1], m_w_out, v_w_out), ("w1_ffn2", w1_ffn2, gf2[0], m_w1_ffn2, v_w1_ffn2),
        ("w3_ffn2", w3_ffn2, gf2[1], m_w3_ffn2, v_w3_ffn2), ("w2_ffn2", w2_ffn2, gf2[2], m_w2_ffn2, v_w2_ffn2),
    ]:
        g_out, d, nm_, nv_ = _adamw(w[0], g, m[0], v[0], f"adamw_{nm}", emit_grad=True)
        big[nm] = (g_out[None], d[None], nm_[None], nv_[None])
    big["w_ada"] = (g_wada[None], d_wada[None], nm_wada[None], nv_wada[None])

    order = ["w_ada", "b_ada", "g_ffn1", "w1_ffn1", "w3_ffn1", "w2_ffn1", "g_mix", "w_in", "spatial_w", "spatial_b",
             "g_v", "g_q", "g_k", "sinks", "rel_bias", "w_out", "g_ffn2", "w1_ffn2", "w3_ffn2", "w2_ffn2"]
    small_names = ["b_ada", "g_ffn1", "g_mix", "g_ffn2", "spatial_w", "spatial_b", "g_v", "g_q", "g_k", "sinks", "rel_bias"]
    for i, nm in enumerate(small_names):
        big[nm] = (sg[i], sd[i], sm[i], sv[i])
    outs = [loss, grad_x[None]]
    for kind in range(4):
        outs += [big[nm][kind] for nm in order]
    return tuple(outs)
```

```python
import functools
import math

import jax
import jax.numpy as jnp
import numpy as np
from jax import lax
from jax.experimental import pallas as pl
from jax.experimental.pallas import tpu as pltpu

F32 = jnp.float32
BF16 = jnp.bfloat16
MESH = pl.DeviceIdType.MESH
ANY = pl.BlockSpec(memory_space=pl.ANY)

EPS = 1e-6
BLOCK = 128
A_HEADS = 8
A_DIM = 128
A_WIDTH = A_HEADS * A_DIM
B_HEADS = 16
KV_HEADS = 2
GROUP = B_HEADS // KV_HEADS
HEAD_DIM = 64
B_WIDTH = B_HEADS * HEAD_DIM
KV_WIDTH = KV_HEADS * HEAD_DIM
Q_OFF = 2 * A_WIDTH
K_OFF = Q_OFF + B_WIDTH
V_OFF = K_OFF + KV_WIDTH
IN_COLS = V_OFF + KV_WIDTH
N_BUCKETS = 32
MAX_DISTANCE = 128
N_MOD = 9
N_CHIPS = 4
N_DEV = 8
NEG = -1e30

ADAM_LR = 0.001
ADAM_B1 = 0.9
ADAM_B2 = 0.999
ADAM_EPS = 1e-08
ADAM_WD = 0.01
ADAM_STEP = 10

LANES = 128
SUBLANES = 8
BF16_ROWS = 16
VMEM_LIMIT = 60 * 1024 * 1024

INV_SQRT2 = 1.0 / math.sqrt(2.0)
INV_SQRT_2PI = 1.0 / math.sqrt(2.0 * math.pi)


def _tile(n, pref, mult=LANES):
    t = (min(pref, n) // mult) * mult
    while t >= mult:
        if n % t == 0:
            return t
        t -= mult
    return n


def _params(sem):
    return pltpu.CompilerParams(dimension_semantics=sem, vmem_limit_bytes=VMEM_LIMIT)


class _Exchange:
    def __init__(self, operands, out_shapes, aliases, n_sems, plan):
        self.operands, self.out_shapes, self.aliases, self.n_sems, self.plan = operands, out_shapes, aliases, n_sems, plan


def _pallas(body, *, name, grid, in_specs, out_specs, out_shape, args, scratch_shapes=(), semantics=None, ex=None,
            after=()):
    if ex is None:
        n_in = len(in_specs)

        def ordered(*refs):
            body(*refs[:n_in], *refs[n_in + len(after):])

        return pl.pallas_call(ordered if after else body, name=name, grid=grid,
                              in_specs=list(in_specs) + [ANY] * len(after), out_specs=out_specs, out_shape=out_shape,
                              scratch_shapes=list(scratch_shapes), compiler_params=_params(semantics))(*args, *after)
    assert not after
    n_in, n_out, n_scr = len(in_specs), len(out_specs), len(scratch_shapes)
    e_in, e_out = len(ex.operands), len(ex.out_shapes)

    def wrapped(*refs):
        ins, refs = refs[:n_in], refs[n_in:]
        ex_ins, refs = refs[:e_in], refs[e_in:]
        outs, refs = refs[:n_out], refs[n_out:]
        ex_outs, refs = refs[:e_out], refs[e_out:]
        scratch, (send_sems, recv_sems) = refs[:n_scr], refs[n_scr:]
        first, last = True, True
        for d, size in enumerate(grid):
            first = jnp.logical_and(first, pl.program_id(d) == 0)
            last = jnp.logical_and(last, pl.program_id(d) == size - 1)

        def start():
            sends, _ = ex.plan(ex_ins, ex_outs, send_sems, recv_sems)
            for cp in sends:
                _remote(*cp).start()

        def finish():
            sends, arrivals = ex.plan(ex_ins, ex_outs, send_sems, recv_sems)
            for cp in arrivals:
                _remote(*cp).wait_recv()
            for cp in sends:
                _remote(*cp).wait_send()

        if grid:
            pl.when(first)(start)
        else:
            start()
        if body is not None:
            body(*ins, *outs, *scratch)
        if grid:
            pl.when(last)(finish)
        else:
            finish()

    kwargs = dict(grid=grid) if grid else {}
    return pl.pallas_call(
        wrapped,
        name=name,
        in_specs=list(in_specs) + [ANY] * e_in,
        out_specs=list(out_specs) + [ANY] * e_out,
        out_shape=list(out_shape) + list(ex.out_shapes),
        input_output_aliases={n_in + i: n_out + o for i, o in ex.aliases.items()},
        scratch_shapes=list(scratch_shapes) + [pltpu.SemaphoreType.DMA((ex.n_sems,)), pltpu.SemaphoreType.DMA((ex.n_sems,))],
        compiler_params=_params(("arbitrary",) * len(grid) if grid else None),
        **kwargs,
    )(*args, *ex.operands)


def _dot(a, b, dims=(((1,), (0,)), ((), ()))):
    return lax.dot_general(a, b, dims, preferred_element_type=F32)


NN = (((1,), (0,)), ((), ()))
NT = (((1,), (1,)), ((), ()))
TN = (((0,), (0,)), ((), ()))


def _sigmoid(x):
    return 1.0 / (1.0 + jnp.exp(-x))


def _gelu_and_grad(x):
    cdf = 0.5 * (1.0 + lax.erf(x * INV_SQRT2))
    pdf = jnp.exp(-0.5 * x * x) * INV_SQRT_2PI
    return x * cdf, cdf + x * pdf


def _gelu(x):
    return x * (0.5 * (1.0 + lax.erf(x * INV_SQRT2)))


def _rms(x):
    r = lax.rsqrt(jnp.mean(x * x, axis=-1, keepdims=True) + EPS)
    return x * r, r


ROW_CHUNK = 64


def _for_rows(tm, fn):
    rc = min(ROW_CHUNK, tm)

    def step(r, carry):
        fn(pl.ds(pl.multiple_of(r * rc, rc), rc))
        return carry

    lax.fori_loop(0, tm // rc, step, 0)


def _rms_bwd(dy, xhat, r):
    return r * (dy - xhat * jnp.mean(dy * xhat, axis=-1, keepdims=True))


def _matmul(a, b, mode, out_dtype, tm, tn, tk, name, shard_major=False, ex=None):
    if mode == "nn":
        (M, K), N = a.shape, b.shape[1]
    elif mode == "nt":
        (M, K), N = a.shape, b.shape[0]
    else:
        (K, M), N = a.shape, b.shape[1]
    tm, tn, tk = min(tm, M), min(tn, N), min(tk, K)
    assert M % tm == 0 and N % tn == 0 and K % tk == 0, (name, M, N, K, tm, tn, tk)
    nk = K // tk
    dims = {"nn": NN, "nt": NT, "tn": TN}[mode]
    a_spec = pl.BlockSpec((tk, tm), lambda i, j, k: (k, i)) if mode == "tn" else pl.BlockSpec((tm, tk), lambda i, j, k: (i, k))
    b_spec = pl.BlockSpec((tn, tk), lambda i, j, k: (j, k)) if mode == "nt" else pl.BlockSpec((tk, tn), lambda i, j, k: (k, j))
    if shard_major:
        assert tn * N_CHIPS == N
        out_shape = jax.ShapeDtypeStruct((N_CHIPS, M, tn), out_dtype)
        o_spec = pl.BlockSpec((None, tm, tn), lambda i, j, k: (j, i, 0))
    else:
        out_shape = jax.ShapeDtypeStruct((M, N), out_dtype)
        o_spec = pl.BlockSpec((tm, tn), lambda i, j, k: (i, j))

    direct = nk == 1 or out_dtype == F32

    def body(a_ref, b_ref, o_ref, *scratch):
        k = pl.program_id(2)
        if nk == 1:
            o_ref[...] = _dot(a_ref[...], b_ref[...], dims).astype(o_ref.dtype)
            return
        acc_ref = o_ref if direct else scratch[0]

        @pl.when(k == 0)
        def _():
            acc_ref[...] = jnp.zeros(acc_ref.shape, F32)

        acc_ref[...] += _dot(a_ref[...], b_ref[...], dims)
        if not direct:
            @pl.when(k == nk - 1)
            def _():
                o_ref[...] = acc_ref[...].astype(o_ref.dtype)

    outs = _pallas(body, name=name, grid=(M // tm, N // tn, nk), in_specs=[a_spec, b_spec], out_specs=[o_spec],
                   out_shape=[out_shape], scratch_shapes=[] if direct else [pltpu.VMEM((tm, tn), F32)],
                   semantics=("parallel", "parallel", "arbitrary"), args=[a, b], ex=ex)
    return outs[0] if ex is None else outs


def _mod_partial(c_all, w_ada, b_sh, name):
    R, D = c_all.shape
    N = w_ada.shape[1]
    tn = _tile(N, 512)

    def body(c_ref, w_ref, b_ref, o_ref, ca_ref):
        cv = c_ref[...]
        ca = (cv * _sigmoid(cv)).astype(BF16)
        ca_ref[...] = ca
        o_ref[...] = _dot(ca, w_ref[...].astype(BF16)) + b_ref[...]

    return pl.pallas_call(
        body,
        name=name,
        grid=(N // tn,),
        in_specs=[
            pl.BlockSpec((R, D), lambda j: (0, 0)),
            pl.BlockSpec((D, tn), lambda j: (0, j)),
            pl.BlockSpec((1, tn), lambda j: (0, j)),
        ],
        out_specs=[pl.BlockSpec((R, tn), lambda j: (0, j)), pl.BlockSpec((R, D), lambda j: (0, 0))],
        out_shape=[jax.ShapeDtypeStruct((R, N), F32), jax.ShapeDtypeStruct((R, D), BF16)],
        compiler_params=_params(("arbitrary",)),
    )(c_all, w_ada, b_sh)


FFN_BLOCK = 1024


def _ffn_blocks(F):
    if F % FFN_BLOCK == 0 or F < FFN_BLOCK:
        tf = _tile(F, FFN_BLOCK)
        return tf, F // tf, tf
    nj = -(-F // FFN_BLOCK)
    tail = F - (nj - 1) * FFN_BLOCK
    assert tail % LANES == 0
    return FFN_BLOCK, nj, tail
def _ffn_fwd(x, g, sh, sc, gt, w1, w3, w2, tgt, name, ex=None):
    S, D = x.shape
    F = w1.shape[1]
    tm, tf, nj, tail = _tile(S, 512), *_ffn_blocks(F)
    ni = S // tm
    with_loss = tgt is not None

    def body(*refs):
        if with_loss:
            (x_ref, g_ref, sh_ref, sc_ref, gt_ref, w1_ref, w3_ref, w2_ref, tgt_ref,
             gout_ref, df_ref, h_ref, a_ref, b_ref, dgt_ref, loss_ref, acc_ref) = refs
        else:
            (x_ref, g_ref, sh_ref, sc_ref, gt_ref, w1_ref, w3_ref, w2_ref,
             xo_ref, h_ref, a_ref, b_ref, f_ref, acc_ref) = refs
        i, j = pl.program_id(0), pl.program_id(1)

        @pl.when(j == 0)
        def _():
            def prologue(rows):
                xhat, _ = _rms(x_ref[rows, :])
                h_ref[rows, :] = ((xhat * g_ref[...]) * (1.0 + sc_ref[...]) + sh_ref[...]).astype(BF16)

            _for_rows(tm, prologue)

        @pl.when(j == 0)
        def _():
            acc_ref[...] = jnp.zeros(acc_ref.shape, F32)

        def columns(width):
            def run():
                hb = h_ref[...]
                av = _dot(hb, w1_ref[:, :width])
                bv = _dot(hb, w3_ref[:, :width])
                a_ref[:, :width] = av.astype(BF16)
                b_ref[:, :width] = bv.astype(BF16)
                sv = ((av * _sigmoid(av)) * bv).astype(BF16)
                acc_ref[...] += _dot(sv, w2_ref[:width, :])
            return run

        if tail == tf:
            columns(tf)()
        else:
            pl.when(j < nj - 1)(columns(tf))
            pl.when(j == nj - 1)(columns(tail))

        @pl.when(j == nj - 1)
        def _():
            if with_loss:
                @pl.when(i == 0)
                def _():
                    dgt_ref[...] = jnp.zeros(dgt_ref.shape, F32)
                    loss_ref[...] = jnp.zeros(loss_ref.shape, F32)

            def epilogue(rows):
                fv = acc_ref[rows, :]
                half_gate = 0.5 * gt_ref[...]
                xo = x_ref[rows, :] + half_gate * fv
                if not with_loss:
                    xo_ref[rows, :] = xo
                    f_ref[rows, :] = fv.astype(f_ref.dtype)
                    return
                err = xo - tgt_ref[rows, :]
                gout = err * (1.0 / D)
                gout_ref[rows, :] = gout
                df_ref[rows, :] = (half_gate * gout).astype(BF16)
                dgt_ref[...] += 0.5 * jnp.sum(gout * fv, axis=0, keepdims=True)
                loss_part = jnp.sum(jnp.sum(err * err, axis=1, keepdims=True), axis=0, keepdims=True)
                loss_ref[...] += jnp.broadcast_to(loss_part, loss_ref.shape)

            _for_rows(tm, epilogue)

    row = pl.BlockSpec((tm, D), lambda i, j: (i, 0))
    row_in = pl.BlockSpec((tm, D), lambda i, j: (i, 0), pipeline_mode=pl.Buffered(1))
    vec = pl.BlockSpec((1, D), lambda i, j: (0, 0))
    col = pl.BlockSpec((tm, tf), lambda i, j: (i, j))
    in_specs = [row_in, vec, vec, vec, vec,
                pl.BlockSpec((D, tf), lambda i, j: (0, j)),
                pl.BlockSpec((D, tf), lambda i, j: (0, j)),
                pl.BlockSpec((tf, D), lambda i, j: (j, 0))]
    args = [x, g, sh, sc, gt, w1, w3, w2]
    act = jax.ShapeDtypeStruct((S, F), BF16)
    if with_loss:
        in_specs.append(row_in)
        args.append(tgt)
        out_specs = [row, row, row_in, col, col, vec, pl.BlockSpec((1, LANES), lambda i, j: (0, 0))]
        out_shape = [jax.ShapeDtypeStruct((S, D), F32), jax.ShapeDtypeStruct((S, D), BF16),
                     jax.ShapeDtypeStruct((S, D), BF16), act, act,
                     jax.ShapeDtypeStruct((1, D), F32), jax.ShapeDtypeStruct((1, LANES), F32)]
    else:
        out_specs = [row, row, col, col, row]
        out_shape = [jax.ShapeDtypeStruct((S, D), F32), jax.ShapeDtypeStruct((S, D), BF16), act, act,
                     jax.ShapeDtypeStruct((S, D), BF16)]
    return _pallas(body, name=name, grid=(ni, nj), in_specs=in_specs, out_specs=out_specs, out_shape=out_shape,
                   scratch_shapes=[pltpu.VMEM((tm, D), F32)],
                   semantics=("arbitrary", "arbitrary"), args=args, ex=ex)


def _ffn_bwd(df, a, b, w1, w3, w2, name, ex=None):
    S, D = df.shape
    F = a.shape[1]
    tm, tf, nj, tail = _tile(S, 512), *_ffn_blocks(F)

    def body(df_ref, a_ref, b_ref, w1_ref, w3_ref, w2_ref, da_ref, db_ref, s_ref, dh_ref):
        j = pl.program_id(1)

        @pl.when(j == 0)
        def _():
            dh_ref[...] = jnp.zeros(dh_ref.shape, F32)

        def columns(width):
            def run():
                ds = _dot(df_ref[...], w2_ref[:width, :], NT)
                av = a_ref[:, :width].astype(F32)
                bv = b_ref[:, :width].astype(F32)
                sig = _sigmoid(av)
                sil = av * sig
                da = ((ds * bv) * (sig * (1.0 + av * (1.0 - sig)))).astype(BF16)
                db = (ds * sil).astype(BF16)
                da_ref[:, :width] = da
                db_ref[:, :width] = db
                s_ref[:, :width] = (sil * bv).astype(BF16)
                dh_ref[...] += _dot(da, w1_ref[:, :width], NT) + _dot(db, w3_ref[:, :width], NT)
            return run

        if tail == tf:
            columns(tf)()
        else:
            pl.when(j < nj - 1)(columns(tf))
            pl.when(j == nj - 1)(columns(tail))

    row = pl.BlockSpec((tm, D), lambda i, j: (i, 0))
    col = pl.BlockSpec((tm, tf), lambda i, j: (i, j))
    act = jax.ShapeDtypeStruct((S, F), BF16)
    return _pallas(body, name=name, grid=(S // tm, nj),
                   in_specs=[row, col, col,
                             pl.BlockSpec((D, tf), lambda i, j: (0, j)),
                             pl.BlockSpec((D, tf), lambda i, j: (0, j)),
                             pl.BlockSpec((tf, D), lambda i, j: (j, 0))],
                   out_specs=[col, col, col, row],
                   out_shape=[act, act, act, jax.ShapeDtypeStruct((S, D), F32)],
                   semantics=("parallel", "arbitrary"), args=[df, a, b, w1, w3, w2], ex=ex)


def _norm_mod(x, g, sh, sc, name, ex=None):
    S, D = x.shape
    tm = _tile(S, 512)

    def body(x_ref, g_ref, sh_ref, sc_ref, h_ref):
        def step(rows):
            xhat, _ = _rms(x_ref[rows, :])
            h_ref[rows, :] = ((xhat * g_ref[...]) * (1.0 + sc_ref[...]) + sh_ref[...]).astype(BF16)

        _for_rows(tm, step)

    row = pl.BlockSpec((tm, D), lambda i: (i, 0))
    vec = pl.BlockSpec((1, D), lambda i: (0, 0))
    outs = _pallas(body, name=name, grid=(S // tm,), in_specs=[row, vec, vec, vec], out_specs=[row],
                   out_shape=[jax.ShapeDtypeStruct((S, D), BF16)], semantics=("parallel",), args=[x, g, sh, sc], ex=ex)
    return outs[0] if ex is None else outs


def _norm_bwd(dh, x, gres, g, sc, prev, name, ex=None):
    S, D = x.shape
    tm = _tile(S, 256)
    has_prev = prev is not None
    coef = prev[2] if has_prev else None

    def body(*refs):
        if has_prev:
            (dh_ref, x_ref, gr_ref, g_ref, sc_ref, f_ref, gt_ref,
             go_ref, dsh_ref, dsc_ref, dg_ref, dp_ref, dgt_ref) = refs
        else:
            dh_ref, x_ref, gr_ref, g_ref, sc_ref, go_ref, dsh_ref, dsc_ref, dg_ref = refs
        sum_refs = [dsh_ref, dsc_ref, dg_ref] + ([dgt_ref] if has_prev else [])

        @pl.when(pl.program_id(0) == 0)
        def _():
            for ref in sum_refs:
                ref[...] = jnp.zeros(ref.shape, F32)

        def step(rows):
            dh = dh_ref[rows, :]
            xhat, r = _rms(x_ref[rows, :])
            gain = g_ref[...]
            scale1 = 1.0 + sc_ref[...]
            gout = gr_ref[rows, :] + _rms_bwd(dh * scale1 * gain, xhat, r)
            go_ref[rows, :] = gout
            sums = [dh, dh * (xhat * gain), dh * scale1 * xhat]
            if has_prev:
                dp_ref[rows, :] = ((coef * gt_ref[...]) * gout).astype(BF16)
                sums.append(coef * (gout * f_ref[rows, :].astype(F32)))
            for ref, v in zip(sum_refs, sums):
                ref[...] += jnp.sum(v, axis=0, keepdims=True)

        _for_rows(tm, step)

    row = pl.BlockSpec((tm, D), lambda i: (i, 0))
    vec = pl.BlockSpec((1, D), lambda i: (0, 0))
    vshape = jax.ShapeDtypeStruct((1, D), F32)
    in_specs = [row, row, row, vec, vec]
    args = [dh, x, gres, g, sc]
    out_specs = [row, vec, vec, vec]
    out_shape = [jax.ShapeDtypeStruct((S, D), F32), vshape, vshape, vshape]
    if has_prev:
        in_specs += [row, vec]
        args += [prev[0], prev[1]]
        out_specs += [row, vec]
        out_shape += [jax.ShapeDtypeStruct((S, D), BF16), vshape]
    return _pallas(body, name=name, grid=(S // tm,), in_specs=in_specs, out_specs=out_specs, out_shape=out_shape,
                   semantics=("arbitrary",), args=args, ex=ex)


PAIRS = B_HEADS // 2
PAIR_ROWS = (PAIRS // KV_HEADS) * BLOCK
BAND = 2 * BLOCK


def _stack(ref, offset, count):
    return jnp.concatenate([ref[:, offset + p * LANES:offset + (p + 1) * LANES] for p in range(count)], axis=0)


def _seg_mean(x, e_ref):
    return _dot(x.astype(BF16), e_ref[...]) * (1.0 / HEAD_DIM)


def _block_diag(x, x_rolled, left, kv_head):
    if kv_head == 0:
        top, bottom = jnp.where(left, x, 0.0), jnp.where(left, 0.0, x_rolled)
    else:
        top, bottom = jnp.where(left, x_rolled, 0.0), jnp.where(left, 0.0, x)
    return jnp.concatenate([top, bottom], axis=0).astype(BF16)


def _from_block_diag(g, left, kv_head):
    a, b = g[:BAND], g[BAND:]
    if kv_head == 0:
        return jnp.where(left, a + pltpu.roll(b, HEAD_DIM, 1), 0.0)
    return jnp.where(left, 0.0, pltpu.roll(a, HEAD_DIM, 1) + b)


def _pair_softmax(st, sk_ref, kv_head):
    out = []
    for e in range(2):
        seg = st[e * BAND:(e + 1) * BAND]
        sink = jnp.concatenate([jnp.full((1, BLOCK), sk_ref[kv_head * GROUP + 2 * p + e], F32)
                                for p in range(PAIRS // KV_HEADS)], axis=1)
        m = jnp.maximum(jnp.max(seg, axis=0, keepdims=True), sink)
        p_ = jnp.exp(seg - m)
        e_sink = jnp.exp(sink - m)
        inv = 1.0 / (jnp.sum(p_, axis=0, keepdims=True) + e_sink)
        out.append((p_ * inv, e_sink * inv))
    return out


def _lane_mean(x, ones_ref):
    return _dot(x.astype(BF16), ones_ref[...])


def _mixer_specs(nb, last):
    full = lambda shape: pl.BlockSpec(shape, lambda n: (0,) * len(shape))
    z_spec = pl.BlockSpec((BLOCK, IN_COLS), lambda n: (jnp.minimum(n, last), 0))
    zp_spec = pl.BlockSpec((BLOCK, 2 * KV_WIDTH), lambda n: (jnp.clip(n - 1, 0, last), K_OFF // (2 * KV_WIDTH)))
    consts = [full((A_HEADS * BLOCK, A_DIM)), full((A_HEADS * BLOCK, A_DIM)), full((1, LANES)), full((1, LANES)),
              full((LANES, LANES)), full((LANES, LANES)), pl.BlockSpec(memory_space=pltpu.SMEM),
              pl.BlockSpec((None, KV_HEADS, PAIR_ROWS, 2 * BAND), lambda n: (jnp.minimum(n, 1), 0, 0, 0))]
    return full, z_spec, zp_spec, consts


def _mixer_fwd(z, wm, sbp, gvp, gq2, gk2, seg_ones, lane_ones, sinks, biasp, name):
    S = z.shape[0]
    nb = S // BLOCK

    def body(z_ref, zp_ref, wm_ref, sbp_ref, gvp_ref, gq2_ref, gk2_ref, e_ref, l_ref, sk_ref, bias_ref, mix_ref):
        u = _gelu(_stack(z_ref, 0, A_HEADS))
        v = _gelu(_stack(z_ref, A_WIDTH, A_HEADS))
        vhat = v * lax.rsqrt(_lane_mean(v * v, l_ref) + EPS)
        vn = (vhat * gvp_ref[...]).astype(BF16)
        mixed = jnp.concatenate([_dot(wm_ref[h], vn[h * BLOCK:(h + 1) * BLOCK]) for h in range(A_HEADS)], axis=0)
        ya = (u * (mixed + sbp_ref[...])).astype(BF16)
        for h in range(A_HEADS):
            mix_ref[:, h * A_DIM:(h + 1) * A_DIM] = ya[h * BLOCK:(h + 1) * BLOCK]

        left = lax.broadcasted_iota(jnp.int32, (1, LANES), 1) < HEAD_DIM
        kv = jnp.concatenate([zp_ref[...], z_ref[:, K_OFF:K_OFF + 2 * KV_WIDTH]], axis=0)
        k2, v2 = kv[:, :KV_WIDTH], kv[:, KV_WIDTH:]
        kn2 = k2 * lax.rsqrt(_seg_mean(k2 * k2, e_ref) + EPS) * gk2_ref[...]
        kn2_r, v2_r = pltpu.roll(kn2, HEAD_DIM, 1), pltpu.roll(v2, HEAD_DIM, 1)
        qp = _stack(z_ref, Q_OFF, PAIRS)
        qn = (qp * lax.rsqrt(_seg_mean(qp * qp, e_ref) + EPS) * gq2_ref[...]).astype(BF16)
        for kh in range(KV_HEADS):
            kbd, vbd = _block_diag(kn2, kn2_r, left, kh), _block_diag(v2, v2_r, left, kh)
            st = _dot(kbd, qn[kh * PAIR_ROWS:(kh + 1) * PAIR_ROWS], NT) * (HEAD_DIM ** -0.5) + bias_ref[kh]
            wt = jnp.concatenate([w_e for w_e, _ in _pair_softmax(st, sk_ref, kh)], axis=0).astype(BF16)
            o = _dot(wt, vbd, TN).astype(BF16)
            for p in range(PAIRS // KV_HEADS):
                col = A_WIDTH + (kh * (PAIRS // KV_HEADS) + p) * LANES
                mix_ref[:, col:col + LANES] = o[p * BLOCK:(p + 1) * BLOCK]

    full, z_spec, zp_spec, consts = _mixer_specs(nb, nb - 1)
    return pl.pallas_call(
        body,
        name=name,
        grid=(nb,),
        in_specs=[z_spec, zp_spec, full((A_HEADS, BLOCK, BLOCK))] + consts,
        out_specs=pl.BlockSpec((BLOCK, A_WIDTH + B_WIDTH), lambda n: (n, 0)),
        out_shape=jax.ShapeDtypeStruct((S, A_WIDTH + B_WIDTH), BF16),
        compiler_params=_params(("parallel",)),
    )(z, z, wm, sbp, gvp, gq2, gk2, seg_ones, lane_ones, sinks, biasp)


def _mixer_bwd(z, dmix, wm, wm_t, sbp, gvp, gq2, gk2, seg_ones, lane_ones, sinks, biasp, pair_fold, name, ex=None):
    S = z.shape[0]
    nb = S // BLOCK

    def body(z_ref, zp_ref, dmix_ref, wm_ref, wmt_ref, sbp_ref, gvp_ref, gq2_ref, gk2_ref, e_ref, l_ref, sk_ref,
             bias_ref, fold_ref,
             dz_ref, dzkv_ref, dwm_ref, dsb_ref, dgv_ref, dgq_ref, dgk_ref, dsk_ref, dst_ref,
             carry_ref, tot_ref, sbacc_ref, skacc_ref, gqacc_ref, gkacc_ref):
        n = pl.program_id(0)
        left = lax.broadcasted_iota(jnp.int32, (1, LANES), 1) < HEAD_DIM

        @pl.when(n == 0)
        def _():
            for ref in (dwm_ref, dgv_ref, dst_ref, carry_ref, sbacc_ref, skacc_ref, gqacc_ref, gkacc_ref):
                ref[...] = jnp.zeros(ref.shape, ref.dtype)

        @pl.when(n < nb)
        def _():
            u, du_dz = _gelu_and_grad(_stack(z_ref, 0, A_HEADS))
            v, dv_dz = _gelu_and_grad(_stack(z_ref, A_WIDTH, A_HEADS))
            rv = lax.rsqrt(_lane_mean(v * v, l_ref) + EPS)
            vhat = v * rv
            gvp = gvp_ref[...]
            vn = (vhat * gvp).astype(BF16)
            rows = lambda a, h: a[h * BLOCK:(h + 1) * BLOCK]
            mixed = jnp.concatenate([_dot(wm_ref[h], rows(vn, h)) for h in range(A_HEADS)], axis=0) + sbp_ref[...]
            dya = _stack(dmix_ref, 0, A_HEADS).astype(F32)
            dmx = dya * u
            sbacc_ref[...] += dmx
            dmx_b = dmx.astype(BF16)
            for h in range(A_HEADS):
                dwm_ref[h] += _dot(rows(dmx_b, h), rows(vn, h), NT)
            dvn = jnp.concatenate([_dot(wmt_ref[h], rows(dmx_b, h)) for h in range(A_HEADS)], axis=0)
            dgv_ref[...] += jnp.sum(jnp.reshape(dvn * vhat, (A_HEADS, BLOCK, A_DIM)), axis=1)
            dzu = ((dya * mixed) * du_dz).astype(BF16)
            tv = dvn * gvp
            dzv = ((rv * (tv - vhat * _lane_mean(tv * vhat, l_ref))) * dv_dz).astype(BF16)
            for h in range(A_HEADS):
                dz_ref[:, h * A_DIM:(h + 1) * A_DIM] = rows(dzu, h)
                dz_ref[:, A_WIDTH + h * A_DIM:A_WIDTH + (h + 1) * A_DIM] = rows(dzv, h)

            kv = jnp.concatenate([zp_ref[...], z_ref[:, K_OFF:K_OFF + 2 * KV_WIDTH]], axis=0)
            k2, v2 = kv[:, :KV_WIDTH], kv[:, KV_WIDTH:]
            kn2 = k2 * lax.rsqrt(_seg_mean(k2 * k2, e_ref) + EPS) * gk2_ref[...]
            kn2_r, v2_r = pltpu.roll(kn2, HEAD_DIM, 1), pltpu.roll(v2, HEAD_DIM, 1)
            qp = _stack(z_ref, Q_OFF, PAIRS)
            rq = lax.rsqrt(_seg_mean(qp * qp, e_ref) + EPS)
            qhat = qp * rq
            gq2 = gq2_ref[...]
            qn = (qhat * gq2).astype(BF16)
            dop = _stack(dmix_ref, A_WIDTH, PAIRS)
            dqn_parts = []
            dk2n = jnp.zeros((BAND, KV_WIDTH), F32)
            dv2 = jnp.zeros((BAND, KV_WIDTH), F32)
            for kh in range(KV_HEADS):
                kbd, vbd = _block_diag(kn2, kn2_r, left, kh), _block_diag(v2, v2_r, left, kh)
                qg = qn[kh * PAIR_ROWS:(kh + 1) * PAIR_ROWS]
                dog = dop[kh * PAIR_ROWS:(kh + 1) * PAIR_ROWS]
                st = _dot(kbd, qg, NT) * (HEAD_DIM ** -0.5) + bias_ref[kh]
                halves = _pair_softmax(st, sk_ref, kh)
                dpt = _dot(vbd, dog, NT)
                ds_halves, t_halves = [], []
                for e, (w_e, w_sink) in enumerate(halves):
                    dp_e = dpt[e * BAND:(e + 1) * BAND]
                    delta = jnp.sum(w_e * dp_e, axis=0, keepdims=True)
                    ds_halves.append(w_e * (dp_e - delta))
                    t_halves.append(-(w_sink * delta))
                dst = jnp.concatenate(ds_halves, axis=0)
                dst_ref[kh] += dst
                skacc_ref[2 * kh:2 * kh + 2, :] += jnp.concatenate(t_halves, axis=0)
                ds_b = (dst * (HEAD_DIM ** -0.5)).astype(BF16)
                w_b = jnp.concatenate([w_e for w_e, _ in halves], axis=0).astype(BF16)
                dqn_parts.append(_dot(ds_b, kbd, TN))
                dk2n += _from_block_diag(_dot(ds_b, qg), left, kh)
                dv2 += _from_block_diag(_dot(w_b, dog), left, kh)
            dqn = jnp.concatenate(dqn_parts, axis=0)
            gqacc_ref[...] += jnp.sum(dqn * qhat, axis=0, keepdims=True)
            t = dqn * gq2
            dzq = (rq * (t - qhat * _seg_mean(t * qhat, e_ref))).astype(BF16)
            for p in range(PAIRS):
                dz_ref[:, Q_OFF + p * LANES:Q_OFF + (p + 1) * LANES] = rows(dzq, p)
            tot_ref[0] = carry_ref[0] + dk2n[:BLOCK]
            tot_ref[1] = carry_ref[1] + dv2[:BLOCK]
            carry_ref[0] = dk2n[BLOCK:]
            carry_ref[1] = dv2[BLOCK:]

        @pl.when(n == nb)
        def _():
            tot_ref[...] = carry_ref[...]

        kp = zp_ref[:, :KV_WIDTH]
        rk = lax.rsqrt(_seg_mean(kp * kp, e_ref) + EPS)
        khat = kp * rk
        dkn = tot_ref[0]
        gkacc_ref[...] += jnp.sum(dkn * khat, axis=0, keepdims=True)
        t = dkn * gk2_ref[...]
        dzkv_ref[:, :KV_WIDTH] = (rk * (t - khat * _seg_mean(t * khat, e_ref))).astype(BF16)
        dzkv_ref[:, KV_WIDTH:] = tot_ref[1].astype(BF16)

        @pl.when(n == nb)
        def _():
            dsb_ref[...] = jnp.broadcast_to(jnp.sum(sbacc_ref[...], axis=1, keepdims=True), dsb_ref.shape)
            dsk_ref[...] = lax.dot_general(skacc_ref[...], fold_ref[...], NN, precision=lax.Precision.HIGHEST,
                                           preferred_element_type=F32)
            dgq_ref[...] = gqacc_ref[...] + pltpu.roll(gqacc_ref[...], HEAD_DIM, 1)
            dgk_ref[...] = gkacc_ref[...] + pltpu.roll(gkacc_ref[...], HEAD_DIM, 1)

    last = nb - 1
    full, z_spec, zp_spec, consts = _mixer_specs(nb, last)
    return _pallas(
        body,
        name=name,
        grid=(nb + 1,),
        ex=ex,
        in_specs=[z_spec, zp_spec, pl.BlockSpec((BLOCK, A_WIDTH + B_WIDTH), lambda n: (jnp.minimum(n, last), 0)),
                  full((A_HEADS, BLOCK, BLOCK)), full((A_HEADS, BLOCK, BLOCK))] + consts + [full((PAIR_ROWS, LANES))],
        out_specs=[
            pl.BlockSpec((BLOCK, K_OFF), lambda n: (jnp.minimum(n, last), 0)),
            pl.BlockSpec((BLOCK, 2 * KV_WIDTH), lambda n: (jnp.maximum(n - 1, 0), 0)),
            full((A_HEADS, BLOCK, BLOCK)), full((A_HEADS * BLOCK, A_DIM)), full((A_HEADS, A_DIM)),
            full((1, LANES)), full((1, LANES)), full((SUBLANES, LANES)),
            full((KV_HEADS, PAIR_ROWS, 2 * BAND)),
        ],
        out_shape=[
            jax.ShapeDtypeStruct((S, K_OFF), BF16),
            jax.ShapeDtypeStruct((S, 2 * KV_WIDTH), BF16),
            jax.ShapeDtypeStruct((A_HEADS, BLOCK, BLOCK), F32),
            jax.ShapeDtypeStruct((A_HEADS * BLOCK, A_DIM), F32),
            jax.ShapeDtypeStruct((A_HEADS, A_DIM), F32),
            jax.ShapeDtypeStruct((1, LANES), F32),
            jax.ShapeDtypeStruct((1, LANES), F32),
            jax.ShapeDtypeStruct((SUBLANES, LANES), F32),
            jax.ShapeDtypeStruct((KV_HEADS, PAIR_ROWS, 2 * BAND), F32),
        ],
        scratch_shapes=[
            pltpu.VMEM((2, BLOCK, KV_WIDTH), F32),
            pltpu.VMEM((2, BLOCK, KV_WIDTH), F32),
            pltpu.VMEM((A_HEADS * BLOCK, A_DIM), F32),
            pltpu.VMEM((SUBLANES, PAIR_ROWS), F32),
            pltpu.VMEM((1, LANES), F32),
            pltpu.VMEM((1, LANES), F32),
        ],
        semantics=("arbitrary",),
        args=[z, z, dmix, wm, wm_t, sbp, gvp, gq2, gk2, seg_ones, lane_ones, sinks, biasp, pair_fold],
    )


def _mixer_out(mix, w_out, x, gt, name):
    S, D = x.shape
    K = mix.shape[1]
    tm, tn = _tile(S, 1024), _tile(D, 1024)

    def body(m_ref, w_ref, x_ref, gt_ref, xo_ref, y_ref):
        y = _dot(m_ref[...], w_ref[...])
        y_ref[...] = y.astype(BF16)
        xo_ref[...] = x_ref[...] + gt_ref[...] * y

    blk = pl.BlockSpec((tm, tn), lambda j, i: (i, j))
    return pl.pallas_call(
        body,
        name=name,
        grid=(D // tn, S // tm),
        in_specs=[pl.BlockSpec((tm, K), lambda j, i: (i, 0)), pl.BlockSpec((K, tn), lambda j, i: (0, j)),
                  blk, pl.BlockSpec((1, tn), lambda j, i: (0, j))],
        out_specs=[blk, blk],
        out_shape=[jax.ShapeDtypeStruct((S, D), F32), jax.ShapeDtypeStruct((S, D), BF16)],
        compiler_params=_params(("parallel", "parallel")),
    )(mix, w_out, x, gt)


def _bucket_sum(dst, onehot, name):
    def body(d_ref, o_ref, out_ref):
        out_ref[...] = lax.dot_general(o_ref[...], d_ref[...], NT, precision=lax.Precision.HIGHEST,
                                       preferred_element_type=F32)

    return pl.pallas_call(
        body,
        name=name,
        out_shape=jax.ShapeDtypeStruct((N_BUCKETS, B_HEADS), F32),
    )(dst, onehot)


def _adamw_math(w, g, m, v):
    m = ADAM_B1 * m + (1.0 - ADAM_B1) * g
    v = ADAM_B2 * v + (1.0 - ADAM_B2) * (g * g)
    m_hat = m / (1.0 - ADAM_B1 ** ADAM_STEP)
    v_hat = v / (1.0 - ADAM_B2 ** ADAM_STEP)
    delta = -ADAM_LR * (m_hat / (jnp.sqrt(v_hat) + ADAM_EPS) + ADAM_WD * w)
    return delta, m, v


def _adamw(w, g, m, v, name, emit_grad=False, after=()):
    R, C = w.shape
    tr = _tile(R, max(SUBLANES, (1 << 19) // C), SUBLANES)

    def body(w_ref, g_ref, m_ref, v_ref, *out_refs):
        gv = g_ref[...]
        results = _adamw_math(w_ref[...], gv, m_ref[...], v_ref[...])
        for ref, val in zip(out_refs, ((gv,) if emit_grad else ()) + results):
            ref[...] = val

    blk = pl.BlockSpec((tr, C), lambda i: (i, 0))
    shape = jax.ShapeDtypeStruct((R, C), F32)
    n_out = 4 if emit_grad else 3
    return _pallas(body, name=name, grid=(R // tr,), in_specs=[blk] * 4, out_specs=[blk] * n_out,
                   out_shape=[shape] * n_out, semantics=("parallel",), args=[w, g, m, v], after=after)


def _small_update(parts, w, m, v, name):
    R = w.shape[0]

    def body(p_ref, w_ref, m_ref, v_ref, g_ref, d_ref, mo_ref, vo_ref):
        g = p_ref[0]
        for dev in range(1, N_DEV):
            g = g + p_ref[dev]
        g_ref[...] = g
        d, mn, vn = _adamw_math(w_ref[...], g, m_ref[...], v_ref[...])
        d_ref[...] = d
        mo_ref[...] = mn
        vo_ref[...] = vn

    shape = jax.ShapeDtypeStruct((R, LANES), F32)
    return pl.pallas_call(
        body,
        name=name,
        out_shape=[shape] * 4,
        compiler_params=pltpu.CompilerParams(vmem_limit_bytes=VMEM_LIMIT),
    )(parts, w, m, v)


def _place():
    x, y, c = lax.axis_index("x"), lax.axis_index("y"), lax.axis_index("c")
    chips = [(1 - x, y), (x, 1 - y), (1 - x, 1 - y)]
    return x, y, c, chips


def _remote(src, dst, send_sem, recv_sem, to):
    return pltpu.make_async_remote_copy(src_ref=src, dst_ref=dst, send_sem=send_sem, recv_sem=recv_sem,
                                        device_id=to, device_id_type=MESH)


def _allgather_small(block, name):
    m_per, n = block.shape

    def body(x_ref, out_ref, send_sems, recv_sems, local_sem):
        x, y, c, chips = _place()
        me, sibling = (x, y, c), (x, y, 1 - c)

        def rows(px, py, pc):
            return out_ref.at[pl.ds((4 * px + 2 * py + pc) * m_per, m_per), :]

        def copy(k, blk, to, src=None):
            return _remote(rows(*blk) if src is None else src, rows(*blk), send_sems.at[k], recv_sems.at[k], to)

        mine = pltpu.make_async_copy(x_ref, rows(*me), local_sem)
        mine.start()
        first = [copy(0, me, sibling, src=x_ref)]
        first += [copy(1 + j, me, (*chip, c), src=x_ref) for j, chip in enumerate(chips)]
        for cp in first:
            cp.start()
        passed = [copy(4 + j, (*chip, c), sibling) for j, chip in enumerate(chips)]
        for j, chip in enumerate(chips):
            copy(1 + j, (*chip, c), me).wait_recv()
            passed[j].start()
        copy(0, sibling, me).wait_recv()
        for j, chip in enumerate(chips):
            copy(4 + j, (*chip, 1 - c), me).wait_recv()
        for cp in first + passed:
            cp.wait_send()
        mine.wait()

    return pl.pallas_call(
        body,
        name=name,
        out_shape=jax.ShapeDtypeStruct((N_DEV * m_per, n), block.dtype),
        in_specs=[pl.BlockSpec(memory_space=pltpu.VMEM)],
        out_specs=pl.BlockSpec(memory_space=pltpu.VMEM),
        scratch_shapes=[pltpu.SemaphoreType.DMA((7,)), pltpu.SemaphoreType.DMA((7,)), pltpu.SemaphoreType.DMA],
        compiler_params=pltpu.CompilerParams(vmem_limit_bytes=VMEM_LIMIT),
    )(block)


def _half(ref, c, rows):
    start = pl.multiple_of(c * rows, BF16_ROWS)
    if len(ref.shape) == 2:
        return ref.at[pl.ds(start, rows), :]
    return ref.at[:, pl.ds(start, rows), :]


def _same(arrays):
    return [jax.ShapeDtypeStruct(a.shape, a.dtype) for a in arrays], {t: t for t in range(len(arrays))}


def _ex_gather_ici(bufs):
    def plan(ins, outs, send_sems, recv_sems):
        x, y, c, chips = _place()
        sends, arrivals = [], []
        for t, buf in enumerate(bufs):
            rows = buf.shape[1] // 2
            mine = _half(outs[t].at[2 * x + y], c, rows)
            for k, (px, py) in enumerate(chips):
                sems = (send_sems.at[3 * t + k], recv_sems.at[3 * t + k], (px, py, c))
                landed = _half(outs[t].at[2 * px + py], c, rows)
                sends.append((mine, mine, *sems))
                arrivals.append((landed, landed, *sems))
        return sends, arrivals

    shapes, aliases = _same(bufs)
    return _Exchange(bufs, shapes, aliases, 3 * len(bufs), plan)


def _ex_gather_d2d(bufs):
    def plan(ins, outs, send_sems, recv_sems):
        x, y, c, chips = _place()
        sends, arrivals = [], []
        for t, buf in enumerate(bufs):
            rows = buf.shape[1] // 2
            for k, (px, py) in enumerate(chips):
                sems = (send_sems.at[3 * t + k], recv_sems.at[3 * t + k], (x, y, 1 - c))
                landed = _half(outs[t].at[2 * px + py], c, rows)
                other = _half(outs[t].at[2 * px + py], 1 - c, rows)
                sends.append((landed, landed, *sems))
                arrivals.append((other, other, *sems))
        return sends, arrivals

    shapes, aliases = _same(bufs)
    return _Exchange(bufs, shapes, aliases, 3 * len(bufs), plan)


def _ex_swap_halves(grads):
    def plan(ins, outs, send_sems, recv_sems):
        x, y, c, _ = _place()
        sends = [(_half(ins[t], 1 - c, g.shape[1] // 2), outs[t], send_sems.at[t], recv_sems.at[t], (x, y, 1 - c))
                 for t, g in enumerate(grads)]
        return sends, sends

    shapes = [jax.ShapeDtypeStruct((g.shape[0], g.shape[1] // 2, g.shape[2]), g.dtype) for g in grads]
    return _Exchange(grads, shapes, {}, len(grads), plan)


def _ex_scatter(sums):
    def plan(ins, outs, send_sems, recv_sems):
        x, y, c, chips = _place()
        sends = [(ins[t].at[2 * px + py], outs[t].at[k], send_sems.at[3 * t + k], recv_sems.at[3 * t + k], (px, py, c))
                 for t in range(len(sums)) for k, (px, py) in enumerate(chips)]
        return sends, sends

    shapes = [jax.ShapeDtypeStruct((N_CHIPS - 1,) + s.shape[1:], s.dtype) for s in sums]
    return _Exchange(sums, shapes, {}, 3 * len(sums), plan)


def _ex_join_halves(fulls):
    def plan(ins, outs, send_sems, recv_sems):
        x, y, c, _ = _place()
        sends, arrivals = [], []
        for t, full in enumerate(fulls):
            rows = full.shape[0] // 2
            sems = (send_sems.at[t], recv_sems.at[t], (x, y, 1 - c))
            mine, other = _half(outs[t], c, rows), _half(outs[t], 1 - c, rows)
            sends.append((mine, mine, *sems))
            arrivals.append((other, other, *sems))
        return sends, arrivals

    shapes, aliases = _same(fulls)
    return _Exchange(fulls, shapes, aliases, len(fulls), plan)


class _Shifted:
    def __init__(self, sems, offset):
        self.sems, self.offset = sems, offset

    @property
    def at(self):
        return self

    def __getitem__(self, k):
        return self.sems.at[self.offset + k]


def _combine(exchanges):
    operands, out_shapes, aliases, starts = [], [], {}, []
    n_sems = 0
    for e in exchanges:
        starts.append((len(operands), len(out_shapes), n_sems))
        aliases.update({len(operands) + i: len(out_shapes) + o for i, o in e.aliases.items()})
        operands += list(e.operands)
        out_shapes += list(e.out_shapes)
        n_sems += e.n_sems

    def plan(ins, outs, send_sems, recv_sems):
        sends, arrivals = [], []
        for e, (i0, o0, s0) in zip(exchanges, starts):
            s, a = e.plan(ins[i0:i0 + len(e.operands)], outs[o0:o0 + len(e.out_shapes)],
                          _Shifted(send_sems, s0), _Shifted(recv_sems, s0))
            sends += s
            arrivals += a
        return sends, arrivals

    return _Exchange(operands, out_shapes, aliases, n_sems, plan)


class _Reduction:
    def __init__(self, grad, tag, c_arr, jc_arr):
        self.grad, self.tag, self.c_arr, self.jc_arr, self.stage = grad, tag, c_arr, jc_arr, 0

    def exchange(self):
        if self.stage == 0:
            return _ex_swap_halves([self.grad])
        if self.stage == 1:
            return _ex_scatter([self.sums])
        return _ex_join_halves([self.full])

    def advance(self, landed):
        if self.stage == 0:
            self.recv = landed
            self.sums = _chip_sum(self.grad, landed, self.c_arr, f"chip_sum_{self.tag}")
        elif self.stage == 1:
            self.full = _owner_sum(self.grad, self.recv, landed, self.jc_arr, f"owner_sum_{self.tag}")
        else:
            self.result = landed
        self.stage += 1


def _ride(reductions):
    def done(carried):
        for r, landed in zip(reductions, carried):
            r.advance(landed)

    return _combine([r.exchange() for r in reductions]), done


def _exchange_alone(ex, name):
    return _pallas(None, name=name, grid=(), in_specs=[], out_specs=[], out_shape=[], args=[], ex=ex)


SEM = pl.BlockSpec(memory_space=pltpu.SEMAPHORE)
DATAFLOW = pltpu.SideEffectType.DATAFLOW_SIDE_EFFECTING


def _exchange_start(ex, name):
    e_in, e_out = len(ex.operands), len(ex.out_shapes)
    kept = [i for i in range(e_in) if i not in ex.aliases]

    def body(*refs):
        ins, refs = refs[:e_in], refs[e_in:]
        outs, refs = refs[:e_out], refs[e_out:]
        _, (send_sems, recv_sems, token) = refs[:len(kept)], refs[len(kept):]
        sends, _ = ex.plan(ins, outs, send_sems, recv_sems)
        for cp in sends:
            _remote(*cp).start()
        token[...] = jnp.zeros(token.shape, F32)

    sems = pltpu.SemaphoreType.DMA((ex.n_sems,))
    aliases = dict(ex.aliases)
    aliases.update({i: e_out + k for k, i in enumerate(kept)})
    res = pl.pallas_call(
        body,
        name=name,
        in_specs=[ANY] * e_in,
        out_specs=[ANY] * (e_out + len(kept)) + [SEM, SEM, pl.BlockSpec(memory_space=pltpu.VMEM)],
        out_shape=list(ex.out_shapes) + [jax.ShapeDtypeStruct(ex.operands[i].shape, ex.operands[i].dtype) for i in kept]
        + [sems, sems, jax.ShapeDtypeStruct((SUBLANES, LANES), F32)],
        input_output_aliases=aliases,
        compiler_params=pltpu.CompilerParams(has_side_effects=DATAFLOW),
    )(*ex.operands)
    outs, kept_thru, (send_sems, recv_sems, token) = res[:e_out], res[e_out:e_out + len(kept)], res[e_out + len(kept):]
    operands = list(ex.operands)
    for i, o in ex.aliases.items():
        operands[i] = outs[o]
    for k, i in enumerate(kept):
        operands[i] = kept_thru[k]
    return (operands, outs, send_sems, recv_sems), token


def _exchange_wait(ex, state, after, name):
    operands, outs, send_sems, recv_sems = state
    e_out = len(outs)
    kept = [i for i in range(len(operands)) if i not in ex.aliases]

    def body(*refs):
        sources, refs = refs[:len(kept)], refs[len(kept):]
        landing, refs = refs[:e_out], refs[e_out:]
        ins = [None] * len(operands)
        for k, i in enumerate(kept):
            ins[i] = sources[k]
        sends, arrivals = ex.plan(ins, landing, refs[0], refs[1])
        for cp in arrivals:
            _remote(*cp).wait_recv()
        for cp in sends:
            _remote(*cp).wait_send()

    return pl.pallas_call(
        body,
        name=name,
        in_specs=[ANY] * (len(kept) + e_out) + [SEM, SEM] + [ANY] * len(after),
        out_specs=[ANY] * e_out,
        out_shape=[jax.ShapeDtypeStruct(o.shape, o.dtype) for o in outs],
        input_output_aliases={len(kept) + o: o for o in range(e_out)},
        compiler_params=pltpu.CompilerParams(has_side_effects=DATAFLOW),
    )(*[operands[i] for i in kept], *outs, send_sems, recv_sems, *after)


def _cast_to_slot(w, chip_arr, name):
    A, B = w.shape
    ta = _tile(A, max(BF16_ROWS, (1 << 19) // B), BF16_ROWS)

    def body(j_ref, w_ref, o_ref):
        o_ref[...] = w_ref[...].astype(BF16)

    return pl.pallas_call(
        body,
        name=name,
        grid_spec=pltpu.PrefetchScalarGridSpec(
            num_scalar_prefetch=1,
            grid=(A // ta,),
            in_specs=[pl.BlockSpec((ta, B), lambda i, j_ref: (i, 0))],
            out_specs=pl.BlockSpec((None, ta, B), lambda i, j_ref: (j_ref[0], i, 0)),
        ),
        out_shape=jax.ShapeDtypeStruct((N_CHIPS, A, B), BF16),
        compiler_params=_params(("parallel",)),
    )(chip_arr, w)


def _chip_sum(grad, recv, c_arr, name):
    _, A, B = grad.shape
    hA = A // 2
    ta = _tile(hA, max(BF16_ROWS, (1 << 19) // B), BF16_ROWS)
    nh = hA // ta

    def body(c_ref, g_ref, r_ref, o_ref):
        o_ref[...] = (g_ref[...] + r_ref[...]).astype(BF16)

    return pl.pallas_call(
        body,
        name=name,
        grid_spec=pltpu.PrefetchScalarGridSpec(
            num_scalar_prefetch=1,
            grid=(N_CHIPS, nh),
            in_specs=[pl.BlockSpec((None, ta, B), lambda s, i, c_ref: (s, c_ref[0] * nh + i, 0)),
                      pl.BlockSpec((None, ta, B), lambda s, i, c_ref: (s, i, 0))],
            out_specs=pl.BlockSpec((None, ta, B), lambda s, i, c_ref: (s, i, 0)),
        ),
        out_shape=jax.ShapeDtypeStruct((N_CHIPS, hA, B), BF16),
        compiler_params=_params(("parallel", "parallel")),
    )(c_arr, grad, recv)


def _owner_sum(grad, recv, landed, jc_arr, name):
    _, A, B = grad.shape
    hA = A // 2
    ta = _tile(hA, max(BF16_ROWS, (1 << 19) // B), BF16_ROWS)
    nh = hA // ta

    def body(jc_ref, g_ref, r_ref, l0_ref, l1_ref, l2_ref, o_ref):
        total = g_ref[...] + r_ref[...]
        for ref in (l0_ref, l1_ref, l2_ref):
            total = total + ref[...].astype(F32)
        o_ref[...] = total

    def landed_spec(k):
        return pl.BlockSpec((None, ta, B), lambda i, jc_ref: (k, i, 0))

    return pl.pallas_call(
        body,
        name=name,
        grid_spec=pltpu.PrefetchScalarGridSpec(
            num_scalar_prefetch=1,
            grid=(nh,),
            in_specs=[pl.BlockSpec((None, ta, B), lambda i, jc_ref: (jc_ref[0], jc_ref[1] * nh + i, 0)),
                      pl.BlockSpec((None, ta, B), lambda i, jc_ref: (jc_ref[0], i, 0)),
                      landed_spec(0), landed_spec(1), landed_spec(2)],
            out_specs=pl.BlockSpec((ta, B), lambda i, jc_ref: (jc_ref[1] * nh + i, 0)),
        ),
        out_shape=jax.ShapeDtypeStruct((A, B), F32),
        compiler_params=_params(("parallel",)),
    )(jc_arr, grad, recv, landed, landed, landed)


def _pack(parts):
    rows = []
    for p in parts:
        flat = jnp.reshape(p.astype(F32), (-1,))
        tile = SUBLANES * LANES
        padded = -(-flat.shape[0] // tile) * tile
        rows.append(jnp.reshape(jnp.pad(flat, (0, padded - flat.shape[0])), (-1, LANES)))
    return jnp.concatenate(rows, axis=0)


def _unpack(pack, shapes):
    out, row = [], 0
    for shape in shapes:
        size = int(np.prod(shape))
        nrows = -(-size // (SUBLANES * LANES)) * SUBLANES
        out.append(jnp.reshape(jnp.reshape(pack[row:row + nrows], (-1,))[:size], shape))
        row += nrows
    return out


def _bias_tables():
    qi = np.arange(BLOCK)[:, None]
    kj = np.arange(2 * BLOCK)[None, :]
    dist = qi + BLOCK - kj
    in_window = (dist >= 0) & (dist < BLOCK)
    n = np.clip(dist, 0, None)
    max_exact = N_BUCKETS // 2
    nf = np.maximum(n, 1).astype(np.float32)
    large = max_exact + (np.log(nf / max_exact) / math.log(MAX_DISTANCE / max_exact)
                         * (N_BUCKETS - max_exact)).astype(np.int32)
    large = np.minimum(large, N_BUCKETS - 1)
    bucket = np.where(n < max_exact, n, large)
    onehot = (bucket[None] == np.arange(N_BUCKETS)[:, None, None]) & in_window[None]
    first = in_window & (kj >= BLOCK)
    return onehot.astype(np.float32), in_window, first


def kernel(x, c, w_ada, b_ada, g_ffn1, w1_ffn1, w3_ffn1, w2_ffn1, g_mix, w_in, spatial_w, spatial_b, g_v, g_q, g_k, sinks, rel_bias, w_out, g_ffn2, w1_ffn2, w3_ffn2, w2_ffn2, loss_target, m_w_ada, m_b_ada, m_g_ffn1, m_w1_ffn1, m_w3_ffn1, m_w2_ffn1, m_g_mix, m_w_in, m_spatial_w, m_spatial_b, m_g_v, m_g_q, m_g_k, m_sinks, m_rel_bias, m_w_out, m_g_ffn2, m_w1_ffn2, m_w3_ffn2, m_w2_ffn2, v_w_ada, v_b_ada, v_g_ffn1, v_w1_ffn1, v_w3_ffn1, v_w2_ffn1, v_g_mix, v_w_in, v_spatial_w, v_spatial_b, v_g_v, v_g_q, v_g_k, v_sinks, v_rel_bias, v_w_out, v_g_ffn2, v_w1_ffn2, v_w3_ffn2, v_w2_ffn2):
    ax, ay, ac = lax.axis_index("x"), lax.axis_index("y"), lax.axis_index("c")
    chip = 2 * ax + ay
    dev = 2 * chip + ac
    xs = x[0]
    tgt = loss_target[0]
    S, D = xs.shape
    F = N_CHIPS * w1_ffn1.shape[2]
    mod_cols = w_ada.shape[2]

    chip_arr = jnp.reshape(chip, (1,)).astype(jnp.int32)
    c_arr = jnp.reshape(ac, (1,)).astype(jnp.int32)
    jc_arr = jnp.stack([chip, ac]).astype(jnp.int32)
    cast = lambda w, nm: _cast_to_slot(w[0], chip_arr, f"cast_{nm}")
    ffn1_bufs = [cast(w1_ffn1, "w1_ffn1"), cast(w3_ffn1, "w3_ffn1"), cast(w2_ffn1, "w2_ffn1")]
    ffn1_gather = _ex_gather_ici(ffn1_bufs)
    ffn1_state, _ = _exchange_start(ffn1_gather, "gather_ffn1_ici_start")

    c_all = _allgather_small(jnp.pad(c, ((0, SUBLANES - 1), (0, 0))), "gather_c")
    c_all = jnp.pad(c_all[::SUBLANES], ((0, BF16_ROWS - N_DEV), (0, 0)))
    b_sh = lax.dynamic_slice(b_ada, (0, chip * mod_cols), (1, mod_cols))
    mod_part, c_act = _mod_partial(c_all, w_ada[0], b_sh, "mod_partial")
    mod_all = _allgather_small(mod_part[:N_DEV], "gather_mod")
    mod_all = jnp.reshape(mod_all, (N_CHIPS, 2, N_DEV, mod_cols))[:, 0]
    mod = jnp.reshape(lax.dynamic_index_in_dim(mod_all, dev, axis=1, keepdims=False), (1, N_MOD * D))
    sh1, sc1, gt1, sh2, sc2, gt2, sh3, sc3, gt3 = [mod[:, i * D:(i + 1) * D] for i in range(N_MOD)]

    def cols_to_natural(w4):
        return jnp.reshape(jnp.transpose(w4, (1, 0, 2)), (w4.shape[1], -1))

    mixer_bufs = [cast(w_in, "w_in"), cast(w_out, "w_out")]
    ffn2_bufs = [cast(w1_ffn2, "w1_ffn2"), cast(w3_ffn2, "w3_ffn2"), cast(w2_ffn2, "w2_ffn2")]
    ffn1_bufs = _exchange_wait(ffn1_gather, ffn1_state, [mod] + mixer_bufs + ffn2_bufs, "gather_ffn1_ici_wait")
    ffn1_bufs = _exchange_alone(_ex_gather_d2d(ffn1_bufs), "gather_ffn1_d2d")
    w1a, w3a, w2a = cols_to_natural(ffn1_bufs[0]), cols_to_natural(ffn1_bufs[1]), jnp.reshape(ffn1_bufs[2], (F, D))

    onehot_np, in_window_np, first_np = _bias_tables()
    onehot = jnp.asarray(onehot_np)
    bias = jnp.einsum("bij,bh->hij", onehot, rel_bias, precision=lax.Precision.HIGHEST)
    biasm = jnp.stack([jnp.where(jnp.asarray(first_np)[None], bias, NEG),
                       jnp.where(jnp.asarray(in_window_np)[None], bias, NEG)])
    causal = jnp.asarray(np.tril(np.ones((BLOCK, BLOCK), dtype=bool)))
    wm = jnp.where(causal[None], spatial_w[0], 0.0).astype(BF16)
    wm_t = jnp.transpose(wm, (0, 2, 1))
    sink_vec = sinks[0]
    per_group = PAIRS // KV_HEADS
    sbp = jnp.broadcast_to(jnp.reshape(spatial_b[0], (A_HEADS * BLOCK, 1)), (A_HEADS * BLOCK, A_DIM))
    gvp = jnp.repeat(g_v[0], BLOCK, axis=0)
    gq2, gk2 = jnp.concatenate([g_q, g_q], axis=1), jnp.concatenate([g_k, g_k], axis=1)
    seg_ones = jnp.asarray(np.kron(np.eye(2, dtype=np.float32), np.ones((HEAD_DIM, HEAD_DIM), np.float32)), BF16)
    lane_ones = jnp.full((LANES, LANES), 1.0 / LANES, BF16)
    pair_fold = jnp.asarray(np.kron(np.eye(per_group, LANES, dtype=np.float32), np.ones((BLOCK, 1), np.float32)))
    biasp = jnp.reshape(jnp.transpose(jnp.reshape(biasm, (2, KV_HEADS, per_group, 2, BLOCK, BAND)), (0, 1, 3, 5, 2, 4)),
                        (2, KV_HEADS, 2 * BAND, PAIR_ROWS))

    res = _ffn_fwd(xs, g_ffn1, sh1, sc1, gt1, w1a, w3a, w2a, None, "ffn1_fwd", ex=_ex_gather_ici(mixer_bufs + ffn2_bufs))
    (x1, h1, a1, b1, f1), mixer_bufs, ffn2_bufs = res[:5], res[5:7], res[7:]
    h2, *mixer_bufs = _norm_mod(x1, g_mix, sh2, sc2, "mixer_norm", ex=_ex_gather_d2d(mixer_bufs))
    win, wout = cols_to_natural(mixer_bufs[0]), jnp.reshape(mixer_bufs[1], (-1, D))
    z, *ffn2_bufs = _matmul(h2, win, "nn", F32, 1024, _tile(IN_COLS, 1664), D, "mixer_in", ex=_ex_gather_d2d(ffn2_bufs))
    w1b, w3b, w2b = cols_to_natural(ffn2_bufs[0]), cols_to_natural(ffn2_bufs[1]), jnp.reshape(ffn2_bufs[2], (F, D))
    mix = _mixer_fwd(z, wm, sbp, gvp, gq2, gk2, seg_ones, lane_ones, sink_vec, biasp, "mixer_fwd")
    x2, ymix = _mixer_out(mix, wout, x1, gt2, "mixer_out")
    g3, df3, h3, a3, b3, dgt3, loss_sum = _ffn_fwd(x2, g_ffn2, sh3, sc3, gt3, w1b, w3b, w2b, tgt, "ffn2_fwd_loss")
    loss = lax.psum(loss_sum[0, 0] * (0.5 / D), ("x", "y", "c"))

    tk = _tile(S, 2048)

    def ffn_weight_grads(h, da, db, s, df, tag, riding):
        ex, done = _ride(riding) if riding else (None, None)
        gw1 = _matmul(h, da, "tn", F32, 1024, F // N_CHIPS, tk, f"grad_w1_{tag}", shard_major=True, ex=ex)
        if riding:
            done(gw1[1:])
            gw1 = gw1[0]
        r1 = _Reduction(gw1, f"w1_{tag}", c_arr, jc_arr)
        ex, done = _ride([r1])
        gw3, *carried = _matmul(h, db, "tn", F32, 1024, F // N_CHIPS, tk, f"grad_w3_{tag}", shard_major=True, ex=ex)
        done(carried)
        r3 = _Reduction(gw3, f"w3_{tag}", c_arr, jc_arr)
        ex, done = _ride([r1, r3])
        gw2, *carried = _matmul(s, df, "tn", F32, _tile(F, 1408), 1024, tk, f"grad_w2_{tag}", ex=ex)
        done(carried)
        r2 = _Reduction(jnp.reshape(gw2, (N_CHIPS, F // N_CHIPS, D)), f"w2_{tag}", c_arr, jc_arr)
        return r1, r3, r2

    da3, db3, s3, dh3 = _ffn_bwd(df3, a3, b3, w1b, w3b, w2b, "ffn2_bwd")
    r21, r23, r22 = ffn_weight_grads(h3, da3, db3, s3, df3, "ffn2", [])
    ex, done = _ride([r22])
    res = _norm_bwd(dh3, x2, g3, g_ffn2, sc3, (ymix, gt2, 1.0), "ffn2_norm_bwd", ex=ex)
    g2, dsh3, dsc3, dgn3, dy, dgt2 = res[:6]
    done(res[6:])

    ex, done = _ride([r21])
    dmix, *carried = _matmul(dy, wout, "nt", BF16, 1024, 2048, D, "mixer_out_bwd", ex=ex)
    done(carried)
    ex, done = _ride([r23, r22])
    res = _mixer_bwd(z, dmix, wm, wm_t, sbp, gvp, gq2, gk2, seg_ones, lane_ones, sink_vec, biasp, pair_fold,
                     "mixer_bwd", ex=ex)
    dz_main, dz_kv, dwm, dsb, dgv, dgq, dgk, dsk, dst = res[:9]
    dsb = jnp.reshape(dsb[:, 0], (A_HEADS, BLOCK))
    dgq, dgk = dgq[:, :HEAD_DIM], dgk[:, :HEAD_DIM]
    dsk = jnp.reshape(jnp.transpose(jnp.reshape(dsk[:2 * KV_HEADS, :per_group], (KV_HEADS, 2, per_group)), (0, 2, 1)),
                      (1, B_HEADS))
    dst = jnp.reshape(jnp.transpose(jnp.reshape(dst, (KV_HEADS, 2, BAND, per_group, BLOCK)), (0, 3, 1, 4, 2)),
                      (B_HEADS, BLOCK * BAND))
    done(res[9:])
    dz = jnp.concatenate([dz_main, dz_kv], axis=1)
    ex, done = _ride([r23, r22])
    dh2, *carried = _matmul(dz, win, "nt", F32, 1024, 2048, _tile(IN_COLS, 1664), "mixer_in_bwd", ex=ex)
    done(carried)
    drel = _bucket_sum(dst, jnp.reshape(onehot, (N_BUCKETS, -1)), "bucket_sum")
    g1, dsh2, dsc2, dgn2, df1, dgt1 = _norm_bwd(dh2, x1, g2, g_mix, sc2, (f1, gt1, 0.5), "mixer_norm_bwd")

    da1, db1, s1, dh1 = _ffn_bwd(df1, a1, b1, w1a, w3a, w2a, "ffn1_bwd")
    r11, r13, r12 = ffn_weight_grads(h1, da1, db1, s1, df1, "ffn1", [])
    ex, done = _ride([r11, r13, r12])
    gwin_full, *carried = _matmul(h2, dz, "tn", F32, 1024, _tile(IN_COLS, 1664), tk, "grad_w_in", ex=ex)
    done(carried)
    rm_in = _Reduction(jnp.transpose(jnp.reshape(gwin_full, (D, N_CHIPS, -1)), (1, 0, 2)), "w_in", c_arr, jc_arr)
    grad_x, dsh1, dsc1, dgn1 = _norm_bwd(dh1, xs, g1, g_ffn1, sc1, None, "ffn1_norm_bwd")
    ex, done = _ride([r13, r12, rm_in])
    gwout_full, *carried = _matmul(mix, dy, "tn", F32, 1024, 1024, tk, "grad_w_out", ex=ex)
    done(carried)
    rm_out = _Reduction(jnp.reshape(gwout_full, (N_CHIPS, -1, D)), "w_out", c_arr, jc_arr)

    dmod = jnp.concatenate([dsh1, dsc1, dgt1, dsh2, dsc2, dgt2, dsh3, dsc3, dgt3], axis=1)
    small_w = [b_ada, g_ffn1, g_mix, g_ffn2, spatial_w, spatial_b, g_v, g_q, g_k, sinks, rel_bias]
    small_m = [m_b_ada, m_g_ffn1, m_g_mix, m_g_ffn2, m_spatial_w, m_spatial_b, m_g_v, m_g_q, m_g_k, m_sinks, m_rel_bias]
    small_v = [v_b_ada, v_g_ffn1, v_g_mix, v_g_ffn2, v_spatial_w, v_spatial_b, v_g_v, v_g_q, v_g_k, v_sinks, v_rel_bias]
    small_g = [dmod, dgn1, dgn2, dgn3, jnp.where(causal[None], dwm, 0.0), dsb, dgv, dgq, dgk, dsk, drel]
    shapes = [w.shape for w in small_w]
    gpack = _pack(small_g)
    rows = gpack.shape[0]
    gall = jnp.reshape(_allgather_small(gpack, "gather_small"), (N_DEV, rows, LANES))
    sg, sd, sm, sv = _small_update(gall, _pack(small_w), _pack(small_m), _pack(small_v), "small_update")
    sg, sd, sm, sv = [_unpack(p, shapes) for p in (sg, sd, sm, sv)]

    mod_rows = -(-N_MOD * D // (SUBLANES * LANES)) * SUBLANES
    dmod_all = jnp.reshape(gall[:, :mod_rows], (N_DEV, -1))[:, :N_MOD * D]
    dmod_sh = lax.dynamic_slice(dmod_all, (0, chip * mod_cols), (N_DEV, mod_cols))
    dmod_sh = jnp.pad(dmod_sh, ((0, BF16_ROWS - N_DEV), (0, 0))).astype(BF16)
    g_wada = _matmul(c_act, dmod_sh, "tn", F32, 1024, _tile(mod_cols, 512), BF16_ROWS, "grad_w_ada")

    big = {}

    def update(nm, w, g, m, v, token):
        g_out, d, nm_, nv_ = _adamw(w[0], g, m[0], v[0], f"adamw_{nm}", emit_grad=True, after=[token])
        big[nm] = (g_out[None], d[None], nm_[None], nv_[None])
        return d

    ex, done = _ride([r12, rm_in, rm_out])
    state, token = _exchange_start(ex, "reduce_tail_0_start")
    behind = [update("w1_ffn2", w1_ffn2, r21.result, m_w1_ffn2, v_w1_ffn2, token),
              update("w3_ffn2", w3_ffn2, r23.result, m_w3_ffn2, v_w3_ffn2, token),
              update("w2_ffn2", w2_ffn2, r22.result, m_w2_ffn2, v_w2_ffn2, token)]
    done(_exchange_wait(ex, state, behind, "reduce_tail_0_wait"))
    ex, done = _ride([rm_in, rm_out])
    state, token = _exchange_start(ex, "reduce_tail_1_start")
    behind = [update("w1_ffn1", w1_ffn1, r11.result, m_w1_ffn1, v_w1_ffn1, token),
              update("w3_ffn1", w3_ffn1, r13.result, m_w3_ffn1, v_w3_ffn1, token),
              update("w2_ffn1", w2_ffn1, r12.result, m_w2_ffn1, v_w2_ffn1, token)]
    done(_exchange_wait(ex, state, behind, "reduce_tail_1_wait"))
    ex, done = _ride([rm_out])
    state, token = _exchange_start(ex, "reduce_tail_2_start")
    d_wada, nm_wada, nv_wada = _adamw(w_ada[0], g_wada, m_w_ada[0], v_w_ada[0], "adamw_w_ada", after=[token])
    behind = [d_wada, update("w_in", w_in, rm_in.result, m_w_in, v_w_in, token)]
    done(_exchange_wait(ex, state, behind, "reduce_tail_2_wait"))
    update("w_out", w_out, rm_out.result, m_w_out, v_w_out, token)
    big["w_ada"] = (g_wada[None], d_wada[None], nm_wada[None], nv_wada[None])

    order = ["w_ada", "b_ada", "g_ffn1", "w1_ffn1", "w3_ffn1", "w2_ffn1", "g_mix", "w_in", "spatial_w", "spatial_b",
             "g_v", "g_q", "g_k", "sinks", "rel_bias", "w_out", "g_ffn2", "w1_ffn2", "w3_ffn2", "w2_ffn2"]
    small_names = ["b_ada", "g_ffn1", "g_mix", "g_ffn2", "spatial_w", "spatial_b", "g_v", "g_q", "g_k", "sinks", "rel_bias"]
    for i, nm in enumerate(small_names):
        big[nm] = (sg[i], sd[i], sm[i], sv[i])
    outs = [loss, grad_x[None]]
    for kind in range(4):
        outs += [big[nm][kind] for nm in order]
    return tuple(outs)
```

```python
import functools
import math

import jax
import jax.numpy as jnp
import numpy as np
from jax import lax
from jax.experimental import pallas as pl
from jax.experimental.pallas import tpu as pltpu

F32 = jnp.float32
BF16 = jnp.bfloat16
MESH = pl.DeviceIdType.MESH
ANY = pl.BlockSpec(memory_space=pl.ANY)

EPS = 1e-6
BLOCK = 128
A_HEADS = 8
A_DIM = 128
A_WIDTH = A_HEADS * A_DIM
B_HEADS = 16
KV_HEADS = 2
GROUP = B_HEADS // KV_HEADS
HEAD_DIM = 64
B_WIDTH = B_HEADS * HEAD_DIM
KV_WIDTH = KV_HEADS * HEAD_DIM
Q_OFF = 2 * A_WIDTH
K_OFF = Q_OFF + B_WIDTH
V_OFF = K_OFF + KV_WIDTH
IN_COLS = V_OFF + KV_WIDTH
N_BUCKETS = 32
MAX_DISTANCE = 128
N_MOD = 9
N_CHIPS = 4
N_DEV = 8
NEG = -1e30

ADAM_LR = 0.001
ADAM_B1 = 0.9
ADAM_B2 = 0.999
ADAM_EPS = 1e-08
ADAM_WD = 0.01
ADAM_STEP = 10

LANES = 128
SUBLANES = 8
BF16_ROWS = 16
VMEM_LIMIT = 60 * 1024 * 1024

INV_SQRT2 = 1.0 / math.sqrt(2.0)
INV_SQRT_2PI = 1.0 / math.sqrt(2.0 * math.pi)


def _tile(n, pref, mult=LANES):
    t = (min(pref, n) // mult) * mult
    while t >= mult:
        if n % t == 0:
            return t
        t -= mult
    return n


def _params(sem):
    return pltpu.CompilerParams(dimension_semantics=sem, vmem_limit_bytes=VMEM_LIMIT)


class _Exchange:
    def __init__(self, operands, out_shapes, aliases, n_sems, plan):
        self.operands, self.out_shapes, self.aliases, self.n_sems, self.plan = operands, out_shapes, aliases, n_sems, plan


def _pallas(body, *, name, grid, in_specs, out_specs, out_shape, args, scratch_shapes=(), semantics=None, ex=None,
            after=()):
    if ex is None:
        n_in = len(in_specs)

        def ordered(*refs):
            body(*refs[:n_in], *refs[n_in + len(after):])

        return pl.pallas_call(ordered if after else body, name=name, grid=grid,
                              in_specs=list(in_specs) + [ANY] * len(after), out_specs=out_specs, out_shape=out_shape,
                              scratch_shapes=list(scratch_shapes), compiler_params=_params(semantics))(*args, *after)
    assert not after
    n_in, n_out, n_scr = len(in_specs), len(out_specs), len(scratch_shapes)
    e_in, e_out = len(ex.operands), len(ex.out_shapes)

    def wrapped(*refs):
        ins, refs = refs[:n_in], refs[n_in:]
        ex_ins, refs = refs[:e_in], refs[e_in:]
        outs, refs = refs[:n_out], refs[n_out:]
        ex_outs, refs = refs[:e_out], refs[e_out:]
        scratch, (send_sems, recv_sems) = refs[:n_scr], refs[n_scr:]
        first, last = True, True
        for d, size in enumerate(grid):
            first = jnp.logical_and(first, pl.program_id(d) == 0)
            last = jnp.logical_and(last, pl.program_id(d) == size - 1)

        def start():
            sends, _ = ex.plan(ex_ins, ex_outs, send_sems, recv_sems)
            for cp in sends:
                _remote(*cp).start()

        def finish():
            sends, arrivals = ex.plan(ex_ins, ex_outs, send_sems, recv_sems)
            for cp in arrivals:
                _remote(*cp).wait_recv()
            for cp in sends:
                _remote(*cp).wait_send()

        if grid:
            pl.when(first)(start)
        else:
            start()
        if body is not None:
            body(*ins, *outs, *scratch)
        if grid:
            pl.when(last)(finish)
        else:
            finish()

    kwargs = dict(grid=grid) if grid else {}
    return pl.pallas_call(
        wrapped,
        name=name,
        in_specs=list(in_specs) + [ANY] * e_in,
        out_specs=list(out_specs) + [ANY] * e_out,
        out_shape=list(out_shape) + list(ex.out_shapes),
        input_output_aliases={n_in + i: n_out + o for i, o in ex.aliases.items()},
        scratch_shapes=list(scratch_shapes) + [pltpu.SemaphoreType.DMA((ex.n_sems,)), pltpu.SemaphoreType.DMA((ex.n_sems,))],
        compiler_params=_params(("arbitrary",) * len(grid) if grid else None),
        **kwargs,
    )(*args, *ex.operands)


def _dot(a, b, dims=(((1,), (0,)), ((), ()))):
    return lax.dot_general(a, b, dims, preferred_element_type=F32)


NN = (((1,), (0,)), ((), ()))
NT = (((1,), (1,)), ((), ()))
TN = (((0,), (0,)), ((), ()))


def _sigmoid(x):
    return 1.0 / (1.0 + jnp.exp(-x))


def _gelu_and_grad(x):
    cdf = 0.5 * (1.0 + lax.erf(x * INV_SQRT2))
    pdf = jnp.exp(-0.5 * x * x) * INV_SQRT_2PI
    return x * cdf, cdf + x * pdf


def _gelu(x):
    return x * (0.5 * (1.0 + lax.erf(x * INV_SQRT2)))


def _rms(x):
    r = lax.rsqrt(jnp.mean(x * x, axis=-1, keepdims=True) + EPS)
    return x * r, r


ROW_CHUNK = 64


def _for_rows(tm, fn):
    rc = min(ROW_CHUNK, tm)

    def step(r, carry):
        fn(pl.ds(pl.multiple_of(r * rc, rc), rc))
        return carry

    lax.fori_loop(0, tm // rc, step, 0)


def _rms_bwd(dy, xhat, r):
    return r * (dy - xhat * jnp.mean(dy * xhat, axis=-1, keepdims=True))


def _matmul(a, b, mode, out_dtype, tm, tn, tk, name, shard_major=False, ex=None, after=()):
    if mode == "nn":
        (M, K), N = a.shape, b.shape[1]
    elif mode == "nt":
        (M, K), N = a.shape, b.shape[0]
    else:
        (K, M), N = a.shape, b.shape[1]
    tm, tn, tk = min(tm, M), min(tn, N), min(tk, K)
    assert M % tm == 0 and N % tn == 0 and K % tk == 0, (name, M, N, K, tm, tn, tk)
    nk = K // tk
    dims = {"nn": NN, "nt": NT, "tn": TN}[mode]
    a_spec = pl.BlockSpec((tk, tm), lambda i, j, k: (k, i)) if mode == "tn" else pl.BlockSpec((tm, tk), lambda i, j, k: (i, k))
    b_spec = pl.BlockSpec((tn, tk), lambda i, j, k: (j, k)) if mode == "nt" else pl.BlockSpec((tk, tn), lambda i, j, k: (k, j))
    if shard_major:
        assert tn * N_CHIPS == N
        out_shape = jax.ShapeDtypeStruct((N_CHIPS, M, tn), out_dtype)
        o_spec = pl.BlockSpec((None, tm, tn), lambda i, j, k: (j, i, 0))
    else:
        out_shape = jax.ShapeDtypeStruct((M, N), out_dtype)
        o_spec = pl.BlockSpec((tm, tn), lambda i, j, k: (i, j))

    direct = nk == 1 or out_dtype == F32

    def body(a_ref, b_ref, o_ref, *scratch):
        k = pl.program_id(2)
        if nk == 1:
            o_ref[...] = _dot(a_ref[...], b_ref[...], dims).astype(o_ref.dtype)
            return
        acc_ref = o_ref if direct else scratch[0]

        @pl.when(k == 0)
        def _():
            acc_ref[...] = jnp.zeros(acc_ref.shape, F32)

        acc_ref[...] += _dot(a_ref[...], b_ref[...], dims)
        if not direct:
            @pl.when(k == nk - 1)
            def _():
                o_ref[...] = acc_ref[...].astype(o_ref.dtype)

    outs = _pallas(body, name=name, grid=(M // tm, N // tn, nk), in_specs=[a_spec, b_spec], out_specs=[o_spec],
                   out_shape=[out_shape], scratch_shapes=[] if direct else [pltpu.VMEM((tm, tn), F32)],
                   semantics=("parallel", "parallel", "arbitrary"), args=[a, b], ex=ex, after=after)
    return outs[0] if ex is None else outs


def _mod_partial(c_all, w_ada, b_sh, name):
    R, D = c_all.shape
    N = w_ada.shape[1]
    tn = _tile(N, 512)

    def body(c_ref, w_ref, b_ref, o_ref, ca_ref):
        cv = c_ref[...]
        ca = (cv * _sigmoid(cv)).astype(BF16)
        ca_ref[...] = ca
        o_ref[...] = _dot(ca, w_ref[...].astype(BF16)) + b_ref[...]

    return pl.pallas_call(
        body,
        name=name,
        grid=(N // tn,),
        in_specs=[
            pl.BlockSpec((R, D), lambda j: (0, 0)),
            pl.BlockSpec((D, tn), lambda j: (0, j)),
            pl.BlockSpec((1, tn), lambda j: (0, j)),
        ],
        out_specs=[pl.BlockSpec((R, tn), lambda j: (0, j)), pl.BlockSpec((R, D), lambda j: (0, 0))],
        out_shape=[jax.ShapeDtypeStruct((R, N), F32), jax.ShapeDtypeStruct((R, D), BF16)],
        compiler_params=_params(("arbitrary",)),
    )(c_all, w_ada, b_sh)


FFN_BLOCK = 1024


def _ffn_blocks(F):
    if F % FFN_BLOCK == 0 or F < FFN_BLOCK:
        tf = _tile(F, FFN_BLOCK)
        return tf, F // tf, tf
    nj = -(-F // FFN_BLOCK)
    tail = F - (nj - 1) * FFN_BLOCK
    assert tail % LANES == 0
    return FFN_BLOCK, nj, tail
def _ffn_fwd(x, g, sh, sc, gt, w1, w3, w2, tgt, name, ex=None, h=None):
    S, D = x.shape
    F = w1.shape[1]
    tm, tf, nj, tail = _tile(S, 512), *_ffn_blocks(F)
    ni = S // tm
    with_loss = tgt is not None
    assert h is None or not with_loss

    def body(*refs):
        if with_loss:
            (x_ref, g_ref, sh_ref, sc_ref, gt_ref, w1_ref, w3_ref, w2_ref, tgt_ref,
             gout_ref, df_ref, h_ref, a_ref, b_ref, dgt_ref, loss_ref, acc_ref) = refs
        elif h is None:
            (x_ref, g_ref, sh_ref, sc_ref, gt_ref, w1_ref, w3_ref, w2_ref,
             xo_ref, h_ref, a_ref, b_ref, f_ref, acc_ref) = refs
        else:
            (x_ref, g_ref, sh_ref, sc_ref, gt_ref, w1_ref, w3_ref, w2_ref, h_ref,
             xo_ref, a_ref, b_ref, f_ref, acc_ref) = refs
        i, j = pl.program_id(0), pl.program_id(1)

        if h is None:
            @pl.when(j == 0)
            def _():
                def prologue(rows):
                    xhat, _ = _rms(x_ref[rows, :])
                    h_ref[rows, :] = ((xhat * g_ref[...]) * (1.0 + sc_ref[...]) + sh_ref[...]).astype(BF16)

                _for_rows(tm, prologue)

        @pl.when(j == 0)
        def _():
            acc_ref[...] = jnp.zeros(acc_ref.shape, F32)

        def columns(width):
            def run():
                hb = h_ref[...]
                av = _dot(hb, w1_ref[:, :width])
                bv = _dot(hb, w3_ref[:, :width])
                a_ref[:, :width] = av.astype(BF16)
                b_ref[:, :width] = bv.astype(BF16)
                sv = ((av * _sigmoid(av)) * bv).astype(BF16)
                acc_ref[...] += _dot(sv, w2_ref[:width, :])
            return run

        if tail == tf:
            columns(tf)()
        else:
            pl.when(j < nj - 1)(columns(tf))
            pl.when(j == nj - 1)(columns(tail))

        @pl.when(j == nj - 1)
        def _():
            if with_loss:
                @pl.when(i == 0)
                def _():
                    dgt_ref[...] = jnp.zeros(dgt_ref.shape, F32)
                    loss_ref[...] = jnp.zeros(loss_ref.shape, F32)

            def epilogue(rows):
                fv = acc_ref[rows, :]
                half_gate = 0.5 * gt_ref[...]
                xo = x_ref[rows, :] + half_gate * fv
                if not with_loss:
                    xo_ref[rows, :] = xo
                    f_ref[rows, :] = fv.astype(f_ref.dtype)
                    return
                err = xo - tgt_ref[rows, :]
                gout = err * (1.0 / D)
                gout_ref[rows, :] = gout
                df_ref[rows, :] = (half_gate * gout).astype(BF16)
                dgt_ref[...] += 0.5 * jnp.sum(gout * fv, axis=0, keepdims=True)
                loss_part = jnp.sum(jnp.sum(err * err, axis=1, keepdims=True), axis=0, keepdims=True)
                loss_ref[...] += jnp.broadcast_to(loss_part, loss_ref.shape)

            _for_rows(tm, epilogue)

    row = pl.BlockSpec((tm, D), lambda i, j: (i, 0))
    row_in = pl.BlockSpec((tm, D), lambda i, j: (i, 0), pipeline_mode=pl.Buffered(1))
    vec = pl.BlockSpec((1, D), lambda i, j: (0, 0))
    col = pl.BlockSpec((tm, tf), lambda i, j: (i, j))
    in_specs = [row_in, vec, vec, vec, vec,
                pl.BlockSpec((D, tf), lambda i, j: (0, j)),
                pl.BlockSpec((D, tf), lambda i, j: (0, j)),
                pl.BlockSpec((tf, D), lambda i, j: (j, 0))]
    args = [x, g, sh, sc, gt, w1, w3, w2]
    act = jax.ShapeDtypeStruct((S, F), BF16)
    if with_loss:
        in_specs.append(row_in)
        args.append(tgt)
        out_specs = [row, row, row_in, col, col, vec, pl.BlockSpec((1, LANES), lambda i, j: (0, 0))]
        out_shape = [jax.ShapeDtypeStruct((S, D), F32), jax.ShapeDtypeStruct((S, D), BF16),
                     jax.ShapeDtypeStruct((S, D), BF16), act, act,
                     jax.ShapeDtypeStruct((1, D), F32), jax.ShapeDtypeStruct((1, LANES), F32)]
    elif h is None:
        out_specs = [row, row, col, col, row]
        out_shape = [jax.ShapeDtypeStruct((S, D), F32), jax.ShapeDtypeStruct((S, D), BF16), act, act,
                     jax.ShapeDtypeStruct((S, D), BF16)]
    else:
        in_specs.append(row_in)
        args.append(h)
        out_specs = [row, col, col, row]
        out_shape = [jax.ShapeDtypeStruct((S, D), F32), act, act, jax.ShapeDtypeStruct((S, D), BF16)]
    return _pallas(body, name=name, grid=(ni, nj), in_specs=in_specs, out_specs=out_specs, out_shape=out_shape,
                   scratch_shapes=[pltpu.VMEM((tm, D), F32)],
                   semantics=("arbitrary", "arbitrary"), args=args, ex=ex)


def _ffn_bwd(df, a, b, w1, w3, w2, name, ex=None):
    S, D = df.shape
    F = a.shape[1]
    tm, tf, nj, tail = _tile(S, 512), *_ffn_blocks(F)

    def body(df_ref, a_ref, b_ref, w1_ref, w3_ref, w2_ref, da_ref, db_ref, s_ref, dh_ref):
        j = pl.program_id(1)

        @pl.when(j == 0)
        def _():
            dh_ref[...] = jnp.zeros(dh_ref.shape, F32)

        def columns(width):
            def run():
                ds = _dot(df_ref[...], w2_ref[:width, :], NT)
                av = a_ref[:, :width].astype(F32)
                bv = b_ref[:, :width].astype(F32)
                sig = _sigmoid(av)
                sil = av * sig
                da = ((ds * bv) * (sig * (1.0 + av * (1.0 - sig)))).astype(BF16)
                db = (ds * sil).astype(BF16)
                da_ref[:, :width] = da
                db_ref[:, :width] = db
                s_ref[:, :width] = (sil * bv).astype(BF16)
                dh_ref[...] += _dot(da, w1_ref[:, :width], NT) + _dot(db, w3_ref[:, :width], NT)
            return run

        if tail == tf:
            columns(tf)()
        else:
            pl.when(j < nj - 1)(columns(tf))
            pl.when(j == nj - 1)(columns(tail))

    row = pl.BlockSpec((tm, D), lambda i, j: (i, 0))
    col = pl.BlockSpec((tm, tf), lambda i, j: (i, j))
    act = jax.ShapeDtypeStruct((S, F), BF16)
    return _pallas(body, name=name, grid=(S // tm, nj),
                   in_specs=[row, col, col,
                             pl.BlockSpec((D, tf), lambda i, j: (0, j)),
                             pl.BlockSpec((D, tf), lambda i, j: (0, j)),
                             pl.BlockSpec((tf, D), lambda i, j: (j, 0))],
                   out_specs=[col, col, col, row],
                   out_shape=[act, act, act, jax.ShapeDtypeStruct((S, D), F32)],
                   semantics=("parallel", "arbitrary"), args=[df, a, b, w1, w3, w2], ex=ex)


def _norm_mod(x, g, sh, sc, name, ex=None):
    S, D = x.shape
    tm = _tile(S, 512)

    def body(x_ref, g_ref, sh_ref, sc_ref, h_ref):
        def step(rows):
            xhat, _ = _rms(x_ref[rows, :])
            h_ref[rows, :] = ((xhat * g_ref[...]) * (1.0 + sc_ref[...]) + sh_ref[...]).astype(BF16)

        _for_rows(tm, step)

    row = pl.BlockSpec((tm, D), lambda i: (i, 0))
    vec = pl.BlockSpec((1, D), lambda i: (0, 0))
    outs = _pallas(body, name=name, grid=(S // tm,), in_specs=[row, vec, vec, vec], out_specs=[row],
                   out_shape=[jax.ShapeDtypeStruct((S, D), BF16)], semantics=("parallel",), args=[x, g, sh, sc], ex=ex)
    return outs[0] if ex is None else outs


def _norm_bwd(dh, x, gres, g, sc, prev, name, ex=None, after=()):
    S, D = x.shape
    tm = _tile(S, 256)
    has_prev = prev is not None
    coef = prev[2] if has_prev else None

    def body(*refs):
        if has_prev:
            (dh_ref, x_ref, gr_ref, g_ref, sc_ref, f_ref, gt_ref,
             go_ref, dsh_ref, dsc_ref, dg_ref, dp_ref, dgt_ref) = refs
        else:
            dh_ref, x_ref, gr_ref, g_ref, sc_ref, go_ref, dsh_ref, dsc_ref, dg_ref = refs
        sum_refs = [dsh_ref, dsc_ref, dg_ref] + ([dgt_ref] if has_prev else [])

        @pl.when(pl.program_id(0) == 0)
        def _():
            for ref in sum_refs:
                ref[...] = jnp.zeros(ref.shape, F32)

        def step(rows):
            dh = dh_ref[rows, :]
            xhat, r = _rms(x_ref[rows, :])
            gain = g_ref[...]
            scale1 = 1.0 + sc_ref[...]
            gout = gr_ref[rows, :] + _rms_bwd(dh * scale1 * gain, xhat, r)
            go_ref[rows, :] = gout
            sums = [dh, dh * (xhat * gain), dh * scale1 * xhat]
            if has_prev:
                dp_ref[rows, :] = ((coef * gt_ref[...]) * gout).astype(BF16)
                sums.append(coef * (gout * f_ref[rows, :].astype(F32)))
            for ref, v in zip(sum_refs, sums):
                ref[...] += jnp.sum(v, axis=0, keepdims=True)

        _for_rows(tm, step)

    row = pl.BlockSpec((tm, D), lambda i: (i, 0))
    vec = pl.BlockSpec((1, D), lambda i: (0, 0))
    vshape = jax.ShapeDtypeStruct((1, D), F32)
    in_specs = [row, row, row, vec, vec]
    args = [dh, x, gres, g, sc]
    out_specs = [row, vec, vec, vec]
    out_shape = [jax.ShapeDtypeStruct((S, D), F32), vshape, vshape, vshape]
    if has_prev:
        in_specs += [row, vec]
        args += [prev[0], prev[1]]
        out_specs += [row, vec]
        out_shape += [jax.ShapeDtypeStruct((S, D), BF16), vshape]
    return _pallas(body, name=name, grid=(S // tm,), in_specs=in_specs, out_specs=out_specs, out_shape=out_shape,
                   semantics=("arbitrary",), args=args, ex=ex, after=after)


PAIRS = B_HEADS // 2
PAIR_ROWS = (PAIRS // KV_HEADS) * BLOCK
BAND = 2 * BLOCK


def _stack(ref, offset, count):
    return jnp.concatenate([ref[:, offset + p * LANES:offset + (p + 1) * LANES] for p in range(count)], axis=0)


def _seg_mean(x, e_ref):
    return _dot(x.astype(BF16), e_ref[...]) * (1.0 / HEAD_DIM)


def _block_diag(x, x_rolled, left, kv_head):
    if kv_head == 0:
        top, bottom = jnp.where(left, x, 0.0), jnp.where(left, 0.0, x_rolled)
    else:
        top, bottom = jnp.where(left, x_rolled, 0.0), jnp.where(left, 0.0, x)
    return jnp.concatenate([top, bottom], axis=0).astype(BF16)


def _from_block_diag(g, left, kv_head):
    a, b = g[:BAND], g[BAND:]
    if kv_head == 0:
        return jnp.where(left, a + pltpu.roll(b, HEAD_DIM, 1), 0.0)
    return jnp.where(left, 0.0, pltpu.roll(a, HEAD_DIM, 1) + b)


def _pair_softmax(st, sk_ref, kv_head):
    out = []
    for e in range(2):
        seg = st[e * BAND:(e + 1) * BAND]
        sink = jnp.concatenate([jnp.full((1, BLOCK), sk_ref[kv_head * GROUP + 2 * p + e], F32)
                                for p in range(PAIRS // KV_HEADS)], axis=1)
        m = jnp.maximum(jnp.max(seg, axis=0, keepdims=True), sink)
        p_ = jnp.exp(seg - m)
        e_sink = jnp.exp(sink - m)
        inv = 1.0 / (jnp.sum(p_, axis=0, keepdims=True) + e_sink)
        out.append((p_ * inv, e_sink * inv))
    return out


def _lane_mean(x, ones_ref):
    return _dot(x.astype(BF16), ones_ref[...])


def _mixer_specs(nb, last):
    full = lambda shape: pl.BlockSpec(shape, lambda n: (0,) * len(shape))
    z_spec = pl.BlockSpec((BLOCK, IN_COLS), lambda n: (jnp.minimum(n, last), 0))
    zp_spec = pl.BlockSpec((BLOCK, 2 * KV_WIDTH), lambda n: (jnp.clip(n - 1, 0, last), K_OFF // (2 * KV_WIDTH)))
    consts = [full((A_HEADS * BLOCK, A_DIM)), full((A_HEADS * BLOCK, A_DIM)), full((1, LANES)), full((1, LANES)),
              full((LANES, LANES)), full((LANES, LANES)), pl.BlockSpec(memory_space=pltpu.SMEM),
              pl.BlockSpec((None, KV_HEADS, PAIR_ROWS, 2 * BAND), lambda n: (jnp.minimum(n, 1), 0, 0, 0))]
    return full, z_spec, zp_spec, consts


def _mixer_fwd(z, wm, sbp, gvp, gq2, gk2, seg_ones, lane_ones, sinks, biasp, name):
    S = z.shape[0]
    nb = S // BLOCK

    def body(z_ref, zp_ref, wm_ref, sbp_ref, gvp_ref, gq2_ref, gk2_ref, e_ref, l_ref, sk_ref, bias_ref, mix_ref):
        u = _gelu(_stack(z_ref, 0, A_HEADS))
        v = _gelu(_stack(z_ref, A_WIDTH, A_HEADS))
        vhat = v * lax.rsqrt(_lane_mean(v * v, l_ref) + EPS)
        vn = (vhat * gvp_ref[...]).astype(BF16)
        mixed = jnp.concatenate([_dot(wm_ref[h], vn[h * BLOCK:(h + 1) * BLOCK]) for h in range(A_HEADS)], axis=0)
        ya = (u * (mixed + sbp_ref[...])).astype(BF16)
        for h in range(A_HEADS):
            mix_ref[:, h * A_DIM:(h + 1) * A_DIM] = ya[h * BLOCK:(h + 1) * BLOCK]

        left = lax.broadcasted_iota(jnp.int32, (1, LANES), 1) < HEAD_DIM
        kv = jnp.concatenate([zp_ref[...], z_ref[:, K_OFF:K_OFF + 2 * KV_WIDTH]], axis=0)
        k2, v2 = kv[:, :KV_WIDTH], kv[:, KV_WIDTH:]
        kn2 = k2 * lax.rsqrt(_seg_mean(k2 * k2, e_ref) + EPS) * gk2_ref[...]
        kn2_r, v2_r = pltpu.roll(kn2, HEAD_DIM, 1), pltpu.roll(v2, HEAD_DIM, 1)
        qp = _stack(z_ref, Q_OFF, PAIRS)
        qn = (qp * lax.rsqrt(_seg_mean(qp * qp, e_ref) + EPS) * gq2_ref[...]).astype(BF16)
        for kh in range(KV_HEADS):
            kbd, vbd = _block_diag(kn2, kn2_r, left, kh), _block_diag(v2, v2_r, left, kh)
            st = _dot(kbd, qn[kh * PAIR_ROWS:(kh + 1) * PAIR_ROWS], NT) * (HEAD_DIM ** -0.5) + bias_ref[kh]
            wt = jnp.concatenate([w_e for w_e, _ in _pair_softmax(st, sk_ref, kh)], axis=0).astype(BF16)
            o = _dot(wt, vbd, TN).astype(BF16)
            for p in range(PAIRS // KV_HEADS):
                col = A_WIDTH + (kh * (PAIRS // KV_HEADS) + p) * LANES
                mix_ref[:, col:col + LANES] = o[p * BLOCK:(p + 1) * BLOCK]

    full, z_spec, zp_spec, consts = _mixer_specs(nb, nb - 1)
    return pl.pallas_call(
        body,
        name=name,
        grid=(nb,),
        in_specs=[z_spec, zp_spec, full((A_HEADS, BLOCK, BLOCK))] + consts,
        out_specs=pl.BlockSpec((BLOCK, A_WIDTH + B_WIDTH), lambda n: (n, 0)),
        out_shape=jax.ShapeDtypeStruct((S, A_WIDTH + B_WIDTH), BF16),
        compiler_params=_params(("parallel",)),
    )(z, z, wm, sbp, gvp, gq2, gk2, seg_ones, lane_ones, sinks, biasp)


def _mixer_bwd(z, dmix, wm, wm_t, sbp, gvp, gq2, gk2, seg_ones, lane_ones, sinks, biasp, pair_fold, name, ex=None):
    S = z.shape[0]
    nb = S // BLOCK

    def body(z_ref, zp_ref, dmix_ref, wm_ref, wmt_ref, sbp_ref, gvp_ref, gq2_ref, gk2_ref, e_ref, l_ref, sk_ref,
             bias_ref, fold_ref,
             dz_ref, dzkv_ref, dwm_ref, dsb_ref, dgv_ref, dgq_ref, dgk_ref, dsk_ref, dst_ref,
             carry_ref, tot_ref, sbacc_ref, skacc_ref, gqacc_ref, gkacc_ref):
        n = pl.program_id(0)
        left = lax.broadcasted_iota(jnp.int32, (1, LANES), 1) < HEAD_DIM

        @pl.when(n == 0)
        def _():
            for ref in (dwm_ref, dgv_ref, dst_ref, carry_ref, sbacc_ref, skacc_ref, gqacc_ref, gkacc_ref):
                ref[...] = jnp.zeros(ref.shape, ref.dtype)

        @pl.when(n < nb)
        def _():
            u, du_dz = _gelu_and_grad(_stack(z_ref, 0, A_HEADS))
            v, dv_dz = _gelu_and_grad(_stack(z_ref, A_WIDTH, A_HEADS))
            rv = lax.rsqrt(_lane_mean(v * v, l_ref) + EPS)
            vhat = v * rv
            gvp = gvp_ref[...]
            vn = (vhat * gvp).astype(BF16)
            rows = lambda a, h: a[h * BLOCK:(h + 1) * BLOCK]
            mixed = jnp.concatenate([_dot(wm_ref[h], rows(vn, h)) for h in range(A_HEADS)], axis=0) + sbp_ref[...]
            dya = _stack(dmix_ref, 0, A_HEADS).astype(F32)
            dmx = dya * u
            sbacc_ref[...] += dmx
            dmx_b = dmx.astype(BF16)
            for h in range(A_HEADS):
                dwm_ref[h] += _dot(rows(dmx_b, h), rows(vn, h), NT)
            dvn = jnp.concatenate([_dot(wmt_ref[h], rows(dmx_b, h)) for h in range(A_HEADS)], axis=0)
            dgv_ref[...] += jnp.sum(jnp.reshape(dvn * vhat, (A_HEADS, BLOCK, A_DIM)), axis=1)
            dzu = ((dya * mixed) * du_dz).astype(BF16)
            tv = dvn * gvp
            dzv = ((rv * (tv - vhat * _lane_mean(tv * vhat, l_ref))) * dv_dz).astype(BF16)
            for h in range(A_HEADS):
                dz_ref[:, h * A_DIM:(h + 1) * A_DIM] = rows(dzu, h)
                dz_ref[:, A_WIDTH + h * A_DIM:A_WIDTH + (h + 1) * A_DIM] = rows(dzv, h)

            kv = jnp.concatenate([zp_ref[...], z_ref[:, K_OFF:K_OFF + 2 * KV_WIDTH]], axis=0)
            k2, v2 = kv[:, :KV_WIDTH], kv[:, KV_WIDTH:]
            kn2 = k2 * lax.rsqrt(_seg_mean(k2 * k2, e_ref) + EPS) * gk2_ref[...]
            kn2_r, v2_r = pltpu.roll(kn2, HEAD_DIM, 1), pltpu.roll(v2, HEAD_DIM, 1)
            qp = _stack(z_ref, Q_OFF, PAIRS)
            rq = lax.rsqrt(_seg_mean(qp * qp, e_ref) + EPS)
            qhat = qp * rq
            gq2 = gq2_ref[...]
            qn = (qhat * gq2).astype(BF16)
            dop = _stack(dmix_ref, A_WIDTH, PAIRS)
            dqn_parts = []
            dk2n = jnp.zeros((BAND, KV_WIDTH), F32)
            dv2 = jnp.zeros((BAND, KV_WIDTH), F32)
            for kh in range(KV_HEADS):
                kbd, vbd = _block_diag(kn2, kn2_r, left, kh), _block_diag(v2, v2_r, left, kh)
                qg = qn[kh * PAIR_ROWS:(kh + 1) * PAIR_ROWS]
                dog = dop[kh * PAIR_ROWS:(kh + 1) * PAIR_ROWS]
                st = _dot(kbd, qg, NT) * (HEAD_DIM ** -0.5) + bias_ref[kh]
                halves = _pair_softmax(st, sk_ref, kh)
                dpt = _dot(vbd, dog, NT)
                ds_halves, t_halves = [], []
                for e, (w_e, w_sink) in enumerate(halves):
                    dp_e = dpt[e * BAND:(e + 1) * BAND]
                    delta = jnp.sum(w_e * dp_e, axis=0, keepdims=True)
                    ds_halves.append(w_e * (dp_e - delta))
                    t_halves.append(-(w_sink * delta))
                dst = jnp.concatenate(ds_halves, axis=0)
                dst_ref[kh] += dst
                skacc_ref[2 * kh:2 * kh + 2, :] += jnp.concatenate(t_halves, axis=0)
                ds_b = (dst * (HEAD_DIM ** -0.5)).astype(BF16)
                w_b = jnp.concatenate([w_e for w_e, _ in halves], axis=0).astype(BF16)
                dqn_parts.append(_dot(ds_b, kbd, TN))
                dk2n += _from_block_diag(_dot(ds_b, qg), left, kh)
                dv2 += _from_block_diag(_dot(w_b, dog), left, kh)
            dqn = jnp.concatenate(dqn_parts, axis=0)
            gqacc_ref[...] += jnp.sum(dqn * qhat, axis=0, keepdims=True)
            t = dqn * gq2
            dzq = (rq * (t - qhat * _seg_mean(t * qhat, e_ref))).astype(BF16)
            for p in range(PAIRS):
                dz_ref[:, Q_OFF + p * LANES:Q_OFF + (p + 1) * LANES] = rows(dzq, p)
            tot_ref[0] = carry_ref[0] + dk2n[:BLOCK]
            tot_ref[1] = carry_ref[1] + dv2[:BLOCK]
            carry_ref[0] = dk2n[BLOCK:]
            carry_ref[1] = dv2[BLOCK:]

        @pl.when(n == nb)
        def _():
            tot_ref[...] = carry_ref[...]

        kp = zp_ref[:, :KV_WIDTH]
        rk = lax.rsqrt(_seg_mean(kp * kp, e_ref) + EPS)
        khat = kp * rk
        dkn = tot_ref[0]
        gkacc_ref[...] += jnp.sum(dkn * khat, axis=0, keepdims=True)
        t = dkn * gk2_ref[...]
        dzkv_ref[:, :KV_WIDTH] = (rk * (t - khat * _seg_mean(t * khat, e_ref))).astype(BF16)
        dzkv_ref[:, KV_WIDTH:] = tot_ref[1].astype(BF16)

        @pl.when(n == nb)
        def _():
            dsb_ref[...] = jnp.broadcast_to(jnp.sum(sbacc_ref[...], axis=1, keepdims=True), dsb_ref.shape)
            dsk_ref[...] = lax.dot_general(skacc_ref[...], fold_ref[...], NN, precision=lax.Precision.HIGHEST,
                                           preferred_element_type=F32)
            dgq_ref[...] = gqacc_ref[...] + pltpu.roll(gqacc_ref[...], HEAD_DIM, 1)
            dgk_ref[...] = gkacc_ref[...] + pltpu.roll(gkacc_ref[...], HEAD_DIM, 1)

    last = nb - 1
    full, z_spec, zp_spec, consts = _mixer_specs(nb, last)
    return _pallas(
        body,
        name=name,
        grid=(nb + 1,),
        ex=ex,
        in_specs=[z_spec, zp_spec, pl.BlockSpec((BLOCK, A_WIDTH + B_WIDTH), lambda n: (jnp.minimum(n, last), 0)),
                  full((A_HEADS, BLOCK, BLOCK)), full((A_HEADS, BLOCK, BLOCK))] + consts + [full((PAIR_ROWS, LANES))],
        out_specs=[
            pl.BlockSpec((BLOCK, K_OFF), lambda n: (jnp.minimum(n, last), 0)),
            pl.BlockSpec((BLOCK, 2 * KV_WIDTH), lambda n: (jnp.maximum(n - 1, 0), 0)),
            full((A_HEADS, BLOCK, BLOCK)), full((A_HEADS * BLOCK, A_DIM)), full((A_HEADS, A_DIM)),
            full((1, LANES)), full((1, LANES)), full((SUBLANES, LANES)),
            full((KV_HEADS, PAIR_ROWS, 2 * BAND)),
        ],
        out_shape=[
            jax.ShapeDtypeStruct((S, K_OFF), BF16),
            jax.ShapeDtypeStruct((S, 2 * KV_WIDTH), BF16),
            jax.ShapeDtypeStruct((A_HEADS, BLOCK, BLOCK), F32),
            jax.ShapeDtypeStruct((A_HEADS * BLOCK, A_DIM), F32),
            jax.ShapeDtypeStruct((A_HEADS, A_DIM), F32),
            jax.ShapeDtypeStruct((1, LANES), F32),
            jax.ShapeDtypeStruct((1, LANES), F32),
            jax.ShapeDtypeStruct((SUBLANES, LANES), F32),
            jax.ShapeDtypeStruct((KV_HEADS, PAIR_ROWS, 2 * BAND), F32),
        ],
        scratch_shapes=[
            pltpu.VMEM((2, BLOCK, KV_WIDTH), F32),
            pltpu.VMEM((2, BLOCK, KV_WIDTH), F32),
            pltpu.VMEM((A_HEADS * BLOCK, A_DIM), F32),
            pltpu.VMEM((SUBLANES, PAIR_ROWS), F32),
            pltpu.VMEM((1, LANES), F32),
            pltpu.VMEM((1, LANES), F32),
        ],
        semantics=("arbitrary",),
        args=[z, z, dmix, wm, wm_t, sbp, gvp, gq2, gk2, seg_ones, lane_ones, sinks, biasp, pair_fold],
    )


def _mixer_out(mix, w_out, x, gt, name):
    S, D = x.shape
    K = mix.shape[1]
    tm, tn = _tile(S, 1024), _tile(D, 1024)

    def body(m_ref, w_ref, x_ref, gt_ref, xo_ref, y_ref):
        y = _dot(m_ref[...], w_ref[...])
        y_ref[...] = y.astype(BF16)
        xo_ref[...] = x_ref[...] + gt_ref[...] * y

    blk = pl.BlockSpec((tm, tn), lambda j, i: (i, j))
    return pl.pallas_call(
        body,
        name=name,
        grid=(D // tn, S // tm),
        in_specs=[pl.BlockSpec((tm, K), lambda j, i: (i, 0)), pl.BlockSpec((K, tn), lambda j, i: (0, j)),
                  blk, pl.BlockSpec((1, tn), lambda j, i: (0, j))],
        out_specs=[blk, blk],
        out_shape=[jax.ShapeDtypeStruct((S, D), F32), jax.ShapeDtypeStruct((S, D), BF16)],
        compiler_params=_params(("parallel", "parallel")),
    )(mix, w_out, x, gt)


def _bucket_sum(dst, onehot, name):
    def body(d_ref, o_ref, out_ref):
        out_ref[...] = lax.dot_general(o_ref[...], d_ref[...], NT, precision=lax.Precision.HIGHEST,
                                       preferred_element_type=F32)

    return pl.pallas_call(
        body,
        name=name,
        out_shape=jax.ShapeDtypeStruct((N_BUCKETS, B_HEADS), F32),
    )(dst, onehot)


def _adamw_math(w, g, m, v):
    m = ADAM_B1 * m + (1.0 - ADAM_B1) * g
    v = ADAM_B2 * v + (1.0 - ADAM_B2) * (g * g)
    m_hat = m / (1.0 - ADAM_B1 ** ADAM_STEP)
    v_hat = v / (1.0 - ADAM_B2 ** ADAM_STEP)
    delta = -ADAM_LR * (m_hat / (jnp.sqrt(v_hat) + ADAM_EPS) + ADAM_WD * w)
    return delta, m, v


def _adamw(w, g, m, v, name, emit_grad=False, after=()):
    R, C = w.shape
    tr = _tile(R, max(SUBLANES, (1 << 19) // C), SUBLANES)

    def body(w_ref, g_ref, m_ref, v_ref, *out_refs):
        gv = g_ref[...]
        results = _adamw_math(w_ref[...], gv, m_ref[...], v_ref[...])
        for ref, val in zip(out_refs, ((gv,) if emit_grad else ()) + results):
            ref[...] = val

    blk = pl.BlockSpec((tr, C), lambda i: (i, 0))
    shape = jax.ShapeDtypeStruct((R, C), F32)
    n_out = 4 if emit_grad else 3
    return _pallas(body, name=name, grid=(R // tr,), in_specs=[blk] * 4, out_specs=[blk] * n_out,
                   out_shape=[shape] * n_out, semantics=("parallel",), args=[w, g, m, v], after=after)


def _small_update(parts, w, m, v, name):
    R = w.shape[0]

    def body(p_ref, w_ref, m_ref, v_ref, g_ref, d_ref, mo_ref, vo_ref):
        g = p_ref[0]
        for dev in range(1, N_DEV):
            g = g + p_ref[dev]
        g_ref[...] = g
        d, mn, vn = _adamw_math(w_ref[...], g, m_ref[...], v_ref[...])
        d_ref[...] = d
        mo_ref[...] = mn
        vo_ref[...] = vn

    shape = jax.ShapeDtypeStruct((R, LANES), F32)
    return pl.pallas_call(
        body,
        name=name,
        out_shape=[shape] * 4,
        compiler_params=pltpu.CompilerParams(vmem_limit_bytes=VMEM_LIMIT),
    )(parts, w, m, v)


def _place():
    x, y, c = lax.axis_index("x"), lax.axis_index("y"), lax.axis_index("c")
    chips = [(1 - x, y), (x, 1 - y), (1 - x, 1 - y)]
    return x, y, c, chips


def _remote(src, dst, send_sem, recv_sem, to):
    return pltpu.make_async_remote_copy(src_ref=src, dst_ref=dst, send_sem=send_sem, recv_sem=recv_sem,
                                        device_id=to, device_id_type=MESH)


def _allgather_small(block, name):
    m_per, n = block.shape

    def body(x_ref, out_ref, send_sems, recv_sems, local_sem):
        x, y, c, chips = _place()
        me, sibling = (x, y, c), (x, y, 1 - c)

        def rows(px, py, pc):
            return out_ref.at[pl.ds((4 * px + 2 * py + pc) * m_per, m_per), :]

        def copy(k, blk, to, src=None):
            return _remote(rows(*blk) if src is None else src, rows(*blk), send_sems.at[k], recv_sems.at[k], to)

        mine = pltpu.make_async_copy(x_ref, rows(*me), local_sem)
        mine.start()
        first = [copy(0, me, sibling, src=x_ref)]
        first += [copy(1 + j, me, (*chip, c), src=x_ref) for j, chip in enumerate(chips)]
        for cp in first:
            cp.start()
        passed = [copy(4 + j, (*chip, c), sibling) for j, chip in enumerate(chips)]
        for j, chip in enumerate(chips):
            copy(1 + j, (*chip, c), me).wait_recv()
            passed[j].start()
        copy(0, sibling, me).wait_recv()
        for j, chip in enumerate(chips):
            copy(4 + j, (*chip, 1 - c), me).wait_recv()
        for cp in first + passed:
            cp.wait_send()
        mine.wait()

    return pl.pallas_call(
        body,
        name=name,
        out_shape=jax.ShapeDtypeStruct((N_DEV * m_per, n), block.dtype),
        in_specs=[pl.BlockSpec(memory_space=pltpu.VMEM)],
        out_specs=pl.BlockSpec(memory_space=pltpu.VMEM),
        scratch_shapes=[pltpu.SemaphoreType.DMA((7,)), pltpu.SemaphoreType.DMA((7,)), pltpu.SemaphoreType.DMA],
        compiler_params=pltpu.CompilerParams(vmem_limit_bytes=VMEM_LIMIT),
    )(block)


def _half(ref, c, rows):
    start = pl.multiple_of(c * rows, BF16_ROWS)
    if len(ref.shape) == 2:
        return ref.at[pl.ds(start, rows), :]
    return ref.at[:, pl.ds(start, rows), :]


def _same(arrays):
    return [jax.ShapeDtypeStruct(a.shape, a.dtype) for a in arrays], {t: t for t in range(len(arrays))}


def _ex_gather_ici(bufs):
    def plan(ins, outs, send_sems, recv_sems):
        x, y, c, chips = _place()
        sends, arrivals = [], []
        for t, buf in enumerate(bufs):
            rows = buf.shape[1] // 2
            mine = _half(outs[t].at[2 * x + y], c, rows)
            for k, (px, py) in enumerate(chips):
                sems = (send_sems.at[3 * t + k], recv_sems.at[3 * t + k], (px, py, c))
                landed = _half(outs[t].at[2 * px + py], c, rows)
                sends.append((mine, mine, *sems))
                arrivals.append((landed, landed, *sems))
        return sends, arrivals

    shapes, aliases = _same(bufs)
    return _Exchange(bufs, shapes, aliases, 3 * len(bufs), plan)


def _ex_gather_d2d(bufs):
    def plan(ins, outs, send_sems, recv_sems):
        x, y, c, chips = _place()
        sends, arrivals = [], []
        for t, buf in enumerate(bufs):
            rows = buf.shape[1] // 2
            for k, (px, py) in enumerate(chips):
                sems = (send_sems.at[3 * t + k], recv_sems.at[3 * t + k], (x, y, 1 - c))
                landed = _half(outs[t].at[2 * px + py], c, rows)
                other = _half(outs[t].at[2 * px + py], 1 - c, rows)
                sends.append((landed, landed, *sems))
                arrivals.append((other, other, *sems))
        return sends, arrivals

    shapes, aliases = _same(bufs)
    return _Exchange(bufs, shapes, aliases, 3 * len(bufs), plan)


def _ex_swap_halves(grads):
    def plan(ins, outs, send_sems, recv_sems):
        x, y, c, _ = _place()
        sends = [(_half(ins[t], 1 - c, g.shape[1] // 2), outs[t], send_sems.at[t], recv_sems.at[t], (x, y, 1 - c))
                 for t, g in enumerate(grads)]
        return sends, sends

    shapes = [jax.ShapeDtypeStruct((g.shape[0], g.shape[1] // 2, g.shape[2]), g.dtype) for g in grads]
    return _Exchange(grads, shapes, {}, len(grads), plan)


def _ex_scatter(sums):
    def plan(ins, outs, send_sems, recv_sems):
        x, y, c, chips = _place()
        sends = [(ins[t].at[2 * px + py], outs[t].at[k], send_sems.at[3 * t + k], recv_sems.at[3 * t + k], (px, py, c))
                 for t in range(len(sums)) for k, (px, py) in enumerate(chips)]
        return sends, sends

    shapes = [jax.ShapeDtypeStruct((N_CHIPS - 1,) + s.shape[1:], s.dtype) for s in sums]
    return _Exchange(sums, shapes, {}, 3 * len(sums), plan)


def _ex_join_halves(fulls):
    def plan(ins, outs, send_sems, recv_sems):
        x, y, c, _ = _place()
        sends, arrivals = [], []
        for t, full in enumerate(fulls):
            rows = full.shape[0] // 2
            sems = (send_sems.at[t], recv_sems.at[t], (x, y, 1 - c))
            mine, other = _half(outs[t], c, rows), _half(outs[t], 1 - c, rows)
            sends.append((mine, mine, *sems))
            arrivals.append((other, other, *sems))
        return sends, arrivals

    shapes, aliases = _same(fulls)
    return _Exchange(fulls, shapes, aliases, len(fulls), plan)


class _Shifted:
    def __init__(self, sems, offset):
        self.sems, self.offset = sems, offset

    @property
    def at(self):
        return self

    def __getitem__(self, k):
        return self.sems.at[self.offset + k]


def _combine(exchanges):
    operands, out_shapes, aliases, starts = [], [], {}, []
    n_sems = 0
    for e in exchanges:
        starts.append((len(operands), len(out_shapes), n_sems))
        aliases.update({len(operands) + i: len(out_shapes) + o for i, o in e.aliases.items()})
        operands += list(e.operands)
        out_shapes += list(e.out_shapes)
        n_sems += e.n_sems

    def plan(ins, outs, send_sems, recv_sems):
        sends, arrivals = [], []
        for e, (i0, o0, s0) in zip(exchanges, starts):
            s, a = e.plan(ins[i0:i0 + len(e.operands)], outs[o0:o0 + len(e.out_shapes)],
                          _Shifted(send_sems, s0), _Shifted(recv_sems, s0))
            sends += s
            arrivals += a
        return sends, arrivals

    return _Exchange(operands, out_shapes, aliases, n_sems, plan)


class _Reduction:
    def __init__(self, grad, tag, c_arr, jc_arr):
        self.grad, self.tag, self.c_arr, self.jc_arr, self.stage = grad, tag, c_arr, jc_arr, 0

    def exchange(self):
        if self.stage == 0:
            return _ex_swap_halves([self.grad])
        if self.stage == 1:
            return _ex_scatter([self.sums])
        return _ex_join_halves([self.full])

    def advance(self, landed):
        if self.stage == 0:
            self.recv = landed
            self.sums = _chip_sum(self.grad, landed, self.c_arr, f"chip_sum_{self.tag}")
        elif self.stage == 1:
            self.full = _owner_sum(self.grad, self.recv, landed, self.jc_arr, f"owner_sum_{self.tag}")
        else:
            self.result = landed
        self.stage += 1


def _ride(reductions):
    def done(carried):
        for r, landed in zip(reductions, carried):
            r.advance(landed)

    return _combine([r.exchange() for r in reductions]), done


def _exchange_alone(ex, name):
    return _pallas(None, name=name, grid=(), in_specs=[], out_specs=[], out_shape=[], args=[], ex=ex)


SEM = pl.BlockSpec(memory_space=pltpu.SEMAPHORE)
DATAFLOW = pltpu.SideEffectType.DATAFLOW_SIDE_EFFECTING


def _exchange_start(ex, name):
    e_in, e_out = len(ex.operands), len(ex.out_shapes)
    kept = [i for i in range(e_in) if i not in ex.aliases]

    def body(*refs):
        ins, refs = refs[:e_in], refs[e_in:]
        outs, refs = refs[:e_out], refs[e_out:]
        _, (send_sems, recv_sems, token) = refs[:len(kept)], refs[len(kept):]
        sends, _ = ex.plan(ins, outs, send_sems, recv_sems)
        for cp in sends:
            _remote(*cp).start()
        token[...] = jnp.zeros(token.shape, F32)

    sems = pltpu.SemaphoreType.DMA((ex.n_sems,))
    aliases = dict(ex.aliases)
    aliases.update({i: e_out + k for k, i in enumerate(kept)})
    res = pl.pallas_call(
        body,
        name=name,
        in_specs=[ANY] * e_in,
        out_specs=[ANY] * (e_out + len(kept)) + [SEM, SEM, pl.BlockSpec(memory_space=pltpu.VMEM)],
        out_shape=list(ex.out_shapes) + [jax.ShapeDtypeStruct(ex.operands[i].shape, ex.operands[i].dtype) for i in kept]
        + [sems, sems, jax.ShapeDtypeStruct((SUBLANES, LANES), F32)],
        input_output_aliases=aliases,
        compiler_params=pltpu.CompilerParams(has_side_effects=DATAFLOW),
    )(*ex.operands)
    outs, kept_thru, (send_sems, recv_sems, token) = res[:e_out], res[e_out:e_out + len(kept)], res[e_out + len(kept):]
    operands = list(ex.operands)
    for i, o in ex.aliases.items():
        operands[i] = outs[o]
    for k, i in enumerate(kept):
        operands[i] = kept_thru[k]
    return (operands, outs, send_sems, recv_sems), token


def _exchange_wait(ex, state, after, name):
    operands, outs, send_sems, recv_sems = state
    e_out = len(outs)
    kept = [i for i in range(len(operands)) if i not in ex.aliases]

    def body(*refs):
        sources, refs = refs[:len(kept)], refs[len(kept):]
        landing, refs = refs[:e_out], refs[e_out:]
        ins = [None] * len(operands)
        for k, i in enumerate(kept):
            ins[i] = sources[k]
        sends, arrivals = ex.plan(ins, landing, refs[0], refs[1])
        for cp in arrivals:
            _remote(*cp).wait_recv()
        for cp in sends:
            _remote(*cp).wait_send()

    return pl.pallas_call(
        body,
        name=name,
        in_specs=[ANY] * (len(kept) + e_out) + [SEM, SEM] + [ANY] * len(after),
        out_specs=[ANY] * e_out,
        out_shape=[jax.ShapeDtypeStruct(o.shape, o.dtype) for o in outs],
        input_output_aliases={len(kept) + o: o for o in range(e_out)},
        compiler_params=pltpu.CompilerParams(has_side_effects=DATAFLOW),
    )(*[operands[i] for i in kept], *outs, send_sems, recv_sems, *after)


def _cast_to_slot(w, chip_arr, name, after=()):
    A, B = w.shape
    ta = _tile(A, max(BF16_ROWS, (1 << 19) // B), BF16_ROWS)

    def body(j_ref, w_ref, *rest):
        rest[-1][...] = w_ref[...].astype(BF16)

    return pl.pallas_call(
        body,
        name=name,
        grid_spec=pltpu.PrefetchScalarGridSpec(
            num_scalar_prefetch=1,
            grid=(A // ta,),
            in_specs=[pl.BlockSpec((ta, B), lambda i, j_ref: (i, 0))] + [ANY] * len(after),
            out_specs=pl.BlockSpec((None, ta, B), lambda i, j_ref: (j_ref[0], i, 0)),
        ),
        out_shape=jax.ShapeDtypeStruct((N_CHIPS, A, B), BF16),
        compiler_params=_params(("parallel",)),
    )(chip_arr, w, *after)


def _chip_sum(grad, recv, c_arr, name):
    _, A, B = grad.shape
    hA = A // 2
    ta = _tile(hA, max(BF16_ROWS, (1 << 19) // B), BF16_ROWS)
    nh = hA // ta

    def body(c_ref, g_ref, r_ref, o_ref):
        o_ref[...] = (g_ref[...] + r_ref[...]).astype(BF16)

    return pl.pallas_call(
        body,
        name=name,
        grid_spec=pltpu.PrefetchScalarGridSpec(
            num_scalar_prefetch=1,
            grid=(N_CHIPS, nh),
            in_specs=[pl.BlockSpec((None, ta, B), lambda s, i, c_ref: (s, c_ref[0] * nh + i, 0)),
                      pl.BlockSpec((None, ta, B), lambda s, i, c_ref: (s, i, 0))],
            out_specs=pl.BlockSpec((None, ta, B), lambda s, i, c_ref: (s, i, 0)),
        ),
        out_shape=jax.ShapeDtypeStruct((N_CHIPS, hA, B), BF16),
        compiler_params=_params(("parallel", "parallel")),
    )(c_arr, grad, recv)


def _owner_sum(grad, recv, landed, jc_arr, name):
    _, A, B = grad.shape
    hA = A // 2
    ta = _tile(hA, max(BF16_ROWS, (1 << 19) // B), BF16_ROWS)
    nh = hA // ta

    def body(jc_ref, g_ref, r_ref, l0_ref, l1_ref, l2_ref, o_ref):
        total = g_ref[...] + r_ref[...]
        for ref in (l0_ref, l1_ref, l2_ref):
            total = total + ref[...].astype(F32)
        o_ref[...] = total

    def landed_spec(k):
        return pl.BlockSpec((None, ta, B), lambda i, jc_ref: (k, i, 0))

    return pl.pallas_call(
        body,
        name=name,
        grid_spec=pltpu.PrefetchScalarGridSpec(
            num_scalar_prefetch=1,
            grid=(nh,),
            in_specs=[pl.BlockSpec((None, ta, B), lambda i, jc_ref: (jc_ref[0], jc_ref[1] * nh + i, 0)),
                      pl.BlockSpec((None, ta, B), lambda i, jc_ref: (jc_ref[0], i, 0)),
                      landed_spec(0), landed_spec(1), landed_spec(2)],
            out_specs=pl.BlockSpec((ta, B), lambda i, jc_ref: (jc_ref[1] * nh + i, 0)),
        ),
        out_shape=jax.ShapeDtypeStruct((A, B), F32),
        compiler_params=_params(("parallel",)),
    )(jc_arr, grad, recv, landed, landed, landed)


def _pack(parts):
    rows = []
    for p in parts:
        flat = jnp.reshape(p.astype(F32), (-1,))
        tile = SUBLANES * LANES
        padded = -(-flat.shape[0] // tile) * tile
        rows.append(jnp.reshape(jnp.pad(flat, (0, padded - flat.shape[0])), (-1, LANES)))
    return jnp.concatenate(rows, axis=0)


def _unpack(pack, shapes):
    out, row = [], 0
    for shape in shapes:
        size = int(np.prod(shape))
        nrows = -(-size // (SUBLANES * LANES)) * SUBLANES
        out.append(jnp.reshape(jnp.reshape(pack[row:row + nrows], (-1,))[:size], shape))
        row += nrows
    return out


def _bias_tables():
    qi = np.arange(BLOCK)[:, None]
    kj = np.arange(2 * BLOCK)[None, :]
    dist = qi + BLOCK - kj
    in_window = (dist >= 0) & (dist < BLOCK)
    n = np.clip(dist, 0, None)
    max_exact = N_BUCKETS // 2
    nf = np.maximum(n, 1).astype(np.float32)
    large = max_exact + (np.log(nf / max_exact) / math.log(MAX_DISTANCE / max_exact)
                         * (N_BUCKETS - max_exact)).astype(np.int32)
    large = np.minimum(large, N_BUCKETS - 1)
    bucket = np.where(n < max_exact, n, large)
    onehot = (bucket[None] == np.arange(N_BUCKETS)[:, None, None]) & in_window[None]
    first = in_window & (kj >= BLOCK)
    return onehot.astype(np.float32), in_window, first


def kernel(x, c, w_ada, b_ada, g_ffn1, w1_ffn1, w3_ffn1, w2_ffn1, g_mix, w_in, spatial_w, spatial_b, g_v, g_q, g_k, sinks, rel_bias, w_out, g_ffn2, w1_ffn2, w3_ffn2, w2_ffn2, loss_target, m_w_ada, m_b_ada, m_g_ffn1, m_w1_ffn1, m_w3_ffn1, m_w2_ffn1, m_g_mix, m_w_in, m_spatial_w, m_spatial_b, m_g_v, m_g_q, m_g_k, m_sinks, m_rel_bias, m_w_out, m_g_ffn2, m_w1_ffn2, m_w3_ffn2, m_w2_ffn2, v_w_ada, v_b_ada, v_g_ffn1, v_w1_ffn1, v_w3_ffn1, v_w2_ffn1, v_g_mix, v_w_in, v_spatial_w, v_spatial_b, v_g_v, v_g_q, v_g_k, v_sinks, v_rel_bias, v_w_out, v_g_ffn2, v_w1_ffn2, v_w3_ffn2, v_w2_ffn2):
    ax, ay, ac = lax.axis_index("x"), lax.axis_index("y"), lax.axis_index("c")
    chip = 2 * ax + ay
    dev = 2 * chip + ac
    xs = x[0]
    tgt = loss_target[0]
    S, D = xs.shape
    F = N_CHIPS * w1_ffn1.shape[2]
    mod_cols = w_ada.shape[2]

    chip_arr = jnp.reshape(chip, (1,)).astype(jnp.int32)
    c_arr = jnp.reshape(ac, (1,)).astype(jnp.int32)
    jc_arr = jnp.stack([chip, ac]).astype(jnp.int32)
    cast = lambda w, nm: _cast_to_slot(w[0], chip_arr, f"cast_{nm}")
    ffn1_bufs = [cast(w1_ffn1, "w1_ffn1"), cast(w3_ffn1, "w3_ffn1"), cast(w2_ffn1, "w2_ffn1")]
    ffn1_gather = _ex_gather_ici(ffn1_bufs)
    ffn1_state, started = _exchange_start(ffn1_gather, "gather_ffn1_ici_start")

    c_all = _allgather_small(jnp.pad(c + started[:1, :1], ((0, SUBLANES - 1), (0, 0))), "gather_c")
    c_all = jnp.pad(c_all[::SUBLANES], ((0, BF16_ROWS - N_DEV), (0, 0)))
    b_sh = lax.dynamic_slice(b_ada, (0, chip * mod_cols), (1, mod_cols))
    mod_part, c_act = _mod_partial(c_all, w_ada[0], b_sh, "mod_partial")
    mod_all = _allgather_small(mod_part[:N_DEV], "gather_mod")
    mod_all = jnp.reshape(mod_all, (N_CHIPS, 2, N_DEV, mod_cols))[:, 0]
    mod = jnp.reshape(lax.dynamic_index_in_dim(mod_all, dev, axis=1, keepdims=False), (1, N_MOD * D))
    sh1, sc1, gt1, sh2, sc2, gt2, sh3, sc3, gt3 = [mod[:, i * D:(i + 1) * D] for i in range(N_MOD)]

    def cols_to_natural(w4):
        return jnp.reshape(jnp.transpose(w4, (1, 0, 2)), (w4.shape[1], -1))

    cast = lambda w, nm: _cast_to_slot(w[0], chip_arr, f"cast_{nm}", after=[started])
    mixer_bufs = [cast(w_in, "w_in"), cast(w_out, "w_out")]
    ffn2_bufs = [cast(w1_ffn2, "w1_ffn2"), cast(w3_ffn2, "w3_ffn2"), cast(w2_ffn2, "w2_ffn2")]
    h1 = _norm_mod(xs, g_ffn1, sh1, sc1, "ffn1_norm")
    ffn1_bufs = _exchange_wait(ffn1_gather, ffn1_state, [h1] + mixer_bufs + ffn2_bufs, "gather_ffn1_ici_wait")
    ffn1_bufs = _exchange_alone(_ex_gather_d2d(ffn1_bufs), "gather_ffn1_d2d")
    w1a, w3a, w2a = cols_to_natural(ffn1_bufs[0]), cols_to_natural(ffn1_bufs[1]), jnp.reshape(ffn1_bufs[2], (F, D))

    onehot_np, in_window_np, first_np = _bias_tables()
    onehot = jnp.asarray(onehot_np)
    bias = jnp.einsum("bij,bh->hij", onehot, rel_bias, precision=lax.Precision.HIGHEST)
    biasm = jnp.stack([jnp.where(jnp.asarray(first_np)[None], bias, NEG),
                       jnp.where(jnp.asarray(in_window_np)[None], bias, NEG)])
    causal = jnp.asarray(np.tril(np.ones((BLOCK, BLOCK), dtype=bool)))
    wm = jnp.where(causal[None], spatial_w[0], 0.0).astype(BF16)
    wm_t = jnp.transpose(wm, (0, 2, 1))
    sink_vec = sinks[0]
    per_group = PAIRS // KV_HEADS
    sbp = jnp.broadcast_to(jnp.reshape(spatial_b[0], (A_HEADS * BLOCK, 1)), (A_HEADS * BLOCK, A_DIM))
    gvp = jnp.repeat(g_v[0], BLOCK, axis=0)
    gq2, gk2 = jnp.concatenate([g_q, g_q], axis=1), jnp.concatenate([g_k, g_k], axis=1)
    seg_ones = jnp.asarray(np.kron(np.eye(2, dtype=np.float32), np.ones((HEAD_DIM, HEAD_DIM), np.float32)), BF16)
    lane_ones = jnp.full((LANES, LANES), 1.0 / LANES, BF16)
    pair_fold = jnp.asarray(np.kron(np.eye(per_group, LANES, dtype=np.float32), np.ones((BLOCK, 1), np.float32)))
    biasp = jnp.reshape(jnp.transpose(jnp.reshape(biasm, (2, KV_HEADS, per_group, 2, BLOCK, BAND)), (0, 1, 3, 5, 2, 4)),
                        (2, KV_HEADS, 2 * BAND, PAIR_ROWS))

    res = _ffn_fwd(xs, g_ffn1, sh1, sc1, gt1, w1a, w3a, w2a, None, "ffn1_fwd", ex=_ex_gather_ici(mixer_bufs + ffn2_bufs),
                   h=h1)
    (x1, a1, b1, f1), mixer_bufs, ffn2_bufs = res[:4], res[4:6], res[6:]
    h2, *mixer_bufs = _norm_mod(x1, g_mix, sh2, sc2, "mixer_norm", ex=_ex_gather_d2d(mixer_bufs))
    win, wout = cols_to_natural(mixer_bufs[0]), jnp.reshape(mixer_bufs[1], (-1, D))
    z, *ffn2_bufs = _matmul(h2, win, "nn", F32, 1024, _tile(IN_COLS, 1664), D, "mixer_in", ex=_ex_gather_d2d(ffn2_bufs))
    w1b, w3b, w2b = cols_to_natural(ffn2_bufs[0]), cols_to_natural(ffn2_bufs[1]), jnp.reshape(ffn2_bufs[2], (F, D))
    mix = _mixer_fwd(z, wm, sbp, gvp, gq2, gk2, seg_ones, lane_ones, sink_vec, biasp, "mixer_fwd")
    x2, ymix = _mixer_out(mix, wout, x1, gt2, "mixer_out")
    g3, df3, h3, a3, b3, dgt3, loss_sum = _ffn_fwd(x2, g_ffn2, sh3, sc3, gt3, w1b, w3b, w2b, tgt, "ffn2_fwd_loss")
    loss = lax.psum(loss_sum[0, 0] * (0.5 / D), ("x", "y", "c"))

    tk = _tile(S, 2048)

    def ffn_weight_grads(h, da, db, s, df, tag, riding):
        ex, done = _ride(riding) if riding else (None, None)
        gw1 = _matmul(h, da, "tn", F32, 1024, F // N_CHIPS, tk, f"grad_w1_{tag}", shard_major=True, ex=ex)
        if riding:
            done(gw1[1:])
            gw1 = gw1[0]
        r1 = _Reduction(gw1, f"w1_{tag}", c_arr, jc_arr)
        ex, done = _ride([r1])
        gw3, *carried = _matmul(h, db, "tn", F32, 1024, F // N_CHIPS, tk, f"grad_w3_{tag}", shard_major=True, ex=ex)
        done(carried)
        r3 = _Reduction(gw3, f"w3_{tag}", c_arr, jc_arr)
        ex, done = _ride([r1, r3])
        gw2, *carried = _matmul(s, df, "tn", F32, _tile(F, 1408), 1024, tk, f"grad_w2_{tag}", ex=ex)
        done(carried)
        r2 = _Reduction(jnp.reshape(gw2, (N_CHIPS, F // N_CHIPS, D)), f"w2_{tag}", c_arr, jc_arr)
        return r1, r3, r2

    da3, db3, s3, dh3 = _ffn_bwd(df3, a3, b3, w1b, w3b, w2b, "ffn2_bwd")
    r21, r23, r22 = ffn_weight_grads(h3, da3, db3, s3, df3, "ffn2", [])
    ex, done = _ride([r22])
    res = _norm_bwd(dh3, x2, g3, g_ffn2, sc3, (ymix, gt2, 1.0), "ffn2_norm_bwd", ex=ex)
    g2, dsh3, dsc3, dgn3, dy, dgt2 = res[:6]
    done(res[6:])

    ex, done = _ride([r21])
    dmix, *carried = _matmul(dy, wout, "nt", BF16, 1024, 2048, D, "mixer_out_bwd", ex=ex)
    done(carried)
    ex, done = _ride([r23, r22])
    res = _mixer_bwd(z, dmix, wm, wm_t, sbp, gvp, gq2, gk2, seg_ones, lane_ones, sink_vec, biasp, pair_fold,
                     "mixer_bwd", ex=ex)
    dz_main, dz_kv, dwm, dsb, dgv, dgq, dgk, dsk, dst = res[:9]
    dsb = jnp.reshape(dsb[:, 0], (A_HEADS, BLOCK))
    dgq, dgk = dgq[:, :HEAD_DIM], dgk[:, :HEAD_DIM]
    dsk = jnp.reshape(jnp.transpose(jnp.reshape(dsk[:2 * KV_HEADS, :per_group], (KV_HEADS, 2, per_group)), (0, 2, 1)),
                      (1, B_HEADS))
    dst = jnp.reshape(jnp.transpose(jnp.reshape(dst, (KV_HEADS, 2, BAND, per_group, BLOCK)), (0, 3, 1, 4, 2)),
                      (B_HEADS, BLOCK * BAND))
    done(res[9:])
    dz = jnp.concatenate([dz_main, dz_kv], axis=1)
    ex, done = _ride([r23, r22])
    dh2, *carried = _matmul(dz, win, "nt", F32, 1024, 2048, _tile(IN_COLS, 1664), "mixer_in_bwd", ex=ex)
    done(carried)
    drel = _bucket_sum(dst, jnp.reshape(onehot, (N_BUCKETS, -1)), "bucket_sum")
    g1, dsh2, dsc2, dgn2, df1, dgt1 = _norm_bwd(dh2, x1, g2, g_mix, sc2, (f1, gt1, 0.5), "mixer_norm_bwd")

    da1, db1, s1, dh1 = _ffn_bwd(df1, a1, b1, w1a, w3a, w2a, "ffn1_bwd")
    r11, r13, r12 = ffn_weight_grads(h1, da1, db1, s1, df1, "ffn1", [])
    ex, done = _ride([r11, r13, r12])
    gwin_full, *carried = _matmul(h2, dz, "tn", F32, 1024, _tile(IN_COLS, 1664), tk, "grad_w_in", ex=ex)
    done(carried)
    rm_in = _Reduction(jnp.transpose(jnp.reshape(gwin_full, (D, N_CHIPS, -1)), (1, 0, 2)), "w_in", c_arr, jc_arr)
    ex, done = _ride([r13, r12, rm_in])
    state, started = _exchange_start(ex, "reduce_late_start")
    gwout_full = _matmul(mix, dy, "tn", F32, 1024, 1024, tk, "grad_w_out", after=[started])
    grad_x, dsh1, dsc1, dgn1 = _norm_bwd(dh1, xs, g1, g_ffn1, sc1, None, "ffn1_norm_bwd", after=[started])
    done(_exchange_wait(ex, state, [gwout_full, grad_x], "reduce_late_wait"))
    rm_out = _Reduction(jnp.reshape(gwout_full, (N_CHIPS, -1, D)), "w_out", c_arr, jc_arr)

    dmod = jnp.concatenate([dsh1, dsc1, dgt1, dsh2, dsc2, dgt2, dsh3, dsc3, dgt3], axis=1)
    small_w = [b_ada, g_ffn1, g_mix, g_ffn2, spatial_w, spatial_b, g_v, g_q, g_k, sinks, rel_bias]
    small_m = [m_b_ada, m_g_ffn1, m_g_mix, m_g_ffn2, m_spatial_w, m_spatial_b, m_g_v, m_g_q, m_g_k, m_sinks, m_rel_bias]
    small_v = [v_b_ada, v_g_ffn1, v_g_mix, v_g_ffn2, v_spatial_w, v_spatial_b, v_g_v, v_g_q, v_g_k, v_sinks, v_rel_bias]
    small_g = [dmod, dgn1, dgn2, dgn3, jnp.where(causal[None], dwm, 0.0), dsb, dgv, dgq, dgk, dsk, drel]
    shapes = [w.shape for w in small_w]
    gpack = _pack(small_g)
    rows = gpack.shape[0]
    gall = jnp.reshape(_allgather_small(gpack, "gather_small"), (N_DEV, rows, LANES))
    sg, sd, sm, sv = _small_update(gall, _pack(small_w), _pack(small_m), _pack(small_v), "small_update")
    sg, sd, sm, sv = [_unpack(p, shapes) for p in (sg, sd, sm, sv)]

    mod_rows = -(-N_MOD * D // (SUBLANES * LANES)) * SUBLANES
    dmod_all = jnp.reshape(gall[:, :mod_rows], (N_DEV, -1))[:, :N_MOD * D]
    dmod_sh = lax.dynamic_slice(dmod_all, (0, chip * mod_cols), (N_DEV, mod_cols))
    dmod_sh = jnp.pad(dmod_sh, ((0, BF16_ROWS - N_DEV), (0, 0))).astype(BF16)
    g_wada = _matmul(c_act, dmod_sh, "tn", F32, 1024, _tile(mod_cols, 512), BF16_ROWS, "grad_w_ada")

    big = {}

    def update(nm, w, g, m, v, token):
        g_out, d, nm_, nv_ = _adamw(w[0], g, m[0], v[0], f"adamw_{nm}", emit_grad=True, after=[token])
        big[nm] = (g_out[None], d[None], nm_[None], nv_[None])
        return d

    ex, done = _ride([r12, rm_in, rm_out])
    state, token = _exchange_start(ex, "reduce_tail_0_start")
    behind = [update("w1_ffn2", w1_ffn2, r21.result, m_w1_ffn2, v_w1_ffn2, token),
              update("w3_ffn2", w3_ffn2, r23.result, m_w3_ffn2, v_w3_ffn2, token),
              update("w2_ffn2", w2_ffn2, r22.result, m_w2_ffn2, v_w2_ffn2, token)]
    done(_exchange_wait(ex, state, behind, "reduce_tail_0_wait"))
    ex, done = _ride([rm_in, rm_out])
    state, token = _exchange_start(ex, "reduce_tail_1_start")
    behind = [update("w1_ffn1", w1_ffn1, r11.result, m_w1_ffn1, v_w1_ffn1, token),
              update("w3_ffn1", w3_ffn1, r13.result, m_w3_ffn1, v_w3_ffn1, token),
              update("w2_ffn1", w2_ffn1, r12.result, m_w2_ffn1, v_w2_ffn1, token)]
    done(_exchange_wait(ex, state, behind, "reduce_tail_1_wait"))
    ex, done = _ride([rm_out])
    state, token = _exchange_start(ex, "reduce_tail_2_start")
    d_wada, nm_wada, nv_wada = _adamw(w_ada[0], g_wada, m_w_ada[0], v_w_ada[0], "adamw_w_ada", after=[token])
    behind = [d_wada, update("w_in", w_in, rm_in.result, m_w_in, v_w_in, token)]
    done(_exchange_wait(ex, state, behind, "reduce_tail_2_wait"))
    update("w_out", w_out, rm_out.result, m_w_out, v_w_out, token)
    big["w_ada"] = (g_wada[None], d_wada[None], nm_wada[None], nv_wada[None])

    order = ["w_ada", "b_ada", "g_ffn1", "w1_ffn1", "w3_ffn1", "w2_ffn1", "g_mix", "w_in", "spatial_w", "spatial_b",
             "g_v", "g_q", "g_k", "sinks", "rel_bias", "w_out", "g_ffn2", "w1_ffn2", "w3_ffn2", "w2_ffn2"]
    small_names = ["b_ada", "g_ffn1", "g_mix", "g_ffn2", "spatial_w", "spatial_b", "g_v", "g_q", "g_k", "sinks", "rel_bias"]
    for i, nm in enumerate(small_names):
        big[nm] = (sg[i], sd[i], sm[i], sv[i])
    outs = [loss, grad_x[None]]
    for kind in range(4):
        outs += [big[nm][kind] for nm in order]
    return tuple(outs)
```

```python
import functools
import math

import jax
import jax.numpy as jnp
import numpy as np
from jax import lax
from jax.experimental import pallas as pl
from jax.experimental.pallas import tpu as pltpu

F32 = jnp.float32
BF16 = jnp.bfloat16
MESH = pl.DeviceIdType.MESH
ANY = pl.BlockSpec(memory_space=pl.ANY)

EPS = 1e-6
BLOCK = 128
A_HEADS = 8
A_DIM = 128
A_WIDTH = A_HEADS * A_DIM
B_HEADS = 16
KV_HEADS = 2
GROUP = B_HEADS // KV_HEADS
HEAD_DIM = 64
B_WIDTH = B_HEADS * HEAD_DIM
KV_WIDTH = KV_HEADS * HEAD_DIM
Q_OFF = 2 * A_WIDTH
K_OFF = Q_OFF + B_WIDTH
V_OFF = K_OFF + KV_WIDTH
IN_COLS = V_OFF + KV_WIDTH
N_BUCKETS = 32
MAX_DISTANCE = 128
N_MOD = 9
N_CHIPS = 4
N_DEV = 8
NEG = -1e30

ADAM_LR = 0.001
ADAM_B1 = 0.9
ADAM_B2 = 0.999
ADAM_EPS = 1e-08
ADAM_WD = 0.01
ADAM_STEP = 10

LANES = 128
SUBLANES = 8
BF16_ROWS = 16
VMEM_LIMIT = 60 * 1024 * 1024

INV_SQRT2 = 1.0 / math.sqrt(2.0)
INV_SQRT_2PI = 1.0 / math.sqrt(2.0 * math.pi)


def _tile(n, pref, mult=LANES):
    t = (min(pref, n) // mult) * mult
    while t >= mult:
        if n % t == 0:
            return t
        t -= mult
    return n


def _params(sem):
    return pltpu.CompilerParams(dimension_semantics=sem, vmem_limit_bytes=VMEM_LIMIT)


class _Exchange:
    def __init__(self, operands, out_shapes, aliases, n_sems, plan):
        self.operands, self.out_shapes, self.aliases, self.n_sems, self.plan = operands, out_shapes, aliases, n_sems, plan


def _pallas(body, *, name, grid, in_specs, out_specs, out_shape, args, scratch_shapes=(), semantics=None, ex=None,
            after=()):
    if ex is None:
        n_in = len(in_specs)

        def ordered(*refs):
            body(*refs[:n_in], *refs[n_in + len(after):])

        return pl.pallas_call(ordered if after else body, name=name, grid=grid,
                              in_specs=list(in_specs) + [ANY] * len(after), out_specs=out_specs, out_shape=out_shape,
                              scratch_shapes=list(scratch_shapes), compiler_params=_params(semantics))(*args, *after)
    assert not after
    n_in, n_out, n_scr = len(in_specs), len(out_specs), len(scratch_shapes)
    e_in, e_out = len(ex.operands), len(ex.out_shapes)

    def wrapped(*refs):
        ins, refs = refs[:n_in], refs[n_in:]
        ex_ins, refs = refs[:e_in], refs[e_in:]
        outs, refs = refs[:n_out], refs[n_out:]
        ex_outs, refs = refs[:e_out], refs[e_out:]
        scratch, (send_sems, recv_sems) = refs[:n_scr], refs[n_scr:]
        first, last = True, True
        for d, size in enumerate(grid):
            first = jnp.logical_and(first, pl.program_id(d) == 0)
            last = jnp.logical_and(last, pl.program_id(d) == size - 1)

        def start():
            sends, _ = ex.plan(ex_ins, ex_outs, send_sems, recv_sems)
            for cp in sends:
                _remote(*cp).start()

        def finish():
            sends, arrivals = ex.plan(ex_ins, ex_outs, send_sems, recv_sems)
            for cp in arrivals:
                _remote(*cp).wait_recv()
            for cp in sends:
                _remote(*cp).wait_send()

        if grid:
            pl.when(first)(start)
        else:
            start()
        if body is not None:
            body(*ins, *outs, *scratch)
        if grid:
            pl.when(last)(finish)
        else:
            finish()

    kwargs = dict(grid=grid) if grid else {}
    return pl.pallas_call(
        wrapped,
        name=name,
        in_specs=list(in_specs) + [ANY] * e_in,
        out_specs=list(out_specs) + [ANY] * e_out,
        out_shape=list(out_shape) + list(ex.out_shapes),
        input_output_aliases={n_in + i: n_out + o for i, o in ex.aliases.items()},
        scratch_shapes=list(scratch_shapes) + [pltpu.SemaphoreType.DMA((ex.n_sems,)), pltpu.SemaphoreType.DMA((ex.n_sems,))],
        compiler_params=_params(("arbitrary",) * len(grid) if grid else None),
        **kwargs,
    )(*args, *ex.operands)


def _dot(a, b, dims=(((1,), (0,)), ((), ()))):
    return lax.dot_general(a, b, dims, preferred_element_type=F32)


NN = (((1,), (0,)), ((), ()))
NT = (((1,), (1,)), ((), ()))
TN = (((0,), (0,)), ((), ()))


def _sigmoid(x):
    return 1.0 / (1.0 + jnp.exp(-x))


def _gelu_and_grad(x):
    cdf = 0.5 * (1.0 + lax.erf(x * INV_SQRT2))
    pdf = jnp.exp(-0.5 * x * x) * INV_SQRT_2PI
    return x * cdf, cdf + x * pdf


def _gelu(x):
    return x * (0.5 * (1.0 + lax.erf(x * INV_SQRT2)))


def _rms(x):
    r = lax.rsqrt(jnp.mean(x * x, axis=-1, keepdims=True) + EPS)
    return x * r, r


ROW_CHUNK = 64


def _for_rows(tm, fn):
    rc = min(ROW_CHUNK, tm)

    def step(r, carry):
        fn(pl.ds(pl.multiple_of(r * rc, rc), rc))
        return carry

    lax.fori_loop(0, tm // rc, step, 0)


def _rms_bwd(dy, xhat, r):
    return r * (dy - xhat * jnp.mean(dy * xhat, axis=-1, keepdims=True))


def _matmul(a, b, mode, out_dtype, tm, tn, tk, name, shard_major=False, ex=None, after=()):
    if mode == "nn":
        (M, K), N = a.shape, b.shape[1]
    elif mode == "nt":
        (M, K), N = a.shape, b.shape[0]
    else:
        (K, M), N = a.shape, b.shape[1]
    tm, tn, tk = min(tm, M), min(tn, N), min(tk, K)
    assert M % tm == 0 and N % tn == 0 and K % tk == 0, (name, M, N, K, tm, tn, tk)
    nk = K // tk
    dims = {"nn": NN, "nt": NT, "tn": TN}[mode]
    a_spec = pl.BlockSpec((tk, tm), lambda i, j, k: (k, i)) if mode == "tn" else pl.BlockSpec((tm, tk), lambda i, j, k: (i, k))
    b_spec = pl.BlockSpec((tn, tk), lambda i, j, k: (j, k)) if mode == "nt" else pl.BlockSpec((tk, tn), lambda i, j, k: (k, j))
    if shard_major:
        assert tn * N_CHIPS == N
        out_shape = jax.ShapeDtypeStruct((N_CHIPS, M, tn), out_dtype)
        o_spec = pl.BlockSpec((None, tm, tn), lambda i, j, k: (j, i, 0))
    else:
        out_shape = jax.ShapeDtypeStruct((M, N), out_dtype)
        o_spec = pl.BlockSpec((tm, tn), lambda i, j, k: (i, j))

    direct = nk == 1 or out_dtype == F32

    def body(a_ref, b_ref, o_ref, *scratch):
        k = pl.program_id(2)
        if nk == 1:
            o_ref[...] = _dot(a_ref[...], b_ref[...], dims).astype(o_ref.dtype)
            return
        acc_ref = o_ref if direct else scratch[0]

        @pl.when(k == 0)
        def _():
            acc_ref[...] = jnp.zeros(acc_ref.shape, F32)

        acc_ref[...] += _dot(a_ref[...], b_ref[...], dims)
        if not direct:
            @pl.when(k == nk - 1)
            def _():
                o_ref[...] = acc_ref[...].astype(o_ref.dtype)

    outs = _pallas(body, name=name, grid=(M // tm, N // tn, nk), in_specs=[a_spec, b_spec], out_specs=[o_spec],
                   out_shape=[out_shape], scratch_shapes=[] if direct else [pltpu.VMEM((tm, tn), F32)],
                   semantics=("parallel", "parallel", "arbitrary"), args=[a, b], ex=ex, after=after)
    return outs[0] if ex is None else outs


def _mod_partial(c_all, w_ada, b_sh, name):
    R, D = c_all.shape
    N = w_ada.shape[1]
    tn = _tile(N, 512)

    def body(c_ref, w_ref, b_ref, o_ref, ca_ref):
        cv = c_ref[...]
        ca = (cv * _sigmoid(cv)).astype(BF16)
        ca_ref[...] = ca
        o_ref[...] = _dot(ca, w_ref[...].astype(BF16)) + b_ref[...]

    return pl.pallas_call(
        body,
        name=name,
        grid=(N // tn,),
        in_specs=[
            pl.BlockSpec((R, D), lambda j: (0, 0)),
            pl.BlockSpec((D, tn), lambda j: (0, j)),
            pl.BlockSpec((1, tn), lambda j: (0, j)),
        ],
        out_specs=[pl.BlockSpec((R, tn), lambda j: (0, j)), pl.BlockSpec((R, D), lambda j: (0, 0))],
        out_shape=[jax.ShapeDtypeStruct((R, N), F32), jax.ShapeDtypeStruct((R, D), BF16)],
        compiler_params=_params(("arbitrary",)),
    )(c_all, w_ada, b_sh)


FFN_BLOCK = 1024


def _ffn_blocks(F):
    if F % FFN_BLOCK == 0 or F < FFN_BLOCK:
        tf = _tile(F, FFN_BLOCK)
        return tf, F // tf, tf
    nj = -(-F // FFN_BLOCK)
    tail = F - (nj - 1) * FFN_BLOCK
    assert tail % LANES == 0
    return FFN_BLOCK, nj, tail
def _ffn_fwd(x, g, sh, sc, gt, w1, w3, w2, tgt, name, ex=None, h=None):
    S, D = x.shape
    F = w1.shape[1]
    tm, tf, nj, tail = _tile(S, 512), *_ffn_blocks(F)
    ni = S // tm
    with_loss = tgt is not None
    assert h is None or not with_loss

    def body(*refs):
        if with_loss:
            (x_ref, g_ref, sh_ref, sc_ref, gt_ref, w1_ref, w3_ref, w2_ref, tgt_ref,
             gout_ref, df_ref, h_ref, a_ref, b_ref, dgt_ref, loss_ref, acc_ref) = refs
        elif h is None:
            (x_ref, g_ref, sh_ref, sc_ref, gt_ref, w1_ref, w3_ref, w2_ref,
             xo_ref, h_ref, a_ref, b_ref, f_ref, acc_ref) = refs
        else:
            (x_ref, g_ref, sh_ref, sc_ref, gt_ref, w1_ref, w3_ref, w2_ref, h_ref,
             xo_ref, a_ref, b_ref, f_ref, acc_ref) = refs
        i, j = pl.program_id(0), pl.program_id(1)

        if h is None:
            @pl.when(j == 0)
            def _():
                def prologue(rows):
                    xhat, _ = _rms(x_ref[rows, :])
                    h_ref[rows, :] = ((xhat * g_ref[...]) * (1.0 + sc_ref[...]) + sh_ref[...]).astype(BF16)

                _for_rows(tm, prologue)

        @pl.when(j == 0)
        def _():
            acc_ref[...] = jnp.zeros(acc_ref.shape, F32)

        def columns(width):
            def run():
                hb = h_ref[...]
                av = _dot(hb, w1_ref[:, :width])
                bv = _dot(hb, w3_ref[:, :width])
                a_ref[:, :width] = av.astype(BF16)
                b_ref[:, :width] = bv.astype(BF16)
                sv = ((av * _sigmoid(av)) * bv).astype(BF16)
                acc_ref[...] += _dot(sv, w2_ref[:width, :])
            return run

        if tail == tf:
            columns(tf)()
        else:
            pl.when(j < nj - 1)(columns(tf))
            pl.when(j == nj - 1)(columns(tail))

        @pl.when(j == nj - 1)
        def _():
            if with_loss:
                @pl.when(i == 0)
                def _():
                    dgt_ref[...] = jnp.zeros(dgt_ref.shape, F32)
                    loss_ref[...] = jnp.zeros(loss_ref.shape, F32)

            def epilogue(rows):
                fv = acc_ref[rows, :]
                half_gate = 0.5 * gt_ref[...]
                xo = x_ref[rows, :] + half_gate * fv
                if not with_loss:
                    xo_ref[rows, :] = xo
                    f_ref[rows, :] = fv.astype(f_ref.dtype)
                    return
                err = xo - tgt_ref[rows, :]
                gout = err * (1.0 / D)
                gout_ref[rows, :] = gout
                df_ref[rows, :] = (half_gate * gout).astype(BF16)
                dgt_ref[...] += 0.5 * jnp.sum(gout * fv, axis=0, keepdims=True)
                loss_part = jnp.sum(jnp.sum(err * err, axis=1, keepdims=True), axis=0, keepdims=True)
                loss_ref[...] += jnp.broadcast_to(loss_part, loss_ref.shape)

            _for_rows(tm, epilogue)

    row = pl.BlockSpec((tm, D), lambda i, j: (i, 0))
    row_in = pl.BlockSpec((tm, D), lambda i, j: (i, 0), pipeline_mode=pl.Buffered(1))
    vec = pl.BlockSpec((1, D), lambda i, j: (0, 0))
    col = pl.BlockSpec((tm, tf), lambda i, j: (i, j))
    in_specs = [row_in, vec, vec, vec, vec,
                pl.BlockSpec((D, tf), lambda i, j: (0, j)),
                pl.BlockSpec((D, tf), lambda i, j: (0, j)),
                pl.BlockSpec((tf, D), lambda i, j: (j, 0))]
    args = [x, g, sh, sc, gt, w1, w3, w2]
    act = jax.ShapeDtypeStruct((S, F), BF16)
    if with_loss:
        in_specs.append(row_in)
        args.append(tgt)
        out_specs = [row, row, row_in, col, col, vec, pl.BlockSpec((1, LANES), lambda i, j: (0, 0))]
        out_shape = [jax.ShapeDtypeStruct((S, D), F32), jax.ShapeDtypeStruct((S, D), BF16),
                     jax.ShapeDtypeStruct((S, D), BF16), act, act,
                     jax.ShapeDtypeStruct((1, D), F32), jax.ShapeDtypeStruct((1, LANES), F32)]
    elif h is None:
        out_specs = [row, row, col, col, row]
        out_shape = [jax.ShapeDtypeStruct((S, D), F32), jax.ShapeDtypeStruct((S, D), BF16), act, act,
                     jax.ShapeDtypeStruct((S, D), BF16)]
    else:
        in_specs.append(row_in)
        args.append(h)
        out_specs = [row, col, col, row]
        out_shape = [jax.ShapeDtypeStruct((S, D), F32), act, act, jax.ShapeDtypeStruct((S, D), BF16)]
    return _pallas(body, name=name, grid=(ni, nj), in_specs=in_specs, out_specs=out_specs, out_shape=out_shape,
                   scratch_shapes=[pltpu.VMEM((tm, D), F32)],
                   semantics=("arbitrary", "arbitrary"), args=args, ex=ex)


def _ffn_bwd(df, a, b, w1, w3, w2, name, ex=None):
    S, D = df.shape
    F = a.shape[1]
    tm, tf, nj, tail = _tile(S, 512), *_ffn_blocks(F)

    def body(df_ref, a_ref, b_ref, w1_ref, w3_ref, w2_ref, da_ref, db_ref, s_ref, dh_ref):
        j = pl.program_id(1)

        @pl.when(j == 0)
        def _():
            dh_ref[...] = jnp.zeros(dh_ref.shape, F32)

        def columns(width):
            def run():
                ds = _dot(df_ref[...], w2_ref[:width, :], NT)
                av = a_ref[:, :width].astype(F32)
                bv = b_ref[:, :width].astype(F32)
                sig = _sigmoid(av)
                sil = av * sig
                da = ((ds * bv) * (sig * (1.0 + av * (1.0 - sig)))).astype(BF16)
                db = (ds * sil).astype(BF16)
                da_ref[:, :width] = da
                db_ref[:, :width] = db
                s_ref[:, :width] = (sil * bv).astype(BF16)
                dh_ref[...] += _dot(da, w1_ref[:, :width], NT) + _dot(db, w3_ref[:, :width], NT)
            return run

        if tail == tf:
            columns(tf)()
        else:
            pl.when(j < nj - 1)(columns(tf))
            pl.when(j == nj - 1)(columns(tail))

    row = pl.BlockSpec((tm, D), lambda i, j: (i, 0))
    col = pl.BlockSpec((tm, tf), lambda i, j: (i, j))
    act = jax.ShapeDtypeStruct((S, F), BF16)
    return _pallas(body, name=name, grid=(S // tm, nj),
                   in_specs=[row, col, col,
                             pl.BlockSpec((D, tf), lambda i, j: (0, j)),
                             pl.BlockSpec((D, tf), lambda i, j: (0, j)),
                             pl.BlockSpec((tf, D), lambda i, j: (j, 0))],
                   out_specs=[col, col, col, row],
                   out_shape=[act, act, act, jax.ShapeDtypeStruct((S, D), F32)],
                   semantics=("parallel", "arbitrary"), args=[df, a, b, w1, w3, w2], ex=ex)


def _norm_mod(x, g, sh, sc, name, ex=None, after=()):
    S, D = x.shape
    tm = _tile(S, 512)

    def body(x_ref, g_ref, sh_ref, sc_ref, h_ref):
        def step(rows):
            xhat, _ = _rms(x_ref[rows, :])
            h_ref[rows, :] = ((xhat * g_ref[...]) * (1.0 + sc_ref[...]) + sh_ref[...]).astype(BF16)

        _for_rows(tm, step)

    row = pl.BlockSpec((tm, D), lambda i: (i, 0))
    vec = pl.BlockSpec((1, D), lambda i: (0, 0))
    outs = _pallas(body, name=name, grid=(S // tm,), in_specs=[row, vec, vec, vec], out_specs=[row],
                   out_shape=[jax.ShapeDtypeStruct((S, D), BF16)], semantics=("parallel",), args=[x, g, sh, sc], ex=ex,
                   after=after)
    return outs[0] if ex is None else outs


def _norm_bwd(dh, x, gres, g, sc, prev, name, ex=None, after=()):
    S, D = x.shape
    tm = _tile(S, 256)
    has_prev = prev is not None
    coef = prev[2] if has_prev else None

    def body(*refs):
        if has_prev:
            (dh_ref, x_ref, gr_ref, g_ref, sc_ref, f_ref, gt_ref,
             go_ref, dsh_ref, dsc_ref, dg_ref, dp_ref, dgt_ref) = refs
        else:
            dh_ref, x_ref, gr_ref, g_ref, sc_ref, go_ref, dsh_ref, dsc_ref, dg_ref = refs
        sum_refs = [dsh_ref, dsc_ref, dg_ref] + ([dgt_ref] if has_prev else [])

        @pl.when(pl.program_id(0) == 0)
        def _():
            for ref in sum_refs:
                ref[...] = jnp.zeros(ref.shape, F32)

        def step(rows):
            dh = dh_ref[rows, :]
            xhat, r = _rms(x_ref[rows, :])
            gain = g_ref[...]
            scale1 = 1.0 + sc_ref[...]
            gout = gr_ref[rows, :] + _rms_bwd(dh * scale1 * gain, xhat, r)
            go_ref[rows, :] = gout
            sums = [dh, dh * (xhat * gain), dh * scale1 * xhat]
            if has_prev:
                dp_ref[rows, :] = ((coef * gt_ref[...]) * gout).astype(BF16)
                sums.append(coef * (gout * f_ref[rows, :].astype(F32)))
            for ref, v in zip(sum_refs, sums):
                ref[...] += jnp.sum(v, axis=0, keepdims=True)

        _for_rows(tm, step)

    row = pl.BlockSpec((tm, D), lambda i: (i, 0))
    vec = pl.BlockSpec((1, D), lambda i: (0, 0))
    vshape = jax.ShapeDtypeStruct((1, D), F32)
    in_specs = [row, row, row, vec, vec]
    args = [dh, x, gres, g, sc]
    out_specs = [row, vec, vec, vec]
    out_shape = [jax.ShapeDtypeStruct((S, D), F32), vshape, vshape, vshape]
    if has_prev:
        in_specs += [row, vec]
        args += [prev[0], prev[1]]
        out_specs += [row, vec]
        out_shape += [jax.ShapeDtypeStruct((S, D), BF16), vshape]
    return _pallas(body, name=name, grid=(S // tm,), in_specs=in_specs, out_specs=out_specs, out_shape=out_shape,
                   semantics=("arbitrary",), args=args, ex=ex, after=after)


PAIRS = B_HEADS // 2
PAIR_ROWS = (PAIRS // KV_HEADS) * BLOCK
BAND = 2 * BLOCK


def _stack(ref, offset, count):
    return jnp.concatenate([ref[:, offset + p * LANES:offset + (p + 1) * LANES] for p in range(count)], axis=0)


def _seg_mean(x, e_ref):
    return _dot(x.astype(BF16), e_ref[...]) * (1.0 / HEAD_DIM)


def _block_diag(x, x_rolled, left, kv_head):
    if kv_head == 0:
        top, bottom = jnp.where(left, x, 0.0), jnp.where(left, 0.0, x_rolled)
    else:
        top, bottom = jnp.where(left, x_rolled, 0.0), jnp.where(left, 0.0, x)
    return jnp.concatenate([top, bottom], axis=0).astype(BF16)


def _from_block_diag(g, left, kv_head):
    a, b = g[:BAND], g[BAND:]
    if kv_head == 0:
        return jnp.where(left, a + pltpu.roll(b, HEAD_DIM, 1), 0.0)
    return jnp.where(left, 0.0, pltpu.roll(a, HEAD_DIM, 1) + b)


def _pair_softmax(st, sk_ref, kv_head):
    out = []
    for e in range(2):
        seg = st[e * BAND:(e + 1) * BAND]
        sink = jnp.concatenate([jnp.full((1, BLOCK), sk_ref[kv_head * GROUP + 2 * p + e], F32)
                                for p in range(PAIRS // KV_HEADS)], axis=1)
        m = jnp.maximum(jnp.max(seg, axis=0, keepdims=True), sink)
        p_ = jnp.exp(seg - m)
        e_sink = jnp.exp(sink - m)
        inv = 1.0 / (jnp.sum(p_, axis=0, keepdims=True) + e_sink)
        out.append((p_ * inv, e_sink * inv))
    return out


def _lane_mean(x, ones_ref):
    return _dot(x.astype(BF16), ones_ref[...])


def _mixer_specs(nb, last):
    full = lambda shape: pl.BlockSpec(shape, lambda n: (0,) * len(shape))
    z_spec = pl.BlockSpec((BLOCK, IN_COLS), lambda n: (jnp.minimum(n, last), 0))
    zp_spec = pl.BlockSpec((BLOCK, 2 * KV_WIDTH), lambda n: (jnp.clip(n - 1, 0, last), K_OFF // (2 * KV_WIDTH)))
    consts = [full((A_HEADS * BLOCK, A_DIM)), full((A_HEADS * BLOCK, A_DIM)), full((1, LANES)), full((1, LANES)),
              full((LANES, LANES)), full((LANES, LANES)), pl.BlockSpec(memory_space=pltpu.SMEM),
              pl.BlockSpec((None, KV_HEADS, PAIR_ROWS, 2 * BAND), lambda n: (jnp.minimum(n, 1), 0, 0, 0))]
    return full, z_spec, zp_spec, consts


def _mixer_fwd(z, wm, sbp, gvp, gq2, gk2, seg_ones, lane_ones, sinks, biasp, name):
    S = z.shape[0]
    nb = S // BLOCK

    def body(z_ref, zp_ref, wm_ref, sbp_ref, gvp_ref, gq2_ref, gk2_ref, e_ref, l_ref, sk_ref, bias_ref, mix_ref):
        u = _gelu(_stack(z_ref, 0, A_HEADS))
        v = _gelu(_stack(z_ref, A_WIDTH, A_HEADS))
        vhat = v * lax.rsqrt(_lane_mean(v * v, l_ref) + EPS)
        vn = (vhat * gvp_ref[...]).astype(BF16)
        mixed = jnp.concatenate([_dot(wm_ref[h], vn[h * BLOCK:(h + 1) * BLOCK]) for h in range(A_HEADS)], axis=0)
        ya = (u * (mixed + sbp_ref[...])).astype(BF16)
        for h in range(A_HEADS):
            mix_ref[:, h * A_DIM:(h + 1) * A_DIM] = ya[h * BLOCK:(h + 1) * BLOCK]

        left = lax.broadcasted_iota(jnp.int32, (1, LANES), 1) < HEAD_DIM
        kv = jnp.concatenate([zp_ref[...], z_ref[:, K_OFF:K_OFF + 2 * KV_WIDTH]], axis=0)
        k2, v2 = kv[:, :KV_WIDTH], kv[:, KV_WIDTH:]
        kn2 = k2 * lax.rsqrt(_seg_mean(k2 * k2, e_ref) + EPS) * gk2_ref[...]
        kn2_r, v2_r = pltpu.roll(kn2, HEAD_DIM, 1), pltpu.roll(v2, HEAD_DIM, 1)
        qp = _stack(z_ref, Q_OFF, PAIRS)
        qn = (qp * lax.rsqrt(_seg_mean(qp * qp, e_ref) + EPS) * gq2_ref[...]).astype(BF16)
        for kh in range(KV_HEADS):
            kbd, vbd = _block_diag(kn2, kn2_r, left, kh), _block_diag(v2, v2_r, left, kh)
            st = _dot(kbd, qn[kh * PAIR_ROWS:(kh + 1) * PAIR_ROWS], NT) * (HEAD_DIM ** -0.5) + bias_ref[kh]
            wt = jnp.concatenate([w_e for w_e, _ in _pair_softmax(st, sk_ref, kh)], axis=0).astype(BF16)
            o = _dot(wt, vbd, TN).astype(BF16)
            for p in range(PAIRS // KV_HEADS):
                col = A_WIDTH + (kh * (PAIRS // KV_HEADS) + p) * LANES
                mix_ref[:, col:col + LANES] = o[p * BLOCK:(p + 1) * BLOCK]

    full, z_spec, zp_spec, consts = _mixer_specs(nb, nb - 1)
    return pl.pallas_call(
        body,
        name=name,
        grid=(nb,),
        in_specs=[z_spec, zp_spec, full((A_HEADS, BLOCK, BLOCK))] + consts,
        out_specs=pl.BlockSpec((BLOCK, A_WIDTH + B_WIDTH), lambda n: (n, 0)),
        out_shape=jax.ShapeDtypeStruct((S, A_WIDTH + B_WIDTH), BF16),
        compiler_params=_params(("parallel",)),
    )(z, z, wm, sbp, gvp, gq2, gk2, seg_ones, lane_ones, sinks, biasp)


def _mixer_bwd(z, dmix, wm, wm_t, sbp, gvp, gq2, gk2, seg_ones, lane_ones, sinks, biasp, pair_fold, name, ex=None):
    S = z.shape[0]
    nb = S // BLOCK

    def body(z_ref, zp_ref, dmix_ref, wm_ref, wmt_ref, sbp_ref, gvp_ref, gq2_ref, gk2_ref, e_ref, l_ref, sk_ref,
             bias_ref, fold_ref,
             dz_ref, dzkv_ref, dwm_ref, dsb_ref, dgv_ref, dgq_ref, dgk_ref, dsk_ref, dst_ref,
             carry_ref, tot_ref, sbacc_ref, skacc_ref, gqacc_ref, gkacc_ref):
        n = pl.program_id(0)
        left = lax.broadcasted_iota(jnp.int32, (1, LANES), 1) < HEAD_DIM

        @pl.when(n == 0)
        def _():
            for ref in (dwm_ref, dgv_ref, dst_ref, carry_ref, sbacc_ref, skacc_ref, gqacc_ref, gkacc_ref):
                ref[...] = jnp.zeros(ref.shape, ref.dtype)

        @pl.when(n < nb)
        def _():
            u, du_dz = _gelu_and_grad(_stack(z_ref, 0, A_HEADS))
            v, dv_dz = _gelu_and_grad(_stack(z_ref, A_WIDTH, A_HEADS))
            rv = lax.rsqrt(_lane_mean(v * v, l_ref) + EPS)
            vhat = v * rv
            gvp = gvp_ref[...]
            vn = (vhat * gvp).astype(BF16)
            rows = lambda a, h: a[h * BLOCK:(h + 1) * BLOCK]
            mixed = jnp.concatenate([_dot(wm_ref[h], rows(vn, h)) for h in range(A_HEADS)], axis=0) + sbp_ref[...]
            dya = _stack(dmix_ref, 0, A_HEADS).astype(F32)
            dmx = dya * u
            sbacc_ref[...] += dmx
            dmx_b = dmx.astype(BF16)
            for h in range(A_HEADS):
                dwm_ref[h] += _dot(rows(dmx_b, h), rows(vn, h), NT)
            dvn = jnp.concatenate([_dot(wmt_ref[h], rows(dmx_b, h)) for h in range(A_HEADS)], axis=0)
            dgv_ref[...] += jnp.sum(jnp.reshape(dvn * vhat, (A_HEADS, BLOCK, A_DIM)), axis=1)
            dzu = ((dya * mixed) * du_dz).astype(BF16)
            tv = dvn * gvp
            dzv = ((rv * (tv - vhat * _lane_mean(tv * vhat, l_ref))) * dv_dz).astype(BF16)
            for h in range(A_HEADS):
                dz_ref[:, h * A_DIM:(h + 1) * A_DIM] = rows(dzu, h)
                dz_ref[:, A_WIDTH + h * A_DIM:A_WIDTH + (h + 1) * A_DIM] = rows(dzv, h)

            kv = jnp.concatenate([zp_ref[...], z_ref[:, K_OFF:K_OFF + 2 * KV_WIDTH]], axis=0)
            k2, v2 = kv[:, :KV_WIDTH], kv[:, KV_WIDTH:]
            kn2 = k2 * lax.rsqrt(_seg_mean(k2 * k2, e_ref) + EPS) * gk2_ref[...]
            kn2_r, v2_r = pltpu.roll(kn2, HEAD_DIM, 1), pltpu.roll(v2, HEAD_DIM, 1)
            qp = _stack(z_ref, Q_OFF, PAIRS)
            rq = lax.rsqrt(_seg_mean(qp * qp, e_ref) + EPS)
            qhat = qp * rq
            gq2 = gq2_ref[...]
            qn = (qhat * gq2).astype(BF16)
            dop = _stack(dmix_ref, A_WIDTH, PAIRS)
            dqn_parts = []
            dk2n = jnp.zeros((BAND, KV_WIDTH), F32)
            dv2 = jnp.zeros((BAND, KV_WIDTH), F32)
            for kh in range(KV_HEADS):
                kbd, vbd = _block_diag(kn2, kn2_r, left, kh), _block_diag(v2, v2_r, left, kh)
                qg = qn[kh * PAIR_ROWS:(kh + 1) * PAIR_ROWS]
                dog = dop[kh * PAIR_ROWS:(kh + 1) * PAIR_ROWS]
                st = _dot(kbd, qg, NT) * (HEAD_DIM ** -0.5) + bias_ref[kh]
                halves = _pair_softmax(st, sk_ref, kh)
                dpt = _dot(vbd, dog, NT)
                ds_halves, t_halves = [], []
                for e, (w_e, w_sink) in enumerate(halves):
                    dp_e = dpt[e * BAND:(e + 1) * BAND]
                    delta = jnp.sum(w_e * dp_e, axis=0, keepdims=True)
                    ds_halves.append(w_e * (dp_e - delta))
                    t_halves.append(-(w_sink * delta))
                dst = jnp.concatenate(ds_halves, axis=0)
                dst_ref[kh] += dst
                skacc_ref[2 * kh:2 * kh + 2, :] += jnp.concatenate(t_halves, axis=0)
                ds_b = (dst * (HEAD_DIM ** -0.5)).astype(BF16)
                w_b = jnp.concatenate([w_e for w_e, _ in halves], axis=0).astype(BF16)
                dqn_parts.append(_dot(ds_b, kbd, TN))
                dk2n += _from_block_diag(_dot(ds_b, qg), left, kh)
                dv2 += _from_block_diag(_dot(w_b, dog), left, kh)
            dqn = jnp.concatenate(dqn_parts, axis=0)
            gqacc_ref[...] += jnp.sum(dqn * qhat, axis=0, keepdims=True)
            t = dqn * gq2
            dzq = (rq * (t - qhat * _seg_mean(t * qhat, e_ref))).astype(BF16)
            for p in range(PAIRS):
                dz_ref[:, Q_OFF + p * LANES:Q_OFF + (p + 1) * LANES] = rows(dzq, p)
            tot_ref[0] = carry_ref[0] + dk2n[:BLOCK]
            tot_ref[1] = carry_ref[1] + dv2[:BLOCK]
            carry_ref[0] = dk2n[BLOCK:]
            carry_ref[1] = dv2[BLOCK:]

        @pl.when(n == nb)
        def _():
            tot_ref[...] = carry_ref[...]

        kp = zp_ref[:, :KV_WIDTH]
        rk = lax.rsqrt(_seg_mean(kp * kp, e_ref) + EPS)
        khat = kp * rk
        dkn = tot_ref[0]
        gkacc_ref[...] += jnp.sum(dkn * khat, axis=0, keepdims=True)
        t = dkn * gk2_ref[...]
        dzkv_ref[:, :KV_WIDTH] = (rk * (t - khat * _seg_mean(t * khat, e_ref))).astype(BF16)
        dzkv_ref[:, KV_WIDTH:] = tot_ref[1].astype(BF16)

        @pl.when(n == nb)
        def _():
            dsb_ref[...] = jnp.broadcast_to(jnp.sum(sbacc_ref[...], axis=1, keepdims=True), dsb_ref.shape)
            dsk_ref[...] = lax.dot_general(skacc_ref[...], fold_ref[...], NN, precision=lax.Precision.HIGHEST,
                                           preferred_element_type=F32)
            dgq_ref[...] = gqacc_ref[...] + pltpu.roll(gqacc_ref[...], HEAD_DIM, 1)
            dgk_ref[...] = gkacc_ref[...] + pltpu.roll(gkacc_ref[...], HEAD_DIM, 1)

    last = nb - 1
    full, z_spec, zp_spec, consts = _mixer_specs(nb, last)
    return _pallas(
        body,
        name=name,
        grid=(nb + 1,),
        ex=ex,
        in_specs=[z_spec, zp_spec, pl.BlockSpec((BLOCK, A_WIDTH + B_WIDTH), lambda n: (jnp.minimum(n, last), 0)),
                  full((A_HEADS, BLOCK, BLOCK)), full((A_HEADS, BLOCK, BLOCK))] + consts + [full((PAIR_ROWS, LANES))],
        out_specs=[
            pl.BlockSpec((BLOCK, K_OFF), lambda n: (jnp.minimum(n, last), 0)),
            pl.BlockSpec((BLOCK, 2 * KV_WIDTH), lambda n: (jnp.maximum(n - 1, 0), 0)),
            full((A_HEADS, BLOCK, BLOCK)), full((A_HEADS * BLOCK, A_DIM)), full((A_HEADS, A_DIM)),
            full((1, LANES)), full((1, LANES)), full((SUBLANES, LANES)),
            full((KV_HEADS, PAIR_ROWS, 2 * BAND)),
        ],
        out_shape=[
            jax.ShapeDtypeStruct((S, K_OFF), BF16),
            jax.ShapeDtypeStruct((S, 2 * KV_WIDTH), BF16),
            jax.ShapeDtypeStruct((A_HEADS, BLOCK, BLOCK), F32),
            jax.ShapeDtypeStruct((A_HEADS * BLOCK, A_DIM), F32),
            jax.ShapeDtypeStruct((A_HEADS, A_DIM), F32),
            jax.ShapeDtypeStruct((1, LANES), F32),
            jax.ShapeDtypeStruct((1, LANES), F32),
            jax.ShapeDtypeStruct((SUBLANES, LANES), F32),
            jax.ShapeDtypeStruct((KV_HEADS, PAIR_ROWS, 2 * BAND), F32),
        ],
        scratch_shapes=[
            pltpu.VMEM((2, BLOCK, KV_WIDTH), F32),
            pltpu.VMEM((2, BLOCK, KV_WIDTH), F32),
            pltpu.VMEM((A_HEADS * BLOCK, A_DIM), F32),
            pltpu.VMEM((SUBLANES, PAIR_ROWS), F32),
            pltpu.VMEM((1, LANES), F32),
            pltpu.VMEM((1, LANES), F32),
        ],
        semantics=("arbitrary",),
        args=[z, z, dmix, wm, wm_t, sbp, gvp, gq2, gk2, seg_ones, lane_ones, sinks, biasp, pair_fold],
    )


def _mixer_out(mix, w_out, x, gt, name):
    S, D = x.shape
    K = mix.shape[1]
    tm, tn = _tile(S, 1024), _tile(D, 1024)

    def body(m_ref, w_ref, x_ref, gt_ref, xo_ref, y_ref):
        y = _dot(m_ref[...], w_ref[...])
        y_ref[...] = y.astype(BF16)
        xo_ref[...] = x_ref[...] + gt_ref[...] * y

    blk = pl.BlockSpec((tm, tn), lambda j, i: (i, j))
    return pl.pallas_call(
        body,
        name=name,
        grid=(D // tn, S // tm),
        in_specs=[pl.BlockSpec((tm, K), lambda j, i: (i, 0)), pl.BlockSpec((K, tn), lambda j, i: (0, j)),
                  blk, pl.BlockSpec((1, tn), lambda j, i: (0, j))],
        out_specs=[blk, blk],
        out_shape=[jax.ShapeDtypeStruct((S, D), F32), jax.ShapeDtypeStruct((S, D), BF16)],
        compiler_params=_params(("parallel", "parallel")),
    )(mix, w_out, x, gt)


def _bucket_sum(dst, onehot, name):
    def body(d_ref, o_ref, out_ref):
        out_ref[...] = lax.dot_general(o_ref[...], d_ref[...], NT, precision=lax.Precision.HIGHEST,
                                       preferred_element_type=F32)

    return pl.pallas_call(
        body,
        name=name,
        out_shape=jax.ShapeDtypeStruct((N_BUCKETS, B_HEADS), F32),
    )(dst, onehot)


def _adamw_math(w, g, m, v):
    m = ADAM_B1 * m + (1.0 - ADAM_B1) * g
    v = ADAM_B2 * v + (1.0 - ADAM_B2) * (g * g)
    m_hat = m / (1.0 - ADAM_B1 ** ADAM_STEP)
    v_hat = v / (1.0 - ADAM_B2 ** ADAM_STEP)
    delta = -ADAM_LR * (m_hat / (jnp.sqrt(v_hat) + ADAM_EPS) + ADAM_WD * w)
    return delta, m, v


def _adamw(w, g, m, v, name, emit_grad=False, after=()):
    R, C = w.shape
    tr = _tile(R, max(SUBLANES, (1 << 19) // C), SUBLANES)

    def body(w_ref, g_ref, m_ref, v_ref, *out_refs):
        gv = g_ref[...]
        results = _adamw_math(w_ref[...], gv, m_ref[...], v_ref[...])
        for ref, val in zip(out_refs, ((gv,) if emit_grad else ()) + results):
            ref[...] = val

    blk = pl.BlockSpec((tr, C), lambda i: (i, 0))
    shape = jax.ShapeDtypeStruct((R, C), F32)
    n_out = 4 if emit_grad else 3
    return _pallas(body, name=name, grid=(R // tr,), in_specs=[blk] * 4, out_specs=[blk] * n_out,
                   out_shape=[shape] * n_out, semantics=("parallel",), args=[w, g, m, v], after=after)


def _small_update(parts, w, m, v, name):
    R = w.shape[0]

    def body(p_ref, w_ref, m_ref, v_ref, g_ref, d_ref, mo_ref, vo_ref):
        g = p_ref[0]
        for dev in range(1, N_DEV):
            g = g + p_ref[dev]
        g_ref[...] = g
        d, mn, vn = _adamw_math(w_ref[...], g, m_ref[...], v_ref[...])
        d_ref[...] = d
        mo_ref[...] = mn
        vo_ref[...] = vn

    shape = jax.ShapeDtypeStruct((R, LANES), F32)
    return pl.pallas_call(
        body,
        name=name,
        out_shape=[shape] * 4,
        compiler_params=pltpu.CompilerParams(vmem_limit_bytes=VMEM_LIMIT),
    )(parts, w, m, v)


def _place():
    x, y, c = lax.axis_index("x"), lax.axis_index("y"), lax.axis_index("c")
    chips = [(1 - x, y), (x, 1 - y), (1 - x, 1 - y)]
    return x, y, c, chips


def _remote(src, dst, send_sem, recv_sem, to):
    return pltpu.make_async_remote_copy(src_ref=src, dst_ref=dst, send_sem=send_sem, recv_sem=recv_sem,
                                        device_id=to, device_id_type=MESH)


def _allgather_small(block, name):
    m_per, n = block.shape

    def body(x_ref, out_ref, send_sems, recv_sems, local_sem):
        x, y, c, chips = _place()
        me, sibling = (x, y, c), (x, y, 1 - c)

        def rows(px, py, pc):
            return out_ref.at[pl.ds((4 * px + 2 * py + pc) * m_per, m_per), :]

        def copy(k, blk, to, src=None):
            return _remote(rows(*blk) if src is None else src, rows(*blk), send_sems.at[k], recv_sems.at[k], to)

        mine = pltpu.make_async_copy(x_ref, rows(*me), local_sem)
        mine.start()
        first = [copy(0, me, sibling, src=x_ref)]
        first += [copy(1 + j, me, (*chip, c), src=x_ref) for j, chip in enumerate(chips)]
        for cp in first:
            cp.start()
        passed = [copy(4 + j, (*chip, c), sibling) for j, chip in enumerate(chips)]
        for j, chip in enumerate(chips):
            copy(1 + j, (*chip, c), me).wait_recv()
            passed[j].start()
        copy(0, sibling, me).wait_recv()
        for j, chip in enumerate(chips):
            copy(4 + j, (*chip, 1 - c), me).wait_recv()
        for cp in first + passed:
            cp.wait_send()
        mine.wait()

    return pl.pallas_call(
        body,
        name=name,
        out_shape=jax.ShapeDtypeStruct((N_DEV * m_per, n), block.dtype),
        in_specs=[pl.BlockSpec(memory_space=pltpu.VMEM)],
        out_specs=pl.BlockSpec(memory_space=pltpu.VMEM),
        scratch_shapes=[pltpu.SemaphoreType.DMA((7,)), pltpu.SemaphoreType.DMA((7,)), pltpu.SemaphoreType.DMA],
        compiler_params=pltpu.CompilerParams(vmem_limit_bytes=VMEM_LIMIT),
    )(block)


def _half(ref, c, rows):
    start = pl.multiple_of(c * rows, BF16_ROWS)
    if len(ref.shape) == 2:
        return ref.at[pl.ds(start, rows), :]
    return ref.at[:, pl.ds(start, rows), :]


def _same(arrays):
    return [jax.ShapeDtypeStruct(a.shape, a.dtype) for a in arrays], {t: t for t in range(len(arrays))}


def _ex_gather_ici(bufs):
    def plan(ins, outs, send_sems, recv_sems):
        x, y, c, chips = _place()
        sends, arrivals = [], []
        for t, buf in enumerate(bufs):
            rows = buf.shape[1] // 2
            mine = _half(outs[t].at[2 * x + y], c, rows)
            for k, (px, py) in enumerate(chips):
                sems = (send_sems.at[3 * t + k], recv_sems.at[3 * t + k], (px, py, c))
                landed = _half(outs[t].at[2 * px + py], c, rows)
                sends.append((mine, mine, *sems))
                arrivals.append((landed, landed, *sems))
        return sends, arrivals

    shapes, aliases = _same(bufs)
    return _Exchange(bufs, shapes, aliases, 3 * len(bufs), plan)


def _ex_gather_near(bufs):
    def plan(ins, outs, send_sems, recv_sems):
        x, y, c, chips = _place()
        sends, arrivals = [], []
        for t, buf in enumerate(bufs):
            rows = buf.shape[1] // 2
            mine = _half(outs[t].at[2 * x + y], c, rows)
            for k, (px, py) in enumerate(chips[:2]):
                sems = (send_sems.at[2 * t + k], recv_sems.at[2 * t + k], (px, py, c))
                landed = _half(outs[t].at[2 * px + py], c, rows)
                sends.append((mine, mine, *sems))
                arrivals.append((landed, landed, *sems))
        return sends, arrivals

    shapes, aliases = _same(bufs)
    return _Exchange(bufs, shapes, aliases, 2 * len(bufs), plan)


def _ex_gather_relay(bufs):
    def plan(ins, outs, send_sems, recv_sems):
        x, y, c, chips = _place()
        (xn, yn, diag) = chips
        slot = lambda chip: 2 * chip[0] + chip[1]
        sends, arrivals = [], []
        for t, buf in enumerate(bufs):
            quarter = buf.shape[1] // 4

            def piece(chip, q):
                start = pl.multiple_of(c * 2 * quarter + q * quarter, BF16_ROWS)
                return outs[t].at[slot(chip), pl.ds(start, quarter), :]

            for k, (held, to) in enumerate([(xn, yn), (yn, xn)]):
                sems = (send_sems.at[2 * t + k], recv_sems.at[2 * t + k], (*to, c))
                sends.append((piece(held, k), piece(held, k), *sems))
                arrivals.append((piece(diag, k), piece(diag, k), *sems))
        return sends, arrivals

    shapes, aliases = _same(bufs)
    return _Exchange(bufs, shapes, aliases, 2 * len(bufs), plan)


def _ex_gather_d2d(bufs):
    def plan(ins, outs, send_sems, recv_sems):
        x, y, c, chips = _place()
        sends, arrivals = [], []
        for t, buf in enumerate(bufs):
            rows = buf.shape[1] // 2
            for k, (px, py) in enumerate(chips):
                sems = (send_sems.at[3 * t + k], recv_sems.at[3 * t + k], (x, y, 1 - c))
                landed = _half(outs[t].at[2 * px + py], c, rows)
                other = _half(outs[t].at[2 * px + py], 1 - c, rows)
                sends.append((landed, landed, *sems))
                arrivals.append((other, other, *sems))
        return sends, arrivals

    shapes, aliases = _same(bufs)
    return _Exchange(bufs, shapes, aliases, 3 * len(bufs), plan)


def _ex_swap_halves(grads):
    def plan(ins, outs, send_sems, recv_sems):
        x, y, c, _ = _place()
        sends = [(_half(ins[t], 1 - c, g.shape[1] // 2), outs[t], send_sems.at[t], recv_sems.at[t], (x, y, 1 - c))
                 for t, g in enumerate(grads)]
        return sends, sends

    shapes = [jax.ShapeDtypeStruct((g.shape[0], g.shape[1] // 2, g.shape[2]), g.dtype) for g in grads]
    return _Exchange(grads, shapes, {}, len(grads), plan)


def _ex_scatter(sums):
    def plan(ins, outs, send_sems, recv_sems):
        x, y, c, chips = _place()
        sends = [(ins[t].at[2 * px + py], outs[t].at[k], send_sems.at[3 * t + k], recv_sems.at[3 * t + k], (px, py, c))
                 for t in range(len(sums)) for k, (px, py) in enumerate(chips)]
        return sends, sends

    shapes = [jax.ShapeDtypeStruct((N_CHIPS - 1,) + s.shape[1:], s.dtype) for s in sums]
    return _Exchange(sums, shapes, {}, 3 * len(sums), plan)


def _ex_join_halves(fulls):
    def plan(ins, outs, send_sems, recv_sems):
        x, y, c, _ = _place()
        sends, arrivals = [], []
        for t, full in enumerate(fulls):
            rows = full.shape[0] // 2
            sems = (send_sems.at[t], recv_sems.at[t], (x, y, 1 - c))
            mine, other = _half(outs[t], c, rows), _half(outs[t], 1 - c, rows)
            sends.append((mine, mine, *sems))
            arrivals.append((other, other, *sems))
        return sends, arrivals

    shapes, aliases = _same(fulls)
    return _Exchange(fulls, shapes, aliases, len(fulls), plan)


class _Shifted:
    def __init__(self, sems, offset):
        self.sems, self.offset = sems, offset

    @property
    def at(self):
        return self

    def __getitem__(self, k):
        return self.sems.at[self.offset + k]


def _combine(exchanges):
    operands, out_shapes, aliases, starts = [], [], {}, []
    n_sems = 0
    for e in exchanges:
        starts.append((len(operands), len(out_shapes), n_sems))
        aliases.update({len(operands) + i: len(out_shapes) + o for i, o in e.aliases.items()})
        operands += list(e.operands)
        out_shapes += list(e.out_shapes)
        n_sems += e.n_sems

    def plan(ins, outs, send_sems, recv_sems):
        sends, arrivals = [], []
        for e, (i0, o0, s0) in zip(exchanges, starts):
            s, a = e.plan(ins[i0:i0 + len(e.operands)], outs[o0:o0 + len(e.out_shapes)],
                          _Shifted(send_sems, s0), _Shifted(recv_sems, s0))
            sends += s
            arrivals += a
        return sends, arrivals

    return _Exchange(operands, out_shapes, aliases, n_sems, plan)


class _Reduction:
    def __init__(self, grad, tag, c_arr, jc_arr):
        self.grad, self.tag, self.c_arr, self.jc_arr, self.stage = grad, tag, c_arr, jc_arr, 0

    def exchange(self):
        if self.stage == 0:
            return _ex_swap_halves([self.grad])
        if self.stage == 1:
            return _ex_scatter([self.sums])
        return _ex_join_halves([self.full])

    def advance(self, landed):
        if self.stage == 0:
            self.recv = landed
            self.sums = _chip_sum(self.grad, landed, self.c_arr, f"chip_sum_{self.tag}")
        elif self.stage == 1:
            self.full = _owner_sum(self.grad, self.recv, landed, self.jc_arr, f"owner_sum_{self.tag}")
        else:
            self.result = landed
        self.stage += 1


def _ride(reductions):
    def done(carried):
        for r, landed in zip(reductions, carried):
            r.advance(landed)

    return _combine([r.exchange() for r in reductions]), done


def _exchange_alone(ex, name):
    return _pallas(None, name=name, grid=(), in_specs=[], out_specs=[], out_shape=[], args=[], ex=ex)


SEM = pl.BlockSpec(memory_space=pltpu.SEMAPHORE)
DATAFLOW = pltpu.SideEffectType.DATAFLOW_SIDE_EFFECTING


def _exchange_start(ex, name):
    e_in, e_out = len(ex.operands), len(ex.out_shapes)
    kept = [i for i in range(e_in) if i not in ex.aliases]

    def body(*refs):
        ins, refs = refs[:e_in], refs[e_in:]
        outs, refs = refs[:e_out], refs[e_out:]
        _, (send_sems, recv_sems, token) = refs[:len(kept)], refs[len(kept):]
        sends, _ = ex.plan(ins, outs, send_sems, recv_sems)
        for cp in sends:
            _remote(*cp).start()
        token[...] = jnp.zeros(token.shape, F32)

    sems = pltpu.SemaphoreType.DMA((ex.n_sems,))
    aliases = dict(ex.aliases)
    aliases.update({i: e_out + k for k, i in enumerate(kept)})
    res = pl.pallas_call(
        body,
        name=name,
        in_specs=[ANY] * e_in,
        out_specs=[ANY] * (e_out + len(kept)) + [SEM, SEM, pl.BlockSpec(memory_space=pltpu.VMEM)],
        out_shape=list(ex.out_shapes) + [jax.ShapeDtypeStruct(ex.operands[i].shape, ex.operands[i].dtype) for i in kept]
        + [sems, sems, jax.ShapeDtypeStruct((SUBLANES, LANES), F32)],
        input_output_aliases=aliases,
        compiler_params=pltpu.CompilerParams(has_side_effects=DATAFLOW),
    )(*ex.operands)
    outs, kept_thru, (send_sems, recv_sems, token) = res[:e_out], res[e_out:e_out + len(kept)], res[e_out + len(kept):]
    operands = list(ex.operands)
    for i, o in ex.aliases.items():
        operands[i] = outs[o]
    for k, i in enumerate(kept):
        operands[i] = kept_thru[k]
    return (operands, outs, send_sems, recv_sems), token


def _exchange_wait(ex, state, after, name):
    operands, outs, send_sems, recv_sems = state
    e_out = len(outs)
    kept = [i for i in range(len(operands)) if i not in ex.aliases]

    def body(*refs):
        sources, refs = refs[:len(kept)], refs[len(kept):]
        landing, refs = refs[:e_out], refs[e_out:]
        ins = [None] * len(operands)
        for k, i in enumerate(kept):
            ins[i] = sources[k]
        sends, arrivals = ex.plan(ins, landing, refs[0], refs[1])
        for cp in arrivals:
            _remote(*cp).wait_recv()
        for cp in sends:
            _remote(*cp).wait_send()

    return pl.pallas_call(
        body,
        name=name,
        in_specs=[ANY] * (len(kept) + e_out) + [SEM, SEM] + [ANY] * len(after),
        out_specs=[ANY] * e_out,
        out_shape=[jax.ShapeDtypeStruct(o.shape, o.dtype) for o in outs],
        input_output_aliases={len(kept) + o: o for o in range(e_out)},
        compiler_params=pltpu.CompilerParams(has_side_effects=DATAFLOW),
    )(*[operands[i] for i in kept], *outs, send_sems, recv_sems, *after)


def _cast_to_slot(w, chip_arr, name, after=()):
    A, B = w.shape
    ta = _tile(A, max(BF16_ROWS, (1 << 19) // B), BF16_ROWS)

    def body(j_ref, w_ref, *rest):
        rest[-1][...] = w_ref[...].astype(BF16)

    return pl.pallas_call(
        body,
        name=name,
        grid_spec=pltpu.PrefetchScalarGridSpec(
            num_scalar_prefetch=1,
            grid=(A // ta,),
            in_specs=[pl.BlockSpec((ta, B), lambda i, j_ref: (i, 0))] + [ANY] * len(after),
            out_specs=pl.BlockSpec((None, ta, B), lambda i, j_ref: (j_ref[0], i, 0)),
        ),
        out_shape=jax.ShapeDtypeStruct((N_CHIPS, A, B), BF16),
        compiler_params=_params(("parallel",)),
    )(chip_arr, w, *after)


def _chip_sum(grad, recv, c_arr, name):
    _, A, B = grad.shape
    hA = A // 2
    ta = _tile(hA, max(BF16_ROWS, (1 << 19) // B), BF16_ROWS)
    nh = hA // ta

    def body(c_ref, g_ref, r_ref, o_ref):
        o_ref[...] = (g_ref[...] + r_ref[...]).astype(BF16)

    return pl.pallas_call(
        body,
        name=name,
        grid_spec=pltpu.PrefetchScalarGridSpec(
            num_scalar_prefetch=1,
            grid=(N_CHIPS, nh),
            in_specs=[pl.BlockSpec((None, ta, B), lambda s, i, c_ref: (s, c_ref[0] * nh + i, 0)),
                      pl.BlockSpec((None, ta, B), lambda s, i, c_ref: (s, i, 0))],
            out_specs=pl.BlockSpec((None, ta, B), lambda s, i, c_ref: (s, i, 0)),
        ),
        out_shape=jax.ShapeDtypeStruct((N_CHIPS, hA, B), BF16),
        compiler_params=_params(("parallel", "parallel")),
    )(c_arr, grad, recv)


def _owner_sum(grad, recv, landed, jc_arr, name):
    _, A, B = grad.shape
    hA = A // 2
    ta = _tile(hA, max(BF16_ROWS, (1 << 19) // B), BF16_ROWS)
    nh = hA // ta

    def body(jc_ref, g_ref, r_ref, l0_ref, l1_ref, l2_ref, o_ref):
        total = g_ref[...] + r_ref[...]
        for ref in (l0_ref, l1_ref, l2_ref):
            total = total + ref[...].astype(F32)
        o_ref[...] = total

    def landed_spec(k):
        return pl.BlockSpec((None, ta, B), lambda i, jc_ref: (k, i, 0))

    return pl.pallas_call(
        body,
        name=name,
        grid_spec=pltpu.PrefetchScalarGridSpec(
            num_scalar_prefetch=1,
            grid=(nh,),
            in_specs=[pl.BlockSpec((None, ta, B), lambda i, jc_ref: (jc_ref[0], jc_ref[1] * nh + i, 0)),
                      pl.BlockSpec((None, ta, B), lambda i, jc_ref: (jc_ref[0], i, 0)),
                      landed_spec(0), landed_spec(1), landed_spec(2)],
            out_specs=pl.BlockSpec((ta, B), lambda i, jc_ref: (jc_ref[1] * nh + i, 0)),
        ),
        out_shape=jax.ShapeDtypeStruct((A, B), F32),
        compiler_params=_params(("parallel",)),
    )(jc_arr, grad, recv, landed, landed, landed)


def _pack(parts):
    rows = []
    for p in parts:
        flat = jnp.reshape(p.astype(F32), (-1,))
        tile = SUBLANES * LANES
        padded = -(-flat.shape[0] // tile) * tile
        rows.append(jnp.reshape(jnp.pad(flat, (0, padded - flat.shape[0])), (-1, LANES)))
    return jnp.concatenate(rows, axis=0)


def _unpack(pack, shapes):
    out, row = [], 0
    for shape in shapes:
        size = int(np.prod(shape))
        nrows = -(-size // (SUBLANES * LANES)) * SUBLANES
        out.append(jnp.reshape(jnp.reshape(pack[row:row + nrows], (-1,))[:size], shape))
        row += nrows
    return out


def _bias_tables():
    qi = np.arange(BLOCK)[:, None]
    kj = np.arange(2 * BLOCK)[None, :]
    dist = qi + BLOCK - kj
    in_window = (dist >= 0) & (dist < BLOCK)
    n = np.clip(dist, 0, None)
    max_exact = N_BUCKETS // 2
    nf = np.maximum(n, 1).astype(np.float32)
    large = max_exact + (np.log(nf / max_exact) / math.log(MAX_DISTANCE / max_exact)
                         * (N_BUCKETS - max_exact)).astype(np.int32)
    large = np.minimum(large, N_BUCKETS - 1)
    bucket = np.where(n < max_exact, n, large)
    onehot = (bucket[None] == np.arange(N_BUCKETS)[:, None, None]) & in_window[None]
    first = in_window & (kj >= BLOCK)
    return onehot.astype(np.float32), in_window, first


def kernel(x, c, w_ada, b_ada, g_ffn1, w1_ffn1, w3_ffn1, w2_ffn1, g_mix, w_in, spatial_w, spatial_b, g_v, g_q, g_k, sinks, rel_bias, w_out, g_ffn2, w1_ffn2, w3_ffn2, w2_ffn2, loss_target, m_w_ada, m_b_ada, m_g_ffn1, m_w1_ffn1, m_w3_ffn1, m_w2_ffn1, m_g_mix, m_w_in, m_spatial_w, m_spatial_b, m_g_v, m_g_q, m_g_k, m_sinks, m_rel_bias, m_w_out, m_g_ffn2, m_w1_ffn2, m_w3_ffn2, m_w2_ffn2, v_w_ada, v_b_ada, v_g_ffn1, v_w1_ffn1, v_w3_ffn1, v_w2_ffn1, v_g_mix, v_w_in, v_spatial_w, v_spatial_b, v_g_v, v_g_q, v_g_k, v_sinks, v_rel_bias, v_w_out, v_g_ffn2, v_w1_ffn2, v_w3_ffn2, v_w2_ffn2):
    ax, ay, ac = lax.axis_index("x"), lax.axis_index("y"), lax.axis_index("c")
    chip = 2 * ax + ay
    dev = 2 * chip + ac
    xs = x[0]
    tgt = loss_target[0]
    S, D = xs.shape
    F = N_CHIPS * w1_ffn1.shape[2]
    mod_cols = w_ada.shape[2]

    chip_arr = jnp.reshape(chip, (1,)).astype(jnp.int32)
    c_arr = jnp.reshape(ac, (1,)).astype(jnp.int32)
    jc_arr = jnp.stack([chip, ac]).astype(jnp.int32)

    c_all = _allgather_small(jnp.pad(c, ((0, SUBLANES - 1), (0, 0))), "gather_c")
    c_all = jnp.pad(c_all[::SUBLANES], ((0, BF16_ROWS - N_DEV), (0, 0)))
    b_sh = lax.dynamic_slice(b_ada, (0, chip * mod_cols), (1, mod_cols))
    mod_part, c_act = _mod_partial(c_all, w_ada[0], b_sh, "mod_partial")
    mod_all = _allgather_small(mod_part[:N_DEV], "gather_mod")
    mod_all = jnp.reshape(mod_all, (N_CHIPS, 2, N_DEV, mod_cols))[:, 0]
    mod = jnp.reshape(lax.dynamic_index_in_dim(mod_all, dev, axis=1, keepdims=False), (1, N_MOD * D))
    sh1, sc1, gt1, sh2, sc2, gt2, sh3, sc3, gt3 = [mod[:, i * D:(i + 1) * D] for i in range(N_MOD)]

    def cols_to_natural(w4):
        return jnp.reshape(jnp.transpose(w4, (1, 0, 2)), (w4.shape[1], -1))

    cast = lambda w, nm, after: _cast_to_slot(w[0], chip_arr, f"cast_{nm}", after=after)
    ffn1_bufs = [cast(w1_ffn1, "w1_ffn1", [mod]), cast(w3_ffn1, "w3_ffn1", [mod]), cast(w2_ffn1, "w2_ffn1", [mod])]
    near = _ex_gather_near(ffn1_bufs)
    state, started = _exchange_start(near, "gather_ffn1_near_start")
    mixer_bufs = [cast(w_in, "w_in", [started]), cast(w_out, "w_out", [started])]
    ffn2_bufs = [cast(w1_ffn2, "w1_ffn2", [started]), cast(w3_ffn2, "w3_ffn2", [started]),
                 cast(w2_ffn2, "w2_ffn2", [started])]
    ffn1_bufs = _exchange_wait(near, state, mixer_bufs + ffn2_bufs, "gather_ffn1_near_wait")
    relay = _ex_gather_relay(ffn1_bufs)
    state, started = _exchange_start(relay, "gather_ffn1_relay_start")
    h1 = _norm_mod(xs, g_ffn1, sh1, sc1, "ffn1_norm", after=[started])
    ffn1_bufs = _exchange_wait(relay, state, [h1], "gather_ffn1_relay_wait")
    ffn1_bufs = _exchange_alone(_ex_gather_d2d(ffn1_bufs), "gather_ffn1_d2d")
    w1a, w3a, w2a = cols_to_natural(ffn1_bufs[0]), cols_to_natural(ffn1_bufs[1]), jnp.reshape(ffn1_bufs[2], (F, D))

    onehot_np, in_window_np, first_np = _bias_tables()
    onehot = jnp.asarray(onehot_np)
    bias = jnp.einsum("bij,bh->hij", onehot, rel_bias, precision=lax.Precision.HIGHEST)
    biasm = jnp.stack([jnp.where(jnp.asarray(first_np)[None], bias, NEG),
                       jnp.where(jnp.asarray(in_window_np)[None], bias, NEG)])
    causal = jnp.asarray(np.tril(np.ones((BLOCK, BLOCK), dtype=bool)))
    wm = jnp.where(causal[None], spatial_w[0], 0.0).astype(BF16)
    wm_t = jnp.transpose(wm, (0, 2, 1))
    sink_vec = sinks[0]
    per_group = PAIRS // KV_HEADS
    sbp = jnp.broadcast_to(jnp.reshape(spatial_b[0], (A_HEADS * BLOCK, 1)), (A_HEADS * BLOCK, A_DIM))
    gvp = jnp.repeat(g_v[0], BLOCK, axis=0)
    gq2, gk2 = jnp.concatenate([g_q, g_q], axis=1), jnp.concatenate([g_k, g_k], axis=1)
    seg_ones = jnp.asarray(np.kron(np.eye(2, dtype=np.float32), np.ones((HEAD_DIM, HEAD_DIM), np.float32)), BF16)
    lane_ones = jnp.full((LANES, LANES), 1.0 / LANES, BF16)
    pair_fold = jnp.asarray(np.kron(np.eye(per_group, LANES, dtype=np.float32), np.ones((BLOCK, 1), np.float32)))
    biasp = jnp.reshape(jnp.transpose(jnp.reshape(biasm, (2, KV_HEADS, per_group, 2, BLOCK, BAND)), (0, 1, 3, 5, 2, 4)),
                        (2, KV_HEADS, 2 * BAND, PAIR_ROWS))

    res = _ffn_fwd(xs, g_ffn1, sh1, sc1, gt1, w1a, w3a, w2a, None, "ffn1_fwd", ex=_ex_gather_ici(mixer_bufs + ffn2_bufs),
                   h=h1)
    (x1, a1, b1, f1), mixer_bufs, ffn2_bufs = res[:4], res[4:6], res[6:]
    h2, *mixer_bufs = _norm_mod(x1, g_mix, sh2, sc2, "mixer_norm", ex=_ex_gather_d2d(mixer_bufs))
    win, wout = cols_to_natural(mixer_bufs[0]), jnp.reshape(mixer_bufs[1], (-1, D))
    z, *ffn2_bufs = _matmul(h2, win, "nn", F32, 1024, _tile(IN_COLS, 1664), D, "mixer_in", ex=_ex_gather_d2d(ffn2_bufs))
    w1b, w3b, w2b = cols_to_natural(ffn2_bufs[0]), cols_to_natural(ffn2_bufs[1]), jnp.reshape(ffn2_bufs[2], (F, D))
    mix = _mixer_fwd(z, wm, sbp, gvp, gq2, gk2, seg_ones, lane_ones, sink_vec, biasp, "mixer_fwd")
    x2, ymix = _mixer_out(mix, wout, x1, gt2, "mixer_out")
    g3, df3, h3, a3, b3, dgt3, loss_sum = _ffn_fwd(x2, g_ffn2, sh3, sc3, gt3, w1b, w3b, w2b, tgt, "ffn2_fwd_loss")
    loss = lax.psum(loss_sum[0, 0] * (0.5 / D), ("x", "y", "c"))

    tk = _tile(S, 2048)

    def ffn_weight_grads(h, da, db, s, df, tag, riding):
        ex, done = _ride(riding) if riding else (None, None)
        gw1 = _matmul(h, da, "tn", F32, 1024, F // N_CHIPS, tk, f"grad_w1_{tag}", shard_major=True, ex=ex)
        if riding:
            done(gw1[1:])
            gw1 = gw1[0]
        r1 = _Reduction(gw1, f"w1_{tag}", c_arr, jc_arr)
        ex, done = _ride([r1])
        gw3, *carried = _matmul(h, db, "tn", F32, 1024, F // N_CHIPS, tk, f"grad_w3_{tag}", shard_major=True, ex=ex)
        done(carried)
        r3 = _Reduction(gw3, f"w3_{tag}", c_arr, jc_arr)
        ex, done = _ride([r1, r3])
        gw2, *carried = _matmul(s, df, "tn", F32, _tile(F, 1408), 1024, tk, f"grad_w2_{tag}", ex=ex)
        done(carried)
        r2 = _Reduction(jnp.reshape(gw2, (N_CHIPS, F // N_CHIPS, D)), f"w2_{tag}", c_arr, jc_arr)
        return r1, r3, r2

    da3, db3, s3, dh3 = _ffn_bwd(df3, a3, b3, w1b, w3b, w2b, "ffn2_bwd")
    r21, r23, r22 = ffn_weight_grads(h3, da3, db3, s3, df3, "ffn2", [])
    ex, done = _ride([r22])
    res = _norm_bwd(dh3, x2, g3, g_ffn2, sc3, (ymix, gt2, 1.0), "ffn2_norm_bwd", ex=ex)
    g2, dsh3, dsc3, dgn3, dy, dgt2 = res[:6]
    done(res[6:])

    ex, done = _ride([r21])
    dmix, *carried = _matmul(dy, wout, "nt", BF16, 1024, 2048, D, "mixer_out_bwd", ex=ex)
    done(carried)
    ex, done = _ride([r23, r22])
    res = _mixer_bwd(z, dmix, wm, wm_t, sbp, gvp, gq2, gk2, seg_ones, lane_ones, sink_vec, biasp, pair_fold,
                     "mixer_bwd", ex=ex)
    dz_main, dz_kv, dwm, dsb, dgv, dgq, dgk, dsk, dst = res[:9]
    dsb = jnp.reshape(dsb[:, 0], (A_HEADS, BLOCK))
    dgq, dgk = dgq[:, :HEAD_DIM], dgk[:, :HEAD_DIM]
    dsk = jnp.reshape(jnp.transpose(jnp.reshape(dsk[:2 * KV_HEADS, :per_group], (KV_HEADS, 2, per_group)), (0, 2, 1)),
                      (1, B_HEADS))
    dst = jnp.reshape(jnp.transpose(jnp.reshape(dst, (KV_HEADS, 2, BAND, per_group, BLOCK)), (0, 3, 1, 4, 2)),
                      (B_HEADS, BLOCK * BAND))
    done(res[9:])
    dz = jnp.concatenate([dz_main, dz_kv], axis=1)
    ex, done = _ride([r23, r22])
    dh2, *carried = _matmul(dz, win, "nt", F32, 1024, 2048, _tile(IN_COLS, 1664), "mixer_in_bwd", ex=ex)
    done(carried)
    drel = _bucket_sum(dst, jnp.reshape(onehot, (N_BUCKETS, -1)), "bucket_sum")
    g1, dsh2, dsc2, dgn2, df1, dgt1 = _norm_bwd(dh2, x1, g2, g_mix, sc2, (f1, gt1, 0.5), "mixer_norm_bwd")

    da1, db1, s1, dh1 = _ffn_bwd(df1, a1, b1, w1a, w3a, w2a, "ffn1_bwd")
    r11, r13, r12 = ffn_weight_grads(h1, da1, db1, s1, df1, "ffn1", [])
    ex, done = _ride([r11, r13, r12])
    gwin_full, *carried = _matmul(h2, dz, "tn", F32, 1024, _tile(IN_COLS, 1664), tk, "grad_w_in", ex=ex)
    done(carried)
    rm_in = _Reduction(jnp.transpose(jnp.reshape(gwin_full, (D, N_CHIPS, -1)), (1, 0, 2)), "w_in", c_arr, jc_arr)
    ex, done = _ride([r13, r12, rm_in])
    state, started = _exchange_start(ex, "reduce_late_start")
    gwout_full = _matmul(mix, dy, "tn", F32, 1024, 1024, tk, "grad_w_out", after=[started])
    grad_x, dsh1, dsc1, dgn1 = _norm_bwd(dh1, xs, g1, g_ffn1, sc1, None, "ffn1_norm_bwd", after=[started])
    done(_exchange_wait(ex, state, [gwout_full, grad_x], "reduce_late_wait"))
    rm_out = _Reduction(jnp.reshape(gwout_full, (N_CHIPS, -1, D)), "w_out", c_arr, jc_arr)

    dmod = jnp.concatenate([dsh1, dsc1, dgt1, dsh2, dsc2, dgt2, dsh3, dsc3, dgt3], axis=1)
    small_w = [b_ada, g_ffn1, g_mix, g_ffn2, spatial_w, spatial_b, g_v, g_q, g_k, sinks, rel_bias]
    small_m = [m_b_ada, m_g_ffn1, m_g_mix, m_g_ffn2, m_spatial_w, m_spatial_b, m_g_v, m_g_q, m_g_k, m_sinks, m_rel_bias]
    small_v = [v_b_ada, v_g_ffn1, v_g_mix, v_g_ffn2, v_spatial_w, v_spatial_b, v_g_v, v_g_q, v_g_k, v_sinks, v_rel_bias]
    small_g = [dmod, dgn1, dgn2, dgn3, jnp.where(causal[None], dwm, 0.0), dsb, dgv, dgq, dgk, dsk, drel]
    shapes = [w.shape for w in small_w]
    gpack = _pack(small_g)
    rows = gpack.shape[0]
    gall = jnp.reshape(_allgather_small(gpack, "gather_small"), (N_DEV, rows, LANES))
    sg, sd, sm, sv = _small_update(gall, _pack(small_w), _pack(small_m), _pack(small_v), "small_update")
    sg, sd, sm, sv = [_unpack(p, shapes) for p in (sg, sd, sm, sv)]

    mod_rows = -(-N_MOD * D // (SUBLANES * LANES)) * SUBLANES
    dmod_all = jnp.reshape(gall[:, :mod_rows], (N_DEV, -1))[:, :N_MOD * D]
    dmod_sh = lax.dynamic_slice(dmod_all, (0, chip * mod_cols), (N_DEV, mod_cols))
    dmod_sh = jnp.pad(dmod_sh, ((0, BF16_ROWS - N_DEV), (0, 0))).astype(BF16)
    g_wada = _matmul(c_act, dmod_sh, "tn", F32, 1024, _tile(mod_cols, 512), BF16_ROWS, "grad_w_ada")

    big = {}

    def update(nm, w, g, m, v, token):
        g_out, d, nm_, nv_ = _adamw(w[0], g, m[0], v[0], f"adamw_{nm}", emit_grad=True, after=[token])
        big[nm] = (g_out[None], d[None], nm_[None], nv_[None])
        return d

    ex, done = _ride([r12, rm_in, rm_out])
    state, token = _exchange_start(ex, "reduce_tail_0_start")
    behind = [update("w1_ffn2", w1_ffn2, r21.result, m_w1_ffn2, v_w1_ffn2, token),
              update("w3_ffn2", w3_ffn2, r23.result, m_w3_ffn2, v_w3_ffn2, token),
              update("w2_ffn2", w2_ffn2, r22.result, m_w2_ffn2, v_w2_ffn2, token)]
    done(_exchange_wait(ex, state, behind, "reduce_tail_0_wait"))
    ex, done = _ride([rm_in, rm_out])
    state, token = _exchange_start(ex, "reduce_tail_1_start")
    behind = [update("w1_ffn1", w1_ffn1, r11.result, m_w1_ffn1, v_w1_ffn1, token),
              update("w3_ffn1", w3_ffn1, r13.result, m_w3_ffn1, v_w3_ffn1, token),
              update("w2_ffn1", w2_ffn1, r12.result, m_w2_ffn1, v_w2_ffn1, token)]
    done(_exchange_wait(ex, state, behind, "reduce_tail_1_wait"))
    ex, done = _ride([rm_out])
    state, token = _exchange_start(ex, "reduce_tail_2_start")
    d_wada, nm_wada, nv_wada = _adamw(w_ada[0], g_wada, m_w_ada[0], v_w_ada[0], "adamw_w_ada", after=[token])
    behind = [d_wada, update("w_in", w_in, rm_in.result, m_w_in, v_w_in, token)]
    done(_exchange_wait(ex, state, behind, "reduce_tail_2_wait"))
    update("w_out", w_out, rm_out.result, m_w_out, v_w_out, token)
    big["w_ada"] = (g_wada[None], d_wada[None], nm_wada[None], nv_wada[None])

    order = ["w_ada", "b_ada", "g_ffn1", "w1_ffn1", "w3_ffn1", "w2_ffn1", "g_mix", "w_in", "spatial_w", "spatial_b",
             "g_v", "g_q", "g_k", "sinks", "rel_bias", "w_out", "g_ffn2", "w1_ffn2", "w3_ffn2", "w2_ffn2"]
    small_names = ["b_ada", "g_ffn1", "g_mix", "g_ffn2", "spatial_w", "spatial_b", "g_v", "g_q", "g_k", "sinks", "rel_bias"]
    for i, nm in enumerate(small_names):
        big[nm] = (sg[i], sd[i], sm[i], sv[i])
    outs = [loss, grad_x[None]]
    for kind in range(4):
        outs += [big[nm][kind] for nm in order]
    return tuple(outs)
```

```python
import functools
import math

import jax
import jax.numpy as jnp
import numpy as np
from jax import lax
from jax.experimental import pallas as pl
from jax.experimental.pallas import tpu as pltpu

F32 = jnp.float32
BF16 = jnp.bfloat16
MESH = pl.DeviceIdType.MESH
ANY = pl.BlockSpec(memory_space=pl.ANY)

EPS = 1e-6
BLOCK = 128
A_HEADS = 8
A_DIM = 128
A_WIDTH = A_HEADS * A_DIM
B_HEADS = 16
KV_HEADS = 2
GROUP = B_HEADS // KV_HEADS
HEAD_DIM = 64
B_WIDTH = B_HEADS * HEAD_DIM
KV_WIDTH = KV_HEADS * HEAD_DIM
Q_OFF = 2 * A_WIDTH
K_OFF = Q_OFF + B_WIDTH
V_OFF = K_OFF + KV_WIDTH
IN_COLS = V_OFF + KV_WIDTH
N_BUCKETS = 32
MAX_DISTANCE = 128
N_MOD = 9
N_CHIPS = 4
N_DEV = 8
NEG = -1e30

ADAM_LR = 0.001
ADAM_B1 = 0.9
ADAM_B2 = 0.999
ADAM_EPS = 1e-08
ADAM_WD = 0.01
ADAM_STEP = 10

LANES = 128
SUBLANES = 8
BF16_ROWS = 16
VMEM_LIMIT = 60 * 1024 * 1024

INV_SQRT2 = 1.0 / math.sqrt(2.0)
INV_SQRT_2PI = 1.0 / math.sqrt(2.0 * math.pi)


def _tile(n, pref, mult=LANES):
    t = (min(pref, n) // mult) * mult
    while t >= mult:
        if n % t == 0:
            return t
        t -= mult
    return n


def _params(sem):
    return pltpu.CompilerParams(dimension_semantics=sem, vmem_limit_bytes=VMEM_LIMIT)


class _Exchange:
    def __init__(self, operands, out_shapes, aliases, n_sems, plan):
        self.operands, self.out_shapes, self.aliases, self.n_sems, self.plan = operands, out_shapes, aliases, n_sems, plan


def _pallas(body, *, name, grid, in_specs, out_specs, out_shape, args, scratch_shapes=(), semantics=None, ex=None,
            after=()):
    if ex is None:
        n_in = len(in_specs)

        def ordered(*refs):
            body(*refs[:n_in], *refs[n_in + len(after):])

        return pl.pallas_call(ordered if after else body, name=name, grid=grid,
                              in_specs=list(in_specs) + [ANY] * len(after), out_specs=out_specs, out_shape=out_shape,
                              scratch_shapes=list(scratch_shapes), compiler_params=_params(semantics))(*args, *after)
    assert not after
    n_in, n_out, n_scr = len(in_specs), len(out_specs), len(scratch_shapes)
    e_in, e_out = len(ex.operands), len(ex.out_shapes)

    def wrapped(*refs):
        ins, refs = refs[:n_in], refs[n_in:]
        ex_ins, refs = refs[:e_in], refs[e_in:]
        outs, refs = refs[:n_out], refs[n_out:]
        ex_outs, refs = refs[:e_out], refs[e_out:]
        scratch, (send_sems, recv_sems) = refs[:n_scr], refs[n_scr:]
        first, last = True, True
        for d, size in enumerate(grid):
            first = jnp.logical_and(first, pl.program_id(d) == 0)
            last = jnp.logical_and(last, pl.program_id(d) == size - 1)

        def start():
            sends, _ = ex.plan(ex_ins, ex_outs, send_sems, recv_sems)
            for cp in sends:
                _remote(*cp).start()

        def finish():
            sends, arrivals = ex.plan(ex_ins, ex_outs, send_sems, recv_sems)
            for cp in arrivals:
                _remote(*cp).wait_recv()
            for cp in sends:
                _remote(*cp).wait_send()

        if grid:
            pl.when(first)(start)
        else:
            start()
        if body is not None:
            body(*ins, *outs, *scratch)
        if grid:
            pl.when(last)(finish)
        else:
            finish()

    kwargs = dict(grid=grid) if grid else {}
    return pl.pallas_call(
        wrapped,
        name=name,
        in_specs=list(in_specs) + [ANY] * e_in,
        out_specs=list(out_specs) + [ANY] * e_out,
        out_shape=list(out_shape) + list(ex.out_shapes),
        input_output_aliases={n_in + i: n_out + o for i, o in ex.aliases.items()},
        scratch_shapes=list(scratch_shapes) + [pltpu.SemaphoreType.DMA((ex.n_sems,)), pltpu.SemaphoreType.DMA((ex.n_sems,))],
        compiler_params=_params(("arbitrary",) * len(grid) if grid else None),
        **kwargs,
    )(*args, *ex.operands)


def _dot(a, b, dims=(((1,), (0,)), ((), ()))):
    return lax.dot_general(a, b, dims, preferred_element_type=F32)


NN = (((1,), (0,)), ((), ()))
NT = (((1,), (1,)), ((), ()))
TN = (((0,), (0,)), ((), ()))


def _sigmoid(x):
    return 1.0 / (1.0 + jnp.exp(-x))


def _gelu_and_grad(x):
    cdf = 0.5 * (1.0 + lax.erf(x * INV_SQRT2))
    pdf = jnp.exp(-0.5 * x * x) * INV_SQRT_2PI
    return x * cdf, cdf + x * pdf


def _gelu(x):
    return x * (0.5 * (1.0 + lax.erf(x * INV_SQRT2)))


def _rms(x):
    r = lax.rsqrt(jnp.mean(x * x, axis=-1, keepdims=True) + EPS)
    return x * r, r


ROW_CHUNK = 64


def _for_rows(tm, fn):
    rc = min(ROW_CHUNK, tm)

    def step(r, carry):
        fn(pl.ds(pl.multiple_of(r * rc, rc), rc))
        return carry

    lax.fori_loop(0, tm // rc, step, 0)


def _rms_bwd(dy, xhat, r):
    return r * (dy - xhat * jnp.mean(dy * xhat, axis=-1, keepdims=True))


def _matmul(a, b, mode, out_dtype, tm, tn, tk, name, shard_major=False, ex=None, after=()):
    if mode == "nn":
        (M, K), N = a.shape, b.shape[1]
    elif mode == "nt":
        (M, K), N = a.shape, b.shape[0]
    else:
        (K, M), N = a.shape, b.shape[1]
    tm, tn, tk = min(tm, M), min(tn, N), min(tk, K)
    assert M % tm == 0 and N % tn == 0 and K % tk == 0, (name, M, N, K, tm, tn, tk)
    nk = K // tk
    dims = {"nn": NN, "nt": NT, "tn": TN}[mode]
    a_spec = pl.BlockSpec((tk, tm), lambda i, j, k: (k, i)) if mode == "tn" else pl.BlockSpec((tm, tk), lambda i, j, k: (i, k))
    b_spec = pl.BlockSpec((tn, tk), lambda i, j, k: (j, k)) if mode == "nt" else pl.BlockSpec((tk, tn), lambda i, j, k: (k, j))
    if shard_major:
        assert tn * N_CHIPS == N
        out_shape = jax.ShapeDtypeStruct((N_CHIPS, M, tn), out_dtype)
        o_spec = pl.BlockSpec((None, tm, tn), lambda i, j, k: (j, i, 0))
    else:
        out_shape = jax.ShapeDtypeStruct((M, N), out_dtype)
        o_spec = pl.BlockSpec((tm, tn), lambda i, j, k: (i, j))

    direct = nk == 1 or out_dtype == F32

    def body(a_ref, b_ref, o_ref, *scratch):
        k = pl.program_id(2)
        if nk == 1:
            o_ref[...] = _dot(a_ref[...], b_ref[...], dims).astype(o_ref.dtype)
            return
        acc_ref = o_ref if direct else scratch[0]

        @pl.when(k == 0)
        def _():
            acc_ref[...] = jnp.zeros(acc_ref.shape, F32)

        acc_ref[...] += _dot(a_ref[...], b_ref[...], dims)
        if not direct:
            @pl.when(k == nk - 1)
            def _():
                o_ref[...] = acc_ref[...].astype(o_ref.dtype)

    outs = _pallas(body, name=name, grid=(M // tm, N // tn, nk), in_specs=[a_spec, b_spec], out_specs=[o_spec],
                   out_shape=[out_shape], scratch_shapes=[] if direct else [pltpu.VMEM((tm, tn), F32)],
                   semantics=("parallel", "parallel", "arbitrary"), args=[a, b], ex=ex, after=after)
    return outs[0] if ex is None else outs


def _mod_partial(c_all, w_ada, b_sh, name):
    R, D = c_all.shape
    N = w_ada.shape[1]
    tn = _tile(N, 512)

    def body(c_ref, w_ref, b_ref, o_ref, ca_ref):
        cv = c_ref[...]
        ca = (cv * _sigmoid(cv)).astype(BF16)
        ca_ref[...] = ca
        o_ref[...] = _dot(ca, w_ref[...].astype(BF16)) + b_ref[...]

    return pl.pallas_call(
        body,
        name=name,
        grid=(N // tn,),
        in_specs=[
            pl.BlockSpec((R, D), lambda j: (0, 0)),
            pl.BlockSpec((D, tn), lambda j: (0, j)),
            pl.BlockSpec((1, tn), lambda j: (0, j)),
        ],
        out_specs=[pl.BlockSpec((R, tn), lambda j: (0, j)), pl.BlockSpec((R, D), lambda j: (0, 0))],
        out_shape=[jax.ShapeDtypeStruct((R, N), F32), jax.ShapeDtypeStruct((R, D), BF16)],
        compiler_params=_params(("arbitrary",)),
    )(c_all, w_ada, b_sh)


FFN_BLOCK = 1024


def _ffn_blocks(F):
    if F % FFN_BLOCK == 0 or F < FFN_BLOCK:
        tf = _tile(F, FFN_BLOCK)
        return tf, F // tf, tf
    nj = -(-F // FFN_BLOCK)
    tail = F - (nj - 1) * FFN_BLOCK
    assert tail % LANES == 0
    return FFN_BLOCK, nj, tail
def _ffn_fwd(x, g, sh, sc, gt, w1, w3, w2, tgt, name, ex=None, h=None):
    S, D = x.shape
    F = w1.shape[1]
    tm, tf, nj, tail = _tile(S, 512), *_ffn_blocks(F)
    ni = S // tm
    with_loss = tgt is not None
    assert h is None or not with_loss

    def body(*refs):
        if with_loss:
            (x_ref, g_ref, sh_ref, sc_ref, gt_ref, w1_ref, w3_ref, w2_ref, tgt_ref,
             gout_ref, df_ref, h_ref, a_ref, b_ref, dgt_ref, loss_ref, acc_ref) = refs
        elif h is None:
            (x_ref, g_ref, sh_ref, sc_ref, gt_ref, w1_ref, w3_ref, w2_ref,
             xo_ref, h_ref, a_ref, b_ref, f_ref, acc_ref) = refs
        else:
            (x_ref, g_ref, sh_ref, sc_ref, gt_ref, w1_ref, w3_ref, w2_ref, h_ref,
             xo_ref, a_ref, b_ref, f_ref, acc_ref) = refs
        i, j = pl.program_id(0), pl.program_id(1)

        if h is None:
            @pl.when(j == 0)
            def _():
                def prologue(rows):
                    xhat, _ = _rms(x_ref[rows, :])
                    h_ref[rows, :] = ((xhat * g_ref[...]) * (1.0 + sc_ref[...]) + sh_ref[...]).astype(BF16)

                _for_rows(tm, prologue)

        @pl.when(j == 0)
        def _():
            acc_ref[...] = jnp.zeros(acc_ref.shape, F32)

        def columns(width):
            def run():
                hb = h_ref[...]
                av = _dot(hb, w1_ref[:, :width])
                bv = _dot(hb, w3_ref[:, :width])
                a_ref[:, :width] = av.astype(BF16)
                b_ref[:, :width] = bv.astype(BF16)
                sv = ((av * _sigmoid(av)) * bv).astype(BF16)
                acc_ref[...] += _dot(sv, w2_ref[:width, :])
            return run

        if tail == tf:
            columns(tf)()
        else:
            pl.when(j < nj - 1)(columns(tf))
            pl.when(j == nj - 1)(columns(tail))

        @pl.when(j == nj - 1)
        def _():
            if with_loss:
                @pl.when(i == 0)
                def _():
                    dgt_ref[...] = jnp.zeros(dgt_ref.shape, F32)
                    loss_ref[...] = jnp.zeros(loss_ref.shape, F32)

            def epilogue(rows):
                fv = acc_ref[rows, :]
                half_gate = 0.5 * gt_ref[...]
                xo = x_ref[rows, :] + half_gate * fv
                if not with_loss:
                    xo_ref[rows, :] = xo
                    f_ref[rows, :] = fv.astype(f_ref.dtype)
                    return
                err = xo - tgt_ref[rows, :]
                gout = err * (1.0 / D)
                gout_ref[rows, :] = gout
                df_ref[rows, :] = (half_gate * gout).astype(BF16)
                dgt_ref[...] += 0.5 * jnp.sum(gout * fv, axis=0, keepdims=True)
                loss_part = jnp.sum(jnp.sum(err * err, axis=1, keepdims=True), axis=0, keepdims=True)
                loss_ref[...] += jnp.broadcast_to(loss_part, loss_ref.shape)

            _for_rows(tm, epilogue)

    row = pl.BlockSpec((tm, D), lambda i, j: (i, 0))
    row_in = pl.BlockSpec((tm, D), lambda i, j: (i, 0), pipeline_mode=pl.Buffered(1))
    vec = pl.BlockSpec((1, D), lambda i, j: (0, 0))
    col = pl.BlockSpec((tm, tf), lambda i, j: (i, j))
    in_specs = [row_in, vec, vec, vec, vec,
                pl.BlockSpec((D, tf), lambda i, j: (0, j)),
                pl.BlockSpec((D, tf), lambda i, j: (0, j)),
                pl.BlockSpec((tf, D), lambda i, j: (j, 0))]
    args = [x, g, sh, sc, gt, w1, w3, w2]
    act = jax.ShapeDtypeStruct((S, F), BF16)
    if with_loss:
        in_specs.append(row_in)
        args.append(tgt)
        out_specs = [row, row, row_in, col, col, vec, pl.BlockSpec((1, LANES), lambda i, j: (0, 0))]
        out_shape = [jax.ShapeDtypeStruct((S, D), F32), jax.ShapeDtypeStruct((S, D), BF16),
                     jax.ShapeDtypeStruct((S, D), BF16), act, act,
                     jax.ShapeDtypeStruct((1, D), F32), jax.ShapeDtypeStruct((1, LANES), F32)]
    elif h is None:
        out_specs = [row, row, col, col, row]
        out_shape = [jax.ShapeDtypeStruct((S, D), F32), jax.ShapeDtypeStruct((S, D), BF16), act, act,
                     jax.ShapeDtypeStruct((S, D), BF16)]
    else:
        in_specs.append(row_in)
        args.append(h)
        out_specs = [row, col, col, row]
        out_shape = [jax.ShapeDtypeStruct((S, D), F32), act, act, jax.ShapeDtypeStruct((S, D), BF16)]
    return _pallas(body, name=name, grid=(ni, nj), in_specs=in_specs, out_specs=out_specs, out_shape=out_shape,
                   scratch_shapes=[pltpu.VMEM((tm, D), F32)],
                   semantics=("arbitrary", "arbitrary"), args=args, ex=ex)


def _ffn_bwd(df, a, b, w1, w3, w2, name, ex=None):
    S, D = df.shape
    F = a.shape[1]
    tm, tf, nj, tail = _tile(S, 512), *_ffn_blocks(F)

    def body(df_ref, a_ref, b_ref, w1_ref, w3_ref, w2_ref, da_ref, db_ref, s_ref, dh_ref):
        j = pl.program_id(1)

        @pl.when(j == 0)
        def _():
            dh_ref[...] = jnp.zeros(dh_ref.shape, F32)

        def columns(width):
            def run():
                ds = _dot(df_ref[...], w2_ref[:width, :], NT)
                av = a_ref[:, :width].astype(F32)
                bv = b_ref[:, :width].astype(F32)
                sig = _sigmoid(av)
                sil = av * sig
                da = ((ds * bv) * (sig * (1.0 + av * (1.0 - sig)))).astype(BF16)
                db = (ds * sil).astype(BF16)
                da_ref[:, :width] = da
                db_ref[:, :width] = db
                s_ref[:, :width] = (sil * bv).astype(BF16)
                dh_ref[...] += _dot(da, w1_ref[:, :width], NT) + _dot(db, w3_ref[:, :width], NT)
            return run

        if tail == tf:
            columns(tf)()
        else:
            pl.when(j < nj - 1)(columns(tf))
            pl.when(j == nj - 1)(columns(tail))

    row = pl.BlockSpec((tm, D), lambda i, j: (i, 0))
    col = pl.BlockSpec((tm, tf), lambda i, j: (i, j))
    act = jax.ShapeDtypeStruct((S, F), BF16)
    return _pallas(body, name=name, grid=(S // tm, nj),
                   in_specs=[row, col, col,
                             pl.BlockSpec((D, tf), lambda i, j: (0, j)),
                             pl.BlockSpec((D, tf), lambda i, j: (0, j)),
                             pl.BlockSpec((tf, D), lambda i, j: (j, 0))],
                   out_specs=[col, col, col, row],
                   out_shape=[act, act, act, jax.ShapeDtypeStruct((S, D), F32)],
                   semantics=("parallel", "arbitrary"), args=[df, a, b, w1, w3, w2], ex=ex)


def _norm_mod(x, g, sh, sc, name, ex=None, after=()):
    S, D = x.shape
    tm = _tile(S, 512)

    def body(x_ref, g_ref, sh_ref, sc_ref, h_ref):
        def step(rows):
            xhat, _ = _rms(x_ref[rows, :])
            h_ref[rows, :] = ((xhat * g_ref[...]) * (1.0 + sc_ref[...]) + sh_ref[...]).astype(BF16)

        _for_rows(tm, step)

    row = pl.BlockSpec((tm, D), lambda i: (i, 0))
    vec = pl.BlockSpec((1, D), lambda i: (0, 0))
    outs = _pallas(body, name=name, grid=(S // tm,), in_specs=[row, vec, vec, vec], out_specs=[row],
                   out_shape=[jax.ShapeDtypeStruct((S, D), BF16)], semantics=("parallel",), args=[x, g, sh, sc], ex=ex,
                   after=after)
    return outs[0] if ex is None else outs


def _norm_bwd(dh, x, gres, g, sc, prev, name, ex=None, after=()):
    S, D = x.shape
    tm = _tile(S, 256)
    has_prev = prev is not None
    coef = prev[2] if has_prev else None

    def body(*refs):
        if has_prev:
            (dh_ref, x_ref, gr_ref, g_ref, sc_ref, f_ref, gt_ref,
             go_ref, dsh_ref, dsc_ref, dg_ref, dp_ref, dgt_ref) = refs
        else:
            dh_ref, x_ref, gr_ref, g_ref, sc_ref, go_ref, dsh_ref, dsc_ref, dg_ref = refs
        sum_refs = [dsh_ref, dsc_ref, dg_ref] + ([dgt_ref] if has_prev else [])

        @pl.when(pl.program_id(0) == 0)
        def _():
            for ref in sum_refs:
                ref[...] = jnp.zeros(ref.shape, F32)

        def step(rows):
            dh = dh_ref[rows, :]
            xhat, r = _rms(x_ref[rows, :])
            gain = g_ref[...]
            scale1 = 1.0 + sc_ref[...]
            gout = gr_ref[rows, :] + _rms_bwd(dh * scale1 * gain, xhat, r)
            go_ref[rows, :] = gout
            sums = [dh, dh * (xhat * gain), dh * scale1 * xhat]
            if has_prev:
                dp_ref[rows, :] = ((coef * gt_ref[...]) * gout).astype(BF16)
                sums.append(coef * (gout * f_ref[rows, :].astype(F32)))
            for ref, v in zip(sum_refs, sums):
                ref[...] += jnp.sum(v, axis=0, keepdims=True)

        _for_rows(tm, step)

    row = pl.BlockSpec((tm, D), lambda i: (i, 0))
    vec = pl.BlockSpec((1, D), lambda i: (0, 0))
    vshape = jax.ShapeDtypeStruct((1, D), F32)
    in_specs = [row, row, row, vec, vec]
    args = [dh, x, gres, g, sc]
    out_specs = [row, vec, vec, vec]
    out_shape = [jax.ShapeDtypeStruct((S, D), F32), vshape, vshape, vshape]
    if has_prev:
        in_specs += [row, vec]
        args += [prev[0], prev[1]]
        out_specs += [row, vec]
        out_shape += [jax.ShapeDtypeStruct((S, D), BF16), vshape]
    return _pallas(body, name=name, grid=(S // tm,), in_specs=in_specs, out_specs=out_specs, out_shape=out_shape,
                   semantics=("arbitrary",), args=args, ex=ex, after=after)


PAIRS = B_HEADS // 2
PAIR_ROWS = (PAIRS // KV_HEADS) * BLOCK
BAND = 2 * BLOCK


def _stack(ref, offset, count):
    return jnp.concatenate([ref[:, offset + p * LANES:offset + (p + 1) * LANES] for p in range(count)], axis=0)


def _seg_mean(x, e_ref):
    return _dot(x.astype(BF16), e_ref[...]) * (1.0 / HEAD_DIM)


def _block_diag(x, x_rolled, left, kv_head):
    if kv_head == 0:
        top, bottom = jnp.where(left, x, 0.0), jnp.where(left, 0.0, x_rolled)
    else:
        top, bottom = jnp.where(left, x_rolled, 0.0), jnp.where(left, 0.0, x)
    return jnp.concatenate([top, bottom], axis=0).astype(BF16)


def _from_block_diag(g, left, kv_head):
    a, b = g[:BAND], g[BAND:]
    if kv_head == 0:
        return jnp.where(left, a + pltpu.roll(b, HEAD_DIM, 1), 0.0)
    return jnp.where(left, 0.0, pltpu.roll(a, HEAD_DIM, 1) + b)


def _pair_softmax(st, sk_ref, kv_head):
    out = []
    for e in range(2):
        seg = st[e * BAND:(e + 1) * BAND]
        sink = jnp.concatenate([jnp.full((1, BLOCK), sk_ref[kv_head * GROUP + 2 * p + e], F32)
                                for p in range(PAIRS // KV_HEADS)], axis=1)
        m = jnp.maximum(jnp.max(seg, axis=0, keepdims=True), sink)
        p_ = jnp.exp(seg - m)
        e_sink = jnp.exp(sink - m)
        inv = 1.0 / (jnp.sum(p_, axis=0, keepdims=True) + e_sink)
        out.append((p_ * inv, e_sink * inv))
    return out


def _lane_mean(x, ones_ref):
    return _dot(x.astype(BF16), ones_ref[...])


def _mixer_specs(nb, last):
    full = lambda shape: pl.BlockSpec(shape, lambda n: (0,) * len(shape))
    z_spec = pl.BlockSpec((BLOCK, IN_COLS), lambda n: (jnp.minimum(n, last), 0))
    zp_spec = pl.BlockSpec((BLOCK, 2 * KV_WIDTH), lambda n: (jnp.clip(n - 1, 0, last), K_OFF // (2 * KV_WIDTH)))
    consts = [full((A_HEADS * BLOCK, A_DIM)), full((A_HEADS * BLOCK, A_DIM)), full((1, LANES)), full((1, LANES)),
              full((LANES, LANES)), full((LANES, LANES)), pl.BlockSpec(memory_space=pltpu.SMEM),
              pl.BlockSpec((None, KV_HEADS, PAIR_ROWS, 2 * BAND), lambda n: (jnp.minimum(n, 1), 0, 0, 0))]
    return full, z_spec, zp_spec, consts


def _mixer_fwd(z, wm, sbp, gvp, gq2, gk2, seg_ones, lane_ones, sinks, biasp, name):
    S = z.shape[0]
    nb = S // BLOCK

    def body(z_ref, zp_ref, wm_ref, sbp_ref, gvp_ref, gq2_ref, gk2_ref, e_ref, l_ref, sk_ref, bias_ref, mix_ref):
        u = _gelu(_stack(z_ref, 0, A_HEADS))
        v = _gelu(_stack(z_ref, A_WIDTH, A_HEADS))
        vhat = v * lax.rsqrt(_lane_mean(v * v, l_ref) + EPS)
        vn = (vhat * gvp_ref[...]).astype(BF16)
        mixed = jnp.concatenate([_dot(wm_ref[h], vn[h * BLOCK:(h + 1) * BLOCK]) for h in range(A_HEADS)], axis=0)
        ya = (u * (mixed + sbp_ref[...])).astype(BF16)
        for h in range(A_HEADS):
            mix_ref[:, h * A_DIM:(h + 1) * A_DIM] = ya[h * BLOCK:(h + 1) * BLOCK]

        left = lax.broadcasted_iota(jnp.int32, (1, LANES), 1) < HEAD_DIM
        kv = jnp.concatenate([zp_ref[...], z_ref[:, K_OFF:K_OFF + 2 * KV_WIDTH]], axis=0)
        k2, v2 = kv[:, :KV_WIDTH], kv[:, KV_WIDTH:]
        kn2 = k2 * lax.rsqrt(_seg_mean(k2 * k2, e_ref) + EPS) * gk2_ref[...]
        kn2_r, v2_r = pltpu.roll(kn2, HEAD_DIM, 1), pltpu.roll(v2, HEAD_DIM, 1)
        qp = _stack(z_ref, Q_OFF, PAIRS)
        qn = (qp * lax.rsqrt(_seg_mean(qp * qp, e_ref) + EPS) * gq2_ref[...]).astype(BF16)
        for kh in range(KV_HEADS):
            kbd, vbd = _block_diag(kn2, kn2_r, left, kh), _block_diag(v2, v2_r, left, kh)
            st = _dot(kbd, qn[kh * PAIR_ROWS:(kh + 1) * PAIR_ROWS], NT) * (HEAD_DIM ** -0.5) + bias_ref[kh]
            wt = jnp.concatenate([w_e for w_e, _ in _pair_softmax(st, sk_ref, kh)], axis=0).astype(BF16)
            o = _dot(wt, vbd, TN).astype(BF16)
            for p in range(PAIRS // KV_HEADS):
                col = A_WIDTH + (kh * (PAIRS // KV_HEADS) + p) * LANES
                mix_ref[:, col:col + LANES] = o[p * BLOCK:(p + 1) * BLOCK]

    full, z_spec, zp_spec, consts = _mixer_specs(nb, nb - 1)
    return pl.pallas_call(
        body,
        name=name,
        grid=(nb,),
        in_specs=[z_spec, zp_spec, full((A_HEADS, BLOCK, BLOCK))] + consts,
        out_specs=pl.BlockSpec((BLOCK, A_WIDTH + B_WIDTH), lambda n: (n, 0)),
        out_shape=jax.ShapeDtypeStruct((S, A_WIDTH + B_WIDTH), BF16),
        compiler_params=_params(("parallel",)),
    )(z, z, wm, sbp, gvp, gq2, gk2, seg_ones, lane_ones, sinks, biasp)


def _mixer_bwd(z, dmix, wm, wm_t, sbp, gvp, gq2, gk2, seg_ones, lane_ones, sinks, biasp, pair_fold, name, ex=None):
    S = z.shape[0]
    nb = S // BLOCK

    def body(z_ref, zp_ref, dmix_ref, wm_ref, wmt_ref, sbp_ref, gvp_ref, gq2_ref, gk2_ref, e_ref, l_ref, sk_ref,
             bias_ref, fold_ref,
             dz_ref, dzkv_ref, dwm_ref, dsb_ref, dgv_ref, dgq_ref, dgk_ref, dsk_ref, dst_ref,
             carry_ref, tot_ref, sbacc_ref, skacc_ref, gqacc_ref, gkacc_ref):
        n = pl.program_id(0)
        left = lax.broadcasted_iota(jnp.int32, (1, LANES), 1) < HEAD_DIM

        @pl.when(n == 0)
        def _():
            for ref in (dwm_ref, dgv_ref, dst_ref, carry_ref, sbacc_ref, skacc_ref, gqacc_ref, gkacc_ref):
                ref[...] = jnp.zeros(ref.shape, ref.dtype)

        @pl.when(n < nb)
        def _():
            u, du_dz = _gelu_and_grad(_stack(z_ref, 0, A_HEADS))
            v, dv_dz = _gelu_and_grad(_stack(z_ref, A_WIDTH, A_HEADS))
            rv = lax.rsqrt(_lane_mean(v * v, l_ref) + EPS)
            vhat = v * rv
            gvp = gvp_ref[...]
            vn = (vhat * gvp).astype(BF16)
            rows = lambda a, h: a[h * BLOCK:(h + 1) * BLOCK]
            mixed = jnp.concatenate([_dot(wm_ref[h], rows(vn, h)) for h in range(A_HEADS)], axis=0) + sbp_ref[...]
            dya = _stack(dmix_ref, 0, A_HEADS).astype(F32)
            dmx = dya * u
            sbacc_ref[...] += dmx
            dmx_b = dmx.astype(BF16)
            for h in range(A_HEADS):
                dwm_ref[h] += _dot(rows(dmx_b, h), rows(vn, h), NT)
            dvn = jnp.concatenate([_dot(wmt_ref[h], rows(dmx_b, h)) for h in range(A_HEADS)], axis=0)
            dgv_ref[...] += jnp.sum(jnp.reshape(dvn * vhat, (A_HEADS, BLOCK, A_DIM)), axis=1)
            dzu = ((dya * mixed) * du_dz).astype(BF16)
            tv = dvn * gvp
            dzv = ((rv * (tv - vhat * _lane_mean(tv * vhat, l_ref))) * dv_dz).astype(BF16)
            for h in range(A_HEADS):
                dz_ref[:, h * A_DIM:(h + 1) * A_DIM] = rows(dzu, h)
                dz_ref[:, A_WIDTH + h * A_DIM:A_WIDTH + (h + 1) * A_DIM] = rows(dzv, h)

            kv = jnp.concatenate([zp_ref[...], z_ref[:, K_OFF:K_OFF + 2 * KV_WIDTH]], axis=0)
            k2, v2 = kv[:, :KV_WIDTH], kv[:, KV_WIDTH:]
            kn2 = k2 * lax.rsqrt(_seg_mean(k2 * k2, e_ref) + EPS) * gk2_ref[...]
            kn2_r, v2_r = pltpu.roll(kn2, HEAD_DIM, 1), pltpu.roll(v2, HEAD_DIM, 1)
            qp = _stack(z_ref, Q_OFF, PAIRS)
            rq = lax.rsqrt(_seg_mean(qp * qp, e_ref) + EPS)
            qhat = qp * rq
            gq2 = gq2_ref[...]
            qn = (qhat * gq2).astype(BF16)
            dop = _stack(dmix_ref, A_WIDTH, PAIRS)
            dqn_parts = []
            dk2n = jnp.zeros((BAND, KV_WIDTH), F32)
            dv2 = jnp.zeros((BAND, KV_WIDTH), F32)
            for kh in range(KV_HEADS):
                kbd, vbd = _block_diag(kn2, kn2_r, left, kh), _block_diag(v2, v2_r, left, kh)
                qg = qn[kh * PAIR_ROWS:(kh + 1) * PAIR_ROWS]
                dog = dop[kh * PAIR_ROWS:(kh + 1) * PAIR_ROWS]
                st = _dot(kbd, qg, NT) * (HEAD_DIM ** -0.5) + bias_ref[kh]
                halves = _pair_softmax(st, sk_ref, kh)
                dpt = _dot(vbd, dog, NT)
                ds_halves, t_halves = [], []
                for e, (w_e, w_sink) in enumerate(halves):
                    dp_e = dpt[e * BAND:(e + 1) * BAND]
                    delta = jnp.sum(w_e * dp_e, axis=0, keepdims=True)
                    ds_halves.append(w_e * (dp_e - delta))
                    t_halves.append(-(w_sink * delta))
                dst = jnp.concatenate(ds_halves, axis=0)
                dst_ref[kh] += dst
                skacc_ref[2 * kh:2 * kh + 2, :] += jnp.concatenate(t_halves, axis=0)
                ds_b = (dst * (HEAD_DIM ** -0.5)).astype(BF16)
                w_b = jnp.concatenate([w_e for w_e, _ in halves], axis=0).astype(BF16)
                dqn_parts.append(_dot(ds_b, kbd, TN))
                dk2n += _from_block_diag(_dot(ds_b, qg), left, kh)
                dv2 += _from_block_diag(_dot(w_b, dog), left, kh)
            dqn = jnp.concatenate(dqn_parts, axis=0)
            gqacc_ref[...] += jnp.sum(dqn * qhat, axis=0, keepdims=True)
            t = dqn * gq2
            dzq = (rq * (t - qhat * _seg_mean(t * qhat, e_ref))).astype(BF16)
            for p in range(PAIRS):
                dz_ref[:, Q_OFF + p * LANES:Q_OFF + (p + 1) * LANES] = rows(dzq, p)
            tot_ref[0] = carry_ref[0] + dk2n[:BLOCK]
            tot_ref[1] = carry_ref[1] + dv2[:BLOCK]
            carry_ref[0] = dk2n[BLOCK:]
            carry_ref[1] = dv2[BLOCK:]

        @pl.when(n == nb)
        def _():
            tot_ref[...] = carry_ref[...]

        kp = zp_ref[:, :KV_WIDTH]
        rk = lax.rsqrt(_seg_mean(kp * kp, e_ref) + EPS)
        khat = kp * rk
        dkn = tot_ref[0]
        gkacc_ref[...] += jnp.sum(dkn * khat, axis=0, keepdims=True)
        t = dkn * gk2_ref[...]
        dzkv_ref[:, :KV_WIDTH] = (rk * (t - khat * _seg_mean(t * khat, e_ref))).astype(BF16)
        dzkv_ref[:, KV_WIDTH:] = tot_ref[1].astype(BF16)

        @pl.when(n == nb)
        def _():
            dsb_ref[...] = jnp.broadcast_to(jnp.sum(sbacc_ref[...], axis=1, keepdims=True), dsb_ref.shape)
            dsk_ref[...] = lax.dot_general(skacc_ref[...], fold_ref[...], NN, precision=lax.Precision.HIGHEST,
                                           preferred_element_type=F32)
            dgq_ref[...] = gqacc_ref[...] + pltpu.roll(gqacc_ref[...], HEAD_DIM, 1)
            dgk_ref[...] = gkacc_ref[...] + pltpu.roll(gkacc_ref[...], HEAD_DIM, 1)

    last = nb - 1
    full, z_spec, zp_spec, consts = _mixer_specs(nb, last)
    return _pallas(
        body,
        name=name,
        grid=(nb + 1,),
        ex=ex,
        in_specs=[z_spec, zp_spec, pl.BlockSpec((BLOCK, A_WIDTH + B_WIDTH), lambda n: (jnp.minimum(n, last), 0)),
                  full((A_HEADS, BLOCK, BLOCK)), full((A_HEADS, BLOCK, BLOCK))] + consts + [full((PAIR_ROWS, LANES))],
        out_specs=[
            pl.BlockSpec((BLOCK, K_OFF), lambda n: (jnp.minimum(n, last), 0)),
            pl.BlockSpec((BLOCK, 2 * KV_WIDTH), lambda n: (jnp.maximum(n - 1, 0), 0)),
            full((A_HEADS, BLOCK, BLOCK)), full((A_HEADS * BLOCK, A_DIM)), full((A_HEADS, A_DIM)),
            full((1, LANES)), full((1, LANES)), full((SUBLANES, LANES)),
            full((KV_HEADS, PAIR_ROWS, 2 * BAND)),
        ],
        out_shape=[
            jax.ShapeDtypeStruct((S, K_OFF), BF16),
            jax.ShapeDtypeStruct((S, 2 * KV_WIDTH), BF16),
            jax.ShapeDtypeStruct((A_HEADS, BLOCK, BLOCK), F32),
            jax.ShapeDtypeStruct((A_HEADS * BLOCK, A_DIM), F32),
            jax.ShapeDtypeStruct((A_HEADS, A_DIM), F32),
            jax.ShapeDtypeStruct((1, LANES), F32),
            jax.ShapeDtypeStruct((1, LANES), F32),
            jax.ShapeDtypeStruct((SUBLANES, LANES), F32),
            jax.ShapeDtypeStruct((KV_HEADS, PAIR_ROWS, 2 * BAND), F32),
        ],
        scratch_shapes=[
            pltpu.VMEM((2, BLOCK, KV_WIDTH), F32),
            pltpu.VMEM((2, BLOCK, KV_WIDTH), F32),
            pltpu.VMEM((A_HEADS * BLOCK, A_DIM), F32),
            pltpu.VMEM((SUBLANES, PAIR_ROWS), F32),
            pltpu.VMEM((1, LANES), F32),
            pltpu.VMEM((1, LANES), F32),
        ],
        semantics=("arbitrary",),
        args=[z, z, dmix, wm, wm_t, sbp, gvp, gq2, gk2, seg_ones, lane_ones, sinks, biasp, pair_fold],
    )


def _mixer_out(mix, w_out, x, gt, name):
    S, D = x.shape
    K = mix.shape[1]
    tm, tn = _tile(S, 1024), _tile(D, 1024)

    def body(m_ref, w_ref, x_ref, gt_ref, xo_ref, y_ref):
        y = _dot(m_ref[...], w_ref[...])
        y_ref[...] = y.astype(BF16)
        xo_ref[...] = x_ref[...] + gt_ref[...] * y

    blk = pl.BlockSpec((tm, tn), lambda j, i: (i, j))
    return pl.pallas_call(
        body,
        name=name,
        grid=(D // tn, S // tm),
        in_specs=[pl.BlockSpec((tm, K), lambda j, i: (i, 0)), pl.BlockSpec((K, tn), lambda j, i: (0, j)),
                  blk, pl.BlockSpec((1, tn), lambda j, i: (0, j))],
        out_specs=[blk, blk],
        out_shape=[jax.ShapeDtypeStruct((S, D), F32), jax.ShapeDtypeStruct((S, D), BF16)],
        compiler_params=_params(("parallel", "parallel")),
    )(mix, w_out, x, gt)


def _bucket_sum(dst, onehot, name):
    def body(d_ref, o_ref, out_ref):
        out_ref[...] = lax.dot_general(o_ref[...], d_ref[...], NT, precision=lax.Precision.HIGHEST,
                                       preferred_element_type=F32)

    return pl.pallas_call(
        body,
        name=name,
        out_shape=jax.ShapeDtypeStruct((N_BUCKETS, B_HEADS), F32),
    )(dst, onehot)


def _adamw_math(w, g, m, v):
    m = ADAM_B1 * m + (1.0 - ADAM_B1) * g
    v = ADAM_B2 * v + (1.0 - ADAM_B2) * (g * g)
    m_hat = m / (1.0 - ADAM_B1 ** ADAM_STEP)
    v_hat = v / (1.0 - ADAM_B2 ** ADAM_STEP)
    delta = -ADAM_LR * (m_hat / (jnp.sqrt(v_hat) + ADAM_EPS) + ADAM_WD * w)
    return delta, m, v


def _adamw(w, g, m, v, name, emit_grad=False, after=()):
    R, C = w.shape
    tr = _tile(R, max(SUBLANES, (1 << 19) // C), SUBLANES)

    def body(w_ref, g_ref, m_ref, v_ref, *out_refs):
        gv = g_ref[...]
        results = _adamw_math(w_ref[...], gv, m_ref[...], v_ref[...])
        for ref, val in zip(out_refs, ((gv,) if emit_grad else ()) + results):
            ref[...] = val

    blk = pl.BlockSpec((tr, C), lambda i: (i, 0))
    shape = jax.ShapeDtypeStruct((R, C), F32)
    n_out = 4 if emit_grad else 3
    return _pallas(body, name=name, grid=(R // tr,), in_specs=[blk] * 4, out_specs=[blk] * n_out,
                   out_shape=[shape] * n_out, semantics=("parallel",), args=[w, g, m, v], after=after)


def _small_update(parts, w, m, v, name):
    R = w.shape[0]

    def body(p_ref, w_ref, m_ref, v_ref, g_ref, d_ref, mo_ref, vo_ref):
        g = p_ref[0]
        for dev in range(1, N_DEV):
            g = g + p_ref[dev]
        g_ref[...] = g
        d, mn, vn = _adamw_math(w_ref[...], g, m_ref[...], v_ref[...])
        d_ref[...] = d
        mo_ref[...] = mn
        vo_ref[...] = vn

    shape = jax.ShapeDtypeStruct((R, LANES), F32)
    return pl.pallas_call(
        body,
        name=name,
        out_shape=[shape] * 4,
        compiler_params=pltpu.CompilerParams(vmem_limit_bytes=VMEM_LIMIT),
    )(parts, w, m, v)


def _place():
    x, y, c = lax.axis_index("x"), lax.axis_index("y"), lax.axis_index("c")
    chips = [(1 - x, y), (x, 1 - y), (1 - x, 1 - y)]
    return x, y, c, chips


def _remote(src, dst, send_sem, recv_sem, to):
    return pltpu.make_async_remote_copy(src_ref=src, dst_ref=dst, send_sem=send_sem, recv_sem=recv_sem,
                                        device_id=to, device_id_type=MESH)


def _allgather_small(block, name):
    m_per, n = block.shape

    def body(x_ref, out_ref, send_sems, recv_sems, local_sem):
        x, y, c, chips = _place()
        me, sibling = (x, y, c), (x, y, 1 - c)

        def rows(px, py, pc):
            return out_ref.at[pl.ds((4 * px + 2 * py + pc) * m_per, m_per), :]

        def copy(k, blk, to, src=None):
            return _remote(rows(*blk) if src is None else src, rows(*blk), send_sems.at[k], recv_sems.at[k], to)

        mine = pltpu.make_async_copy(x_ref, rows(*me), local_sem)
        mine.start()
        first = [copy(0, me, sibling, src=x_ref)]
        first += [copy(1 + j, me, (*chip, c), src=x_ref) for j, chip in enumerate(chips)]
        for cp in first:
            cp.start()
        passed = [copy(4 + j, (*chip, c), sibling) for j, chip in enumerate(chips)]
        for j, chip in enumerate(chips):
            copy(1 + j, (*chip, c), me).wait_recv()
            passed[j].start()
        copy(0, sibling, me).wait_recv()
        for j, chip in enumerate(chips):
            copy(4 + j, (*chip, 1 - c), me).wait_recv()
        for cp in first + passed:
            cp.wait_send()
        mine.wait()

    return pl.pallas_call(
        body,
        name=name,
        out_shape=jax.ShapeDtypeStruct((N_DEV * m_per, n), block.dtype),
        in_specs=[pl.BlockSpec(memory_space=pltpu.VMEM)],
        out_specs=pl.BlockSpec(memory_space=pltpu.VMEM),
        scratch_shapes=[pltpu.SemaphoreType.DMA((7,)), pltpu.SemaphoreType.DMA((7,)), pltpu.SemaphoreType.DMA],
        compiler_params=pltpu.CompilerParams(vmem_limit_bytes=VMEM_LIMIT),
    )(block)


def _half(ref, c, rows):
    start = pl.multiple_of(c * rows, BF16_ROWS)
    if len(ref.shape) == 2:
        return ref.at[pl.ds(start, rows), :]
    return ref.at[:, pl.ds(start, rows), :]


def _same(arrays):
    return [jax.ShapeDtypeStruct(a.shape, a.dtype) for a in arrays], {t: t for t in range(len(arrays))}


def _ex_gather_ici(bufs):
    def plan(ins, outs, send_sems, recv_sems):
        x, y, c, chips = _place()
        sends, arrivals = [], []
        for t, buf in enumerate(bufs):
            rows = buf.shape[1] // 2
            mine = _half(outs[t].at[2 * x + y], c, rows)
            for k, (px, py) in enumerate(chips):
                sems = (send_sems.at[3 * t + k], recv_sems.at[3 * t + k], (px, py, c))
                landed = _half(outs[t].at[2 * px + py], c, rows)
                sends.append((mine, mine, *sems))
                arrivals.append((landed, landed, *sems))
        return sends, arrivals

    shapes, aliases = _same(bufs)
    return _Exchange(bufs, shapes, aliases, 3 * len(bufs), plan)


def _ex_gather_near(bufs):
    def plan(ins, outs, send_sems, recv_sems):
        x, y, c, chips = _place()
        sends, arrivals = [], []
        for t, buf in enumerate(bufs):
            rows = buf.shape[1] // 2
            mine = _half(outs[t].at[2 * x + y], c, rows)
            for k, (px, py) in enumerate(chips[:2]):
                sems = (send_sems.at[2 * t + k], recv_sems.at[2 * t + k], (px, py, c))
                landed = _half(outs[t].at[2 * px + py], c, rows)
                sends.append((mine, mine, *sems))
                arrivals.append((landed, landed, *sems))
        return sends, arrivals

    shapes, aliases = _same(bufs)
    return _Exchange(bufs, shapes, aliases, 2 * len(bufs), plan)


def _ex_gather_relay(bufs):
    def plan(ins, outs, send_sems, recv_sems):
        x, y, c, chips = _place()
        (xn, yn, diag) = chips
        slot = lambda chip: 2 * chip[0] + chip[1]
        sends, arrivals = [], []
        for t, buf in enumerate(bufs):
            quarter = buf.shape[1] // 4

            def piece(chip, q):
                start = pl.multiple_of(c * 2 * quarter + q * quarter, BF16_ROWS)
                return outs[t].at[slot(chip), pl.ds(start, quarter), :]

            for k, (held, to) in enumerate([(xn, yn), (yn, xn)]):
                sems = (send_sems.at[2 * t + k], recv_sems.at[2 * t + k], (*to, c))
                sends.append((piece(held, k), piece(held, k), *sems))
                arrivals.append((piece(diag, k), piece(diag, k), *sems))
        onward = _ex_gather_d2d(bufs, which=(0, 1))
        more_sends, more_arrivals = onward.plan(ins, outs, _Shifted(send_sems, 2 * len(bufs)),
                                                _Shifted(recv_sems, 2 * len(bufs)))
        return sends + more_sends, arrivals + more_arrivals

    shapes, aliases = _same(bufs)
    return _Exchange(bufs, shapes, aliases, 2 * len(bufs) + 3 * len(bufs), plan)


def _ex_gather_d2d(bufs, which=(0, 1, 2)):
    def plan(ins, outs, send_sems, recv_sems):
        x, y, c, chips = _place()
        sends, arrivals = [], []
        for t, buf in enumerate(bufs):
            rows = buf.shape[1] // 2
            for k in which:
                px, py = chips[k]
                sems = (send_sems.at[3 * t + k], recv_sems.at[3 * t + k], (x, y, 1 - c))
                landed = _half(outs[t].at[2 * px + py], c, rows)
                other = _half(outs[t].at[2 * px + py], 1 - c, rows)
                sends.append((landed, landed, *sems))
                arrivals.append((other, other, *sems))
        return sends, arrivals

    shapes, aliases = _same(bufs)
    return _Exchange(bufs, shapes, aliases, 3 * len(bufs), plan)


def _ex_swap_halves(grads):
    def plan(ins, outs, send_sems, recv_sems):
        x, y, c, _ = _place()
        sends = [(_half(ins[t], 1 - c, g.shape[1] // 2), outs[t], send_sems.at[t], recv_sems.at[t], (x, y, 1 - c))
                 for t, g in enumerate(grads)]
        return sends, sends

    shapes = [jax.ShapeDtypeStruct((g.shape[0], g.shape[1] // 2, g.shape[2]), g.dtype) for g in grads]
    return _Exchange(grads, shapes, {}, len(grads), plan)


def _ex_scatter(sums):
    def plan(ins, outs, send_sems, recv_sems):
        x, y, c, chips = _place()
        sends = [(ins[t].at[2 * px + py], outs[t].at[k], send_sems.at[3 * t + k], recv_sems.at[3 * t + k], (px, py, c))
                 for t in range(len(sums)) for k, (px, py) in enumerate(chips)]
        return sends, sends

    shapes = [jax.ShapeDtypeStruct((N_CHIPS - 1,) + s.shape[1:], s.dtype) for s in sums]
    return _Exchange(sums, shapes, {}, 3 * len(sums), plan)


def _ex_join_halves(fulls):
    def plan(ins, outs, send_sems, recv_sems):
        x, y, c, _ = _place()
        sends, arrivals = [], []
        for t, full in enumerate(fulls):
            rows = full.shape[0] // 2
            sems = (send_sems.at[t], recv_sems.at[t], (x, y, 1 - c))
            mine, other = _half(outs[t], c, rows), _half(outs[t], 1 - c, rows)
            sends.append((mine, mine, *sems))
            arrivals.append((other, other, *sems))
        return sends, arrivals

    shapes, aliases = _same(fulls)
    return _Exchange(fulls, shapes, aliases, len(fulls), plan)


class _Shifted:
    def __init__(self, sems, offset):
        self.sems, self.offset = sems, offset

    @property
    def at(self):
        return self

    def __getitem__(self, k):
        return self.sems.at[self.offset + k]


def _combine(exchanges):
    operands, out_shapes, aliases, starts = [], [], {}, []
    n_sems = 0
    for e in exchanges:
        starts.append((len(operands), len(out_shapes), n_sems))
        aliases.update({len(operands) + i: len(out_shapes) + o for i, o in e.aliases.items()})
        operands += list(e.operands)
        out_shapes += list(e.out_shapes)
        n_sems += e.n_sems

    def plan(ins, outs, send_sems, recv_sems):
        sends, arrivals = [], []
        for e, (i0, o0, s0) in zip(exchanges, starts):
            s, a = e.plan(ins[i0:i0 + len(e.operands)], outs[o0:o0 + len(e.out_shapes)],
                          _Shifted(send_sems, s0), _Shifted(recv_sems, s0))
            sends += s
            arrivals += a
        return sends, arrivals

    return _Exchange(operands, out_shapes, aliases, n_sems, plan)


class _Reduction:
    def __init__(self, grad, tag, c_arr, jc_arr):
        self.grad, self.tag, self.c_arr, self.jc_arr, self.stage = grad, tag, c_arr, jc_arr, 0

    def exchange(self):
        if self.stage == 0:
            return _ex_swap_halves([self.grad])
        if self.stage == 1:
            return _ex_scatter([self.sums])
        return _ex_join_halves([self.full])

    def advance(self, landed):
        if self.stage == 0:
            self.recv = landed
            self.sums = _chip_sum(self.grad, landed, self.c_arr, f"chip_sum_{self.tag}")
        elif self.stage == 1:
            self.full = _owner_sum(self.grad, self.recv, landed, self.jc_arr, f"owner_sum_{self.tag}")
        else:
            self.result = landed
        self.stage += 1


def _ride(reductions):
    def done(carried):
        for r, landed in zip(reductions, carried):
            r.advance(landed)

    return _combine([r.exchange() for r in reductions]), done


def _exchange_alone(ex, name):
    return _pallas(None, name=name, grid=(), in_specs=[], out_specs=[], out_shape=[], args=[], ex=ex)


SEM = pl.BlockSpec(memory_space=pltpu.SEMAPHORE)
DATAFLOW = pltpu.SideEffectType.DATAFLOW_SIDE_EFFECTING


def _exchange_start(ex, name):
    e_in, e_out = len(ex.operands), len(ex.out_shapes)
    kept = [i for i in range(e_in) if i not in ex.aliases]

    def body(*refs):
        ins, refs = refs[:e_in], refs[e_in:]
        outs, refs = refs[:e_out], refs[e_out:]
        _, (send_sems, recv_sems, token) = refs[:len(kept)], refs[len(kept):]
        sends, _ = ex.plan(ins, outs, send_sems, recv_sems)
        for cp in sends:
            _remote(*cp).start()
        token[...] = jnp.zeros(token.shape, F32)

    sems = pltpu.SemaphoreType.DMA((ex.n_sems,))
    aliases = dict(ex.aliases)
    aliases.update({i: e_out + k for k, i in enumerate(kept)})
    res = pl.pallas_call(
        body,
        name=name,
        in_specs=[ANY] * e_in,
        out_specs=[ANY] * (e_out + len(kept)) + [SEM, SEM, pl.BlockSpec(memory_space=pltpu.VMEM)],
        out_shape=list(ex.out_shapes) + [jax.ShapeDtypeStruct(ex.operands[i].shape, ex.operands[i].dtype) for i in kept]
        + [sems, sems, jax.ShapeDtypeStruct((SUBLANES, LANES), F32)],
        input_output_aliases=aliases,
        compiler_params=pltpu.CompilerParams(has_side_effects=DATAFLOW),
    )(*ex.operands)
    outs, kept_thru, (send_sems, recv_sems, token) = res[:e_out], res[e_out:e_out + len(kept)], res[e_out + len(kept):]
    operands = list(ex.operands)
    for i, o in ex.aliases.items():
        operands[i] = outs[o]
    for k, i in enumerate(kept):
        operands[i] = kept_thru[k]
    return (operands, outs, send_sems, recv_sems), token


def _exchange_wait(ex, state, after, name):
    operands, outs, send_sems, recv_sems = state
    e_out = len(outs)
    kept = [i for i in range(len(operands)) if i not in ex.aliases]

    def body(*refs):
        sources, refs = refs[:len(kept)], refs[len(kept):]
        landing, refs = refs[:e_out], refs[e_out:]
        ins = [None] * len(operands)
        for k, i in enumerate(kept):
            ins[i] = sources[k]
        sends, arrivals = ex.plan(ins, landing, refs[0], refs[1])
        for cp in arrivals:
            _remote(*cp).wait_recv()
        for cp in sends:
            _remote(*cp).wait_send()

    return pl.pallas_call(
        body,
        name=name,
        in_specs=[ANY] * (len(kept) + e_out) + [SEM, SEM] + [ANY] * len(after),
        out_specs=[ANY] * e_out,
        out_shape=[jax.ShapeDtypeStruct(o.shape, o.dtype) for o in outs],
        input_output_aliases={len(kept) + o: o for o in range(e_out)},
        compiler_params=pltpu.CompilerParams(has_side_effects=DATAFLOW),
    )(*[operands[i] for i in kept], *outs, send_sems, recv_sems, *after)


def _cast_to_slot(w, chip_arr, name, after=()):
    A, B = w.shape
    ta = _tile(A, max(BF16_ROWS, (1 << 19) // B), BF16_ROWS)

    def body(j_ref, w_ref, *rest):
        rest[-1][...] = w_ref[...].astype(BF16)

    return pl.pallas_call(
        body,
        name=name,
        grid_spec=pltpu.PrefetchScalarGridSpec(
            num_scalar_prefetch=1,
            grid=(A // ta,),
            in_specs=[pl.BlockSpec((ta, B), lambda i, j_ref: (i, 0))] + [ANY] * len(after),
            out_specs=pl.BlockSpec((None, ta, B), lambda i, j_ref: (j_ref[0], i, 0)),
        ),
        out_shape=jax.ShapeDtypeStruct((N_CHIPS, A, B), BF16),
        compiler_params=_params(("parallel",)),
    )(chip_arr, w, *after)


def _chip_sum(grad, recv, c_arr, name):
    _, A, B = grad.shape
    hA = A // 2
    ta = _tile(hA, max(BF16_ROWS, (1 << 19) // B), BF16_ROWS)
    nh = hA // ta

    def body(c_ref, g_ref, r_ref, o_ref):
        o_ref[...] = (g_ref[...] + r_ref[...]).astype(BF16)

    return pl.pallas_call(
        body,
        name=name,
        grid_spec=pltpu.PrefetchScalarGridSpec(
            num_scalar_prefetch=1,
            grid=(N_CHIPS, nh),
            in_specs=[pl.BlockSpec((None, ta, B), lambda s, i, c_ref: (s, c_ref[0] * nh + i, 0)),
                      pl.BlockSpec((None, ta, B), lambda s, i, c_ref: (s, i, 0))],
            out_specs=pl.BlockSpec((None, ta, B), lambda s, i, c_ref: (s, i, 0)),
        ),
        out_shape=jax.ShapeDtypeStruct((N_CHIPS, hA, B), BF16),
        compiler_params=_params(("parallel", "parallel")),
    )(c_arr, grad, recv)


def _owner_sum(grad, recv, landed, jc_arr, name):
    _, A, B = grad.shape
    hA = A // 2
    ta = _tile(hA, max(BF16_ROWS, (1 << 19) // B), BF16_ROWS)
    nh = hA // ta

    def body(jc_ref, g_ref, r_ref, l0_ref, l1_ref, l2_ref, o_ref):
        total = g_ref[...] + r_ref[...]
        for ref in (l0_ref, l1_ref, l2_ref):
            total = total + ref[...].astype(F32)
        o_ref[...] = total

    def landed_spec(k):
        return pl.BlockSpec((None, ta, B), lambda i, jc_ref: (k, i, 0))

    return pl.pallas_call(
        body,
        name=name,
        grid_spec=pltpu.PrefetchScalarGridSpec(
            num_scalar_prefetch=1,
            grid=(nh,),
            in_specs=[pl.BlockSpec((None, ta, B), lambda i, jc_ref: (jc_ref[0], jc_ref[1] * nh + i, 0)),
                      pl.BlockSpec((None, ta, B), lambda i, jc_ref: (jc_ref[0], i, 0)),
                      landed_spec(0), landed_spec(1), landed_spec(2)],
            out_specs=pl.BlockSpec((ta, B), lambda i, jc_ref: (jc_ref[1] * nh + i, 0)),
        ),
        out_shape=jax.ShapeDtypeStruct((A, B), F32),
        compiler_params=_params(("parallel",)),
    )(jc_arr, grad, recv, landed, landed, landed)


def _pack(parts):
    rows = []
    for p in parts:
        flat = jnp.reshape(p.astype(F32), (-1,))
        tile = SUBLANES * LANES
        padded = -(-flat.shape[0] // tile) * tile
        rows.append(jnp.reshape(jnp.pad(flat, (0, padded - flat.shape[0])), (-1, LANES)))
    return jnp.concatenate(rows, axis=0)


def _unpack(pack, shapes):
    out, row = [], 0
    for shape in shapes:
        size = int(np.prod(shape))
        nrows = -(-size // (SUBLANES * LANES)) * SUBLANES
        out.append(jnp.reshape(jnp.reshape(pack[row:row + nrows], (-1,))[:size], shape))
        row += nrows
    return out


def _bias_tables():
    qi = np.arange(BLOCK)[:, None]
    kj = np.arange(2 * BLOCK)[None, :]
    dist = qi + BLOCK - kj
    in_window = (dist >= 0) & (dist < BLOCK)
    n = np.clip(dist, 0, None)
    max_exact = N_BUCKETS // 2
    nf = np.maximum(n, 1).astype(np.float32)
    large = max_exact + (np.log(nf / max_exact) / math.log(MAX_DISTANCE / max_exact)
                         * (N_BUCKETS - max_exact)).astype(np.int32)
    large = np.minimum(large, N_BUCKETS - 1)
    bucket = np.where(n < max_exact, n, large)
    onehot = (bucket[None] == np.arange(N_BUCKETS)[:, None, None]) & in_window[None]
    first = in_window & (kj >= BLOCK)
    return onehot.astype(np.float32), in_window, first


def kernel(x, c, w_ada, b_ada, g_ffn1, w1_ffn1, w3_ffn1, w2_ffn1, g_mix, w_in, spatial_w, spatial_b, g_v, g_q, g_k, sinks, rel_bias, w_out, g_ffn2, w1_ffn2, w3_ffn2, w2_ffn2, loss_target, m_w_ada, m_b_ada, m_g_ffn1, m_w1_ffn1, m_w3_ffn1, m_w2_ffn1, m_g_mix, m_w_in, m_spatial_w, m_spatial_b, m_g_v, m_g_q, m_g_k, m_sinks, m_rel_bias, m_w_out, m_g_ffn2, m_w1_ffn2, m_w3_ffn2, m_w2_ffn2, v_w_ada, v_b_ada, v_g_ffn1, v_w1_ffn1, v_w3_ffn1, v_w2_ffn1, v_g_mix, v_w_in, v_spatial_w, v_spatial_b, v_g_v, v_g_q, v_g_k, v_sinks, v_rel_bias, v_w_out, v_g_ffn2, v_w1_ffn2, v_w3_ffn2, v_w2_ffn2):
    ax, ay, ac = lax.axis_index("x"), lax.axis_index("y"), lax.axis_index("c")
    chip = 2 * ax + ay
    dev = 2 * chip + ac
    xs = x[0]
    tgt = loss_target[0]
    S, D = xs.shape
    F = N_CHIPS * w1_ffn1.shape[2]
    mod_cols = w_ada.shape[2]

    chip_arr = jnp.reshape(chip, (1,)).astype(jnp.int32)
    c_arr = jnp.reshape(ac, (1,)).astype(jnp.int32)
    jc_arr = jnp.stack([chip, ac]).astype(jnp.int32)

    c_all = _allgather_small(jnp.pad(c, ((0, SUBLANES - 1), (0, 0))), "gather_c")
    c_all = jnp.pad(c_all[::SUBLANES], ((0, BF16_ROWS - N_DEV), (0, 0)))
    b_sh = lax.dynamic_slice(b_ada, (0, chip * mod_cols), (1, mod_cols))
    mod_part, c_act = _mod_partial(c_all, w_ada[0], b_sh, "mod_partial")
    mod_all = _allgather_small(mod_part[:N_DEV], "gather_mod")
    mod_all = jnp.reshape(mod_all, (N_CHIPS, 2, N_DEV, mod_cols))[:, 0]
    mod = jnp.reshape(lax.dynamic_index_in_dim(mod_all, dev, axis=1, keepdims=False), (1, N_MOD * D))
    sh1, sc1, gt1, sh2, sc2, gt2, sh3, sc3, gt3 = [mod[:, i * D:(i + 1) * D] for i in range(N_MOD)]

    def cols_to_natural(w4):
        return jnp.reshape(jnp.transpose(w4, (1, 0, 2)), (w4.shape[1], -1))

    cast = lambda w, nm, after: _cast_to_slot(w[0], chip_arr, f"cast_{nm}", after=after)
    ffn1_bufs = [cast(w1_ffn1, "w1_ffn1", [mod]), cast(w3_ffn1, "w3_ffn1", [mod]), cast(w2_ffn1, "w2_ffn1", [mod])]
    near = _ex_gather_near(ffn1_bufs)
    state, started = _exchange_start(near, "gather_ffn1_near_start")
    mixer_bufs = [cast(w_in, "w_in", [started]), cast(w_out, "w_out", [started])]
    ffn2_bufs = [cast(w1_ffn2, "w1_ffn2", [started]), cast(w3_ffn2, "w3_ffn2", [started]),
                 cast(w2_ffn2, "w2_ffn2", [started])]
    ffn1_bufs = _exchange_wait(near, state, mixer_bufs + ffn2_bufs, "gather_ffn1_near_wait")
    relay = _ex_gather_relay(ffn1_bufs)
    state, started = _exchange_start(relay, "gather_ffn1_relay_start")
    h1 = _norm_mod(xs, g_ffn1, sh1, sc1, "ffn1_norm", after=[started])
    ffn1_bufs = _exchange_wait(relay, state, [h1], "gather_ffn1_relay_wait")
    ffn1_bufs = _exchange_alone(_ex_gather_d2d(ffn1_bufs, which=(2,)), "gather_ffn1_d2d")
    w1a, w3a, w2a = cols_to_natural(ffn1_bufs[0]), cols_to_natural(ffn1_bufs[1]), jnp.reshape(ffn1_bufs[2], (F, D))

    onehot_np, in_window_np, first_np = _bias_tables()
    onehot = jnp.asarray(onehot_np)
    bias = jnp.einsum("bij,bh->hij", onehot, rel_bias, precision=lax.Precision.HIGHEST)
    biasm = jnp.stack([jnp.where(jnp.asarray(first_np)[None], bias, NEG),
                       jnp.where(jnp.asarray(in_window_np)[None], bias, NEG)])
    causal = jnp.asarray(np.tril(np.ones((BLOCK, BLOCK), dtype=bool)))
    wm = jnp.where(causal[None], spatial_w[0], 0.0).astype(BF16)
    wm_t = jnp.transpose(wm, (0, 2, 1))
    sink_vec = sinks[0]
    per_group = PAIRS // KV_HEADS
    sbp = jnp.broadcast_to(jnp.reshape(spatial_b[0], (A_HEADS * BLOCK, 1)), (A_HEADS * BLOCK, A_DIM))
    gvp = jnp.repeat(g_v[0], BLOCK, axis=0)
    gq2, gk2 = jnp.concatenate([g_q, g_q], axis=1), jnp.concatenate([g_k, g_k], axis=1)
    seg_ones = jnp.asarray(np.kron(np.eye(2, dtype=np.float32), np.ones((HEAD_DIM, HEAD_DIM), np.float32)), BF16)
    lane_ones = jnp.full((LANES, LANES), 1.0 / LANES, BF16)
    pair_fold = jnp.asarray(np.kron(np.eye(per_group, LANES, dtype=np.float32), np.ones((BLOCK, 1), np.float32)))
    biasp = jnp.reshape(jnp.transpose(jnp.reshape(biasm, (2, KV_HEADS, per_group, 2, BLOCK, BAND)), (0, 1, 3, 5, 2, 4)),
                        (2, KV_HEADS, 2 * BAND, PAIR_ROWS))

    res = _ffn_fwd(xs, g_ffn1, sh1, sc1, gt1, w1a, w3a, w2a, None, "ffn1_fwd", ex=_ex_gather_ici(mixer_bufs + ffn2_bufs),
                   h=h1)
    (x1, a1, b1, f1), mixer_bufs, ffn2_bufs = res[:4], res[4:6], res[6:]
    h2, *mixer_bufs = _norm_mod(x1, g_mix, sh2, sc2, "mixer_norm", ex=_ex_gather_d2d(mixer_bufs))
    win, wout = cols_to_natural(mixer_bufs[0]), jnp.reshape(mixer_bufs[1], (-1, D))
    z, *ffn2_bufs = _matmul(h2, win, "nn", F32, 1024, _tile(IN_COLS, 1664), D, "mixer_in", ex=_ex_gather_d2d(ffn2_bufs))
    w1b, w3b, w2b = cols_to_natural(ffn2_bufs[0]), cols_to_natural(ffn2_bufs[1]), jnp.reshape(ffn2_bufs[2], (F, D))
    mix = _mixer_fwd(z, wm, sbp, gvp, gq2, gk2, seg_ones, lane_ones, sink_vec, biasp, "mixer_fwd")
    x2, ymix = _mixer_out(mix, wout, x1, gt2, "mixer_out")
    g3, df3, h3, a3, b3, dgt3, loss_sum = _ffn_fwd(x2, g_ffn2, sh3, sc3, gt3, w1b, w3b, w2b, tgt, "ffn2_fwd_loss")
    loss = lax.psum(loss_sum[0, 0] * (0.5 / D), ("x", "y", "c"))

    tk = _tile(S, 2048)

    def ffn_weight_grads(h, da, db, s, df, tag, riding):
        ex, done = _ride(riding) if riding else (None, None)
        gw1 = _matmul(h, da, "tn", F32, 1024, F // N_CHIPS, tk, f"grad_w1_{tag}", shard_major=True, ex=ex)
        if riding:
            done(gw1[1:])
            gw1 = gw1[0]
        r1 = _Reduction(gw1, f"w1_{tag}", c_arr, jc_arr)
        ex, done = _ride([r1])
        gw3, *carried = _matmul(h, db, "tn", F32, 1024, F // N_CHIPS, tk, f"grad_w3_{tag}", shard_major=True, ex=ex)
        done(carried)
        r3 = _Reduction(gw3, f"w3_{tag}", c_arr, jc_arr)
        ex, done = _ride([r1, r3])
        gw2, *carried = _matmul(s, df, "tn", F32, _tile(F, 1408), 1024, tk, f"grad_w2_{tag}", ex=ex)
        done(carried)
        r2 = _Reduction(jnp.reshape(gw2, (N_CHIPS, F // N_CHIPS, D)), f"w2_{tag}", c_arr, jc_arr)
        return r1, r3, r2

    da3, db3, s3, dh3 = _ffn_bwd(df3, a3, b3, w1b, w3b, w2b, "ffn2_bwd")
    r21, r23, r22 = ffn_weight_grads(h3, da3, db3, s3, df3, "ffn2", [])
    ex, done = _ride([r21, r22])
    state, started = _exchange_start(ex, "reduce_ffn2_start")
    g2, dsh3, dsc3, dgn3, dy, dgt2 = _norm_bwd(dh3, x2, g3, g_ffn2, sc3, (ymix, gt2, 1.0), "ffn2_norm_bwd",
                                               after=[started])
    dmix = _matmul(dy, wout, "nt", BF16, 1024, 2048, D, "mixer_out_bwd")
    done(_exchange_wait(ex, state, [dmix], "reduce_ffn2_wait"))
    ex, done = _ride([r23, r22])
    res = _mixer_bwd(z, dmix, wm, wm_t, sbp, gvp, gq2, gk2, seg_ones, lane_ones, sink_vec, biasp, pair_fold,
                     "mixer_bwd", ex=ex)
    dz_main, dz_kv, dwm, dsb, dgv, dgq, dgk, dsk, dst = res[:9]
    dsb = jnp.reshape(dsb[:, 0], (A_HEADS, BLOCK))
    dgq, dgk = dgq[:, :HEAD_DIM], dgk[:, :HEAD_DIM]
    dsk = jnp.reshape(jnp.transpose(jnp.reshape(dsk[:2 * KV_HEADS, :per_group], (KV_HEADS, 2, per_group)), (0, 2, 1)),
                      (1, B_HEADS))
    dst = jnp.reshape(jnp.transpose(jnp.reshape(dst, (KV_HEADS, 2, BAND, per_group, BLOCK)), (0, 3, 1, 4, 2)),
                      (B_HEADS, BLOCK * BAND))
    done(res[9:])
    dz = jnp.concatenate([dz_main, dz_kv], axis=1)
    ex, done = _ride([r23, r22])
    dh2, *carried = _matmul(dz, win, "nt", F32, 1024, 2048, _tile(IN_COLS, 1664), "mixer_in_bwd", ex=ex)
    done(carried)
    drel = _bucket_sum(dst, jnp.reshape(onehot, (N_BUCKETS, -1)), "bucket_sum")
    g1, dsh2, dsc2, dgn2, df1, dgt1 = _norm_bwd(dh2, x1, g2, g_mix, sc2, (f1, gt1, 0.5), "mixer_norm_bwd")

    da1, db1, s1, dh1 = _ffn_bwd(df1, a1, b1, w1a, w3a, w2a, "ffn1_bwd")
    r11, r13, r12 = ffn_weight_grads(h1, da1, db1, s1, df1, "ffn1", [])
    ex, done = _ride([r11, r13, r12])
    gwin_full, *carried = _matmul(h2, dz, "tn", F32, 1024, _tile(IN_COLS, 1664), tk, "grad_w_in", ex=ex)
    done(carried)
    rm_in = _Reduction(jnp.transpose(jnp.reshape(gwin_full, (D, N_CHIPS, -1)), (1, 0, 2)), "w_in", c_arr, jc_arr)
    ex, done = _ride([r13, r12, rm_in])
    state, started = _exchange_start(ex, "reduce_late_start")
    gwout_full = _matmul(mix, dy, "tn", F32, 1024, 1024, tk, "grad_w_out", after=[started])
    grad_x, dsh1, dsc1, dgn1 = _norm_bwd(dh1, xs, g1, g_ffn1, sc1, None, "ffn1_norm_bwd", after=[started])
    done(_exchange_wait(ex, state, [gwout_full, grad_x], "reduce_late_wait"))
    rm_out = _Reduction(jnp.reshape(gwout_full, (N_CHIPS, -1, D)), "w_out", c_arr, jc_arr)

    dmod = jnp.concatenate([dsh1, dsc1, dgt1, dsh2, dsc2, dgt2, dsh3, dsc3, dgt3], axis=1)
    small_w = [b_ada, g_ffn1, g_mix, g_ffn2, spatial_w, spatial_b, g_v, g_q, g_k, sinks, rel_bias]
    small_m = [m_b_ada, m_g_ffn1, m_g_mix, m_g_ffn2, m_spatial_w, m_spatial_b, m_g_v, m_g_q, m_g_k, m_sinks, m_rel_bias]
    small_v = [v_b_ada, v_g_ffn1, v_g_mix, v_g_ffn2, v_spatial_w, v_spatial_b, v_g_v, v_g_q, v_g_k, v_sinks, v_rel_bias]
    small_g = [dmod, dgn1, dgn2, dgn3, jnp.where(causal[None], dwm, 0.0), dsb, dgv, dgq, dgk, dsk, drel]
    shapes = [w.shape for w in small_w]
    gpack = _pack(small_g)
    rows = gpack.shape[0]
    gall = jnp.reshape(_allgather_small(gpack, "gather_small"), (N_DEV, rows, LANES))
    sg, sd, sm, sv = _small_update(gall, _pack(small_w), _pack(small_m), _pack(small_v), "small_update")
    sg, sd, sm, sv = [_unpack(p, shapes) for p in (sg, sd, sm, sv)]

    mod_rows = -(-N_MOD * D // (SUBLANES * LANES)) * SUBLANES
    dmod_all = jnp.reshape(gall[:, :mod_rows], (N_DEV, -1))[:, :N_MOD * D]
    dmod_sh = lax.dynamic_slice(dmod_all, (0, chip * mod_cols), (N_DEV, mod_cols))
    dmod_sh = jnp.pad(dmod_sh, ((0, BF16_ROWS - N_DEV), (0, 0))).astype(BF16)
    g_wada = _matmul(c_act, dmod_sh, "tn", F32, 1024, _tile(mod_cols, 512), BF16_ROWS, "grad_w_ada")

    big = {}

    def update(nm, w, g, m, v, token):
        g_out, d, nm_, nv_ = _adamw(w[0], g, m[0], v[0], f"adamw_{nm}", emit_grad=True, after=[token])
        big[nm] = (g_out[None], d[None], nm_[None], nv_[None])
        return d

    ex, done = _ride([r12, rm_in, rm_out])
    state, token = _exchange_start(ex, "reduce_tail_0_start")
    behind = [update("w1_ffn2", w1_ffn2, r21.result, m_w1_ffn2, v_w1_ffn2, token),
              update("w3_ffn2", w3_ffn2, r23.result, m_w3_ffn2, v_w3_ffn2, token),
              update("w2_ffn2", w2_ffn2, r22.result, m_w2_ffn2, v_w2_ffn2, token)]
    done(_exchange_wait(ex, state, behind, "reduce_tail_0_wait"))
    ex, done = _ride([rm_in, rm_out])
    state, token = _exchange_start(ex, "reduce_tail_1_start")
    behind = [update("w1_ffn1", w1_ffn1, r11.result, m_w1_ffn1, v_w1_ffn1, token),
              update("w3_ffn1", w3_ffn1, r13.result, m_w3_ffn1, v_w3_ffn1, token),
              update("w2_ffn1", w2_ffn1, r12.result, m_w2_ffn1, v_w2_ffn1, token)]
    done(_exchange_wait(ex, state, behind, "reduce_tail_1_wait"))
    ex, done = _ride([rm_out])
    state, token = _exchange_start(ex, "reduce_tail_2_start")
    d_wada, nm_wada, nv_wada = _adamw(w_ada[0], g_wada, m_w_ada[0], v_w_ada[0], "adamw_w_ada", after=[token])
    behind = [d_wada, update("w_in", w_in, rm_in.result, m_w_in, v_w_in, token)]
    done(_exchange_wait(ex, state, behind, "reduce_tail_2_wait"))
    update("w_out", w_out, rm_out.result, m_w_out, v_w_out, token)
    big["w_ada"] = (g_wada[None], d_wada[None], nm_wada[None], nv_wada[None])

    order = ["w_ada", "b_ada", "g_ffn1", "w1_ffn1", "w3_ffn1", "w2_ffn1", "g_mix", "w_in", "spatial_w", "spatial_b",
             "g_v", "g_q", "g_k", "sinks", "rel_bias", "w_out", "g_ffn2", "w1_ffn2", "w3_ffn2", "w2_ffn2"]
    small_names = ["b_ada", "g_ffn1", "g_mix", "g_ffn2", "spatial_w", "spatial_b", "g_v", "g_q", "g_k", "sinks", "rel_bias"]
    for i, nm in enumerate(small_names):
        big[nm] = (sg[i], sd[i], sm[i], sv[i])
    outs = [loss, grad_x[None]]
    for kind in range(4):
        outs += [big[nm][kind] for nm in order]
    return tuple(outs)
```

```python
import functools
import math

import jax
import jax.numpy as jnp
import numpy as np
from jax import lax
from jax.experimental import pallas as pl
from jax.experimental.pallas import tpu as pltpu

F32 = jnp.float32
BF16 = jnp.bfloat16
MESH = pl.DeviceIdType.MESH
ANY = pl.BlockSpec(memory_space=pl.ANY)

EPS = 1e-6
BLOCK = 128
A_HEADS = 8
A_DIM = 128
A_WIDTH = A_HEADS * A_DIM
B_HEADS = 16
KV_HEADS = 2
GROUP = B_HEADS // KV_HEADS
HEAD_DIM = 64
B_WIDTH = B_HEADS * HEAD_DIM
KV_WIDTH = KV_HEADS * HEAD_DIM
Q_OFF = 2 * A_WIDTH
K_OFF = Q_OFF + B_WIDTH
V_OFF = K_OFF + KV_WIDTH
IN_COLS = V_OFF + KV_WIDTH
N_BUCKETS = 32
MAX_DISTANCE = 128
N_MOD = 9
N_CHIPS = 4
N_DEV = 8
NEG = -1e30

ADAM_LR = 0.001
ADAM_B1 = 0.9
ADAM_B2 = 0.999
ADAM_EPS = 1e-08
ADAM_WD = 0.01
ADAM_STEP = 10

LANES = 128
SUBLANES = 8
BF16_ROWS = 16
VMEM_LIMIT = 60 * 1024 * 1024

INV_SQRT2 = 1.0 / math.sqrt(2.0)
INV_SQRT_2PI = 1.0 / math.sqrt(2.0 * math.pi)


def _tile(n, pref, mult=LANES):
    t = (min(pref, n) // mult) * mult
    while t >= mult:
        if n % t == 0:
            return t
        t -= mult
    return n


def _params(sem):
    return pltpu.CompilerParams(dimension_semantics=sem, vmem_limit_bytes=VMEM_LIMIT)


class _Exchange:
    def __init__(self, operands, out_shapes, aliases, n_sems, plan):
        self.operands, self.out_shapes, self.aliases, self.n_sems, self.plan = operands, out_shapes, aliases, n_sems, plan


def _pallas(body, *, name, grid, in_specs, out_specs, out_shape, args, scratch_shapes=(), semantics=None, ex=None,
            after=()):
    if ex is None:
        n_in = len(in_specs)

        def ordered(*refs):
            body(*refs[:n_in], *refs[n_in + len(after):])

        return pl.pallas_call(ordered if after else body, name=name, grid=grid,
                              in_specs=list(in_specs) + [ANY] * len(after), out_specs=out_specs, out_shape=out_shape,
                              scratch_shapes=list(scratch_shapes), compiler_params=_params(semantics))(*args, *after)
    assert not after
    n_in, n_out, n_scr = len(in_specs), len(out_specs), len(scratch_shapes)
    e_in, e_out = len(ex.operands), len(ex.out_shapes)

    def wrapped(*refs):
        ins, refs = refs[:n_in], refs[n_in:]
        ex_ins, refs = refs[:e_in], refs[e_in:]
        outs, refs = refs[:n_out], refs[n_out:]
        ex_outs, refs = refs[:e_out], refs[e_out:]
        scratch, (send_sems, recv_sems) = refs[:n_scr], refs[n_scr:]
        first, last = True, True
        for d, size in enumerate(grid):
            first = jnp.logical_and(first, pl.program_id(d) == 0)
            last = jnp.logical_and(last, pl.program_id(d) == size - 1)

        def start():
            sends, _ = ex.plan(ex_ins, ex_outs, send_sems, recv_sems)
            for cp in sends:
                _remote(*cp).start()

        def finish():
            sends, arrivals = ex.plan(ex_ins, ex_outs, send_sems, recv_sems)
            for cp in arrivals:
                _remote(*cp).wait_recv()
            for cp in sends:
                _remote(*cp).wait_send()

        if grid:
            pl.when(first)(start)
        else:
            start()
        if body is not None:
            body(*ins, *outs, *scratch)
        if grid:
            pl.when(last)(finish)
        else:
            finish()

    kwargs = dict(grid=grid) if grid else {}
    return pl.pallas_call(
        wrapped,
        name=name,
        in_specs=list(in_specs) + [ANY] * e_in,
        out_specs=list(out_specs) + [ANY] * e_out,
        out_shape=list(out_shape) + list(ex.out_shapes),
        input_output_aliases={n_in + i: n_out + o for i, o in ex.aliases.items()},
        scratch_shapes=list(scratch_shapes) + [pltpu.SemaphoreType.DMA((ex.n_sems,)), pltpu.SemaphoreType.DMA((ex.n_sems,))],
        compiler_params=_params(("arbitrary",) * len(grid) if grid else None),
        **kwargs,
    )(*args, *ex.operands)


def _dot(a, b, dims=(((1,), (0,)), ((), ()))):
    return lax.dot_general(a, b, dims, preferred_element_type=F32)


NN = (((1,), (0,)), ((), ()))
NT = (((1,), (1,)), ((), ()))
TN = (((0,), (0,)), ((), ()))


def _sigmoid(x):
    return 1.0 / (1.0 + jnp.exp(-x))


def _gelu_and_grad(x):
    cdf = 0.5 * (1.0 + lax.erf(x * INV_SQRT2))
    pdf = jnp.exp(-0.5 * x * x) * INV_SQRT_2PI
    return x * cdf, cdf + x * pdf


def _gelu(x):
    return x * (0.5 * (1.0 + lax.erf(x * INV_SQRT2)))


def _rms(x):
    r = lax.rsqrt(jnp.mean(x * x, axis=-1, keepdims=True) + EPS)
    return x * r, r


ROW_CHUNK = 64


def _for_rows(tm, fn):
    rc = min(ROW_CHUNK, tm)

    def step(r, carry):
        fn(pl.ds(pl.multiple_of(r * rc, rc), rc))
        return carry

    lax.fori_loop(0, tm // rc, step, 0)


def _rms_bwd(dy, xhat, r):
    return r * (dy - xhat * jnp.mean(dy * xhat, axis=-1, keepdims=True))


def _matmul(a, b, mode, out_dtype, tm, tn, tk, name, shard_major=False, ex=None, after=()):
    if mode == "nn":
        (M, K), N = a.shape, b.shape[1]
    elif mode == "nt":
        (M, K), N = a.shape, b.shape[0]
    else:
        (K, M), N = a.shape, b.shape[1]
    tm, tn, tk = min(tm, M), min(tn, N), min(tk, K)
    assert M % tm == 0 and N % tn == 0 and K % tk == 0, (name, M, N, K, tm, tn, tk)
    nk = K // tk
    dims = {"nn": NN, "nt": NT, "tn": TN}[mode]
    a_spec = pl.BlockSpec((tk, tm), lambda i, j, k: (k, i)) if mode == "tn" else pl.BlockSpec((tm, tk), lambda i, j, k: (i, k))
    b_spec = pl.BlockSpec((tn, tk), lambda i, j, k: (j, k)) if mode == "nt" else pl.BlockSpec((tk, tn), lambda i, j, k: (k, j))
    if shard_major:
        assert tn * N_CHIPS == N
        out_shape = jax.ShapeDtypeStruct((N_CHIPS, M, tn), out_dtype)
        o_spec = pl.BlockSpec((None, tm, tn), lambda i, j, k: (j, i, 0))
    else:
        out_shape = jax.ShapeDtypeStruct((M, N), out_dtype)
        o_spec = pl.BlockSpec((tm, tn), lambda i, j, k: (i, j))

    direct = nk == 1 or out_dtype == F32

    def body(a_ref, b_ref, o_ref, *scratch):
        k = pl.program_id(2)
        if nk == 1:
            o_ref[...] = _dot(a_ref[...], b_ref[...], dims).astype(o_ref.dtype)
            return
        acc_ref = o_ref if direct else scratch[0]

        @pl.when(k == 0)
        def _():
            acc_ref[...] = jnp.zeros(acc_ref.shape, F32)

        acc_ref[...] += _dot(a_ref[...], b_ref[...], dims)
        if not direct:
            @pl.when(k == nk - 1)
            def _():
                o_ref[...] = acc_ref[...].astype(o_ref.dtype)

    outs = _pallas(body, name=name, grid=(M // tm, N // tn, nk), in_specs=[a_spec, b_spec], out_specs=[o_spec],
                   out_shape=[out_shape], scratch_shapes=[] if direct else [pltpu.VMEM((tm, tn), F32)],
                   semantics=("parallel", "parallel", "arbitrary"), args=[a, b], ex=ex, after=after)
    return outs[0] if ex is None else outs


def _mod_partial(c_all, w_ada, b_sh, name):
    R, D = c_all.shape
    N = w_ada.shape[1]
    tn = _tile(N, 512)

    def body(c_ref, w_ref, b_ref, o_ref, ca_ref):
        cv = c_ref[...]
        ca = (cv * _sigmoid(cv)).astype(BF16)
        ca_ref[...] = ca
        o_ref[...] = _dot(ca, w_ref[...].astype(BF16)) + b_ref[...]

    return pl.pallas_call(
        body,
        name=name,
        grid=(N // tn,),
        in_specs=[
            pl.BlockSpec((R, D), lambda j: (0, 0)),
            pl.BlockSpec((D, tn), lambda j: (0, j)),
            pl.BlockSpec((1, tn), lambda j: (0, j)),
        ],
        out_specs=[pl.BlockSpec((R, tn), lambda j: (0, j)), pl.BlockSpec((R, D), lambda j: (0, 0))],
        out_shape=[jax.ShapeDtypeStruct((R, N), F32), jax.ShapeDtypeStruct((R, D), BF16)],
        compiler_params=_params(("arbitrary",)),
    )(c_all, w_ada, b_sh)


FFN_BLOCK = 1024


def _ffn_blocks(F):
    if F % FFN_BLOCK == 0 or F < FFN_BLOCK:
        tf = _tile(F, FFN_BLOCK)
        return tf, F // tf, tf
    nj = -(-F // FFN_BLOCK)
    tail = F - (nj - 1) * FFN_BLOCK
    assert tail % LANES == 0
    return FFN_BLOCK, nj, tail
def _ffn_fwd(x, g, sh, sc, gt, w1, w3, w2, tgt, name, ex=None, h=None):
    S, D = x.shape
    F = w1.shape[1]
    tm, tf, nj, tail = _tile(S, 512), *_ffn_blocks(F)
    ni = S // tm
    with_loss = tgt is not None
    assert h is None or not with_loss

    def body(*refs):
        if with_loss:
            (x_ref, g_ref, sh_ref, sc_ref, gt_ref, w1_ref, w3_ref, w2_ref, tgt_ref,
             gout_ref, df_ref, h_ref, a_ref, b_ref, dgt_ref, loss_ref, acc_ref) = refs
        elif h is None:
            (x_ref, g_ref, sh_ref, sc_ref, gt_ref, w1_ref, w3_ref, w2_ref,
             xo_ref, h_ref, a_ref, b_ref, f_ref, acc_ref) = refs
        else:
            (x_ref, g_ref, sh_ref, sc_ref, gt_ref, w1_ref, w3_ref, w2_ref, h_ref,
             xo_ref, a_ref, b_ref, f_ref, acc_ref) = refs
        i, j = pl.program_id(0), pl.program_id(1)

        if h is None:
            @pl.when(j == 0)
            def _():
                def prologue(rows):
                    xhat, _ = _rms(x_ref[rows, :])
                    h_ref[rows, :] = ((xhat * g_ref[...]) * (1.0 + sc_ref[...]) + sh_ref[...]).astype(BF16)

                _for_rows(tm, prologue)

        @pl.when(j == 0)
        def _():
            acc_ref[...] = jnp.zeros(acc_ref.shape, F32)

        def columns(width):
            def run():
                hb = h_ref[...]
                av = _dot(hb, w1_ref[:, :width])
                bv = _dot(hb, w3_ref[:, :width])
                a_ref[:, :width] = av.astype(BF16)
                b_ref[:, :width] = bv.astype(BF16)
                sv = ((av * _sigmoid(av)) * bv).astype(BF16)
                acc_ref[...] += _dot(sv, w2_ref[:width, :])
            return run

        if tail == tf:
            columns(tf)()
        else:
            pl.when(j < nj - 1)(columns(tf))
            pl.when(j == nj - 1)(columns(tail))

        @pl.when(j == nj - 1)
        def _():
            if with_loss:
                @pl.when(i == 0)
                def _():
                    dgt_ref[...] = jnp.zeros(dgt_ref.shape, F32)
                    loss_ref[...] = jnp.zeros(loss_ref.shape, F32)

            def epilogue(rows):
                fv = acc_ref[rows, :]
                half_gate = 0.5 * gt_ref[...]
                xo = x_ref[rows, :] + half_gate * fv
                if not with_loss:
                    xo_ref[rows, :] = xo
                    f_ref[rows, :] = fv.astype(f_ref.dtype)
                    return
                err = xo - tgt_ref[rows, :]
                gout = err * (1.0 / D)
                gout_ref[rows, :] = gout
                df_ref[rows, :] = (half_gate * gout).astype(BF16)
                dgt_ref[...] += 0.5 * jnp.sum(gout * fv, axis=0, keepdims=True)
                loss_part = jnp.sum(jnp.sum(err * err, axis=1, keepdims=True), axis=0, keepdims=True)
                loss_ref[...] += jnp.broadcast_to(loss_part, loss_ref.shape)

            _for_rows(tm, epilogue)

    row = pl.BlockSpec((tm, D), lambda i, j: (i, 0))
    row_in = pl.BlockSpec((tm, D), lambda i, j: (i, 0), pipeline_mode=pl.Buffered(1))
    vec = pl.BlockSpec((1, D), lambda i, j: (0, 0))
    col = pl.BlockSpec((tm, tf), lambda i, j: (i, j))
    in_specs = [row_in, vec, vec, vec, vec,
                pl.BlockSpec((D, tf), lambda i, j: (0, j)),
                pl.BlockSpec((D, tf), lambda i, j: (0, j)),
                pl.BlockSpec((tf, D), lambda i, j: (j, 0))]
    args = [x, g, sh, sc, gt, w1, w3, w2]
    act = jax.ShapeDtypeStruct((S, F), BF16)
    if with_loss:
        in_specs.append(row_in)
        args.append(tgt)
        out_specs = [row, row, row_in, col, col, vec, pl.BlockSpec((1, LANES), lambda i, j: (0, 0))]
        out_shape = [jax.ShapeDtypeStruct((S, D), F32), jax.ShapeDtypeStruct((S, D), BF16),
                     jax.ShapeDtypeStruct((S, D), BF16), act, act,
                     jax.ShapeDtypeStruct((1, D), F32), jax.ShapeDtypeStruct((1, LANES), F32)]
    elif h is None:
        out_specs = [row, row, col, col, row]
        out_shape = [jax.ShapeDtypeStruct((S, D), F32), jax.ShapeDtypeStruct((S, D), BF16), act, act,
                     jax.ShapeDtypeStruct((S, D), BF16)]
    else:
        in_specs.append(row_in)
        args.append(h)
        out_specs = [row, col, col, row]
        out_shape = [jax.ShapeDtypeStruct((S, D), F32), act, act, jax.ShapeDtypeStruct((S, D), BF16)]
    return _pallas(body, name=name, grid=(ni, nj), in_specs=in_specs, out_specs=out_specs, out_shape=out_shape,
                   scratch_shapes=[pltpu.VMEM((tm, D), F32)],
                   semantics=("arbitrary", "arbitrary"), args=args, ex=ex)


def _ffn_bwd(df, a, b, w1, w3, w2, name, ex=None):
    S, D = df.shape
    F = a.shape[1]
    tm, tf, nj, tail = _tile(S, 512), *_ffn_blocks(F)

    def body(df_ref, a_ref, b_ref, w1_ref, w3_ref, w2_ref, da_ref, db_ref, s_ref, dh_ref):
        j = pl.program_id(1)

        @pl.when(j == 0)
        def _():
            dh_ref[...] = jnp.zeros(dh_ref.shape, F32)

        def columns(width):
            def run():
                ds = _dot(df_ref[...], w2_ref[:width, :], NT)
                av = a_ref[:, :width].astype(F32)
                bv = b_ref[:, :width].astype(F32)
                sig = _sigmoid(av)
                sil = av * sig
                da = ((ds * bv) * (sig * (1.0 + av * (1.0 - sig)))).astype(BF16)
                db = (ds * sil).astype(BF16)
                da_ref[:, :width] = da
                db_ref[:, :width] = db
                s_ref[:, :width] = (sil * bv).astype(BF16)
                dh_ref[...] += _dot(da, w1_ref[:, :width], NT) + _dot(db, w3_ref[:, :width], NT)
            return run

        if tail == tf:
            columns(tf)()
        else:
            pl.when(j < nj - 1)(columns(tf))
            pl.when(j == nj - 1)(columns(tail))

    row = pl.BlockSpec((tm, D), lambda i, j: (i, 0))
    col = pl.BlockSpec((tm, tf), lambda i, j: (i, j))
    act = jax.ShapeDtypeStruct((S, F), BF16)
    return _pallas(body, name=name, grid=(S // tm, nj),
                   in_specs=[row, col, col,
                             pl.BlockSpec((D, tf), lambda i, j: (0, j)),
                             pl.BlockSpec((D, tf), lambda i, j: (0, j)),
                             pl.BlockSpec((tf, D), lambda i, j: (j, 0))],
                   out_specs=[col, col, col, row],
                   out_shape=[act, act, act, jax.ShapeDtypeStruct((S, D), F32)],
                   semantics=("parallel", "arbitrary"), args=[df, a, b, w1, w3, w2], ex=ex)


def _norm_mod(x, g, sh, sc, name, ex=None, after=()):
    S, D = x.shape
    tm = _tile(S, 512)

    def body(x_ref, g_ref, sh_ref, sc_ref, h_ref):
        def step(rows):
            xhat, _ = _rms(x_ref[rows, :])
            h_ref[rows, :] = ((xhat * g_ref[...]) * (1.0 + sc_ref[...]) + sh_ref[...]).astype(BF16)

        _for_rows(tm, step)

    row = pl.BlockSpec((tm, D), lambda i: (i, 0))
    vec = pl.BlockSpec((1, D), lambda i: (0, 0))
    outs = _pallas(body, name=name, grid=(S // tm,), in_specs=[row, vec, vec, vec], out_specs=[row],
                   out_shape=[jax.ShapeDtypeStruct((S, D), BF16)], semantics=("parallel",), args=[x, g, sh, sc], ex=ex,
                   after=after)
    return outs[0] if ex is None else outs


def _norm_bwd(dh, x, gres, g, sc, prev, name, ex=None, after=()):
    S, D = x.shape
    tm = _tile(S, 256)
    has_prev = prev is not None
    coef = prev[2] if has_prev else None

    def body(*refs):
        if has_prev:
            (dh_ref, x_ref, gr_ref, g_ref, sc_ref, f_ref, gt_ref,
             go_ref, dsh_ref, dsc_ref, dg_ref, dp_ref, dgt_ref) = refs
        else:
            dh_ref, x_ref, gr_ref, g_ref, sc_ref, go_ref, dsh_ref, dsc_ref, dg_ref = refs
        sum_refs = [dsh_ref, dsc_ref, dg_ref] + ([dgt_ref] if has_prev else [])

        @pl.when(pl.program_id(0) == 0)
        def _():
            for ref in sum_refs:
                ref[...] = jnp.zeros(ref.shape, F32)

        def step(rows):
            dh = dh_ref[rows, :]
            xhat, r = _rms(x_ref[rows, :])
            gain = g_ref[...]
            scale1 = 1.0 + sc_ref[...]
            gout = gr_ref[rows, :] + _rms_bwd(dh * scale1 * gain, xhat, r)
            go_ref[rows, :] = gout
            sums = [dh, dh * (xhat * gain), dh * scale1 * xhat]
            if has_prev:
                dp_ref[rows, :] = ((coef * gt_ref[...]) * gout).astype(BF16)
                sums.append(coef * (gout * f_ref[rows, :].astype(F32)))
            for ref, v in zip(sum_refs, sums):
                ref[...] += jnp.sum(v, axis=0, keepdims=True)

        _for_rows(tm, step)

    row = pl.BlockSpec((tm, D), lambda i: (i, 0))
    vec = pl.BlockSpec((1, D), lambda i: (0, 0))
    vshape = jax.ShapeDtypeStruct((1, D), F32)
    in_specs = [row, row, row, vec, vec]
    args = [dh, x, gres, g, sc]
    out_specs = [row, vec, vec, vec]
    out_shape = [jax.ShapeDtypeStruct((S, D), F32), vshape, vshape, vshape]
    if has_prev:
        in_specs += [row, vec]
        args += [prev[0], prev[1]]
        out_specs += [row, vec]
        out_shape += [jax.ShapeDtypeStruct((S, D), BF16), vshape]
    return _pallas(body, name=name, grid=(S // tm,), in_specs=in_specs, out_specs=out_specs, out_shape=out_shape,
                   semantics=("arbitrary",), args=args, ex=ex, after=after)


PAIRS = B_HEADS // 2
PAIR_ROWS = (PAIRS // KV_HEADS) * BLOCK
BAND = 2 * BLOCK


def _stack(ref, offset, count):
    return jnp.concatenate([ref[:, offset + p * LANES:offset + (p + 1) * LANES] for p in range(count)], axis=0)


def _seg_mean(x, e_ref):
    return _dot(x.astype(BF16), e_ref[...]) * (1.0 / HEAD_DIM)


def _block_diag(x, x_rolled, left, kv_head):
    if kv_head == 0:
        top, bottom = jnp.where(left, x, 0.0), jnp.where(left, 0.0, x_rolled)
    else:
        top, bottom = jnp.where(left, x_rolled, 0.0), jnp.where(left, 0.0, x)
    return jnp.concatenate([top, bottom], axis=0).astype(BF16)


def _from_block_diag(g, left, kv_head):
    a, b = g[:BAND], g[BAND:]
    if kv_head == 0:
        return jnp.where(left, a + pltpu.roll(b, HEAD_DIM, 1), 0.0)
    return jnp.where(left, 0.0, pltpu.roll(a, HEAD_DIM, 1) + b)


def _pair_softmax(st, sk_ref, kv_head):
    out = []
    for e in range(2):
        seg = st[e * BAND:(e + 1) * BAND]
        sink = jnp.concatenate([jnp.full((1, BLOCK), sk_ref[kv_head * GROUP + 2 * p + e], F32)
                                for p in range(PAIRS // KV_HEADS)], axis=1)
        m = jnp.maximum(jnp.max(seg, axis=0, keepdims=True), sink)
        p_ = jnp.exp(seg - m)
        e_sink = jnp.exp(sink - m)
        inv = 1.0 / (jnp.sum(p_, axis=0, keepdims=True) + e_sink)
        out.append((p_ * inv, e_sink * inv))
    return out


def _lane_mean(x, ones_ref):
    return _dot(x.astype(BF16), ones_ref[...])


def _mixer_specs(nb, last):
    full = lambda shape: pl.BlockSpec(shape, lambda n: (0,) * len(shape))
    z_spec = pl.BlockSpec((BLOCK, IN_COLS), lambda n: (jnp.minimum(n, last), 0))
    zp_spec = pl.BlockSpec((BLOCK, 2 * KV_WIDTH), lambda n: (jnp.clip(n - 1, 0, last), K_OFF // (2 * KV_WIDTH)))
    consts = [full((A_HEADS * BLOCK, A_DIM)), full((A_HEADS * BLOCK, A_DIM)), full((1, LANES)), full((1, LANES)),
              full((LANES, LANES)), full((LANES, LANES)), pl.BlockSpec(memory_space=pltpu.SMEM),
              pl.BlockSpec((None, KV_HEADS, PAIR_ROWS, 2 * BAND), lambda n: (jnp.minimum(n, 1), 0, 0, 0))]
    return full, z_spec, zp_spec, consts


def _mixer_fwd(z, wm, sbp, gvp, gq2, gk2, seg_ones, lane_ones, sinks, biasp, name):
    S = z.shape[0]
    nb = S // BLOCK

    def body(z_ref, zp_ref, wm_ref, sbp_ref, gvp_ref, gq2_ref, gk2_ref, e_ref, l_ref, sk_ref, bias_ref, mix_ref):
        u = _gelu(_stack(z_ref, 0, A_HEADS))
        v = _gelu(_stack(z_ref, A_WIDTH, A_HEADS))
        vhat = v * lax.rsqrt(_lane_mean(v * v, l_ref) + EPS)
        vn = (vhat * gvp_ref[...]).astype(BF16)
        mixed = jnp.concatenate([_dot(wm_ref[h], vn[h * BLOCK:(h + 1) * BLOCK]) for h in range(A_HEADS)], axis=0)
        ya = (u * (mixed + sbp_ref[...])).astype(BF16)
        for h in range(A_HEADS):
            mix_ref[:, h * A_DIM:(h + 1) * A_DIM] = ya[h * BLOCK:(h + 1) * BLOCK]

        left = lax.broadcasted_iota(jnp.int32, (1, LANES), 1) < HEAD_DIM
        kv = jnp.concatenate([zp_ref[...], z_ref[:, K_OFF:K_OFF + 2 * KV_WIDTH]], axis=0)
        k2, v2 = kv[:, :KV_WIDTH], kv[:, KV_WIDTH:]
        kn2 = k2 * lax.rsqrt(_seg_mean(k2 * k2, e_ref) + EPS) * gk2_ref[...]
        kn2_r, v2_r = pltpu.roll(kn2, HEAD_DIM, 1), pltpu.roll(v2, HEAD_DIM, 1)
        qp = _stack(z_ref, Q_OFF, PAIRS)
        qn = (qp * lax.rsqrt(_seg_mean(qp * qp, e_ref) + EPS) * gq2_ref[...]).astype(BF16)
        for kh in range(KV_HEADS):
            kbd, vbd = _block_diag(kn2, kn2_r, left, kh), _block_diag(v2, v2_r, left, kh)
            st = _dot(kbd, qn[kh * PAIR_ROWS:(kh + 1) * PAIR_ROWS], NT) * (HEAD_DIM ** -0.5) + bias_ref[kh]
            wt = jnp.concatenate([w_e for w_e, _ in _pair_softmax(st, sk_ref, kh)], axis=0).astype(BF16)
            o = _dot(wt, vbd, TN).astype(BF16)
            for p in range(PAIRS // KV_HEADS):
                col = A_WIDTH + (kh * (PAIRS // KV_HEADS) + p) * LANES
                mix_ref[:, col:col + LANES] = o[p * BLOCK:(p + 1) * BLOCK]

    full, z_spec, zp_spec, consts = _mixer_specs(nb, nb - 1)
    return pl.pallas_call(
        body,
        name=name,
        grid=(nb,),
        in_specs=[z_spec, zp_spec, full((A_HEADS, BLOCK, BLOCK))] + consts,
        out_specs=pl.BlockSpec((BLOCK, A_WIDTH + B_WIDTH), lambda n: (n, 0)),
        out_shape=jax.ShapeDtypeStruct((S, A_WIDTH + B_WIDTH), BF16),
        compiler_params=_params(("parallel",)),
    )(z, z, wm, sbp, gvp, gq2, gk2, seg_ones, lane_ones, sinks, biasp)


def _mixer_bwd(z, dmix, wm, wm_t, sbp, gvp, gq2, gk2, seg_ones, lane_ones, sinks, biasp, pair_fold, name, ex=None):
    S = z.shape[0]
    nb = S // BLOCK

    def body(z_ref, zp_ref, dmix_ref, wm_ref, wmt_ref, sbp_ref, gvp_ref, gq2_ref, gk2_ref, e_ref, l_ref, sk_ref,
             bias_ref, fold_ref,
             dz_ref, dzkv_ref, dwm_ref, dsb_ref, dgv_ref, dgq_ref, dgk_ref, dsk_ref, dst_ref,
             carry_ref, tot_ref, sbacc_ref, skacc_ref, gqacc_ref, gkacc_ref):
        n = pl.program_id(0)
        left = lax.broadcasted_iota(jnp.int32, (1, LANES), 1) < HEAD_DIM

        @pl.when(n == 0)
        def _():
            for ref in (dwm_ref, dgv_ref, dst_ref, carry_ref, sbacc_ref, skacc_ref, gqacc_ref, gkacc_ref):
                ref[...] = jnp.zeros(ref.shape, ref.dtype)

        @pl.when(n < nb)
        def _():
            u, du_dz = _gelu_and_grad(_stack(z_ref, 0, A_HEADS))
            v, dv_dz = _gelu_and_grad(_stack(z_ref, A_WIDTH, A_HEADS))
            rv = lax.rsqrt(_lane_mean(v * v, l_ref) + EPS)
            vhat = v * rv
            gvp = gvp_ref[...]
            vn = (vhat * gvp).astype(BF16)
            rows = lambda a, h: a[h * BLOCK:(h + 1) * BLOCK]
            mixed = jnp.concatenate([_dot(wm_ref[h], rows(vn, h)) for h in range(A_HEADS)], axis=0) + sbp_ref[...]
            dya = _stack(dmix_ref, 0, A_HEADS).astype(F32)
            dmx = dya * u
            sbacc_ref[...] += dmx
            dmx_b = dmx.astype(BF16)
            for h in range(A_HEADS):
                dwm_ref[h] += _dot(rows(dmx_b, h), rows(vn, h), NT)
            dvn = jnp.concatenate([_dot(wmt_ref[h], rows(dmx_b, h)) for h in range(A_HEADS)], axis=0)
            dgv_ref[...] += jnp.sum(jnp.reshape(dvn * vhat, (A_HEADS, BLOCK, A_DIM)), axis=1)
            dzu = ((dya * mixed) * du_dz).astype(BF16)
            tv = dvn * gvp
            dzv = ((rv * (tv - vhat * _lane_mean(tv * vhat, l_ref))) * dv_dz).astype(BF16)
            for h in range(A_HEADS):
                dz_ref[:, h * A_DIM:(h + 1) * A_DIM] = rows(dzu, h)
                dz_ref[:, A_WIDTH + h * A_DIM:A_WIDTH + (h + 1) * A_DIM] = rows(dzv, h)

            kv = jnp.concatenate([zp_ref[...], z_ref[:, K_OFF:K_OFF + 2 * KV_WIDTH]], axis=0)
            k2, v2 = kv[:, :KV_WIDTH], kv[:, KV_WIDTH:]
            kn2 = k2 * lax.rsqrt(_seg_mean(k2 * k2, e_ref) + EPS) * gk2_ref[...]
            kn2_r, v2_r = pltpu.roll(kn2, HEAD_DIM, 1), pltpu.roll(v2, HEAD_DIM, 1)
            qp = _stack(z_ref, Q_OFF, PAIRS)
            rq = lax.rsqrt(_seg_mean(qp * qp, e_ref) + EPS)
            qhat = qp * rq
            gq2 = gq2_ref[...]
            qn = (qhat * gq2).astype(BF16)
            dop = _stack(dmix_ref, A_WIDTH, PAIRS)
            dqn_parts = []
            dk2n = jnp.zeros((BAND, KV_WIDTH), F32)
            dv2 = jnp.zeros((BAND, KV_WIDTH), F32)
            for kh in range(KV_HEADS):
                kbd, vbd = _block_diag(kn2, kn2_r, left, kh), _block_diag(v2, v2_r, left, kh)
                qg = qn[kh * PAIR_ROWS:(kh + 1) * PAIR_ROWS]
                dog = dop[kh * PAIR_ROWS:(kh + 1) * PAIR_ROWS]
                st = _dot(kbd, qg, NT) * (HEAD_DIM ** -0.5) + bias_ref[kh]
                halves = _pair_softmax(st, sk_ref, kh)
                dpt = _dot(vbd, dog, NT)
                ds_halves, t_halves = [], []
                for e, (w_e, w_sink) in enumerate(halves):
                    dp_e = dpt[e * BAND:(e + 1) * BAND]
                    delta = jnp.sum(w_e * dp_e, axis=0, keepdims=True)
                    ds_halves.append(w_e * (dp_e - delta))
                    t_halves.append(-(w_sink * delta))
                dst = jnp.concatenate(ds_halves, axis=0)
                dst_ref[kh] += dst
                skacc_ref[2 * kh:2 * kh + 2, :] += jnp.concatenate(t_halves, axis=0)
                ds_b = (dst * (HEAD_DIM ** -0.5)).astype(BF16)
                w_b = jnp.concatenate([w_e for w_e, _ in halves], axis=0).astype(BF16)
                dqn_parts.append(_dot(ds_b, kbd, TN))
                dk2n += _from_block_diag(_dot(ds_b, qg), left, kh)
                dv2 += _from_block_diag(_dot(w_b, dog), left, kh)
            dqn = jnp.concatenate(dqn_parts, axis=0)
            gqacc_ref[...] += jnp.sum(dqn * qhat, axis=0, keepdims=True)
            t = dqn * gq2
            dzq = (rq * (t - qhat * _seg_mean(t * qhat, e_ref))).astype(BF16)
            for p in range(PAIRS):
                dz_ref[:, Q_OFF + p * LANES:Q_OFF + (p + 1) * LANES] = rows(dzq, p)
            tot_ref[0] = carry_ref[0] + dk2n[:BLOCK]
            tot_ref[1] = carry_ref[1] + dv2[:BLOCK]
            carry_ref[0] = dk2n[BLOCK:]
            carry_ref[1] = dv2[BLOCK:]

        @pl.when(n == nb)
        def _():
            tot_ref[...] = carry_ref[...]

        kp = zp_ref[:, :KV_WIDTH]
        rk = lax.rsqrt(_seg_mean(kp * kp, e_ref) + EPS)
        khat = kp * rk
        dkn = tot_ref[0]
        gkacc_ref[...] += jnp.sum(dkn * khat, axis=0, keepdims=True)
        t = dkn * gk2_ref[...]
        dzkv_ref[:, :KV_WIDTH] = (rk * (t - khat * _seg_mean(t * khat, e_ref))).astype(BF16)
        dzkv_ref[:, KV_WIDTH:] = tot_ref[1].astype(BF16)

        @pl.when(n == nb)
        def _():
            dsb_ref[...] = jnp.broadcast_to(jnp.sum(sbacc_ref[...], axis=1, keepdims=True), dsb_ref.shape)
            dsk_ref[...] = lax.dot_general(skacc_ref[...], fold_ref[...], NN, precision=lax.Precision.HIGHEST,
                                           preferred_element_type=F32)
            dgq_ref[...] = gqacc_ref[...] + pltpu.roll(gqacc_ref[...], HEAD_DIM, 1)
            dgk_ref[...] = gkacc_ref[...] + pltpu.roll(gkacc_ref[...], HEAD_DIM, 1)

    last = nb - 1
    full, z_spec, zp_spec, consts = _mixer_specs(nb, last)
    return _pallas(
        body,
        name=name,
        grid=(nb + 1,),
        ex=ex,
        in_specs=[z_spec, zp_spec, pl.BlockSpec((BLOCK, A_WIDTH + B_WIDTH), lambda n: (jnp.minimum(n, last), 0)),
                  full((A_HEADS, BLOCK, BLOCK)), full((A_HEADS, BLOCK, BLOCK))] + consts + [full((PAIR_ROWS, LANES))],
        out_specs=[
            pl.BlockSpec((BLOCK, K_OFF), lambda n: (jnp.minimum(n, last), 0)),
            pl.BlockSpec((BLOCK, 2 * KV_WIDTH), lambda n: (jnp.maximum(n - 1, 0), 0)),
            full((A_HEADS, BLOCK, BLOCK)), full((A_HEADS * BLOCK, A_DIM)), full((A_HEADS, A_DIM)),
            full((1, LANES)), full((1, LANES)), full((SUBLANES, LANES)),
            full((KV_HEADS, PAIR_ROWS, 2 * BAND)),
        ],
        out_shape=[
            jax.ShapeDtypeStruct((S, K_OFF), BF16),
            jax.ShapeDtypeStruct((S, 2 * KV_WIDTH), BF16),
            jax.ShapeDtypeStruct((A_HEADS, BLOCK, BLOCK), F32),
            jax.ShapeDtypeStruct((A_HEADS * BLOCK, A_DIM), F32),
            jax.ShapeDtypeStruct((A_HEADS, A_DIM), F32),
            jax.ShapeDtypeStruct((1, LANES), F32),
            jax.ShapeDtypeStruct((1, LANES), F32),
            jax.ShapeDtypeStruct((SUBLANES, LANES), F32),
            jax.ShapeDtypeStruct((KV_HEADS, PAIR_ROWS, 2 * BAND), F32),
        ],
        scratch_shapes=[
            pltpu.VMEM((2, BLOCK, KV_WIDTH), F32),
            pltpu.VMEM((2, BLOCK, KV_WIDTH), F32),
            pltpu.VMEM((A_HEADS * BLOCK, A_DIM), F32),
            pltpu.VMEM((SUBLANES, PAIR_ROWS), F32),
            pltpu.VMEM((1, LANES), F32),
            pltpu.VMEM((1, LANES), F32),
        ],
        semantics=("arbitrary",),
        args=[z, z, dmix, wm, wm_t, sbp, gvp, gq2, gk2, seg_ones, lane_ones, sinks, biasp, pair_fold],
    )


def _mixer_out(mix, w_out, x, gt, name):
    S, D = x.shape
    K = mix.shape[1]
    tm, tn = _tile(S, 1024), _tile(D, 1024)

    def body(m_ref, w_ref, x_ref, gt_ref, xo_ref, y_ref):
        y = _dot(m_ref[...], w_ref[...])
        y_ref[...] = y.astype(BF16)
        xo_ref[...] = x_ref[...] + gt_ref[...] * y

    blk = pl.BlockSpec((tm, tn), lambda j, i: (i, j))
    return pl.pallas_call(
        body,
        name=name,
        grid=(D // tn, S // tm),
        in_specs=[pl.BlockSpec((tm, K), lambda j, i: (i, 0)), pl.BlockSpec((K, tn), lambda j, i: (0, j)),
                  blk, pl.BlockSpec((1, tn), lambda j, i: (0, j))],
        out_specs=[blk, blk],
        out_shape=[jax.ShapeDtypeStruct((S, D), F32), jax.ShapeDtypeStruct((S, D), BF16)],
        compiler_params=_params(("parallel", "parallel")),
    )(mix, w_out, x, gt)


def _bucket_sum(dst, onehot, name):
    def body(d_ref, o_ref, out_ref):
        out_ref[...] = lax.dot_general(o_ref[...], d_ref[...], NT, precision=lax.Precision.HIGHEST,
                                       preferred_element_type=F32)

    return pl.pallas_call(
        body,
        name=name,
        out_shape=jax.ShapeDtypeStruct((N_BUCKETS, B_HEADS), F32),
    )(dst, onehot)


def _adamw_math(w, g, m, v):
    m = ADAM_B1 * m + (1.0 - ADAM_B1) * g
    v = ADAM_B2 * v + (1.0 - ADAM_B2) * (g * g)
    m_hat = m / (1.0 - ADAM_B1 ** ADAM_STEP)
    v_hat = v / (1.0 - ADAM_B2 ** ADAM_STEP)
    delta = -ADAM_LR * (m_hat / (jnp.sqrt(v_hat) + ADAM_EPS) + ADAM_WD * w)
    return delta, m, v


def _adamw(w, g, m, v, name, emit_grad=False, after=()):
    R, C = w.shape
    tr = _tile(R, max(SUBLANES, (1 << 19) // C), SUBLANES)

    def body(w_ref, g_ref, m_ref, v_ref, *out_refs):
        gv = g_ref[...]
        results = _adamw_math(w_ref[...], gv, m_ref[...], v_ref[...])
        for ref, val in zip(out_refs, ((gv,) if emit_grad else ()) + results):
            ref[...] = val

    blk = pl.BlockSpec((tr, C), lambda i: (i, 0))
    shape = jax.ShapeDtypeStruct((R, C), F32)
    n_out = 4 if emit_grad else 3
    return _pallas(body, name=name, grid=(R // tr,), in_specs=[blk] * 4, out_specs=[blk] * n_out,
                   out_shape=[shape] * n_out, semantics=("parallel",), args=[w, g, m, v], after=after)


def _small_update(parts, w, m, v, name):
    R = w.shape[0]

    def body(p_ref, w_ref, m_ref, v_ref, g_ref, d_ref, mo_ref, vo_ref):
        g = p_ref[0]
        for dev in range(1, N_DEV):
            g = g + p_ref[dev]
        g_ref[...] = g
        d, mn, vn = _adamw_math(w_ref[...], g, m_ref[...], v_ref[...])
        d_ref[...] = d
        mo_ref[...] = mn
        vo_ref[...] = vn

    shape = jax.ShapeDtypeStruct((R, LANES), F32)
    return pl.pallas_call(
        body,
        name=name,
        out_shape=[shape] * 4,
        compiler_params=pltpu.CompilerParams(vmem_limit_bytes=VMEM_LIMIT),
    )(parts, w, m, v)


def _place():
    x, y, c = lax.axis_index("x"), lax.axis_index("y"), lax.axis_index("c")
    chips = [(1 - x, y), (x, 1 - y), (1 - x, 1 - y)]
    return x, y, c, chips


def _remote(src, dst, send_sem, recv_sem, to):
    return pltpu.make_async_remote_copy(src_ref=src, dst_ref=dst, send_sem=send_sem, recv_sem=recv_sem,
                                        device_id=to, device_id_type=MESH)


def _allgather_small(block, name):
    m_per, n = block.shape

    def body(x_ref, out_ref, send_sems, recv_sems, local_sem):
        x, y, c, chips = _place()
        me, sibling = (x, y, c), (x, y, 1 - c)

        def rows(px, py, pc):
            return out_ref.at[pl.ds((4 * px + 2 * py + pc) * m_per, m_per), :]

        def copy(k, blk, to, src=None):
            return _remote(rows(*blk) if src is None else src, rows(*blk), send_sems.at[k], recv_sems.at[k], to)

        mine = pltpu.make_async_copy(x_ref, rows(*me), local_sem)
        mine.start()
        first = [copy(0, me, sibling, src=x_ref)]
        first += [copy(1 + j, me, (*chip, c), src=x_ref) for j, chip in enumerate(chips)]
        for cp in first:
            cp.start()
        passed = [copy(4 + j, (*chip, c), sibling) for j, chip in enumerate(chips)]
        for j, chip in enumerate(chips):
            copy(1 + j, (*chip, c), me).wait_recv()
            passed[j].start()
        copy(0, sibling, me).wait_recv()
        for j, chip in enumerate(chips):
            copy(4 + j, (*chip, 1 - c), me).wait_recv()
        for cp in first + passed:
            cp.wait_send()
        mine.wait()

    return pl.pallas_call(
        body,
        name=name,
        out_shape=jax.ShapeDtypeStruct((N_DEV * m_per, n), block.dtype),
        in_specs=[pl.BlockSpec(memory_space=pltpu.VMEM)],
        out_specs=pl.BlockSpec(memory_space=pltpu.VMEM),
        scratch_shapes=[pltpu.SemaphoreType.DMA((7,)), pltpu.SemaphoreType.DMA((7,)), pltpu.SemaphoreType.DMA],
        compiler_params=pltpu.CompilerParams(vmem_limit_bytes=VMEM_LIMIT),
    )(block)


def _half(ref, c, rows):
    start = pl.multiple_of(c * rows, BF16_ROWS)
    if len(ref.shape) == 2:
        return ref.at[pl.ds(start, rows), :]
    return ref.at[:, pl.ds(start, rows), :]


def _slot(ref, index):
    if len(ref.shape) == 3:
        return ref.at[index]
    width = ref.shape[1] // N_CHIPS
    return ref.at[:, pl.ds(pl.multiple_of(index * width, LANES), width)]


def _shard_rows(buf):
    return buf.shape[1] if len(buf.shape) == 3 else buf.shape[0]


def _same(arrays):
    return [jax.ShapeDtypeStruct(a.shape, a.dtype) for a in arrays], {t: t for t in range(len(arrays))}


def _ex_gather_ici(bufs):
    def plan(ins, outs, send_sems, recv_sems):
        x, y, c, chips = _place()
        sends, arrivals = [], []
        for t, buf in enumerate(bufs):
            rows = _shard_rows(buf) // 2
            mine = _half(_slot(outs[t], 2 * x + y), c, rows)
            for k, (px, py) in enumerate(chips):
                sems = (send_sems.at[3 * t + k], recv_sems.at[3 * t + k], (px, py, c))
                landed = _half(_slot(outs[t], 2 * px + py), c, rows)
                sends.append((mine, mine, *sems))
                arrivals.append((landed, landed, *sems))
        return sends, arrivals

    shapes, aliases = _same(bufs)
    return _Exchange(bufs, shapes, aliases, 3 * len(bufs), plan)


def _ex_gather_near(bufs):
    def plan(ins, outs, send_sems, recv_sems):
        x, y, c, chips = _place()
        sends, arrivals = [], []
        for t, buf in enumerate(bufs):
            rows = _shard_rows(buf) // 2
            mine = _half(_slot(outs[t], 2 * x + y), c, rows)
            for k, (px, py) in enumerate(chips[:2]):
                sems = (send_sems.at[2 * t + k], recv_sems.at[2 * t + k], (px, py, c))
                landed = _half(_slot(outs[t], 2 * px + py), c, rows)
                sends.append((mine, mine, *sems))
                arrivals.append((landed, landed, *sems))
        return sends, arrivals

    shapes, aliases = _same(bufs)
    return _Exchange(bufs, shapes, aliases, 2 * len(bufs), plan)


def _ex_gather_relay(bufs):
    def plan(ins, outs, send_sems, recv_sems):
        x, y, c, chips = _place()
        (xn, yn, diag) = chips
        slot = lambda chip: 2 * chip[0] + chip[1]
        sends, arrivals = [], []
        for t, buf in enumerate(bufs):
            quarter = _shard_rows(buf) // 4

            def piece(chip, q):
                start = pl.multiple_of(c * 2 * quarter + q * quarter, BF16_ROWS)
                return _slot(outs[t], slot(chip)).at[pl.ds(start, quarter), :]

            for k, (held, to) in enumerate([(xn, yn), (yn, xn)]):
                sems = (send_sems.at[2 * t + k], recv_sems.at[2 * t + k], (*to, c))
                sends.append((piece(held, k), piece(held, k), *sems))
                arrivals.append((piece(diag, k), piece(diag, k), *sems))
        onward = _ex_gather_d2d(bufs, which=(0, 1))
        more_sends, more_arrivals = onward.plan(ins, outs, _Shifted(send_sems, 2 * len(bufs)),
                                                _Shifted(recv_sems, 2 * len(bufs)))
        return sends + more_sends, arrivals + more_arrivals

    shapes, aliases = _same(bufs)
    return _Exchange(bufs, shapes, aliases, 2 * len(bufs) + 3 * len(bufs), plan)


def _ex_gather_d2d(bufs, which=(0, 1, 2)):
    def plan(ins, outs, send_sems, recv_sems):
        x, y, c, chips = _place()
        sends, arrivals = [], []
        for t, buf in enumerate(bufs):
            rows = _shard_rows(buf) // 2
            for k in which:
                px, py = chips[k]
                sems = (send_sems.at[3 * t + k], recv_sems.at[3 * t + k], (x, y, 1 - c))
                landed = _half(_slot(outs[t], 2 * px + py), c, rows)
                other = _half(_slot(outs[t], 2 * px + py), 1 - c, rows)
                sends.append((landed, landed, *sems))
                arrivals.append((other, other, *sems))
        return sends, arrivals

    shapes, aliases = _same(bufs)
    return _Exchange(bufs, shapes, aliases, 3 * len(bufs), plan)


def _ex_swap_halves(grads):
    def plan(ins, outs, send_sems, recv_sems):
        x, y, c, _ = _place()
        sends = [(_half(ins[t], 1 - c, g.shape[1] // 2), outs[t], send_sems.at[t], recv_sems.at[t], (x, y, 1 - c))
                 for t, g in enumerate(grads)]
        return sends, sends

    shapes = [jax.ShapeDtypeStruct((g.shape[0], g.shape[1] // 2, g.shape[2]), g.dtype) for g in grads]
    return _Exchange(grads, shapes, {}, len(grads), plan)


def _ex_scatter(sums):
    def plan(ins, outs, send_sems, recv_sems):
        x, y, c, chips = _place()
        sends = [(ins[t].at[2 * px + py], outs[t].at[k], send_sems.at[3 * t + k], recv_sems.at[3 * t + k], (px, py, c))
                 for t in range(len(sums)) for k, (px, py) in enumerate(chips)]
        return sends, sends

    shapes = [jax.ShapeDtypeStruct((N_CHIPS - 1,) + s.shape[1:], s.dtype) for s in sums]
    return _Exchange(sums, shapes, {}, 3 * len(sums), plan)


def _ex_join_halves(fulls):
    def plan(ins, outs, send_sems, recv_sems):
        x, y, c, _ = _place()
        sends, arrivals = [], []
        for t, full in enumerate(fulls):
            rows = full.shape[0] // 2
            sems = (send_sems.at[t], recv_sems.at[t], (x, y, 1 - c))
            mine, other = _half(outs[t], c, rows), _half(outs[t], 1 - c, rows)
            sends.append((mine, mine, *sems))
            arrivals.append((other, other, *sems))
        return sends, arrivals

    shapes, aliases = _same(fulls)
    return _Exchange(fulls, shapes, aliases, len(fulls), plan)


class _Shifted:
    def __init__(self, sems, offset):
        self.sems, self.offset = sems, offset

    @property
    def at(self):
        return self

    def __getitem__(self, k):
        return self.sems.at[self.offset + k]


def _combine(exchanges):
    operands, out_shapes, aliases, starts = [], [], {}, []
    n_sems = 0
    for e in exchanges:
        starts.append((len(operands), len(out_shapes), n_sems))
        aliases.update({len(operands) + i: len(out_shapes) + o for i, o in e.aliases.items()})
        operands += list(e.operands)
        out_shapes += list(e.out_shapes)
        n_sems += e.n_sems

    def plan(ins, outs, send_sems, recv_sems):
        sends, arrivals = [], []
        for e, (i0, o0, s0) in zip(exchanges, starts):
            s, a = e.plan(ins[i0:i0 + len(e.operands)], outs[o0:o0 + len(e.out_shapes)],
                          _Shifted(send_sems, s0), _Shifted(recv_sems, s0))
            sends += s
            arrivals += a
        return sends, arrivals

    return _Exchange(operands, out_shapes, aliases, n_sems, plan)


class _Reduction:
    def __init__(self, grad, tag, c_arr, jc_arr):
        self.grad, self.tag, self.c_arr, self.jc_arr, self.stage = grad, tag, c_arr, jc_arr, 0

    def exchange(self):
        if self.stage == 0:
            return _ex_swap_halves([self.grad])
        if self.stage == 1:
            return _ex_scatter([self.sums])
        return _ex_join_halves([self.full])

    def advance(self, landed):
        if self.stage == 0:
            self.recv = landed
            self.sums = _chip_sum(self.grad, landed, self.c_arr, f"chip_sum_{self.tag}")
        elif self.stage == 1:
            self.full = _owner_sum(self.grad, self.recv, landed, self.jc_arr, f"owner_sum_{self.tag}")
        else:
            self.result = landed
        self.stage += 1


def _ride(reductions):
    def done(carried):
        for r, landed in zip(reductions, carried):
            r.advance(landed)

    return _combine([r.exchange() for r in reductions]), done


def _exchange_alone(ex, name):
    return _pallas(None, name=name, grid=(), in_specs=[], out_specs=[], out_shape=[], args=[], ex=ex)


SEM = pl.BlockSpec(memory_space=pltpu.SEMAPHORE)
DATAFLOW = pltpu.SideEffectType.DATAFLOW_SIDE_EFFECTING


def _exchange_start(ex, name):
    e_in, e_out = len(ex.operands), len(ex.out_shapes)
    kept = [i for i in range(e_in) if i not in ex.aliases]

    def body(*refs):
        ins, refs = refs[:e_in], refs[e_in:]
        outs, refs = refs[:e_out], refs[e_out:]
        _, (send_sems, recv_sems, token) = refs[:len(kept)], refs[len(kept):]
        sends, _ = ex.plan(ins, outs, send_sems, recv_sems)
        for cp in sends:
            _remote(*cp).start()
        token[...] = jnp.zeros(token.shape, F32)

    sems = pltpu.SemaphoreType.DMA((ex.n_sems,))
    aliases = dict(ex.aliases)
    aliases.update({i: e_out + k for k, i in enumerate(kept)})
    res = pl.pallas_call(
        body,
        name=name,
        in_specs=[ANY] * e_in,
        out_specs=[ANY] * (e_out + len(kept)) + [SEM, SEM, pl.BlockSpec(memory_space=pltpu.VMEM)],
        out_shape=list(ex.out_shapes) + [jax.ShapeDtypeStruct(ex.operands[i].shape, ex.operands[i].dtype) for i in kept]
        + [sems, sems, jax.ShapeDtypeStruct((SUBLANES, LANES), F32)],
        input_output_aliases=aliases,
        compiler_params=pltpu.CompilerParams(has_side_effects=DATAFLOW),
    )(*ex.operands)
    outs, kept_thru, (send_sems, recv_sems, token) = res[:e_out], res[e_out:e_out + len(kept)], res[e_out + len(kept):]
    operands = list(ex.operands)
    for i, o in ex.aliases.items():
        operands[i] = outs[o]
    for k, i in enumerate(kept):
        operands[i] = kept_thru[k]
    return (operands, outs, send_sems, recv_sems), token


def _exchange_wait(ex, state, after, name):
    operands, outs, send_sems, recv_sems = state
    e_out = len(outs)
    kept = [i for i in range(len(operands)) if i not in ex.aliases]

    def body(*refs):
        sources, refs = refs[:len(kept)], refs[len(kept):]
        landing, refs = refs[:e_out], refs[e_out:]
        ins = [None] * len(operands)
        for k, i in enumerate(kept):
            ins[i] = sources[k]
        sends, arrivals = ex.plan(ins, landing, refs[0], refs[1])
        for cp in arrivals:
            _remote(*cp).wait_recv()
        for cp in sends:
            _remote(*cp).wait_send()

    return pl.pallas_call(
        body,
        name=name,
        in_specs=[ANY] * (len(kept) + e_out) + [SEM, SEM] + [ANY] * len(after),
        out_specs=[ANY] * e_out,
        out_shape=[jax.ShapeDtypeStruct(o.shape, o.dtype) for o in outs],
        input_output_aliases={len(kept) + o: o for o in range(e_out)},
        compiler_params=pltpu.CompilerParams(has_side_effects=DATAFLOW),
    )(*[operands[i] for i in kept], *outs, send_sems, recv_sems, *after)


def _cast_to_slot(w, chip_arr, name, after=(), natural=False):
    A, B = w.shape
    ta = _tile(A, max(BF16_ROWS, (1 << 19) // B), BF16_ROWS)

    def body(j_ref, w_ref, *rest):
        rest[-1][...] = w_ref[...].astype(BF16)

    if natural:
        assert B % LANES == 0
        out_spec = pl.BlockSpec((ta, B), lambda i, j_ref: (i, j_ref[0]))
        out_shape = jax.ShapeDtypeStruct((A, N_CHIPS * B), BF16)
    else:
        out_spec = pl.BlockSpec((None, ta, B), lambda i, j_ref: (j_ref[0], i, 0))
        out_shape = jax.ShapeDtypeStruct((N_CHIPS, A, B), BF16)
    return pl.pallas_call(
        body,
        name=name,
        grid_spec=pltpu.PrefetchScalarGridSpec(
            num_scalar_prefetch=1,
            grid=(A // ta,),
            in_specs=[pl.BlockSpec((ta, B), lambda i, j_ref: (i, 0))] + [ANY] * len(after),
            out_specs=out_spec,
        ),
        out_shape=out_shape,
        compiler_params=_params(("parallel",)),
    )(chip_arr, w, *after)


def _chip_sum(grad, recv, c_arr, name):
    _, A, B = grad.shape
    hA = A // 2
    ta = _tile(hA, max(BF16_ROWS, (1 << 19) // B), BF16_ROWS)
    nh = hA // ta

    def body(c_ref, g_ref, r_ref, o_ref):
        o_ref[...] = (g_ref[...] + r_ref[...]).astype(BF16)

    return pl.pallas_call(
        body,
        name=name,
        grid_spec=pltpu.PrefetchScalarGridSpec(
            num_scalar_prefetch=1,
            grid=(N_CHIPS, nh),
            in_specs=[pl.BlockSpec((None, ta, B), lambda s, i, c_ref: (s, c_ref[0] * nh + i, 0)),
                      pl.BlockSpec((None, ta, B), lambda s, i, c_ref: (s, i, 0))],
            out_specs=pl.BlockSpec((None, ta, B), lambda s, i, c_ref: (s, i, 0)),
        ),
        out_shape=jax.ShapeDtypeStruct((N_CHIPS, hA, B), BF16),
        compiler_params=_params(("parallel", "parallel")),
    )(c_arr, grad, recv)


def _owner_sum(grad, recv, landed, jc_arr, name):
    _, A, B = grad.shape
    hA = A // 2
    ta = _tile(hA, max(BF16_ROWS, (1 << 19) // B), BF16_ROWS)
    nh = hA // ta

    def body(jc_ref, g_ref, r_ref, l0_ref, l1_ref, l2_ref, o_ref):
        total = g_ref[...] + r_ref[...]
        for ref in (l0_ref, l1_ref, l2_ref):
            total = total + ref[...].astype(F32)
        o_ref[...] = total

    def landed_spec(k):
        return pl.BlockSpec((None, ta, B), lambda i, jc_ref: (k, i, 0))

    return pl.pallas_call(
        body,
        name=name,
        grid_spec=pltpu.PrefetchScalarGridSpec(
            num_scalar_prefetch=1,
            grid=(nh,),
            in_specs=[pl.BlockSpec((None, ta, B), lambda i, jc_ref: (jc_ref[0], jc_ref[1] * nh + i, 0)),
                      pl.BlockSpec((None, ta, B), lambda i, jc_ref: (jc_ref[0], i, 0)),
                      landed_spec(0), landed_spec(1), landed_spec(2)],
            out_specs=pl.BlockSpec((ta, B), lambda i, jc_ref: (jc_ref[1] * nh + i, 0)),
        ),
        out_shape=jax.ShapeDtypeStruct((A, B), F32),
        compiler_params=_params(("parallel",)),
    )(jc_arr, grad, recv, landed, landed, landed)


def _pack(parts):
    rows = []
    for p in parts:
        flat = jnp.reshape(p.astype(F32), (-1,))
        tile = SUBLANES * LANES
        padded = -(-flat.shape[0] // tile) * tile
        rows.append(jnp.reshape(jnp.pad(flat, (0, padded - flat.shape[0])), (-1, LANES)))
    return jnp.concatenate(rows, axis=0)


def _unpack(pack, shapes):
    out, row = [], 0
    for shape in shapes:
        size = int(np.prod(shape))
        nrows = -(-size // (SUBLANES * LANES)) * SUBLANES
        out.append(jnp.reshape(jnp.reshape(pack[row:row + nrows], (-1,))[:size], shape))
        row += nrows
    return out


def _bias_tables():
    qi = np.arange(BLOCK)[:, None]
    kj = np.arange(2 * BLOCK)[None, :]
    dist = qi + BLOCK - kj
    in_window = (dist >= 0) & (dist < BLOCK)
    n = np.clip(dist, 0, None)
    max_exact = N_BUCKETS // 2
    nf = np.maximum(n, 1).astype(np.float32)
    large = max_exact + (np.log(nf / max_exact) / math.log(MAX_DISTANCE / max_exact)
                         * (N_BUCKETS - max_exact)).astype(np.int32)
    large = np.minimum(large, N_BUCKETS - 1)
    bucket = np.where(n < max_exact, n, large)
    onehot = (bucket[None] == np.arange(N_BUCKETS)[:, None, None]) & in_window[None]
    first = in_window & (kj >= BLOCK)
    return onehot.astype(np.float32), in_window, first


def kernel(x, c, w_ada, b_ada, g_ffn1, w1_ffn1, w3_ffn1, w2_ffn1, g_mix, w_in, spatial_w, spatial_b, g_v, g_q, g_k, sinks, rel_bias, w_out, g_ffn2, w1_ffn2, w3_ffn2, w2_ffn2, loss_target, m_w_ada, m_b_ada, m_g_ffn1, m_w1_ffn1, m_w3_ffn1, m_w2_ffn1, m_g_mix, m_w_in, m_spatial_w, m_spatial_b, m_g_v, m_g_q, m_g_k, m_sinks, m_rel_bias, m_w_out, m_g_ffn2, m_w1_ffn2, m_w3_ffn2, m_w2_ffn2, v_w_ada, v_b_ada, v_g_ffn1, v_w1_ffn1, v_w3_ffn1, v_w2_ffn1, v_g_mix, v_w_in, v_spatial_w, v_spatial_b, v_g_v, v_g_q, v_g_k, v_sinks, v_rel_bias, v_w_out, v_g_ffn2, v_w1_ffn2, v_w3_ffn2, v_w2_ffn2):
    ax, ay, ac = lax.axis_index("x"), lax.axis_index("y"), lax.axis_index("c")
    chip = 2 * ax + ay
    dev = 2 * chip + ac
    xs = x[0]
    tgt = loss_target[0]
    S, D = xs.shape
    F = N_CHIPS * w1_ffn1.shape[2]
    mod_cols = w_ada.shape[2]

    chip_arr = jnp.reshape(chip, (1,)).astype(jnp.int32)
    c_arr = jnp.reshape(ac, (1,)).astype(jnp.int32)
    jc_arr = jnp.stack([chip, ac]).astype(jnp.int32)

    c_all = _allgather_small(jnp.pad(c, ((0, SUBLANES - 1), (0, 0))), "gather_c")
    c_all = jnp.pad(c_all[::SUBLANES], ((0, BF16_ROWS - N_DEV), (0, 0)))
    b_sh = lax.dynamic_slice(b_ada, (0, chip * mod_cols), (1, mod_cols))
    mod_part, c_act = _mod_partial(c_all, w_ada[0], b_sh, "mod_partial")
    mod_all = _allgather_small(mod_part[:N_DEV], "gather_mod")
    mod_all = jnp.reshape(mod_all, (N_CHIPS, 2, N_DEV, mod_cols))[:, 0]
    mod = jnp.reshape(lax.dynamic_index_in_dim(mod_all, dev, axis=1, keepdims=False), (1, N_MOD * D))
    sh1, sc1, gt1, sh2, sc2, gt2, sh3, sc3, gt3 = [mod[:, i * D:(i + 1) * D] for i in range(N_MOD)]

    def cols_to_natural(w4):
        return jnp.reshape(jnp.transpose(w4, (1, 0, 2)), (w4.shape[1], -1))

    def cast(w, nm, after):
        return _cast_to_slot(w[0], chip_arr, f"cast_{nm}", after=after, natural=nm.startswith(("w1_", "w3_")))

    ffn1_bufs = [cast(w1_ffn1, "w1_ffn1", [mod]), cast(w3_ffn1, "w3_ffn1", [mod]), cast(w2_ffn1, "w2_ffn1", [mod])]
    near = _ex_gather_near(ffn1_bufs)
    state, started = _exchange_start(near, "gather_ffn1_near_start")
    mixer_bufs = [cast(w_in, "w_in", [started]), cast(w_out, "w_out", [started])]
    ffn2_bufs = [cast(w1_ffn2, "w1_ffn2", [started]), cast(w3_ffn2, "w3_ffn2", [started]),
                 cast(w2_ffn2, "w2_ffn2", [started])]
    ffn1_bufs = _exchange_wait(near, state, mixer_bufs + ffn2_bufs, "gather_ffn1_near_wait")
    relay = _ex_gather_relay(ffn1_bufs)
    state, started = _exchange_start(relay, "gather_ffn1_relay_start")
    h1 = _norm_mod(xs, g_ffn1, sh1, sc1, "ffn1_norm", after=[started])
    ffn1_bufs = _exchange_wait(relay, state, [h1], "gather_ffn1_relay_wait")
    ffn1_bufs = _exchange_alone(_ex_gather_d2d(ffn1_bufs, which=(2,)), "gather_ffn1_d2d")
    w1a, w3a, w2a = ffn1_bufs[0], ffn1_bufs[1], jnp.reshape(ffn1_bufs[2], (F, D))

    onehot_np, in_window_np, first_np = _bias_tables()
    onehot = jnp.asarray(onehot_np)
    bias = jnp.einsum("bij,bh->hij", onehot, rel_bias, precision=lax.Precision.HIGHEST)
    biasm = jnp.stack([jnp.where(jnp.asarray(first_np)[None], bias, NEG),
                       jnp.where(jnp.asarray(in_window_np)[None], bias, NEG)])
    causal = jnp.asarray(np.tril(np.ones((BLOCK, BLOCK), dtype=bool)))
    wm = jnp.where(causal[None], spatial_w[0], 0.0).astype(BF16)
    wm_t = jnp.transpose(wm, (0, 2, 1))
    sink_vec = sinks[0]
    per_group = PAIRS // KV_HEADS
    sbp = jnp.broadcast_to(jnp.reshape(spatial_b[0], (A_HEADS * BLOCK, 1)), (A_HEADS * BLOCK, A_DIM))
    gvp = jnp.repeat(g_v[0], BLOCK, axis=0)
    gq2, gk2 = jnp.concatenate([g_q, g_q], axis=1), jnp.concatenate([g_k, g_k], axis=1)
    seg_ones = jnp.asarray(np.kron(np.eye(2, dtype=np.float32), np.ones((HEAD_DIM, HEAD_DIM), np.float32)), BF16)
    lane_ones = jnp.full((LANES, LANES), 1.0 / LANES, BF16)
    pair_fold = jnp.asarray(np.kron(np.eye(per_group, LANES, dtype=np.float32), np.ones((BLOCK, 1), np.float32)))
    biasp = jnp.reshape(jnp.transpose(jnp.reshape(biasm, (2, KV_HEADS, per_group, 2, BLOCK, BAND)), (0, 1, 3, 5, 2, 4)),
                        (2, KV_HEADS, 2 * BAND, PAIR_ROWS))

    res = _ffn_fwd(xs, g_ffn1, sh1, sc1, gt1, w1a, w3a, w2a, None, "ffn1_fwd", ex=_ex_gather_ici(mixer_bufs + ffn2_bufs),
                   h=h1)
    (x1, a1, b1, f1), mixer_bufs, ffn2_bufs = res[:4], res[4:6], res[6:]
    h2, *mixer_bufs = _norm_mod(x1, g_mix, sh2, sc2, "mixer_norm", ex=_ex_gather_d2d(mixer_bufs))
    win, wout = cols_to_natural(mixer_bufs[0]), jnp.reshape(mixer_bufs[1], (-1, D))
    z, *ffn2_bufs = _matmul(h2, win, "nn", F32, 1024, _tile(IN_COLS, 1664), D, "mixer_in", ex=_ex_gather_d2d(ffn2_bufs))
    w1b, w3b, w2b = ffn2_bufs[0], ffn2_bufs[1], jnp.reshape(ffn2_bufs[2], (F, D))
    mix = _mixer_fwd(z, wm, sbp, gvp, gq2, gk2, seg_ones, lane_ones, sink_vec, biasp, "mixer_fwd")
    x2, ymix = _mixer_out(mix, wout, x1, gt2, "mixer_out")
    g3, df3, h3, a3, b3, dgt3, loss_sum = _ffn_fwd(x2, g_ffn2, sh3, sc3, gt3, w1b, w3b, w2b, tgt, "ffn2_fwd_loss")
    loss = lax.psum(loss_sum[0, 0] * (0.5 / D), ("x", "y", "c"))

    tk = _tile(S, 2048)

    def ffn_weight_grads(h, da, db, s, df, tag, riding):
        ex, done = _ride(riding) if riding else (None, None)
        gw1 = _matmul(h, da, "tn", F32, 1024, F // N_CHIPS, tk, f"grad_w1_{tag}", shard_major=True, ex=ex)
        if riding:
            done(gw1[1:])
            gw1 = gw1[0]
        r1 = _Reduction(gw1, f"w1_{tag}", c_arr, jc_arr)
        ex, done = _ride([r1])
        gw3, *carried = _matmul(h, db, "tn", F32, 1024, F // N_CHIPS, tk, f"grad_w3_{tag}", shard_major=True, ex=ex)
        done(carried)
        r3 = _Reduction(gw3, f"w3_{tag}", c_arr, jc_arr)
        ex, done = _ride([r1, r3])
        gw2, *carried = _matmul(s, df, "tn", F32, _tile(F, 1408), 1024, tk, f"grad_w2_{tag}", ex=ex)
        done(carried)
        r2 = _Reduction(jnp.reshape(gw2, (N_CHIPS, F // N_CHIPS, D)), f"w2_{tag}", c_arr, jc_arr)
        return r1, r3, r2

    da3, db3, s3, dh3 = _ffn_bwd(df3, a3, b3, w1b, w3b, w2b, "ffn2_bwd")
    r21, r23, r22 = ffn_weight_grads(h3, da3, db3, s3, df3, "ffn2", [])
    ex, done = _ride([r21, r22])
    state, started = _exchange_start(ex, "reduce_ffn2_start")
    g2, dsh3, dsc3, dgn3, dy, dgt2 = _norm_bwd(dh3, x2, g3, g_ffn2, sc3, (ymix, gt2, 1.0), "ffn2_norm_bwd",
                                               after=[started])
    dmix = _matmul(dy, wout, "nt", BF16, 1024, 2048, D, "mixer_out_bwd")
    done(_exchange_wait(ex, state, [dmix], "reduce_ffn2_wait"))
    ex, done = _ride([r23, r22])
    res = _mixer_bwd(z, dmix, wm, wm_t, sbp, gvp, gq2, gk2, seg_ones, lane_ones, sink_vec, biasp, pair_fold,
                     "mixer_bwd", ex=ex)
    dz_main, dz_kv, dwm, dsb, dgv, dgq, dgk, dsk, dst = res[:9]
    dsb = jnp.reshape(dsb[:, 0], (A_HEADS, BLOCK))
    dgq, dgk = dgq[:, :HEAD_DIM], dgk[:, :HEAD_DIM]
    dsk = jnp.reshape(jnp.transpose(jnp.reshape(dsk[:2 * KV_HEADS, :per_group], (KV_HEADS, 2, per_group)), (0, 2, 1)),
                      (1, B_HEADS))
    dst = jnp.reshape(jnp.transpose(jnp.reshape(dst, (KV_HEADS, 2, BAND, per_group, BLOCK)), (0, 3, 1, 4, 2)),
                      (B_HEADS, BLOCK * BAND))
    done(res[9:])
    dz = jnp.concatenate([dz_main, dz_kv], axis=1)
    ex, done = _ride([r23, r22])
    dh2, *carried = _matmul(dz, win, "nt", F32, 1024, 2048, _tile(IN_COLS, 1664), "mixer_in_bwd", ex=ex)
    done(carried)
    drel = _bucket_sum(dst, jnp.reshape(onehot, (N_BUCKETS, -1)), "bucket_sum")
    g1, dsh2, dsc2, dgn2, df1, dgt1 = _norm_bwd(dh2, x1, g2, g_mix, sc2, (f1, gt1, 0.5), "mixer_norm_bwd")

    da1, db1, s1, dh1 = _ffn_bwd(df1, a1, b1, w1a, w3a, w2a, "ffn1_bwd")
    r11, r13, r12 = ffn_weight_grads(h1, da1, db1, s1, df1, "ffn1", [])
    ex, done = _ride([r11, r13, r12])
    gwin_full, *carried = _matmul(h2, dz, "tn", F32, 1024, _tile(IN_COLS, 1664), tk, "grad_w_in", ex=ex)
    done(carried)
    rm_in = _Reduction(jnp.transpose(jnp.reshape(gwin_full, (D, N_CHIPS, -1)), (1, 0, 2)), "w_in", c_arr, jc_arr)
    ex, done = _ride([r13, r12, rm_in])
    state, started = _exchange_start(ex, "reduce_late_start")
    gwout_full = _matmul(mix, dy, "tn", F32, 1024, 1024, tk, "grad_w_out", after=[started])
    grad_x, dsh1, dsc1, dgn1 = _norm_bwd(dh1, xs, g1, g_ffn1, sc1, None, "ffn1_norm_bwd", after=[started])
    done(_exchange_wait(ex, state, [gwout_full, grad_x], "reduce_late_wait"))
    rm_out = _Reduction(jnp.reshape(gwout_full, (N_CHIPS, -1, D)), "w_out", c_arr, jc_arr)

    dmod = jnp.concatenate([dsh1, dsc1, dgt1, dsh2, dsc2, dgt2, dsh3, dsc3, dgt3], axis=1)
    small_w = [b_ada, g_ffn1, g_mix, g_ffn2, spatial_w, spatial_b, g_v, g_q, g_k, sinks, rel_bias]
    small_m = [m_b_ada, m_g_ffn1, m_g_mix, m_g_ffn2, m_spatial_w, m_spatial_b, m_g_v, m_g_q, m_g_k, m_sinks, m_rel_bias]
    small_v = [v_b_ada, v_g_ffn1, v_g_mix, v_g_ffn2, v_spatial_w, v_spatial_b, v_g_v, v_g_q, v_g_k, v_sinks, v_rel_bias]
    small_g = [dmod, dgn1, dgn2, dgn3, jnp.where(causal[None], dwm, 0.0), dsb, dgv, dgq, dgk, dsk, drel]
    shapes = [w.shape for w in small_w]
    gpack = _pack(small_g)
    rows = gpack.shape[0]
    gall = jnp.reshape(_allgather_small(gpack, "gather_small"), (N_DEV, rows, LANES))
    sg, sd, sm, sv = _small_update(gall, _pack(small_w), _pack(small_m), _pack(small_v), "small_update")
    sg, sd, sm, sv = [_unpack(p, shapes) for p in (sg, sd, sm, sv)]

    mod_rows = -(-N_MOD * D // (SUBLANES * LANES)) * SUBLANES
    dmod_all = jnp.reshape(gall[:, :mod_rows], (N_DEV, -1))[:, :N_MOD * D]
    dmod_sh = lax.dynamic_slice(dmod_all, (0, chip * mod_cols), (N_DEV, mod_cols))
    dmod_sh = jnp.pad(dmod_sh, ((0, BF16_ROWS - N_DEV), (0, 0))).astype(BF16)
    g_wada = _matmul(c_act, dmod_sh, "tn", F32, 1024, _tile(mod_cols, 512), BF16_ROWS, "grad_w_ada")

    big = {}

    def update(nm, w, g, m, v, token):
        g_out, d, nm_, nv_ = _adamw(w[0], g, m[0], v[0], f"adamw_{nm}", emit_grad=True, after=[token])
        big[nm] = (g_out[None], d[None], nm_[None], nv_[None])
        return d

    ex, done = _ride([r12, rm_in, rm_out])
    state, token = _exchange_start(ex, "reduce_tail_0_start")
    behind = [update("w1_ffn2", w1_ffn2, r21.result, m_w1_ffn2, v_w1_ffn2, token),
              update("w3_ffn2", w3_ffn2, r23.result, m_w3_ffn2, v_w3_ffn2, token),
              update("w2_ffn2", w2_ffn2, r22.result, m_w2_ffn2, v_w2_ffn2, token)]
    done(_exchange_wait(ex, state, behind, "reduce_tail_0_wait"))
    ex, done = _ride([rm_in, rm_out])
    state, token = _exchange_start(ex, "reduce_tail_1_start")
    behind = [update("w1_ffn1", w1_ffn1, r11.result, m_w1_ffn1, v_w1_ffn1, token),
              update("w3_ffn1", w3_ffn1, r13.result, m_w3_ffn1, v_w3_ffn1, token),
              update("w2_ffn1", w2_ffn1, r12.result, m_w2_ffn1, v_w2_ffn1, token)]
    done(_exchange_wait(ex, state, behind, "reduce_tail_1_wait"))
    ex, done = _ride([rm_out])
    state, token = _exchange_start(ex, "reduce_tail_2_start")
    d_wada, nm_wada, nv_wada = _adamw(w_ada[0], g_wada, m_w_ada[0], v_w_ada[0], "adamw_w_ada", after=[token])
    behind = [d_wada, update("w_in", w_in, rm_in.result, m_w_in, v_w_in, token)]
    done(_exchange_wait(ex, state, behind, "reduce_tail_2_wait"))
    update("w_out", w_out, rm_out.result, m_w_out, v_w_out, token)
    big["w_ada"] = (g_wada[None], d_wada[None], nm_wada[None], nv_wada[None])

    order = ["w_ada", "b_ada", "g_ffn1", "w1_ffn1", "w3_ffn1", "w2_ffn1", "g_mix", "w_in", "spatial_w", "spatial_b",
             "g_v", "g_q", "g_k", "sinks", "rel_bias", "w_out", "g_ffn2", "w1_ffn2", "w3_ffn2", "w2_ffn2"]
    small_names = ["b_ada", "g_ffn1", "g_mix", "g_ffn2", "spatial_w", "spatial_b", "g_v", "g_q", "g_k", "sinks", "rel_bias"]
    for i, nm in enumerate(small_names):
        big[nm] = (sg[i], sd[i], sm[i], sv[i])
    outs = [loss, grad_x[None]]
    for kind in range(4):
        outs += [big[nm][kind] for nm in order]
    return tuple(outs)
```

```python
import functools
import math

import jax
import jax.numpy as jnp
import numpy as np
from jax import lax
from jax.experimental import pallas as pl
from jax.experimental.pallas import tpu as pltpu

F32 = jnp.float32
BF16 = jnp.bfloat16
MESH = pl.DeviceIdType.MESH
ANY = pl.BlockSpec(memory_space=pl.ANY)

EPS = 1e-6
BLOCK = 128
A_HEADS = 8
A_DIM = 128
A_WIDTH = A_HEADS * A_DIM
B_HEADS = 16
KV_HEADS = 2
GROUP = B_HEADS // KV_HEADS
HEAD_DIM = 64
B_WIDTH = B_HEADS * HEAD_DIM
KV_WIDTH = KV_HEADS * HEAD_DIM
Q_OFF = 2 * A_WIDTH
K_OFF = Q_OFF + B_WIDTH
V_OFF = K_OFF + KV_WIDTH
IN_COLS = V_OFF + KV_WIDTH
N_BUCKETS = 32
MAX_DISTANCE = 128
N_MOD = 9
N_CHIPS = 4
N_DEV = 8
NEG = -1e30

ADAM_LR = 0.001
ADAM_B1 = 0.9
ADAM_B2 = 0.999
ADAM_EPS = 1e-08
ADAM_WD = 0.01
ADAM_STEP = 10

LANES = 128
SUBLANES = 8
BF16_ROWS = 16
VMEM_LIMIT = 60 * 1024 * 1024

INV_SQRT2 = 1.0 / math.sqrt(2.0)
INV_SQRT_2PI = 1.0 / math.sqrt(2.0 * math.pi)


def _tile(n, pref, mult=LANES):
    t = (min(pref, n) // mult) * mult
    while t >= mult:
        if n % t == 0:
            return t
        t -= mult
    return n


def _params(sem):
    return pltpu.CompilerParams(dimension_semantics=sem, vmem_limit_bytes=VMEM_LIMIT)


class _Exchange:
    def __init__(self, operands, out_shapes, aliases, n_sems, plan):
        self.operands, self.out_shapes, self.aliases, self.n_sems, self.plan = operands, out_shapes, aliases, n_sems, plan


def _pallas(body, *, name, grid, in_specs, out_specs, out_shape, args, scratch_shapes=(), semantics=None, ex=None,
            after=()):
    if ex is None:
        n_in = len(in_specs)

        def ordered(*refs):
            body(*refs[:n_in], *refs[n_in + len(after):])

        return pl.pallas_call(ordered if after else body, name=name, grid=grid,
                              in_specs=list(in_specs) + [ANY] * len(after), out_specs=out_specs, out_shape=out_shape,
                              scratch_shapes=list(scratch_shapes), compiler_params=_params(semantics))(*args, *after)
    assert not after
    n_in, n_out, n_scr = len(in_specs), len(out_specs), len(scratch_shapes)
    e_in, e_out = len(ex.operands), len(ex.out_shapes)

    def wrapped(*refs):
        ins, refs = refs[:n_in], refs[n_in:]
        ex_ins, refs = refs[:e_in], refs[e_in:]
        outs, refs = refs[:n_out], refs[n_out:]
        ex_outs, refs = refs[:e_out], refs[e_out:]
        scratch, (send_sems, recv_sems) = refs[:n_scr], refs[n_scr:]
        first, last = True, True
        for d, size in enumerate(grid):
            first = jnp.logical_and(first, pl.program_id(d) == 0)
            last = jnp.logical_and(last, pl.program_id(d) == size - 1)

        def start():
            sends, _ = ex.plan(ex_ins, ex_outs, send_sems, recv_sems)
            for cp in sends:
                _remote(*cp).start()

        def finish():
            sends, arrivals = ex.plan(ex_ins, ex_outs, send_sems, recv_sems)
            for cp in arrivals:
                _remote(*cp).wait_recv()
            for cp in sends:
                _remote(*cp).wait_send()

        if grid:
            pl.when(first)(start)
        else:
            start()
        if body is not None:
            body(*ins, *outs, *scratch)
        if grid:
            pl.when(last)(finish)
        else:
            finish()

    kwargs = dict(grid=grid) if grid else {}
    return pl.pallas_call(
        wrapped,
        name=name,
        in_specs=list(in_specs) + [ANY] * e_in,
        out_specs=list(out_specs) + [ANY] * e_out,
        out_shape=list(out_shape) + list(ex.out_shapes),
        input_output_aliases={n_in + i: n_out + o for i, o in ex.aliases.items()},
        scratch_shapes=list(scratch_shapes) + [pltpu.SemaphoreType.DMA((ex.n_sems,)), pltpu.SemaphoreType.DMA((ex.n_sems,))],
        compiler_params=_params(("arbitrary",) * len(grid) if grid else None),
        **kwargs,
    )(*args, *ex.operands)


def _dot(a, b, dims=(((1,), (0,)), ((), ()))):
    return lax.dot_general(a, b, dims, preferred_element_type=F32)


NN = (((1,), (0,)), ((), ()))
NT = (((1,), (1,)), ((), ()))
TN = (((0,), (0,)), ((), ()))


def _sigmoid(x):
    return 1.0 / (1.0 + jnp.exp(-x))


def _gelu_and_grad(x):
    cdf = 0.5 * (1.0 + lax.erf(x * INV_SQRT2))
    pdf = jnp.exp(-0.5 * x * x) * INV_SQRT_2PI
    return x * cdf, cdf + x * pdf


def _gelu(x):
    return x * (0.5 * (1.0 + lax.erf(x * INV_SQRT2)))


def _rms(x):
    r = lax.rsqrt(jnp.mean(x * x, axis=-1, keepdims=True) + EPS)
    return x * r, r


ROW_CHUNK = 64


def _for_rows(tm, fn):
    rc = min(ROW_CHUNK, tm)

    def step(r, carry):
        fn(pl.ds(pl.multiple_of(r * rc, rc), rc))
        return carry

    lax.fori_loop(0, tm // rc, step, 0)


def _rms_bwd(dy, xhat, r):
    return r * (dy - xhat * jnp.mean(dy * xhat, axis=-1, keepdims=True))


def _matmul(a, b, mode, out_dtype, tm, tn, tk, name, shard_major=False, ex=None, after=()):
    if mode == "nn":
        (M, K), N = a.shape, b.shape[1]
    elif mode == "nt":
        (M, K), N = a.shape, b.shape[0]
    else:
        (K, M), N = a.shape, b.shape[1]
    tm, tn, tk = min(tm, M), min(tn, N), min(tk, K)
    assert M % tm == 0 and N % tn == 0 and K % tk == 0, (name, M, N, K, tm, tn, tk)
    nk = K // tk
    dims = {"nn": NN, "nt": NT, "tn": TN}[mode]
    a_spec = pl.BlockSpec((tk, tm), lambda i, j, k: (k, i)) if mode == "tn" else pl.BlockSpec((tm, tk), lambda i, j, k: (i, k))
    b_spec = pl.BlockSpec((tn, tk), lambda i, j, k: (j, k)) if mode == "nt" else pl.BlockSpec((tk, tn), lambda i, j, k: (k, j))
    if shard_major:
        assert tn * N_CHIPS == N
        out_shape = jax.ShapeDtypeStruct((N_CHIPS, M, tn), out_dtype)
        o_spec = pl.BlockSpec((None, tm, tn), lambda i, j, k: (j, i, 0))
    else:
        out_shape = jax.ShapeDtypeStruct((M, N), out_dtype)
        o_spec = pl.BlockSpec((tm, tn), lambda i, j, k: (i, j))

    direct = nk == 1 or out_dtype == F32

    def body(a_ref, b_ref, o_ref, *scratch):
        k = pl.program_id(2)
        if nk == 1:
            o_ref[...] = _dot(a_ref[...], b_ref[...], dims).astype(o_ref.dtype)
            return
        acc_ref = o_ref if direct else scratch[0]

        @pl.when(k == 0)
        def _():
            acc_ref[...] = jnp.zeros(acc_ref.shape, F32)

        acc_ref[...] += _dot(a_ref[...], b_ref[...], dims)
        if not direct:
            @pl.when(k == nk - 1)
            def _():
                o_ref[...] = acc_ref[...].astype(o_ref.dtype)

    outs = _pallas(body, name=name, grid=(M // tm, N // tn, nk), in_specs=[a_spec, b_spec], out_specs=[o_spec],
                   out_shape=[out_shape], scratch_shapes=[] if direct else [pltpu.VMEM((tm, tn), F32)],
                   semantics=("parallel", "parallel", "arbitrary"), args=[a, b], ex=ex, after=after)
    return outs[0] if ex is None else outs


def _mod_partial(c_all, w_ada, b_sh, name):
    R, D = c_all.shape
    N = w_ada.shape[1]
    tn = _tile(N, 512)

    def body(c_ref, w_ref, b_ref, o_ref, ca_ref):
        cv = c_ref[...]
        ca = (cv * _sigmoid(cv)).astype(BF16)
        ca_ref[...] = ca
        o_ref[...] = _dot(ca, w_ref[...].astype(BF16)) + b_ref[...]

    return pl.pallas_call(
        body,
        name=name,
        grid=(N // tn,),
        in_specs=[
            pl.BlockSpec((R, D), lambda j: (0, 0)),
            pl.BlockSpec((D, tn), lambda j: (0, j)),
            pl.BlockSpec((1, tn), lambda j: (0, j)),
        ],
        out_specs=[pl.BlockSpec((R, tn), lambda j: (0, j)), pl.BlockSpec((R, D), lambda j: (0, 0))],
        out_shape=[jax.ShapeDtypeStruct((R, N), F32), jax.ShapeDtypeStruct((R, D), BF16)],
        compiler_params=_params(("arbitrary",)),
    )(c_all, w_ada, b_sh)


FFN_BLOCK = 1024


def _ffn_blocks(F):
    if F % FFN_BLOCK == 0 or F < FFN_BLOCK:
        tf = _tile(F, FFN_BLOCK)
        return tf, F // tf, tf
    nj = -(-F // FFN_BLOCK)
    tail = F - (nj - 1) * FFN_BLOCK
    assert tail % LANES == 0
    return FFN_BLOCK, nj, tail
def _ffn_fwd(x, g, sh, sc, gt, w1, w3, w2, tgt, name, ex=None, h=None):
    S, D = x.shape
    F = w1.shape[1]
    tm, tf, nj, tail = _tile(S, 512), *_ffn_blocks(F)
    ni = S // tm
    with_loss = tgt is not None
    assert h is None or not with_loss

    def body(*refs):
        if with_loss:
            (x_ref, g_ref, sh_ref, sc_ref, gt_ref, w1_ref, w3_ref, w2_ref, tgt_ref,
             gout_ref, df_ref, h_ref, a_ref, b_ref, dgt_ref, loss_ref, acc_ref) = refs
        elif h is None:
            (x_ref, g_ref, sh_ref, sc_ref, gt_ref, w1_ref, w3_ref, w2_ref,
             xo_ref, h_ref, a_ref, b_ref, f_ref, acc_ref) = refs
        else:
            (x_ref, g_ref, sh_ref, sc_ref, gt_ref, w1_ref, w3_ref, w2_ref, h_ref,
             xo_ref, a_ref, b_ref, f_ref, acc_ref) = refs
        i, j = pl.program_id(0), pl.program_id(1)

        if h is None:
            @pl.when(j == 0)
            def _():
                def prologue(rows):
                    xhat, _ = _rms(x_ref[rows, :])
                    h_ref[rows, :] = ((xhat * g_ref[...]) * (1.0 + sc_ref[...]) + sh_ref[...]).astype(BF16)

                _for_rows(tm, prologue)

        @pl.when(j == 0)
        def _():
            acc_ref[...] = jnp.zeros(acc_ref.shape, F32)

        def columns(width):
            def run():
                hb = h_ref[...]
                av = _dot(hb, w1_ref[:, :width])
                bv = _dot(hb, w3_ref[:, :width])
                a_ref[:, :width] = av.astype(BF16)
                b_ref[:, :width] = bv.astype(BF16)
                sv = ((av * _sigmoid(av)) * bv).astype(BF16)
                acc_ref[...] += _dot(sv, w2_ref[:width, :])
            return run

        if tail == tf:
            columns(tf)()
        else:
            pl.when(j < nj - 1)(columns(tf))
            pl.when(j == nj - 1)(columns(tail))

        @pl.when(j == nj - 1)
        def _():
            if with_loss:
                @pl.when(i == 0)
                def _():
                    dgt_ref[...] = jnp.zeros(dgt_ref.shape, F32)
                    loss_ref[...] = jnp.zeros(loss_ref.shape, F32)

            def epilogue(rows):
                fv = acc_ref[rows, :]
                half_gate = 0.5 * gt_ref[...]
                xo = x_ref[rows, :] + half_gate * fv
                if not with_loss:
                    xo_ref[rows, :] = xo
                    f_ref[rows, :] = fv.astype(f_ref.dtype)
                    return
                err = xo - tgt_ref[rows, :]
                gout = err * (1.0 / D)
                gout_ref[rows, :] = gout
                df_ref[rows, :] = (half_gate * gout).astype(BF16)
                dgt_ref[...] += 0.5 * jnp.sum(gout * fv, axis=0, keepdims=True)
                loss_part = jnp.sum(jnp.sum(err * err, axis=1, keepdims=True), axis=0, keepdims=True)
                loss_ref[...] += jnp.broadcast_to(loss_part, loss_ref.shape)

            _for_rows(tm, epilogue)

    row = pl.BlockSpec((tm, D), lambda i, j: (i, 0))
    row_in = pl.BlockSpec((tm, D), lambda i, j: (i, 0), pipeline_mode=pl.Buffered(1))
    vec = pl.BlockSpec((1, D), lambda i, j: (0, 0))
    col = pl.BlockSpec((tm, tf), lambda i, j: (i, j))
    in_specs = [row_in, vec, vec, vec, vec,
                pl.BlockSpec((D, tf), lambda i, j: (0, j)),
                pl.BlockSpec((D, tf), lambda i, j: (0, j)),
                pl.BlockSpec((tf, D), lambda i, j: (j, 0))]
    args = [x, g, sh, sc, gt, w1, w3, w2]
    act = jax.ShapeDtypeStruct((S, F), BF16)
    if with_loss:
        in_specs.append(row_in)
        args.append(tgt)
        out_specs = [row, row, row_in, col, col, vec, pl.BlockSpec((1, LANES), lambda i, j: (0, 0))]
        out_shape = [jax.ShapeDtypeStruct((S, D), F32), jax.ShapeDtypeStruct((S, D), BF16),
                     jax.ShapeDtypeStruct((S, D), BF16), act, act,
                     jax.ShapeDtypeStruct((1, D), F32), jax.ShapeDtypeStruct((1, LANES), F32)]
    elif h is None:
        out_specs = [row, row, col, col, row]
        out_shape = [jax.ShapeDtypeStruct((S, D), F32), jax.ShapeDtypeStruct((S, D), BF16), act, act,
                     jax.ShapeDtypeStruct((S, D), BF16)]
    else:
        in_specs.append(row_in)
        args.append(h)
        out_specs = [row, col, col, row]
        out_shape = [jax.ShapeDtypeStruct((S, D), F32), act, act, jax.ShapeDtypeStruct((S, D), BF16)]
    return _pallas(body, name=name, grid=(ni, nj), in_specs=in_specs, out_specs=out_specs, out_shape=out_shape,
                   scratch_shapes=[pltpu.VMEM((tm, D), F32)],
                   semantics=("arbitrary", "arbitrary"), args=args, ex=ex)


def _ffn_bwd(df, a, b, w1, w3, w2, name, ex=None):
    S, D = df.shape
    F = a.shape[1]
    tm, tf, nj, tail = _tile(S, 512), *_ffn_blocks(F)

    def body(df_ref, a_ref, b_ref, w1_ref, w3_ref, w2_ref, da_ref, db_ref, s_ref, dh_out_ref, dh_ref):
        j = pl.program_id(1)

        @pl.when(j == 0)
        def _():
            dh_ref[...] = jnp.zeros(dh_ref.shape, F32)

        def columns(width):
            def run():
                ds = _dot(df_ref[...], w2_ref[:width, :], NT)
                av = a_ref[:, :width].astype(F32)
                bv = b_ref[:, :width].astype(F32)
                sig = _sigmoid(av)
                sil = av * sig
                da = ((ds * bv) * (sig * (1.0 + av * (1.0 - sig)))).astype(BF16)
                db = (ds * sil).astype(BF16)
                da_ref[:, :width] = da
                db_ref[:, :width] = db
                s_ref[:, :width] = (sil * bv).astype(BF16)
                dh_ref[...] += _dot(da, w1_ref[:, :width], NT) + _dot(db, w3_ref[:, :width], NT)
            return run

        if tail == tf:
            columns(tf)()
        else:
            pl.when(j < nj - 1)(columns(tf))
            pl.when(j == nj - 1)(columns(tail))

        @pl.when(j == nj - 1)
        def _():
            dh_out_ref[...] = dh_ref[...].astype(BF16)

    row = pl.BlockSpec((tm, D), lambda i, j: (i, 0))
    col = pl.BlockSpec((tm, tf), lambda i, j: (i, j))
    act = jax.ShapeDtypeStruct((S, F), BF16)
    return _pallas(body, name=name, grid=(S // tm, nj), scratch_shapes=[pltpu.VMEM((tm, D), F32)],
                   in_specs=[row, col, col,
                             pl.BlockSpec((D, tf), lambda i, j: (0, j)),
                             pl.BlockSpec((D, tf), lambda i, j: (0, j)),
                             pl.BlockSpec((tf, D), lambda i, j: (j, 0))],
                   out_specs=[col, col, col, row],
                   out_shape=[act, act, act, jax.ShapeDtypeStruct((S, D), BF16)],
                   semantics=("parallel", "arbitrary"), args=[df, a, b, w1, w3, w2], ex=ex)


def _norm_mod(x, g, sh, sc, name, ex=None, after=()):
    S, D = x.shape
    tm = _tile(S, 512)

    def body(x_ref, g_ref, sh_ref, sc_ref, h_ref):
        def step(rows):
            xhat, _ = _rms(x_ref[rows, :])
            h_ref[rows, :] = ((xhat * g_ref[...]) * (1.0 + sc_ref[...]) + sh_ref[...]).astype(BF16)

        _for_rows(tm, step)

    row = pl.BlockSpec((tm, D), lambda i: (i, 0))
    vec = pl.BlockSpec((1, D), lambda i: (0, 0))
    outs = _pallas(body, name=name, grid=(S // tm,), in_specs=[row, vec, vec, vec], out_specs=[row],
                   out_shape=[jax.ShapeDtypeStruct((S, D), BF16)], semantics=("parallel",), args=[x, g, sh, sc], ex=ex,
                   after=after)
    return outs[0] if ex is None else outs


def _norm_bwd(dh, x, gres, g, sc, prev, name, ex=None, after=()):
    S, D = x.shape
    tm = _tile(S, 256)
    has_prev = prev is not None
    coef = prev[2] if has_prev else None

    def body(*refs):
        if has_prev:
            (dh_ref, x_ref, gr_ref, g_ref, sc_ref, f_ref, gt_ref,
             go_ref, dsh_ref, dsc_ref, dg_ref, dp_ref, dgt_ref) = refs
        else:
            dh_ref, x_ref, gr_ref, g_ref, sc_ref, go_ref, dsh_ref, dsc_ref, dg_ref = refs
        sum_refs = [dsh_ref, dsc_ref, dg_ref] + ([dgt_ref] if has_prev else [])

        @pl.when(pl.program_id(0) == 0)
        def _():
            for ref in sum_refs:
                ref[...] = jnp.zeros(ref.shape, F32)

        def step(rows):
            dh = dh_ref[rows, :].astype(F32)
            xhat, r = _rms(x_ref[rows, :])
            gain = g_ref[...]
            scale1 = 1.0 + sc_ref[...]
            gout = gr_ref[rows, :] + _rms_bwd(dh * scale1 * gain, xhat, r)
            go_ref[rows, :] = gout
            sums = [dh, dh * (xhat * gain), dh * scale1 * xhat]
            if has_prev:
                dp_ref[rows, :] = ((coef * gt_ref[...]) * gout).astype(BF16)
                sums.append(coef * (gout * f_ref[rows, :].astype(F32)))
            for ref, v in zip(sum_refs, sums):
                ref[...] += jnp.sum(v, axis=0, keepdims=True)

        _for_rows(tm, step)

    row = pl.BlockSpec((tm, D), lambda i: (i, 0))
    vec = pl.BlockSpec((1, D), lambda i: (0, 0))
    vshape = jax.ShapeDtypeStruct((1, D), F32)
    in_specs = [row, row, row, vec, vec]
    args = [dh, x, gres, g, sc]
    out_specs = [row, vec, vec, vec]
    out_shape = [jax.ShapeDtypeStruct((S, D), F32), vshape, vshape, vshape]
    if has_prev:
        in_specs += [row, vec]
        args += [prev[0], prev[1]]
        out_specs += [row, vec]
        out_shape += [jax.ShapeDtypeStruct((S, D), BF16), vshape]
    return _pallas(body, name=name, grid=(S // tm,), in_specs=in_specs, out_specs=out_specs, out_shape=out_shape,
                   semantics=("arbitrary",), args=args, ex=ex, after=after)


PAIRS = B_HEADS // 2
PAIR_ROWS = (PAIRS // KV_HEADS) * BLOCK
BAND = 2 * BLOCK


def _stack(ref, offset, count):
    return jnp.concatenate([ref[:, offset + p * LANES:offset + (p + 1) * LANES] for p in range(count)], axis=0)


def _seg_mean(x, e_ref):
    return _dot(x.astype(BF16), e_ref[...]) * (1.0 / HEAD_DIM)


def _block_diag(x, x_rolled, left, kv_head):
    if kv_head == 0:
        top, bottom = jnp.where(left, x, 0.0), jnp.where(left, 0.0, x_rolled)
    else:
        top, bottom = jnp.where(left, x_rolled, 0.0), jnp.where(left, 0.0, x)
    return jnp.concatenate([top, bottom], axis=0).astype(BF16)


def _from_block_diag(g, left, kv_head):
    a, b = g[:BAND], g[BAND:]
    if kv_head == 0:
        return jnp.where(left, a + pltpu.roll(b, HEAD_DIM, 1), 0.0)
    return jnp.where(left, 0.0, pltpu.roll(a, HEAD_DIM, 1) + b)


def _pair_softmax(st, sk_ref, kv_head):
    out = []
    for e in range(2):
        seg = st[e * BAND:(e + 1) * BAND]
        sink = jnp.concatenate([jnp.full((1, BLOCK), sk_ref[kv_head * GROUP + 2 * p + e], F32)
                                for p in range(PAIRS // KV_HEADS)], axis=1)
        m = jnp.maximum(jnp.max(seg, axis=0, keepdims=True), sink)
        p_ = jnp.exp(seg - m)
        e_sink = jnp.exp(sink - m)
        inv = 1.0 / (jnp.sum(p_, axis=0, keepdims=True) + e_sink)
        out.append((p_ * inv, e_sink * inv))
    return out


def _lane_mean(x, ones_ref):
    return _dot(x.astype(BF16), ones_ref[...])


def _mixer_specs(nb, last):
    full = lambda shape: pl.BlockSpec(shape, lambda n: (0,) * len(shape))
    z_spec = pl.BlockSpec((BLOCK, IN_COLS), lambda n: (jnp.minimum(n, last), 0))
    zp_spec = pl.BlockSpec((BLOCK, 2 * KV_WIDTH), lambda n: (jnp.clip(n - 1, 0, last), K_OFF // (2 * KV_WIDTH)))
    consts = [full((A_HEADS * BLOCK, A_DIM)), full((A_HEADS * BLOCK, A_DIM)), full((1, LANES)), full((1, LANES)),
              full((LANES, LANES)), full((LANES, LANES)), pl.BlockSpec(memory_space=pltpu.SMEM),
              pl.BlockSpec((None, KV_HEADS, PAIR_ROWS, 2 * BAND), lambda n: (jnp.minimum(n, 1), 0, 0, 0))]
    return full, z_spec, zp_spec, consts


def _mixer_fwd(z, wm, sbp, gvp, gq2, gk2, seg_ones, lane_ones, sinks, biasp, name):
    S = z.shape[0]
    nb = S // BLOCK

    def body(z_ref, zp_ref, wm_ref, sbp_ref, gvp_ref, gq2_ref, gk2_ref, e_ref, l_ref, sk_ref, bias_ref, mix_ref):
        u = _gelu(_stack(z_ref, 0, A_HEADS))
        v = _gelu(_stack(z_ref, A_WIDTH, A_HEADS))
        vhat = v * lax.rsqrt(_lane_mean(v * v, l_ref) + EPS)
        vn = (vhat * gvp_ref[...]).astype(BF16)
        mixed = jnp.concatenate([_dot(wm_ref[h], vn[h * BLOCK:(h + 1) * BLOCK]) for h in range(A_HEADS)], axis=0)
        ya = (u * (mixed + sbp_ref[...])).astype(BF16)
        for h in range(A_HEADS):
            mix_ref[:, h * A_DIM:(h + 1) * A_DIM] = ya[h * BLOCK:(h + 1) * BLOCK]

        left = lax.broadcasted_iota(jnp.int32, (1, LANES), 1) < HEAD_DIM
        kv = jnp.concatenate([zp_ref[...], z_ref[:, K_OFF:K_OFF + 2 * KV_WIDTH]], axis=0)
        k2, v2 = kv[:, :KV_WIDTH], kv[:, KV_WIDTH:]
        kn2 = k2 * lax.rsqrt(_seg_mean(k2 * k2, e_ref) + EPS) * gk2_ref[...]
        kn2_r, v2_r = pltpu.roll(kn2, HEAD_DIM, 1), pltpu.roll(v2, HEAD_DIM, 1)
        qp = _stack(z_ref, Q_OFF, PAIRS)
        qn = (qp * lax.rsqrt(_seg_mean(qp * qp, e_ref) + EPS) * gq2_ref[...]).astype(BF16)
        for kh in range(KV_HEADS):
            kbd, vbd = _block_diag(kn2, kn2_r, left, kh), _block_diag(v2, v2_r, left, kh)
            st = _dot(kbd, qn[kh * PAIR_ROWS:(kh + 1) * PAIR_ROWS], NT) * (HEAD_DIM ** -0.5) + bias_ref[kh]
            wt = jnp.concatenate([w_e for w_e, _ in _pair_softmax(st, sk_ref, kh)], axis=0).astype(BF16)
            o = _dot(wt, vbd, TN).astype(BF16)
            for p in range(PAIRS // KV_HEADS):
                col = A_WIDTH + (kh * (PAIRS // KV_HEADS) + p) * LANES
                mix_ref[:, col:col + LANES] = o[p * BLOCK:(p + 1) * BLOCK]

    full, z_spec, zp_spec, consts = _mixer_specs(nb, nb - 1)
    return pl.pallas_call(
        body,
        name=name,
        grid=(nb,),
        in_specs=[z_spec, zp_spec, full((A_HEADS, BLOCK, BLOCK))] + consts,
        out_specs=pl.BlockSpec((BLOCK, A_WIDTH + B_WIDTH), lambda n: (n, 0)),
        out_shape=jax.ShapeDtypeStruct((S, A_WIDTH + B_WIDTH), BF16),
        compiler_params=_params(("parallel",)),
    )(z, z, wm, sbp, gvp, gq2, gk2, seg_ones, lane_ones, sinks, biasp)


def _mixer_bwd(z, dmix, wm, wm_t, sbp, gvp, gq2, gk2, seg_ones, lane_ones, sinks, biasp, pair_fold, name, ex=None):
    S = z.shape[0]
    nb = S // BLOCK

    def body(z_ref, zp_ref, dmix_ref, wm_ref, wmt_ref, sbp_ref, gvp_ref, gq2_ref, gk2_ref, e_ref, l_ref, sk_ref,
             bias_ref, fold_ref,
             dz_ref, dzkv_ref, dwm_ref, dsb_ref, dgv_ref, dgq_ref, dgk_ref, dsk_ref, dst_ref,
             carry_ref, tot_ref, sbacc_ref, skacc_ref, gqacc_ref, gkacc_ref):
        n = pl.program_id(0)
        left = lax.broadcasted_iota(jnp.int32, (1, LANES), 1) < HEAD_DIM

        @pl.when(n == 0)
        def _():
            for ref in (dwm_ref, dgv_ref, dst_ref, carry_ref, sbacc_ref, skacc_ref, gqacc_ref, gkacc_ref):
                ref[...] = jnp.zeros(ref.shape, ref.dtype)

        @pl.when(n < nb)
        def _():
            u, du_dz = _gelu_and_grad(_stack(z_ref, 0, A_HEADS))
            v, dv_dz = _gelu_and_grad(_stack(z_ref, A_WIDTH, A_HEADS))
            rv = lax.rsqrt(_lane_mean(v * v, l_ref) + EPS)
            vhat = v * rv
            gvp = gvp_ref[...]
            vn = (vhat * gvp).astype(BF16)
            rows = lambda a, h: a[h * BLOCK:(h + 1) * BLOCK]
            mixed = jnp.concatenate([_dot(wm_ref[h], rows(vn, h)) for h in range(A_HEADS)], axis=0) + sbp_ref[...]
            dya = _stack(dmix_ref, 0, A_HEADS).astype(F32)
            dmx = dya * u
            sbacc_ref[...] += dmx
            dmx_b = dmx.astype(BF16)
            for h in range(A_HEADS):
                dwm_ref[h] += _dot(rows(dmx_b, h), rows(vn, h), NT)
            dvn = jnp.concatenate([_dot(wmt_ref[h], rows(dmx_b, h)) for h in range(A_HEADS)], axis=0)
            dgv_ref[...] += jnp.sum(jnp.reshape(dvn * vhat, (A_HEADS, BLOCK, A_DIM)), axis=1)
            dzu = ((dya * mixed) * du_dz).astype(BF16)
            tv = dvn * gvp
            dzv = ((rv * (tv - vhat * _lane_mean(tv * vhat, l_ref))) * dv_dz).astype(BF16)
            for h in range(A_HEADS):
                dz_ref[:, h * A_DIM:(h + 1) * A_DIM] = rows(dzu, h)
                dz_ref[:, A_WIDTH + h * A_DIM:A_WIDTH + (h + 1) * A_DIM] = rows(dzv, h)

            kv = jnp.concatenate([zp_ref[...], z_ref[:, K_OFF:K_OFF + 2 * KV_WIDTH]], axis=0)
            k2, v2 = kv[:, :KV_WIDTH], kv[:, KV_WIDTH:]
            kn2 = k2 * lax.rsqrt(_seg_mean(k2 * k2, e_ref) + EPS) * gk2_ref[...]
            kn2_r, v2_r = pltpu.roll(kn2, HEAD_DIM, 1), pltpu.roll(v2, HEAD_DIM, 1)
            qp = _stack(z_ref, Q_OFF, PAIRS)
            rq = lax.rsqrt(_seg_mean(qp * qp, e_ref) + EPS)
            qhat = qp * rq
            gq2 = gq2_ref[...]
            qn = (qhat * gq2).astype(BF16)
            dop = _stack(dmix_ref, A_WIDTH, PAIRS)
            dqn_parts = []
            dk2n = jnp.zeros((BAND, KV_WIDTH), F32)
            dv2 = jnp.zeros((BAND, KV_WIDTH), F32)
            for kh in range(KV_HEADS):
                kbd, vbd = _block_diag(kn2, kn2_r, left, kh), _block_diag(v2, v2_r, left, kh)
                qg = qn[kh * PAIR_ROWS:(kh + 1) * PAIR_ROWS]
                dog = dop[kh * PAIR_ROWS:(kh + 1) * PAIR_ROWS]
                st = _dot(kbd, qg, NT) * (HEAD_DIM ** -0.5) + bias_ref[kh]
                halves = _pair_softmax(st, sk_ref, kh)
                dpt = _dot(vbd, dog, NT)
                ds_halves, t_halves = [], []
                for e, (w_e, w_sink) in enumerate(halves):
                    dp_e = dpt[e * BAND:(e + 1) * BAND]
                    delta = jnp.sum(w_e * dp_e, axis=0, keepdims=True)
                    ds_halves.append(w_e * (dp_e - delta))
                    t_halves.append(-(w_sink * delta))
                dst = jnp.concatenate(ds_halves, axis=0)
                dst_ref[kh] += dst
                skacc_ref[2 * kh:2 * kh + 2, :] += jnp.concatenate(t_halves, axis=0)
                ds_b = (dst * (HEAD_DIM ** -0.5)).astype(BF16)
                w_b = jnp.concatenate([w_e for w_e, _ in halves], axis=0).astype(BF16)
                dqn_parts.append(_dot(ds_b, kbd, TN))
                dk2n += _from_block_diag(_dot(ds_b, qg), left, kh)
                dv2 += _from_block_diag(_dot(w_b, dog), left, kh)
            dqn = jnp.concatenate(dqn_parts, axis=0)
            gqacc_ref[...] += jnp.sum(dqn * qhat, axis=0, keepdims=True)
            t = dqn * gq2
            dzq = (rq * (t - qhat * _seg_mean(t * qhat, e_ref))).astype(BF16)
            for p in range(PAIRS):
                dz_ref[:, Q_OFF + p * LANES:Q_OFF + (p + 1) * LANES] = rows(dzq, p)
            tot_ref[0] = carry_ref[0] + dk2n[:BLOCK]
            tot_ref[1] = carry_ref[1] + dv2[:BLOCK]
            carry_ref[0] = dk2n[BLOCK:]
            carry_ref[1] = dv2[BLOCK:]

        @pl.when(n == nb)
        def _():
            tot_ref[...] = carry_ref[...]

        kp = zp_ref[:, :KV_WIDTH]
        rk = lax.rsqrt(_seg_mean(kp * kp, e_ref) + EPS)
        khat = kp * rk
        dkn = tot_ref[0]
        gkacc_ref[...] += jnp.sum(dkn * khat, axis=0, keepdims=True)
        t = dkn * gk2_ref[...]
        dzkv_ref[:, :KV_WIDTH] = (rk * (t - khat * _seg_mean(t * khat, e_ref))).astype(BF16)
        dzkv_ref[:, KV_WIDTH:] = tot_ref[1].astype(BF16)

        @pl.when(n == nb)
        def _():
            dsb_ref[...] = jnp.broadcast_to(jnp.sum(sbacc_ref[...], axis=1, keepdims=True), dsb_ref.shape)
            dsk_ref[...] = lax.dot_general(skacc_ref[...], fold_ref[...], NN, precision=lax.Precision.HIGHEST,
                                           preferred_element_type=F32)
            dgq_ref[...] = gqacc_ref[...] + pltpu.roll(gqacc_ref[...], HEAD_DIM, 1)
            dgk_ref[...] = gkacc_ref[...] + pltpu.roll(gkacc_ref[...], HEAD_DIM, 1)

    last = nb - 1
    full, z_spec, zp_spec, consts = _mixer_specs(nb, last)
    return _pallas(
        body,
        name=name,
        grid=(nb + 1,),
        ex=ex,
        in_specs=[z_spec, zp_spec, pl.BlockSpec((BLOCK, A_WIDTH + B_WIDTH), lambda n: (jnp.minimum(n, last), 0)),
                  full((A_HEADS, BLOCK, BLOCK)), full((A_HEADS, BLOCK, BLOCK))] + consts + [full((PAIR_ROWS, LANES))],
        out_specs=[
            pl.BlockSpec((BLOCK, K_OFF), lambda n: (jnp.minimum(n, last), 0)),
            pl.BlockSpec((BLOCK, 2 * KV_WIDTH), lambda n: (jnp.maximum(n - 1, 0), 0)),
            full((A_HEADS, BLOCK, BLOCK)), full((A_HEADS * BLOCK, A_DIM)), full((A_HEADS, A_DIM)),
            full((1, LANES)), full((1, LANES)), full((SUBLANES, LANES)),
            full((KV_HEADS, PAIR_ROWS, 2 * BAND)),
        ],
        out_shape=[
            jax.ShapeDtypeStruct((S, K_OFF), BF16),
            jax.ShapeDtypeStruct((S, 2 * KV_WIDTH), BF16),
            jax.ShapeDtypeStruct((A_HEADS, BLOCK, BLOCK), F32),
            jax.ShapeDtypeStruct((A_HEADS * BLOCK, A_DIM), F32),
            jax.ShapeDtypeStruct((A_HEADS, A_DIM), F32),
            jax.ShapeDtypeStruct((1, LANES), F32),
            jax.ShapeDtypeStruct((1, LANES), F32),
            jax.ShapeDtypeStruct((SUBLANES, LANES), F32),
            jax.ShapeDtypeStruct((KV_HEADS, PAIR_ROWS, 2 * BAND), F32),
        ],
        scratch_shapes=[
            pltpu.VMEM((2, BLOCK, KV_WIDTH), F32),
            pltpu.VMEM((2, BLOCK, KV_WIDTH), F32),
            pltpu.VMEM((A_HEADS * BLOCK, A_DIM), F32),
            pltpu.VMEM((SUBLANES, PAIR_ROWS), F32),
            pltpu.VMEM((1, LANES), F32),
            pltpu.VMEM((1, LANES), F32),
        ],
        semantics=("arbitrary",),
        args=[z, z, dmix, wm, wm_t, sbp, gvp, gq2, gk2, seg_ones, lane_ones, sinks, biasp, pair_fold],
    )


def _mixer_out(mix, w_out, x, gt, name):
    S, D = x.shape
    K = mix.shape[1]
    tm, tn = _tile(S, 1024), _tile(D, 1024)

    def body(m_ref, w_ref, x_ref, gt_ref, xo_ref, y_ref):
        y = _dot(m_ref[...], w_ref[...])
        y_ref[...] = y.astype(BF16)
        xo_ref[...] = x_ref[...] + gt_ref[...] * y

    blk = pl.BlockSpec((tm, tn), lambda j, i: (i, j))
    return pl.pallas_call(
        body,
        name=name,
        grid=(D // tn, S // tm),
        in_specs=[pl.BlockSpec((tm, K), lambda j, i: (i, 0)), pl.BlockSpec((K, tn), lambda j, i: (0, j)),
                  blk, pl.BlockSpec((1, tn), lambda j, i: (0, j))],
        out_specs=[blk, blk],
        out_shape=[jax.ShapeDtypeStruct((S, D), F32), jax.ShapeDtypeStruct((S, D), BF16)],
        compiler_params=_params(("parallel", "parallel")),
    )(mix, w_out, x, gt)


def _bucket_sum(dst, onehot, name):
    def body(d_ref, o_ref, out_ref):
        out_ref[...] = lax.dot_general(o_ref[...], d_ref[...], NT, precision=lax.Precision.HIGHEST,
                                       preferred_element_type=F32)

    return pl.pallas_call(
        body,
        name=name,
        out_shape=jax.ShapeDtypeStruct((N_BUCKETS, B_HEADS), F32),
    )(dst, onehot)


def _adamw_math(w, g, m, v):
    m = ADAM_B1 * m + (1.0 - ADAM_B1) * g
    v = ADAM_B2 * v + (1.0 - ADAM_B2) * (g * g)
    m_hat = m / (1.0 - ADAM_B1 ** ADAM_STEP)
    v_hat = v / (1.0 - ADAM_B2 ** ADAM_STEP)
    delta = -ADAM_LR * (m_hat / (jnp.sqrt(v_hat) + ADAM_EPS) + ADAM_WD * w)
    return delta, m, v


def _adamw(w, g, m, v, name, emit_grad=False, after=()):
    R, C = w.shape
    tr = _tile(R, max(SUBLANES, (1 << 19) // C), SUBLANES)

    def body(w_ref, g_ref, m_ref, v_ref, *out_refs):
        gv = g_ref[...]
        results = _adamw_math(w_ref[...], gv, m_ref[...], v_ref[...])
        for ref, val in zip(out_refs, ((gv,) if emit_grad else ()) + results):
            ref[...] = val

    blk = pl.BlockSpec((tr, C), lambda i: (i, 0))
    shape = jax.ShapeDtypeStruct((R, C), F32)
    n_out = 4 if emit_grad else 3
    return _pallas(body, name=name, grid=(R // tr,), in_specs=[blk] * 4, out_specs=[blk] * n_out,
                   out_shape=[shape] * n_out, semantics=("parallel",), args=[w, g, m, v], after=after)


def _small_update(parts, w, m, v, name):
    R = w.shape[0]

    def body(p_ref, w_ref, m_ref, v_ref, g_ref, d_ref, mo_ref, vo_ref):
        g = p_ref[0]
        for dev in range(1, N_DEV):
            g = g + p_ref[dev]
        g_ref[...] = g
        d, mn, vn = _adamw_math(w_ref[...], g, m_ref[...], v_ref[...])
        d_ref[...] = d
        mo_ref[...] = mn
        vo_ref[...] = vn

    shape = jax.ShapeDtypeStruct((R, LANES), F32)
    return pl.pallas_call(
        body,
        name=name,
        out_shape=[shape] * 4,
        compiler_params=pltpu.CompilerParams(vmem_limit_bytes=VMEM_LIMIT),
    )(parts, w, m, v)


def _place():
    x, y, c = lax.axis_index("x"), lax.axis_index("y"), lax.axis_index("c")
    chips = [(1 - x, y), (x, 1 - y), (1 - x, 1 - y)]
    return x, y, c, chips


def _remote(src, dst, send_sem, recv_sem, to):
    return pltpu.make_async_remote_copy(src_ref=src, dst_ref=dst, send_sem=send_sem, recv_sem=recv_sem,
                                        device_id=to, device_id_type=MESH)


def _allgather_small(block, name):
    m_per, n = block.shape

    def body(x_ref, out_ref, send_sems, recv_sems, local_sem):
        x, y, c, chips = _place()
        me, sibling = (x, y, c), (x, y, 1 - c)

        def rows(px, py, pc):
            return out_ref.at[pl.ds((4 * px + 2 * py + pc) * m_per, m_per), :]

        def copy(k, blk, to, src=None):
            return _remote(rows(*blk) if src is None else src, rows(*blk), send_sems.at[k], recv_sems.at[k], to)

        mine = pltpu.make_async_copy(x_ref, rows(*me), local_sem)
        mine.start()
        first = [copy(0, me, sibling, src=x_ref)]
        first += [copy(1 + j, me, (*chip, c), src=x_ref) for j, chip in enumerate(chips)]
        for cp in first:
            cp.start()
        passed = [copy(4 + j, (*chip, c), sibling) for j, chip in enumerate(chips)]
        for j, chip in enumerate(chips):
            copy(1 + j, (*chip, c), me).wait_recv()
            passed[j].start()
        copy(0, sibling, me).wait_recv()
        for j, chip in enumerate(chips):
            copy(4 + j, (*chip, 1 - c), me).wait_recv()
        for cp in first + passed:
            cp.wait_send()
        mine.wait()

    return pl.pallas_call(
        body,
        name=name,
        out_shape=jax.ShapeDtypeStruct((N_DEV * m_per, n), block.dtype),
        in_specs=[pl.BlockSpec(memory_space=pltpu.VMEM)],
        out_specs=pl.BlockSpec(memory_space=pltpu.VMEM),
        scratch_shapes=[pltpu.SemaphoreType.DMA((7,)), pltpu.SemaphoreType.DMA((7,)), pltpu.SemaphoreType.DMA],
        compiler_params=pltpu.CompilerParams(vmem_limit_bytes=VMEM_LIMIT),
    )(block)


def _half(ref, c, rows):
    start = pl.multiple_of(c * rows, BF16_ROWS)
    if len(ref.shape) == 2:
        return ref.at[pl.ds(start, rows), :]
    return ref.at[:, pl.ds(start, rows), :]


def _slot(ref, index):
    if len(ref.shape) == 3:
        return ref.at[index]
    width = ref.shape[1] // N_CHIPS
    return ref.at[:, pl.ds(pl.multiple_of(index * width, LANES), width)]


def _shard_rows(buf):
    return buf.shape[1] if len(buf.shape) == 3 else buf.shape[0]


def _same(arrays):
    return [jax.ShapeDtypeStruct(a.shape, a.dtype) for a in arrays], {t: t for t in range(len(arrays))}


def _ex_gather_ici(bufs):
    def plan(ins, outs, send_sems, recv_sems):
        x, y, c, chips = _place()
        sends, arrivals = [], []
        for t, buf in enumerate(bufs):
            rows = _shard_rows(buf) // 2
            mine = _half(_slot(outs[t], 2 * x + y), c, rows)
            for k, (px, py) in enumerate(chips):
                sems = (send_sems.at[3 * t + k], recv_sems.at[3 * t + k], (px, py, c))
                landed = _half(_slot(outs[t], 2 * px + py), c, rows)
                sends.append((mine, mine, *sems))
                arrivals.append((landed, landed, *sems))
        return sends, arrivals

    shapes, aliases = _same(bufs)
    return _Exchange(bufs, shapes, aliases, 3 * len(bufs), plan)


def _ex_gather_near(bufs):
    def plan(ins, outs, send_sems, recv_sems):
        x, y, c, chips = _place()
        sends, arrivals = [], []
        for t, buf in enumerate(bufs):
            rows = _shard_rows(buf) // 2
            mine = _half(_slot(outs[t], 2 * x + y), c, rows)
            for k, (px, py) in enumerate(chips[:2]):
                sems = (send_sems.at[2 * t + k], recv_sems.at[2 * t + k], (px, py, c))
                landed = _half(_slot(outs[t], 2 * px + py), c, rows)
                sends.append((mine, mine, *sems))
                arrivals.append((landed, landed, *sems))
        return sends, arrivals

    shapes, aliases = _same(bufs)
    return _Exchange(bufs, shapes, aliases, 2 * len(bufs), plan)


def _ex_gather_relay(bufs):
    def plan(ins, outs, send_sems, recv_sems):
        x, y, c, chips = _place()
        (xn, yn, diag) = chips
        slot = lambda chip: 2 * chip[0] + chip[1]
        sends, arrivals = [], []
        for t, buf in enumerate(bufs):
            quarter = _shard_rows(buf) // 4

            def piece(chip, q):
                start = pl.multiple_of(c * 2 * quarter + q * quarter, BF16_ROWS)
                return _slot(outs[t], slot(chip)).at[pl.ds(start, quarter), :]

            for k, (held, to) in enumerate([(xn, yn), (yn, xn)]):
                sems = (send_sems.at[2 * t + k], recv_sems.at[2 * t + k], (*to, c))
                sends.append((piece(held, k), piece(held, k), *sems))
                arrivals.append((piece(diag, k), piece(diag, k), *sems))
        onward = _ex_gather_d2d(bufs, which=(0, 1))
        more_sends, more_arrivals = onward.plan(ins, outs, _Shifted(send_sems, 2 * len(bufs)),
                                                _Shifted(recv_sems, 2 * len(bufs)))
        return sends + more_sends, arrivals + more_arrivals

    shapes, aliases = _same(bufs)
    return _Exchange(bufs, shapes, aliases, 2 * len(bufs) + 3 * len(bufs), plan)


def _ex_gather_d2d(bufs, which=(0, 1, 2)):
    def plan(ins, outs, send_sems, recv_sems):
        x, y, c, chips = _place()
        sends, arrivals = [], []
        for t, buf in enumerate(bufs):
            rows = _shard_rows(buf) // 2
            for k in which:
                px, py = chips[k]
                sems = (send_sems.at[3 * t + k], recv_sems.at[3 * t + k], (x, y, 1 - c))
                landed = _half(_slot(outs[t], 2 * px + py), c, rows)
                other = _half(_slot(outs[t], 2 * px + py), 1 - c, rows)
                sends.append((landed, landed, *sems))
                arrivals.append((other, other, *sems))
        return sends, arrivals

    shapes, aliases = _same(bufs)
    return _Exchange(bufs, shapes, aliases, 3 * len(bufs), plan)


def _ex_swap_halves(grads):
    def plan(ins, outs, send_sems, recv_sems):
        x, y, c, _ = _place()
        sends = [(_half(ins[t], 1 - c, g.shape[1] // 2), outs[t], send_sems.at[t], recv_sems.at[t], (x, y, 1 - c))
                 for t, g in enumerate(grads)]
        return sends, sends

    shapes = [jax.ShapeDtypeStruct((g.shape[0], g.shape[1] // 2, g.shape[2]), g.dtype) for g in grads]
    return _Exchange(grads, shapes, {}, len(grads), plan)


def _ex_scatter(sums):
    def plan(ins, outs, send_sems, recv_sems):
        x, y, c, chips = _place()
        sends = [(ins[t].at[2 * px + py], outs[t].at[k], send_sems.at[3 * t + k], recv_sems.at[3 * t + k], (px, py, c))
                 for t in range(len(sums)) for k, (px, py) in enumerate(chips)]
        return sends, sends

    shapes = [jax.ShapeDtypeStruct((N_CHIPS - 1,) + s.shape[1:], s.dtype) for s in sums]
    return _Exchange(sums, shapes, {}, 3 * len(sums), plan)


def _ex_join_halves(fulls):
    def plan(ins, outs, send_sems, recv_sems):
        x, y, c, _ = _place()
        sends, arrivals = [], []
        for t, full in enumerate(fulls):
            rows = full.shape[0] // 2
            sems = (send_sems.at[t], recv_sems.at[t], (x, y, 1 - c))
            mine, other = _half(outs[t], c, rows), _half(outs[t], 1 - c, rows)
            sends.append((mine, mine, *sems))
            arrivals.append((other, other, *sems))
        return sends, arrivals

    shapes, aliases = _same(fulls)
    return _Exchange(fulls, shapes, aliases, len(fulls), plan)


class _Shifted:
    def __init__(self, sems, offset):
        self.sems, self.offset = sems, offset

    @property
    def at(self):
        return self

    def __getitem__(self, k):
        return self.sems.at[self.offset + k]


def _combine(exchanges):
    operands, out_shapes, aliases, starts = [], [], {}, []
    n_sems = 0
    for e in exchanges:
        starts.append((len(operands), len(out_shapes), n_sems))
        aliases.update({len(operands) + i: len(out_shapes) + o for i, o in e.aliases.items()})
        operands += list(e.operands)
        out_shapes += list(e.out_shapes)
        n_sems += e.n_sems

    def plan(ins, outs, send_sems, recv_sems):
        sends, arrivals = [], []
        for e, (i0, o0, s0) in zip(exchanges, starts):
            s, a = e.plan(ins[i0:i0 + len(e.operands)], outs[o0:o0 + len(e.out_shapes)],
                          _Shifted(send_sems, s0), _Shifted(recv_sems, s0))
            sends += s
            arrivals += a
        return sends, arrivals

    return _Exchange(operands, out_shapes, aliases, n_sems, plan)


class _Reduction:
    def __init__(self, grad, tag, c_arr, jc_arr):
        self.grad, self.tag, self.c_arr, self.jc_arr, self.stage = grad, tag, c_arr, jc_arr, 0

    def exchange(self):
        if self.stage == 0:
            return _ex_swap_halves([self.grad])
        if self.stage == 1:
            return _ex_scatter([self.sums])
        return _ex_join_halves([self.full])

    def advance(self, landed):
        if self.stage == 0:
            self.recv = landed
            self.sums = _chip_sum(self.grad, landed, self.c_arr, f"chip_sum_{self.tag}")
        elif self.stage == 1:
            self.full = _owner_sum(self.grad, self.recv, landed, self.jc_arr, f"owner_sum_{self.tag}")
        else:
            self.result = landed
        self.stage += 1


def _ride(reductions):
    def done(carried):
        for r, landed in zip(reductions, carried):
            r.advance(landed)

    return _combine([r.exchange() for r in reductions]), done


def _exchange_alone(ex, name):
    return _pallas(None, name=name, grid=(), in_specs=[], out_specs=[], out_shape=[], args=[], ex=ex)


SEM = pl.BlockSpec(memory_space=pltpu.SEMAPHORE)
DATAFLOW = pltpu.SideEffectType.DATAFLOW_SIDE_EFFECTING


def _exchange_start(ex, name):
    e_in, e_out = len(ex.operands), len(ex.out_shapes)
    kept = [i for i in range(e_in) if i not in ex.aliases]

    def body(*refs):
        ins, refs = refs[:e_in], refs[e_in:]
        outs, refs = refs[:e_out], refs[e_out:]
        _, (send_sems, recv_sems, token) = refs[:len(kept)], refs[len(kept):]
        sends, _ = ex.plan(ins, outs, send_sems, recv_sems)
        for cp in sends:
            _remote(*cp).start()
        token[...] = jnp.zeros(token.shape, F32)

    sems = pltpu.SemaphoreType.DMA((ex.n_sems,))
    aliases = dict(ex.aliases)
    aliases.update({i: e_out + k for k, i in enumerate(kept)})
    res = pl.pallas_call(
        body,
        name=name,
        in_specs=[ANY] * e_in,
        out_specs=[ANY] * (e_out + len(kept)) + [SEM, SEM, pl.BlockSpec(memory_space=pltpu.VMEM)],
        out_shape=list(ex.out_shapes) + [jax.ShapeDtypeStruct(ex.operands[i].shape, ex.operands[i].dtype) for i in kept]
        + [sems, sems, jax.ShapeDtypeStruct((SUBLANES, LANES), F32)],
        input_output_aliases=aliases,
        compiler_params=pltpu.CompilerParams(has_side_effects=DATAFLOW),
    )(*ex.operands)
    outs, kept_thru, (send_sems, recv_sems, token) = res[:e_out], res[e_out:e_out + len(kept)], res[e_out + len(kept):]
    operands = list(ex.operands)
    for i, o in ex.aliases.items():
        operands[i] = outs[o]
    for k, i in enumerate(kept):
        operands[i] = kept_thru[k]
    return (operands, outs, send_sems, recv_sems), token


def _exchange_wait(ex, state, after, name):
    operands, outs, send_sems, recv_sems = state
    e_out = len(outs)
    kept = [i for i in range(len(operands)) if i not in ex.aliases]

    def body(*refs):
        sources, refs = refs[:len(kept)], refs[len(kept):]
        landing, refs = refs[:e_out], refs[e_out:]
        ins = [None] * len(operands)
        for k, i in enumerate(kept):
            ins[i] = sources[k]
        sends, arrivals = ex.plan(ins, landing, refs[0], refs[1])
        for cp in arrivals:
            _remote(*cp).wait_recv()
        for cp in sends:
            _remote(*cp).wait_send()

    return pl.pallas_call(
        body,
        name=name,
        in_specs=[ANY] * (len(kept) + e_out) + [SEM, SEM] + [ANY] * len(after),
        out_specs=[ANY] * e_out,
        out_shape=[jax.ShapeDtypeStruct(o.shape, o.dtype) for o in outs],
        input_output_aliases={len(kept) + o: o for o in range(e_out)},
        compiler_params=pltpu.CompilerParams(has_side_effects=DATAFLOW),
    )(*[operands[i] for i in kept], *outs, send_sems, recv_sems, *after)


def _cast_to_slot(w, chip_arr, name, after=(), natural=False):
    A, B = w.shape
    ta = _tile(A, max(BF16_ROWS, (1 << 19) // B), BF16_ROWS)

    def body(j_ref, w_ref, *rest):
        rest[-1][...] = w_ref[...].astype(BF16)

    if natural:
        assert B % LANES == 0
        out_spec = pl.BlockSpec((ta, B), lambda i, j_ref: (i, j_ref[0]))
        out_shape = jax.ShapeDtypeStruct((A, N_CHIPS * B), BF16)
    else:
        out_spec = pl.BlockSpec((None, ta, B), lambda i, j_ref: (j_ref[0], i, 0))
        out_shape = jax.ShapeDtypeStruct((N_CHIPS, A, B), BF16)
    return pl.pallas_call(
        body,
        name=name,
        grid_spec=pltpu.PrefetchScalarGridSpec(
            num_scalar_prefetch=1,
            grid=(A // ta,),
            in_specs=[pl.BlockSpec((ta, B), lambda i, j_ref: (i, 0))] + [ANY] * len(after),
            out_specs=out_spec,
        ),
        out_shape=out_shape,
        compiler_params=_params(("parallel",)),
    )(chip_arr, w, *after)


def _chip_sum(grad, recv, c_arr, name):
    _, A, B = grad.shape
    hA = A // 2
    ta = _tile(hA, max(BF16_ROWS, (1 << 19) // B), BF16_ROWS)
    nh = hA // ta

    def body(c_ref, g_ref, r_ref, o_ref):
        o_ref[...] = (g_ref[...] + r_ref[...]).astype(BF16)

    return pl.pallas_call(
        body,
        name=name,
        grid_spec=pltpu.PrefetchScalarGridSpec(
            num_scalar_prefetch=1,
            grid=(N_CHIPS, nh),
            in_specs=[pl.BlockSpec((None, ta, B), lambda s, i, c_ref: (s, c_ref[0] * nh + i, 0)),
                      pl.BlockSpec((None, ta, B), lambda s, i, c_ref: (s, i, 0))],
            out_specs=pl.BlockSpec((None, ta, B), lambda s, i, c_ref: (s, i, 0)),
        ),
        out_shape=jax.ShapeDtypeStruct((N_CHIPS, hA, B), BF16),
        compiler_params=_params(("parallel", "parallel")),
    )(c_arr, grad, recv)


def _owner_sum(grad, recv, landed, jc_arr, name):
    _, A, B = grad.shape
    hA = A // 2
    ta = _tile(hA, max(BF16_ROWS, (1 << 19) // B), BF16_ROWS)
    nh = hA // ta

    def body(jc_ref, g_ref, r_ref, l0_ref, l1_ref, l2_ref, o_ref):
        total = g_ref[...] + r_ref[...]
        for ref in (l0_ref, l1_ref, l2_ref):
            total = total + ref[...].astype(F32)
        o_ref[...] = total

    def landed_spec(k):
        return pl.BlockSpec((None, ta, B), lambda i, jc_ref: (k, i, 0))

    return pl.pallas_call(
        body,
        name=name,
        grid_spec=pltpu.PrefetchScalarGridSpec(
            num_scalar_prefetch=1,
            grid=(nh,),
            in_specs=[pl.BlockSpec((None, ta, B), lambda i, jc_ref: (jc_ref[0], jc_ref[1] * nh + i, 0)),
                      pl.BlockSpec((None, ta, B), lambda i, jc_ref: (jc_ref[0], i, 0)),
                      landed_spec(0), landed_spec(1), landed_spec(2)],
            out_specs=pl.BlockSpec((ta, B), lambda i, jc_ref: (jc_ref[1] * nh + i, 0)),
        ),
        out_shape=jax.ShapeDtypeStruct((A, B), F32),
        compiler_params=_params(("parallel",)),
    )(jc_arr, grad, recv, landed, landed, landed)


def _pack(parts):
    rows = []
    for p in parts:
        flat = jnp.reshape(p.astype(F32), (-1,))
        tile = SUBLANES * LANES
        padded = -(-flat.shape[0] // tile) * tile
        rows.append(jnp.reshape(jnp.pad(flat, (0, padded - flat.shape[0])), (-1, LANES)))
    return jnp.concatenate(rows, axis=0)


def _unpack(pack, shapes):
    out, row = [], 0
    for shape in shapes:
        size = int(np.prod(shape))
        nrows = -(-size // (SUBLANES * LANES)) * SUBLANES
        out.append(jnp.reshape(jnp.reshape(pack[row:row + nrows], (-1,))[:size], shape))
        row += nrows
    return out


def _bias_tables():
    qi = np.arange(BLOCK)[:, None]
    kj = np.arange(2 * BLOCK)[None, :]
    dist = qi + BLOCK - kj
    in_window = (dist >= 0) & (dist < BLOCK)
    n = np.clip(dist, 0, None)
    max_exact = N_BUCKETS // 2
    nf = np.maximum(n, 1).astype(np.float32)
    large = max_exact + (np.log(nf / max_exact) / math.log(MAX_DISTANCE / max_exact)
                         * (N_BUCKETS - max_exact)).astype(np.int32)
    large = np.minimum(large, N_BUCKETS - 1)
    bucket = np.where(n < max_exact, n, large)
    onehot = (bucket[None] == np.arange(N_BUCKETS)[:, None, None]) & in_window[None]
    first = in_window & (kj >= BLOCK)
    return onehot.astype(np.float32), in_window, first


def kernel(x, c, w_ada, b_ada, g_ffn1, w1_ffn1, w3_ffn1, w2_ffn1, g_mix, w_in, spatial_w, spatial_b, g_v, g_q, g_k, sinks, rel_bias, w_out, g_ffn2, w1_ffn2, w3_ffn2, w2_ffn2, loss_target, m_w_ada, m_b_ada, m_g_ffn1, m_w1_ffn1, m_w3_ffn1, m_w2_ffn1, m_g_mix, m_w_in, m_spatial_w, m_spatial_b, m_g_v, m_g_q, m_g_k, m_sinks, m_rel_bias, m_w_out, m_g_ffn2, m_w1_ffn2, m_w3_ffn2, m_w2_ffn2, v_w_ada, v_b_ada, v_g_ffn1, v_w1_ffn1, v_w3_ffn1, v_w2_ffn1, v_g_mix, v_w_in, v_spatial_w, v_spatial_b, v_g_v, v_g_q, v_g_k, v_sinks, v_rel_bias, v_w_out, v_g_ffn2, v_w1_ffn2, v_w3_ffn2, v_w2_ffn2):
    ax, ay, ac = lax.axis_index("x"), lax.axis_index("y"), lax.axis_index("c")
    chip = 2 * ax + ay
    dev = 2 * chip + ac
    xs = x[0]
    tgt = loss_target[0]
    S, D = xs.shape
    F = N_CHIPS * w1_ffn1.shape[2]
    mod_cols = w_ada.shape[2]

    chip_arr = jnp.reshape(chip, (1,)).astype(jnp.int32)
    c_arr = jnp.reshape(ac, (1,)).astype(jnp.int32)
    jc_arr = jnp.stack([chip, ac]).astype(jnp.int32)

    c_all = _allgather_small(jnp.pad(c, ((0, SUBLANES - 1), (0, 0))), "gather_c")
    c_all = jnp.pad(c_all[::SUBLANES], ((0, BF16_ROWS - N_DEV), (0, 0)))
    b_sh = lax.dynamic_slice(b_ada, (0, chip * mod_cols), (1, mod_cols))
    mod_part, c_act = _mod_partial(c_all, w_ada[0], b_sh, "mod_partial")
    mod_all = _allgather_small(mod_part[:N_DEV], "gather_mod")
    mod_all = jnp.reshape(mod_all, (N_CHIPS, 2, N_DEV, mod_cols))[:, 0]
    mod = jnp.reshape(lax.dynamic_index_in_dim(mod_all, dev, axis=1, keepdims=False), (1, N_MOD * D))
    sh1, sc1, gt1, sh2, sc2, gt2, sh3, sc3, gt3 = [mod[:, i * D:(i + 1) * D] for i in range(N_MOD)]

    def cols_to_natural(w4):
        return jnp.reshape(jnp.transpose(w4, (1, 0, 2)), (w4.shape[1], -1))

    def cast(w, nm, after):
        return _cast_to_slot(w[0], chip_arr, f"cast_{nm}", after=after, natural=nm.startswith(("w1_", "w3_")))

    ffn1_bufs = [cast(w1_ffn1, "w1_ffn1", [mod]), cast(w3_ffn1, "w3_ffn1", [mod]), cast(w2_ffn1, "w2_ffn1", [mod])]
    near = _ex_gather_near(ffn1_bufs)
    state, started = _exchange_start(near, "gather_ffn1_near_start")
    mixer_bufs = [cast(w_in, "w_in", [started]), cast(w_out, "w_out", [started])]
    ffn2_bufs = [cast(w1_ffn2, "w1_ffn2", [started]), cast(w3_ffn2, "w3_ffn2", [started]),
                 cast(w2_ffn2, "w2_ffn2", [started])]
    ffn1_bufs = _exchange_wait(near, state, mixer_bufs + ffn2_bufs, "gather_ffn1_near_wait")
    relay = _ex_gather_relay(ffn1_bufs)
    state, started = _exchange_start(relay, "gather_ffn1_relay_start")
    h1 = _norm_mod(xs, g_ffn1, sh1, sc1, "ffn1_norm", after=[started])
    ffn1_bufs = _exchange_wait(relay, state, [h1], "gather_ffn1_relay_wait")
    ffn1_bufs = _exchange_alone(_ex_gather_d2d(ffn1_bufs, which=(2,)), "gather_ffn1_d2d")
    w1a, w3a, w2a = ffn1_bufs[0], ffn1_bufs[1], jnp.reshape(ffn1_bufs[2], (F, D))

    onehot_np, in_window_np, first_np = _bias_tables()
    onehot = jnp.asarray(onehot_np)
    bias = jnp.einsum("bij,bh->hij", onehot, rel_bias, precision=lax.Precision.HIGHEST)
    biasm = jnp.stack([jnp.where(jnp.asarray(first_np)[None], bias, NEG),
                       jnp.where(jnp.asarray(in_window_np)[None], bias, NEG)])
    causal = jnp.asarray(np.tril(np.ones((BLOCK, BLOCK), dtype=bool)))
    wm = jnp.where(causal[None], spatial_w[0], 0.0).astype(BF16)
    wm_t = jnp.transpose(wm, (0, 2, 1))
    sink_vec = sinks[0]
    per_group = PAIRS // KV_HEADS
    sbp = jnp.broadcast_to(jnp.reshape(spatial_b[0], (A_HEADS * BLOCK, 1)), (A_HEADS * BLOCK, A_DIM))
    gvp = jnp.repeat(g_v[0], BLOCK, axis=0)
    gq2, gk2 = jnp.concatenate([g_q, g_q], axis=1), jnp.concatenate([g_k, g_k], axis=1)
    seg_ones = jnp.asarray(np.kron(np.eye(2, dtype=np.float32), np.ones((HEAD_DIM, HEAD_DIM), np.float32)), BF16)
    lane_ones = jnp.full((LANES, LANES), 1.0 / LANES, BF16)
    pair_fold = jnp.asarray(np.kron(np.eye(per_group, LANES, dtype=np.float32), np.ones((BLOCK, 1), np.float32)))
    biasp = jnp.reshape(jnp.transpose(jnp.reshape(biasm, (2, KV_HEADS, per_group, 2, BLOCK, BAND)), (0, 1, 3, 5, 2, 4)),
                        (2, KV_HEADS, 2 * BAND, PAIR_ROWS))

    res = _ffn_fwd(xs, g_ffn1, sh1, sc1, gt1, w1a, w3a, w2a, None, "ffn1_fwd", ex=_ex_gather_ici(mixer_bufs + ffn2_bufs),
                   h=h1)
    (x1, a1, b1, f1), mixer_bufs, ffn2_bufs = res[:4], res[4:6], res[6:]
    h2, *mixer_bufs = _norm_mod(x1, g_mix, sh2, sc2, "mixer_norm", ex=_ex_gather_d2d(mixer_bufs))
    win, wout = cols_to_natural(mixer_bufs[0]), jnp.reshape(mixer_bufs[1], (-1, D))
    z, *ffn2_bufs = _matmul(h2, win, "nn", F32, 1024, _tile(IN_COLS, 1664), D, "mixer_in", ex=_ex_gather_d2d(ffn2_bufs))
    w1b, w3b, w2b = ffn2_bufs[0], ffn2_bufs[1], jnp.reshape(ffn2_bufs[2], (F, D))
    mix = _mixer_fwd(z, wm, sbp, gvp, gq2, gk2, seg_ones, lane_ones, sink_vec, biasp, "mixer_fwd")
    x2, ymix = _mixer_out(mix, wout, x1, gt2, "mixer_out")
    g3, df3, h3, a3, b3, dgt3, loss_sum = _ffn_fwd(x2, g_ffn2, sh3, sc3, gt3, w1b, w3b, w2b, tgt, "ffn2_fwd_loss")
    loss = lax.psum(loss_sum[0, 0] * (0.5 / D), ("x", "y", "c"))

    tk = _tile(S, 2048)

    def ffn_weight_grads(h, da, db, s, df, tag, riding):
        ex, done = _ride(riding) if riding else (None, None)
        gw1 = _matmul(h, da, "tn", F32, 1024, F // N_CHIPS, tk, f"grad_w1_{tag}", shard_major=True, ex=ex)
        if riding:
            done(gw1[1:])
            gw1 = gw1[0]
        r1 = _Reduction(gw1, f"w1_{tag}", c_arr, jc_arr)
        ex, done = _ride([r1])
        gw3, *carried = _matmul(h, db, "tn", F32, 1024, F // N_CHIPS, tk, f"grad_w3_{tag}", shard_major=True, ex=ex)
        done(carried)
        r3 = _Reduction(gw3, f"w3_{tag}", c_arr, jc_arr)
        ex, done = _ride([r1, r3])
        gw2, *carried = _matmul(s, df, "tn", F32, _tile(F, 1408), 1024, tk, f"grad_w2_{tag}", ex=ex)
        done(carried)
        r2 = _Reduction(jnp.reshape(gw2, (N_CHIPS, F // N_CHIPS, D)), f"w2_{tag}", c_arr, jc_arr)
        return r1, r3, r2

    da3, db3, s3, dh3 = _ffn_bwd(df3, a3, b3, w1b, w3b, w2b, "ffn2_bwd")
    r21, r23, r22 = ffn_weight_grads(h3, da3, db3, s3, df3, "ffn2", [])
    ex, done = _ride([r21, r22])
    state, started = _exchange_start(ex, "reduce_ffn2_start")
    g2, dsh3, dsc3, dgn3, dy, dgt2 = _norm_bwd(dh3, x2, g3, g_ffn2, sc3, (ymix, gt2, 1.0), "ffn2_norm_bwd",
                                               after=[started])
    dmix = _matmul(dy, wout, "nt", BF16, 1024, 2048, D, "mixer_out_bwd")
    done(_exchange_wait(ex, state, [dmix], "reduce_ffn2_wait"))
    ex, done = _ride([r23, r22])
    res = _mixer_bwd(z, dmix, wm, wm_t, sbp, gvp, gq2, gk2, seg_ones, lane_ones, sink_vec, biasp, pair_fold,
                     "mixer_bwd", ex=ex)
    dz_main, dz_kv, dwm, dsb, dgv, dgq, dgk, dsk, dst = res[:9]
    dsb = jnp.reshape(dsb[:, 0], (A_HEADS, BLOCK))
    dgq, dgk = dgq[:, :HEAD_DIM], dgk[:, :HEAD_DIM]
    dsk = jnp.reshape(jnp.transpose(jnp.reshape(dsk[:2 * KV_HEADS, :per_group], (KV_HEADS, 2, per_group)), (0, 2, 1)),
                      (1, B_HEADS))
    dst = jnp.reshape(jnp.transpose(jnp.reshape(dst, (KV_HEADS, 2, BAND, per_group, BLOCK)), (0, 3, 1, 4, 2)),
                      (B_HEADS, BLOCK * BAND))
    done(res[9:])
    dz = jnp.concatenate([dz_main, dz_kv], axis=1)
    ex, done = _ride([r23, r22])
    dh2, *carried = _matmul(dz, win, "nt", BF16, 1024, 2048, _tile(IN_COLS, 1664), "mixer_in_bwd", ex=ex)
    done(carried)
    drel = _bucket_sum(dst, jnp.reshape(onehot, (N_BUCKETS, -1)), "bucket_sum")
    g1, dsh2, dsc2, dgn2, df1, dgt1 = _norm_bwd(dh2, x1, g2, g_mix, sc2, (f1, gt1, 0.5), "mixer_norm_bwd")

    da1, db1, s1, dh1 = _ffn_bwd(df1, a1, b1, w1a, w3a, w2a, "ffn1_bwd")
    r11, r13, r12 = ffn_weight_grads(h1, da1, db1, s1, df1, "ffn1", [])
    ex, done = _ride([r11, r13, r12])
    gwin_full, *carried = _matmul(h2, dz, "tn", F32, 1024, _tile(IN_COLS, 1664), tk, "grad_w_in", ex=ex)
    done(carried)
    rm_in = _Reduction(jnp.transpose(jnp.reshape(gwin_full, (D, N_CHIPS, -1)), (1, 0, 2)), "w_in", c_arr, jc_arr)
    ex, done = _ride([r13, r12, rm_in])
    state, started = _exchange_start(ex, "reduce_late_start")
    gwout_full = _matmul(mix, dy, "tn", F32, 1024, 1024, tk, "grad_w_out", after=[started])
    grad_x, dsh1, dsc1, dgn1 = _norm_bwd(dh1, xs, g1, g_ffn1, sc1, None, "ffn1_norm_bwd", after=[started])
    done(_exchange_wait(ex, state, [gwout_full, grad_x], "reduce_late_wait"))
    rm_out = _Reduction(jnp.reshape(gwout_full, (N_CHIPS, -1, D)), "w_out", c_arr, jc_arr)

    dmod = jnp.concatenate([dsh1, dsc1, dgt1, dsh2, dsc2, dgt2, dsh3, dsc3, dgt3], axis=1)
    small_w = [b_ada, g_ffn1, g_mix, g_ffn2, spatial_w, spatial_b, g_v, g_q, g_k, sinks, rel_bias]
    small_m = [m_b_ada, m_g_ffn1, m_g_mix, m_g_ffn2, m_spatial_w, m_spatial_b, m_g_v, m_g_q, m_g_k, m_sinks, m_rel_bias]
    small_v = [v_b_ada, v_g_ffn1, v_g_mix, v_g_ffn2, v_spatial_w, v_spatial_b, v_g_v, v_g_q, v_g_k, v_sinks, v_rel_bias]
    small_g = [dmod, dgn1, dgn2, dgn3, jnp.where(causal[None], dwm, 0.0), dsb, dgv, dgq, dgk, dsk, drel]
    shapes = [w.shape for w in small_w]
    gpack = _pack(small_g)
    rows = gpack.shape[0]
    gall = jnp.reshape(_allgather_small(gpack, "gather_small"), (N_DEV, rows, LANES))
    sg, sd, sm, sv = _small_update(gall, _pack(small_w), _pack(small_m), _pack(small_v), "small_update")
    sg, sd, sm, sv = [_unpack(p, shapes) for p in (sg, sd, sm, sv)]

    mod_rows = -(-N_MOD * D // (SUBLANES * LANES)) * SUBLANES
    dmod_all = jnp.reshape(gall[:, :mod_rows], (N_DEV, -1))[:, :N_MOD * D]
    dmod_sh = lax.dynamic_slice(dmod_all, (0, chip * mod_cols), (N_DEV, mod_cols))
    dmod_sh = jnp.pad(dmod_sh, ((0, BF16_ROWS - N_DEV), (0, 0))).astype(BF16)
    g_wada = _matmul(c_act, dmod_sh, "tn", F32, 1024, _tile(mod_cols, 512), BF16_ROWS, "grad_w_ada")

    big = {}

    def update(nm, w, g, m, v, token):
        g_out, d, nm_, nv_ = _adamw(w[0], g, m[0], v[0], f"adamw_{nm}", emit_grad=True, after=[token])
        big[nm] = (g_out[None], d[None], nm_[None], nv_[None])
        return d

    ex, done = _ride([r12, rm_in, rm_out])
    state, token = _exchange_start(ex, "reduce_tail_0_start")
    behind = [update("w1_ffn2", w1_ffn2, r21.result, m_w1_ffn2, v_w1_ffn2, token),
              update("w3_ffn2", w3_ffn2, r23.result, m_w3_ffn2, v_w3_ffn2, token),
              update("w2_ffn2", w2_ffn2, r22.result, m_w2_ffn2, v_w2_ffn2, token)]
    done(_exchange_wait(ex, state, behind, "reduce_tail_0_wait"))
    ex, done = _ride([rm_in, rm_out])
    state, token = _exchange_start(ex, "reduce_tail_1_start")
    behind = [update("w1_ffn1", w1_ffn1, r11.result, m_w1_ffn1, v_w1_ffn1, token),
              update("w3_ffn1", w3_ffn1, r13.result, m_w3_ffn1, v_w3_ffn1, token),
              update("w2_ffn1", w2_ffn1, r12.result, m_w2_ffn1, v_w2_ffn1, token)]
    done(_exchange_wait(ex, state, behind, "reduce_tail_1_wait"))
    ex, done = _ride([rm_out])
    state, token = _exchange_start(ex, "reduce_tail_2_start")
    d_wada, nm_wada, nv_wada = _adamw(w_ada[0], g_wada, m_w_ada[0], v_w_ada[0], "adamw_w_ada", after=[token])
    behind = [d_wada, update("w_in", w_in, rm_in.result, m_w_in, v_w_in, token)]
    done(_exchange_wait(ex, state, behind, "reduce_tail_2_wait"))
    update("w_out", w_out, rm_out.result, m_w_out, v_w_out, token)
    big["w_ada"] = (g_wada[None], d_wada[None], nm_wada[None], nv_wada[None])

    order = ["w_ada", "b_ada", "g_ffn1", "w1_ffn1", "w3_ffn1", "w2_ffn1", "g_mix", "w_in", "spatial_w", "spatial_b",
             "g_v", "g_q", "g_k", "sinks", "rel_bias", "w_out", "g_ffn2", "w1_ffn2", "w3_ffn2", "w2_ffn2"]
    small_names = ["b_ada", "g_ffn1", "g_mix", "g_ffn2", "spatial_w", "spatial_b", "g_v", "g_q", "g_k", "sinks", "rel_bias"]
    for i, nm in enumerate(small_names):
        big[nm] = (sg[i], sd[i], sm[i], sv[i])
    outs = [loss, grad_x[None]]
    for kind in range(4):
        outs += [big[nm][kind] for nm in order]
    return tuple(outs)
```

```python
import functools
import math

import jax
import jax.numpy as jnp
import numpy as np
from jax import lax
from jax.experimental import pallas as pl
from jax.experimental.pallas import tpu as pltpu

F32 = jnp.float32
BF16 = jnp.bfloat16
MESH = pl.DeviceIdType.MESH
ANY = pl.BlockSpec(memory_space=pl.ANY)

EPS = 1e-6
BLOCK = 128
A_HEADS = 8
A_DIM = 128
A_WIDTH = A_HEADS * A_DIM
B_HEADS = 16
KV_HEADS = 2
GROUP = B_HEADS // KV_HEADS
HEAD_DIM = 64
B_WIDTH = B_HEADS * HEAD_DIM
KV_WIDTH = KV_HEADS * HEAD_DIM
Q_OFF = 2 * A_WIDTH
K_OFF = Q_OFF + B_WIDTH
V_OFF = K_OFF + KV_WIDTH
IN_COLS = V_OFF + KV_WIDTH
N_BUCKETS = 32
MAX_DISTANCE = 128
N_MOD = 9
N_CHIPS = 4
N_DEV = 8
NEG = -1e30

ADAM_LR = 0.001
ADAM_B1 = 0.9
ADAM_B2 = 0.999
ADAM_EPS = 1e-08
ADAM_WD = 0.01
ADAM_STEP = 10

LANES = 128
SUBLANES = 8
BF16_ROWS = 16
VMEM_LIMIT = 60 * 1024 * 1024

INV_SQRT2 = 1.0 / math.sqrt(2.0)
INV_SQRT_2PI = 1.0 / math.sqrt(2.0 * math.pi)


def _tile(n, pref, mult=LANES):
    t = (min(pref, n) // mult) * mult
    while t >= mult:
        if n % t == 0:
            return t
        t -= mult
    return n


def _params(sem):
    return pltpu.CompilerParams(dimension_semantics=sem, vmem_limit_bytes=VMEM_LIMIT)


class _Exchange:
    def __init__(self, operands, out_shapes, aliases, n_sems, plan):
        self.operands, self.out_shapes, self.aliases, self.n_sems, self.plan = operands, out_shapes, aliases, n_sems, plan


def _pallas(body, *, name, grid, in_specs, out_specs, out_shape, args, scratch_shapes=(), semantics=None, ex=None,
            after=()):
    if ex is None:
        n_in = len(in_specs)

        def ordered(*refs):
            body(*refs[:n_in], *refs[n_in + len(after):])

        return pl.pallas_call(ordered if after else body, name=name, grid=grid,
                              in_specs=list(in_specs) + [ANY] * len(after), out_specs=out_specs, out_shape=out_shape,
                              scratch_shapes=list(scratch_shapes), compiler_params=_params(semantics))(*args, *after)
    assert not after
    n_in, n_out, n_scr = len(in_specs), len(out_specs), len(scratch_shapes)
    e_in, e_out = len(ex.operands), len(ex.out_shapes)

    def wrapped(*refs):
        ins, refs = refs[:n_in], refs[n_in:]
        ex_ins, refs = refs[:e_in], refs[e_in:]
        outs, refs = refs[:n_out], refs[n_out:]
        ex_outs, refs = refs[:e_out], refs[e_out:]
        scratch, (send_sems, recv_sems) = refs[:n_scr], refs[n_scr:]
        first, last = True, True
        for d, size in enumerate(grid):
            first = jnp.logical_and(first, pl.program_id(d) == 0)
            last = jnp.logical_and(last, pl.program_id(d) == size - 1)

        def start():
            sends, _ = ex.plan(ex_ins, ex_outs, send_sems, recv_sems)
            for cp in sends:
                _remote(*cp).start()

        def finish():
            sends, arrivals = ex.plan(ex_ins, ex_outs, send_sems, recv_sems)
            for cp in arrivals:
                _remote(*cp).wait_recv()
            for cp in sends:
                _remote(*cp).wait_send()

        if grid:
            pl.when(first)(start)
        else:
            start()
        if body is not None:
            body(*ins, *outs, *scratch)
        if grid:
            pl.when(last)(finish)
        else:
            finish()

    kwargs = dict(grid=grid) if grid else {}
    return pl.pallas_call(
        wrapped,
        name=name,
        in_specs=list(in_specs) + [ANY] * e_in,
        out_specs=list(out_specs) + [ANY] * e_out,
        out_shape=list(out_shape) + list(ex.out_shapes),
        input_output_aliases={n_in + i: n_out + o for i, o in ex.aliases.items()},
        scratch_shapes=list(scratch_shapes) + [pltpu.SemaphoreType.DMA((ex.n_sems,)), pltpu.SemaphoreType.DMA((ex.n_sems,))],
        compiler_params=_params(("arbitrary",) * len(grid) if grid else None),
        **kwargs,
    )(*args, *ex.operands)


def _dot(a, b, dims=(((1,), (0,)), ((), ()))):
    return lax.dot_general(a, b, dims, preferred_element_type=F32)


NN = (((1,), (0,)), ((), ()))
NT = (((1,), (1,)), ((), ()))
TN = (((0,), (0,)), ((), ()))


def _sigmoid(x):
    return 1.0 / (1.0 + jnp.exp(-x))


def _gelu_and_grad(x):
    cdf = 0.5 * (1.0 + lax.erf(x * INV_SQRT2))
    pdf = jnp.exp(-0.5 * x * x) * INV_SQRT_2PI
    return x * cdf, cdf + x * pdf


def _gelu(x):
    return x * (0.5 * (1.0 + lax.erf(x * INV_SQRT2)))


def _rms(x):
    r = lax.rsqrt(jnp.mean(x * x, axis=-1, keepdims=True) + EPS)
    return x * r, r


ROW_CHUNK = 64


def _for_rows(tm, fn):
    rc = min(ROW_CHUNK, tm)

    def step(r, carry):
        fn(pl.ds(pl.multiple_of(r * rc, rc), rc))
        return carry

    lax.fori_loop(0, tm // rc, step, 0)


def _rms_bwd(dy, xhat, r):
    return r * (dy - xhat * jnp.mean(dy * xhat, axis=-1, keepdims=True))


def _matmul(a, b, mode, out_dtype, tm, tn, tk, name, shard_major=False, ex=None, after=()):
    if mode == "nn":
        (M, K), N = a.shape, b.shape[1]
    elif mode == "nt":
        (M, K), N = a.shape, b.shape[0]
    else:
        (K, M), N = a.shape, b.shape[1]
    tm, tn, tk = min(tm, M), min(tn, N), min(tk, K)
    assert M % tm == 0 and N % tn == 0 and K % tk == 0, (name, M, N, K, tm, tn, tk)
    nk = K // tk
    dims = {"nn": NN, "nt": NT, "tn": TN}[mode]
    a_spec = pl.BlockSpec((tk, tm), lambda i, j, k: (k, i)) if mode == "tn" else pl.BlockSpec((tm, tk), lambda i, j, k: (i, k))
    b_spec = pl.BlockSpec((tn, tk), lambda i, j, k: (j, k)) if mode == "nt" else pl.BlockSpec((tk, tn), lambda i, j, k: (k, j))
    if shard_major:
        assert tn * N_CHIPS == N
        out_shape = jax.ShapeDtypeStruct((N_CHIPS, M, tn), out_dtype)
        o_spec = pl.BlockSpec((None, tm, tn), lambda i, j, k: (j, i, 0))
    else:
        out_shape = jax.ShapeDtypeStruct((M, N), out_dtype)
        o_spec = pl.BlockSpec((tm, tn), lambda i, j, k: (i, j))

    direct = nk == 1 or out_dtype == F32

    def body(a_ref, b_ref, o_ref, *scratch):
        k = pl.program_id(2)
        if nk == 1:
            o_ref[...] = _dot(a_ref[...], b_ref[...], dims).astype(o_ref.dtype)
            return
        acc_ref = o_ref if direct else scratch[0]

        @pl.when(k == 0)
        def _():
            acc_ref[...] = jnp.zeros(acc_ref.shape, F32)

        acc_ref[...] += _dot(a_ref[...], b_ref[...], dims)
        if not direct:
            @pl.when(k == nk - 1)
            def _():
                o_ref[...] = acc_ref[...].astype(o_ref.dtype)

    outs = _pallas(body, name=name, grid=(M // tm, N // tn, nk), in_specs=[a_spec, b_spec], out_specs=[o_spec],
                   out_shape=[out_shape], scratch_shapes=[] if direct else [pltpu.VMEM((tm, tn), F32)],
                   semantics=("parallel", "parallel", "arbitrary"), args=[a, b], ex=ex, after=after)
    return outs[0] if ex is None else outs


def _mod_partial(c_all, w_ada, b_sh, name):
    R, D = c_all.shape
    N = w_ada.shape[1]
    tn = _tile(N, 512)

    def body(c_ref, w_ref, b_ref, o_ref, ca_ref):
        cv = c_ref[...]
        ca = (cv * _sigmoid(cv)).astype(BF16)
        ca_ref[...] = ca
        o_ref[...] = _dot(ca, w_ref[...].astype(BF16)) + b_ref[...]

    return pl.pallas_call(
        body,
        name=name,
        grid=(N // tn,),
        in_specs=[
            pl.BlockSpec((R, D), lambda j: (0, 0)),
            pl.BlockSpec((D, tn), lambda j: (0, j)),
            pl.BlockSpec((1, tn), lambda j: (0, j)),
        ],
        out_specs=[pl.BlockSpec((R, tn), lambda j: (0, j)), pl.BlockSpec((R, D), lambda j: (0, 0))],
        out_shape=[jax.ShapeDtypeStruct((R, N), F32), jax.ShapeDtypeStruct((R, D), BF16)],
        compiler_params=_params(("arbitrary",)),
    )(c_all, w_ada, b_sh)


FFN_BLOCK = 1024


def _ffn_blocks(F):
    if F % FFN_BLOCK == 0 or F < FFN_BLOCK:
        tf = _tile(F, FFN_BLOCK)
        return tf, F // tf, tf
    nj = -(-F // FFN_BLOCK)
    tail = F - (nj - 1) * FFN_BLOCK
    assert tail % LANES == 0
    return FFN_BLOCK, nj, tail
def _ffn_steps(j, nj, tf, tail, columns):
    if nj == 1:
        columns(tail, True)()
        return
    pl.when(j == 0)(columns(tf, True))
    if nj > 2:
        pl.when(jnp.logical_and(j > 0, j < nj - 1))(columns(tf, False))
    pl.when(j == nj - 1)(columns(tail, False))


def _ffn_fwd(x, g, sh, sc, gt, w1, w3, w2, tgt, name, ex=None, h=None):
    S, D = x.shape
    F = w1.shape[1]
    tm, tf, nj, tail = _tile(S, 512), *_ffn_blocks(F)
    ni = S // tm
    with_loss = tgt is not None
    assert h is None or not with_loss

    def body(*refs):
        if with_loss:
            (x_ref, g_ref, sh_ref, sc_ref, gt_ref, w1_ref, w3_ref, w2_ref, tgt_ref,
             gout_ref, df_ref, h_ref, a_ref, b_ref, dgt_ref, loss_ref, acc_ref) = refs
        elif h is None:
            (x_ref, g_ref, sh_ref, sc_ref, gt_ref, w1_ref, w3_ref, w2_ref,
             xo_ref, h_ref, a_ref, b_ref, f_ref, acc_ref) = refs
        else:
            (x_ref, g_ref, sh_ref, sc_ref, gt_ref, w1_ref, w3_ref, w2_ref, h_ref,
             xo_ref, a_ref, b_ref, f_ref, acc_ref) = refs
        i, j = pl.program_id(0), pl.program_id(1)

        if h is None:
            @pl.when(j == 0)
            def _():
                def prologue(rows):
                    xhat, _ = _rms(x_ref[rows, :])
                    h_ref[rows, :] = ((xhat * g_ref[...]) * (1.0 + sc_ref[...]) + sh_ref[...]).astype(BF16)

                _for_rows(tm, prologue)

        def columns(width, first):
            def run():
                hb = h_ref[...]
                av = _dot(hb, w1_ref[:, :width])
                bv = _dot(hb, w3_ref[:, :width])
                a_ref[:, :width] = av.astype(BF16)
                b_ref[:, :width] = bv.astype(BF16)
                sv = ((av * _sigmoid(av)) * bv).astype(BF16)
                if first:
                    acc_ref[...] = _dot(sv, w2_ref[:width, :])
                else:
                    acc_ref[...] += _dot(sv, w2_ref[:width, :])
            return run

        _ffn_steps(j, nj, tf, tail, columns)

        @pl.when(j == nj - 1)
        def _():
            if with_loss:
                @pl.when(i == 0)
                def _():
                    dgt_ref[...] = jnp.zeros(dgt_ref.shape, F32)
                    loss_ref[...] = jnp.zeros(loss_ref.shape, F32)

            def epilogue(rows):
                fv = acc_ref[rows, :]
                half_gate = 0.5 * gt_ref[...]
                xo = x_ref[rows, :] + half_gate * fv
                if not with_loss:
                    xo_ref[rows, :] = xo
                    f_ref[rows, :] = fv.astype(f_ref.dtype)
                    return
                err = xo - tgt_ref[rows, :]
                gout = err * (1.0 / D)
                gout_ref[rows, :] = gout
                df_ref[rows, :] = (half_gate * gout).astype(BF16)
                dgt_ref[...] += 0.5 * jnp.sum(gout * fv, axis=0, keepdims=True)
                loss_part = jnp.sum(jnp.sum(err * err, axis=1, keepdims=True), axis=0, keepdims=True)
                loss_ref[...] += jnp.broadcast_to(loss_part, loss_ref.shape)

            _for_rows(tm, epilogue)

    row = pl.BlockSpec((tm, D), lambda i, j: (i, 0))
    row_in = pl.BlockSpec((tm, D), lambda i, j: (i, 0), pipeline_mode=pl.Buffered(1))
    vec = pl.BlockSpec((1, D), lambda i, j: (0, 0))
    col = pl.BlockSpec((tm, tf), lambda i, j: (i, j))
    in_specs = [row_in, vec, vec, vec, vec,
                pl.BlockSpec((D, tf), lambda i, j: (0, j)),
                pl.BlockSpec((D, tf), lambda i, j: (0, j)),
                pl.BlockSpec((tf, D), lambda i, j: (j, 0))]
    args = [x, g, sh, sc, gt, w1, w3, w2]
    act = jax.ShapeDtypeStruct((S, F), BF16)
    if with_loss:
        in_specs.append(row_in)
        args.append(tgt)
        out_specs = [row, row, row_in, col, col, vec, pl.BlockSpec((1, LANES), lambda i, j: (0, 0))]
        out_shape = [jax.ShapeDtypeStruct((S, D), F32), jax.ShapeDtypeStruct((S, D), BF16),
                     jax.ShapeDtypeStruct((S, D), BF16), act, act,
                     jax.ShapeDtypeStruct((1, D), F32), jax.ShapeDtypeStruct((1, LANES), F32)]
    elif h is None:
        out_specs = [row, row, col, col, row]
        out_shape = [jax.ShapeDtypeStruct((S, D), F32), jax.ShapeDtypeStruct((S, D), BF16), act, act,
                     jax.ShapeDtypeStruct((S, D), BF16)]
    else:
        in_specs.append(row_in)
        args.append(h)
        out_specs = [row, col, col, row]
        out_shape = [jax.ShapeDtypeStruct((S, D), F32), act, act, jax.ShapeDtypeStruct((S, D), BF16)]
    return _pallas(body, name=name, grid=(ni, nj), in_specs=in_specs, out_specs=out_specs, out_shape=out_shape,
                   scratch_shapes=[pltpu.VMEM((tm, D), F32)],
                   semantics=("arbitrary", "arbitrary"), args=args, ex=ex)


def _ffn_bwd(df, a, b, w1, w3, w2, name, ex=None):
    S, D = df.shape
    F = a.shape[1]
    tm, tf, nj, tail = _tile(S, 512), *_ffn_blocks(F)

    def body(df_ref, a_ref, b_ref, w1_ref, w3_ref, w2_ref, da_ref, db_ref, s_ref, dh_out_ref, dh_ref):
        j = pl.program_id(1)

        def columns(width, first):
            def run():
                ds = _dot(df_ref[...], w2_ref[:width, :], NT)
                av = a_ref[:, :width].astype(F32)
                bv = b_ref[:, :width].astype(F32)
                sig = _sigmoid(av)
                sil = av * sig
                da = ((ds * bv) * (sig * (1.0 + av * (1.0 - sig)))).astype(BF16)
                db = (ds * sil).astype(BF16)
                da_ref[:, :width] = da
                db_ref[:, :width] = db
                s_ref[:, :width] = (sil * bv).astype(BF16)
                part = _dot(da, w1_ref[:, :width], NT) + _dot(db, w3_ref[:, :width], NT)
                if first:
                    dh_ref[...] = part
                else:
                    dh_ref[...] += part
            return run

        _ffn_steps(j, nj, tf, tail, columns)

        @pl.when(j == nj - 1)
        def _():
            dh_out_ref[...] = dh_ref[...].astype(BF16)

    row = pl.BlockSpec((tm, D), lambda i, j: (i, 0))
    col = pl.BlockSpec((tm, tf), lambda i, j: (i, j))
    act = jax.ShapeDtypeStruct((S, F), BF16)
    return _pallas(body, name=name, grid=(S // tm, nj), scratch_shapes=[pltpu.VMEM((tm, D), F32)],
                   in_specs=[row, col, col,
                             pl.BlockSpec((D, tf), lambda i, j: (0, j)),
                             pl.BlockSpec((D, tf), lambda i, j: (0, j)),
                             pl.BlockSpec((tf, D), lambda i, j: (j, 0))],
                   out_specs=[col, col, col, row],
                   out_shape=[act, act, act, jax.ShapeDtypeStruct((S, D), BF16)],
                   semantics=("parallel", "arbitrary"), args=[df, a, b, w1, w3, w2], ex=ex)


def _norm_mod(x, g, sh, sc, name, ex=None, after=()):
    S, D = x.shape
    tm = _tile(S, 512)

    def body(x_ref, g_ref, sh_ref, sc_ref, h_ref):
        def step(rows):
            xhat, _ = _rms(x_ref[rows, :])
            h_ref[rows, :] = ((xhat * g_ref[...]) * (1.0 + sc_ref[...]) + sh_ref[...]).astype(BF16)

        _for_rows(tm, step)

    row = pl.BlockSpec((tm, D), lambda i: (i, 0))
    vec = pl.BlockSpec((1, D), lambda i: (0, 0))
    outs = _pallas(body, name=name, grid=(S // tm,), in_specs=[row, vec, vec, vec], out_specs=[row],
                   out_shape=[jax.ShapeDtypeStruct((S, D), BF16)], semantics=("parallel",), args=[x, g, sh, sc], ex=ex,
                   after=after)
    return outs[0] if ex is None else outs


def _norm_bwd(dh, x, gres, g, sc, prev, name, ex=None, after=()):
    S, D = x.shape
    tm = _tile(S, 256)
    has_prev = prev is not None
    coef = prev[2] if has_prev else None

    def body(*refs):
        if has_prev:
            (dh_ref, x_ref, gr_ref, g_ref, sc_ref, f_ref, gt_ref,
             go_ref, dsh_ref, dsc_ref, dg_ref, dp_ref, dgt_ref) = refs
        else:
            dh_ref, x_ref, gr_ref, g_ref, sc_ref, go_ref, dsh_ref, dsc_ref, dg_ref = refs
        sum_refs = [dsh_ref, dsc_ref, dg_ref] + ([dgt_ref] if has_prev else [])

        @pl.when(pl.program_id(0) == 0)
        def _():
            for ref in sum_refs:
                ref[...] = jnp.zeros(ref.shape, F32)

        def step(rows):
            dh = dh_ref[rows, :].astype(F32)
            xhat, r = _rms(x_ref[rows, :])
            gain = g_ref[...]
            scale1 = 1.0 + sc_ref[...]
            gout = gr_ref[rows, :] + _rms_bwd(dh * scale1 * gain, xhat, r)
            go_ref[rows, :] = gout
            sums = [dh, dh * (xhat * gain), dh * scale1 * xhat]
            if has_prev:
                dp_ref[rows, :] = ((coef * gt_ref[...]) * gout).astype(BF16)
                sums.append(coef * (gout * f_ref[rows, :].astype(F32)))
            for ref, v in zip(sum_refs, sums):
                ref[...] += jnp.sum(v, axis=0, keepdims=True)

        _for_rows(tm, step)

    row = pl.BlockSpec((tm, D), lambda i: (i, 0))
    vec = pl.BlockSpec((1, D), lambda i: (0, 0))
    vshape = jax.ShapeDtypeStruct((1, D), F32)
    in_specs = [row, row, row, vec, vec]
    args = [dh, x, gres, g, sc]
    out_specs = [row, vec, vec, vec]
    out_shape = [jax.ShapeDtypeStruct((S, D), F32), vshape, vshape, vshape]
    if has_prev:
        in_specs += [row, vec]
        args += [prev[0], prev[1]]
        out_specs += [row, vec]
        out_shape += [jax.ShapeDtypeStruct((S, D), BF16), vshape]
    return _pallas(body, name=name, grid=(S // tm,), in_specs=in_specs, out_specs=out_specs, out_shape=out_shape,
                   semantics=("arbitrary",), args=args, ex=ex, after=after)


PAIRS = B_HEADS // 2
PAIR_ROWS = (PAIRS // KV_HEADS) * BLOCK
BAND = 2 * BLOCK


def _stack(ref, offset, count):
    return jnp.concatenate([ref[:, offset + p * LANES:offset + (p + 1) * LANES] for p in range(count)], axis=0)


def _seg_mean(x, e_ref):
    return _dot(x.astype(BF16), e_ref[...]) * (1.0 / HEAD_DIM)


def _block_diag(x, x_rolled, left, kv_head):
    if kv_head == 0:
        top, bottom = jnp.where(left, x, 0.0), jnp.where(left, 0.0, x_rolled)
    else:
        top, bottom = jnp.where(left, x_rolled, 0.0), jnp.where(left, 0.0, x)
    return jnp.concatenate([top, bottom], axis=0).astype(BF16)


def _from_block_diag(g, left, kv_head):
    a, b = g[:BAND], g[BAND:]
    if kv_head == 0:
        return jnp.where(left, a + pltpu.roll(b, HEAD_DIM, 1), 0.0)
    return jnp.where(left, 0.0, pltpu.roll(a, HEAD_DIM, 1) + b)


def _pair_softmax(st, sk_ref, kv_head):
    out = []
    for e in range(2):
        seg = st[e * BAND:(e + 1) * BAND]
        sink = jnp.concatenate([jnp.full((1, BLOCK), sk_ref[kv_head * GROUP + 2 * p + e], F32)
                                for p in range(PAIRS // KV_HEADS)], axis=1)
        m = jnp.maximum(jnp.max(seg, axis=0, keepdims=True), sink)
        p_ = jnp.exp(seg - m)
        e_sink = jnp.exp(sink - m)
        inv = 1.0 / (jnp.sum(p_, axis=0, keepdims=True) + e_sink)
        out.append((p_ * inv, e_sink * inv))
    return out


def _lane_mean(x, ones_ref):
    return _dot(x.astype(BF16), ones_ref[...])


def _mixer_specs(nb, last):
    full = lambda shape: pl.BlockSpec(shape, lambda n: (0,) * len(shape))
    z_spec = pl.BlockSpec((BLOCK, IN_COLS), lambda n: (jnp.minimum(n, last), 0))
    zp_spec = pl.BlockSpec((BLOCK, 2 * KV_WIDTH), lambda n: (jnp.clip(n - 1, 0, last), K_OFF // (2 * KV_WIDTH)))
    consts = [full((A_HEADS * BLOCK, A_DIM)), full((A_HEADS * BLOCK, A_DIM)), full((1, LANES)), full((1, LANES)),
              full((LANES, LANES)), full((LANES, LANES)), pl.BlockSpec(memory_space=pltpu.SMEM),
              pl.BlockSpec((None, KV_HEADS, PAIR_ROWS, 2 * BAND), lambda n: (jnp.minimum(n, 1), 0, 0, 0))]
    return full, z_spec, zp_spec, consts


def _mixer_fwd(z, wm, sbp, gvp, gq2, gk2, seg_ones, lane_ones, sinks, biasp, name):
    S = z.shape[0]
    nb = S // BLOCK

    def body(z_ref, zp_ref, wm_ref, sbp_ref, gvp_ref, gq2_ref, gk2_ref, e_ref, l_ref, sk_ref, bias_ref, mix_ref):
        u = _gelu(_stack(z_ref, 0, A_HEADS))
        v = _gelu(_stack(z_ref, A_WIDTH, A_HEADS))
        vhat = v * lax.rsqrt(_lane_mean(v * v, l_ref) + EPS)
        vn = (vhat * gvp_ref[...]).astype(BF16)
        mixed = jnp.concatenate([_dot(wm_ref[h], vn[h * BLOCK:(h + 1) * BLOCK]) for h in range(A_HEADS)], axis=0)
        ya = (u * (mixed + sbp_ref[...])).astype(BF16)
        for h in range(A_HEADS):
            mix_ref[:, h * A_DIM:(h + 1) * A_DIM] = ya[h * BLOCK:(h + 1) * BLOCK]

        left = lax.broadcasted_iota(jnp.int32, (1, LANES), 1) < HEAD_DIM
        kv = jnp.concatenate([zp_ref[...], z_ref[:, K_OFF:K_OFF + 2 * KV_WIDTH]], axis=0)
        k2, v2 = kv[:, :KV_WIDTH], kv[:, KV_WIDTH:]
        kn2 = k2 * lax.rsqrt(_seg_mean(k2 * k2, e_ref) + EPS) * gk2_ref[...]
        kn2_r, v2_r = pltpu.roll(kn2, HEAD_DIM, 1), pltpu.roll(v2, HEAD_DIM, 1)
        qp = _stack(z_ref, Q_OFF, PAIRS)
        qn = (qp * lax.rsqrt(_seg_mean(qp * qp, e_ref) + EPS) * gq2_ref[...]).astype(BF16)
        for kh in range(KV_HEADS):
            kbd, vbd = _block_diag(kn2, kn2_r, left, kh), _block_diag(v2, v2_r, left, kh)
            st = _dot(kbd, qn[kh * PAIR_ROWS:(kh + 1) * PAIR_ROWS], NT) * (HEAD_DIM ** -0.5) + bias_ref[kh]
            wt = jnp.concatenate([w_e for w_e, _ in _pair_softmax(st, sk_ref, kh)], axis=0).astype(BF16)
            o = _dot(wt, vbd, TN).astype(BF16)
            for p in range(PAIRS // KV_HEADS):
                col = A_WIDTH + (kh * (PAIRS // KV_HEADS) + p) * LANES
                mix_ref[:, col:col + LANES] = o[p * BLOCK:(p + 1) * BLOCK]

    full, z_spec, zp_spec, consts = _mixer_specs(nb, nb - 1)
    return pl.pallas_call(
        body,
        name=name,
        grid=(nb,),
        in_specs=[z_spec, zp_spec, full((A_HEADS, BLOCK, BLOCK))] + consts,
        out_specs=pl.BlockSpec((BLOCK, A_WIDTH + B_WIDTH), lambda n: (n, 0)),
        out_shape=jax.ShapeDtypeStruct((S, A_WIDTH + B_WIDTH), BF16),
        compiler_params=_params(("parallel",)),
    )(z, z, wm, sbp, gvp, gq2, gk2, seg_ones, lane_ones, sinks, biasp)


def _mixer_bwd(z, dmix, wm, wm_t, sbp, gvp, gq2, gk2, seg_ones, lane_ones, sinks, biasp, pair_fold, name, ex=None):
    S = z.shape[0]
    nb = S // BLOCK

    def body(z_ref, zp_ref, dmix_ref, wm_ref, wmt_ref, sbp_ref, gvp_ref, gq2_ref, gk2_ref, e_ref, l_ref, sk_ref,
             bias_ref, fold_ref,
             dz_ref, dzkv_ref, dwm_ref, dsb_ref, dgv_ref, dgq_ref, dgk_ref, dsk_ref, dst_ref,
             carry_ref, tot_ref, sbacc_ref, skacc_ref, gqacc_ref, gkacc_ref):
        n = pl.program_id(0)
        left = lax.broadcasted_iota(jnp.int32, (1, LANES), 1) < HEAD_DIM

        @pl.when(n == 0)
        def _():
            for ref in (dwm_ref, dgv_ref, dst_ref, carry_ref, sbacc_ref, skacc_ref, gqacc_ref, gkacc_ref):
                ref[...] = jnp.zeros(ref.shape, ref.dtype)

        @pl.when(n < nb)
        def _():
            u, du_dz = _gelu_and_grad(_stack(z_ref, 0, A_HEADS))
            v, dv_dz = _gelu_and_grad(_stack(z_ref, A_WIDTH, A_HEADS))
            rv = lax.rsqrt(_lane_mean(v * v, l_ref) + EPS)
            vhat = v * rv
            gvp = gvp_ref[...]
            vn = (vhat * gvp).astype(BF16)
            rows = lambda a, h: a[h * BLOCK:(h + 1) * BLOCK]
            mixed = jnp.concatenate([_dot(wm_ref[h], rows(vn, h)) for h in range(A_HEADS)], axis=0) + sbp_ref[...]
            dya = _stack(dmix_ref, 0, A_HEADS).astype(F32)
            dmx = dya * u
            sbacc_ref[...] += dmx
            dmx_b = dmx.astype(BF16)
            for h in range(A_HEADS):
                dwm_ref[h] += _dot(rows(dmx_b, h), rows(vn, h), NT)
            dvn = jnp.concatenate([_dot(wmt_ref[h], rows(dmx_b, h)) for h in range(A_HEADS)], axis=0)
            dgv_ref[...] += jnp.sum(jnp.reshape(dvn * vhat, (A_HEADS, BLOCK, A_DIM)), axis=1)
            dzu = ((dya * mixed) * du_dz).astype(BF16)
            tv = dvn * gvp
            dzv = ((rv * (tv - vhat * _lane_mean(tv * vhat, l_ref))) * dv_dz).astype(BF16)
            for h in range(A_HEADS):
                dz_ref[:, h * A_DIM:(h + 1) * A_DIM] = rows(dzu, h)
                dz_ref[:, A_WIDTH + h * A_DIM:A_WIDTH + (h + 1) * A_DIM] = rows(dzv, h)

            kv = jnp.concatenate([zp_ref[...], z_ref[:, K_OFF:K_OFF + 2 * KV_WIDTH]], axis=0)
            k2, v2 = kv[:, :KV_WIDTH], kv[:, KV_WIDTH:]
            kn2 = k2 * lax.rsqrt(_seg_mean(k2 * k2, e_ref) + EPS) * gk2_ref[...]
            kn2_r, v2_r = pltpu.roll(kn2, HEAD_DIM, 1), pltpu.roll(v2, HEAD_DIM, 1)
            qp = _stack(z_ref, Q_OFF, PAIRS)
            rq = lax.rsqrt(_seg_mean(qp * qp, e_ref) + EPS)
            qhat = qp * rq
            gq2 = gq2_ref[...]
            qn = (qhat * gq2).astype(BF16)
            dop = _stack(dmix_ref, A_WIDTH, PAIRS)
            dqn_parts = []
            dk2n = jnp.zeros((BAND, KV_WIDTH), F32)
            dv2 = jnp.zeros((BAND, KV_WIDTH), F32)
            for kh in range(KV_HEADS):
                kbd, vbd = _block_diag(kn2, kn2_r, left, kh), _block_diag(v2, v2_r, left, kh)
                qg = qn[kh * PAIR_ROWS:(kh + 1) * PAIR_ROWS]
                dog = dop[kh * PAIR_ROWS:(kh + 1) * PAIR_ROWS]
                st = _dot(kbd, qg, NT) * (HEAD_DIM ** -0.5) + bias_ref[kh]
                halves = _pair_softmax(st, sk_ref, kh)
                dpt = _dot(vbd, dog, NT)
                ds_halves, t_halves = [], []
                for e, (w_e, w_sink) in enumerate(halves):
                    dp_e = dpt[e * BAND:(e + 1) * BAND]
                    delta = jnp.sum(w_e * dp_e, axis=0, keepdims=True)
                    ds_halves.append(w_e * (dp_e - delta))
                    t_halves.append(-(w_sink * delta))
                dst = jnp.concatenate(ds_halves, axis=0)
                dst_ref[kh] += dst
                skacc_ref[2 * kh:2 * kh + 2, :] += jnp.concatenate(t_halves, axis=0)
                ds_b = (dst * (HEAD_DIM ** -0.5)).astype(BF16)
                w_b = jnp.concatenate([w_e for w_e, _ in halves], axis=0).astype(BF16)
                dqn_parts.append(_dot(ds_b, kbd, TN))
                dk2n += _from_block_diag(_dot(ds_b, qg), left, kh)
                dv2 += _from_block_diag(_dot(w_b, dog), left, kh)
            dqn = jnp.concatenate(dqn_parts, axis=0)
            gqacc_ref[...] += jnp.sum(dqn * qhat, axis=0, keepdims=True)
            t = dqn * gq2
            dzq = (rq * (t - qhat * _seg_mean(t * qhat, e_ref))).astype(BF16)
            for p in range(PAIRS):
                dz_ref[:, Q_OFF + p * LANES:Q_OFF + (p + 1) * LANES] = rows(dzq, p)
            tot_ref[0] = carry_ref[0] + dk2n[:BLOCK]
            tot_ref[1] = carry_ref[1] + dv2[:BLOCK]
            carry_ref[0] = dk2n[BLOCK:]
            carry_ref[1] = dv2[BLOCK:]

        @pl.when(n == nb)
        def _():
            tot_ref[...] = carry_ref[...]

        kp = zp_ref[:, :KV_WIDTH]
        rk = lax.rsqrt(_seg_mean(kp * kp, e_ref) + EPS)
        khat = kp * rk
        dkn = tot_ref[0]
        gkacc_ref[...] += jnp.sum(dkn * khat, axis=0, keepdims=True)
        t = dkn * gk2_ref[...]
        dzkv_ref[:, :KV_WIDTH] = (rk * (t - khat * _seg_mean(t * khat, e_ref))).astype(BF16)
        dzkv_ref[:, KV_WIDTH:] = tot_ref[1].astype(BF16)

        @pl.when(n == nb)
        def _():
            dsb_ref[...] = jnp.broadcast_to(jnp.sum(sbacc_ref[...], axis=1, keepdims=True), dsb_ref.shape)
            dsk_ref[...] = lax.dot_general(skacc_ref[...], fold_ref[...], NN, precision=lax.Precision.HIGHEST,
                                           preferred_element_type=F32)
            dgq_ref[...] = gqacc_ref[...] + pltpu.roll(gqacc_ref[...], HEAD_DIM, 1)
            dgk_ref[...] = gkacc_ref[...] + pltpu.roll(gkacc_ref[...], HEAD_DIM, 1)

    last = nb - 1
    full, z_spec, zp_spec, consts = _mixer_specs(nb, last)
    return _pallas(
        body,
        name=name,
        grid=(nb + 1,),
        ex=ex,
        in_specs=[z_spec, zp_spec, pl.BlockSpec((BLOCK, A_WIDTH + B_WIDTH), lambda n: (jnp.minimum(n, last), 0)),
                  full((A_HEADS, BLOCK, BLOCK)), full((A_HEADS, BLOCK, BLOCK))] + consts + [full((PAIR_ROWS, LANES))],
        out_specs=[
            pl.BlockSpec((BLOCK, K_OFF), lambda n: (jnp.minimum(n, last), 0)),
            pl.BlockSpec((BLOCK, 2 * KV_WIDTH), lambda n: (jnp.maximum(n - 1, 0), 0)),
            full((A_HEADS, BLOCK, BLOCK)), full((A_HEADS * BLOCK, A_DIM)), full((A_HEADS, A_DIM)),
            full((1, LANES)), full((1, LANES)), full((SUBLANES, LANES)),
            full((KV_HEADS, PAIR_ROWS, 2 * BAND)),
        ],
        out_shape=[
            jax.ShapeDtypeStruct((S, K_OFF), BF16),
            jax.ShapeDtypeStruct((S, 2 * KV_WIDTH), BF16),
            jax.ShapeDtypeStruct((A_HEADS, BLOCK, BLOCK), F32),
            jax.ShapeDtypeStruct((A_HEADS * BLOCK, A_DIM), F32),
            jax.ShapeDtypeStruct((A_HEADS, A_DIM), F32),
            jax.ShapeDtypeStruct((1, LANES), F32),
            jax.ShapeDtypeStruct((1, LANES), F32),
            jax.ShapeDtypeStruct((SUBLANES, LANES), F32),
            jax.ShapeDtypeStruct((KV_HEADS, PAIR_ROWS, 2 * BAND), F32),
        ],
        scratch_shapes=[
            pltpu.VMEM((2, BLOCK, KV_WIDTH), F32),
            pltpu.VMEM((2, BLOCK, KV_WIDTH), F32),
            pltpu.VMEM((A_HEADS * BLOCK, A_DIM), F32),
            pltpu.VMEM((SUBLANES, PAIR_ROWS), F32),
            pltpu.VMEM((1, LANES), F32),
            pltpu.VMEM((1, LANES), F32),
        ],
        semantics=("arbitrary",),
        args=[z, z, dmix, wm, wm_t, sbp, gvp, gq2, gk2, seg_ones, lane_ones, sinks, biasp, pair_fold],
    )


def _mixer_out(mix, w_out, x, gt, name):
    S, D = x.shape
    K = mix.shape[1]
    tm, tn = _tile(S, 1024), _tile(D, 1024)

    def body(m_ref, w_ref, x_ref, gt_ref, xo_ref, y_ref):
        y = _dot(m_ref[...], w_ref[...])
        y_ref[...] = y.astype(BF16)
        xo_ref[...] = x_ref[...] + gt_ref[...] * y

    blk = pl.BlockSpec((tm, tn), lambda j, i: (i, j))
    return pl.pallas_call(
        body,
        name=name,
        grid=(D // tn, S // tm),
        in_specs=[pl.BlockSpec((tm, K), lambda j, i: (i, 0)), pl.BlockSpec((K, tn), lambda j, i: (0, j)),
                  blk, pl.BlockSpec((1, tn), lambda j, i: (0, j))],
        out_specs=[blk, blk],
        out_shape=[jax.ShapeDtypeStruct((S, D), F32), jax.ShapeDtypeStruct((S, D), BF16)],
        compiler_params=_params(("parallel", "parallel")),
    )(mix, w_out, x, gt)


def _bucket_sum(dst, onehot, name):
    def body(d_ref, o_ref, out_ref):
        out_ref[...] = lax.dot_general(o_ref[...], d_ref[...], NT, precision=lax.Precision.HIGHEST,
                                       preferred_element_type=F32)

    return pl.pallas_call(
        body,
        name=name,
        out_shape=jax.ShapeDtypeStruct((N_BUCKETS, B_HEADS), F32),
    )(dst, onehot)


def _adamw_math(w, g, m, v):
    m = ADAM_B1 * m + (1.0 - ADAM_B1) * g
    v = ADAM_B2 * v + (1.0 - ADAM_B2) * (g * g)
    m_hat = m / (1.0 - ADAM_B1 ** ADAM_STEP)
    v_hat = v / (1.0 - ADAM_B2 ** ADAM_STEP)
    delta = -ADAM_LR * (m_hat / (jnp.sqrt(v_hat) + ADAM_EPS) + ADAM_WD * w)
    return delta, m, v


def _adamw(w, g, m, v, name, emit_grad=False, after=()):
    R, C = w.shape
    tr = _tile(R, max(SUBLANES, (1 << 19) // C), SUBLANES)

    def body(w_ref, g_ref, m_ref, v_ref, *out_refs):
        gv = g_ref[...]
        results = _adamw_math(w_ref[...], gv, m_ref[...], v_ref[...])
        for ref, val in zip(out_refs, ((gv,) if emit_grad else ()) + results):
            ref[...] = val

    blk = pl.BlockSpec((tr, C), lambda i: (i, 0))
    shape = jax.ShapeDtypeStruct((R, C), F32)
    n_out = 4 if emit_grad else 3
    return _pallas(body, name=name, grid=(R // tr,), in_specs=[blk] * 4, out_specs=[blk] * n_out,
                   out_shape=[shape] * n_out, semantics=("parallel",), args=[w, g, m, v], after=after)


def _small_update(parts, w, m, v, name):
    R = w.shape[0]

    def body(p_ref, w_ref, m_ref, v_ref, g_ref, d_ref, mo_ref, vo_ref):
        g = p_ref[0]
        for dev in range(1, N_DEV):
            g = g + p_ref[dev]
        g_ref[...] = g
        d, mn, vn = _adamw_math(w_ref[...], g, m_ref[...], v_ref[...])
        d_ref[...] = d
        mo_ref[...] = mn
        vo_ref[...] = vn

    shape = jax.ShapeDtypeStruct((R, LANES), F32)
    return pl.pallas_call(
        body,
        name=name,
        out_shape=[shape] * 4,
        compiler_params=pltpu.CompilerParams(vmem_limit_bytes=VMEM_LIMIT),
    )(parts, w, m, v)


def _place():
    x, y, c = lax.axis_index("x"), lax.axis_index("y"), lax.axis_index("c")
    chips = [(1 - x, y), (x, 1 - y), (1 - x, 1 - y)]
    return x, y, c, chips


def _remote(src, dst, send_sem, recv_sem, to):
    return pltpu.make_async_remote_copy(src_ref=src, dst_ref=dst, send_sem=send_sem, recv_sem=recv_sem,
                                        device_id=to, device_id_type=MESH)


def _allgather_small(block, name):
    m_per, n = block.shape

    def body(x_ref, out_ref, send_sems, recv_sems, local_sem):
        x, y, c, chips = _place()
        me, sibling = (x, y, c), (x, y, 1 - c)

        def rows(px, py, pc):
            return out_ref.at[pl.ds((4 * px + 2 * py + pc) * m_per, m_per), :]

        def copy(k, blk, to, src=None):
            return _remote(rows(*blk) if src is None else src, rows(*blk), send_sems.at[k], recv_sems.at[k], to)

        mine = pltpu.make_async_copy(x_ref, rows(*me), local_sem)
        mine.start()
        first = [copy(0, me, sibling, src=x_ref)]
        first += [copy(1 + j, me, (*chip, c), src=x_ref) for j, chip in enumerate(chips)]
        for cp in first:
            cp.start()
        passed = [copy(4 + j, (*chip, c), sibling) for j, chip in enumerate(chips)]
        for j, chip in enumerate(chips):
            copy(1 + j, (*chip, c), me).wait_recv()
            passed[j].start()
        copy(0, sibling, me).wait_recv()
        for j, chip in enumerate(chips):
            copy(4 + j, (*chip, 1 - c), me).wait_recv()
        for cp in first + passed:
            cp.wait_send()
        mine.wait()

    return pl.pallas_call(
        body,
        name=name,
        out_shape=jax.ShapeDtypeStruct((N_DEV * m_per, n), block.dtype),
        in_specs=[pl.BlockSpec(memory_space=pltpu.VMEM)],
        out_specs=pl.BlockSpec(memory_space=pltpu.VMEM),
        scratch_shapes=[pltpu.SemaphoreType.DMA((7,)), pltpu.SemaphoreType.DMA((7,)), pltpu.SemaphoreType.DMA],
        compiler_params=pltpu.CompilerParams(vmem_limit_bytes=VMEM_LIMIT),
    )(block)


def _half(ref, c, rows):
    start = pl.multiple_of(c * rows, BF16_ROWS)
    if len(ref.shape) == 2:
        return ref.at[pl.ds(start, rows), :]
    return ref.at[:, pl.ds(start, rows), :]


def _slot(ref, index):
    if len(ref.shape) == 3:
        return ref.at[index]
    width = ref.shape[1] // N_CHIPS
    return ref.at[:, pl.ds(pl.multiple_of(index * width, LANES), width)]


def _shard_rows(buf):
    return buf.shape[1] if len(buf.shape) == 3 else buf.shape[0]


def _same(arrays):
    return [jax.ShapeDtypeStruct(a.shape, a.dtype) for a in arrays], {t: t for t in range(len(arrays))}


def _ex_gather_ici(bufs):
    def plan(ins, outs, send_sems, recv_sems):
        x, y, c, chips = _place()
        sends, arrivals = [], []
        for t, buf in enumerate(bufs):
            rows = _shard_rows(buf) // 2
            mine = _half(_slot(outs[t], 2 * x + y), c, rows)
            for k, (px, py) in enumerate(chips):
                sems = (send_sems.at[3 * t + k], recv_sems.at[3 * t + k], (px, py, c))
                landed = _half(_slot(outs[t], 2 * px + py), c, rows)
                sends.append((mine, mine, *sems))
                arrivals.append((landed, landed, *sems))
        return sends, arrivals

    shapes, aliases = _same(bufs)
    return _Exchange(bufs, shapes, aliases, 3 * len(bufs), plan)


def _ex_gather_near(bufs):
    def plan(ins, outs, send_sems, recv_sems):
        x, y, c, chips = _place()
        sends, arrivals = [], []
        for t, buf in enumerate(bufs):
            rows = _shard_rows(buf) // 2
            mine = _half(_slot(outs[t], 2 * x + y), c, rows)
            for k, (px, py) in enumerate(chips[:2]):
                sems = (send_sems.at[2 * t + k], recv_sems.at[2 * t + k], (px, py, c))
                landed = _half(_slot(outs[t], 2 * px + py), c, rows)
                sends.append((mine, mine, *sems))
                arrivals.append((landed, landed, *sems))
        return sends, arrivals

    shapes, aliases = _same(bufs)
    return _Exchange(bufs, shapes, aliases, 2 * len(bufs), plan)


def _ex_gather_relay(bufs):
    def plan(ins, outs, send_sems, recv_sems):
        x, y, c, chips = _place()
        (xn, yn, diag) = chips
        slot = lambda chip: 2 * chip[0] + chip[1]
        sends, arrivals = [], []
        for t, buf in enumerate(bufs):
            quarter = _shard_rows(buf) // 4

            def piece(chip, q):
                start = pl.multiple_of(c * 2 * quarter + q * quarter, BF16_ROWS)
                return _slot(outs[t], slot(chip)).at[pl.ds(start, quarter), :]

            for k, (held, to) in enumerate([(xn, yn), (yn, xn)]):
                sems = (send_sems.at[2 * t + k], recv_sems.at[2 * t + k], (*to, c))
                sends.append((piece(held, k), piece(held, k), *sems))
                arrivals.append((piece(diag, k), piece(diag, k), *sems))
        onward = _ex_gather_d2d(bufs, which=(0, 1))
        more_sends, more_arrivals = onward.plan(ins, outs, _Shifted(send_sems, 2 * len(bufs)),
                                                _Shifted(recv_sems, 2 * len(bufs)))
        return sends + more_sends, arrivals + more_arrivals

    shapes, aliases = _same(bufs)
    return _Exchange(bufs, shapes, aliases, 2 * len(bufs) + 3 * len(bufs), plan)


def _ex_gather_d2d(bufs, which=(0, 1, 2)):
    def plan(ins, outs, send_sems, recv_sems):
        x, y, c, chips = _place()
        sends, arrivals = [], []
        for t, buf in enumerate(bufs):
            rows = _shard_rows(buf) // 2
            for k in which:
                px, py = chips[k]
                sems = (send_sems.at[3 * t + k], recv_sems.at[3 * t + k], (x, y, 1 - c))
                landed = _half(_slot(outs[t], 2 * px + py), c, rows)
                other = _half(_slot(outs[t], 2 * px + py), 1 - c, rows)
                sends.append((landed, landed, *sems))
                arrivals.append((other, other, *sems))
        return sends, arrivals

    shapes, aliases = _same(bufs)
    return _Exchange(bufs, shapes, aliases, 3 * len(bufs), plan)


def _ex_swap_halves(grads):
    def plan(ins, outs, send_sems, recv_sems):
        x, y, c, _ = _place()
        sends = [(_half(ins[t], 1 - c, g.shape[1] // 2), outs[t], send_sems.at[t], recv_sems.at[t], (x, y, 1 - c))
                 for t, g in enumerate(grads)]
        return sends, sends

    shapes = [jax.ShapeDtypeStruct((g.shape[0], g.shape[1] // 2, g.shape[2]), g.dtype) for g in grads]
    return _Exchange(grads, shapes, {}, len(grads), plan)


def _ex_scatter(sums):
    def plan(ins, outs, send_sems, recv_sems):
        x, y, c, chips = _place()
        sends = [(ins[t].at[2 * px + py], outs[t].at[k], send_sems.at[3 * t + k], recv_sems.at[3 * t + k], (px, py, c))
                 for t in range(len(sums)) for k, (px, py) in enumerate(chips)]
        return sends, sends

    shapes = [jax.ShapeDtypeStruct((N_CHIPS - 1,) + s.shape[1:], s.dtype) for s in sums]
    return _Exchange(sums, shapes, {}, 3 * len(sums), plan)


def _ex_join_halves(fulls):
    def plan(ins, outs, send_sems, recv_sems):
        x, y, c, _ = _place()
        sends, arrivals = [], []
        for t, full in enumerate(fulls):
            rows = full.shape[0] // 2
            sems = (send_sems.at[t], recv_sems.at[t], (x, y, 1 - c))
            mine, other = _half(outs[t], c, rows), _half(outs[t], 1 - c, rows)
            sends.append((mine, mine, *sems))
            arrivals.append((other, other, *sems))
        return sends, arrivals

    shapes, aliases = _same(fulls)
    return _Exchange(fulls, shapes, aliases, len(fulls), plan)


class _Shifted:
    def __init__(self, sems, offset):
        self.sems, self.offset = sems, offset

    @property
    def at(self):
        return self

    def __getitem__(self, k):
        return self.sems.at[self.offset + k]


def _combine(exchanges):
    operands, out_shapes, aliases, starts = [], [], {}, []
    n_sems = 0
    for e in exchanges:
        starts.append((len(operands), len(out_shapes), n_sems))
        aliases.update({len(operands) + i: len(out_shapes) + o for i, o in e.aliases.items()})
        operands += list(e.operands)
        out_shapes += list(e.out_shapes)
        n_sems += e.n_sems

    def plan(ins, outs, send_sems, recv_sems):
        sends, arrivals = [], []
        for e, (i0, o0, s0) in zip(exchanges, starts):
            s, a = e.plan(ins[i0:i0 + len(e.operands)], outs[o0:o0 + len(e.out_shapes)],
                          _Shifted(send_sems, s0), _Shifted(recv_sems, s0))
            sends += s
            arrivals += a
        return sends, arrivals

    return _Exchange(operands, out_shapes, aliases, n_sems, plan)


class _Reduction:
    def __init__(self, grad, tag, c_arr, jc_arr):
        self.grad, self.tag, self.c_arr, self.jc_arr, self.stage = grad, tag, c_arr, jc_arr, 0

    def exchange(self):
        if self.stage == 0:
            return _ex_swap_halves([self.grad])
        if self.stage == 1:
            return _ex_scatter([self.sums])
        return _ex_join_halves([self.full])

    def advance(self, landed):
        if self.stage == 0:
            self.recv = landed
            self.sums = _chip_sum(self.grad, landed, self.c_arr, f"chip_sum_{self.tag}")
        elif self.stage == 1:
            self.full = _owner_sum(self.grad, self.recv, landed, self.jc_arr, f"owner_sum_{self.tag}")
        else:
            self.result = landed
        self.stage += 1


def _ride(reductions):
    def done(carried):
        for r, landed in zip(reductions, carried):
            r.advance(landed)

    return _combine([r.exchange() for r in reductions]), done


def _exchange_alone(ex, name):
    return _pallas(None, name=name, grid=(), in_specs=[], out_specs=[], out_shape=[], args=[], ex=ex)


SEM = pl.BlockSpec(memory_space=pltpu.SEMAPHORE)
DATAFLOW = pltpu.SideEffectType.DATAFLOW_SIDE_EFFECTING


def _exchange_start(ex, name):
    e_in, e_out = len(ex.operands), len(ex.out_shapes)
    kept = [i for i in range(e_in) if i not in ex.aliases]

    def body(*refs):
        ins, refs = refs[:e_in], refs[e_in:]
        outs, refs = refs[:e_out], refs[e_out:]
        _, (send_sems, recv_sems, token) = refs[:len(kept)], refs[len(kept):]
        sends, _ = ex.plan(ins, outs, send_sems, recv_sems)
        for cp in sends:
            _remote(*cp).start()
        token[...] = jnp.zeros(token.shape, F32)

    sems = pltpu.SemaphoreType.DMA((ex.n_sems,))
    aliases = dict(ex.aliases)
    aliases.update({i: e_out + k for k, i in enumerate(kept)})
    res = pl.pallas_call(
        body,
        name=name,
        in_specs=[ANY] * e_in,
        out_specs=[ANY] * (e_out + len(kept)) + [SEM, SEM, pl.BlockSpec(memory_space=pltpu.VMEM)],
        out_shape=list(ex.out_shapes) + [jax.ShapeDtypeStruct(ex.operands[i].shape, ex.operands[i].dtype) for i in kept]
        + [sems, sems, jax.ShapeDtypeStruct((SUBLANES, LANES), F32)],
        input_output_aliases=aliases,
        compiler_params=pltpu.CompilerParams(has_side_effects=DATAFLOW),
    )(*ex.operands)
    outs, kept_thru, (send_sems, recv_sems, token) = res[:e_out], res[e_out:e_out + len(kept)], res[e_out + len(kept):]
    operands = list(ex.operands)
    for i, o in ex.aliases.items():
        operands[i] = outs[o]
    for k, i in enumerate(kept):
        operands[i] = kept_thru[k]
    return (operands, outs, send_sems, recv_sems), token


def _exchange_wait(ex, state, after, name):
    operands, outs, send_sems, recv_sems = state
    e_out = len(outs)
    kept = [i for i in range(len(operands)) if i not in ex.aliases]

    def body(*refs):
        sources, refs = refs[:len(kept)], refs[len(kept):]
        landing, refs = refs[:e_out], refs[e_out:]
        ins = [None] * len(operands)
        for k, i in enumerate(kept):
            ins[i] = sources[k]
        sends, arrivals = ex.plan(ins, landing, refs[0], refs[1])
        for cp in arrivals:
            _remote(*cp).wait_recv()
        for cp in sends:
            _remote(*cp).wait_send()

    return pl.pallas_call(
        body,
        name=name,
        in_specs=[ANY] * (len(kept) + e_out) + [SEM, SEM] + [ANY] * len(after),
        out_specs=[ANY] * e_out,
        out_shape=[jax.ShapeDtypeStruct(o.shape, o.dtype) for o in outs],
        input_output_aliases={len(kept) + o: o for o in range(e_out)},
        compiler_params=pltpu.CompilerParams(has_side_effects=DATAFLOW),
    )(*[operands[i] for i in kept], *outs, send_sems, recv_sems, *after)


def _cast_to_slot(w, chip_arr, name, after=(), natural=False):
    A, B = w.shape
    ta = _tile(A, max(BF16_ROWS, (1 << 19) // B), BF16_ROWS)

    def body(j_ref, w_ref, *rest):
        rest[-1][...] = w_ref[...].astype(BF16)

    if natural:
        assert B % LANES == 0
        out_spec = pl.BlockSpec((ta, B), lambda i, j_ref: (i, j_ref[0]))
        out_shape = jax.ShapeDtypeStruct((A, N_CHIPS * B), BF16)
    else:
        out_spec = pl.BlockSpec((None, ta, B), lambda i, j_ref: (j_ref[0], i, 0))
        out_shape = jax.ShapeDtypeStruct((N_CHIPS, A, B), BF16)
    return pl.pallas_call(
        body,
        name=name,
        grid_spec=pltpu.PrefetchScalarGridSpec(
            num_scalar_prefetch=1,
            grid=(A // ta,),
            in_specs=[pl.BlockSpec((ta, B), lambda i, j_ref: (i, 0))] + [ANY] * len(after),
            out_specs=out_spec,
        ),
        out_shape=out_shape,
        compiler_params=_params(("parallel",)),
    )(chip_arr, w, *after)


def _chip_sum(grad, recv, c_arr, name):
    _, A, B = grad.shape
    hA = A // 2
    ta = _tile(hA, max(BF16_ROWS, (1 << 19) // B), BF16_ROWS)
    nh = hA // ta

    def body(c_ref, g_ref, r_ref, o_ref):
        o_ref[...] = (g_ref[...] + r_ref[...]).astype(BF16)

    return pl.pallas_call(
        body,
        name=name,
        grid_spec=pltpu.PrefetchScalarGridSpec(
            num_scalar_prefetch=1,
            grid=(N_CHIPS, nh),
            in_specs=[pl.BlockSpec((None, ta, B), lambda s, i, c_ref: (s, c_ref[0] * nh + i, 0)),
                      pl.BlockSpec((None, ta, B), lambda s, i, c_ref: (s, i, 0))],
            out_specs=pl.BlockSpec((None, ta, B), lambda s, i, c_ref: (s, i, 0)),
        ),
        out_shape=jax.ShapeDtypeStruct((N_CHIPS, hA, B), BF16),
        compiler_params=_params(("parallel", "parallel")),
    )(c_arr, grad, recv)


def _owner_sum(grad, recv, landed, jc_arr, name):
    _, A, B = grad.shape
    hA = A // 2
    ta = _tile(hA, max(BF16_ROWS, (1 << 19) // B), BF16_ROWS)
    nh = hA // ta

    def body(jc_ref, g_ref, r_ref, l0_ref, l1_ref, l2_ref, o_ref):
        total = g_ref[...] + r_ref[...]
        for ref in (l0_ref, l1_ref, l2_ref):
            total = total + ref[...].astype(F32)
        o_ref[...] = total

    def landed_spec(k):
        return pl.BlockSpec((None, ta, B), lambda i, jc_ref: (k, i, 0))

    return pl.pallas_call(
        body,
        name=name,
        grid_spec=pltpu.PrefetchScalarGridSpec(
            num_scalar_prefetch=1,
            grid=(nh,),
            in_specs=[pl.BlockSpec((None, ta, B), lambda i, jc_ref: (jc_ref[0], jc_ref[1] * nh + i, 0)),
                      pl.BlockSpec((None, ta, B), lambda i, jc_ref: (jc_ref[0], i, 0)),
                      landed_spec(0), landed_spec(1), landed_spec(2)],
            out_specs=pl.BlockSpec((ta, B), lambda i, jc_ref: (jc_ref[1] * nh + i, 0)),
        ),
        out_shape=jax.ShapeDtypeStruct((A, B), F32),
        compiler_params=_params(("parallel",)),
    )(jc_arr, grad, recv, landed, landed, landed)


def _pack(parts):
    rows = []
    for p in parts:
        flat = jnp.reshape(p.astype(F32), (-1,))
        tile = SUBLANES * LANES
        padded = -(-flat.shape[0] // tile) * tile
        rows.append(jnp.reshape(jnp.pad(flat, (0, padded - flat.shape[0])), (-1, LANES)))
    return jnp.concatenate(rows, axis=0)


def _unpack(pack, shapes):
    out, row = [], 0
    for shape in shapes:
        size = int(np.prod(shape))
        nrows = -(-size // (SUBLANES * LANES)) * SUBLANES
        out.append(jnp.reshape(jnp.reshape(pack[row:row + nrows], (-1,))[:size], shape))
        row += nrows
    return out


def _bias_tables():
    qi = np.arange(BLOCK)[:, None]
    kj = np.arange(2 * BLOCK)[None, :]
    dist = qi + BLOCK - kj
    in_window = (dist >= 0) & (dist < BLOCK)
    n = np.clip(dist, 0, None)
    max_exact = N_BUCKETS // 2
    nf = np.maximum(n, 1).astype(np.float32)
    large = max_exact + (np.log(nf / max_exact) / math.log(MAX_DISTANCE / max_exact)
                         * (N_BUCKETS - max_exact)).astype(np.int32)
    large = np.minimum(large, N_BUCKETS - 1)
    bucket = np.where(n < max_exact, n, large)
    onehot = (bucket[None] == np.arange(N_BUCKETS)[:, None, None]) & in_window[None]
    first = in_window & (kj >= BLOCK)
    return onehot.astype(np.float32), in_window, first


def kernel(x, c, w_ada, b_ada, g_ffn1, w1_ffn1, w3_ffn1, w2_ffn1, g_mix, w_in, spatial_w, spatial_b, g_v, g_q, g_k, sinks, rel_bias, w_out, g_ffn2, w1_ffn2, w3_ffn2, w2_ffn2, loss_target, m_w_ada, m_b_ada, m_g_ffn1, m_w1_ffn1, m_w3_ffn1, m_w2_ffn1, m_g_mix, m_w_in, m_spatial_w, m_spatial_b, m_g_v, m_g_q, m_g_k, m_sinks, m_rel_bias, m_w_out, m_g_ffn2, m_w1_ffn2, m_w3_ffn2, m_w2_ffn2, v_w_ada, v_b_ada, v_g_ffn1, v_w1_ffn1, v_w3_ffn1, v_w2_ffn1, v_g_mix, v_w_in, v_spatial_w, v_spatial_b, v_g_v, v_g_q, v_g_k, v_sinks, v_rel_bias, v_w_out, v_g_ffn2, v_w1_ffn2, v_w3_ffn2, v_w2_ffn2):
    ax, ay, ac = lax.axis_index("x"), lax.axis_index("y"), lax.axis_index("c")
    chip = 2 * ax + ay
    dev = 2 * chip + ac
    xs = x[0]
    tgt = loss_target[0]
    S, D = xs.shape
    F = N_CHIPS * w1_ffn1.shape[2]
    mod_cols = w_ada.shape[2]

    chip_arr = jnp.reshape(chip, (1,)).astype(jnp.int32)
    c_arr = jnp.reshape(ac, (1,)).astype(jnp.int32)
    jc_arr = jnp.stack([chip, ac]).astype(jnp.int32)

    c_all = _allgather_small(jnp.pad(c, ((0, SUBLANES - 1), (0, 0))), "gather_c")
    c_all = jnp.pad(c_all[::SUBLANES], ((0, BF16_ROWS - N_DEV), (0, 0)))
    b_sh = lax.dynamic_slice(b_ada, (0, chip * mod_cols), (1, mod_cols))
    mod_part, c_act = _mod_partial(c_all, w_ada[0], b_sh, "mod_partial")
    mod_all = _allgather_small(mod_part[:N_DEV], "gather_mod")
    mod_all = jnp.reshape(mod_all, (N_CHIPS, 2, N_DEV, mod_cols))[:, 0]
    mod = jnp.reshape(lax.dynamic_index_in_dim(mod_all, dev, axis=1, keepdims=False), (1, N_MOD * D))
    sh1, sc1, gt1, sh2, sc2, gt2, sh3, sc3, gt3 = [mod[:, i * D:(i + 1) * D] for i in range(N_MOD)]

    def cols_to_natural(w4):
        return jnp.reshape(jnp.transpose(w4, (1, 0, 2)), (w4.shape[1], -1))

    def cast(w, nm, after):
        return _cast_to_slot(w[0], chip_arr, f"cast_{nm}", after=after, natural=nm.startswith(("w1_", "w3_")))

    ffn1_bufs = [cast(w1_ffn1, "w1_ffn1", [mod]), cast(w3_ffn1, "w3_ffn1", [mod]), cast(w2_ffn1, "w2_ffn1", [mod])]
    near = _ex_gather_near(ffn1_bufs)
    state, started = _exchange_start(near, "gather_ffn1_near_start")
    mixer_bufs = [cast(w_in, "w_in", [started]), cast(w_out, "w_out", [started])]
    ffn2_bufs = [cast(w1_ffn2, "w1_ffn2", [started]), cast(w3_ffn2, "w3_ffn2", [started]),
                 cast(w2_ffn2, "w2_ffn2", [started])]
    ffn1_bufs = _exchange_wait(near, state, mixer_bufs + ffn2_bufs, "gather_ffn1_near_wait")
    relay = _ex_gather_relay(ffn1_bufs)
    state, started = _exchange_start(relay, "gather_ffn1_relay_start")
    h1 = _norm_mod(xs, g_ffn1, sh1, sc1, "ffn1_norm", after=[started])
    ffn1_bufs = _exchange_wait(relay, state, [h1], "gather_ffn1_relay_wait")
    ffn1_bufs = _exchange_alone(_ex_gather_d2d(ffn1_bufs, which=(2,)), "gather_ffn1_d2d")
    w1a, w3a, w2a = ffn1_bufs[0], ffn1_bufs[1], jnp.reshape(ffn1_bufs[2], (F, D))

    onehot_np, in_window_np, first_np = _bias_tables()
    onehot = jnp.asarray(onehot_np)
    bias = jnp.einsum("bij,bh->hij", onehot, rel_bias, precision=lax.Precision.HIGHEST)
    biasm = jnp.stack([jnp.where(jnp.asarray(first_np)[None], bias, NEG),
                       jnp.where(jnp.asarray(in_window_np)[None], bias, NEG)])
    causal = jnp.asarray(np.tril(np.ones((BLOCK, BLOCK), dtype=bool)))
    wm = jnp.where(causal[None], spatial_w[0], 0.0).astype(BF16)
    wm_t = jnp.transpose(wm, (0, 2, 1))
    sink_vec = sinks[0]
    per_group = PAIRS // KV_HEADS
    sbp = jnp.broadcast_to(jnp.reshape(spatial_b[0], (A_HEADS * BLOCK, 1)), (A_HEADS * BLOCK, A_DIM))
    gvp = jnp.repeat(g_v[0], BLOCK, axis=0)
    gq2, gk2 = jnp.concatenate([g_q, g_q], axis=1), jnp.concatenate([g_k, g_k], axis=1)
    seg_ones = jnp.asarray(np.kron(np.eye(2, dtype=np.float32), np.ones((HEAD_DIM, HEAD_DIM), np.float32)), BF16)
    lane_ones = jnp.full((LANES, LANES), 1.0 / LANES, BF16)
    pair_fold = jnp.asarray(np.kron(np.eye(per_group, LANES, dtype=np.float32), np.ones((BLOCK, 1), np.float32)))
    biasp = jnp.reshape(jnp.transpose(jnp.reshape(biasm, (2, KV_HEADS, per_group, 2, BLOCK, BAND)), (0, 1, 3, 5, 2, 4)),
                        (2, KV_HEADS, 2 * BAND, PAIR_ROWS))

    res = _ffn_fwd(xs, g_ffn1, sh1, sc1, gt1, w1a, w3a, w2a, None, "ffn1_fwd", ex=_ex_gather_ici(mixer_bufs + ffn2_bufs),
                   h=h1)
    (x1, a1, b1, f1), mixer_bufs, ffn2_bufs = res[:4], res[4:6], res[6:]
    h2, *mixer_bufs = _norm_mod(x1, g_mix, sh2, sc2, "mixer_norm", ex=_ex_gather_d2d(mixer_bufs))
    win, wout = cols_to_natural(mixer_bufs[0]), jnp.reshape(mixer_bufs[1], (-1, D))
    z, *ffn2_bufs = _matmul(h2, win, "nn", F32, 1024, _tile(IN_COLS, 1664), D, "mixer_in", ex=_ex_gather_d2d(ffn2_bufs))
    w1b, w3b, w2b = ffn2_bufs[0], ffn2_bufs[1], jnp.reshape(ffn2_bufs[2], (F, D))
    mix = _mixer_fwd(z, wm, sbp, gvp, gq2, gk2, seg_ones, lane_ones, sink_vec, biasp, "mixer_fwd")
    x2, ymix = _mixer_out(mix, wout, x1, gt2, "mixer_out")
    g3, df3, h3, a3, b3, dgt3, loss_sum = _ffn_fwd(x2, g_ffn2, sh3, sc3, gt3, w1b, w3b, w2b, tgt, "ffn2_fwd_loss")
    loss = lax.psum(loss_sum[0, 0] * (0.5 / D), ("x", "y", "c"))

    tk = _tile(S, 2048)

    def ffn_weight_grads(h, da, db, s, df, tag, riding):
        ex, done = _ride(riding) if riding else (None, None)
        gw1 = _matmul(h, da, "tn", F32, 1024, F // N_CHIPS, tk, f"grad_w1_{tag}", shard_major=True, ex=ex)
        if riding:
            done(gw1[1:])
            gw1 = gw1[0]
        r1 = _Reduction(gw1, f"w1_{tag}", c_arr, jc_arr)
        ex, done = _ride([r1])
        gw3, *carried = _matmul(h, db, "tn", F32, 1024, F // N_CHIPS, tk, f"grad_w3_{tag}", shard_major=True, ex=ex)
        done(carried)
        r3 = _Reduction(gw3, f"w3_{tag}", c_arr, jc_arr)
        ex, done = _ride([r1, r3])
        gw2, *carried = _matmul(s, df, "tn", F32, _tile(F, 1408), 1024, tk, f"grad_w2_{tag}", ex=ex)
        done(carried)
        r2 = _Reduction(jnp.reshape(gw2, (N_CHIPS, F // N_CHIPS, D)), f"w2_{tag}", c_arr, jc_arr)
        return r1, r3, r2

    da3, db3, s3, dh3 = _ffn_bwd(df3, a3, b3, w1b, w3b, w2b, "ffn2_bwd")
    r21, r23, r22 = ffn_weight_grads(h3, da3, db3, s3, df3, "ffn2", [])
    ex, done = _ride([r21, r22])
    state, started = _exchange_start(ex, "reduce_ffn2_start")
    g2, dsh3, dsc3, dgn3, dy, dgt2 = _norm_bwd(dh3, x2, g3, g_ffn2, sc3, (ymix, gt2, 1.0), "ffn2_norm_bwd",
                                               after=[started])
    dmix = _matmul(dy, wout, "nt", BF16, 1024, 2048, D, "mixer_out_bwd")
    done(_exchange_wait(ex, state, [dmix], "reduce_ffn2_wait"))
    ex, done = _ride([r23, r22])
    res = _mixer_bwd(z, dmix, wm, wm_t, sbp, gvp, gq2, gk2, seg_ones, lane_ones, sink_vec, biasp, pair_fold,
                     "mixer_bwd", ex=ex)
    dz_main, dz_kv, dwm, dsb, dgv, dgq, dgk, dsk, dst = res[:9]
    dsb = jnp.reshape(dsb[:, 0], (A_HEADS, BLOCK))
    dgq, dgk = dgq[:, :HEAD_DIM], dgk[:, :HEAD_DIM]
    dsk = jnp.reshape(jnp.transpose(jnp.reshape(dsk[:2 * KV_HEADS, :per_group], (KV_HEADS, 2, per_group)), (0, 2, 1)),
                      (1, B_HEADS))
    dst = jnp.reshape(jnp.transpose(jnp.reshape(dst, (KV_HEADS, 2, BAND, per_group, BLOCK)), (0, 3, 1, 4, 2)),
                      (B_HEADS, BLOCK * BAND))
    done(res[9:])
    dz = jnp.concatenate([dz_main, dz_kv], axis=1)
    ex, done = _ride([r23, r22])
    dh2, *carried = _matmul(dz, win, "nt", BF16, 1024, 2048, _tile(IN_COLS, 1664), "mixer_in_bwd", ex=ex)
    done(carried)
    drel = _bucket_sum(dst, jnp.reshape(onehot, (N_BUCKETS, -1)), "bucket_sum")
    g1, dsh2, dsc2, dgn2, df1, dgt1 = _norm_bwd(dh2, x1, g2, g_mix, sc2, (f1, gt1, 0.5), "mixer_norm_bwd")

    da1, db1, s1, dh1 = _ffn_bwd(df1, a1, b1, w1a, w3a, w2a, "ffn1_bwd")
    r11, r13, r12 = ffn_weight_grads(h1, da1, db1, s1, df1, "ffn1", [])
    ex, done = _ride([r11, r13, r12])
    gwin_full, *carried = _matmul(h2, dz, "tn", F32, 1024, _tile(IN_COLS, 1664), tk, "grad_w_in", ex=ex)
    done(carried)
    rm_in = _Reduction(jnp.transpose(jnp.reshape(gwin_full, (D, N_CHIPS, -1)), (1, 0, 2)), "w_in", c_arr, jc_arr)
    ex, done = _ride([r13, r12, rm_in])
    state, started = _exchange_start(ex, "reduce_late_start")
    gwout_full = _matmul(mix, dy, "tn", F32, 1024, 1024, tk, "grad_w_out", after=[started])
    grad_x, dsh1, dsc1, dgn1 = _norm_bwd(dh1, xs, g1, g_ffn1, sc1, None, "ffn1_norm_bwd", after=[started])
    done(_exchange_wait(ex, state, [gwout_full, grad_x], "reduce_late_wait"))
    rm_out = _Reduction(jnp.reshape(gwout_full, (N_CHIPS, -1, D)), "w_out", c_arr, jc_arr)

    dmod = jnp.concatenate([dsh1, dsc1, dgt1, dsh2, dsc2, dgt2, dsh3, dsc3, dgt3], axis=1)
    small_w = [b_ada, g_ffn1, g_mix, g_ffn2, spatial_w, spatial_b, g_v, g_q, g_k, sinks, rel_bias]
    small_m = [m_b_ada, m_g_ffn1, m_g_mix, m_g_ffn2, m_spatial_w, m_spatial_b, m_g_v, m_g_q, m_g_k, m_sinks, m_rel_bias]
    small_v = [v_b_ada, v_g_ffn1, v_g_mix, v_g_ffn2, v_spatial_w, v_spatial_b, v_g_v, v_g_q, v_g_k, v_sinks, v_rel_bias]
    small_g = [dmod, dgn1, dgn2, dgn3, jnp.where(causal[None], dwm, 0.0), dsb, dgv, dgq, dgk, dsk, drel]
    shapes = [w.shape for w in small_w]
    gpack = _pack(small_g)
    rows = gpack.shape[0]
    gall = jnp.reshape(_allgather_small(gpack, "gather_small"), (N_DEV, rows, LANES))
    sg, sd, sm, sv = _small_update(gall, _pack(small_w), _pack(small_m), _pack(small_v), "small_update")
    sg, sd, sm, sv = [_unpack(p, shapes) for p in (sg, sd, sm, sv)]

    mod_rows = -(-N_MOD * D // (SUBLANES * LANES)) * SUBLANES
    dmod_all = jnp.reshape(gall[:, :mod_rows], (N_DEV, -1))[:, :N_MOD * D]
    dmod_sh = lax.dynamic_slice(dmod_all, (0, chip * mod_cols), (N_DEV, mod_cols))
    dmod_sh = jnp.pad(dmod_sh, ((0, BF16_ROWS - N_DEV), (0, 0))).astype(BF16)
    g_wada = _matmul(c_act, dmod_sh, "tn", F32, 1024, _tile(mod_cols, 512), BF16_ROWS, "grad_w_ada")

    big = {}

    def update(nm, w, g, m, v, token):
        g_out, d, nm_, nv_ = _adamw(w[0], g, m[0], v[0], f"adamw_{nm}", emit_grad=True, after=[token])
        big[nm] = (g_out[None], d[None], nm_[None], nv_[None])
        return d

    ex, done = _ride([r12, rm_in, rm_out])
    state, token = _exchange_start(ex, "reduce_tail_0_start")
    behind = [update("w1_ffn2", w1_ffn2, r21.result, m_w1_ffn2, v_w1_ffn2, token),
              update("w3_ffn2", w3_ffn2, r23.result, m_w3_ffn2, v_w3_ffn2, token),
              update("w2_ffn2", w2_ffn2, r22.result, m_w2_ffn2, v_w2_ffn2, token)]
    done(_exchange_wait(ex, state, behind, "reduce_tail_0_wait"))
    ex, done = _ride([rm_in, rm_out])
    state, token = _exchange_start(ex, "reduce_tail_1_start")
    behind = [update("w1_ffn1", w1_ffn1, r11.result, m_w1_ffn1, v_w1_ffn1, token),
              update("w3_ffn1", w3_ffn1, r13.result, m_w3_ffn1, v_w3_ffn1, token),
              update("w2_ffn1", w2_ffn1, r12.result, m_w2_ffn1, v_w2_ffn1, token)]
    done(_exchange_wait(ex, state, behind, "reduce_tail_1_wait"))
    ex, done = _ride([rm_out])
    state, token = _exchange_start(ex, "reduce_tail_2_start")
    d_wada, nm_wada, nv_wada = _adamw(w_ada[0], g_wada, m_w_ada[0], v_w_ada[0], "adamw_w_ada", after=[token])
    behind = [d_wada, update("w_in", w_in, rm_in.result, m_w_in, v_w_in, token)]
    done(_exchange_wait(ex, state, behind, "reduce_tail_2_wait"))
    update("w_out", w_out, rm_out.result, m_w_out, v_w_out, token)
    big["w_ada"] = (g_wada[None], d_wada[None], nm_wada[None], nv_wada[None])

    order = ["w_ada", "b_ada", "g_ffn1", "w1_ffn1", "w3_ffn1", "w2_ffn1", "g_mix", "w_in", "spatial_w", "spatial_b",
             "g_v", "g_q", "g_k", "sinks", "rel_bias", "w_out", "g_ffn2", "w1_ffn2", "w3_ffn2", "w2_ffn2"]
    small_names = ["b_ada", "g_ffn1", "g_mix", "g_ffn2", "spatial_w", "spatial_b", "g_v", "g_q", "g_k", "sinks", "rel_bias"]
    for i, nm in enumerate(small_names):
        big[nm] = (sg[i], sd[i], sm[i], sv[i])
    outs = [loss, grad_x[None]]
    for kind in range(4):
        outs += [big[nm][kind] for nm in order]
    return tuple(outs)
```

```python
import functools
import math

import jax
import jax.numpy as jnp
import numpy as np
from jax import lax
from jax.experimental import pallas as pl
from jax.experimental.pallas import tpu as pltpu

F32 = jnp.float32
BF16 = jnp.bfloat16
MESH = pl.DeviceIdType.MESH
ANY = pl.BlockSpec(memory_space=pl.ANY)

EPS = 1e-6
BLOCK = 128
A_HEADS = 8
A_DIM = 128
A_WIDTH = A_HEADS * A_DIM
B_HEADS = 16
KV_HEADS = 2
GROUP = B_HEADS // KV_HEADS
HEAD_DIM = 64
B_WIDTH = B_HEADS * HEAD_DIM
KV_WIDTH = KV_HEADS * HEAD_DIM
Q_OFF = 2 * A_WIDTH
K_OFF = Q_OFF + B_WIDTH
V_OFF = K_OFF + KV_WIDTH
IN_COLS = V_OFF + KV_WIDTH
N_BUCKETS = 32
MAX_DISTANCE = 128
N_MOD = 9
N_CHIPS = 4
N_DEV = 8
NEG = -1e30

ADAM_LR = 0.001
ADAM_B1 = 0.9
ADAM_B2 = 0.999
ADAM_EPS = 1e-08
ADAM_WD = 0.01
ADAM_STEP = 10

LANES = 128
SUBLANES = 8
BF16_ROWS = 16
VMEM_LIMIT = 60 * 1024 * 1024

INV_SQRT2 = 1.0 / math.sqrt(2.0)
INV_SQRT_2PI = 1.0 / math.sqrt(2.0 * math.pi)


def _tile(n, pref, mult=LANES):
    t = (min(pref, n) // mult) * mult
    while t >= mult:
        if n % t == 0:
            return t
        t -= mult
    return n


def _params(sem):
    return pltpu.CompilerParams(dimension_semantics=sem, vmem_limit_bytes=VMEM_LIMIT)


class _Exchange:
    def __init__(self, operands, out_shapes, aliases, n_sems, plan):
        self.operands, self.out_shapes, self.aliases, self.n_sems, self.plan = operands, out_shapes, aliases, n_sems, plan


def _pallas(body, *, name, grid, in_specs, out_specs, out_shape, args, scratch_shapes=(), semantics=None, ex=None,
            after=()):
    if ex is None:
        n_in = len(in_specs)

        def ordered(*refs):
            body(*refs[:n_in], *refs[n_in + len(after):])

        return pl.pallas_call(ordered if after else body, name=name, grid=grid,
                              in_specs=list(in_specs) + [ANY] * len(after), out_specs=out_specs, out_shape=out_shape,
                              scratch_shapes=list(scratch_shapes), compiler_params=_params(semantics))(*args, *after)
    assert not after
    n_in, n_out, n_scr = len(in_specs), len(out_specs), len(scratch_shapes)
    e_in, e_out = len(ex.operands), len(ex.out_shapes)

    def wrapped(*refs):
        ins, refs = refs[:n_in], refs[n_in:]
        ex_ins, refs = refs[:e_in], refs[e_in:]
        outs, refs = refs[:n_out], refs[n_out:]
        ex_outs, refs = refs[:e_out], refs[e_out:]
        scratch, (send_sems, recv_sems) = refs[:n_scr], refs[n_scr:]
        first, last = True, True
        for d, size in enumerate(grid):
            first = jnp.logical_and(first, pl.program_id(d) == 0)
            last = jnp.logical_and(last, pl.program_id(d) == size - 1)

        def start():
            sends, _ = ex.plan(ex_ins, ex_outs, send_sems, recv_sems)
            for cp in sends:
                _remote(*cp).start()

        def finish():
            sends, arrivals = ex.plan(ex_ins, ex_outs, send_sems, recv_sems)
            for cp in arrivals:
                _remote(*cp).wait_recv()
            for cp in sends:
                _remote(*cp).wait_send()

        if grid:
            pl.when(first)(start)
        else:
            start()
        if body is not None:
            body(*ins, *outs, *scratch)
        if grid:
            pl.when(last)(finish)
        else:
            finish()

    kwargs = dict(grid=grid) if grid else {}
    return pl.pallas_call(
        wrapped,
        name=name,
        in_specs=list(in_specs) + [ANY] * e_in,
        out_specs=list(out_specs) + [ANY] * e_out,
        out_shape=list(out_shape) + list(ex.out_shapes),
        input_output_aliases={n_in + i: n_out + o for i, o in ex.aliases.items()},
        scratch_shapes=list(scratch_shapes) + [pltpu.SemaphoreType.DMA((ex.n_sems,)), pltpu.SemaphoreType.DMA((ex.n_sems,))],
        compiler_params=_params(("arbitrary",) * len(grid) if grid else None),
        **kwargs,
    )(*args, *ex.operands)


def _dot(a, b, dims=(((1,), (0,)), ((), ()))):
    return lax.dot_general(a, b, dims, preferred_element_type=F32)


NN = (((1,), (0,)), ((), ()))
NT = (((1,), (1,)), ((), ()))
TN = (((0,), (0,)), ((), ()))


def _sigmoid(x):
    return 1.0 / (1.0 + jnp.exp(-x))


def _gelu_and_grad(x):
    cdf = 0.5 * (1.0 + lax.erf(x * INV_SQRT2))
    pdf = jnp.exp(-0.5 * x * x) * INV_SQRT_2PI
    return x * cdf, cdf + x * pdf


def _gelu(x):
    return x * (0.5 * (1.0 + lax.erf(x * INV_SQRT2)))


def _rms(x):
    r = lax.rsqrt(jnp.mean(x * x, axis=-1, keepdims=True) + EPS)
    return x * r, r


ROW_CHUNK = 64


def _for_rows(tm, fn):
    rc = min(ROW_CHUNK, tm)

    def step(r, carry):
        fn(pl.ds(pl.multiple_of(r * rc, rc), rc))
        return carry

    lax.fori_loop(0, tm // rc, step, 0)


def _rms_bwd(dy, xhat, r):
    return r * (dy - xhat * jnp.mean(dy * xhat, axis=-1, keepdims=True))


def _matmul(a, b, mode, out_dtype, tm, tn, tk, name, shard_major=False, ex=None, after=()):
    if mode == "nn":
        (M, K), N = a.shape, b.shape[1]
    elif mode == "nt":
        (M, K), N = a.shape, b.shape[0]
    else:
        (K, M), N = a.shape, b.shape[1]
    tm, tn, tk = min(tm, M), min(tn, N), min(tk, K)
    assert M % tm == 0 and N % tn == 0 and K % tk == 0, (name, M, N, K, tm, tn, tk)
    nk = K // tk
    dims = {"nn": NN, "nt": NT, "tn": TN}[mode]
    a_spec = pl.BlockSpec((tk, tm), lambda i, j, k: (k, i)) if mode == "tn" else pl.BlockSpec((tm, tk), lambda i, j, k: (i, k))
    b_spec = pl.BlockSpec((tn, tk), lambda i, j, k: (j, k)) if mode == "nt" else pl.BlockSpec((tk, tn), lambda i, j, k: (k, j))
    if shard_major:
        assert tn * N_CHIPS == N
        out_shape = jax.ShapeDtypeStruct((N_CHIPS, M, tn), out_dtype)
        o_spec = pl.BlockSpec((None, tm, tn), lambda i, j, k: (j, i, 0))
    else:
        out_shape = jax.ShapeDtypeStruct((M, N), out_dtype)
        o_spec = pl.BlockSpec((tm, tn), lambda i, j, k: (i, j))

    direct = nk == 1 or out_dtype == F32

    def body(a_ref, b_ref, o_ref, *scratch):
        k = pl.program_id(2)
        if nk == 1:
            o_ref[...] = _dot(a_ref[...], b_ref[...], dims).astype(o_ref.dtype)
            return
        acc_ref = o_ref if direct else scratch[0]

        @pl.when(k == 0)
        def _():
            acc_ref[...] = _dot(a_ref[...], b_ref[...], dims)

        @pl.when(k > 0)
        def _():
            acc_ref[...] += _dot(a_ref[...], b_ref[...], dims)

        if not direct:
            @pl.when(k == nk - 1)
            def _():
                o_ref[...] = acc_ref[...].astype(o_ref.dtype)

    outs = _pallas(body, name=name, grid=(M // tm, N // tn, nk), in_specs=[a_spec, b_spec], out_specs=[o_spec],
                   out_shape=[out_shape], scratch_shapes=[] if direct else [pltpu.VMEM((tm, tn), F32)],
                   semantics=("parallel", "parallel", "arbitrary"), args=[a, b], ex=ex, after=after)
    return outs[0] if ex is None else outs


def _mod_partial(c_all, w_ada, b_sh, name):
    R, D = c_all.shape
    N = w_ada.shape[1]
    tn = _tile(N, 512)

    def body(c_ref, w_ref, b_ref, o_ref, ca_ref):
        cv = c_ref[...]
        ca = (cv * _sigmoid(cv)).astype(BF16)
        ca_ref[...] = ca
        o_ref[...] = _dot(ca, w_ref[...].astype(BF16)) + b_ref[...]

    return pl.pallas_call(
        body,
        name=name,
        grid=(N // tn,),
        in_specs=[
            pl.BlockSpec((R, D), lambda j: (0, 0)),
            pl.BlockSpec((D, tn), lambda j: (0, j)),
            pl.BlockSpec((1, tn), lambda j: (0, j)),
        ],
        out_specs=[pl.BlockSpec((R, tn), lambda j: (0, j)), pl.BlockSpec((R, D), lambda j: (0, 0))],
        out_shape=[jax.ShapeDtypeStruct((R, N), F32), jax.ShapeDtypeStruct((R, D), BF16)],
        compiler_params=_params(("arbitrary",)),
    )(c_all, w_ada, b_sh)


FFN_BLOCK = 1024


def _ffn_blocks(F):
    if F % FFN_BLOCK == 0 or F < FFN_BLOCK:
        tf = _tile(F, FFN_BLOCK)
        return tf, F // tf, tf
    nj = -(-F // FFN_BLOCK)
    tail = F - (nj - 1) * FFN_BLOCK
    assert tail % LANES == 0
    return FFN_BLOCK, nj, tail
def _ffn_steps(j, nj, tf, tail, columns):
    if nj == 1:
        columns(tail, True)()
        return
    pl.when(j == 0)(columns(tf, True))
    if nj > 2:
        pl.when(jnp.logical_and(j > 0, j < nj - 1))(columns(tf, False))
    pl.when(j == nj - 1)(columns(tail, False))


def _ffn_fwd(x, g, sh, sc, gt, w1, w3, w2, tgt, name, ex=None, h=None):
    S, D = x.shape
    F = w1.shape[1]
    tm, tf, nj, tail = _tile(S, 512), *_ffn_blocks(F)
    ni = S // tm
    with_loss = tgt is not None
    assert h is None or not with_loss

    def body(*refs):
        if with_loss:
            (x_ref, g_ref, sh_ref, sc_ref, gt_ref, w1_ref, w3_ref, w2_ref, tgt_ref,
             gout_ref, df_ref, h_ref, a_ref, b_ref, dgt_ref, loss_ref, acc_ref) = refs
        elif h is None:
            (x_ref, g_ref, sh_ref, sc_ref, gt_ref, w1_ref, w3_ref, w2_ref,
             xo_ref, h_ref, a_ref, b_ref, f_ref, acc_ref) = refs
        else:
            (x_ref, g_ref, sh_ref, sc_ref, gt_ref, w1_ref, w3_ref, w2_ref, h_ref,
             xo_ref, a_ref, b_ref, f_ref, acc_ref) = refs
        i, j = pl.program_id(0), pl.program_id(1)

        if h is None:
            @pl.when(j == 0)
            def _():
                def prologue(rows):
                    xhat, _ = _rms(x_ref[rows, :])
                    h_ref[rows, :] = ((xhat * g_ref[...]) * (1.0 + sc_ref[...]) + sh_ref[...]).astype(BF16)

                _for_rows(tm, prologue)

        def columns(width, first):
            def run():
                hb = h_ref[...]
                av = _dot(hb, w1_ref[:, :width])
                bv = _dot(hb, w3_ref[:, :width])
                a_ref[:, :width] = av.astype(BF16)
                b_ref[:, :width] = bv.astype(BF16)
                sv = ((av * _sigmoid(av)) * bv).astype(BF16)
                if first:
                    acc_ref[...] = _dot(sv, w2_ref[:width, :])
                else:
                    acc_ref[...] += _dot(sv, w2_ref[:width, :])
            return run

        _ffn_steps(j, nj, tf, tail, columns)

        @pl.when(j == nj - 1)
        def _():
            if with_loss:
                @pl.when(i == 0)
                def _():
                    dgt_ref[...] = jnp.zeros(dgt_ref.shape, F32)
                    loss_ref[...] = jnp.zeros(loss_ref.shape, F32)

            def epilogue(rows):
                fv = acc_ref[rows, :]
                half_gate = 0.5 * gt_ref[...]
                xo = x_ref[rows, :] + half_gate * fv
                if not with_loss:
                    xo_ref[rows, :] = xo
                    f_ref[rows, :] = fv.astype(f_ref.dtype)
                    return
                err = xo - tgt_ref[rows, :]
                gout = err * (1.0 / D)
                gout_ref[rows, :] = gout
                df_ref[rows, :] = (half_gate * gout).astype(BF16)
                dgt_ref[...] += 0.5 * jnp.sum(gout * fv, axis=0, keepdims=True)
                loss_part = jnp.sum(jnp.sum(err * err, axis=1, keepdims=True), axis=0, keepdims=True)
                loss_ref[...] += jnp.broadcast_to(loss_part, loss_ref.shape)

            _for_rows(tm, epilogue)

    row = pl.BlockSpec((tm, D), lambda i, j: (i, 0))
    row_in = pl.BlockSpec((tm, D), lambda i, j: (i, 0), pipeline_mode=pl.Buffered(1))
    vec = pl.BlockSpec((1, D), lambda i, j: (0, 0))
    col = pl.BlockSpec((tm, tf), lambda i, j: (i, j))
    in_specs = [row_in, vec, vec, vec, vec,
                pl.BlockSpec((D, tf), lambda i, j: (0, j)),
                pl.BlockSpec((D, tf), lambda i, j: (0, j)),
                pl.BlockSpec((tf, D), lambda i, j: (j, 0))]
    args = [x, g, sh, sc, gt, w1, w3, w2]
    act = jax.ShapeDtypeStruct((S, F), BF16)
    if with_loss:
        in_specs.append(row_in)
        args.append(tgt)
        out_specs = [row, row, row_in, col, col, vec, pl.BlockSpec((1, LANES), lambda i, j: (0, 0))]
        out_shape = [jax.ShapeDtypeStruct((S, D), F32), jax.ShapeDtypeStruct((S, D), BF16),
                     jax.ShapeDtypeStruct((S, D), BF16), act, act,
                     jax.ShapeDtypeStruct((1, D), F32), jax.ShapeDtypeStruct((1, LANES), F32)]
    elif h is None:
        out_specs = [row, row, col, col, row]
        out_shape = [jax.ShapeDtypeStruct((S, D), F32), jax.ShapeDtypeStruct((S, D), BF16), act, act,
                     jax.ShapeDtypeStruct((S, D), BF16)]
    else:
        in_specs.append(row_in)
        args.append(h)
        out_specs = [row, col, col, row]
        out_shape = [jax.ShapeDtypeStruct((S, D), F32), act, act, jax.ShapeDtypeStruct((S, D), BF16)]
    return _pallas(body, name=name, grid=(ni, nj), in_specs=in_specs, out_specs=out_specs, out_shape=out_shape,
                   scratch_shapes=[pltpu.VMEM((tm, D), F32)],
                   semantics=("arbitrary", "arbitrary"), args=args, ex=ex)


def _ffn_bwd(df, a, b, w1, w3, w2, name, ex=None):
    S, D = df.shape
    F = a.shape[1]
    tm, tf, nj, tail = _tile(S, 512), *_ffn_blocks(F)

    def body(df_ref, a_ref, b_ref, w1_ref, w3_ref, w2_ref, da_ref, db_ref, s_ref, dh_out_ref, dh_ref):
        j = pl.program_id(1)

        def columns(width, first):
            def run():
                ds = _dot(df_ref[...], w2_ref[:width, :], NT)
                av = a_ref[:, :width].astype(F32)
                bv = b_ref[:, :width].astype(F32)
                sig = _sigmoid(av)
                sil = av * sig
                da = ((ds * bv) * (sig * (1.0 + av * (1.0 - sig)))).astype(BF16)
                db = (ds * sil).astype(BF16)
                da_ref[:, :width] = da
                db_ref[:, :width] = db
                s_ref[:, :width] = (sil * bv).astype(BF16)
                part = _dot(da, w1_ref[:, :width], NT) + _dot(db, w3_ref[:, :width], NT)
                if first:
                    dh_ref[...] = part
                else:
                    dh_ref[...] += part
            return run

        _ffn_steps(j, nj, tf, tail, columns)

        @pl.when(j == nj - 1)
        def _():
            dh_out_ref[...] = dh_ref[...].astype(BF16)

    row = pl.BlockSpec((tm, D), lambda i, j: (i, 0))
    col = pl.BlockSpec((tm, tf), lambda i, j: (i, j))
    act = jax.ShapeDtypeStruct((S, F), BF16)
    return _pallas(body, name=name, grid=(S // tm, nj), scratch_shapes=[pltpu.VMEM((tm, D), F32)],
                   in_specs=[row, col, col,
                             pl.BlockSpec((D, tf), lambda i, j: (0, j)),
                             pl.BlockSpec((D, tf), lambda i, j: (0, j)),
                             pl.BlockSpec((tf, D), lambda i, j: (j, 0))],
                   out_specs=[col, col, col, row],
                   out_shape=[act, act, act, jax.ShapeDtypeStruct((S, D), BF16)],
                   semantics=("parallel", "arbitrary"), args=[df, a, b, w1, w3, w2], ex=ex)


def _norm_mod(x, g, sh, sc, name, ex=None, after=()):
    S, D = x.shape
    tm = _tile(S, 512)

    def body(x_ref, g_ref, sh_ref, sc_ref, h_ref):
        def step(rows):
            xhat, _ = _rms(x_ref[rows, :])
            h_ref[rows, :] = ((xhat * g_ref[...]) * (1.0 + sc_ref[...]) + sh_ref[...]).astype(BF16)

        _for_rows(tm, step)

    row = pl.BlockSpec((tm, D), lambda i: (i, 0))
    vec = pl.BlockSpec((1, D), lambda i: (0, 0))
    outs = _pallas(body, name=name, grid=(S // tm,), in_specs=[row, vec, vec, vec], out_specs=[row],
                   out_shape=[jax.ShapeDtypeStruct((S, D), BF16)], semantics=("parallel",), args=[x, g, sh, sc], ex=ex,
                   after=after)
    return outs[0] if ex is None else outs


def _norm_bwd(dh, x, gres, g, sc, prev, name, ex=None, after=()):
    S, D = x.shape
    tm = _tile(S, 256)
    has_prev = prev is not None
    coef = prev[2] if has_prev else None

    def body(*refs):
        if has_prev:
            (dh_ref, x_ref, gr_ref, g_ref, sc_ref, f_ref, gt_ref,
             go_ref, dsh_ref, dsc_ref, dg_ref, dp_ref, dgt_ref) = refs
        else:
            dh_ref, x_ref, gr_ref, g_ref, sc_ref, go_ref, dsh_ref, dsc_ref, dg_ref = refs
        sum_refs = [dsh_ref, dsc_ref, dg_ref] + ([dgt_ref] if has_prev else [])

        @pl.when(pl.program_id(0) == 0)
        def _():
            for ref in sum_refs:
                ref[...] = jnp.zeros(ref.shape, F32)

        def step(rows):
            dh = dh_ref[rows, :].astype(F32)
            xhat, r = _rms(x_ref[rows, :])
            gain = g_ref[...]
            scale1 = 1.0 + sc_ref[...]
            gout = gr_ref[rows, :] + _rms_bwd(dh * scale1 * gain, xhat, r)
            go_ref[rows, :] = gout
            sums = [dh, dh * (xhat * gain), dh * scale1 * xhat]
            if has_prev:
                dp_ref[rows, :] = ((coef * gt_ref[...]) * gout).astype(BF16)
                sums.append(coef * (gout * f_ref[rows, :].astype(F32)))
            for ref, v in zip(sum_refs, sums):
                ref[...] += jnp.sum(v, axis=0, keepdims=True)

        _for_rows(tm, step)

    row = pl.BlockSpec((tm, D), lambda i: (i, 0))
    vec = pl.BlockSpec((1, D), lambda i: (0, 0))
    vshape = jax.ShapeDtypeStruct((1, D), F32)
    in_specs = [row, row, row, vec, vec]
    args = [dh, x, gres, g, sc]
    out_specs = [row, vec, vec, vec]
    out_shape = [jax.ShapeDtypeStruct((S, D), F32), vshape, vshape, vshape]
    if has_prev:
        in_specs += [row, vec]
        args += [prev[0], prev[1]]
        out_specs += [row, vec]
        out_shape += [jax.ShapeDtypeStruct((S, D), BF16), vshape]
    return _pallas(body, name=name, grid=(S // tm,), in_specs=in_specs, out_specs=out_specs, out_shape=out_shape,
                   semantics=("arbitrary",), args=args, ex=ex, after=after)


PAIRS = B_HEADS // 2
PAIR_ROWS = (PAIRS // KV_HEADS) * BLOCK
BAND = 2 * BLOCK


def _stack(ref, offset, count):
    return jnp.concatenate([ref[:, offset + p * LANES:offset + (p + 1) * LANES] for p in range(count)], axis=0)


def _seg_mean(x, e_ref):
    return _dot(x.astype(BF16), e_ref[...]) * (1.0 / HEAD_DIM)


def _block_diag(x, x_rolled, left, kv_head):
    if kv_head == 0:
        top, bottom = jnp.where(left, x, 0.0), jnp.where(left, 0.0, x_rolled)
    else:
        top, bottom = jnp.where(left, x_rolled, 0.0), jnp.where(left, 0.0, x)
    return jnp.concatenate([top, bottom], axis=0).astype(BF16)


def _from_block_diag(g, left, kv_head):
    a, b = g[:BAND], g[BAND:]
    if kv_head == 0:
        return jnp.where(left, a + pltpu.roll(b, HEAD_DIM, 1), 0.0)
    return jnp.where(left, 0.0, pltpu.roll(a, HEAD_DIM, 1) + b)


def _pair_softmax(st, sk_ref, kv_head):
    out = []
    for e in range(2):
        seg = st[e * BAND:(e + 1) * BAND]
        sink = jnp.concatenate([jnp.full((1, BLOCK), sk_ref[kv_head * GROUP + 2 * p + e], F32)
                                for p in range(PAIRS // KV_HEADS)], axis=1)
        m = jnp.maximum(jnp.max(seg, axis=0, keepdims=True), sink)
        p_ = jnp.exp(seg - m)
        e_sink = jnp.exp(sink - m)
        inv = 1.0 / (jnp.sum(p_, axis=0, keepdims=True) + e_sink)
        out.append((p_ * inv, e_sink * inv))
    return out


def _lane_mean(x, ones_ref):
    return _dot(x.astype(BF16), ones_ref[...])


def _mixer_specs(nb, last):
    full = lambda shape: pl.BlockSpec(shape, lambda n: (0,) * len(shape))
    z_spec = pl.BlockSpec((BLOCK, IN_COLS), lambda n: (jnp.minimum(n, last), 0))
    zp_spec = pl.BlockSpec((BLOCK, 2 * KV_WIDTH), lambda n: (jnp.clip(n - 1, 0, last), K_OFF // (2 * KV_WIDTH)))
    consts = [full((A_HEADS * BLOCK, A_DIM)), full((A_HEADS * BLOCK, A_DIM)), full((1, LANES)), full((1, LANES)),
              full((LANES, LANES)), full((LANES, LANES)), pl.BlockSpec(memory_space=pltpu.SMEM),
              pl.BlockSpec((None, KV_HEADS, PAIR_ROWS, 2 * BAND), lambda n: (jnp.minimum(n, 1), 0, 0, 0))]
    return full, z_spec, zp_spec, consts


def _mixer_fwd(z, wm, sbp, gvp, gq2, gk2, seg_ones, lane_ones, sinks, biasp, name):
    S = z.shape[0]
    nb = S // BLOCK

    def body(z_ref, zp_ref, wm_ref, sbp_ref, gvp_ref, gq2_ref, gk2_ref, e_ref, l_ref, sk_ref, bias_ref, mix_ref):
        u = _gelu(_stack(z_ref, 0, A_HEADS))
        v = _gelu(_stack(z_ref, A_WIDTH, A_HEADS))
        vhat = v * lax.rsqrt(_lane_mean(v * v, l_ref) + EPS)
        vn = (vhat * gvp_ref[...]).astype(BF16)
        mixed = jnp.concatenate([_dot(wm_ref[h], vn[h * BLOCK:(h + 1) * BLOCK]) for h in range(A_HEADS)], axis=0)
        ya = (u * (mixed + sbp_ref[...])).astype(BF16)
        for h in range(A_HEADS):
            mix_ref[:, h * A_DIM:(h + 1) * A_DIM] = ya[h * BLOCK:(h + 1) * BLOCK]

        left = lax.broadcasted_iota(jnp.int32, (1, LANES), 1) < HEAD_DIM
        kv = jnp.concatenate([zp_ref[...], z_ref[:, K_OFF:K_OFF + 2 * KV_WIDTH]], axis=0)
        k2, v2 = kv[:, :KV_WIDTH], kv[:, KV_WIDTH:]
        kn2 = k2 * lax.rsqrt(_seg_mean(k2 * k2, e_ref) + EPS) * gk2_ref[...]
        kn2_r, v2_r = pltpu.roll(kn2, HEAD_DIM, 1), pltpu.roll(v2, HEAD_DIM, 1)
        qp = _stack(z_ref, Q_OFF, PAIRS)
        qn = (qp * lax.rsqrt(_seg_mean(qp * qp, e_ref) + EPS) * gq2_ref[...]).astype(BF16)
        for kh in range(KV_HEADS):
            kbd, vbd = _block_diag(kn2, kn2_r, left, kh), _block_diag(v2, v2_r, left, kh)
            st = _dot(kbd, qn[kh * PAIR_ROWS:(kh + 1) * PAIR_ROWS], NT) * (HEAD_DIM ** -0.5) + bias_ref[kh]
            wt = jnp.concatenate([w_e for w_e, _ in _pair_softmax(st, sk_ref, kh)], axis=0).astype(BF16)
            o = _dot(wt, vbd, TN).astype(BF16)
            for p in range(PAIRS // KV_HEADS):
                col = A_WIDTH + (kh * (PAIRS // KV_HEADS) + p) * LANES
                mix_ref[:, col:col + LANES] = o[p * BLOCK:(p + 1) * BLOCK]

    full, z_spec, zp_spec, consts = _mixer_specs(nb, nb - 1)
    return pl.pallas_call(
        body,
        name=name,
        grid=(nb,),
        in_specs=[z_spec, zp_spec, full((A_HEADS, BLOCK, BLOCK))] + consts,
        out_specs=pl.BlockSpec((BLOCK, A_WIDTH + B_WIDTH), lambda n: (n, 0)),
        out_shape=jax.ShapeDtypeStruct((S, A_WIDTH + B_WIDTH), BF16),
        compiler_params=_params(("parallel",)),
    )(z, z, wm, sbp, gvp, gq2, gk2, seg_ones, lane_ones, sinks, biasp)


def _mixer_bwd(z, dmix, wm, wm_t, sbp, gvp, gq2, gk2, seg_ones, lane_ones, sinks, biasp, pair_fold, name, ex=None):
    S = z.shape[0]
    nb = S // BLOCK

    def body(z_ref, zp_ref, dmix_ref, wm_ref, wmt_ref, sbp_ref, gvp_ref, gq2_ref, gk2_ref, e_ref, l_ref, sk_ref,
             bias_ref, fold_ref,
             dz_ref, dzkv_ref, dwm_ref, dsb_ref, dgv_ref, dgq_ref, dgk_ref, dsk_ref, dst_ref,
             carry_ref, tot_ref, sbacc_ref, skacc_ref, gqacc_ref, gkacc_ref):
        n = pl.program_id(0)
        left = lax.broadcasted_iota(jnp.int32, (1, LANES), 1) < HEAD_DIM

        @pl.when(n == 0)
        def _():
            for ref in (dwm_ref, dgv_ref, dst_ref, carry_ref, sbacc_ref, skacc_ref, gqacc_ref, gkacc_ref):
                ref[...] = jnp.zeros(ref.shape, ref.dtype)

        @pl.when(n < nb)
        def _():
            u, du_dz = _gelu_and_grad(_stack(z_ref, 0, A_HEADS))
            v, dv_dz = _gelu_and_grad(_stack(z_ref, A_WIDTH, A_HEADS))
            rv = lax.rsqrt(_lane_mean(v * v, l_ref) + EPS)
            vhat = v * rv
            gvp = gvp_ref[...]
            vn = (vhat * gvp).astype(BF16)
            rows = lambda a, h: a[h * BLOCK:(h + 1) * BLOCK]
            mixed = jnp.concatenate([_dot(wm_ref[h], rows(vn, h)) for h in range(A_HEADS)], axis=0) + sbp_ref[...]
            dya = _stack(dmix_ref, 0, A_HEADS).astype(F32)
            dmx = dya * u
            sbacc_ref[...] += dmx
            dmx_b = dmx.astype(BF16)
            for h in range(A_HEADS):
                dwm_ref[h] += _dot(rows(dmx_b, h), rows(vn, h), NT)
            dvn = jnp.concatenate([_dot(wmt_ref[h], rows(dmx_b, h)) for h in range(A_HEADS)], axis=0)
            dgv_ref[...] += jnp.sum(jnp.reshape(dvn * vhat, (A_HEADS, BLOCK, A_DIM)), axis=1)
            dzu = ((dya * mixed) * du_dz).astype(BF16)
            tv = dvn * gvp
            dzv = ((rv * (tv - vhat * _lane_mean(tv * vhat, l_ref))) * dv_dz).astype(BF16)
            for h in range(A_HEADS):
                dz_ref[:, h * A_DIM:(h + 1) * A_DIM] = rows(dzu, h)
                dz_ref[:, A_WIDTH + h * A_DIM:A_WIDTH + (h + 1) * A_DIM] = rows(dzv, h)

            kv = jnp.concatenate([zp_ref[...], z_ref[:, K_OFF:K_OFF + 2 * KV_WIDTH]], axis=0)
            k2, v2 = kv[:, :KV_WIDTH], kv[:, KV_WIDTH:]
            kn2 = k2 * lax.rsqrt(_seg_mean(k2 * k2, e_ref) + EPS) * gk2_ref[...]
            kn2_r, v2_r = pltpu.roll(kn2, HEAD_DIM, 1), pltpu.roll(v2, HEAD_DIM, 1)
            qp = _stack(z_ref, Q_OFF, PAIRS)
            rq = lax.rsqrt(_seg_mean(qp * qp, e_ref) + EPS)
            qhat = qp * rq
            gq2 = gq2_ref[...]
            qn = (qhat * gq2).astype(BF16)
            dop = _stack(dmix_ref, A_WIDTH, PAIRS)
            dqn_parts = []
            dk2n = jnp.zeros((BAND, KV_WIDTH), F32)
            dv2 = jnp.zeros((BAND, KV_WIDTH), F32)
            for kh in range(KV_HEADS):
                kbd, vbd = _block_diag(kn2, kn2_r, left, kh), _block_diag(v2, v2_r, left, kh)
                qg = qn[kh * PAIR_ROWS:(kh + 1) * PAIR_ROWS]
                dog = dop[kh * PAIR_ROWS:(kh + 1) * PAIR_ROWS]
                st = _dot(kbd, qg, NT) * (HEAD_DIM ** -0.5) + bias_ref[kh]
                halves = _pair_softmax(st, sk_ref, kh)
                dpt = _dot(vbd, dog, NT)
                ds_halves, t_halves = [], []
                for e, (w_e, w_sink) in enumerate(halves):
                    dp_e = dpt[e * BAND:(e + 1) * BAND]
                    delta = jnp.sum(w_e * dp_e, axis=0, keepdims=True)
                    ds_halves.append(w_e * (dp_e - delta))
                    t_halves.append(-(w_sink * delta))
                dst = jnp.concatenate(ds_halves, axis=0)
                dst_ref[kh] += dst
                skacc_ref[2 * kh:2 * kh + 2, :] += jnp.concatenate(t_halves, axis=0)
                ds_b = (dst * (HEAD_DIM ** -0.5)).astype(BF16)
                w_b = jnp.concatenate([w_e for w_e, _ in halves], axis=0).astype(BF16)
                dqn_parts.append(_dot(ds_b, kbd, TN))
                dk2n += _from_block_diag(_dot(ds_b, qg), left, kh)
                dv2 += _from_block_diag(_dot(w_b, dog), left, kh)
            dqn = jnp.concatenate(dqn_parts, axis=0)
            gqacc_ref[...] += jnp.sum(dqn * qhat, axis=0, keepdims=True)
            t = dqn * gq2
            dzq = (rq * (t - qhat * _seg_mean(t * qhat, e_ref))).astype(BF16)
            for p in range(PAIRS):
                dz_ref[:, Q_OFF + p * LANES:Q_OFF + (p + 1) * LANES] = rows(dzq, p)
            tot_ref[0] = carry_ref[0] + dk2n[:BLOCK]
            tot_ref[1] = carry_ref[1] + dv2[:BLOCK]
            carry_ref[0] = dk2n[BLOCK:]
            carry_ref[1] = dv2[BLOCK:]

        @pl.when(n == nb)
        def _():
            tot_ref[...] = carry_ref[...]

        kp = zp_ref[:, :KV_WIDTH]
        rk = lax.rsqrt(_seg_mean(kp * kp, e_ref) + EPS)
        khat = kp * rk
        dkn = tot_ref[0]
        gkacc_ref[...] += jnp.sum(dkn * khat, axis=0, keepdims=True)
        t = dkn * gk2_ref[...]
        dzkv_ref[:, :KV_WIDTH] = (rk * (t - khat * _seg_mean(t * khat, e_ref))).astype(BF16)
        dzkv_ref[:, KV_WIDTH:] = tot_ref[1].astype(BF16)

        @pl.when(n == nb)
        def _():
            dsb_ref[...] = jnp.broadcast_to(jnp.sum(sbacc_ref[...], axis=1, keepdims=True), dsb_ref.shape)
            dsk_ref[...] = lax.dot_general(skacc_ref[...], fold_ref[...], NN, precision=lax.Precision.HIGHEST,
                                           preferred_element_type=F32)
            dgq_ref[...] = gqacc_ref[...] + pltpu.roll(gqacc_ref[...], HEAD_DIM, 1)
            dgk_ref[...] = gkacc_ref[...] + pltpu.roll(gkacc_ref[...], HEAD_DIM, 1)

    last = nb - 1
    full, z_spec, zp_spec, consts = _mixer_specs(nb, last)
    return _pallas(
        body,
        name=name,
        grid=(nb + 1,),
        ex=ex,
        in_specs=[z_spec, zp_spec, pl.BlockSpec((BLOCK, A_WIDTH + B_WIDTH), lambda n: (jnp.minimum(n, last), 0)),
                  full((A_HEADS, BLOCK, BLOCK)), full((A_HEADS, BLOCK, BLOCK))] + consts + [full((PAIR_ROWS, LANES))],
        out_specs=[
            pl.BlockSpec((BLOCK, K_OFF), lambda n: (jnp.minimum(n, last), 0)),
            pl.BlockSpec((BLOCK, 2 * KV_WIDTH), lambda n: (jnp.maximum(n - 1, 0), 0)),
            full((A_HEADS, BLOCK, BLOCK)), full((A_HEADS * BLOCK, A_DIM)), full((A_HEADS, A_DIM)),
            full((1, LANES)), full((1, LANES)), full((SUBLANES, LANES)),
            full((KV_HEADS, PAIR_ROWS, 2 * BAND)),
        ],
        out_shape=[
            jax.ShapeDtypeStruct((S, K_OFF), BF16),
            jax.ShapeDtypeStruct((S, 2 * KV_WIDTH), BF16),
            jax.ShapeDtypeStruct((A_HEADS, BLOCK, BLOCK), F32),
            jax.ShapeDtypeStruct((A_HEADS * BLOCK, A_DIM), F32),
            jax.ShapeDtypeStruct((A_HEADS, A_DIM), F32),
            jax.ShapeDtypeStruct((1, LANES), F32),
            jax.ShapeDtypeStruct((1, LANES), F32),
            jax.ShapeDtypeStruct((SUBLANES, LANES), F32),
            jax.ShapeDtypeStruct((KV_HEADS, PAIR_ROWS, 2 * BAND), F32),
        ],
        scratch_shapes=[
            pltpu.VMEM((2, BLOCK, KV_WIDTH), F32),
            pltpu.VMEM((2, BLOCK, KV_WIDTH), F32),
            pltpu.VMEM((A_HEADS * BLOCK, A_DIM), F32),
            pltpu.VMEM((SUBLANES, PAIR_ROWS), F32),
            pltpu.VMEM((1, LANES), F32),
            pltpu.VMEM((1, LANES), F32),
        ],
        semantics=("arbitrary",),
        args=[z, z, dmix, wm, wm_t, sbp, gvp, gq2, gk2, seg_ones, lane_ones, sinks, biasp, pair_fold],
    )


def _mixer_out(mix, w_out, x, gt, name):
    S, D = x.shape
    K = mix.shape[1]
    tm, tn = _tile(S, 1024), _tile(D, 1024)

    def body(m_ref, w_ref, x_ref, gt_ref, xo_ref, y_ref):
        y = _dot(m_ref[...], w_ref[...])
        y_ref[...] = y.astype(BF16)
        xo_ref[...] = x_ref[...] + gt_ref[...] * y

    blk = pl.BlockSpec((tm, tn), lambda j, i: (i, j))
    return pl.pallas_call(
        body,
        name=name,
        grid=(D // tn, S // tm),
        in_specs=[pl.BlockSpec((tm, K), lambda j, i: (i, 0)), pl.BlockSpec((K, tn), lambda j, i: (0, j)),
                  blk, pl.BlockSpec((1, tn), lambda j, i: (0, j))],
        out_specs=[blk, blk],
        out_shape=[jax.ShapeDtypeStruct((S, D), F32), jax.ShapeDtypeStruct((S, D), BF16)],
        compiler_params=_params(("parallel", "parallel")),
    )(mix, w_out, x, gt)


def _bucket_sum(dst, onehot, name):
    def body(d_ref, o_ref, out_ref):
        out_ref[...] = lax.dot_general(o_ref[...], d_ref[...], NT, precision=lax.Precision.HIGHEST,
                                       preferred_element_type=F32)

    return pl.pallas_call(
        body,
        name=name,
        out_shape=jax.ShapeDtypeStruct((N_BUCKETS, B_HEADS), F32),
    )(dst, onehot)


def _adamw_math(w, g, m, v):
    m = ADAM_B1 * m + (1.0 - ADAM_B1) * g
    v = ADAM_B2 * v + (1.0 - ADAM_B2) * (g * g)
    m_hat = m / (1.0 - ADAM_B1 ** ADAM_STEP)
    v_hat = v / (1.0 - ADAM_B2 ** ADAM_STEP)
    delta = -ADAM_LR * (m_hat / (jnp.sqrt(v_hat) + ADAM_EPS) + ADAM_WD * w)
    return delta, m, v


def _adamw(w, g, m, v, name, emit_grad=False, after=()):
    R, C = w.shape
    tr = _tile(R, max(SUBLANES, (1 << 19) // C), SUBLANES)

    def body(w_ref, g_ref, m_ref, v_ref, *out_refs):
        gv = g_ref[...]
        results = _adamw_math(w_ref[...], gv, m_ref[...], v_ref[...])
        for ref, val in zip(out_refs, ((gv,) if emit_grad else ()) + results):
            ref[...] = val

    blk = pl.BlockSpec((tr, C), lambda i: (i, 0))
    shape = jax.ShapeDtypeStruct((R, C), F32)
    n_out = 4 if emit_grad else 3
    return _pallas(body, name=name, grid=(R // tr,), in_specs=[blk] * 4, out_specs=[blk] * n_out,
                   out_shape=[shape] * n_out, semantics=("parallel",), args=[w, g, m, v], after=after)


def _small_update(parts, w, m, v, name):
    R = w.shape[0]

    def body(p_ref, w_ref, m_ref, v_ref, g_ref, d_ref, mo_ref, vo_ref):
        g = p_ref[0]
        for dev in range(1, N_DEV):
            g = g + p_ref[dev]
        g_ref[...] = g
        d, mn, vn = _adamw_math(w_ref[...], g, m_ref[...], v_ref[...])
        d_ref[...] = d
        mo_ref[...] = mn
        vo_ref[...] = vn

    shape = jax.ShapeDtypeStruct((R, LANES), F32)
    return pl.pallas_call(
        body,
        name=name,
        out_shape=[shape] * 4,
        compiler_params=pltpu.CompilerParams(vmem_limit_bytes=VMEM_LIMIT),
    )(parts, w, m, v)


def _place():
    x, y, c = lax.axis_index("x"), lax.axis_index("y"), lax.axis_index("c")
    chips = [(1 - x, y), (x, 1 - y), (1 - x, 1 - y)]
    return x, y, c, chips


def _remote(src, dst, send_sem, recv_sem, to):
    return pltpu.make_async_remote_copy(src_ref=src, dst_ref=dst, send_sem=send_sem, recv_sem=recv_sem,
                                        device_id=to, device_id_type=MESH)


def _allgather_small(block, name):
    m_per, n = block.shape

    def body(x_ref, out_ref, send_sems, recv_sems, local_sem):
        x, y, c, chips = _place()
        me, sibling = (x, y, c), (x, y, 1 - c)

        def rows(px, py, pc):
            return out_ref.at[pl.ds((4 * px + 2 * py + pc) * m_per, m_per), :]

        def copy(k, blk, to, src=None):
            return _remote(rows(*blk) if src is None else src, rows(*blk), send_sems.at[k], recv_sems.at[k], to)

        mine = pltpu.make_async_copy(x_ref, rows(*me), local_sem)
        mine.start()
        first = [copy(0, me, sibling, src=x_ref)]
        first += [copy(1 + j, me, (*chip, c), src=x_ref) for j, chip in enumerate(chips)]
        for cp in first:
            cp.start()
        passed = [copy(4 + j, (*chip, c), sibling) for j, chip in enumerate(chips)]
        for j, chip in enumerate(chips):
            copy(1 + j, (*chip, c), me).wait_recv()
            passed[j].start()
        copy(0, sibling, me).wait_recv()
        for j, chip in enumerate(chips):
            copy(4 + j, (*chip, 1 - c), me).wait_recv()
        for cp in first + passed:
            cp.wait_send()
        mine.wait()

    return pl.pallas_call(
        body,
        name=name,
        out_shape=jax.ShapeDtypeStruct((N_DEV * m_per, n), block.dtype),
        in_specs=[pl.BlockSpec(memory_space=pltpu.VMEM)],
        out_specs=pl.BlockSpec(memory_space=pltpu.VMEM),
        scratch_shapes=[pltpu.SemaphoreType.DMA((7,)), pltpu.SemaphoreType.DMA((7,)), pltpu.SemaphoreType.DMA],
        compiler_params=pltpu.CompilerParams(vmem_limit_bytes=VMEM_LIMIT),
    )(block)


def _half(ref, c, rows):
    start = pl.multiple_of(c * rows, BF16_ROWS)
    if len(ref.shape) == 2:
        return ref.at[pl.ds(start, rows), :]
    return ref.at[:, pl.ds(start, rows), :]


def _slot(ref, index):
    if len(ref.shape) == 3:
        return ref.at[index]
    width = ref.shape[1] // N_CHIPS
    return ref.at[:, pl.ds(pl.multiple_of(index * width, LANES), width)]


def _shard_rows(buf):
    return buf.shape[1] if len(buf.shape) == 3 else buf.shape[0]


def _same(arrays):
    return [jax.ShapeDtypeStruct(a.shape, a.dtype) for a in arrays], {t: t for t in range(len(arrays))}


def _ex_gather_ici(bufs):
    def plan(ins, outs, send_sems, recv_sems):
        x, y, c, chips = _place()
        sends, arrivals = [], []
        for t, buf in enumerate(bufs):
            rows = _shard_rows(buf) // 2
            mine = _half(_slot(outs[t], 2 * x + y), c, rows)
            for k, (px, py) in enumerate(chips):
                sems = (send_sems.at[3 * t + k], recv_sems.at[3 * t + k], (px, py, c))
                landed = _half(_slot(outs[t], 2 * px + py), c, rows)
                sends.append((mine, mine, *sems))
                arrivals.append((landed, landed, *sems))
        return sends, arrivals

    shapes, aliases = _same(bufs)
    return _Exchange(bufs, shapes, aliases, 3 * len(bufs), plan)


def _ex_gather_near(bufs):
    def plan(ins, outs, send_sems, recv_sems):
        x, y, c, chips = _place()
        sends, arrivals = [], []
        for t, buf in enumerate(bufs):
            rows = _shard_rows(buf) // 2
            mine = _half(_slot(outs[t], 2 * x + y), c, rows)
            for k, (px, py) in enumerate(chips[:2]):
                sems = (send_sems.at[2 * t + k], recv_sems.at[2 * t + k], (px, py, c))
                landed = _half(_slot(outs[t], 2 * px + py), c, rows)
                sends.append((mine, mine, *sems))
                arrivals.append((landed, landed, *sems))
        return sends, arrivals

    shapes, aliases = _same(bufs)
    return _Exchange(bufs, shapes, aliases, 2 * len(bufs), plan)


def _ex_gather_relay(bufs):
    def plan(ins, outs, send_sems, recv_sems):
        x, y, c, chips = _place()
        (xn, yn, diag) = chips
        slot = lambda chip: 2 * chip[0] + chip[1]
        sends, arrivals = [], []
        for t, buf in enumerate(bufs):
            quarter = _shard_rows(buf) // 4

            def piece(chip, q):
                start = pl.multiple_of(c * 2 * quarter + q * quarter, BF16_ROWS)
                return _slot(outs[t], slot(chip)).at[pl.ds(start, quarter), :]

            for k, (held, to) in enumerate([(xn, yn), (yn, xn)]):
                sems = (send_sems.at[2 * t + k], recv_sems.at[2 * t + k], (*to, c))
                sends.append((piece(held, k), piece(held, k), *sems))
                arrivals.append((piece(diag, k), piece(diag, k), *sems))
        onward = _ex_gather_d2d(bufs, which=(0, 1))
        more_sends, more_arrivals = onward.plan(ins, outs, _Shifted(send_sems, 2 * len(bufs)),
                                                _Shifted(recv_sems, 2 * len(bufs)))
        return sends + more_sends, arrivals + more_arrivals

    shapes, aliases = _same(bufs)
    return _Exchange(bufs, shapes, aliases, 2 * len(bufs) + 3 * len(bufs), plan)


def _ex_gather_d2d(bufs, which=(0, 1, 2)):
    def plan(ins, outs, send_sems, recv_sems):
        x, y, c, chips = _place()
        sends, arrivals = [], []
        for t, buf in enumerate(bufs):
            rows = _shard_rows(buf) // 2
            for k in which:
                px, py = chips[k]
                sems = (send_sems.at[3 * t + k], recv_sems.at[3 * t + k], (x, y, 1 - c))
                landed = _half(_slot(outs[t], 2 * px + py), c, rows)
                other = _half(_slot(outs[t], 2 * px + py), 1 - c, rows)
                sends.append((landed, landed, *sems))
                arrivals.append((other, other, *sems))
        return sends, arrivals

    shapes, aliases = _same(bufs)
    return _Exchange(bufs, shapes, aliases, 3 * len(bufs), plan)


def _ex_swap_halves(grads):
    def plan(ins, outs, send_sems, recv_sems):
        x, y, c, _ = _place()
        sends = [(_half(ins[t], 1 - c, g.shape[1] // 2), outs[t], send_sems.at[t], recv_sems.at[t], (x, y, 1 - c))
                 for t, g in enumerate(grads)]
        return sends, sends

    shapes = [jax.ShapeDtypeStruct((g.shape[0], g.shape[1] // 2, g.shape[2]), g.dtype) for g in grads]
    return _Exchange(grads, shapes, {}, len(grads), plan)


def _ex_scatter(sums):
    def plan(ins, outs, send_sems, recv_sems):
        x, y, c, chips = _place()
        sends = [(ins[t].at[2 * px + py], outs[t].at[k], send_sems.at[3 * t + k], recv_sems.at[3 * t + k], (px, py, c))
                 for t in range(len(sums)) for k, (px, py) in enumerate(chips)]
        return sends, sends

    shapes = [jax.ShapeDtypeStruct((N_CHIPS - 1,) + s.shape[1:], s.dtype) for s in sums]
    return _Exchange(sums, shapes, {}, 3 * len(sums), plan)


def _ex_join_halves(fulls):
    def plan(ins, outs, send_sems, recv_sems):
        x, y, c, _ = _place()
        sends, arrivals = [], []
        for t, full in enumerate(fulls):
            rows = full.shape[0] // 2
            sems = (send_sems.at[t], recv_sems.at[t], (x, y, 1 - c))
            mine, other = _half(outs[t], c, rows), _half(outs[t], 1 - c, rows)
            sends.append((mine, mine, *sems))
            arrivals.append((other, other, *sems))
        return sends, arrivals

    shapes, aliases = _same(fulls)
    return _Exchange(fulls, shapes, aliases, len(fulls), plan)


class _Shifted:
    def __init__(self, sems, offset):
        self.sems, self.offset = sems, offset

    @property
    def at(self):
        return self

    def __getitem__(self, k):
        return self.sems.at[self.offset + k]


def _combine(exchanges):
    operands, out_shapes, aliases, starts = [], [], {}, []
    n_sems = 0
    for e in exchanges:
        starts.append((len(operands), len(out_shapes), n_sems))
        aliases.update({len(operands) + i: len(out_shapes) + o for i, o in e.aliases.items()})
        operands += list(e.operands)
        out_shapes += list(e.out_shapes)
        n_sems += e.n_sems

    def plan(ins, outs, send_sems, recv_sems):
        sends, arrivals = [], []
        for e, (i0, o0, s0) in zip(exchanges, starts):
            s, a = e.plan(ins[i0:i0 + len(e.operands)], outs[o0:o0 + len(e.out_shapes)],
                          _Shifted(send_sems, s0), _Shifted(recv_sems, s0))
            sends += s
            arrivals += a
        return sends, arrivals

    return _Exchange(operands, out_shapes, aliases, n_sems, plan)


class _Reduction:
    def __init__(self, grad, tag, c_arr, jc_arr):
        self.grad, self.tag, self.c_arr, self.jc_arr, self.stage = grad, tag, c_arr, jc_arr, 0

    def exchange(self):
        if self.stage == 0:
            return _ex_swap_halves([self.grad])
        if self.stage == 1:
            return _ex_scatter([self.sums])
        return _ex_join_halves([self.full])

    def advance(self, landed):
        if self.stage == 0:
            self.recv = landed
            self.sums = _chip_sum(self.grad, landed, self.c_arr, f"chip_sum_{self.tag}")
        elif self.stage == 1:
            self.full = _owner_sum(self.grad, self.recv, landed, self.jc_arr, f"owner_sum_{self.tag}")
        else:
            self.result = landed
        self.stage += 1


def _ride(reductions):
    def done(carried):
        for r, landed in zip(reductions, carried):
            r.advance(landed)

    return _combine([r.exchange() for r in reductions]), done


def _exchange_alone(ex, name):
    return _pallas(None, name=name, grid=(), in_specs=[], out_specs=[], out_shape=[], args=[], ex=ex)


SEM = pl.BlockSpec(memory_space=pltpu.SEMAPHORE)
DATAFLOW = pltpu.SideEffectType.DATAFLOW_SIDE_EFFECTING


def _exchange_start(ex, name):
    e_in, e_out = len(ex.operands), len(ex.out_shapes)
    kept = [i for i in range(e_in) if i not in ex.aliases]

    def body(*refs):
        ins, refs = refs[:e_in], refs[e_in:]
        outs, refs = refs[:e_out], refs[e_out:]
        _, (send_sems, recv_sems, token) = refs[:len(kept)], refs[len(kept):]
        sends, _ = ex.plan(ins, outs, send_sems, recv_sems)
        for cp in sends:
            _remote(*cp).start()
        token[...] = jnp.zeros(token.shape, F32)

    sems = pltpu.SemaphoreType.DMA((ex.n_sems,))
    aliases = dict(ex.aliases)
    aliases.update({i: e_out + k for k, i in enumerate(kept)})
    res = pl.pallas_call(
        body,
        name=name,
        in_specs=[ANY] * e_in,
        out_specs=[ANY] * (e_out + len(kept)) + [SEM, SEM, pl.BlockSpec(memory_space=pltpu.VMEM)],
        out_shape=list(ex.out_shapes) + [jax.ShapeDtypeStruct(ex.operands[i].shape, ex.operands[i].dtype) for i in kept]
        + [sems, sems, jax.ShapeDtypeStruct((SUBLANES, LANES), F32)],
        input_output_aliases=aliases,
        compiler_params=pltpu.CompilerParams(has_side_effects=DATAFLOW),
    )(*ex.operands)
    outs, kept_thru, (send_sems, recv_sems, token) = res[:e_out], res[e_out:e_out + len(kept)], res[e_out + len(kept):]
    operands = list(ex.operands)
    for i, o in ex.aliases.items():
        operands[i] = outs[o]
    for k, i in enumerate(kept):
        operands[i] = kept_thru[k]
    return (operands, outs, send_sems, recv_sems), token


def _exchange_wait(ex, state, after, name):
    operands, outs, send_sems, recv_sems = state
    e_out = len(outs)
    kept = [i for i in range(len(operands)) if i not in ex.aliases]

    def body(*refs):
        sources, refs = refs[:len(kept)], refs[len(kept):]
        landing, refs = refs[:e_out], refs[e_out:]
        ins = [None] * len(operands)
        for k, i in enumerate(kept):
            ins[i] = sources[k]
        sends, arrivals = ex.plan(ins, landing, refs[0], refs[1])
        for cp in arrivals:
            _remote(*cp).wait_recv()
        for cp in sends:
            _remote(*cp).wait_send()

    return pl.pallas_call(
        body,
        name=name,
        in_specs=[ANY] * (len(kept) + e_out) + [SEM, SEM] + [ANY] * len(after),
        out_specs=[ANY] * e_out,
        out_shape=[jax.ShapeDtypeStruct(o.shape, o.dtype) for o in outs],
        input_output_aliases={len(kept) + o: o for o in range(e_out)},
        compiler_params=pltpu.CompilerParams(has_side_effects=DATAFLOW),
    )(*[operands[i] for i in kept], *outs, send_sems, recv_sems, *after)


def _cast_to_slot(w, chip_arr, name, after=(), natural=False):
    A, B = w.shape
    ta = _tile(A, max(BF16_ROWS, (1 << 19) // B), BF16_ROWS)

    def body(j_ref, w_ref, *rest):
        rest[-1][...] = w_ref[...].astype(BF16)

    if natural:
        assert B % LANES == 0
        out_spec = pl.BlockSpec((ta, B), lambda i, j_ref: (i, j_ref[0]))
        out_shape = jax.ShapeDtypeStruct((A, N_CHIPS * B), BF16)
    else:
        out_spec = pl.BlockSpec((None, ta, B), lambda i, j_ref: (j_ref[0], i, 0))
        out_shape = jax.ShapeDtypeStruct((N_CHIPS, A, B), BF16)
    return pl.pallas_call(
        body,
        name=name,
        grid_spec=pltpu.PrefetchScalarGridSpec(
            num_scalar_prefetch=1,
            grid=(A // ta,),
            in_specs=[pl.BlockSpec((ta, B), lambda i, j_ref: (i, 0))] + [ANY] * len(after),
            out_specs=out_spec,
        ),
        out_shape=out_shape,
        compiler_params=_params(("parallel",)),
    )(chip_arr, w, *after)


def _chip_sum(grad, recv, c_arr, name):
    _, A, B = grad.shape
    hA = A // 2
    ta = _tile(hA, max(BF16_ROWS, (1 << 19) // B), BF16_ROWS)
    nh = hA // ta

    def body(c_ref, g_ref, r_ref, o_ref):
        o_ref[...] = (g_ref[...] + r_ref[...]).astype(BF16)

    return pl.pallas_call(
        body,
        name=name,
        grid_spec=pltpu.PrefetchScalarGridSpec(
            num_scalar_prefetch=1,
            grid=(N_CHIPS, nh),
            in_specs=[pl.BlockSpec((None, ta, B), lambda s, i, c_ref: (s, c_ref[0] * nh + i, 0)),
                      pl.BlockSpec((None, ta, B), lambda s, i, c_ref: (s, i, 0))],
            out_specs=pl.BlockSpec((None, ta, B), lambda s, i, c_ref: (s, i, 0)),
        ),
        out_shape=jax.ShapeDtypeStruct((N_CHIPS, hA, B), BF16),
        compiler_params=_params(("parallel", "parallel")),
    )(c_arr, grad, recv)


def _owner_sum(grad, recv, landed, jc_arr, name):
    _, A, B = grad.shape
    hA = A // 2
    ta = _tile(hA, max(BF16_ROWS, (1 << 19) // B), BF16_ROWS)
    nh = hA // ta

    def body(jc_ref, g_ref, r_ref, l0_ref, l1_ref, l2_ref, o_ref):
        total = g_ref[...] + r_ref[...]
        for ref in (l0_ref, l1_ref, l2_ref):
            total = total + ref[...].astype(F32)
        o_ref[...] = total

    def landed_spec(k):
        return pl.BlockSpec((None, ta, B), lambda i, jc_ref: (k, i, 0))

    return pl.pallas_call(
        body,
        name=name,
        grid_spec=pltpu.PrefetchScalarGridSpec(
            num_scalar_prefetch=1,
            grid=(nh,),
            in_specs=[pl.BlockSpec((None, ta, B), lambda i, jc_ref: (jc_ref[0], jc_ref[1] * nh + i, 0)),
                      pl.BlockSpec((None, ta, B), lambda i, jc_ref: (jc_ref[0], i, 0)),
                      landed_spec(0), landed_spec(1), landed_spec(2)],
            out_specs=pl.BlockSpec((ta, B), lambda i, jc_ref: (jc_ref[1] * nh + i, 0)),
        ),
        out_shape=jax.ShapeDtypeStruct((A, B), F32),
        compiler_params=_params(("parallel",)),
    )(jc_arr, grad, recv, landed, landed, landed)


def _pack(parts):
    rows = []
    for p in parts:
        flat = jnp.reshape(p.astype(F32), (-1,))
        tile = SUBLANES * LANES
        padded = -(-flat.shape[0] // tile) * tile
        rows.append(jnp.reshape(jnp.pad(flat, (0, padded - flat.shape[0])), (-1, LANES)))
    return jnp.concatenate(rows, axis=0)


def _unpack(pack, shapes):
    out, row = [], 0
    for shape in shapes:
        size = int(np.prod(shape))
        nrows = -(-size // (SUBLANES * LANES)) * SUBLANES
        out.append(jnp.reshape(jnp.reshape(pack[row:row + nrows], (-1,))[:size], shape))
        row += nrows
    return out


def _bias_tables():
    qi = np.arange(BLOCK)[:, None]
    kj = np.arange(2 * BLOCK)[None, :]
    dist = qi + BLOCK - kj
    in_window = (dist >= 0) & (dist < BLOCK)
    n = np.clip(dist, 0, None)
    max_exact = N_BUCKETS // 2
    nf = np.maximum(n, 1).astype(np.float32)
    large = max_exact + (np.log(nf / max_exact) / math.log(MAX_DISTANCE / max_exact)
                         * (N_BUCKETS - max_exact)).astype(np.int32)
    large = np.minimum(large, N_BUCKETS - 1)
    bucket = np.where(n < max_exact, n, large)
    onehot = (bucket[None] == np.arange(N_BUCKETS)[:, None, None]) & in_window[None]
    first = in_window & (kj >= BLOCK)
    return onehot.astype(np.float32), in_window, first


def kernel(x, c, w_ada, b_ada, g_ffn1, w1_ffn1, w3_ffn1, w2_ffn1, g_mix, w_in, spatial_w, spatial_b, g_v, g_q, g_k, sinks, rel_bias, w_out, g_ffn2, w1_ffn2, w3_ffn2, w2_ffn2, loss_target, m_w_ada, m_b_ada, m_g_ffn1, m_w1_ffn1, m_w3_ffn1, m_w2_ffn1, m_g_mix, m_w_in, m_spatial_w, m_spatial_b, m_g_v, m_g_q, m_g_k, m_sinks, m_rel_bias, m_w_out, m_g_ffn2, m_w1_ffn2, m_w3_ffn2, m_w2_ffn2, v_w_ada, v_b_ada, v_g_ffn1, v_w1_ffn1, v_w3_ffn1, v_w2_ffn1, v_g_mix, v_w_in, v_spatial_w, v_spatial_b, v_g_v, v_g_q, v_g_k, v_sinks, v_rel_bias, v_w_out, v_g_ffn2, v_w1_ffn2, v_w3_ffn2, v_w2_ffn2):
    ax, ay, ac = lax.axis_index("x"), lax.axis_index("y"), lax.axis_index("c")
    chip = 2 * ax + ay
    dev = 2 * chip + ac
    xs = x[0]
    tgt = loss_target[0]
    S, D = xs.shape
    F = N_CHIPS * w1_ffn1.shape[2]
    mod_cols = w_ada.shape[2]

    chip_arr = jnp.reshape(chip, (1,)).astype(jnp.int32)
    c_arr = jnp.reshape(ac, (1,)).astype(jnp.int32)
    jc_arr = jnp.stack([chip, ac]).astype(jnp.int32)

    c_all = _allgather_small(jnp.pad(c, ((0, SUBLANES - 1), (0, 0))), "gather_c")
    c_all = jnp.pad(c_all[::SUBLANES], ((0, BF16_ROWS - N_DEV), (0, 0)))
    b_sh = lax.dynamic_slice(b_ada, (0, chip * mod_cols), (1, mod_cols))
    mod_part, c_act = _mod_partial(c_all, w_ada[0], b_sh, "mod_partial")
    mod_all = _allgather_small(mod_part[:N_DEV], "gather_mod")
    mod_all = jnp.reshape(mod_all, (N_CHIPS, 2, N_DEV, mod_cols))[:, 0]
    mod = jnp.reshape(lax.dynamic_index_in_dim(mod_all, dev, axis=1, keepdims=False), (1, N_MOD * D))
    sh1, sc1, gt1, sh2, sc2, gt2, sh3, sc3, gt3 = [mod[:, i * D:(i + 1) * D] for i in range(N_MOD)]

    def cols_to_natural(w4):
        return jnp.reshape(jnp.transpose(w4, (1, 0, 2)), (w4.shape[1], -1))

    def cast(w, nm, after):
        return _cast_to_slot(w[0], chip_arr, f"cast_{nm}", after=after, natural=nm.startswith(("w1_", "w3_")))

    ffn1_bufs = [cast(w1_ffn1, "w1_ffn1", [mod]), cast(w3_ffn1, "w3_ffn1", [mod]), cast(w2_ffn1, "w2_ffn1", [mod])]
    near = _ex_gather_near(ffn1_bufs)
    state, started = _exchange_start(near, "gather_ffn1_near_start")
    mixer_bufs = [cast(w_in, "w_in", [started]), cast(w_out, "w_out", [started])]
    ffn2_bufs = [cast(w1_ffn2, "w1_ffn2", [started]), cast(w3_ffn2, "w3_ffn2", [started]),
                 cast(w2_ffn2, "w2_ffn2", [started])]
    ffn1_bufs = _exchange_wait(near, state, mixer_bufs + ffn2_bufs, "gather_ffn1_near_wait")
    relay = _ex_gather_relay(ffn1_bufs)
    state, started = _exchange_start(relay, "gather_ffn1_relay_start")
    h1 = _norm_mod(xs, g_ffn1, sh1, sc1, "ffn1_norm", after=[started])
    ffn1_bufs = _exchange_wait(relay, state, [h1], "gather_ffn1_relay_wait")
    ffn1_bufs = _exchange_alone(_ex_gather_d2d(ffn1_bufs, which=(2,)), "gather_ffn1_d2d")
    w1a, w3a, w2a = ffn1_bufs[0], ffn1_bufs[1], jnp.reshape(ffn1_bufs[2], (F, D))

    onehot_np, in_window_np, first_np = _bias_tables()
    onehot = jnp.asarray(onehot_np)
    bias = jnp.einsum("bij,bh->hij", onehot, rel_bias, precision=lax.Precision.HIGHEST)
    biasm = jnp.stack([jnp.where(jnp.asarray(first_np)[None], bias, NEG),
                       jnp.where(jnp.asarray(in_window_np)[None], bias, NEG)])
    causal = jnp.asarray(np.tril(np.ones((BLOCK, BLOCK), dtype=bool)))
    wm = jnp.where(causal[None], spatial_w[0], 0.0).astype(BF16)
    wm_t = jnp.transpose(wm, (0, 2, 1))
    sink_vec = sinks[0]
    per_group = PAIRS // KV_HEADS
    sbp = jnp.broadcast_to(jnp.reshape(spatial_b[0], (A_HEADS * BLOCK, 1)), (A_HEADS * BLOCK, A_DIM))
    gvp = jnp.repeat(g_v[0], BLOCK, axis=0)
    gq2, gk2 = jnp.concatenate([g_q, g_q], axis=1), jnp.concatenate([g_k, g_k], axis=1)
    seg_ones = jnp.asarray(np.kron(np.eye(2, dtype=np.float32), np.ones((HEAD_DIM, HEAD_DIM), np.float32)), BF16)
    lane_ones = jnp.full((LANES, LANES), 1.0 / LANES, BF16)
    pair_fold = jnp.asarray(np.kron(np.eye(per_group, LANES, dtype=np.float32), np.ones((BLOCK, 1), np.float32)))
    biasp = jnp.reshape(jnp.transpose(jnp.reshape(biasm, (2, KV_HEADS, per_group, 2, BLOCK, BAND)), (0, 1, 3, 5, 2, 4)),
                        (2, KV_HEADS, 2 * BAND, PAIR_ROWS))

    res = _ffn_fwd(xs, g_ffn1, sh1, sc1, gt1, w1a, w3a, w2a, None, "ffn1_fwd", ex=_ex_gather_ici(mixer_bufs + ffn2_bufs),
                   h=h1)
    (x1, a1, b1, f1), mixer_bufs, ffn2_bufs = res[:4], res[4:6], res[6:]
    h2, *mixer_bufs = _norm_mod(x1, g_mix, sh2, sc2, "mixer_norm", ex=_ex_gather_d2d(mixer_bufs))
    win, wout = cols_to_natural(mixer_bufs[0]), jnp.reshape(mixer_bufs[1], (-1, D))
    z, *ffn2_bufs = _matmul(h2, win, "nn", F32, 1024, _tile(IN_COLS, 1664), D, "mixer_in", ex=_ex_gather_d2d(ffn2_bufs))
    w1b, w3b, w2b = ffn2_bufs[0], ffn2_bufs[1], jnp.reshape(ffn2_bufs[2], (F, D))
    mix = _mixer_fwd(z, wm, sbp, gvp, gq2, gk2, seg_ones, lane_ones, sink_vec, biasp, "mixer_fwd")
    x2, ymix = _mixer_out(mix, wout, x1, gt2, "mixer_out")
    g3, df3, h3, a3, b3, dgt3, loss_sum = _ffn_fwd(x2, g_ffn2, sh3, sc3, gt3, w1b, w3b, w2b, tgt, "ffn2_fwd_loss")
    loss = lax.psum(loss_sum[0, 0] * (0.5 / D), ("x", "y", "c"))

    tk = _tile(S, 2048)

    def ffn_weight_grads(h, da, db, s, df, tag, riding):
        ex, done = _ride(riding) if riding else (None, None)
        gw1 = _matmul(h, da, "tn", F32, 1024, F // N_CHIPS, tk, f"grad_w1_{tag}", shard_major=True, ex=ex)
        if riding:
            done(gw1[1:])
            gw1 = gw1[0]
        r1 = _Reduction(gw1, f"w1_{tag}", c_arr, jc_arr)
        ex, done = _ride([r1])
        gw3, *carried = _matmul(h, db, "tn", F32, 1024, F // N_CHIPS, tk, f"grad_w3_{tag}", shard_major=True, ex=ex)
        done(carried)
        r3 = _Reduction(gw3, f"w3_{tag}", c_arr, jc_arr)
        ex, done = _ride([r1, r3])
        gw2, *carried = _matmul(s, df, "tn", F32, _tile(F, 1408), 1024, tk, f"grad_w2_{tag}", ex=ex)
        done(carried)
        r2 = _Reduction(jnp.reshape(gw2, (N_CHIPS, F // N_CHIPS, D)), f"w2_{tag}", c_arr, jc_arr)
        return r1, r3, r2

    da3, db3, s3, dh3 = _ffn_bwd(df3, a3, b3, w1b, w3b, w2b, "ffn2_bwd")
    r21, r23, r22 = ffn_weight_grads(h3, da3, db3, s3, df3, "ffn2", [])
    ex, done = _ride([r21, r22])
    state, started = _exchange_start(ex, "reduce_ffn2_start")
    g2, dsh3, dsc3, dgn3, dy, dgt2 = _norm_bwd(dh3, x2, g3, g_ffn2, sc3, (ymix, gt2, 1.0), "ffn2_norm_bwd",
                                               after=[started])
    dmix = _matmul(dy, wout, "nt", BF16, 1024, 2048, D, "mixer_out_bwd")
    done(_exchange_wait(ex, state, [dmix], "reduce_ffn2_wait"))
    ex, done = _ride([r23, r22])
    res = _mixer_bwd(z, dmix, wm, wm_t, sbp, gvp, gq2, gk2, seg_ones, lane_ones, sink_vec, biasp, pair_fold,
                     "mixer_bwd", ex=ex)
    dz_main, dz_kv, dwm, dsb, dgv, dgq, dgk, dsk, dst = res[:9]
    dsb = jnp.reshape(dsb[:, 0], (A_HEADS, BLOCK))
    dgq, dgk = dgq[:, :HEAD_DIM], dgk[:, :HEAD_DIM]
    dsk = jnp.reshape(jnp.transpose(jnp.reshape(dsk[:2 * KV_HEADS, :per_group], (KV_HEADS, 2, per_group)), (0, 2, 1)),
                      (1, B_HEADS))
    dst = jnp.reshape(jnp.transpose(jnp.reshape(dst, (KV_HEADS, 2, BAND, per_group, BLOCK)), (0, 3, 1, 4, 2)),
                      (B_HEADS, BLOCK * BAND))
    done(res[9:])
    dz = jnp.concatenate([dz_main, dz_kv], axis=1)
    ex, done = _ride([r23, r22])
    dh2, *carried = _matmul(dz, win, "nt", BF16, 1024, 2048, _tile(IN_COLS, 1664), "mixer_in_bwd", ex=ex)
    done(carried)
    drel = _bucket_sum(dst, jnp.reshape(onehot, (N_BUCKETS, -1)), "bucket_sum")
    g1, dsh2, dsc2, dgn2, df1, dgt1 = _norm_bwd(dh2, x1, g2, g_mix, sc2, (f1, gt1, 0.5), "mixer_norm_bwd")

    da1, db1, s1, dh1 = _ffn_bwd(df1, a1, b1, w1a, w3a, w2a, "ffn1_bwd")
    r11, r13, r12 = ffn_weight_grads(h1, da1, db1, s1, df1, "ffn1", [])
    ex, done = _ride([r11, r13, r12])
    gwin_full, *carried = _matmul(h2, dz, "tn", F32, 1024, _tile(IN_COLS, 1664), tk, "grad_w_in", ex=ex)
    done(carried)
    rm_in = _Reduction(jnp.transpose(jnp.reshape(gwin_full, (D, N_CHIPS, -1)), (1, 0, 2)), "w_in", c_arr, jc_arr)
    ex, done = _ride([r13, r12, rm_in])
    state, started = _exchange_start(ex, "reduce_late_start")
    gwout_full = _matmul(mix, dy, "tn", F32, 1024, 1024, tk, "grad_w_out", after=[started])
    grad_x, dsh1, dsc1, dgn1 = _norm_bwd(dh1, xs, g1, g_ffn1, sc1, None, "ffn1_norm_bwd", after=[started])
    done(_exchange_wait(ex, state, [gwout_full, grad_x], "reduce_late_wait"))
    rm_out = _Reduction(jnp.reshape(gwout_full, (N_CHIPS, -1, D)), "w_out", c_arr, jc_arr)

    dmod = jnp.concatenate([dsh1, dsc1, dgt1, dsh2, dsc2, dgt2, dsh3, dsc3, dgt3], axis=1)
    small_w = [b_ada, g_ffn1, g_mix, g_ffn2, spatial_w, spatial_b, g_v, g_q, g_k, sinks, rel_bias]
    small_m = [m_b_ada, m_g_ffn1, m_g_mix, m_g_ffn2, m_spatial_w, m_spatial_b, m_g_v, m_g_q, m_g_k, m_sinks, m_rel_bias]
    small_v = [v_b_ada, v_g_ffn1, v_g_mix, v_g_ffn2, v_spatial_w, v_spatial_b, v_g_v, v_g_q, v_g_k, v_sinks, v_rel_bias]
    small_g = [dmod, dgn1, dgn2, dgn3, jnp.where(causal[None], dwm, 0.0), dsb, dgv, dgq, dgk, dsk, drel]
    shapes = [w.shape for w in small_w]
    gpack = _pack(small_g)
    rows = gpack.shape[0]
    gall = jnp.reshape(_allgather_small(gpack, "gather_small"), (N_DEV, rows, LANES))
    sg, sd, sm, sv = _small_update(gall, _pack(small_w), _pack(small_m), _pack(small_v), "small_update")
    sg, sd, sm, sv = [_unpack(p, shapes) for p in (sg, sd, sm, sv)]

    mod_rows = -(-N_MOD * D // (SUBLANES * LANES)) * SUBLANES
    dmod_all = jnp.reshape(gall[:, :mod_rows], (N_DEV, -1))[:, :N_MOD * D]
    dmod_sh = lax.dynamic_slice(dmod_all, (0, chip * mod_cols), (N_DEV, mod_cols))
    dmod_sh = jnp.pad(dmod_sh, ((0, BF16_ROWS - N_DEV), (0, 0))).astype(BF16)
    g_wada = _matmul(c_act, dmod_sh, "tn", F32, 1024, _tile(mod_cols, 512), BF16_ROWS, "grad_w_ada")

    big = {}

    def update(nm, w, g, m, v, token):
        g_out, d, nm_, nv_ = _adamw(w[0], g, m[0], v[0], f"adamw_{nm}", emit_grad=True, after=[token])
        big[nm] = (g_out[None], d[None], nm_[None], nv_[None])
        return d

    ex, done = _ride([r12, rm_in, rm_out])
    state, token = _exchange_start(ex, "reduce_tail_0_start")
    behind = [update("w1_ffn2", w1_ffn2, r21.result, m_w1_ffn2, v_w1_ffn2, token),
              update("w3_ffn2", w3_ffn2, r23.result, m_w3_ffn2, v_w3_ffn2, token),
              update("w2_ffn2", w2_ffn2, r22.result, m_w2_ffn2, v_w2_ffn2, token)]
    done(_exchange_wait(ex, state, behind, "reduce_tail_0_wait"))
    ex, done = _ride([rm_in, rm_out])
    state, token = _exchange_start(ex, "reduce_tail_1_start")
    behind = [update("w1_ffn1", w1_ffn1, r11.result, m_w1_ffn1, v_w1_ffn1, token),
              update("w3_ffn1", w3_ffn1, r13.result, m_w3_ffn1, v_w3_ffn1, token),
              update("w2_ffn1", w2_ffn1, r12.result, m_w2_ffn1, v_w2_ffn1, token)]
    done(_exchange_wait(ex, state, behind, "reduce_tail_1_wait"))
    ex, done = _ride([rm_out])
    state, token = _exchange_start(ex, "reduce_tail_2_start")
    d_wada, nm_wada, nv_wada = _adamw(w_ada[0], g_wada, m_w_ada[0], v_w_ada[0], "adamw_w_ada", after=[token])
    behind = [d_wada, update("w_in", w_in, rm_in.result, m_w_in, v_w_in, token)]
    done(_exchange_wait(ex, state, behind, "reduce_tail_2_wait"))
    update("w_out", w_out, rm_out.result, m_w_out, v_w_out, token)
    big["w_ada"] = (g_wada[None], d_wada[None], nm_wada[None], nv_wada[None])

    order = ["w_ada", "b_ada", "g_ffn1", "w1_ffn1", "w3_ffn1", "w2_ffn1", "g_mix", "w_in", "spatial_w", "spatial_b",
             "g_v", "g_q", "g_k", "sinks", "rel_bias", "w_out", "g_ffn2", "w1_ffn2", "w3_ffn2", "w2_ffn2"]
    small_names = ["b_ada", "g_ffn1", "g_mix", "g_ffn2", "spatial_w", "spatial_b", "g_v", "g_q", "g_k", "sinks", "rel_bias"]
    for i, nm in enumerate(small_names):
        big[nm] = (sg[i], sd[i], sm[i], sv[i])
    outs = [loss, grad_x[None]]
    for kind in range(4):
        outs += [big[nm][kind] for nm in order]
    return tuple(outs)
```
